```python
import jax, jax.numpy as jnp
from jax import lax
import numpy as np


D_MODEL = 1024
BATCH = 8
SEQ = 4096
DEPTH = 2

CHUNK = 64
N_BRANCH = 3
BRANCH_WIDTH = D_MODEL // 2
POOL_WINDOWS = (2, 4, 8, 16)
N_POOL_GROUPS = len(POOL_WINDOWS)
POOL_GROUP = BRANCH_WIDTH // N_POOL_GROUPS
CONV_K = 3
SB_HEAD_DIM = 64
SB_HEADS = BRANCH_WIDTH // SB_HEAD_DIM
Q_BLOCK = 128
RMS_EPS = 1e-6
IN_SIZES = (BRANCH_WIDTH,) * 10 + (N_BRANCH * D_MODEL,)
N_IN = sum(IN_SIZES)

kernel_name = "hybrid_pool_conv_stickbreak_block"


def _split_points():
    return [int(p) for p in np.cumsum(IN_SIZES)[:-1]]


def rms_norm(x, g):
    xf = x.astype(jnp.float32)
    y = xf * lax.rsqrt(jnp.mean(xf * xf, axis=-1, keepdims=True) + RMS_EPS)
    return (y * g.astype(jnp.float32)).astype(x.dtype)


def pool_mixer(v, w_group, scale):
    b, s, _ = v.shape
    vg = v.astype(jnp.float32).reshape(b, s, N_POOL_GROUPS, POOL_GROUP)
    csum = jnp.cumsum(vg, axis=1)
    pos = jnp.arange(s)
    outs = []
    for gi, w in enumerate(POOL_WINDOWS):
        c = csum[:, :, gi]
        lag = jnp.pad(c[:, :s - w], ((0, 0), (w, 0), (0, 0)))
        cnt = jnp.minimum(pos + 1, w).astype(jnp.float32)[None, :, None]
        outs.append((c - lag) / cnt - vg[:, :, gi])
    pooled = jnp.stack(outs, axis=2).astype(v.dtype)
    mixed = jnp.einsum('bsgc,gcd->bsgd', pooled, w_group)
    return mixed.reshape(b, s, BRANCH_WIDTH) * scale


def conv_mixer(xc, gate_b, gate_c, w, bias):
    z = gate_c * xc
    y = lax.conv_general_dilated(
        z, w[:, None, :].astype(z.dtype), window_strides=(1,), padding=[(CONV_K - 1, 0)],
        dimension_numbers=('NWC', 'WIO', 'NWC'), feature_group_count=BRANCH_WIDTH)
    return gate_b * (y + bias)


def stick_breaking_attention(q, k, v):
    b, s, _ = q.shape
    nblk = s // Q_BLOCK
    qb = q.reshape(b, nblk, Q_BLOCK, SB_HEADS, SB_HEAD_DIM).transpose(1, 0, 2, 3, 4)
    kf = k.reshape(b, s, SB_HEADS, SB_HEAD_DIM).astype(jnp.float32)
    vf = v.reshape(b, s, SB_HEADS, SB_HEAD_DIM).astype(jnp.float32)
    key_pos = jnp.arange(s)
    scale = SB_HEAD_DIM ** -0.5

    def block(args):
        qi, i = args
        logits = jnp.einsum('bqhd,bkhd->bhqk', qi.astype(jnp.float32), kf) * scale
        q_pos = i * Q_BLOCK + jnp.arange(Q_BLOCK)
        mask = key_pos[None, :] < q_pos[:, None]
        log_keep = jnp.where(mask, jax.nn.log_sigmoid(-logits), 0.0)
        later = lax.cumsum(log_keep, axis=3, reverse=True) - log_keep
        weights = jnp.where(mask, jnp.exp(jax.nn.log_sigmoid(logits) + later), 0.0)
        return jnp.einsum('bhqk,bkhd->bqhd', weights, vf)

    out = lax.map(block, (qb, jnp.arange(nblk)))
    return out.transpose(1, 0, 2, 3, 4).reshape(b, s, BRANCH_WIDTH).astype(q.dtype)


def hybrid_layer(x, g_pre, w_in, pool_w, pool_scale, conv_w, conv_b, w_branch, w_out, g_post):
    b, s, _ = x.shape
    h = rms_norm(x, g_pre)
    u = jnp.einsum('bsd,dn->bsn', h, w_in)
    (pool_v, pool_g, conv_x, conv_gb, conv_gc, conv_g,
     sb_q, sb_k, sb_v, sb_g, merge) = jnp.split(u, _split_points(), axis=-1)
    y_pool = pool_mixer(pool_v, pool_w, pool_scale) * jax.nn.silu(pool_g)
    y_conv = conv_mixer(conv_x, conv_gb, conv_gc, conv_w, conv_b) * jax.nn.silu(conv_g)
    y_sb = stick_breaking_attention(sb_q, sb_k, sb_v) * jax.nn.silu(sb_g)
    branches = jnp.stack([y_pool, y_conv, y_sb], axis=2)
    proj = jnp.einsum('bsnw,nwd->bsnd', branches, w_branch)
    gates = jax.nn.sigmoid(merge.reshape(b, s, N_BRANCH, D_MODEL))
    merged = jnp.sum(gates * proj, axis=2)
    out = jnp.einsum('bsd,de->bse', merged, w_out)
    return x + rms_norm(out, g_post)


def _fwd_setup_inputs(seed: int = 0) -> dict:
    key = jax.random.key(seed)
    ks = jax.random.split(key, 10)
    f32 = jnp.float32
    x = jax.random.normal(ks[0], (BATCH, SEQ, D_MODEL), f32)
    pre_norm_g = 1.0 + 0.05 * jax.random.normal(ks[1], (DEPTH, D_MODEL), f32)
    w_in = jax.random.normal(ks[2], (DEPTH, D_MODEL, N_IN), f32) * D_MODEL ** -0.5
    pool_w = jax.random.normal(ks[3], (DEPTH, N_POOL_GROUPS, POOL_GROUP, POOL_GROUP), f32) * POOL_GROUP ** -0.5
    pool_scale = 1.0 + 0.1 * jax.random.normal(ks[4], (DEPTH, BRANCH_WIDTH), f32)
    conv_w = jax.random.normal(ks[5], (DEPTH, CONV_K, BRANCH_WIDTH), f32) * CONV_K ** -0.5
    conv_b = 0.01 * jax.random.normal(ks[6], (DEPTH, BRANCH_WIDTH), f32)
    w_branch = jax.random.normal(ks[7], (DEPTH, N_BRANCH, BRANCH_WIDTH, D_MODEL), f32) * BRANCH_WIDTH ** -0.5
    w_out = jax.random.normal(ks[8], (DEPTH, D_MODEL, D_MODEL), f32) * D_MODEL ** -0.5
    post_norm_g = 1.0 + 0.05 * jax.random.normal(ks[9], (DEPTH, D_MODEL), f32)
    return {"x": x, "pre_norm_g": pre_norm_g, "w_in": w_in, "pool_w": pool_w,
            "pool_scale": pool_scale, "conv_w": conv_w, "conv_b": conv_b,
            "w_branch": w_branch, "w_out": w_out, "post_norm_g": post_norm_g}


def _fwd_reference(x, pre_norm_g, w_in, pool_w, pool_scale, conv_w, conv_b, w_branch, w_out, post_norm_g):
    for l in range(DEPTH):
        x = hybrid_layer(x, pre_norm_g[l], w_in[l], pool_w[l], pool_scale[l], conv_w[l],
                         conv_b[l], w_branch[l], w_out[l], post_norm_g[l])
    return x


import jax as _jax
import jax.numpy as _jnp

TWIN_FORMAT = 'train_step'
FWD_PARAMS = ['x', 'pre_norm_g', 'w_in', 'pool_w', 'pool_scale', 'conv_w', 'conv_b', 'w_branch', 'w_out', 'post_norm_g']
TWIN_WEIGHTS = ['pre_norm_g', 'w_in', 'pool_w', 'pool_scale', 'conv_w', 'conv_b', 'w_branch', 'w_out', 'post_norm_g']
TWIN_DIFF_INPUT = 'x'
TWIN_INPUTS = ['x', 'pre_norm_g', 'w_in', 'pool_w', 'pool_scale', 'conv_w', 'conv_b', 'w_branch', 'w_out', 'post_norm_g', 'loss_target', 'm_pre_norm_g', 'm_w_in', 'm_pool_w', 'm_pool_scale', 'm_conv_w', 'm_conv_b', 'm_w_branch', 'm_w_out', 'm_post_norm_g', 'v_pre_norm_g', 'v_w_in', 'v_pool_w', 'v_pool_scale', 'v_conv_w', 'v_conv_b', 'v_w_branch', 'v_w_out', 'v_post_norm_g']
TWIN_OUTPUTS = ['loss', 'grad_x', 'grad_pre_norm_g', 'grad_w_in', 'grad_pool_w', 'grad_pool_scale', 'grad_conv_w', 'grad_conv_b', 'grad_w_branch', 'grad_w_out', 'grad_post_norm_g', 'delta_pre_norm_g', 'delta_w_in', 'delta_pool_w', 'delta_pool_scale', 'delta_conv_w', 'delta_conv_b', 'delta_w_branch', 'delta_w_out', 'delta_post_norm_g', 'new_m_pre_norm_g', 'new_m_w_in', 'new_m_pool_w', 'new_m_pool_scale', 'new_m_conv_w', 'new_m_conv_b', 'new_m_w_branch', 'new_m_w_out', 'new_m_post_norm_g', 'new_v_pre_norm_g', 'new_v_w_in', 'new_v_pool_w', 'new_v_pool_scale', 'new_v_conv_w', 'new_v_conv_b', 'new_v_w_branch', 'new_v_w_out', 'new_v_post_norm_g']
TWIN_LEAF_KINDS = {'loss': 'loss', 'grad_x': 'grad_x', 'grad_pre_norm_g': 'grad_w', 'grad_w_in': 'grad_w', 'grad_pool_w': 'grad_w', 'grad_pool_scale': 'grad_w', 'grad_conv_w': 'grad_w', 'grad_conv_b': 'grad_w', 'grad_w_branch': 'grad_w', 'grad_w_out': 'grad_w', 'grad_post_norm_g': 'grad_w', 'delta_pre_norm_g': 'delta_w', 'delta_w_in': 'delta_w', 'delta_pool_w': 'delta_w', 'delta_pool_scale': 'delta_w', 'delta_conv_w': 'delta_w', 'delta_conv_b': 'delta_w', 'delta_w_branch': 'delta_w', 'delta_w_out': 'delta_w', 'delta_post_norm_g': 'delta_w', 'new_m_pre_norm_g': 'new_m', 'new_m_w_in': 'new_m', 'new_m_pool_w': 'new_m', 'new_m_pool_scale': 'new_m', 'new_m_conv_w': 'new_m', 'new_m_conv_b': 'new_m', 'new_m_w_branch': 'new_m', 'new_m_w_out': 'new_m', 'new_m_post_norm_g': 'new_m', 'new_v_pre_norm_g': 'new_v', 'new_v_w_in': 'new_v', 'new_v_pool_w': 'new_v', 'new_v_pool_scale': 'new_v', 'new_v_conv_w': 'new_v', 'new_v_conv_b': 'new_v', 'new_v_w_branch': 'new_v', 'new_v_w_out': 'new_v', 'new_v_post_norm_g': 'new_v'}


def _forward(args):
    return _fwd_reference(*[args[k] for k in FWD_PARAMS])


def _output_shape():
    def fwd():
        inp = _fwd_setup_inputs(0)
        return _fwd_reference(*[inp[k] for k in FWD_PARAMS])
    out = _jax.eval_shape(fwd)
    return out.shape, out.dtype

N_MICROBATCH = 1
ADAM_LR = 0.001
ADAM_B1 = 0.9
ADAM_B2 = 0.999
ADAM_EPS = 1e-08
ADAM_WD = 0.01
ADAM_STEP = 10
PER_EXAMPLE_BATCH_AXIS = {'x': 0, 'loss_target': 0}
SHARED_INPUTS = []
_WEIGHT_DTYPES = {'pre_norm_g': _jnp.float32, 'w_in': _jnp.float32, 'pool_w': _jnp.float32, 'pool_scale': _jnp.float32, 'conv_w': _jnp.float32, 'conv_b': _jnp.float32, 'w_branch': _jnp.float32, 'w_out': _jnp.float32, 'post_norm_g': _jnp.float32}
MOMENT_SCALE = {'pre_norm_g': 7.589086e-01, 'w_in': 2.703112e-01, 'pool_w': 4.308479e-01, 'pool_scale': 4.097419e-01, 'conv_w': 4.061429e-01, 'conv_b': 4.957912e-01, 'w_branch': 2.627232e-01, 'w_out': 4.682758e-01, 'post_norm_g': 3.202976e+01}


def _to_microbatches(a, axis):
    t = _jnp.moveaxis(a, axis, 0)
    t = t.reshape((N_MICROBATCH, t.shape[0] // N_MICROBATCH) + t.shape[1:])
    return _jnp.moveaxis(t, 1, axis + 1)


def setup_inputs(seed: int = 0) -> dict:
    inp = _fwd_setup_inputs(seed)
    key = _jax.random.fold_in(_jax.random.key(seed), 7919)
    shape, _ = _output_shape()
    out = dict(inp)
    out["loss_target"] = _jax.random.normal(_jax.random.fold_in(key, 0), shape, _jnp.float32)
    for i, name in enumerate(TWIN_WEIGHTS):
        w = inp[name].astype(_jnp.float32)
        if MOMENT_SCALE is None:
            s = _jnp.sqrt(_jnp.mean(_jnp.square(w)) + 1e-30)
        else:
            s = MOMENT_SCALE[name]
        km, kv = _jax.random.split(_jax.random.fold_in(key, i + 1))
        out[name] = w
        out["m_" + name] = s * _jax.random.normal(km, w.shape, _jnp.float32)
        out["v_" + name] = (s * s) * _jax.random.uniform(kv, w.shape, _jnp.float32, 0.5, 1.5)
    if N_MICROBATCH > 1:
        for name, axis in PER_EXAMPLE_BATCH_AXIS.items():
            out[name] = _to_microbatches(out[name], axis)
    return {'x': out['x'], 'pre_norm_g': out['pre_norm_g'], 'w_in': out['w_in'], 'pool_w': out['pool_w'], 'pool_scale': out['pool_scale'], 'conv_w': out['conv_w'], 'conv_b': out['conv_b'], 'w_branch': out['w_branch'], 'w_out': out['w_out'], 'post_norm_g': out['post_norm_g'], 'loss_target': out['loss_target'], 'm_pre_norm_g': out['m_pre_norm_g'], 'm_w_in': out['m_w_in'], 'm_pool_w': out['m_pool_w'], 'm_pool_scale': out['m_pool_scale'], 'm_conv_w': out['m_conv_w'], 'm_conv_b': out['m_conv_b'], 'm_w_branch': out['m_w_branch'], 'm_w_out': out['m_w_out'], 'm_post_norm_g': out['m_post_norm_g'], 'v_pre_norm_g': out['v_pre_norm_g'], 'v_w_in': out['v_w_in'], 'v_pool_w': out['v_pool_w'], 'v_pool_scale': out['v_pool_scale'], 'v_conv_w': out['v_conv_w'], 'v_conv_b': out['v_conv_b'], 'v_w_branch': out['v_w_branch'], 'v_w_out': out['v_w_out'], 'v_post_norm_g': out['v_post_norm_g']}


def _loss(weights, diff, rest, loss_target):
    with _jax.named_scope("forward"):
        args = {**rest, TWIN_DIFF_INPUT: diff, **{k: w.astype(_WEIGHT_DTYPES[k]) for k, w in weights.items()}}
        y = _forward(args)
    with _jax.named_scope("loss_head"):
        err = _jnp.square(y.astype(_jnp.float32) - loss_target)
        return 0.5 * _jnp.sum(_jnp.mean(err, axis=-1)) if err.ndim else 0.5 * err


def _adamw(w, g, m, v):
    m = ADAM_B1 * m + (1.0 - ADAM_B1) * g
    v = ADAM_B2 * v + (1.0 - ADAM_B2) * _jnp.square(g)
    m_hat = m / (1.0 - ADAM_B1 ** ADAM_STEP)
    v_hat = v / (1.0 - ADAM_B2 ** ADAM_STEP)
    delta = -ADAM_LR * (m_hat / (_jnp.sqrt(v_hat) + ADAM_EPS) + ADAM_WD * w)
    return delta, m, v


def reference(x, pre_norm_g, w_in, pool_w, pool_scale, conv_w, conv_b, w_branch, w_out, post_norm_g, loss_target, m_pre_norm_g, m_w_in, m_pool_w, m_pool_scale, m_conv_w, m_conv_b, m_w_branch, m_w_out, m_post_norm_g, v_pre_norm_g, v_w_in, v_pool_w, v_pool_scale, v_conv_w, v_conv_b, v_w_branch, v_w_out, v_post_norm_g):
    given = dict(x=x, pre_norm_g=pre_norm_g, w_in=w_in, pool_w=pool_w, pool_scale=pool_scale, conv_w=conv_w, conv_b=conv_b, w_branch=w_branch, w_out=w_out, post_norm_g=post_norm_g, loss_target=loss_target, m_pre_norm_g=m_pre_norm_g, m_w_in=m_w_in, m_pool_w=m_pool_w, m_pool_scale=m_pool_scale, m_conv_w=m_conv_w, m_conv_b=m_conv_b, m_w_branch=m_w_branch, m_w_out=m_w_out, m_post_norm_g=m_post_norm_g, v_pre_norm_g=v_pre_norm_g, v_w_in=v_w_in, v_pool_w=v_pool_w, v_pool_scale=v_pool_scale, v_conv_w=v_conv_w, v_conv_b=v_conv_b, v_w_branch=v_w_branch, v_w_out=v_w_out, v_post_norm_g=v_post_norm_g)
    weights = {n: given[n] for n in TWIN_WEIGHTS}
    shared = {n: given[n] for n in SHARED_INPUTS}
    per_example = {n: given[n] for n in ['x']}
    grad_fn = _jax.value_and_grad(_loss, argnums=(0, 1))

    def one_microbatch(ex, loss_target):
        ex = dict(ex)
        diff = ex.pop(TWIN_DIFF_INPUT)
        return grad_fn(weights, diff, {**shared, **ex}, loss_target)

    if N_MICROBATCH == 1:
        loss, (grad_w, grad_x) = one_microbatch(per_example, given["loss_target"])
    else:
        def body(carry, xs):
            loss_sum, grad_sum = carry
            l_k, (gw_k, gx_k) = one_microbatch(xs[0], xs[1])
            with _jax.named_scope("update"):
                return (loss_sum + l_k, _jax.tree.map(_jnp.add, grad_sum, gw_k)), gx_k

        init = (_jnp.zeros((), _jnp.float32), _jax.tree.map(_jnp.zeros_like, weights))
        (loss, grad_w), grad_x = _jax.lax.scan(body, init, (per_example, given["loss_target"]))
    with _jax.named_scope("update"):
        delta_w, new_m, new_v = {}, {}, {}
        for n in TWIN_WEIGHTS:
            delta_w[n], new_m[n], new_v[n] = _adamw(weights[n], grad_w[n], given["m_" + n], given["v_" + n])
    return (loss, grad_x, *[grad_w[n] for n in TWIN_WEIGHTS], *[delta_w[n] for n in TWIN_WEIGHTS],
            *[new_m[n] for n in TWIN_WEIGHTS], *[new_v[n] for n in TWIN_WEIGHTS])
```

```python
import functools

import jax
import jax.numpy as jnp
from jax import lax
from jax.experimental import pallas as pl
from jax.experimental.pallas import tpu as pltpu

F32 = jnp.float32
BF16 = jnp.bfloat16

N_DEV = 8
D_MODEL = 1024
WIDTH = 512
N_IN = 8192
COLS_PER_DEV = N_IN // N_DEV
HEAD_DIM = 64
LANES = 128
SB_SCALE = HEAD_DIM ** -0.5
RMS_EPS = 1e-6
POOL_HALO = 16
CONV_HALO = 8
ADAM_LR, ADAM_B1, ADAM_B2, ADAM_EPS, ADAM_WD, ADAM_STEP = 0.001, 0.9, 0.999, 1e-08, 0.01, 10
VMEM_LIMIT = 56 * 1024 * 1024

CB_POOL_V, CB_POOL_G = 0, 4
CB_CONV_X, CB_CONV_GB, CB_CONV_GC, CB_CONV_G = 8, 12, 16, 20
CB_SB_Q, CB_SB_K, CB_SB_V, CB_SB_G = 24, 28, 32, 36
MERGE_BLOCK_1024 = 5


def _pcall(body, **kw):
    return pl.pallas_call(body, **kw)


def _params(sem=None):
    if sem is None:
        return pltpu.CompilerParams(vmem_limit_bytes=VMEM_LIMIT)
    return pltpu.CompilerParams(dimension_semantics=sem, vmem_limit_bytes=VMEM_LIMIT)


def _sigmoid(x):
    return 1.0 / (1.0 + jnp.exp(-x))


def _dot(a, b):
    return jnp.dot(a, b, preferred_element_type=F32)


def _dot_nt(a, b):
    return lax.dot_general(a, b, (((1,), (1,)), ((), ())), preferred_element_type=F32)


def _dot_tn(a, b):
    return lax.dot_general(a, b, (((0,), (0,)), ((), ())), preferred_element_type=F32)


def _split_bf16(x):
    hi = x.astype(BF16)
    lo = (x - hi.astype(F32)).astype(BF16)
    return hi, lo


def _exchange(arrs, gather, name):
    n = len(arrs)
    n_peer = N_DEV - 1

    def body(*refs):
        ins, outs = refs[:n], refs[n:2 * n]
        send_sems, recv_sems, local_sems = refs[2 * n:]
        x, y, c = lax.axis_index("x"), lax.axis_index("y"), lax.axis_index("c")
        me = 4 * x + 2 * y + c

        def flip(v, bit):
            return 1 - v if bit else v

        peers = []
        for k in range(1, N_DEV):
            px, py, pc = flip(x, (k >> 2) & 1), flip(y, (k >> 1) & 1), flip(c, k & 1)
            peers.append(((px, py, pc), 4 * px + 2 * py + pc))

        local = []
        for a in range(n):
            src = ins[a] if gather else ins[a].at[me]
            cp = pltpu.make_async_copy(src, outs[a].at[me], local_sems.at[a])
            cp.start()
            local.append(cp)
        sent = []
        for k, (peer, peer_id) in enumerate(peers):
            for a in range(n):
                src = ins[a] if gather else ins[a].at[peer_id]
                cp = pltpu.make_async_remote_copy(
                    src_ref=src, dst_ref=outs[a].at[me],
                    send_sem=send_sems.at[a * n_peer + k], recv_sem=recv_sems.at[a * n_peer + k],
                    device_id=peer, device_id_type=pl.DeviceIdType.MESH)
                cp.start()
                sent.append(cp)
        for k, (peer, peer_id) in enumerate(peers):
            for a in range(n):
                src = ins[a] if gather else ins[a].at[peer_id]
                pltpu.make_async_remote_copy(
                    src_ref=src, dst_ref=outs[a].at[peer_id],
                    send_sem=send_sems.at[a * n_peer + k], recv_sem=recv_sems.at[a * n_peer + k],
                    device_id=peer, device_id_type=pl.DeviceIdType.MESH).wait_recv()
        for cp in sent:
            cp.wait_send()
        for cp in local:
            cp.wait()

    out_shape = []
    for a in arrs:
        blk = a.shape if gather else a.shape[1:]
        out_shape.append(jax.ShapeDtypeStruct((N_DEV,) + tuple(blk), a.dtype))
    any_spec = pl.BlockSpec(memory_space=pl.ANY)
    return _pcall(
        body, name=name,
        out_shape=tuple(out_shape),
        in_specs=[any_spec] * n, out_specs=tuple([any_spec] * n),
        scratch_shapes=[pltpu.SemaphoreType.DMA((n * n_peer,)), pltpu.SemaphoreType.DMA((n * n_peer,)),
                        pltpu.SemaphoreType.DMA((n,))],
    )(*arrs)


def _in_proj_fwd(x, g, w_all, layer, name):
    s = x.shape[0]
    tm = min(512, s)

    def body(x_ref, g_ref, w_ref, u_ref, h_ref, hs):
        @pl.when(pl.program_id(1) == 0)
        def _():
            xv = x_ref[...]
            r = lax.rsqrt(jnp.mean(xv * xv, axis=-1, keepdims=True) + RMS_EPS)
            hv = (xv * r * g_ref[...]).astype(BF16)
            hs[...] = hv
            h_ref[...] = hv
        u_ref[...] = _dot(hs[...], w_ref[...])

    return _pcall(
        body, name=name, grid=(s // tm, N_DEV),
        in_specs=[pl.BlockSpec((tm, D_MODEL), lambda i, j: (i, 0)),
                  pl.BlockSpec((1, D_MODEL), lambda i, j: (0, 0)),
                  pl.BlockSpec((None, None, D_MODEL, COLS_PER_DEV), lambda i, j: (j, layer, 0, 0))],
        out_specs=(pl.BlockSpec((tm, COLS_PER_DEV), lambda i, j: (i, j)),
                   pl.BlockSpec((tm, D_MODEL), lambda i, j: (i, 0))),
        out_shape=(jax.ShapeDtypeStruct((s, N_IN), F32), jax.ShapeDtypeStruct((s, D_MODEL), BF16)),
        scratch_shapes=[pltpu.VMEM((tm, D_MODEL), BF16)],
        compiler_params=_params(("parallel", "arbitrary")),
    )(x, g, w_all)


def _pool_window(vs, t0, t, grp):
    ext = vs[pl.ds(t0, t + POOL_HALO), :]
    s2 = ext + pltpu.roll(ext, 1, 0)
    s4 = s2 + pltpu.roll(s2, 2, 0)
    s8 = s4 + pltpu.roll(s4, 4, 0)
    s16 = s8 + pltpu.roll(s8, 8, 0)
    sel = jnp.where(grp == 0, s2, jnp.where(grp == 1, s4, jnp.where(grp == 2, s8, s16)))
    return sel[POOL_HALO:, :], ext[POOL_HALO:, :]


def _pool_count(t0, t, grp):
    pos = t0 + lax.broadcasted_iota(jnp.int32, (t, 1), 0)
    return jnp.minimum(pos + 1, jnp.left_shift(2, grp)).astype(F32)


def _pool_fwd(u, pool_w, pool_scale, name):
    s = u.shape[0]
    t = min(256, s)

    def body(pv_ref, pg_ref, w_ref, sc_ref, y_ref, vs):
        grp = pl.program_id(0)
        vs[0:POOL_HALO, :] = jnp.zeros((POOL_HALO, LANES), F32)
        vs[POOL_HALO:, :] = pv_ref[...]
        wb = w_ref[...].astype(BF16)
        scale = sc_ref[...]

        def tile(i, carry):
            t0 = pl.multiple_of(i * t, t)
            win, v = _pool_window(vs, t0, t, grp)
            pooled = win / _pool_count(t0, t, grp) - v
            mixed = _dot(pooled.astype(BF16), wb)
            gate = pg_ref[pl.ds(t0, t), :]
            y_ref[pl.ds(t0, t), :] = (mixed * scale * (gate * _sigmoid(gate))).astype(BF16)
            return carry

        lax.fori_loop(0, s // t, tile, 0)

    return _pcall(
        body, name=name, grid=(4,),
        in_specs=[pl.BlockSpec((s, LANES), lambda g: (0, CB_POOL_V + g)),
                  pl.BlockSpec((s, LANES), lambda g: (0, CB_POOL_G + g)),
                  pl.BlockSpec((None, LANES, LANES), lambda g: (g, 0, 0)),
                  pl.BlockSpec((1, LANES), lambda g: (0, g))],
        out_specs=pl.BlockSpec((s, LANES), lambda g: (0, g)),
        out_shape=jax.ShapeDtypeStruct((s, WIDTH), BF16),
        scratch_shapes=[pltpu.VMEM((POOL_HALO + s, LANES), F32)],
        compiler_params=_params(("arbitrary",)),
    )(u, u, pool_w, pool_scale)


def _conv_taps(zs, t0, t):
    ext = zs[pl.ds(t0, t + CONV_HALO), :]
    z0 = ext[CONV_HALO:, :]
    z1 = pltpu.roll(ext, 1, 0)[CONV_HALO:, :]
    z2 = pltpu.roll(ext, 2, 0)[CONV_HALO:, :]
    return z0, z1, z2


def _conv_fwd(u, conv_w, conv_b, name):
    s = u.shape[0]
    t = min(256, s)

    def body(xc_ref, gb_ref, gc_ref, cg_ref, w_ref, b_ref, y_ref, zs):
        zs[0:CONV_HALO, :] = jnp.zeros((CONV_HALO, LANES), F32)
        zs[CONV_HALO:, :] = gc_ref[...] * xc_ref[...]
        w0, w1, w2 = w_ref[0:1, :], w_ref[1:2, :], w_ref[2:3, :]
        bias = b_ref[...]

        def tile(i, carry):
            t0 = pl.multiple_of(i * t, t)
            z0, z1, z2 = _conv_taps(zs, t0, t)
            conv = w0 * z2 + w1 * z1 + w2 * z0
            gate = cg_ref[pl.ds(t0, t), :]
            y = gb_ref[pl.ds(t0, t), :] * (conv + bias) * (gate * _sigmoid(gate))
            y_ref[pl.ds(t0, t), :] = y.astype(BF16)
            return carry

        lax.fori_loop(0, s // t, tile, 0)

    col = lambda base: pl.BlockSpec((s, LANES), lambda j: (0, base + j))
    return _pcall(
        body, name=name, grid=(4,),
        in_specs=[col(CB_CONV_X), col(CB_CONV_GB), col(CB_CONV_GC), col(CB_CONV_G),
                  pl.BlockSpec((3, LANES), lambda j: (0, j)),
                  pl.BlockSpec((1, LANES), lambda j: (0, j))],
        out_specs=pl.BlockSpec((s, LANES), lambda j: (0, j)),
        out_shape=jax.ShapeDtypeStruct((s, WIDTH), BF16),
        scratch_shapes=[pltpu.VMEM((CONV_HALO + s, LANES), F32)],
        compiler_params=_params(("arbitrary",)),
    )(u, u, u, u, conv_w, conv_b)


def _sb_masks(tq, tk):
    lane_q = lax.broadcasted_iota(jnp.int32, (tq, LANES), 1) < HEAD_DIM
    lane_k = lax.broadcasted_iota(jnp.int32, (tk, LANES), 1) < HEAD_DIM
    row = lax.broadcasted_iota(jnp.int32, (tq, tk), 0)
    col = lax.broadcasted_iota(jnp.int32, (tq, tk), 1)
    return lane_q, lane_k, col < row


def _suffix_matrix(tk, inclusive):
    r = lax.broadcasted_iota(jnp.int32, (tk, 2 * tk), 0)
    c = lax.broadcasted_iota(jnp.int32, (tk, 2 * tk), 1)
    tri = (r >= c) if inclusive else (r > c)
    return jnp.where(c >= tk, 1.0, jnp.where(tri, 1.0, 0.0)).astype(BF16)


def _suffix_sums(x, m):
    hi, lo = _split_bf16(x)
    return _dot(hi, m) + _dot(lo, m)


def _sb_weights(qh, kb, carry, mask, u_strict, tk):
    z = _dot_nt(qh, kb)
    ls = jnp.minimum(z, 0.0) - jnp.log(1.0 + jnp.exp(-jnp.abs(z)))
    lk = ls - z
    if mask is not None:
        lk = jnp.where(mask, lk, 0.0)
    cs = _suffix_sums(lk, u_strict)
    w = jnp.exp(ls + cs[:, :tk] + carry)
    if mask is not None:
        w = jnp.where(mask, w, 0.0)
    return w, ls, carry + cs[:, tk:]


def _sb_fwd(u, name):
    s = u.shape[0]
    tq = tk = min(128, s)

    def body(q_ref, k_ref, v_ref, g_ref, o_ref, y_ref):
        i = pl.program_id(1)
        lane_q, lane_k, mask = _sb_masks(tq, tk)
        u_strict = _suffix_matrix(tk, False)
        qv = q_ref[...] * SB_SCALE
        q0 = jnp.where(lane_q, qv, 0.0).astype(BF16)
        q1 = jnp.where(lane_q, 0.0, qv).astype(BF16)

        def block(kj, acc, c0, c1, diag_mask):
            off = pl.multiple_of(kj * tk, tk)
            kb = k_ref[pl.ds(off, tk), :].astype(BF16)
            vf = v_ref[pl.ds(off, tk), :]
            vcat = jnp.concatenate([jnp.where(lane_k, vf, 0.0), jnp.where(lane_k, 0.0, vf)], axis=0).astype(BF16)
            w0, _, c0 = _sb_weights(q0, kb, c0, diag_mask, u_strict, tk)
            w1, _, c1 = _sb_weights(q1, kb, c1, diag_mask, u_strict, tk)
            wcat = jnp.concatenate([w0.astype(BF16), w1.astype(BF16)], axis=1)
            return acc + _dot(wcat, vcat), c0, c1

        zero = jnp.zeros((tq, tk), F32)
        state = block(i, jnp.zeros((tq, LANES), F32), zero, zero, mask)
        acc, _, _ = lax.fori_loop(0, i, lambda n, st: block(i - 1 - n, *st, None), state)
        o_ref[...] = acc
        gate = g_ref[...]
        y_ref[...] = (acc * (gate * _sigmoid(gate))).astype(BF16)

    return _pcall(
        body, name=name, grid=(4, s // tq),
        in_specs=[pl.BlockSpec((tq, LANES), lambda p, i: (i, CB_SB_Q + p)),
                  pl.BlockSpec((s, LANES), lambda p, i: (0, CB_SB_K + p)),
                  pl.BlockSpec((s, LANES), lambda p, i: (0, CB_SB_V + p)),
                  pl.BlockSpec((tq, LANES), lambda p, i: (i, CB_SB_G + p))],
        out_specs=(pl.BlockSpec((tq, LANES), lambda p, i: (i, p)),
                   pl.BlockSpec((tq, LANES), lambda p, i: (i, p))),
        out_shape=(jax.ShapeDtypeStruct((s, WIDTH), F32), jax.ShapeDtypeStruct((s, WIDTH), BF16)),
        compiler_params=_params(("parallel", "arbitrary")),
    )(u, u, u, u)


def _merge_out_fwd(y_pool, y_conv, y_sb, u, wb_all, wo_all, x, g_post, layer, name):
    s = x.shape[0]
    tm = min(256, s)

    def body(yp, yc, ys, m0, m1, m2, wb_ref, wo_ref, x_ref, g_ref, out_ref, merged_ref, pre_ref):
        merged = jnp.zeros((tm, D_MODEL), F32)
        for n, (y_ref, m_ref) in enumerate(((yp, m0), (yc, m1), (ys, m2))):
            merged = merged + _sigmoid(m_ref[...]) * _dot(y_ref[...], wb_ref[n])
        mb = merged.astype(BF16)
        merged_ref[...] = mb
        pre = _dot(mb, wo_ref[...].reshape(D_MODEL, D_MODEL))
        pre_ref[...] = pre
        r = lax.rsqrt(jnp.mean(pre * pre, axis=-1, keepdims=True) + RMS_EPS)
        out_ref[...] = x_ref[...] + pre * r * g_ref[...]

    rows = lambda w: pl.BlockSpec((tm, w), lambda i: (i, 0))
    merge = lambda n: pl.BlockSpec((tm, D_MODEL), lambda i: (i, MERGE_BLOCK_1024 + n))
    return _pcall(
        body, name=name, grid=(s // tm,),
        in_specs=[rows(WIDTH), rows(WIDTH), rows(WIDTH), merge(0), merge(1), merge(2),
                  pl.BlockSpec((None, 3, WIDTH, D_MODEL), lambda i: (layer, 0, 0, 0)),
                  pl.BlockSpec((N_DEV, None, D_MODEL // N_DEV, D_MODEL), lambda i: (0, layer, 0, 0)),
                  rows(D_MODEL), pl.BlockSpec((1, D_MODEL), lambda i: (0, 0))],
        out_specs=(rows(D_MODEL), rows(D_MODEL), rows(D_MODEL)),
        out_shape=(jax.ShapeDtypeStruct((s, D_MODEL), F32), jax.ShapeDtypeStruct((s, D_MODEL), BF16),
                   jax.ShapeDtypeStruct((s, D_MODEL), F32)),
        compiler_params=_params(("arbitrary",)),
    )(y_pool, y_conv, y_sb, u, u, u, wb_all, wo_all, x, g_post)


def _loss_and_grad(y, target, name):
    s = y.shape[0]
    tm = min(512, s)

    def body(y_ref, t_ref, dy_ref, loss_ref, acc):
        i = pl.program_id(0)

        @pl.when(i == 0)
        def _():
            acc[...] = jnp.zeros_like(acc)
        err = y_ref[...] - t_ref[...]
        dy_ref[...] = err / D_MODEL
        acc[...] += jnp.sum(err * err, axis=0, keepdims=True)

        @pl.when(i == pl.num_programs(0) - 1)
        def _():
            total = jnp.sum(acc[...], axis=1, keepdims=True) * (0.5 / D_MODEL)
            loss_ref[...] = jnp.broadcast_to(total, (1, LANES))

    return _pcall(
        body, name=name, grid=(s // tm,),
        in_specs=[pl.BlockSpec((tm, D_MODEL), lambda i: (i, 0)), pl.BlockSpec((tm, D_MODEL), lambda i: (i, 0))],
        out_specs=(pl.BlockSpec((tm, D_MODEL), lambda i: (i, 0)), pl.BlockSpec((1, LANES), lambda i: (0, 0))),
        out_shape=(jax.ShapeDtypeStruct((s, D_MODEL), F32), jax.ShapeDtypeStruct((1, LANES), F32)),
        scratch_shapes=[pltpu.VMEM((1, D_MODEL), F32)],
        compiler_params=_params(("arbitrary",)),
    )(y, target)


def _out_proj_bwd(dy, pre, g_post, merged, wo_all, layer, name):
    s = dy.shape[0]
    tm = min(256, s)
    n_tiles = s // tm

    def body(dy_ref, pre_ref, g_ref, mg_ref, wo_ref, dm_ref, dwo_ref, dg_ref, acc):
        i = pl.program_id(0)

        @pl.when(i == 0)
        def _():
            acc[...] = jnp.zeros_like(acc)
            dg_ref[...] = jnp.zeros_like(dg_ref)
        dyv, pre_v = dy_ref[...], pre_ref[...]
        r = lax.rsqrt(jnp.mean(pre_v * pre_v, axis=-1, keepdims=True) + RMS_EPS)
        dg_ref[...] += jnp.sum(dyv * pre_v * r, axis=0, keepdims=True)
        a = dyv * g_ref[...]
        dpre = r * a - pre_v * (r * r * r) * jnp.mean(a * pre_v, axis=-1, keepdims=True)
        db = dpre.astype(BF16)
        acc[...] += _dot_tn(mg_ref[...], db)
        dm_ref[...] = _dot_nt(db, wo_ref[...].reshape(D_MODEL, D_MODEL))

        @pl.when(i == n_tiles - 1)
        def _():
            dwo_ref[...] = acc[...].astype(BF16)

    rows = lambda: pl.BlockSpec((tm, D_MODEL), lambda i: (i, 0))
    return _pcall(
        body, name=name, grid=(n_tiles,),
        in_specs=[rows(), rows(), pl.BlockSpec((1, D_MODEL), lambda i: (0, 0)), rows(),
                  pl.BlockSpec((N_DEV, None, D_MODEL // N_DEV, D_MODEL), lambda i: (0, layer, 0, 0))],
        out_specs=(rows(), pl.BlockSpec((D_MODEL, D_MODEL), lambda i: (0, 0)),
                   pl.BlockSpec((1, D_MODEL), lambda i: (0, 0))),
        out_shape=(jax.ShapeDtypeStruct((s, D_MODEL), F32), jax.ShapeDtypeStruct((D_MODEL, D_MODEL), BF16),
                   jax.ShapeDtypeStruct((1, D_MODEL), F32)),
        scratch_shapes=[pltpu.VMEM((D_MODEL, D_MODEL), F32)],
        compiler_params=_params(("arbitrary",)),
    )(dy, pre, g_post, merged, wo_all)


def _merge_bwd(dmerged, y_pool, y_conv, y_sb, u, wb_all, layer, name):
    s = dmerged.shape[0]
    tm = min(256, s)
    n_tiles = s // tm
    cols = D_MODEL // N_DEV

    def body(dm_ref, yp, yc, ys, m0, m1, m2, wb_ref, dum_ref, dyp, dyc, dys, dwb_ref, acc):
        i = pl.program_id(0)

        @pl.when(i == 0)
        def _():
            acc[...] = jnp.zeros_like(acc)
        dm = dm_ref[...]
        for n, (y_ref, m_ref, dy_ref) in enumerate(((yp, m0, dyp), (yc, m1, dyc), (ys, m2, dys))):
            yv = y_ref[...]
            wb = wb_ref[n]
            gate = _sigmoid(m_ref[...])
            proj = _dot(yv, wb)
            dum_ref[:, n * D_MODEL:(n + 1) * D_MODEL] = (dm * proj * gate * (1.0 - gate)).astype(BF16)
            dproj = (dm * gate).astype(BF16)
            acc[n] += _dot_tn(yv, dproj)
            dy_ref[...] = _dot_nt(dproj, wb)

        @pl.when(i == n_tiles - 1)
        def _():
            for j in range(N_DEV):
                for n in range(3):
                    dwb_ref[j, n] = acc[n, :, j * cols:(j + 1) * cols].astype(BF16)

    rows = lambda w: pl.BlockSpec((tm, w), lambda i: (i, 0))
    merge = lambda n: pl.BlockSpec((tm, D_MODEL), lambda i: (i, MERGE_BLOCK_1024 + n))
    return _pcall(
        body, name=name, grid=(n_tiles,),
        in_specs=[rows(D_MODEL), rows(WIDTH), rows(WIDTH), rows(WIDTH), merge(0), merge(1), merge(2),
                  pl.BlockSpec((None, 3, WIDTH, D_MODEL), lambda i: (layer, 0, 0, 0))],
        out_specs=(rows(3 * D_MODEL), rows(WIDTH), rows(WIDTH), rows(WIDTH),
                   pl.BlockSpec((N_DEV, 3, WIDTH, cols), lambda i: (0, 0, 0, 0))),
        out_shape=(jax.ShapeDtypeStruct((s, 3 * D_MODEL), BF16),
                   jax.ShapeDtypeStruct((s, WIDTH), F32), jax.ShapeDtypeStruct((s, WIDTH), F32),
                   jax.ShapeDtypeStruct((s, WIDTH), F32),
                   jax.ShapeDtypeStruct((N_DEV, 3, WIDTH, cols), BF16)),
        scratch_shapes=[pltpu.VMEM((3, WIDTH, D_MODEL), F32)],
        compiler_params=_params(("arbitrary",)),
    )(dmerged, y_pool, y_conv, y_sb, u, u, u, wb_all)


def _sb_bwd(u, o, dys, name):
    s = u.shape[0]
    tq = tk = min(128, s)

    def body(q_ref, k_ref, v_ref, g_ref, o_ref, dys_ref, dq_ref, dk_ref, dv_ref, dg_ref):
        i = pl.program_id(1)

        @pl.when(i == 0)
        def _():
            dk_ref[...] = jnp.zeros_like(dk_ref)
            dv_ref[...] = jnp.zeros_like(dv_ref)
        lane_q, lane_k, mask = _sb_masks(tq, tk)
        u_strict = _suffix_matrix(tk, False)
        u_incl = _suffix_matrix(tk, True)

        gate = g_ref[...]
        sg = _sigmoid(gate)
        dy = dys_ref[...]
        ov = o_ref[...]
        dg_ref[...] = (dy * ov * (sg * (1.0 + gate * (1.0 - sg)))).astype(BF16)
        do = (dy * (gate * sg)).astype(BF16)
        prod = do.astype(F32) * ov
        d0 = jnp.broadcast_to(jnp.sum(jnp.where(lane_q, prod, 0.0), axis=1, keepdims=True), (tq, tk))
        d1 = jnp.broadcast_to(jnp.sum(jnp.where(lane_q, 0.0, prod), axis=1, keepdims=True), (tq, tk))
        do0 = jnp.where(lane_q, do, jnp.zeros_like(do))
        do1 = jnp.where(lane_q, jnp.zeros_like(do), do)
        docat = jnp.concatenate([do0, do1], axis=0)
        qv = q_ref[...] * SB_SCALE
        q0 = jnp.where(lane_q, qv, 0.0).astype(BF16)
        q1 = jnp.where(lane_q, 0.0, qv).astype(BF16)
        qcat = jnp.concatenate([q0, q1], axis=0)

        def head(qh, doh, dsum, kb, vb, cl, cg, diag_mask):
            w, ls, cl = _sb_weights(qh, kb, cl, diag_mask, u_strict, tk)
            wb = w.astype(BF16)
            g = _dot_nt(doh, vb) * wb.astype(F32)
            gs = _suffix_sums(g, u_incl)
            before = dsum - (gs[:, :tk] + cg)
            dz = g - jnp.exp(ls) * (g + before)
            if diag_mask is not None:
                dz = jnp.where(diag_mask, dz, 0.0)
            return wb, dz.astype(BF16), cl, cg + gs[:, tk:]

        def block(kj, dq, cl0, cg0, cl1, cg1, diag_mask):
            off = pl.multiple_of(kj * tk, tk)
            kf = k_ref[pl.ds(off, tk), :]
            kb = kf.astype(BF16)
            vb = v_ref[pl.ds(off, tk), :].astype(BF16)
            kcat = jnp.concatenate([jnp.where(lane_k, kf, 0.0), jnp.where(lane_k, 0.0, kf)], axis=0).astype(BF16)
            w0, dz0, cl0, cg0 = head(q0, do0, d0, kb, vb, cl0, cg0, diag_mask)
            w1, dz1, cl1, cg1 = head(q1, do1, d1, kb, vb, cl1, cg1, diag_mask)
            dq = dq + _dot(jnp.concatenate([dz0, dz1], axis=1), kcat)
            dk_ref[pl.ds(off, tk), :] += _dot_tn(jnp.concatenate([dz0, dz1], axis=0), qcat)
            dv_ref[pl.ds(off, tk), :] += _dot_tn(jnp.concatenate([w0, w1], axis=0), docat)
            return dq, cl0, cg0, cl1, cg1

        zero = jnp.zeros((tq, tk), F32)
        state = block(i, jnp.zeros((tq, LANES), F32), zero, zero, zero, zero, mask)
        state = lax.fori_loop(0, i, lambda n, st: block(i - 1 - n, *st, None), state)
        dq_ref[...] = (state[0] * SB_SCALE).astype(BF16)

    qblk = lambda base: pl.BlockSpec((tq, LANES), lambda p, i: (i, base + p))
    full = lambda base: pl.BlockSpec((s, LANES), lambda p, i: (0, base + p))
    return _pcall(
        body, name=name, grid=(4, s // tq),
        in_specs=[qblk(CB_SB_Q), full(CB_SB_K), full(CB_SB_V), qblk(CB_SB_G), qblk(0), qblk(0)],
        out_specs=(qblk(0), full(0), full(0), qblk(0)),
        out_shape=(jax.ShapeDtypeStruct((s, WIDTH), BF16), jax.ShapeDtypeStruct((s, WIDTH), F32),
                   jax.ShapeDtypeStruct((s, WIDTH), F32), jax.ShapeDtypeStruct((s, WIDTH), BF16)),
        compiler_params=_params(("parallel", "arbitrary")),
    )(u, u, u, u, o, dys)


def _conv_bwd(u, conv_w, conv_b, dyc, name):
    s = u.shape[0]
    t = min(256, s)
    n_tiles = s // t

    def body(xc_ref, gb_ref, gc_ref, cg_ref, w_ref, b_ref, dy_ref,
             dxc_ref, dgb_ref, dgc_ref, dcg_ref, dw_ref, db_ref, zs, ds):
        zs[0:CONV_HALO, :] = jnp.zeros((CONV_HALO, LANES), F32)
        zs[CONV_HALO:, :] = gc_ref[...] * xc_ref[...]
        ds[s:, :] = jnp.zeros((CONV_HALO, LANES), F32)
        w0, w1, w2 = w_ref[0:1, :], w_ref[1:2, :], w_ref[2:3, :]
        bias = b_ref[...]

        def first(i, sums):
            t0 = pl.multiple_of(i * t, t)
            z0, z1, z2 = _conv_taps(zs, t0, t)
            pre = w0 * z2 + w1 * z1 + w2 * z0 + bias
            gate = cg_ref[pl.ds(t0, t), :]
            sg = _sigmoid(gate)
            gb = gb_ref[pl.ds(t0, t), :]
            dy = dy_ref[pl.ds(t0, t), :]
            dcg_ref[pl.ds(t0, t), :] = (dy * gb * pre * (sg * (1.0 + gate * (1.0 - sg)))).astype(BF16)
            dgb_ref[pl.ds(t0, t), :] = (dy * pre * (gate * sg)).astype(BF16)
            dc = dy * gb * (gate * sg)
            ds[pl.ds(t0, t), :] = dc
            red = lambda v: jnp.sum(v, axis=0, keepdims=True)
            return (sums[0] + red(dc * z2), sums[1] + red(dc * z1), sums[2] + red(dc * z0), sums[3] + red(dc))

        zrow = jnp.zeros((1, LANES), F32)
        sw0, sw1, sw2, sb = lax.fori_loop(0, n_tiles, first, (zrow, zrow, zrow, zrow))
        dw_ref[0:1, :] = sw0
        dw_ref[1:2, :] = sw1
        dw_ref[2:3, :] = sw2
        db_ref[...] = sb

        def second(i, carry):
            t0 = pl.multiple_of(i * t, t)
            ext = ds[pl.ds(t0, t + CONV_HALO), :]
            n = t + CONV_HALO
            d0 = ext[:t, :]
            d1 = pltpu.roll(ext, n - 1, 0)[:t, :]
            d2 = pltpu.roll(ext, n - 2, 0)[:t, :]
            dz = w2 * d0 + w1 * d1 + w0 * d2
            dgc_ref[pl.ds(t0, t), :] = (dz * xc_ref[pl.ds(t0, t), :]).astype(BF16)
            dxc_ref[pl.ds(t0, t), :] = (dz * gc_ref[pl.ds(t0, t), :]).astype(BF16)
            return carry

        lax.fori_loop(0, n_tiles, second, 0)

    col = lambda base: pl.BlockSpec((s, LANES), lambda j: (0, base + j))
    dcol = jax.ShapeDtypeStruct((s, WIDTH), BF16)
    return _pcall(
        body, name=name, grid=(4,),
        in_specs=[col(CB_CONV_X), col(CB_CONV_GB), col(CB_CONV_GC), col(CB_CONV_G),
                  pl.BlockSpec((3, LANES), lambda j: (0, j)), pl.BlockSpec((1, LANES), lambda j: (0, j)), col(0)],
        out_specs=(col(0), col(0), col(0), col(0),
                   pl.BlockSpec((3, LANES), lambda j: (0, j)), pl.BlockSpec((1, LANES), lambda j: (0, j))),
        out_shape=(dcol, dcol, dcol, dcol,
                   jax.ShapeDtypeStruct((3, WIDTH), F32), jax.ShapeDtypeStruct((1, WIDTH), F32)),
        scratch_shapes=[pltpu.VMEM((CONV_HALO + s, LANES), F32), pltpu.VMEM((s + CONV_HALO, LANES), F32)],
        compiler_params=_params(("arbitrary",)),
    )(u, u, u, u, conv_w, conv_b, dyc)


def _pool_bwd(u, pool_w, pool_scale, dyp, name):
    s = u.shape[0]
    t = min(256, s)
    n_tiles = s // t

    def body(pv_ref, pg_ref, w_ref, sc_ref, dy_ref, dpv_ref, dpg_ref, dw_ref, dsc_ref, vs, es, dps):
        grp = pl.program_id(0)
        vs[0:POOL_HALO, :] = jnp.zeros((POOL_HALO, LANES), F32)
        vs[POOL_HALO:, :] = pv_ref[...]
        es[s:, :] = jnp.zeros((POOL_HALO, LANES), F32)
        wb = w_ref[...].astype(BF16)
        scale = sc_ref[...]

        def first(i, sums):
            dw, dsc = sums
            t0 = pl.multiple_of(i * t, t)
            win, v = _pool_window(vs, t0, t, grp)
            cnt = _pool_count(t0, t, grp)
            pb = (win / cnt - v).astype(BF16)
            mixed = _dot(pb, wb)
            gate = pg_ref[pl.ds(t0, t), :]
            sg = _sigmoid(gate)
            dy = dy_ref[pl.ds(t0, t), :]
            dpg_ref[pl.ds(t0, t), :] = (dy * (mixed * scale) * (sg * (1.0 + gate * (1.0 - sg)))).astype(BF16)
            dms = dy * (gate * sg)
            dsc = dsc + jnp.sum(dms * mixed, axis=0, keepdims=True)
            dmb = (dms * scale).astype(BF16)
            dw = dw + _dot_tn(pb, dmb)
            dpooled = _dot_nt(dmb, wb)
            dps[pl.ds(t0, t), :] = dpooled
            es[pl.ds(t0, t), :] = dpooled / cnt
            return dw, dsc

        dw, dsc = lax.fori_loop(0, n_tiles, first, (jnp.zeros((LANES, LANES), F32), jnp.zeros((1, LANES), F32)))
        dw_ref[...] = dw
        dsc_ref[...] = dsc

        def second(i, carry):
            t0 = pl.multiple_of(i * t, t)
            ext = es[pl.ds(t0, t + POOL_HALO), :]
            n = t + POOL_HALO
            f2 = ext + pltpu.roll(ext, n - 1, 0)
            f4 = f2 + pltpu.roll(f2, n - 2, 0)
            f8 = f4 + pltpu.roll(f4, n - 4, 0)
            f16 = f8 + pltpu.roll(f8, n - 8, 0)
            sel = jnp.where(grp == 0, f2, jnp.where(grp == 1, f4, jnp.where(grp == 2, f8, f16)))
            dpv_ref[pl.ds(t0, t), :] = (sel[:t, :] - dps[pl.ds(t0, t), :]).astype(BF16)
            return carry

        lax.fori_loop(0, n_tiles, second, 0)

    col = lambda base: pl.BlockSpec((s, LANES), lambda g: (0, base + g))
    dcol = jax.ShapeDtypeStruct((s, WIDTH), BF16)
    return _pcall(
        body, name=name, grid=(4,),
        in_specs=[col(CB_POOL_V), col(CB_POOL_G), pl.BlockSpec((None, LANES, LANES), lambda g: (g, 0, 0)),
                  pl.BlockSpec((1, LANES), lambda g: (0, g)), col(0)],
        out_specs=(col(0), col(0), pl.BlockSpec((None, LANES, LANES), lambda g: (g, 0, 0)),
                   pl.BlockSpec((1, LANES), lambda g: (0, g))),
        out_shape=(dcol, dcol, jax.ShapeDtypeStruct((4, LANES, LANES), F32), jax.ShapeDtypeStruct((1, WIDTH), F32)),
        scratch_shapes=[pltpu.VMEM((POOL_HALO + s, LANES), F32), pltpu.VMEM((s + POOL_HALO, LANES), F32),
                        pltpu.VMEM((s, LANES), F32)],
        compiler_params=_params(("arbitrary",)),
    )(u, u, pool_w, pool_scale, dyp)


def _in_proj_bwd_x(du, w_all, x, g_pre, dy, layer, name):
    s = x.shape[0]
    tm = min(512, s)

    def body(du_ref, w_ref, x_ref, g_ref, dy_ref, dx_ref, dg_ref, acc):
        i, k = pl.program_id(0), pl.program_id(1)

        @pl.when(k == 0)
        def _():
            acc[...] = jnp.zeros_like(acc)

        @pl.when((k == 0) & (i == 0))
        def _():
            dg_ref[...] = jnp.zeros_like(dg_ref)
        acc[...] += _dot_nt(du_ref[...], w_ref[...])

        @pl.when(k == N_DEV - 1)
        def _():
            dh, xv = acc[...], x_ref[...]
            r = lax.rsqrt(jnp.mean(xv * xv, axis=-1, keepdims=True) + RMS_EPS)
            dg_ref[...] += jnp.sum(dh * xv * r, axis=0, keepdims=True)
            a = dh * g_ref[...]
            dx_ref[...] = dy_ref[...] + r * a - xv * (r * r * r) * jnp.mean(a * xv, axis=-1, keepdims=True)

    rows = lambda: pl.BlockSpec((tm, D_MODEL), lambda i, k: (i, 0))
    vec = lambda: pl.BlockSpec((1, D_MODEL), lambda i, k: (0, 0))
    return _pcall(
        body, name=name, grid=(s // tm, N_DEV),
        in_specs=[pl.BlockSpec((tm, COLS_PER_DEV), lambda i, k: (i, k)),
                  pl.BlockSpec((None, None, D_MODEL, COLS_PER_DEV), lambda i, k: (k, layer, 0, 0)),
                  rows(), vec(), rows()],
        out_specs=(rows(), vec()),
        out_shape=(jax.ShapeDtypeStruct((s, D_MODEL), F32), jax.ShapeDtypeStruct((1, D_MODEL), F32)),
        scratch_shapes=[pltpu.VMEM((tm, D_MODEL), F32)],
        compiler_params=_params(("arbitrary", "arbitrary")),
    )(du, w_all, x, g_pre, dy)


def _in_proj_bwd_w(h, du, name):
    s = h.shape[0]
    tk = min(512, s)
    n_k = s // tk

    def body(h_ref, du_ref, out_ref, acc):
        k = pl.program_id(1)

        @pl.when(k == 0)
        def _():
            acc[...] = jnp.zeros_like(acc)
        acc[...] += _dot_tn(h_ref[...], du_ref[...])

        @pl.when(k == n_k - 1)
        def _():
            out_ref[...] = acc[...].astype(BF16)

    return _pcall(
        body, name=name, grid=(N_DEV, n_k),
        in_specs=[pl.BlockSpec((tk, D_MODEL), lambda j, k: (k, 0)),
                  pl.BlockSpec((tk, COLS_PER_DEV), lambda j, k: (k, j))],
        out_specs=pl.BlockSpec((None, D_MODEL, COLS_PER_DEV), lambda j, k: (j, 0, 0)),
        out_shape=jax.ShapeDtypeStruct((N_DEV, D_MODEL, COLS_PER_DEV), BF16),
        scratch_shapes=[pltpu.VMEM((D_MODEL, COLS_PER_DEV), F32)],
        compiler_params=_params(("parallel", "arbitrary")),
    )(h, du)


def _adamw_math(g, w, m, v):
    m_new = ADAM_B1 * m + (1.0 - ADAM_B1) * g
    v_new = ADAM_B2 * v + (1.0 - ADAM_B2) * (g * g)
    m_hat = m_new / (1.0 - ADAM_B1 ** ADAM_STEP)
    v_hat = v_new / (1.0 - ADAM_B2 ** ADAM_STEP)
    delta = -ADAM_LR * (m_hat / (jnp.sqrt(v_hat) + ADAM_EPS) + ADAM_WD * w)
    return delta, m_new, v_new


def _sum_partials(p_ref):
    total = p_ref[0].astype(F32)
    for d in range(1, N_DEV):
        total = total + p_ref[d].astype(F32)
    return total


def _adamw_layers(parts0, parts1, w, m, v, name):
    _, r, c = w.shape
    tr = min(128, r)
    n_r = r // tr

    def body(p0_ref, p1_ref, w_ref, m_ref, v_ref, g_ref, d_ref, mo_ref, vo_ref):
        layer = pl.program_id(0)

        @pl.when(layer == 0)
        def _():
            g_ref[...] = _sum_partials(p0_ref)

        @pl.when(layer == 1)
        def _():
            g_ref[...] = _sum_partials(p1_ref)
        d_ref[...], mo_ref[...], vo_ref[...] = _adamw_math(g_ref[...], w_ref[...], m_ref[...], v_ref[...])

    part = lambda which: pl.BlockSpec((N_DEV, tr, c), lambda l, i: (0, jnp.where(l == which, i, 0), 0))
    par = lambda: pl.BlockSpec((None, tr, c), lambda l, i: (l, i, 0))
    out = jax.ShapeDtypeStruct(w.shape, F32)
    return _pcall(
        body, name=name, grid=(2, n_r),
        in_specs=[part(0), part(1), par(), par(), par()],
        out_specs=(par(), par(), par(), par()),
        out_shape=(out, out, out, out),
        compiler_params=_params(("arbitrary", "arbitrary")),
    )(parts0, parts1, w, m, v)


def _adamw_small(parts, w, m, v, name):
    def body(p_ref, w_ref, m_ref, v_ref, g_ref, d_ref, mo_ref, vo_ref):
        g = _sum_partials(p_ref)
        g_ref[...] = g
        d_ref[...], mo_ref[...], vo_ref[...] = _adamw_math(g, w_ref[...], m_ref[...], v_ref[...])

    out = jax.ShapeDtypeStruct(w.shape, F32)
    return _pcall(body, name=name, out_shape=(out, out, out, out), compiler_params=_params())(parts, w, m, v)


def _adamw_plain(g, w, m, v, name):
    def body(g_ref, w_ref, m_ref, v_ref, d_ref, mo_ref, vo_ref):
        d_ref[...], mo_ref[...], vo_ref[...] = _adamw_math(g_ref[...], w_ref[...], m_ref[...], v_ref[...])

    out = jax.ShapeDtypeStruct(w.shape, F32)
    return _pcall(body, name=name, out_shape=(out, out, out), compiler_params=_params())(g, w, m, v)


def _rows128(a):
    return a.reshape(-1, LANES)


SMALL_NAMES = ("pre_norm_g", "pool_w", "pool_scale", "conv_w", "conv_b", "post_norm_g")


def kernel(x, pre_norm_g, w_in, pool_w, pool_scale, conv_w, conv_b, w_branch, w_out, post_norm_g, loss_target, m_pre_norm_g, m_w_in, m_pool_w, m_pool_scale, m_conv_w, m_conv_b, m_w_branch, m_w_out, m_post_norm_g, v_pre_norm_g, v_w_in, v_pool_w, v_pool_scale, v_conv_w, v_conv_b, v_w_branch, v_w_out, v_post_norm_g):
    s = x.shape[1]
    me = 4 * lax.axis_index("x") + 2 * lax.axis_index("y") + lax.axis_index("c")
    x0 = x[0]
    target = loss_target[0]
    conv_cols = conv_w.shape[-1]

    conv_w_pad = jnp.pad(conv_w.reshape(2 * 3, conv_cols), ((0, 2), (0, LANES - conv_cols)))
    w_in_all, wb_g, wo_all, cw_g = _exchange(
        [w_in.astype(BF16), w_branch.astype(BF16), w_out.astype(BF16), conv_w_pad], True, "gather_weights")
    wb_all = wb_g.transpose(1, 2, 3, 0, 4).reshape(2, 3, WIDTH, D_MODEL)
    conv_w_full = cw_g[:, :6, :conv_cols].reshape(N_DEV, 2, 3, conv_cols).transpose(1, 2, 0, 3).reshape(2, 3, WIDTH)

    saved = []
    xin = x0
    for l in range(2):
        u, h = _in_proj_fwd(xin, pre_norm_g[l:l + 1], w_in_all, l, f"in_proj_fwd_{l}")
        y_pool = _pool_fwd(u, pool_w[l], pool_scale[l:l + 1], f"pool_fwd_{l}")
        y_conv = _conv_fwd(u, conv_w_full[l], conv_b[l:l + 1], f"conv_fwd_{l}")
        o_sb, y_sb = _sb_fwd(u, f"sb_fwd_{l}")
        xout, merged, pre = _merge_out_fwd(y_pool, y_conv, y_sb, u, wb_all, wo_all, xin, post_norm_g[l:l + 1], l,
                                           f"merge_out_fwd_{l}")
        saved.append((xin, u, h, y_pool, y_conv, y_sb, o_sb, merged, pre))
        xin = xout

    dy, loss_row = _loss_and_grad(xin, target, "loss")

    small = [None, None]
    recv = [None, None]
    for l in (1, 0):
        xl, u, h, y_pool, y_conv, y_sb, o_sb, merged, pre = saved[l]
        dmerged, dwo, dg_post = _out_proj_bwd(dy, pre, post_norm_g[l:l + 1], merged, wo_all, l, f"out_proj_bwd_{l}")
        du_merge, dyp, dyc, dys, dwb = _merge_bwd(dmerged, y_pool, y_conv, y_sb, u, wb_all, l, f"merge_bwd_{l}")
        dq, dk, dv, dsg = _sb_bwd(u, o_sb, dys, f"sb_bwd_{l}")
        dxc, dgb, dgc, dcg, dcw, dcb = _conv_bwd(u, conv_w_full[l], conv_b[l:l + 1], dyc, f"conv_bwd_{l}")
        dpv, dpg, dpw, dps = _pool_bwd(u, pool_w[l], pool_scale[l:l + 1], dyp, f"pool_bwd_{l}")
        du = jnp.concatenate([dpv, dpg, dxc, dgb, dgc, dcg, dq, dk.astype(BF16), dv.astype(BF16), dsg, du_merge],
                             axis=1)
        dx, dg_pre = _in_proj_bwd_x(du, w_in_all, xl, pre_norm_g[l:l + 1], dy, l, f"in_proj_bwd_x_{l}")
        dwi = _in_proj_bwd_w(h, du, f"in_proj_bwd_w_{l}")
        recv[l] = _exchange([dwi, dwb.reshape(N_DEV, 3 * WIDTH, D_MODEL // N_DEV),
                             dwo.reshape(N_DEV, D_MODEL // N_DEV, D_MODEL)], False, f"scatter_grads_{l}")
        small[l] = dict(pre_norm_g=dg_pre, pool_w=dpw, pool_scale=dps, conv_w=dcw, conv_b=dcb, post_norm_g=dg_post)
        dy = dx
    grad_x = dy[None]

    packed = jnp.concatenate(
        [_rows128(jnp.stack([small[0][n], small[1][n]])) for n in SMALL_NAMES]
        + [jnp.pad(loss_row, ((0, 7), (0, 0)))], axis=0)
    (packed_all,) = _exchange([packed], True, "gather_small")
    sizes = dict(pre_norm_g=16, pool_w=1024, pool_scale=8, conv_w=24, conv_b=8, post_norm_g=16)
    n_rows = sum(sizes.values())
    loss = jnp.sum(packed_all[:, n_rows, 0])

    given = dict(pre_norm_g=(pre_norm_g, m_pre_norm_g, v_pre_norm_g), pool_w=(pool_w, m_pool_w, v_pool_w),
                 pool_scale=(pool_scale, m_pool_scale, v_pool_scale), conv_b=(conv_b, m_conv_b, v_conv_b),
                 post_norm_g=(post_norm_g, m_post_norm_g, v_post_norm_g))
    zeros_cw = jnp.zeros((sizes["conv_w"], LANES), F32)
    pack3 = [jnp.concatenate([zeros_cw if n == "conv_w" else _rows128(given[n][k]) for n in SMALL_NAMES], axis=0)
             for k in range(3)]
    sg, sd, sm, sv = _adamw_small(packed_all[:, :n_rows], pack3[0], pack3[1], pack3[2], "adamw_small")

    def unpack(buf, name, shape):
        start = 0
        for n in SMALL_NAMES:
            if n == name:
                return buf[start:start + sizes[n]].reshape(shape)
            start += sizes[n]

    out = {}
    for n in ("pre_norm_g", "pool_w", "pool_scale", "conv_b", "post_norm_g"):
        shape = given[n][0].shape
        out[n] = tuple(unpack(b, n, shape) for b in (sg, sd, sm, sv))
    g_cw = lax.dynamic_slice_in_dim(unpack(sg, "conv_w", (2, 3, WIDTH)), me * conv_cols, conv_cols, axis=2)
    cw2 = lambda a: a.reshape(6, conv_cols)
    d_cw, m_cw, v_cw = _adamw_plain(cw2(g_cw), cw2(conv_w), cw2(m_conv_w), cw2(v_conv_w), "adamw_conv_w")
    out["conv_w"] = (g_cw,) + tuple(a.reshape(2, 3, conv_cols) for a in (d_cw, m_cw, v_cw))

    out["w_in"] = _adamw_layers(recv[0][0], recv[1][0], w_in, m_w_in, v_w_in, "adamw_w_in")
    cols = D_MODEL // N_DEV
    wb3 = lambda a: a.reshape(2, 3 * WIDTH, cols)
    out["w_branch"] = tuple(a.reshape(2, 3, WIDTH, cols) for a in _adamw_layers(
        recv[0][1], recv[1][1], wb3(w_branch), wb3(m_w_branch), wb3(v_w_branch), "adamw_w_branch"))
    out["w_out"] = _adamw_layers(recv[0][2], recv[1][2], w_out, m_w_out, v_w_out, "adamw_w_out")

    order = ("pre_norm_g", "w_in", "pool_w", "pool_scale", "conv_w", "conv_b", "w_branch", "w_out", "post_norm_g")
    return (loss, grad_x) + tuple(out[n][k] for k in range(4) for n in order)
```

```python
import functools

import jax
import jax.numpy as jnp
from jax import lax
from jax.experimental import pallas as pl
from jax.experimental.pallas import tpu as pltpu

F32 = jnp.float32
BF16 = jnp.bfloat16

N_DEV = 8
D_MODEL = 1024
WIDTH = 512
N_IN = 8192
COLS_PER_DEV = N_IN // N_DEV
HEAD_DIM = 64
LANES = 128
SB_SCALE = HEAD_DIM ** -0.5
LOG2E = 1.4426950408889634
RMS_EPS = 1e-6
POOL_HALO = 16
CONV_HALO = 8
ADAM_LR, ADAM_B1, ADAM_B2, ADAM_EPS, ADAM_WD, ADAM_STEP = 0.001, 0.9, 0.999, 1e-08, 0.01, 10
VMEM_LIMIT = 56 * 1024 * 1024

CB_POOL_V, CB_POOL_G = 0, 4
CB_CONV_X, CB_CONV_GB, CB_CONV_GC, CB_CONV_G = 8, 12, 16, 20
CB_SB_Q, CB_SB_K, CB_SB_V, CB_SB_G = 24, 28, 32, 36
MERGE_BLOCK_1024 = 5


def _pcall(body, **kw):
    return pl.pallas_call(body, **kw)


def _params(sem=None):
    if sem is None:
        return pltpu.CompilerParams(vmem_limit_bytes=VMEM_LIMIT)
    return pltpu.CompilerParams(dimension_semantics=sem, vmem_limit_bytes=VMEM_LIMIT)


def _sigmoid(x):
    return 1.0 / (1.0 + jnp.exp(-x))


def _dot(a, b):
    return jnp.dot(a, b, preferred_element_type=F32)


def _dot_nt(a, b):
    return lax.dot_general(a, b, (((1,), (1,)), ((), ())), preferred_element_type=F32)


def _dot_tn(a, b):
    return lax.dot_general(a, b, (((0,), (0,)), ((), ())), preferred_element_type=F32)


def _split_bf16(x):
    hi = x.astype(BF16)
    lo = (x - hi.astype(F32)).astype(BF16)
    return hi, lo


def _exchange(arrs, gather, name):
    n = len(arrs)
    n_peer = N_DEV - 1

    def body(*refs):
        ins, outs = refs[:n], refs[n:2 * n]
        send_sems, recv_sems, local_sems = refs[2 * n:]
        x, y, c = lax.axis_index("x"), lax.axis_index("y"), lax.axis_index("c")
        me = 4 * x + 2 * y + c

        def flip(v, bit):
            return 1 - v if bit else v

        peers = []
        for k in range(1, N_DEV):
            px, py, pc = flip(x, (k >> 2) & 1), flip(y, (k >> 1) & 1), flip(c, k & 1)
            peers.append(((px, py, pc), 4 * px + 2 * py + pc))

        local = []
        for a in range(n):
            src = ins[a] if gather else ins[a].at[me]
            cp = pltpu.make_async_copy(src, outs[a].at[me], local_sems.at[a])
            cp.start()
            local.append(cp)
        sent = []
        for k, (peer, peer_id) in enumerate(peers):
            for a in range(n):
                src = ins[a] if gather else ins[a].at[peer_id]
                cp = pltpu.make_async_remote_copy(
                    src_ref=src, dst_ref=outs[a].at[me],
                    send_sem=send_sems.at[a * n_peer + k], recv_sem=recv_sems.at[a * n_peer + k],
                    device_id=peer, device_id_type=pl.DeviceIdType.MESH)
                cp.start()
                sent.append(cp)
        for k, (peer, peer_id) in enumerate(peers):
            for a in range(n):
                src = ins[a] if gather else ins[a].at[peer_id]
                pltpu.make_async_remote_copy(
                    src_ref=src, dst_ref=outs[a].at[peer_id],
                    send_sem=send_sems.at[a * n_peer + k], recv_sem=recv_sems.at[a * n_peer + k],
                    device_id=peer, device_id_type=pl.DeviceIdType.MESH).wait_recv()
        for cp in sent:
            cp.wait_send()
        for cp in local:
            cp.wait()

    out_shape = []
    for a in arrs:
        blk = a.shape if gather else a.shape[1:]
        out_shape.append(jax.ShapeDtypeStruct((N_DEV,) + tuple(blk), a.dtype))
    any_spec = pl.BlockSpec(memory_space=pl.ANY)
    return _pcall(
        body, name=name,
        out_shape=tuple(out_shape),
        in_specs=[any_spec] * n, out_specs=tuple([any_spec] * n),
        scratch_shapes=[pltpu.SemaphoreType.DMA((n * n_peer,)), pltpu.SemaphoreType.DMA((n * n_peer,)),
                        pltpu.SemaphoreType.DMA((n,))],
    )(*arrs)


def _in_proj_fwd(x, g, w_all, layer, name):
    s = x.shape[0]
    tm = min(512, s)

    def body(x_ref, g_ref, w_ref, u_ref, h_ref, hs):
        @pl.when(pl.program_id(1) == 0)
        def _():
            xv = x_ref[...]
            r = lax.rsqrt(jnp.mean(xv * xv, axis=-1, keepdims=True) + RMS_EPS)
            hv = (xv * r * g_ref[...]).astype(BF16)
            hs[...] = hv
            h_ref[...] = hv
        u_ref[...] = _dot(hs[...], w_ref[...])

    return _pcall(
        body, name=name, grid=(s // tm, N_DEV),
        in_specs=[pl.BlockSpec((tm, D_MODEL), lambda i, j: (i, 0)),
                  pl.BlockSpec((1, D_MODEL), lambda i, j: (0, 0)),
                  pl.BlockSpec((None, None, D_MODEL, COLS_PER_DEV), lambda i, j: (j, layer, 0, 0))],
        out_specs=(pl.BlockSpec((tm, COLS_PER_DEV), lambda i, j: (i, j)),
                   pl.BlockSpec((tm, D_MODEL), lambda i, j: (i, 0))),
        out_shape=(jax.ShapeDtypeStruct((s, N_IN), F32), jax.ShapeDtypeStruct((s, D_MODEL), BF16)),
        scratch_shapes=[pltpu.VMEM((tm, D_MODEL), BF16)],
        compiler_params=_params(("parallel", "arbitrary")),
    )(x, g, w_all)


def _pool_window(vs, t0, t, grp):
    ext = vs[pl.ds(t0, t + POOL_HALO), :]
    s2 = ext + pltpu.roll(ext, 1, 0)
    s4 = s2 + pltpu.roll(s2, 2, 0)
    s8 = s4 + pltpu.roll(s4, 4, 0)
    s16 = s8 + pltpu.roll(s8, 8, 0)
    sel = jnp.where(grp == 0, s2, jnp.where(grp == 1, s4, jnp.where(grp == 2, s8, s16)))
    return sel[POOL_HALO:, :], ext[POOL_HALO:, :]


def _pool_count(t0, t, grp):
    pos = t0 + lax.broadcasted_iota(jnp.int32, (t, 1), 0)
    return jnp.minimum(pos + 1, jnp.left_shift(2, grp)).astype(F32)


def _pool_fwd(u, pool_w, pool_scale, name):
    s = u.shape[0]
    t = min(256, s)

    def body(pv_ref, pg_ref, w_ref, sc_ref, y_ref, vs):
        grp = pl.program_id(0)
        vs[0:POOL_HALO, :] = jnp.zeros((POOL_HALO, LANES), F32)
        vs[POOL_HALO:, :] = pv_ref[...]
        wb = w_ref[...].astype(BF16)
        scale = sc_ref[...]

        def tile(i, carry):
            t0 = pl.multiple_of(i * t, t)
            win, v = _pool_window(vs, t0, t, grp)
            pooled = win / _pool_count(t0, t, grp) - v
            mixed = _dot(pooled.astype(BF16), wb)
            gate = pg_ref[pl.ds(t0, t), :]
            y_ref[pl.ds(t0, t), :] = (mixed * scale * (gate * _sigmoid(gate))).astype(BF16)
            return carry

        lax.fori_loop(0, s // t, tile, 0)

    return _pcall(
        body, name=name, grid=(4,),
        in_specs=[pl.BlockSpec((s, LANES), lambda g: (0, CB_POOL_V + g)),
                  pl.BlockSpec((s, LANES), lambda g: (0, CB_POOL_G + g)),
                  pl.BlockSpec((None, LANES, LANES), lambda g: (g, 0, 0)),
                  pl.BlockSpec((1, LANES), lambda g: (0, g))],
        out_specs=pl.BlockSpec((s, LANES), lambda g: (0, g)),
        out_shape=jax.ShapeDtypeStruct((s, WIDTH), BF16),
        scratch_shapes=[pltpu.VMEM((POOL_HALO + s, LANES), F32)],
        compiler_params=_params(("arbitrary",)),
    )(u, u, pool_w, pool_scale)


def _conv_taps(zs, t0, t):
    ext = zs[pl.ds(t0, t + CONV_HALO), :]
    z0 = ext[CONV_HALO:, :]
    z1 = pltpu.roll(ext, 1, 0)[CONV_HALO:, :]
    z2 = pltpu.roll(ext, 2, 0)[CONV_HALO:, :]
    return z0, z1, z2


def _conv_fwd(u, conv_w, conv_b, name):
    s = u.shape[0]
    t = min(256, s)

    def body(xc_ref, gb_ref, gc_ref, cg_ref, w_ref, b_ref, y_ref, zs):
        zs[0:CONV_HALO, :] = jnp.zeros((CONV_HALO, LANES), F32)
        zs[CONV_HALO:, :] = gc_ref[...] * xc_ref[...]
        w0, w1, w2 = w_ref[0:1, :], w_ref[1:2, :], w_ref[2:3, :]
        bias = b_ref[...]

        def tile(i, carry):
            t0 = pl.multiple_of(i * t, t)
            z0, z1, z2 = _conv_taps(zs, t0, t)
            conv = w0 * z2 + w1 * z1 + w2 * z0
            gate = cg_ref[pl.ds(t0, t), :]
            y = gb_ref[pl.ds(t0, t), :] * (conv + bias) * (gate * _sigmoid(gate))
            y_ref[pl.ds(t0, t), :] = y.astype(BF16)
            return carry

        lax.fori_loop(0, s // t, tile, 0)

    col = lambda base: pl.BlockSpec((s, LANES), lambda j: (0, base + j))
    return _pcall(
        body, name=name, grid=(4,),
        in_specs=[col(CB_CONV_X), col(CB_CONV_GB), col(CB_CONV_GC), col(CB_CONV_G),
                  pl.BlockSpec((3, LANES), lambda j: (0, j)),
                  pl.BlockSpec((1, LANES), lambda j: (0, j))],
        out_specs=pl.BlockSpec((s, LANES), lambda j: (0, j)),
        out_shape=jax.ShapeDtypeStruct((s, WIDTH), BF16),
        scratch_shapes=[pltpu.VMEM((CONV_HALO + s, LANES), F32)],
        compiler_params=_params(("arbitrary",)),
    )(u, u, u, u, conv_w, conv_b)


def _first_head_lanes(rows, width=LANES):
    lane = lax.broadcasted_iota(jnp.int32, (rows, width), 1)
    return jnp.bitwise_and(lane, LANES - 1) < HEAD_DIM


def _stack_heads(x, first):
    zero = jnp.zeros_like(x)
    return jnp.concatenate([jnp.where(first, x, zero), jnp.where(first, zero, x)], axis=0).astype(BF16)


def _causal_mask(tq, tk, copies):
    row = lax.broadcasted_iota(jnp.int32, (tq, tk), 0)
    col = lax.broadcasted_iota(jnp.int32, (tq, tk), 1)
    return jnp.concatenate([col < row] * copies, axis=0)


def _suffix_matrix(tk, inclusive):
    r = lax.broadcasted_iota(jnp.int32, (2 * tk, 2 * tk), 0)
    c = lax.broadcasted_iota(jnp.int32, (2 * tk, 2 * tk), 1)
    r = jnp.where(r >= tk, r - tk, r)
    tri = (r >= c) if inclusive else (r > c)
    return jnp.where(c >= tk, 1.0, jnp.where(tri, 1.0, 0.0)).astype(BF16)


def _suffix_sums(x, m):
    hi, lo = _split_bf16(x)
    return _dot(jnp.concatenate([hi, lo], axis=1), m)


def _sb_log_terms(zz, mask, m_strict):
    ls = jnp.minimum(zz, 0.0) - jnp.log2(1.0 + jnp.exp2(-jnp.abs(zz)))
    lk = ls - zz
    if mask is not None:
        lk = jnp.where(mask, lk, 0.0)
    return ls, _suffix_sums(lk, m_strict)


def _sb_fwd(u, name):
    s = u.shape[0]
    tq = tk = min(128, s)
    pairs = 2
    width = pairs * LANES
    rows = 2 * pairs * tq
    lanes_of = lambda a: slice(a * LANES, (a + 1) * LANES)

    def body(q_ref, k_ref, v_ref, g_ref, o_ref, y_ref, kbf, vst):
        i = pl.program_id(1)

        @pl.when(i == 0)
        def _():
            kbf[...] = k_ref[...].astype(BF16)
            first_s = _first_head_lanes(s, width)
            vf = v_ref[...]
            vst[0] = jnp.where(first_s, vf, 0.0).astype(BF16)
            vst[1] = jnp.where(first_s, 0.0, vf).astype(BF16)

        first = _first_head_lanes(tq)
        mask = _causal_mask(tq, tk, 2 * pairs)
        m_strict = _suffix_matrix(tk, False)
        qcat = jnp.concatenate([_stack_heads(q_ref[:, lanes_of(a)] * SB_SCALE, first) for a in range(pairs)], axis=0)

        def scores(b):
            off = pl.multiple_of(jnp.maximum(b, 0) * tk, tk)
            z = [_dot_nt(qcat[a * 2 * tq:(a + 1) * 2 * tq], kbf[pl.ds(off, tk), lanes_of(a)]) for a in range(pairs)]
            return jnp.concatenate(z, axis=0) * LOG2E

        def consume(b, ls, cs, carry, accs, m):
            w = jnp.exp2(ls + cs[:, :tk] + carry)
            if m is not None:
                w = jnp.where(m, w, 0.0)
            wb = w.astype(BF16)
            off = pl.multiple_of(b * tk, tk)
            new = []
            for a in range(pairs):
                r0 = a * 2 * tq
                wcat = jnp.concatenate([wb[r0:r0 + tq], wb[r0 + tq:r0 + 2 * tq]], axis=1)
                vcat = jnp.concatenate([vst[0, pl.ds(off, tk), lanes_of(a)], vst[1, pl.ds(off, tk), lanes_of(a)]],
                                       axis=0)
                new.append(accs[a] + _dot(wcat, vcat))
            return carry + cs[:, tk:], tuple(new)

        ls, cs = _sb_log_terms(scores(i), mask, m_strict)
        accs = tuple(jnp.zeros((tq, LANES), F32) for _ in range(pairs))
        carry, accs = consume(i, ls, cs, jnp.zeros((rows, tk), F32), accs, mask)
        ls, cs = _sb_log_terms(scores(i - 1), None, m_strict)

        def step(n, st):
            ls, cs, carry, accs = st
            carry, accs = consume(i - n, ls, cs, carry, accs, None)
            ls, cs = _sb_log_terms(scores(i - n - 1), None, m_strict)
            return ls, cs, carry, accs

        accs = lax.fori_loop(1, i + 1, step, (ls, cs, carry, accs))[3]
        o = jnp.concatenate(accs, axis=1)
        o_ref[...] = o
        gate = g_ref[...]
        y_ref[...] = (o * (gate * _sigmoid(gate))).astype(BF16)

    base = lambda cb: cb // pairs
    qblk = lambda cb: pl.BlockSpec((tq, width), lambda p, i: (i, base(cb) + p))
    full = lambda cb: pl.BlockSpec((s, width), lambda p, i: (0, base(cb) + p))
    return _pcall(
        body, name=name, grid=(4 // pairs, s // tq),
        in_specs=[qblk(CB_SB_Q), full(CB_SB_K), full(CB_SB_V), qblk(CB_SB_G)],
        out_specs=(qblk(0), qblk(0)),
        out_shape=(jax.ShapeDtypeStruct((s, WIDTH), F32), jax.ShapeDtypeStruct((s, WIDTH), BF16)),
        scratch_shapes=[pltpu.VMEM((s, width), BF16), pltpu.VMEM((2, s, width), BF16)],
        compiler_params=_params(("parallel", "arbitrary")),
    )(u, u, u, u)


def _merge_out_fwd(y_pool, y_conv, y_sb, u, wb_all, wo_all, x, g_post, layer, name):
    s = x.shape[0]
    tm = min(256, s)

    def body(yp, yc, ys, m0, m1, m2, wb_ref, wo_ref, x_ref, g_ref, out_ref, merged_ref, pre_ref):
        merged = jnp.zeros((tm, D_MODEL), F32)
        for n, (y_ref, m_ref) in enumerate(((yp, m0), (yc, m1), (ys, m2))):
            merged = merged + _sigmoid(m_ref[...]) * _dot(y_ref[...], wb_ref[n])
        mb = merged.astype(BF16)
        merged_ref[...] = mb
        pre = _dot(mb, wo_ref[...].reshape(D_MODEL, D_MODEL))
        pre_ref[...] = pre
        r = lax.rsqrt(jnp.mean(pre * pre, axis=-1, keepdims=True) + RMS_EPS)
        out_ref[...] = x_ref[...] + pre * r * g_ref[...]

    rows = lambda w: pl.BlockSpec((tm, w), lambda i: (i, 0))
    merge = lambda n: pl.BlockSpec((tm, D_MODEL), lambda i: (i, MERGE_BLOCK_1024 + n))
    return _pcall(
        body, name=name, grid=(s // tm,),
        in_specs=[rows(WIDTH), rows(WIDTH), rows(WIDTH), merge(0), merge(1), merge(2),
                  pl.BlockSpec((None, 3, WIDTH, D_MODEL), lambda i: (layer, 0, 0, 0)),
                  pl.BlockSpec((N_DEV, None, D_MODEL // N_DEV, D_MODEL), lambda i: (0, layer, 0, 0)),
                  rows(D_MODEL), pl.BlockSpec((1, D_MODEL), lambda i: (0, 0))],
        out_specs=(rows(D_MODEL), rows(D_MODEL), rows(D_MODEL)),
        out_shape=(jax.ShapeDtypeStruct((s, D_MODEL), F32), jax.ShapeDtypeStruct((s, D_MODEL), BF16),
                   jax.ShapeDtypeStruct((s, D_MODEL), F32)),
        compiler_params=_params(("arbitrary",)),
    )(y_pool, y_conv, y_sb, u, u, u, wb_all, wo_all, x, g_post)


def _loss_and_grad(y, target, name):
    s = y.shape[0]
    tm = min(512, s)

    def body(y_ref, t_ref, dy_ref, loss_ref, acc):
        i = pl.program_id(0)

        @pl.when(i == 0)
        def _():
            acc[...] = jnp.zeros_like(acc)
        err = y_ref[...] - t_ref[...]
        dy_ref[...] = err / D_MODEL
        acc[...] += jnp.sum(err * err, axis=0, keepdims=True)

        @pl.when(i == pl.num_programs(0) - 1)
        def _():
            total = jnp.sum(acc[...], axis=1, keepdims=True) * (0.5 / D_MODEL)
            loss_ref[...] = jnp.broadcast_to(total, (1, LANES))

    return _pcall(
        body, name=name, grid=(s // tm,),
        in_specs=[pl.BlockSpec((tm, D_MODEL), lambda i: (i, 0)), pl.BlockSpec((tm, D_MODEL), lambda i: (i, 0))],
        out_specs=(pl.BlockSpec((tm, D_MODEL), lambda i: (i, 0)), pl.BlockSpec((1, LANES), lambda i: (0, 0))),
        out_shape=(jax.ShapeDtypeStruct((s, D_MODEL), F32), jax.ShapeDtypeStruct((1, LANES), F32)),
        scratch_shapes=[pltpu.VMEM((1, D_MODEL), F32)],
        compiler_params=_params(("arbitrary",)),
    )(y, target)


def _out_proj_bwd(dy, pre, g_post, merged, wo_all, layer, name):
    s = dy.shape[0]
    tm = min(256, s)
    n_tiles = s // tm

    def body(dy_ref, pre_ref, g_ref, mg_ref, wo_ref, dm_ref, dwo_ref, dg_ref, acc):
        i = pl.program_id(0)

        @pl.when(i == 0)
        def _():
            acc[...] = jnp.zeros_like(acc)
            dg_ref[...] = jnp.zeros_like(dg_ref)
        dyv, pre_v = dy_ref[...], pre_ref[...]
        r = lax.rsqrt(jnp.mean(pre_v * pre_v, axis=-1, keepdims=True) + RMS_EPS)
        dg_ref[...] += jnp.sum(dyv * pre_v * r, axis=0, keepdims=True)
        a = dyv * g_ref[...]
        dpre = r * a - pre_v * (r * r * r) * jnp.mean(a * pre_v, axis=-1, keepdims=True)
        db = dpre.astype(BF16)
        acc[...] += _dot_tn(mg_ref[...], db)
        dm_ref[...] = _dot_nt(db, wo_ref[...].reshape(D_MODEL, D_MODEL))

        @pl.when(i == n_tiles - 1)
        def _():
            dwo_ref[...] = acc[...].astype(BF16)

    rows = lambda: pl.BlockSpec((tm, D_MODEL), lambda i: (i, 0))
    return _pcall(
        body, name=name, grid=(n_tiles,),
        in_specs=[rows(), rows(), pl.BlockSpec((1, D_MODEL), lambda i: (0, 0)), rows(),
                  pl.BlockSpec((N_DEV, None, D_MODEL // N_DEV, D_MODEL), lambda i: (0, layer, 0, 0))],
        out_specs=(rows(), pl.BlockSpec((D_MODEL, D_MODEL), lambda i: (0, 0)),
                   pl.BlockSpec((1, D_MODEL), lambda i: (0, 0))),
        out_shape=(jax.ShapeDtypeStruct((s, D_MODEL), F32), jax.ShapeDtypeStruct((D_MODEL, D_MODEL), BF16),
                   jax.ShapeDtypeStruct((1, D_MODEL), F32)),
        scratch_shapes=[pltpu.VMEM((D_MODEL, D_MODEL), F32)],
        compiler_params=_params(("arbitrary",)),
    )(dy, pre, g_post, merged, wo_all)


def _merge_bwd(dmerged, y_pool, y_conv, y_sb, u, wb_all, layer, name):
    s = dmerged.shape[0]
    tm = min(256, s)
    n_tiles = s // tm
    cols = D_MODEL // N_DEV

    def body(dm_ref, yp, yc, ys, m0, m1, m2, wb_ref, dum_ref, dyp, dyc, dys, dwb_ref, acc):
        i = pl.program_id(0)

        @pl.when(i == 0)
        def _():
            acc[...] = jnp.zeros_like(acc)
        dm = dm_ref[...]
        for n, (y_ref, m_ref, dy_ref) in enumerate(((yp, m0, dyp), (yc, m1, dyc), (ys, m2, dys))):
            yv = y_ref[...]
            wb = wb_ref[n]
            gate = _sigmoid(m_ref[...])
            proj = _dot(yv, wb)
            dum_ref[:, n * D_MODEL:(n + 1) * D_MODEL] = (dm * proj * gate * (1.0 - gate)).astype(BF16)
            dproj = (dm * gate).astype(BF16)
            acc[n] += _dot_tn(yv, dproj)
            dy_ref[...] = _dot_nt(dproj, wb)

        @pl.when(i == n_tiles - 1)
        def _():
            for j in range(N_DEV):
                for n in range(3):
                    dwb_ref[j, n] = acc[n, :, j * cols:(j + 1) * cols].astype(BF16)

    rows = lambda w: pl.BlockSpec((tm, w), lambda i: (i, 0))
    merge = lambda n: pl.BlockSpec((tm, D_MODEL), lambda i: (i, MERGE_BLOCK_1024 + n))
    return _pcall(
        body, name=name, grid=(n_tiles,),
        in_specs=[rows(D_MODEL), rows(WIDTH), rows(WIDTH), rows(WIDTH), merge(0), merge(1), merge(2),
                  pl.BlockSpec((None, 3, WIDTH, D_MODEL), lambda i: (layer, 0, 0, 0))],
        out_specs=(rows(3 * D_MODEL), rows(WIDTH), rows(WIDTH), rows(WIDTH),
                   pl.BlockSpec((N_DEV, 3, WIDTH, cols), lambda i: (0, 0, 0, 0))),
        out_shape=(jax.ShapeDtypeStruct((s, 3 * D_MODEL), BF16),
                   jax.ShapeDtypeStruct((s, WIDTH), F32), jax.ShapeDtypeStruct((s, WIDTH), F32),
                   jax.ShapeDtypeStruct((s, WIDTH), F32),
                   jax.ShapeDtypeStruct((N_DEV, 3, WIDTH, cols), BF16)),
        scratch_shapes=[pltpu.VMEM((3, WIDTH, D_MODEL), F32)],
        compiler_params=_params(("arbitrary",)),
    )(dmerged, y_pool, y_conv, y_sb, u, u, u, wb_all)


def _sb_bwd(u, o, dys, name):
    s = u.shape[0]
    tq = tk = min(128, s)

    def body(q_ref, k_ref, v_ref, g_ref, o_ref, dys_ref, dq_ref, dk_ref, dv_ref, dg_ref, kbf, vbf, kst):
        i = pl.program_id(1)

        @pl.when(i == 0)
        def _():
            dk_ref[...] = jnp.zeros_like(dk_ref)
            dv_ref[...] = jnp.zeros_like(dv_ref)
            kf = k_ref[...]
            kbf[...] = kf.astype(BF16)
            vbf[...] = v_ref[...].astype(BF16)
            first_s = _first_head_lanes(s)
            kst[0] = jnp.where(first_s, kf, 0.0).astype(BF16)
            kst[1] = jnp.where(first_s, 0.0, kf).astype(BF16)

        lane_q = _first_head_lanes(tq)
        mask = _causal_mask(tq, tk, 2)
        m_strict = _suffix_matrix(tk, False)
        m_incl = _suffix_matrix(tk, True)

        gate = g_ref[...]
        sg = _sigmoid(gate)
        dy = dys_ref[...]
        ov = o_ref[...]
        dg_ref[...] = (dy * ov * (sg * (1.0 + gate * (1.0 - sg)))).astype(BF16)
        do = (dy * (gate * sg)).astype(BF16)
        prod = do.astype(F32) * ov
        row_sum = lambda v: jnp.broadcast_to(jnp.sum(v, axis=1, keepdims=True), (tq, tk))
        dsum = jnp.concatenate([row_sum(jnp.where(lane_q, prod, 0.0)), row_sum(jnp.where(lane_q, 0.0, prod))], axis=0)
        docat = _stack_heads(do, lane_q)
        qcat = _stack_heads(q_ref[...] * SB_SCALE, lane_q)

        def block_start(b):
            return pl.multiple_of(jnp.maximum(b, 0) * tk, tk)

        def scores(b, m):
            zz = _dot_nt(qcat, kbf[pl.ds(block_start(b), tk), :]) * LOG2E
            return _sb_log_terms(zz, m, m_strict)

        def weights(b, ls, cs, cl, m):
            dwt = _dot_nt(docat, vbf[pl.ds(block_start(b), tk), :])
            w = jnp.exp2(ls + cs[:, :tk] + cl)
            if m is not None:
                w = jnp.where(m, w, 0.0)
            wb = w.astype(BF16)
            g = dwt * wb.astype(F32)
            return wb, g, _suffix_sums(g, m_incl), cl + cs[:, tk:]

        def grads(b, ls, wb, g, gs, cg, dq, m):
            before = dsum - (gs[:, :tk] + cg)
            dz = g - jnp.exp2(ls) * (g + before)
            if m is not None:
                dz = jnp.where(m, dz, 0.0)
            dzb = dz.astype(BF16)
            off = pl.multiple_of(b * tk, tk)
            kcat = jnp.concatenate([kst[0, pl.ds(off, tk), :], kst[1, pl.ds(off, tk), :]], axis=0)
            dq = dq + _dot(jnp.concatenate([dzb[:tq], dzb[tq:]], axis=1), kcat)
            dk_ref[pl.ds(off, tk), :] += _dot_tn(dzb, qcat)
            dv_ref[pl.ds(off, tk), :] += _dot_tn(wb, docat)
            return cg + gs[:, tk:], dq

        zero = jnp.zeros((2 * tq, tk), F32)
        ls_e, cs = scores(i, mask)
        wb, g, gs, cl = weights(i, ls_e, cs, zero, mask)
        ls_w, cs = scores(i - 1, None)
        cg, dq = grads(i, ls_e, wb, g, gs, zero, jnp.zeros((tq, LANES), F32), mask)
        wb, g, gs, cl = weights(i - 1, ls_w, cs, cl, None)
        ls_e = ls_w
        ls_w, cs = scores(i - 2, None)

        def step(n, st):
            ls_e, wb, g, gs, cg, dq, cl, ls_w, cs = st
            cg, dq = grads(i - n, ls_e, wb, g, gs, cg, dq, None)
            wb, g, gs, cl = weights(i - n - 1, ls_w, cs, cl, None)
            ls_e = ls_w
            ls_w, cs = scores(i - n - 2, None)
            return ls_e, wb, g, gs, cg, dq, cl, ls_w, cs

        st = lax.fori_loop(1, i + 1, step, (ls_e, wb, g, gs, cg, dq, cl, ls_w, cs))
        dq_ref[...] = (st[5] * SB_SCALE).astype(BF16)

    qblk = lambda base: pl.BlockSpec((tq, LANES), lambda p, i: (i, base + p))
    full = lambda base: pl.BlockSpec((s, LANES), lambda p, i: (0, base + p))
    return _pcall(
        body, name=name, grid=(4, s // tq),
        in_specs=[qblk(CB_SB_Q), full(CB_SB_K), full(CB_SB_V), qblk(CB_SB_G), qblk(0), qblk(0)],
        out_specs=(qblk(0), full(0), full(0), qblk(0)),
        out_shape=(jax.ShapeDtypeStruct((s, WIDTH), BF16), jax.ShapeDtypeStruct((s, WIDTH), F32),
                   jax.ShapeDtypeStruct((s, WIDTH), F32), jax.ShapeDtypeStruct((s, WIDTH), BF16)),
        scratch_shapes=[pltpu.VMEM((s, LANES), BF16), pltpu.VMEM((s, LANES), BF16), pltpu.VMEM((2, s, LANES), BF16)],
        compiler_params=_params(("parallel", "arbitrary")),
    )(u, u, u, u, o, dys)


def _conv_bwd(u, conv_w, conv_b, dyc, name):
    s = u.shape[0]
    t = min(256, s)
    n_tiles = s // t

    def body(xc_ref, gb_ref, gc_ref, cg_ref, w_ref, b_ref, dy_ref,
             dxc_ref, dgb_ref, dgc_ref, dcg_ref, dw_ref, db_ref, zs, ds):
        zs[0:CONV_HALO, :] = jnp.zeros((CONV_HALO, LANES), F32)
        zs[CONV_HALO:, :] = gc_ref[...] * xc_ref[...]
        ds[s:, :] = jnp.zeros((CONV_HALO, LANES), F32)
        w0, w1, w2 = w_ref[0:1, :], w_ref[1:2, :], w_ref[2:3, :]
        bias = b_ref[...]

        def first(i, sums):
            t0 = pl.multiple_of(i * t, t)
            z0, z1, z2 = _conv_taps(zs, t0, t)
            pre = w0 * z2 + w1 * z1 + w2 * z0 + bias
            gate = cg_ref[pl.ds(t0, t), :]
            sg = _sigmoid(gate)
            gb = gb_ref[pl.ds(t0, t), :]
            dy = dy_ref[pl.ds(t0, t), :]
            dcg_ref[pl.ds(t0, t), :] = (dy * gb * pre * (sg * (1.0 + gate * (1.0 - sg)))).astype(BF16)
            dgb_ref[pl.ds(t0, t), :] = (dy * pre * (gate * sg)).astype(BF16)
            dc = dy * gb * (gate * sg)
            ds[pl.ds(t0, t), :] = dc
            red = lambda v: jnp.sum(v, axis=0, keepdims=True)
            return (sums[0] + red(dc * z2), sums[1] + red(dc * z1), sums[2] + red(dc * z0), sums[3] + red(dc))

        zrow = jnp.zeros((1, LANES), F32)
        sw0, sw1, sw2, sb = lax.fori_loop(0, n_tiles, first, (zrow, zrow, zrow, zrow))
        dw_ref[0:1, :] = sw0
        dw_ref[1:2, :] = sw1
        dw_ref[2:3, :] = sw2
        db_ref[...] = sb

        def second(i, carry):
            t0 = pl.multiple_of(i * t, t)
            ext = ds[pl.ds(t0, t + CONV_HALO), :]
            n = t + CONV_HALO
            d0 = ext[:t, :]
            d1 = pltpu.roll(ext, n - 1, 0)[:t, :]
            d2 = pltpu.roll(ext, n - 2, 0)[:t, :]
            dz = w2 * d0 + w1 * d1 + w0 * d2
            dgc_ref[pl.ds(t0, t), :] = (dz * xc_ref[pl.ds(t0, t), :]).astype(BF16)
            dxc_ref[pl.ds(t0, t), :] = (dz * gc_ref[pl.ds(t0, t), :]).astype(BF16)
            return carry

        lax.fori_loop(0, n_tiles, second, 0)

    col = lambda base: pl.BlockSpec((s, LANES), lambda j: (0, base + j))
    dcol = jax.ShapeDtypeStruct((s, WIDTH), BF16)
    return _pcall(
        body, name=name, grid=(4,),
        in_specs=[col(CB_CONV_X), col(CB_CONV_GB), col(CB_CONV_GC), col(CB_CONV_G),
                  pl.BlockSpec((3, LANES), lambda j: (0, j)), pl.BlockSpec((1, LANES), lambda j: (0, j)), col(0)],
        out_specs=(col(0), col(0), col(0), col(0),
                   pl.BlockSpec((3, LANES), lambda j: (0, j)), pl.BlockSpec((1, LANES), lambda j: (0, j))),
        out_shape=(dcol, dcol, dcol, dcol,
                   jax.ShapeDtypeStruct((3, WIDTH), F32), jax.ShapeDtypeStruct((1, WIDTH), F32)),
        scratch_shapes=[pltpu.VMEM((CONV_HALO + s, LANES), F32), pltpu.VMEM((s + CONV_HALO, LANES), F32)],
        compiler_params=_params(("arbitrary",)),
    )(u, u, u, u, conv_w, conv_b, dyc)


def _pool_bwd(u, pool_w, pool_scale, dyp, name):
    s = u.shape[0]
    t = min(256, s)
    n_tiles = s // t

    def body(pv_ref, pg_ref, w_ref, sc_ref, dy_ref, dpv_ref, dpg_ref, dw_ref, dsc_ref, vs, es, dps):
        grp = pl.program_id(0)
        vs[0:POOL_HALO, :] = jnp.zeros((POOL_HALO, LANES), F32)
        vs[POOL_HALO:, :] = pv_ref[...]
        es[s:, :] = jnp.zeros((POOL_HALO, LANES), F32)
        wb = w_ref[...].astype(BF16)
        scale = sc_ref[...]

        def first(i, sums):
            dw, dsc = sums
            t0 = pl.multiple_of(i * t, t)
            win, v = _pool_window(vs, t0, t, grp)
            cnt = _pool_count(t0, t, grp)
            pb = (win / cnt - v).astype(BF16)
            mixed = _dot(pb, wb)
            gate = pg_ref[pl.ds(t0, t), :]
            sg = _sigmoid(gate)
            dy = dy_ref[pl.ds(t0, t), :]
            dpg_ref[pl.ds(t0, t), :] = (dy * (mixed * scale) * (sg * (1.0 + gate * (1.0 - sg)))).astype(BF16)
            dms = dy * (gate * sg)
            dsc = dsc + jnp.sum(dms * mixed, axis=0, keepdims=True)
            dmb = (dms * scale).astype(BF16)
            dw = dw + _dot_tn(pb, dmb)
            dpooled = _dot_nt(dmb, wb)
            dps[pl.ds(t0, t), :] = dpooled
            es[pl.ds(t0, t), :] = dpooled / cnt
            return dw, dsc

        dw, dsc = lax.fori_loop(0, n_tiles, first, (jnp.zeros((LANES, LANES), F32), jnp.zeros((1, LANES), F32)))
        dw_ref[...] = dw
        dsc_ref[...] = dsc

        def second(i, carry):
            t0 = pl.multiple_of(i * t, t)
            ext = es[pl.ds(t0, t + POOL_HALO), :]
            n = t + POOL_HALO
            f2 = ext + pltpu.roll(ext, n - 1, 0)
            f4 = f2 + pltpu.roll(f2, n - 2, 0)
            f8 = f4 + pltpu.roll(f4, n - 4, 0)
            f16 = f8 + pltpu.roll(f8, n - 8, 0)
            sel = jnp.where(grp == 0, f2, jnp.where(grp == 1, f4, jnp.where(grp == 2, f8, f16)))
            dpv_ref[pl.ds(t0, t), :] = (sel[:t, :] - dps[pl.ds(t0, t), :]).astype(BF16)
            return carry

        lax.fori_loop(0, n_tiles, second, 0)

    col = lambda base: pl.BlockSpec((s, LANES), lambda g: (0, base + g))
    dcol = jax.ShapeDtypeStruct((s, WIDTH), BF16)
    return _pcall(
        body, name=name, grid=(4,),
        in_specs=[col(CB_POOL_V), col(CB_POOL_G), pl.BlockSpec((None, LANES, LANES), lambda g: (g, 0, 0)),
                  pl.BlockSpec((1, LANES), lambda g: (0, g)), col(0)],
        out_specs=(col(0), col(0), pl.BlockSpec((None, LANES, LANES), lambda g: (g, 0, 0)),
                   pl.BlockSpec((1, LANES), lambda g: (0, g))),
        out_shape=(dcol, dcol, jax.ShapeDtypeStruct((4, LANES, LANES), F32), jax.ShapeDtypeStruct((1, WIDTH), F32)),
        scratch_shapes=[pltpu.VMEM((POOL_HALO + s, LANES), F32), pltpu.VMEM((s + POOL_HALO, LANES), F32),
                        pltpu.VMEM((s, LANES), F32)],
        compiler_params=_params(("arbitrary",)),
    )(u, u, pool_w, pool_scale, dyp)


def _in_proj_bwd_x(du, w_all, x, g_pre, dy, layer, name):
    s = x.shape[0]
    tm = min(512, s)

    def body(du_ref, w_ref, x_ref, g_ref, dy_ref, dx_ref, dg_ref, acc):
        i, k = pl.program_id(0), pl.program_id(1)

        @pl.when(k == 0)
        def _():
            acc[...] = jnp.zeros_like(acc)

        @pl.when((k == 0) & (i == 0))
        def _():
            dg_ref[...] = jnp.zeros_like(dg_ref)
        acc[...] += _dot_nt(du_ref[...], w_ref[...])

        @pl.when(k == N_DEV - 1)
        def _():
            dh, xv = acc[...], x_ref[...]
            r = lax.rsqrt(jnp.mean(xv * xv, axis=-1, keepdims=True) + RMS_EPS)
            dg_ref[...] += jnp.sum(dh * xv * r, axis=0, keepdims=True)
            a = dh * g_ref[...]
            dx_ref[...] = dy_ref[...] + r * a - xv * (r * r * r) * jnp.mean(a * xv, axis=-1, keepdims=True)

    rows = lambda: pl.BlockSpec((tm, D_MODEL), lambda i, k: (i, 0))
    vec = lambda: pl.BlockSpec((1, D_MODEL), lambda i, k: (0, 0))
    return _pcall(
        body, name=name, grid=(s // tm, N_DEV),
        in_specs=[pl.BlockSpec((tm, COLS_PER_DEV), lambda i, k: (i, k)),
                  pl.BlockSpec((None, None, D_MODEL, COLS_PER_DEV), lambda i, k: (k, layer, 0, 0)),
                  rows(), vec(), rows()],
        out_specs=(rows(), vec()),
        out_shape=(jax.ShapeDtypeStruct((s, D_MODEL), F32), jax.ShapeDtypeStruct((1, D_MODEL), F32)),
        scratch_shapes=[pltpu.VMEM((tm, D_MODEL), F32)],
        compiler_params=_params(("arbitrary", "arbitrary")),
    )(du, w_all, x, g_pre, dy)


def _in_proj_bwd_w(h, du, name):
    s = h.shape[0]
    tk = min(512, s)
    n_k = s // tk

    def body(h_ref, du_ref, out_ref, acc):
        k = pl.program_id(1)

        @pl.when(k == 0)
        def _():
            acc[...] = jnp.zeros_like(acc)
        acc[...] += _dot_tn(h_ref[...], du_ref[...])

        @pl.when(k == n_k - 1)
        def _():
            out_ref[...] = acc[...].astype(BF16)

    return _pcall(
        body, name=name, grid=(N_DEV, n_k),
        in_specs=[pl.BlockSpec((tk, D_MODEL), lambda j, k: (k, 0)),
                  pl.BlockSpec((tk, COLS_PER_DEV), lambda j, k: (k, j))],
        out_specs=pl.BlockSpec((None, D_MODEL, COLS_PER_DEV), lambda j, k: (j, 0, 0)),
        out_shape=jax.ShapeDtypeStruct((N_DEV, D_MODEL, COLS_PER_DEV), BF16),
        scratch_shapes=[pltpu.VMEM((D_MODEL, COLS_PER_DEV), F32)],
        compiler_params=_params(("parallel", "arbitrary")),
    )(h, du)


def _adamw_math(g, w, m, v):
    m_new = ADAM_B1 * m + (1.0 - ADAM_B1) * g
    v_new = ADAM_B2 * v + (1.0 - ADAM_B2) * (g * g)
    m_hat = m_new / (1.0 - ADAM_B1 ** ADAM_STEP)
    v_hat = v_new / (1.0 - ADAM_B2 ** ADAM_STEP)
    delta = -ADAM_LR * (m_hat / (jnp.sqrt(v_hat) + ADAM_EPS) + ADAM_WD * w)
    return delta, m_new, v_new


def _sum_partials(p_ref):
    total = p_ref[0].astype(F32)
    for d in range(1, N_DEV):
        total = total + p_ref[d].astype(F32)
    return total


def _adamw_layers(parts0, parts1, w, m, v, name):
    _, r, c = w.shape
    tr = min(128, r)
    n_r = r // tr

    def body(p0_ref, p1_ref, w_ref, m_ref, v_ref, g_ref, d_ref, mo_ref, vo_ref):
        layer = pl.program_id(0)

        @pl.when(layer == 0)
        def _():
            g_ref[...] = _sum_partials(p0_ref)

        @pl.when(layer == 1)
        def _():
            g_ref[...] = _sum_partials(p1_ref)
        d_ref[...], mo_ref[...], vo_ref[...] = _adamw_math(g_ref[...], w_ref[...], m_ref[...], v_ref[...])

    part = lambda which: pl.BlockSpec((N_DEV, tr, c), lambda l, i: (0, jnp.where(l == which, i, 0), 0))
    par = lambda: pl.BlockSpec((None, tr, c), lambda l, i: (l, i, 0))
    out = jax.ShapeDtypeStruct(w.shape, F32)
    return _pcall(
        body, name=name, grid=(2, n_r),
        in_specs=[part(0), part(1), par(), par(), par()],
        out_specs=(par(), par(), par(), par()),
        out_shape=(out, out, out, out),
        compiler_params=_params(("arbitrary", "arbitrary")),
    )(parts0, parts1, w, m, v)


def _adamw_small(parts, w, m, v, name):
    def body(p_ref, w_ref, m_ref, v_ref, g_ref, d_ref, mo_ref, vo_ref):
        g = _sum_partials(p_ref)
        g_ref[...] = g
        d_ref[...], mo_ref[...], vo_ref[...] = _adamw_math(g, w_ref[...], m_ref[...], v_ref[...])

    out = jax.ShapeDtypeStruct(w.shape, F32)
    return _pcall(body, name=name, out_shape=(out, out, out, out), compiler_params=_params())(parts, w, m, v)


def _adamw_plain(g, w, m, v, name):
    def body(g_ref, w_ref, m_ref, v_ref, d_ref, mo_ref, vo_ref):
        d_ref[...], mo_ref[...], vo_ref[...] = _adamw_math(g_ref[...], w_ref[...], m_ref[...], v_ref[...])

    out = jax.ShapeDtypeStruct(w.shape, F32)
    return _pcall(body, name=name, out_shape=(out, out, out), compiler_params=_params())(g, w, m, v)


def _rows128(a):
    return a.reshape(-1, LANES)


SMALL_NAMES = ("pre_norm_g", "pool_w", "pool_scale", "conv_w", "conv_b", "post_norm_g")


def kernel(x, pre_norm_g, w_in, pool_w, pool_scale, conv_w, conv_b, w_branch, w_out, post_norm_g, loss_target, m_pre_norm_g, m_w_in, m_pool_w, m_pool_scale, m_conv_w, m_conv_b, m_w_branch, m_w_out, m_post_norm_g, v_pre_norm_g, v_w_in, v_pool_w, v_pool_scale, v_conv_w, v_conv_b, v_w_branch, v_w_out, v_post_norm_g):
    s = x.shape[1]
    me = 4 * lax.axis_index("x") + 2 * lax.axis_index("y") + lax.axis_index("c")
    x0 = x[0]
    target = loss_target[0]
    conv_cols = conv_w.shape[-1]

    conv_w_pad = jnp.pad(conv_w.reshape(2 * 3, conv_cols), ((0, 2), (0, LANES - conv_cols)))
    w_in_all, wb_g, wo_all, cw_g = _exchange(
        [w_in.astype(BF16), w_branch.astype(BF16), w_out.astype(BF16), conv_w_pad], True, "gather_weights")
    wb_all = wb_g.transpose(1, 2, 3, 0, 4).reshape(2, 3, WIDTH, D_MODEL)
    conv_w_full = cw_g[:, :6, :conv_cols].reshape(N_DEV, 2, 3, conv_cols).transpose(1, 2, 0, 3).reshape(2, 3, WIDTH)

    saved = []
    xin = x0
    for l in range(2):
        u, h = _in_proj_fwd(xin, pre_norm_g[l:l + 1], w_in_all, l, f"in_proj_fwd_{l}")
        y_pool = _pool_fwd(u, pool_w[l], pool_scale[l:l + 1], f"pool_fwd_{l}")
        y_conv = _conv_fwd(u, conv_w_full[l], conv_b[l:l + 1], f"conv_fwd_{l}")
        o_sb, y_sb = _sb_fwd(u, f"sb_fwd_{l}")
        xout, merged, pre = _merge_out_fwd(y_pool, y_conv, y_sb, u, wb_all, wo_all, xin, post_norm_g[l:l + 1], l,
                                           f"merge_out_fwd_{l}")
        saved.append((xin, u, h, y_pool, y_conv, y_sb, o_sb, merged, pre))
        xin = xout

    dy, loss_row = _loss_and_grad(xin, target, "loss")

    small = [None, None]
    recv = [None, None]
    for l in (1, 0):
        xl, u, h, y_pool, y_conv, y_sb, o_sb, merged, pre = saved[l]
        dmerged, dwo, dg_post = _out_proj_bwd(dy, pre, post_norm_g[l:l + 1], merged, wo_all, l, f"out_proj_bwd_{l}")
        du_merge, dyp, dyc, dys, dwb = _merge_bwd(dmerged, y_pool, y_conv, y_sb, u, wb_all, l, f"merge_bwd_{l}")
        dq, dk, dv, dsg = _sb_bwd(u, o_sb, dys, f"sb_bwd_{l}")
        dxc, dgb, dgc, dcg, dcw, dcb = _conv_bwd(u, conv_w_full[l], conv_b[l:l + 1], dyc, f"conv_bwd_{l}")
        dpv, dpg, dpw, dps = _pool_bwd(u, pool_w[l], pool_scale[l:l + 1], dyp, f"pool_bwd_{l}")
        du = jnp.concatenate([dpv, dpg, dxc, dgb, dgc, dcg, dq, dk.astype(BF16), dv.astype(BF16), dsg, du_merge],
                             axis=1)
        dx, dg_pre = _in_proj_bwd_x(du, w_in_all, xl, pre_norm_g[l:l + 1], dy, l, f"in_proj_bwd_x_{l}")
        dwi = _in_proj_bwd_w(h, du, f"in_proj_bwd_w_{l}")
        recv[l] = _exchange([dwi, dwb.reshape(N_DEV, 3 * WIDTH, D_MODEL // N_DEV),
                             dwo.reshape(N_DEV, D_MODEL // N_DEV, D_MODEL)], False, f"scatter_grads_{l}")
        small[l] = dict(pre_norm_g=dg_pre, pool_w=dpw, pool_scale=dps, conv_w=dcw, conv_b=dcb, post_norm_g=dg_post)
        dy = dx
    grad_x = dy[None]

    packed = jnp.concatenate(
        [_rows128(jnp.stack([small[0][n], small[1][n]])) for n in SMALL_NAMES]
        + [jnp.pad(loss_row, ((0, 7), (0, 0)))], axis=0)
    (packed_all,) = _exchange([packed], True, "gather_small")
    sizes = dict(pre_norm_g=16, pool_w=1024, pool_scale=8, conv_w=24, conv_b=8, post_norm_g=16)
    n_rows = sum(sizes.values())
    loss = jnp.sum(packed_all[:, n_rows, 0])

    given = dict(pre_norm_g=(pre_norm_g, m_pre_norm_g, v_pre_norm_g), pool_w=(pool_w, m_pool_w, v_pool_w),
                 pool_scale=(pool_scale, m_pool_scale, v_pool_scale), conv_b=(conv_b, m_conv_b, v_conv_b),
                 post_norm_g=(post_norm_g, m_post_norm_g, v_post_norm_g))
    zeros_cw = jnp.zeros((sizes["conv_w"], LANES), F32)
    pack3 = [jnp.concatenate([zeros_cw if n == "conv_w" else _rows128(given[n][k]) for n in SMALL_NAMES], axis=0)
             for k in range(3)]
    sg, sd, sm, sv = _adamw_small(packed_all[:, :n_rows], pack3[0], pack3[1], pack3[2], "adamw_small")

    def unpack(buf, name, shape):
        start = 0
        for n in SMALL_NAMES:
            if n == name:
                return buf[start:start + sizes[n]].reshape(shape)
            start += sizes[n]

    out = {}
    for n in ("pre_norm_g", "pool_w", "pool_scale", "conv_b", "post_norm_g"):
        shape = given[n][0].shape
        out[n] = tuple(unpack(b, n, shape) for b in (sg, sd, sm, sv))
    g_cw = lax.dynamic_slice_in_dim(unpack(sg, "conv_w", (2, 3, WIDTH)), me * conv_cols, conv_cols, axis=2)
    cw2 = lambda a: a.reshape(6, conv_cols)
    d_cw, m_cw, v_cw = _adamw_plain(cw2(g_cw), cw2(conv_w), cw2(m_conv_w), cw2(v_conv_w), "adamw_conv_w")
    out["conv_w"] = (g_cw,) + tuple(a.reshape(2, 3, conv_cols) for a in (d_cw, m_cw, v_cw))

    out["w_in"] = _adamw_layers(recv[0][0], recv[1][0], w_in, m_w_in, v_w_in, "adamw_w_in")
    cols = D_MODEL // N_DEV
    wb3 = lambda a: a.reshape(2, 3 * WIDTH, cols)
    out["w_branch"] = tuple(a.reshape(2, 3, WIDTH, cols) for a in _adamw_layers(
        recv[0][1], recv[1][1], wb3(w_branch), wb3(m_w_branch), wb3(v_w_branch), "adamw_w_branch"))
    out["w_out"] = _adamw_layers(recv[0][2], recv[1][2], w_out, m_w_out, v_w_out, "adamw_w_out")

    order = ("pre_norm_g", "w_in", "pool_w", "pool_scale", "conv_w", "conv_b", "w_branch", "w_out", "post_norm_g")
    return (loss, grad_x) + tuple(out[n][k] for k in range(4) for n in order)
```

```python
import functools

import jax
import jax.numpy as jnp
from jax import lax
from jax.experimental import pallas as pl
from jax.experimental.pallas import tpu as pltpu

F32 = jnp.float32
BF16 = jnp.bfloat16

N_DEV = 8
D_MODEL = 1024
WIDTH = 512
N_IN = 8192
COLS_PER_DEV = N_IN // N_DEV
HEAD_DIM = 64
LANES = 128
SB_SCALE = HEAD_DIM ** -0.5
RMS_EPS = 1e-6
POOL_HALO = 16
CONV_HALO = 8
ADAM_LR, ADAM_B1, ADAM_B2, ADAM_EPS, ADAM_WD, ADAM_STEP = 0.001, 0.9, 0.999, 1e-08, 0.01, 10
VMEM_LIMIT = 56 * 1024 * 1024

CB_POOL_V, CB_POOL_G = 0, 4
CB_CONV_X, CB_CONV_GB, CB_CONV_GC, CB_CONV_G = 8, 12, 16, 20
CB_SB_Q, CB_SB_K, CB_SB_V, CB_SB_G = 24, 28, 32, 36
MERGE_BLOCK_1024 = 5


def _pcall(body, **kw):
    return pl.pallas_call(body, **kw)


def _params(sem=None):
    if sem is None:
        return pltpu.CompilerParams(vmem_limit_bytes=VMEM_LIMIT)
    return pltpu.CompilerParams(dimension_semantics=sem, vmem_limit_bytes=VMEM_LIMIT)


def _sigmoid(x):
    return 1.0 / (1.0 + jnp.exp(-x))


def _dot(a, b):
    return jnp.dot(a, b, preferred_element_type=F32)


def _dot_nt(a, b):
    return lax.dot_general(a, b, (((1,), (1,)), ((), ())), preferred_element_type=F32)


def _dot_tn(a, b):
    return lax.dot_general(a, b, (((0,), (0,)), ((), ())), preferred_element_type=F32)


def _split_bf16(x):
    hi = x.astype(BF16)
    lo = (x - hi.astype(F32)).astype(BF16)
    return hi, lo


def _exchange(arrs, gather, name):
    n = len(arrs)
    n_peer = N_DEV - 1

    def body(*refs):
        ins, outs = refs[:n], refs[n:2 * n]
        send_sems, recv_sems, local_sems = refs[2 * n:]
        x, y, c = lax.axis_index("x"), lax.axis_index("y"), lax.axis_index("c")
        me = 4 * x + 2 * y + c

        def flip(v, bit):
            return 1 - v if bit else v

        peers = []
        for k in range(1, N_DEV):
            px, py, pc = flip(x, (k >> 2) & 1), flip(y, (k >> 1) & 1), flip(c, k & 1)
            peers.append(((px, py, pc), 4 * px + 2 * py + pc))

        local = []
        for a in range(n):
            src = ins[a] if gather else ins[a].at[me]
            cp = pltpu.make_async_copy(src, outs[a].at[me], local_sems.at[a])
            cp.start()
            local.append(cp)
        sent = []
        for k, (peer, peer_id) in enumerate(peers):
            for a in range(n):
                src = ins[a] if gather else ins[a].at[peer_id]
                cp = pltpu.make_async_remote_copy(
                    src_ref=src, dst_ref=outs[a].at[me],
                    send_sem=send_sems.at[a * n_peer + k], recv_sem=recv_sems.at[a * n_peer + k],
                    device_id=peer, device_id_type=pl.DeviceIdType.MESH)
                cp.start()
                sent.append(cp)
        for k, (peer, peer_id) in enumerate(peers):
            for a in range(n):
                src = ins[a] if gather else ins[a].at[peer_id]
                pltpu.make_async_remote_copy(
                    src_ref=src, dst_ref=outs[a].at[peer_id],
                    send_sem=send_sems.at[a * n_peer + k], recv_sem=recv_sems.at[a * n_peer + k],
                    device_id=peer, device_id_type=pl.DeviceIdType.MESH).wait_recv()
        for cp in sent:
            cp.wait_send()
        for cp in local:
            cp.wait()

    out_shape = []
    for a in arrs:
        blk = a.shape if gather else a.shape[1:]
        out_shape.append(jax.ShapeDtypeStruct((N_DEV,) + tuple(blk), a.dtype))
    any_spec = pl.BlockSpec(memory_space=pl.ANY)
    return _pcall(
        body, name=name,
        out_shape=tuple(out_shape),
        in_specs=[any_spec] * n, out_specs=tuple([any_spec] * n),
        scratch_shapes=[pltpu.SemaphoreType.DMA((n * n_peer,)), pltpu.SemaphoreType.DMA((n * n_peer,)),
                        pltpu.SemaphoreType.DMA((n,))],
    )(*arrs)


def _in_proj_fwd(x, g, w_all, layer, name):
    s = x.shape[0]
    tm = min(512, s)

    def body(x_ref, g_ref, w_ref, u_ref, h_ref, hs):
        @pl.when(pl.program_id(1) == 0)
        def _():
            xv = x_ref[...]
            r = lax.rsqrt(jnp.mean(xv * xv, axis=-1, keepdims=True) + RMS_EPS)
            hv = (xv * r * g_ref[...]).astype(BF16)
            hs[...] = hv
            h_ref[...] = hv
        u_ref[...] = _dot(hs[...], w_ref[...])

    return _pcall(
        body, name=name, grid=(s // tm, N_DEV),
        in_specs=[pl.BlockSpec((tm, D_MODEL), lambda i, j: (i, 0)),
                  pl.BlockSpec((1, D_MODEL), lambda i, j: (0, 0)),
                  pl.BlockSpec((None, None, D_MODEL, COLS_PER_DEV), lambda i, j: (j, layer, 0, 0))],
        out_specs=(pl.BlockSpec((tm, COLS_PER_DEV), lambda i, j: (i, j)),
                   pl.BlockSpec((tm, D_MODEL), lambda i, j: (i, 0))),
        out_shape=(jax.ShapeDtypeStruct((s, N_IN), F32), jax.ShapeDtypeStruct((s, D_MODEL), BF16)),
        scratch_shapes=[pltpu.VMEM((tm, D_MODEL), BF16)],
        compiler_params=_params(("parallel", "arbitrary")),
    )(x, g, w_all)


def _pool_window(vs, t0, t, grp):
    ext = vs[pl.ds(t0, t + POOL_HALO), :]
    s2 = ext + pltpu.roll(ext, 1, 0)
    s4 = s2 + pltpu.roll(s2, 2, 0)
    s8 = s4 + pltpu.roll(s4, 4, 0)
    s16 = s8 + pltpu.roll(s8, 8, 0)
    sel = jnp.where(grp == 0, s2, jnp.where(grp == 1, s4, jnp.where(grp == 2, s8, s16)))
    return sel[POOL_HALO:, :], ext[POOL_HALO:, :]


def _pool_count(t0, t, grp):
    pos = t0 + lax.broadcasted_iota(jnp.int32, (t, 1), 0)
    return jnp.minimum(pos + 1, jnp.left_shift(2, grp)).astype(F32)


def _pool_fwd(u, pool_w, pool_scale, name):
    s = u.shape[0]
    t = min(256, s)

    def body(pv_ref, pg_ref, w_ref, sc_ref, y_ref, vs):
        grp = pl.program_id(0)
        vs[0:POOL_HALO, :] = jnp.zeros((POOL_HALO, LANES), F32)
        vs[POOL_HALO:, :] = pv_ref[...]
        wb = w_ref[...].astype(BF16)
        scale = sc_ref[...]

        def tile(i, carry):
            t0 = pl.multiple_of(i * t, t)
            win, v = _pool_window(vs, t0, t, grp)
            pooled = win / _pool_count(t0, t, grp) - v
            mixed = _dot(pooled.astype(BF16), wb)
            gate = pg_ref[pl.ds(t0, t), :]
            y_ref[pl.ds(t0, t), :] = (mixed * scale * (gate * _sigmoid(gate))).astype(BF16)
            return carry

        lax.fori_loop(0, s // t, tile, 0)

    return _pcall(
        body, name=name, grid=(4,),
        in_specs=[pl.BlockSpec((s, LANES), lambda g: (0, CB_POOL_V + g)),
                  pl.BlockSpec((s, LANES), lambda g: (0, CB_POOL_G + g)),
                  pl.BlockSpec((None, LANES, LANES), lambda g: (g, 0, 0)),
                  pl.BlockSpec((1, LANES), lambda g: (0, g))],
        out_specs=pl.BlockSpec((s, LANES), lambda g: (0, g)),
        out_shape=jax.ShapeDtypeStruct((s, WIDTH), BF16),
        scratch_shapes=[pltpu.VMEM((POOL_HALO + s, LANES), F32)],
        compiler_params=_params(("arbitrary",)),
    )(u, u, pool_w, pool_scale)


def _conv_taps(zs, t0, t):
    ext = zs[pl.ds(t0, t + CONV_HALO), :]
    z0 = ext[CONV_HALO:, :]
    z1 = pltpu.roll(ext, 1, 0)[CONV_HALO:, :]
    z2 = pltpu.roll(ext, 2, 0)[CONV_HALO:, :]
    return z0, z1, z2


def _conv_fwd(u, conv_w, conv_b, name):
    s = u.shape[0]
    t = min(256, s)

    def body(xc_ref, gb_ref, gc_ref, cg_ref, w_ref, b_ref, y_ref, zs):
        zs[0:CONV_HALO, :] = jnp.zeros((CONV_HALO, LANES), F32)
        zs[CONV_HALO:, :] = gc_ref[...] * xc_ref[...]
        w0, w1, w2 = w_ref[0:1, :], w_ref[1:2, :], w_ref[2:3, :]
        bias = b_ref[...]

        def tile(i, carry):
            t0 = pl.multiple_of(i * t, t)
            z0, z1, z2 = _conv_taps(zs, t0, t)
            conv = w0 * z2 + w1 * z1 + w2 * z0
            gate = cg_ref[pl.ds(t0, t), :]
            y = gb_ref[pl.ds(t0, t), :] * (conv + bias) * (gate * _sigmoid(gate))
            y_ref[pl.ds(t0, t), :] = y.astype(BF16)
            return carry

        lax.fori_loop(0, s // t, tile, 0)

    col = lambda base: pl.BlockSpec((s, LANES), lambda j: (0, base + j))
    return _pcall(
        body, name=name, grid=(4,),
        in_specs=[col(CB_CONV_X), col(CB_CONV_GB), col(CB_CONV_GC), col(CB_CONV_G),
                  pl.BlockSpec((3, LANES), lambda j: (0, j)),
                  pl.BlockSpec((1, LANES), lambda j: (0, j))],
        out_specs=pl.BlockSpec((s, LANES), lambda j: (0, j)),
        out_shape=jax.ShapeDtypeStruct((s, WIDTH), BF16),
        scratch_shapes=[pltpu.VMEM((CONV_HALO + s, LANES), F32)],
        compiler_params=_params(("arbitrary",)),
    )(u, u, u, u, conv_w, conv_b)


def _first_head_lanes(rows, width=LANES):
    lane = lax.broadcasted_iota(jnp.int32, (rows, width), 1)
    return jnp.bitwise_and(lane, LANES - 1) < HEAD_DIM


def _stack_heads(x, first):
    zero = jnp.zeros_like(x)
    return jnp.concatenate([jnp.where(first, x, zero), jnp.where(first, zero, x)], axis=0).astype(BF16)


def _causal_mask(tq, tk, copies):
    row = lax.broadcasted_iota(jnp.int32, (tq, tk), 0)
    col = lax.broadcasted_iota(jnp.int32, (tq, tk), 1)
    return jnp.concatenate([col < row] * copies, axis=0)


def _suffix_matrix(tk, inclusive):
    r = lax.broadcasted_iota(jnp.int32, (2 * tk, 2 * tk), 0)
    c = lax.broadcasted_iota(jnp.int32, (2 * tk, 2 * tk), 1)
    r = jnp.where(r >= tk, r - tk, r)
    tri = (r >= c) if inclusive else (r > c)
    return jnp.where(c >= tk, 1.0, jnp.where(tri, 1.0, 0.0)).astype(BF16)


def _suffix_sums(x, m):
    hi, lo = _split_bf16(x)
    return _dot(jnp.concatenate([hi, lo], axis=1), m)


def _sb_log_terms(z, mask, m_strict):
    ls = jnp.minimum(z, 0.0) - jnp.log(1.0 + jnp.exp(-jnp.abs(z)))
    lk = ls - z
    if mask is not None:
        lk = jnp.where(mask, lk, 0.0)
    return ls, _suffix_sums(lk, m_strict)


SB_PAIRS = 2


def _pair_lanes(a):
    return slice(a * LANES, (a + 1) * LANES)


def _sb_fwd(u, name):
    s = u.shape[0]
    tq = tk = min(128, s)
    pairs = SB_PAIRS
    width = pairs * LANES
    rows = 2 * pairs * tq

    def body(q_ref, k_ref, v_ref, g_ref, o_ref, y_ref, kbf, vst, z_s, ell_s, carry_s):
        i = pl.program_id(1)

        @pl.when(i == 0)
        def _():
            kbf[...] = k_ref[...].astype(BF16)
            first_s = _first_head_lanes(s, width)
            vf = v_ref[...]
            vst[0] = jnp.where(first_s, vf, 0.0).astype(BF16)
            vst[1] = jnp.where(first_s, 0.0, vf).astype(BF16)

        first = _first_head_lanes(tq)
        mask = _causal_mask(tq, tk, 2 * pairs)
        m_strict = _suffix_matrix(tk, False)
        qcat = jnp.concatenate([_stack_heads(q_ref[:, _pair_lanes(a)] * SB_SCALE, first) for a in range(pairs)],
                               axis=0)

        def scores(b):
            off = pl.multiple_of(jnp.maximum(b, 0) * tk, tk)
            z_s[...] = jnp.concatenate(
                [_dot_nt(qcat[a * 2 * tq:(a + 1) * 2 * tq], kbf[pl.ds(off, tk), _pair_lanes(a)])
                 for a in range(pairs)], axis=0)

        def log_weights(m):
            ls, cs = _sb_log_terms(z_s[...], m, m_strict)
            carry = carry_s[...]
            ell_s[...] = ls + cs[:, :tk] + carry
            carry_s[...] = carry + cs[:, tk:]

        def consume(b, accs, m):
            w = jnp.exp(ell_s[...])
            if m is not None:
                w = jnp.where(m, w, 0.0)
            wb = w.astype(BF16)
            off = pl.multiple_of(b * tk, tk)
            new = []
            for a in range(pairs):
                r0 = a * 2 * tq
                wcat = jnp.concatenate([wb[r0:r0 + tq], wb[r0 + tq:r0 + 2 * tq]], axis=1)
                vcat = jnp.concatenate([vst[0, pl.ds(off, tk), _pair_lanes(a)], vst[1, pl.ds(off, tk), _pair_lanes(a)]],
                                       axis=0)
                new.append(accs[a] + _dot(wcat, vcat))
            return tuple(new)

        carry_s[...] = jnp.zeros((rows, tk), F32)
        scores(i)
        log_weights(mask)
        scores(i - 1)
        accs = consume(i, tuple(jnp.zeros((tq, LANES), F32) for _ in range(pairs)), mask)
        log_weights(None)
        scores(i - 2)

        def step(n, accs):
            accs = consume(i - n, accs, None)
            log_weights(None)
            scores(i - n - 2)
            return accs

        accs = lax.fori_loop(1, i + 1, step, accs)
        o = jnp.concatenate(accs, axis=1)
        o_ref[...] = o
        gate = g_ref[...]
        y_ref[...] = (o * (gate * _sigmoid(gate))).astype(BF16)

    base = lambda cb: cb // pairs
    qblk = lambda cb: pl.BlockSpec((tq, width), lambda p, i: (i, base(cb) + p))
    full = lambda cb: pl.BlockSpec((s, width), lambda p, i: (0, base(cb) + p))
    state = pltpu.VMEM((rows, tk), F32)
    return _pcall(
        body, name=name, grid=(4 // pairs, s // tq),
        in_specs=[qblk(CB_SB_Q), full(CB_SB_K), full(CB_SB_V), qblk(CB_SB_G)],
        out_specs=(qblk(0), qblk(0)),
        out_shape=(jax.ShapeDtypeStruct((s, WIDTH), F32), jax.ShapeDtypeStruct((s, WIDTH), BF16)),
        scratch_shapes=[pltpu.VMEM((s, width), BF16), pltpu.VMEM((2, s, width), BF16), state, state, state],
        compiler_params=_params(("parallel", "arbitrary")),
    )(u, u, u, u)


def _merge_out_fwd(y_pool, y_conv, y_sb, u, wb_all, wo_all, x, g_post, layer, name):
    s = x.shape[0]
    tm = min(256, s)

    def body(yp, yc, ys, m0, m1, m2, wb_ref, wo_ref, x_ref, g_ref, out_ref, merged_ref, pre_ref):
        merged = jnp.zeros((tm, D_MODEL), F32)
        for n, (y_ref, m_ref) in enumerate(((yp, m0), (yc, m1), (ys, m2))):
            merged = merged + _sigmoid(m_ref[...]) * _dot(y_ref[...], wb_ref[n])
        mb = merged.astype(BF16)
        merged_ref[...] = mb
        pre = _dot(mb, wo_ref[...].reshape(D_MODEL, D_MODEL))
        pre_ref[...] = pre
        r = lax.rsqrt(jnp.mean(pre * pre, axis=-1, keepdims=True) + RMS_EPS)
        out_ref[...] = x_ref[...] + pre * r * g_ref[...]

    rows = lambda w: pl.BlockSpec((tm, w), lambda i: (i, 0))
    merge = lambda n: pl.BlockSpec((tm, D_MODEL), lambda i: (i, MERGE_BLOCK_1024 + n))
    return _pcall(
        body, name=name, grid=(s // tm,),
        in_specs=[rows(WIDTH), rows(WIDTH), rows(WIDTH), merge(0), merge(1), merge(2),
                  pl.BlockSpec((None, 3, WIDTH, D_MODEL), lambda i: (layer, 0, 0, 0)),
                  pl.BlockSpec((N_DEV, None, D_MODEL // N_DEV, D_MODEL), lambda i: (0, layer, 0, 0)),
                  rows(D_MODEL), pl.BlockSpec((1, D_MODEL), lambda i: (0, 0))],
        out_specs=(rows(D_MODEL), rows(D_MODEL), rows(D_MODEL)),
        out_shape=(jax.ShapeDtypeStruct((s, D_MODEL), F32), jax.ShapeDtypeStruct((s, D_MODEL), BF16),
                   jax.ShapeDtypeStruct((s, D_MODEL), F32)),
        compiler_params=_params(("arbitrary",)),
    )(y_pool, y_conv, y_sb, u, u, u, wb_all, wo_all, x, g_post)


def _loss_and_grad(y, target, name):
    s = y.shape[0]
    tm = min(512, s)

    def body(y_ref, t_ref, dy_ref, loss_ref, acc):
        i = pl.program_id(0)

        @pl.when(i == 0)
        def _():
            acc[...] = jnp.zeros_like(acc)
        err = y_ref[...] - t_ref[...]
        dy_ref[...] = err / D_MODEL
        acc[...] += jnp.sum(err * err, axis=0, keepdims=True)

        @pl.when(i == pl.num_programs(0) - 1)
        def _():
            total = jnp.sum(acc[...], axis=1, keepdims=True) * (0.5 / D_MODEL)
            loss_ref[...] = jnp.broadcast_to(total, (1, LANES))

    return _pcall(
        body, name=name, grid=(s // tm,),
        in_specs=[pl.BlockSpec((tm, D_MODEL), lambda i: (i, 0)), pl.BlockSpec((tm, D_MODEL), lambda i: (i, 0))],
        out_specs=(pl.BlockSpec((tm, D_MODEL), lambda i: (i, 0)), pl.BlockSpec((1, LANES), lambda i: (0, 0))),
        out_shape=(jax.ShapeDtypeStruct((s, D_MODEL), F32), jax.ShapeDtypeStruct((1, LANES), F32)),
        scratch_shapes=[pltpu.VMEM((1, D_MODEL), F32)],
        compiler_params=_params(("arbitrary",)),
    )(y, target)


def _out_proj_bwd(dy, pre, g_post, merged, wo_all, layer, name):
    s = dy.shape[0]
    tm = min(256, s)
    n_tiles = s // tm

    def body(dy_ref, pre_ref, g_ref, mg_ref, wo_ref, dm_ref, dwo_ref, dg_ref, acc):
        i = pl.program_id(0)

        @pl.when(i == 0)
        def _():
            acc[...] = jnp.zeros_like(acc)
            dg_ref[...] = jnp.zeros_like(dg_ref)
        dyv, pre_v = dy_ref[...], pre_ref[...]
        r = lax.rsqrt(jnp.mean(pre_v * pre_v, axis=-1, keepdims=True) + RMS_EPS)
        dg_ref[...] += jnp.sum(dyv * pre_v * r, axis=0, keepdims=True)
        a = dyv * g_ref[...]
        dpre = r * a - pre_v * (r * r * r) * jnp.mean(a * pre_v, axis=-1, keepdims=True)
        db = dpre.astype(BF16)
        acc[...] += _dot_tn(mg_ref[...], db)
        dm_ref[...] = _dot_nt(db, wo_ref[...].reshape(D_MODEL, D_MODEL))

        @pl.when(i == n_tiles - 1)
        def _():
            dwo_ref[...] = acc[...].astype(BF16)

    rows = lambda: pl.BlockSpec((tm, D_MODEL), lambda i: (i, 0))
    return _pcall(
        body, name=name, grid=(n_tiles,),
        in_specs=[rows(), rows(), pl.BlockSpec((1, D_MODEL), lambda i: (0, 0)), rows(),
                  pl.BlockSpec((N_DEV, None, D_MODEL // N_DEV, D_MODEL), lambda i: (0, layer, 0, 0))],
        out_specs=(rows(), pl.BlockSpec((D_MODEL, D_MODEL), lambda i: (0, 0)),
                   pl.BlockSpec((1, D_MODEL), lambda i: (0, 0))),
        out_shape=(jax.ShapeDtypeStruct((s, D_MODEL), F32), jax.ShapeDtypeStruct((D_MODEL, D_MODEL), BF16),
                   jax.ShapeDtypeStruct((1, D_MODEL), F32)),
        scratch_shapes=[pltpu.VMEM((D_MODEL, D_MODEL), F32)],
        compiler_params=_params(("arbitrary",)),
    )(dy, pre, g_post, merged, wo_all)


def _merge_bwd(dmerged, y_pool, y_conv, y_sb, u, wb_all, layer, name):
    s = dmerged.shape[0]
    tm = min(256, s)
    n_tiles = s // tm
    cols = D_MODEL // N_DEV

    def body(dm_ref, yp, yc, ys, m0, m1, m2, wb_ref, dum_ref, dyp, dyc, dys, dwb_ref, acc):
        i = pl.program_id(0)

        @pl.when(i == 0)
        def _():
            acc[...] = jnp.zeros_like(acc)
        dm = dm_ref[...]
        for n, (y_ref, m_ref, dy_ref) in enumerate(((yp, m0, dyp), (yc, m1, dyc), (ys, m2, dys))):
            yv = y_ref[...]
            wb = wb_ref[n]
            gate = _sigmoid(m_ref[...])
            proj = _dot(yv, wb)
            dum_ref[:, n * D_MODEL:(n + 1) * D_MODEL] = (dm * proj * gate * (1.0 - gate)).astype(BF16)
            dproj = (dm * gate).astype(BF16)
            acc[n] += _dot_tn(yv, dproj)
            dy_ref[...] = _dot_nt(dproj, wb)

        @pl.when(i == n_tiles - 1)
        def _():
            for j in range(N_DEV):
                for n in range(3):
                    dwb_ref[j, n] = acc[n, :, j * cols:(j + 1) * cols].astype(BF16)

    rows = lambda w: pl.BlockSpec((tm, w), lambda i: (i, 0))
    merge = lambda n: pl.BlockSpec((tm, D_MODEL), lambda i: (i, MERGE_BLOCK_1024 + n))
    return _pcall(
        body, name=name, grid=(n_tiles,),
        in_specs=[rows(D_MODEL), rows(WIDTH), rows(WIDTH), rows(WIDTH), merge(0), merge(1), merge(2),
                  pl.BlockSpec((None, 3, WIDTH, D_MODEL), lambda i: (layer, 0, 0, 0))],
        out_specs=(rows(3 * D_MODEL), rows(WIDTH), rows(WIDTH), rows(WIDTH),
                   pl.BlockSpec((N_DEV, 3, WIDTH, cols), lambda i: (0, 0, 0, 0))),
        out_shape=(jax.ShapeDtypeStruct((s, 3 * D_MODEL), BF16),
                   jax.ShapeDtypeStruct((s, WIDTH), F32), jax.ShapeDtypeStruct((s, WIDTH), F32),
                   jax.ShapeDtypeStruct((s, WIDTH), F32),
                   jax.ShapeDtypeStruct((N_DEV, 3, WIDTH, cols), BF16)),
        scratch_shapes=[pltpu.VMEM((3, WIDTH, D_MODEL), F32)],
        compiler_params=_params(("arbitrary",)),
    )(dmerged, y_pool, y_conv, y_sb, u, u, u, wb_all)


def _sb_bwd(u, o, dys, name):
    s = u.shape[0]
    tq = tk = min(128, s)
    pairs = SB_PAIRS
    width = pairs * LANES
    rows = 2 * pairs * tq
    pair_rows = lambda a: slice(a * 2 * tq, (a + 1) * 2 * tq)

    def body(q_ref, k_ref, v_ref, g_ref, o_ref, dys_ref, dq_ref, dk_ref, dv_ref, dg_ref,
             kbf, vbf, kst, z_s, ell_s, ls_s, cl_s, wb_s, g_s, bef_s, cg_s):
        i = pl.program_id(1)

        @pl.when(i == 0)
        def _():
            dk_ref[...] = jnp.zeros_like(dk_ref)
            dv_ref[...] = jnp.zeros_like(dv_ref)
            kf = k_ref[...]
            kbf[...] = kf.astype(BF16)
            vbf[...] = v_ref[...].astype(BF16)
            first_s = _first_head_lanes(s, width)
            kst[0] = jnp.where(first_s, kf, 0.0).astype(BF16)
            kst[1] = jnp.where(first_s, 0.0, kf).astype(BF16)

        first = _first_head_lanes(tq)
        mask = _causal_mask(tq, tk, 2 * pairs)
        m_strict = _suffix_matrix(tk, False)
        m_incl = _suffix_matrix(tk, True)

        gate = g_ref[...]
        sg = _sigmoid(gate)
        dy = dys_ref[...]
        ov = o_ref[...]
        dg_ref[...] = (dy * ov * (sg * (1.0 + gate * (1.0 - sg)))).astype(BF16)
        do = (dy * (gate * sg)).astype(BF16)
        prod = do.astype(F32) * ov
        row_sum = lambda v: jnp.broadcast_to(jnp.sum(v, axis=1, keepdims=True), (tq, tk))
        dsum, docat, qcat = [], [], []
        for a in range(pairs):
            pa = prod[:, _pair_lanes(a)]
            dsum += [row_sum(jnp.where(first, pa, 0.0)), row_sum(jnp.where(first, 0.0, pa))]
            docat.append(_stack_heads(do[:, _pair_lanes(a)], first))
            qcat.append(_stack_heads(q_ref[:, _pair_lanes(a)] * SB_SCALE, first))
        dsum = jnp.concatenate(dsum, axis=0)

        def block_start(b):
            return pl.multiple_of(jnp.maximum(b, 0) * tk, tk)

        def scores(b):
            off = block_start(b)
            z_s[...] = jnp.concatenate([_dot_nt(qcat[a], kbf[pl.ds(off, tk), _pair_lanes(a)]) for a in range(pairs)],
                                       axis=0)

        def log_weights(slot, m):
            ls, cs = _sb_log_terms(z_s[...], m, m_strict)
            cl = cl_s[...]
            ell_s[...] = ls + cs[:, :tk] + cl
            cl_s[...] = cl + cs[:, tk:]
            ls_s[slot] = ls

        def weights(b, m):
            off = block_start(b)
            dwt = jnp.concatenate([_dot_nt(docat[a], vbf[pl.ds(off, tk), _pair_lanes(a)]) for a in range(pairs)],
                                  axis=0)
            w = jnp.exp(ell_s[...])
            if m is not None:
                w = jnp.where(m, w, 0.0)
            wb = w.astype(BF16)
            g = dwt * wb.astype(F32)
            gs = _suffix_sums(g, m_incl)
            cg = cg_s[...]
            wb_s[...] = wb
            g_s[...] = g
            bef_s[...] = gs[:, :tk] + cg
            cg_s[...] = cg + gs[:, tk:]

        def grads(b, slot, dqs, m):
            g = g_s[...]
            before = dsum - bef_s[...]
            dz = g - jnp.exp(ls_s[slot]) * (g + before)
            if m is not None:
                dz = jnp.where(m, dz, 0.0)
            dzb = dz.astype(BF16)
            wb = wb_s[...]
            off = pl.multiple_of(b * tk, tk)
            new = []
            for a in range(pairs):
                r0 = a * 2 * tq
                kcat = jnp.concatenate([kst[0, pl.ds(off, tk), _pair_lanes(a)], kst[1, pl.ds(off, tk), _pair_lanes(a)]],
                                       axis=0)
                new.append(dqs[a] + _dot(jnp.concatenate([dzb[r0:r0 + tq], dzb[r0 + tq:r0 + 2 * tq]], axis=1), kcat))
                dk_ref[pl.ds(off, tk), _pair_lanes(a)] += _dot_tn(dzb[pair_rows(a)], qcat[a])
                dv_ref[pl.ds(off, tk), _pair_lanes(a)] += _dot_tn(wb[pair_rows(a)], docat[a])
            return tuple(new)

        zero = jnp.zeros((rows, tk), F32)
        cl_s[...] = zero
        cg_s[...] = zero
        scores(i)
        log_weights(0, mask)
        scores(i - 1)
        weights(i, mask)
        log_weights(1, None)
        scores(i - 2)
        dqs = grads(i, 0, tuple(jnp.zeros((tq, LANES), F32) for _ in range(pairs)), mask)
        weights(i - 1, None)
        log_weights(0, None)
        scores(i - 3)

        def step(n, dqs):
            slot = jnp.bitwise_and(n, 1)
            dqs = grads(i - n, slot, dqs, None)
            weights(i - n - 1, None)
            log_weights(slot, None)
            scores(i - n - 3)
            return dqs

        dqs = lax.fori_loop(1, i + 1, step, dqs)
        dq_ref[...] = (jnp.concatenate(dqs, axis=1) * SB_SCALE).astype(BF16)

    base = lambda cb: cb // pairs
    qblk = lambda cb: pl.BlockSpec((tq, width), lambda p, i: (i, base(cb) + p))
    full = lambda cb: pl.BlockSpec((s, width), lambda p, i: (0, base(cb) + p))
    state = pltpu.VMEM((rows, tk), F32)
    return _pcall(
        body, name=name, grid=(4 // pairs, s // tq),
        in_specs=[qblk(CB_SB_Q), full(CB_SB_K), full(CB_SB_V), qblk(CB_SB_G), qblk(0), qblk(0)],
        out_specs=(qblk(0), full(0), full(0), qblk(0)),
        out_shape=(jax.ShapeDtypeStruct((s, WIDTH), BF16), jax.ShapeDtypeStruct((s, WIDTH), F32),
                   jax.ShapeDtypeStruct((s, WIDTH), F32), jax.ShapeDtypeStruct((s, WIDTH), BF16)),
        scratch_shapes=[pltpu.VMEM((s, width), BF16), pltpu.VMEM((s, width), BF16), pltpu.VMEM((2, s, width), BF16),
                        state, state, pltpu.VMEM((2, rows, tk), F32), state, pltpu.VMEM((rows, tk), BF16),
                        state, state, state],
        compiler_params=_params(("parallel", "arbitrary")),
    )(u, u, u, u, o, dys)


def _conv_bwd(u, conv_w, conv_b, dyc, name):
    s = u.shape[0]
    t = min(256, s)
    n_tiles = s // t

    def body(xc_ref, gb_ref, gc_ref, cg_ref, w_ref, b_ref, dy_ref,
             dxc_ref, dgb_ref, dgc_ref, dcg_ref, dw_ref, db_ref, zs, ds):
        zs[0:CONV_HALO, :] = jnp.zeros((CONV_HALO, LANES), F32)
        zs[CONV_HALO:, :] = gc_ref[...] * xc_ref[...]
        ds[s:, :] = jnp.zeros((CONV_HALO, LANES), F32)
        w0, w1, w2 = w_ref[0:1, :], w_ref[1:2, :], w_ref[2:3, :]
        bias = b_ref[...]

        def first(i, sums):
            t0 = pl.multiple_of(i * t, t)
            z0, z1, z2 = _conv_taps(zs, t0, t)
            pre = w0 * z2 + w1 * z1 + w2 * z0 + bias
            gate = cg_ref[pl.ds(t0, t), :]
            sg = _sigmoid(gate)
            gb = gb_ref[pl.ds(t0, t), :]
            dy = dy_ref[pl.ds(t0, t), :]
            dcg_ref[pl.ds(t0, t), :] = (dy * gb * pre * (sg * (1.0 + gate * (1.0 - sg)))).astype(BF16)
            dgb_ref[pl.ds(t0, t), :] = (dy * pre * (gate * sg)).astype(BF16)
            dc = dy * gb * (gate * sg)
            ds[pl.ds(t0, t), :] = dc
            red = lambda v: jnp.sum(v, axis=0, keepdims=True)
            return (sums[0] + red(dc * z2), sums[1] + red(dc * z1), sums[2] + red(dc * z0), sums[3] + red(dc))

        zrow = jnp.zeros((1, LANES), F32)
        sw0, sw1, sw2, sb = lax.fori_loop(0, n_tiles, first, (zrow, zrow, zrow, zrow))
        dw_ref[0:1, :] = sw0
        dw_ref[1:2, :] = sw1
        dw_ref[2:3, :] = sw2
        db_ref[...] = sb

        def second(i, carry):
            t0 = pl.multiple_of(i * t, t)
            ext = ds[pl.ds(t0, t + CONV_HALO), :]
            n = t + CONV_HALO
            d0 = ext[:t, :]
            d1 = pltpu.roll(ext, n - 1, 0)[:t, :]
            d2 = pltpu.roll(ext, n - 2, 0)[:t, :]
            dz = w2 * d0 + w1 * d1 + w0 * d2
            dgc_ref[pl.ds(t0, t), :] = (dz * xc_ref[pl.ds(t0, t), :]).astype(BF16)
            dxc_ref[pl.ds(t0, t), :] = (dz * gc_ref[pl.ds(t0, t), :]).astype(BF16)
            return carry

        lax.fori_loop(0, n_tiles, second, 0)

    col = lambda base: pl.BlockSpec((s, LANES), lambda j: (0, base + j))
    dcol = jax.ShapeDtypeStruct((s, WIDTH), BF16)
    return _pcall(
        body, name=name, grid=(4,),
        in_specs=[col(CB_CONV_X), col(CB_CONV_GB), col(CB_CONV_GC), col(CB_CONV_G),
                  pl.BlockSpec((3, LANES), lambda j: (0, j)), pl.BlockSpec((1, LANES), lambda j: (0, j)), col(0)],
        out_specs=(col(0), col(0), col(0), col(0),
                   pl.BlockSpec((3, LANES), lambda j: (0, j)), pl.BlockSpec((1, LANES), lambda j: (0, j))),
        out_shape=(dcol, dcol, dcol, dcol,
                   jax.ShapeDtypeStruct((3, WIDTH), F32), jax.ShapeDtypeStruct((1, WIDTH), F32)),
        scratch_shapes=[pltpu.VMEM((CONV_HALO + s, LANES), F32), pltpu.VMEM((s + CONV_HALO, LANES), F32)],
        compiler_params=_params(("arbitrary",)),
    )(u, u, u, u, conv_w, conv_b, dyc)


def _pool_bwd(u, pool_w, pool_scale, dyp, name):
    s = u.shape[0]
    t = min(256, s)
    n_tiles = s // t

    def body(pv_ref, pg_ref, w_ref, sc_ref, dy_ref, dpv_ref, dpg_ref, dw_ref, dsc_ref, vs, es, dps):
        grp = pl.program_id(0)
        vs[0:POOL_HALO, :] = jnp.zeros((POOL_HALO, LANES), F32)
        vs[POOL_HALO:, :] = pv_ref[...]
        es[s:, :] = jnp.zeros((POOL_HALO, LANES), F32)
        wb = w_ref[...].astype(BF16)
        scale = sc_ref[...]

        def first(i, sums):
            dw, dsc = sums
            t0 = pl.multiple_of(i * t, t)
            win, v = _pool_window(vs, t0, t, grp)
            cnt = _pool_count(t0, t, grp)
            pb = (win / cnt - v).astype(BF16)
            mixed = _dot(pb, wb)
            gate = pg_ref[pl.ds(t0, t), :]
            sg = _sigmoid(gate)
            dy = dy_ref[pl.ds(t0, t), :]
            dpg_ref[pl.ds(t0, t), :] = (dy * (mixed * scale) * (sg * (1.0 + gate * (1.0 - sg)))).astype(BF16)
            dms = dy * (gate * sg)
            dsc = dsc + jnp.sum(dms * mixed, axis=0, keepdims=True)
            dmb = (dms * scale).astype(BF16)
            dw = dw + _dot_tn(pb, dmb)
            dpooled = _dot_nt(dmb, wb)
            dps[pl.ds(t0, t), :] = dpooled
            es[pl.ds(t0, t), :] = dpooled / cnt
            return dw, dsc

        dw, dsc = lax.fori_loop(0, n_tiles, first, (jnp.zeros((LANES, LANES), F32), jnp.zeros((1, LANES), F32)))
        dw_ref[...] = dw
        dsc_ref[...] = dsc

        def second(i, carry):
            t0 = pl.multiple_of(i * t, t)
            ext = es[pl.ds(t0, t + POOL_HALO), :]
            n = t + POOL_HALO
            f2 = ext + pltpu.roll(ext, n - 1, 0)
            f4 = f2 + pltpu.roll(f2, n - 2, 0)
            f8 = f4 + pltpu.roll(f4, n - 4, 0)
            f16 = f8 + pltpu.roll(f8, n - 8, 0)
            sel = jnp.where(grp == 0, f2, jnp.where(grp == 1, f4, jnp.where(grp == 2, f8, f16)))
            dpv_ref[pl.ds(t0, t), :] = (sel[:t, :] - dps[pl.ds(t0, t), :]).astype(BF16)
            return carry

        lax.fori_loop(0, n_tiles, second, 0)

    col = lambda base: pl.BlockSpec((s, LANES), lambda g: (0, base + g))
    dcol = jax.ShapeDtypeStruct((s, WIDTH), BF16)
    return _pcall(
        body, name=name, grid=(4,),
        in_specs=[col(CB_POOL_V), col(CB_POOL_G), pl.BlockSpec((None, LANES, LANES), lambda g: (g, 0, 0)),
                  pl.BlockSpec((1, LANES), lambda g: (0, g)), col(0)],
        out_specs=(col(0), col(0), pl.BlockSpec((None, LANES, LANES), lambda g: (g, 0, 0)),
                   pl.BlockSpec((1, LANES), lambda g: (0, g))),
        out_shape=(dcol, dcol, jax.ShapeDtypeStruct((4, LANES, LANES), F32), jax.ShapeDtypeStruct((1, WIDTH), F32)),
        scratch_shapes=[pltpu.VMEM((POOL_HALO + s, LANES), F32), pltpu.VMEM((s + POOL_HALO, LANES), F32),
                        pltpu.VMEM((s, LANES), F32)],
        compiler_params=_params(("arbitrary",)),
    )(u, u, pool_w, pool_scale, dyp)


def _in_proj_bwd_x(du, w_all, x, g_pre, dy, layer, name):
    s = x.shape[0]
    tm = min(512, s)

    def body(du_ref, w_ref, x_ref, g_ref, dy_ref, dx_ref, dg_ref, acc):
        i, k = pl.program_id(0), pl.program_id(1)

        @pl.when(k == 0)
        def _():
            acc[...] = jnp.zeros_like(acc)

        @pl.when((k == 0) & (i == 0))
        def _():
            dg_ref[...] = jnp.zeros_like(dg_ref)
        acc[...] += _dot_nt(du_ref[...], w_ref[...])

        @pl.when(k == N_DEV - 1)
        def _():
            dh, xv = acc[...], x_ref[...]
            r = lax.rsqrt(jnp.mean(xv * xv, axis=-1, keepdims=True) + RMS_EPS)
            dg_ref[...] += jnp.sum(dh * xv * r, axis=0, keepdims=True)
            a = dh * g_ref[...]
            dx_ref[...] = dy_ref[...] + r * a - xv * (r * r * r) * jnp.mean(a * xv, axis=-1, keepdims=True)

    rows = lambda: pl.BlockSpec((tm, D_MODEL), lambda i, k: (i, 0))
    vec = lambda: pl.BlockSpec((1, D_MODEL), lambda i, k: (0, 0))
    return _pcall(
        body, name=name, grid=(s // tm, N_DEV),
        in_specs=[pl.BlockSpec((tm, COLS_PER_DEV), lambda i, k: (i, k)),
                  pl.BlockSpec((None, None, D_MODEL, COLS_PER_DEV), lambda i, k: (k, layer, 0, 0)),
                  rows(), vec(), rows()],
        out_specs=(rows(), vec()),
        out_shape=(jax.ShapeDtypeStruct((s, D_MODEL), F32), jax.ShapeDtypeStruct((1, D_MODEL), F32)),
        scratch_shapes=[pltpu.VMEM((tm, D_MODEL), F32)],
        compiler_params=_params(("arbitrary", "arbitrary")),
    )(du, w_all, x, g_pre, dy)


def _in_proj_bwd_w(h, du, name):
    s = h.shape[0]
    tk = min(512, s)
    n_k = s // tk

    def body(h_ref, du_ref, out_ref, acc):
        k = pl.program_id(1)

        @pl.when(k == 0)
        def _():
            acc[...] = jnp.zeros_like(acc)
        acc[...] += _dot_tn(h_ref[...], du_ref[...])

        @pl.when(k == n_k - 1)
        def _():
            out_ref[...] = acc[...].astype(BF16)

    return _pcall(
        body, name=name, grid=(N_DEV, n_k),
        in_specs=[pl.BlockSpec((tk, D_MODEL), lambda j, k: (k, 0)),
                  pl.BlockSpec((tk, COLS_PER_DEV), lambda j, k: (k, j))],
        out_specs=pl.BlockSpec((None, D_MODEL, COLS_PER_DEV), lambda j, k: (j, 0, 0)),
        out_shape=jax.ShapeDtypeStruct((N_DEV, D_MODEL, COLS_PER_DEV), BF16),
        scratch_shapes=[pltpu.VMEM((D_MODEL, COLS_PER_DEV), F32)],
        compiler_params=_params(("parallel", "arbitrary")),
    )(h, du)


def _adamw_math(g, w, m, v):
    m_new = ADAM_B1 * m + (1.0 - ADAM_B1) * g
    v_new = ADAM_B2 * v + (1.0 - ADAM_B2) * (g * g)
    m_hat = m_new / (1.0 - ADAM_B1 ** ADAM_STEP)
    v_hat = v_new / (1.0 - ADAM_B2 ** ADAM_STEP)
    delta = -ADAM_LR * (m_hat / (jnp.sqrt(v_hat) + ADAM_EPS) + ADAM_WD * w)
    return delta, m_new, v_new


def _sum_partials(p_ref):
    total = p_ref[0].astype(F32)
    for d in range(1, N_DEV):
        total = total + p_ref[d].astype(F32)
    return total


def _adamw_layers(parts0, parts1, w, m, v, name):
    _, r, c = w.shape
    tr = min(128, r)
    n_r = r // tr

    def body(p0_ref, p1_ref, w_ref, m_ref, v_ref, g_ref, d_ref, mo_ref, vo_ref):
        layer = pl.program_id(0)

        @pl.when(layer == 0)
        def _():
            g_ref[...] = _sum_partials(p0_ref)

        @pl.when(layer == 1)
        def _():
            g_ref[...] = _sum_partials(p1_ref)
        d_ref[...], mo_ref[...], vo_ref[...] = _adamw_math(g_ref[...], w_ref[...], m_ref[...], v_ref[...])

    part = lambda which: pl.BlockSpec((N_DEV, tr, c), lambda l, i: (0, jnp.where(l == which, i, 0), 0))
    par = lambda: pl.BlockSpec((None, tr, c), lambda l, i: (l, i, 0))
    out = jax.ShapeDtypeStruct(w.shape, F32)
    return _pcall(
        body, name=name, grid=(2, n_r),
        in_specs=[part(0), part(1), par(), par(), par()],
        out_specs=(par(), par(), par(), par()),
        out_shape=(out, out, out, out),
        compiler_params=_params(("arbitrary", "arbitrary")),
    )(parts0, parts1, w, m, v)


def _adamw_small(parts, w, m, v, name):
    def body(p_ref, w_ref, m_ref, v_ref, g_ref, d_ref, mo_ref, vo_ref):
        g = _sum_partials(p_ref)
        g_ref[...] = g
        d_ref[...], mo_ref[...], vo_ref[...] = _adamw_math(g, w_ref[...], m_ref[...], v_ref[...])

    out = jax.ShapeDtypeStruct(w.shape, F32)
    return _pcall(body, name=name, out_shape=(out, out, out, out), compiler_params=_params())(parts, w, m, v)


def _adamw_plain(g, w, m, v, name):
    def body(g_ref, w_ref, m_ref, v_ref, d_ref, mo_ref, vo_ref):
        d_ref[...], mo_ref[...], vo_ref[...] = _adamw_math(g_ref[...], w_ref[...], m_ref[...], v_ref[...])

    out = jax.ShapeDtypeStruct(w.shape, F32)
    return _pcall(body, name=name, out_shape=(out, out, out), compiler_params=_params())(g, w, m, v)


def _rows128(a):
    return a.reshape(-1, LANES)


SMALL_NAMES = ("pre_norm_g", "pool_w", "pool_scale", "conv_w", "conv_b", "post_norm_g")


def kernel(x, pre_norm_g, w_in, pool_w, pool_scale, conv_w, conv_b, w_branch, w_out, post_norm_g, loss_target, m_pre_norm_g, m_w_in, m_pool_w, m_pool_scale, m_conv_w, m_conv_b, m_w_branch, m_w_out, m_post_norm_g, v_pre_norm_g, v_w_in, v_pool_w, v_pool_scale, v_conv_w, v_conv_b, v_w_branch, v_w_out, v_post_norm_g):
    s = x.shape[1]
    me = 4 * lax.axis_index("x") + 2 * lax.axis_index("y") + lax.axis_index("c")
    x0 = x[0]
    target = loss_target[0]
    conv_cols = conv_w.shape[-1]

    conv_w_pad = jnp.pad(conv_w.reshape(2 * 3, conv_cols), ((0, 2), (0, LANES - conv_cols)))
    w_in_all, wb_g, wo_all, cw_g = _exchange(
        [w_in.astype(BF16), w_branch.astype(BF16), w_out.astype(BF16), conv_w_pad], True, "gather_weights")
    wb_all = wb_g.transpose(1, 2, 3, 0, 4).reshape(2, 3, WIDTH, D_MODEL)
    conv_w_full = cw_g[:, :6, :conv_cols].reshape(N_DEV, 2, 3, conv_cols).transpose(1, 2, 0, 3).reshape(2, 3, WIDTH)

    saved = []
    xin = x0
    for l in range(2):
        u, h = _in_proj_fwd(xin, pre_norm_g[l:l + 1], w_in_all, l, f"in_proj_fwd_{l}")
        y_pool = _pool_fwd(u, pool_w[l], pool_scale[l:l + 1], f"pool_fwd_{l}")
        y_conv = _conv_fwd(u, conv_w_full[l], conv_b[l:l + 1], f"conv_fwd_{l}")
        o_sb, y_sb = _sb_fwd(u, f"sb_fwd_{l}")
        xout, merged, pre = _merge_out_fwd(y_pool, y_conv, y_sb, u, wb_all, wo_all, xin, post_norm_g[l:l + 1], l,
                                           f"merge_out_fwd_{l}")
        saved.append((xin, u, h, y_pool, y_conv, y_sb, o_sb, merged, pre))
        xin = xout

    dy, loss_row = _loss_and_grad(xin, target, "loss")

    small = [None, None]
    recv = [None, None]
    for l in (1, 0):
        xl, u, h, y_pool, y_conv, y_sb, o_sb, merged, pre = saved[l]
        dmerged, dwo, dg_post = _out_proj_bwd(dy, pre, post_norm_g[l:l + 1], merged, wo_all, l, f"out_proj_bwd_{l}")
        du_merge, dyp, dyc, dys, dwb = _merge_bwd(dmerged, y_pool, y_conv, y_sb, u, wb_all, l, f"merge_bwd_{l}")
        dq, dk, dv, dsg = _sb_bwd(u, o_sb, dys, f"sb_bwd_{l}")
        dxc, dgb, dgc, dcg, dcw, dcb = _conv_bwd(u, conv_w_full[l], conv_b[l:l + 1], dyc, f"conv_bwd_{l}")
        dpv, dpg, dpw, dps = _pool_bwd(u, pool_w[l], pool_scale[l:l + 1], dyp, f"pool_bwd_{l}")
        du = jnp.concatenate([dpv, dpg, dxc, dgb, dgc, dcg, dq, dk.astype(BF16), dv.astype(BF16), dsg, du_merge],
                             axis=1)
        dx, dg_pre = _in_proj_bwd_x(du, w_in_all, xl, pre_norm_g[l:l + 1], dy, l, f"in_proj_bwd_x_{l}")
        dwi = _in_proj_bwd_w(h, du, f"in_proj_bwd_w_{l}")
        recv[l] = _exchange([dwi, dwb.reshape(N_DEV, 3 * WIDTH, D_MODEL // N_DEV),
                             dwo.reshape(N_DEV, D_MODEL // N_DEV, D_MODEL)], False, f"scatter_grads_{l}")
        small[l] = dict(pre_norm_g=dg_pre, pool_w=dpw, pool_scale=dps, conv_w=dcw, conv_b=dcb, post_norm_g=dg_post)
        dy = dx
    grad_x = dy[None]

    packed = jnp.concatenate(
        [_rows128(jnp.stack([small[0][n], small[1][n]])) for n in SMALL_NAMES]
        + [jnp.pad(loss_row, ((0, 7), (0, 0)))], axis=0)
    (packed_all,) = _exchange([packed], True, "gather_small")
    sizes = dict(pre_norm_g=16, pool_w=1024, pool_scale=8, conv_w=24, conv_b=8, post_norm_g=16)
    n_rows = sum(sizes.values())
    loss = jnp.sum(packed_all[:, n_rows, 0])

    given = dict(pre_norm_g=(pre_norm_g, m_pre_norm_g, v_pre_norm_g), pool_w=(pool_w, m_pool_w, v_pool_w),
                 pool_scale=(pool_scale, m_pool_scale, v_pool_scale), conv_b=(conv_b, m_conv_b, v_conv_b),
                 post_norm_g=(post_norm_g, m_post_norm_g, v_post_norm_g))
    zeros_cw = jnp.zeros((sizes["conv_w"], LANES), F32)
    pack3 = [jnp.concatenate([zeros_cw if n == "conv_w" else _rows128(given[n][k]) for n in SMALL_NAMES], axis=0)
             for k in range(3)]
    sg, sd, sm, sv = _adamw_small(packed_all[:, :n_rows], pack3[0], pack3[1], pack3[2], "adamw_small")

    def unpack(buf, name, shape):
        start = 0
        for n in SMALL_NAMES:
            if n == name:
                return buf[start:start + sizes[n]].reshape(shape)
            start += sizes[n]

    out = {}
    for n in ("pre_norm_g", "pool_w", "pool_scale", "conv_b", "post_norm_g"):
        shape = given[n][0].shape
        out[n] = tuple(unpack(b, n, shape) for b in (sg, sd, sm, sv))
    g_cw = lax.dynamic_slice_in_dim(unpack(sg, "conv_w", (2, 3, WIDTH)), me * conv_cols, conv_cols, axis=2)
    cw2 = lambda a: a.reshape(6, conv_cols)
    d_cw, m_cw, v_cw = _adamw_plain(cw2(g_cw), cw2(conv_w), cw2(m_conv_w), cw2(v_conv_w), "adamw_conv_w")
    out["conv_w"] = (g_cw,) + tuple(a.reshape(2, 3, conv_cols) for a in (d_cw, m_cw, v_cw))

    out["w_in"] = _adamw_layers(recv[0][0], recv[1][0], w_in, m_w_in, v_w_in, "adamw_w_in")
    cols = D_MODEL // N_DEV
    wb3 = lambda a: a.reshape(2, 3 * WIDTH, cols)
    out["w_branch"] = tuple(a.reshape(2, 3, WIDTH, cols) for a in _adamw_layers(
        recv[0][1], recv[1][1], wb3(w_branch), wb3(m_w_branch), wb3(v_w_branch), "adamw_w_branch"))
    out["w_out"] = _adamw_layers(recv[0][2], recv[1][2], w_out, m_w_out, v_w_out, "adamw_w_out")

    order = ("pre_norm_g", "w_in", "pool_w", "pool_scale", "conv_w", "conv_b", "w_branch", "w_out", "post_norm_g")
    return (loss, grad_x) + tuple(out[n][k] for k in range(4) for n in order)
```

```python
import functools

import jax
import jax.numpy as jnp
from jax import lax
from jax.experimental import pallas as pl
from jax.experimental.pallas import tpu as pltpu

F32 = jnp.float32
BF16 = jnp.bfloat16

N_DEV = 8
D_MODEL = 1024
WIDTH = 512
N_IN = 8192
COLS_PER_DEV = N_IN // N_DEV
HEAD_DIM = 64
LANES = 128
SB_SCALE = HEAD_DIM ** -0.5
RMS_EPS = 1e-6
POOL_HALO = 16
CONV_HALO = 8
ADAM_LR, ADAM_B1, ADAM_B2, ADAM_EPS, ADAM_WD, ADAM_STEP = 0.001, 0.9, 0.999, 1e-08, 0.01, 10
VMEM_LIMIT = 56 * 1024 * 1024

CB_POOL_V, CB_POOL_G = 0, 4
CB_CONV_X, CB_CONV_GB, CB_CONV_GC, CB_CONV_G = 8, 12, 16, 20
CB_SB_Q, CB_SB_K, CB_SB_V, CB_SB_G = 24, 28, 32, 36
MERGE_BLOCK_1024 = 5


def _pcall(body, **kw):
    return pl.pallas_call(body, **kw)


def _params(sem=None):
    if sem is None:
        return pltpu.CompilerParams(vmem_limit_bytes=VMEM_LIMIT)
    return pltpu.CompilerParams(dimension_semantics=sem, vmem_limit_bytes=VMEM_LIMIT)


def _sigmoid(x):
    return 1.0 / (1.0 + jnp.exp(-x))


def _dot(a, b):
    return jnp.dot(a, b, preferred_element_type=F32)


def _dot_nt(a, b):
    return lax.dot_general(a, b, (((1,), (1,)), ((), ())), preferred_element_type=F32)


def _dot_tn(a, b):
    return lax.dot_general(a, b, (((0,), (0,)), ((), ())), preferred_element_type=F32)


def _split_bf16(x):
    hi = x.astype(BF16)
    lo = (x - hi.astype(F32)).astype(BF16)
    return hi, lo


N_PEER = N_DEV - 1
ANY_SPEC = pl.BlockSpec(memory_space=pl.ANY)


def _exchange_copies(ins, outs, send_sems, recv_sems, local_sems, gather):
    n = len(ins)
    x, y, c = lax.axis_index("x"), lax.axis_index("y"), lax.axis_index("c")
    me = 4 * x + 2 * y + c
    flip = lambda v, bit: 1 - v if bit else v
    local, sends, recvs = [], [], []
    for a in range(n):
        src = ins[a] if gather else ins[a].at[me]
        local.append(pltpu.make_async_copy(src, outs[a].at[me], local_sems.at[a]))
    for k in range(N_PEER):
        px, py, pc = flip(x, ((k + 1) >> 2) & 1), flip(y, ((k + 1) >> 1) & 1), flip(c, (k + 1) & 1)
        peer_id = 4 * px + 2 * py + pc
        for a in range(n):
            src = ins[a] if gather else ins[a].at[peer_id]
            common = dict(src_ref=src, send_sem=send_sems.at[a * N_PEER + k], recv_sem=recv_sems.at[a * N_PEER + k],
                          device_id=(px, py, pc), device_id_type=pl.DeviceIdType.MESH)
            sends.append(pltpu.make_async_remote_copy(dst_ref=outs[a].at[me], **common))
            recvs.append(pltpu.make_async_remote_copy(dst_ref=outs[a].at[peer_id], **common))
    return local, sends, recvs


def _exchange_start(ins, outs, sems, gather):
    local, sends, _ = _exchange_copies(ins, outs, *sems, gather)
    for cp in local + sends:
        cp.start()


def _exchange_wait(ins, outs, sems, gather):
    local, sends, recvs = _exchange_copies(ins, outs, *sems, gather)
    for cp in recvs:
        cp.wait_recv()
    for cp in sends:
        cp.wait_send()
    for cp in local:
        cp.wait()


def _exchange_out_shapes(arrs, gather):
    return [jax.ShapeDtypeStruct((N_DEV,) + tuple(a.shape if gather else a.shape[1:]), a.dtype) for a in arrs]


def _exchange_sems(n):
    return [pltpu.SemaphoreType.DMA((n * N_PEER,)), pltpu.SemaphoreType.DMA((n * N_PEER,)),
            pltpu.SemaphoreType.DMA((n,))]


def _exchange(arrs, gather, name):
    n = len(arrs)

    def body(*refs):
        ins, outs, sems = refs[:n], refs[n:2 * n], refs[2 * n:]
        _exchange_start(ins, outs, sems, gather)
        _exchange_wait(ins, outs, sems, gather)

    return _pcall(
        body, name=name,
        out_shape=tuple(_exchange_out_shapes(arrs, gather)),
        in_specs=[ANY_SPEC] * n, out_specs=tuple([ANY_SPEC] * n),
        scratch_shapes=_exchange_sems(n),
    )(*arrs)


def _in_proj_fwd(x, g, w_all, name):
    s = x.shape[0]
    tm = min(512, s)

    def body(x_ref, g_ref, w_ref, u_ref, h_ref, hs):
        @pl.when(pl.program_id(1) == 0)
        def _():
            xv = x_ref[...]
            r = lax.rsqrt(jnp.mean(xv * xv, axis=-1, keepdims=True) + RMS_EPS)
            hv = (xv * r * g_ref[...]).astype(BF16)
            hs[...] = hv
            h_ref[...] = hv
        u_ref[...] = _dot(hs[...], w_ref[...])

    return _pcall(
        body, name=name, grid=(s // tm, N_DEV),
        in_specs=[pl.BlockSpec((tm, D_MODEL), lambda i, j: (i, 0)),
                  pl.BlockSpec((1, D_MODEL), lambda i, j: (0, 0)),
                  pl.BlockSpec((None, D_MODEL, COLS_PER_DEV), lambda i, j: (j, 0, 0))],
        out_specs=(pl.BlockSpec((tm, COLS_PER_DEV), lambda i, j: (i, j)),
                   pl.BlockSpec((tm, D_MODEL), lambda i, j: (i, 0))),
        out_shape=(jax.ShapeDtypeStruct((s, N_IN), F32), jax.ShapeDtypeStruct((s, D_MODEL), BF16)),
        scratch_shapes=[pltpu.VMEM((tm, D_MODEL), BF16)],
        compiler_params=_params(("parallel", "arbitrary")),
    )(x, g, w_all)


def _pool_window(vs, t0, t, grp):
    ext = vs[pl.ds(t0, t + POOL_HALO), :]
    s2 = ext + pltpu.roll(ext, 1, 0)
    s4 = s2 + pltpu.roll(s2, 2, 0)
    s8 = s4 + pltpu.roll(s4, 4, 0)
    s16 = s8 + pltpu.roll(s8, 8, 0)
    sel = jnp.where(grp == 0, s2, jnp.where(grp == 1, s4, jnp.where(grp == 2, s8, s16)))
    return sel[POOL_HALO:, :], ext[POOL_HALO:, :]


def _pool_count(t0, t, grp):
    pos = t0 + lax.broadcasted_iota(jnp.int32, (t, 1), 0)
    return jnp.minimum(pos + 1, jnp.left_shift(2, grp)).astype(F32)


def _pool_fwd(u, pool_w, pool_scale, name):
    s = u.shape[0]
    t = min(256, s)

    def body(pv_ref, pg_ref, w_ref, sc_ref, y_ref, vs):
        grp = pl.program_id(0)
        vs[0:POOL_HALO, :] = jnp.zeros((POOL_HALO, LANES), F32)
        vs[POOL_HALO:, :] = pv_ref[...]
        wb = w_ref[...].astype(BF16)
        scale = sc_ref[...]

        def tile(i, carry):
            t0 = pl.multiple_of(i * t, t)
            win, v = _pool_window(vs, t0, t, grp)
            pooled = win / _pool_count(t0, t, grp) - v
            mixed = _dot(pooled.astype(BF16), wb)
            gate = pg_ref[pl.ds(t0, t), :]
            y_ref[pl.ds(t0, t), :] = (mixed * scale * (gate * _sigmoid(gate))).astype(BF16)
            return carry

        lax.fori_loop(0, s // t, tile, 0)

    return _pcall(
        body, name=name, grid=(4,),
        in_specs=[pl.BlockSpec((s, LANES), lambda g: (0, CB_POOL_V + g)),
                  pl.BlockSpec((s, LANES), lambda g: (0, CB_POOL_G + g)),
                  pl.BlockSpec((None, LANES, LANES), lambda g: (g, 0, 0)),
                  pl.BlockSpec((1, LANES), lambda g: (0, g))],
        out_specs=pl.BlockSpec((s, LANES), lambda g: (0, g)),
        out_shape=jax.ShapeDtypeStruct((s, WIDTH), BF16),
        scratch_shapes=[pltpu.VMEM((POOL_HALO + s, LANES), F32)],
        compiler_params=_params(("arbitrary",)),
    )(u, u, pool_w, pool_scale)


def _conv_taps(zs, t0, t):
    ext = zs[pl.ds(t0, t + CONV_HALO), :]
    z0 = ext[CONV_HALO:, :]
    z1 = pltpu.roll(ext, 1, 0)[CONV_HALO:, :]
    z2 = pltpu.roll(ext, 2, 0)[CONV_HALO:, :]
    return z0, z1, z2


def _conv_fwd(u, conv_w, conv_b, name):
    s = u.shape[0]
    t = min(256, s)

    def body(xc_ref, gb_ref, gc_ref, cg_ref, w_ref, b_ref, y_ref, zs):
        zs[0:CONV_HALO, :] = jnp.zeros((CONV_HALO, LANES), F32)
        zs[CONV_HALO:, :] = gc_ref[...] * xc_ref[...]
        w0, w1, w2 = w_ref[0:1, :], w_ref[1:2, :], w_ref[2:3, :]
        bias = b_ref[...]

        def tile(i, carry):
            t0 = pl.multiple_of(i * t, t)
            z0, z1, z2 = _conv_taps(zs, t0, t)
            conv = w0 * z2 + w1 * z1 + w2 * z0
            gate = cg_ref[pl.ds(t0, t), :]
            y = gb_ref[pl.ds(t0, t), :] * (conv + bias) * (gate * _sigmoid(gate))
            y_ref[pl.ds(t0, t), :] = y.astype(BF16)
            return carry

        lax.fori_loop(0, s // t, tile, 0)

    col = lambda base: pl.BlockSpec((s, LANES), lambda j: (0, base + j))
    return _pcall(
        body, name=name, grid=(4,),
        in_specs=[col(CB_CONV_X), col(CB_CONV_GB), col(CB_CONV_GC), col(CB_CONV_G),
                  pl.BlockSpec((3, LANES), lambda j: (0, j)),
                  pl.BlockSpec((1, LANES), lambda j: (0, j))],
        out_specs=pl.BlockSpec((s, LANES), lambda j: (0, j)),
        out_shape=jax.ShapeDtypeStruct((s, WIDTH), BF16),
        scratch_shapes=[pltpu.VMEM((CONV_HALO + s, LANES), F32)],
        compiler_params=_params(("arbitrary",)),
    )(u, u, u, u, conv_w, conv_b)


def _first_head_lanes(rows, width=LANES):
    lane = lax.broadcasted_iota(jnp.int32, (rows, width), 1)
    return jnp.bitwise_and(lane, LANES - 1) < HEAD_DIM


def _stack_heads(x, first):
    zero = jnp.zeros_like(x)
    return jnp.concatenate([jnp.where(first, x, zero), jnp.where(first, zero, x)], axis=0).astype(BF16)


def _causal_mask(tq, tk, copies):
    row = lax.broadcasted_iota(jnp.int32, (tq, tk), 0)
    col = lax.broadcasted_iota(jnp.int32, (tq, tk), 1)
    return jnp.concatenate([col < row] * copies, axis=0)


def _suffix_matrix(tk, inclusive):
    r = lax.broadcasted_iota(jnp.int32, (2 * tk, 2 * tk), 0)
    c = lax.broadcasted_iota(jnp.int32, (2 * tk, 2 * tk), 1)
    r = jnp.where(r >= tk, r - tk, r)
    tri = (r >= c) if inclusive else (r > c)
    return jnp.where(c >= tk, 1.0, jnp.where(tri, 1.0, 0.0)).astype(BF16)


def _suffix_sums(x, m):
    hi, lo = _split_bf16(x)
    return _dot(jnp.concatenate([hi, lo], axis=1), m)


def _sb_log_terms(z, mask, m_strict):
    ls = jnp.minimum(z, 0.0) - jnp.log(1.0 + jnp.exp(-jnp.abs(z)))
    lk = ls - z
    if mask is not None:
        lk = jnp.where(mask, lk, 0.0)
    return ls, _suffix_sums(lk, m_strict)


SB_PAIRS = 2


def _pair_lanes(a):
    return slice(a * LANES, (a + 1) * LANES)


def _sb_fwd(u, name, xchg=None):
    s = u.shape[0]
    tq = tk = min(128, s)
    pairs = SB_PAIRS
    width = pairs * LANES
    rows = 2 * pairs * tq
    x_arrs, x_gather = xchg if xchg else ((), True)
    n_x = len(x_arrs)
    grid = (4 // pairs, s // tq)

    def body(*refs):
        q_ref, k_ref, v_ref, g_ref = refs[:4]
        x_in, refs = refs[4:4 + n_x], refs[4 + n_x:]
        o_ref, y_ref = refs[:2]
        x_out, refs = refs[2:2 + n_x], refs[2 + n_x:]
        kbf, vst, z_s, ell_s, carry_s = refs[:5]
        x_sems = refs[5:]
        i = pl.program_id(1)
        if n_x:
            @pl.when((pl.program_id(0) == 0) & (i == 0))
            def _():
                _exchange_start(x_in, x_out, x_sems, x_gather)

        @pl.when(i == 0)
        def _():
            kbf[...] = k_ref[...].astype(BF16)
            first_s = _first_head_lanes(s, width)
            vf = v_ref[...]
            vst[0] = jnp.where(first_s, vf, 0.0).astype(BF16)
            vst[1] = jnp.where(first_s, 0.0, vf).astype(BF16)

        first = _first_head_lanes(tq)
        mask = _causal_mask(tq, tk, 2 * pairs)
        m_strict = _suffix_matrix(tk, False)
        qcat = jnp.concatenate([_stack_heads(q_ref[:, _pair_lanes(a)] * SB_SCALE, first) for a in range(pairs)],
                               axis=0)

        def scores(b):
            off = pl.multiple_of(jnp.maximum(b, 0) * tk, tk)
            z_s[...] = jnp.concatenate(
                [_dot_nt(qcat[a * 2 * tq:(a + 1) * 2 * tq], kbf[pl.ds(off, tk), _pair_lanes(a)])
                 for a in range(pairs)], axis=0)

        def log_weights(m):
            ls, cs = _sb_log_terms(z_s[...], m, m_strict)
            carry = carry_s[...]
            ell_s[...] = ls + cs[:, :tk] + carry
            carry_s[...] = carry + cs[:, tk:]

        def consume(b, accs, m):
            w = jnp.exp(ell_s[...])
            if m is not None:
                w = jnp.where(m, w, 0.0)
            wb = w.astype(BF16)
            off = pl.multiple_of(b * tk, tk)
            new = []
            for a in range(pairs):
                r0 = a * 2 * tq
                wcat = jnp.concatenate([wb[r0:r0 + tq], wb[r0 + tq:r0 + 2 * tq]], axis=1)
                vcat = jnp.concatenate([vst[0, pl.ds(off, tk), _pair_lanes(a)], vst[1, pl.ds(off, tk), _pair_lanes(a)]],
                                       axis=0)
                new.append(accs[a] + _dot(wcat, vcat))
            return tuple(new)

        carry_s[...] = jnp.zeros((rows, tk), F32)
        scores(i)
        log_weights(mask)
        scores(i - 1)
        accs = consume(i, tuple(jnp.zeros((tq, LANES), F32) for _ in range(pairs)), mask)
        log_weights(None)
        scores(i - 2)

        def step(n, accs):
            accs = consume(i - n, accs, None)
            log_weights(None)
            scores(i - n - 2)
            return accs

        accs = lax.fori_loop(1, i + 1, step, accs)
        o = jnp.concatenate(accs, axis=1)
        o_ref[...] = o
        gate = g_ref[...]
        y_ref[...] = (o * (gate * _sigmoid(gate))).astype(BF16)
        if n_x:
            @pl.when((pl.program_id(0) == grid[0] - 1) & (i == grid[1] - 1))
            def _():
                _exchange_wait(x_in, x_out, x_sems, x_gather)

    base = lambda cb: cb // pairs
    qblk = lambda cb: pl.BlockSpec((tq, width), lambda p, i: (i, base(cb) + p))
    full = lambda cb: pl.BlockSpec((s, width), lambda p, i: (0, base(cb) + p))
    state = pltpu.VMEM((rows, tk), F32)
    return _pcall(
        body, name=name, grid=grid,
        in_specs=[qblk(CB_SB_Q), full(CB_SB_K), full(CB_SB_V), qblk(CB_SB_G)] + [ANY_SPEC] * n_x,
        out_specs=(qblk(0), qblk(0)) + (ANY_SPEC,) * n_x,
        out_shape=(jax.ShapeDtypeStruct((s, WIDTH), F32), jax.ShapeDtypeStruct((s, WIDTH), BF16))
        + tuple(_exchange_out_shapes(x_arrs, x_gather)),
        scratch_shapes=[pltpu.VMEM((s, width), BF16), pltpu.VMEM((2, s, width), BF16), state, state, state]
        + (_exchange_sems(n_x) if n_x else []),
        compiler_params=_params(("arbitrary", "arbitrary")),
    )(u, u, u, u, *x_arrs)


def _merge_out_fwd(y_pool, y_conv, y_sb, u, wb_all, wo_all, x, g_post, layer, name):
    s = x.shape[0]
    tm = min(256, s)

    def body(yp, yc, ys, m0, m1, m2, wb_ref, wo_ref, x_ref, g_ref, out_ref, merged_ref, pre_ref):
        merged = jnp.zeros((tm, D_MODEL), F32)
        for n, (y_ref, m_ref) in enumerate(((yp, m0), (yc, m1), (ys, m2))):
            merged = merged + _sigmoid(m_ref[...]) * _dot(y_ref[...], wb_ref[n])
        mb = merged.astype(BF16)
        merged_ref[...] = mb
        pre = _dot(mb, wo_ref[...].reshape(D_MODEL, D_MODEL))
        pre_ref[...] = pre
        r = lax.rsqrt(jnp.mean(pre * pre, axis=-1, keepdims=True) + RMS_EPS)
        out_ref[...] = x_ref[...] + pre * r * g_ref[...]

    rows = lambda w: pl.BlockSpec((tm, w), lambda i: (i, 0))
    merge = lambda n: pl.BlockSpec((tm, D_MODEL), lambda i: (i, MERGE_BLOCK_1024 + n))
    return _pcall(
        body, name=name, grid=(s // tm,),
        in_specs=[rows(WIDTH), rows(WIDTH), rows(WIDTH), merge(0), merge(1), merge(2),
                  pl.BlockSpec((None, 3, WIDTH, D_MODEL), lambda i: (layer, 0, 0, 0)),
                  pl.BlockSpec((N_DEV, None, D_MODEL // N_DEV, D_MODEL), lambda i: (0, layer, 0, 0)),
                  rows(D_MODEL), pl.BlockSpec((1, D_MODEL), lambda i: (0, 0))],
        out_specs=(rows(D_MODEL), rows(D_MODEL), rows(D_MODEL)),
        out_shape=(jax.ShapeDtypeStruct((s, D_MODEL), F32), jax.ShapeDtypeStruct((s, D_MODEL), BF16),
                   jax.ShapeDtypeStruct((s, D_MODEL), F32)),
        compiler_params=_params(("arbitrary",)),
    )(y_pool, y_conv, y_sb, u, u, u, wb_all, wo_all, x, g_post)


def _loss_and_grad(y, target, name):
    s = y.shape[0]
    tm = min(512, s)

    def body(y_ref, t_ref, dy_ref, loss_ref, acc):
        i = pl.program_id(0)

        @pl.when(i == 0)
        def _():
            acc[...] = jnp.zeros_like(acc)
        err = y_ref[...] - t_ref[...]
        dy_ref[...] = err / D_MODEL
        acc[...] += jnp.sum(err * err, axis=0, keepdims=True)

        @pl.when(i == pl.num_programs(0) - 1)
        def _():
            total = jnp.sum(acc[...], axis=1, keepdims=True) * (0.5 / D_MODEL)
            loss_ref[...] = jnp.broadcast_to(total, (1, LANES))

    return _pcall(
        body, name=name, grid=(s // tm,),
        in_specs=[pl.BlockSpec((tm, D_MODEL), lambda i: (i, 0)), pl.BlockSpec((tm, D_MODEL), lambda i: (i, 0))],
        out_specs=(pl.BlockSpec((tm, D_MODEL), lambda i: (i, 0)), pl.BlockSpec((1, LANES), lambda i: (0, 0))),
        out_shape=(jax.ShapeDtypeStruct((s, D_MODEL), F32), jax.ShapeDtypeStruct((1, LANES), F32)),
        scratch_shapes=[pltpu.VMEM((1, D_MODEL), F32)],
        compiler_params=_params(("arbitrary",)),
    )(y, target)


def _out_proj_bwd(dy, pre, g_post, merged, wo_all, layer, name):
    s = dy.shape[0]
    tm = min(256, s)
    n_tiles = s // tm

    def body(dy_ref, pre_ref, g_ref, mg_ref, wo_ref, dm_ref, dwo_ref, dg_ref, acc):
        i = pl.program_id(0)

        @pl.when(i == 0)
        def _():
            acc[...] = jnp.zeros_like(acc)
            dg_ref[...] = jnp.zeros_like(dg_ref)
        dyv, pre_v = dy_ref[...], pre_ref[...]
        r = lax.rsqrt(jnp.mean(pre_v * pre_v, axis=-1, keepdims=True) + RMS_EPS)
        dg_ref[...] += jnp.sum(dyv * pre_v * r, axis=0, keepdims=True)
        a = dyv * g_ref[...]
        dpre = r * a - pre_v * (r * r * r) * jnp.mean(a * pre_v, axis=-1, keepdims=True)
        db = dpre.astype(BF16)
        acc[...] += _dot_tn(mg_ref[...], db)
        dm_ref[...] = _dot_nt(db, wo_ref[...].reshape(D_MODEL, D_MODEL))

        @pl.when(i == n_tiles - 1)
        def _():
            dwo_ref[...] = acc[...].astype(BF16)

    rows = lambda: pl.BlockSpec((tm, D_MODEL), lambda i: (i, 0))
    return _pcall(
        body, name=name, grid=(n_tiles,),
        in_specs=[rows(), rows(), pl.BlockSpec((1, D_MODEL), lambda i: (0, 0)), rows(),
                  pl.BlockSpec((N_DEV, None, D_MODEL // N_DEV, D_MODEL), lambda i: (0, layer, 0, 0))],
        out_specs=(rows(), pl.BlockSpec((D_MODEL, D_MODEL), lambda i: (0, 0)),
                   pl.BlockSpec((1, D_MODEL), lambda i: (0, 0))),
        out_shape=(jax.ShapeDtypeStruct((s, D_MODEL), F32), jax.ShapeDtypeStruct((D_MODEL, D_MODEL), BF16),
                   jax.ShapeDtypeStruct((1, D_MODEL), F32)),
        scratch_shapes=[pltpu.VMEM((D_MODEL, D_MODEL), F32)],
        compiler_params=_params(("arbitrary",)),
    )(dy, pre, g_post, merged, wo_all)


def _merge_bwd(dmerged, y_pool, y_conv, y_sb, u, wb_all, layer, name):
    s = dmerged.shape[0]
    tm = min(256, s)
    n_tiles = s // tm
    cols = D_MODEL // N_DEV

    def body(dm_ref, yp, yc, ys, m0, m1, m2, wb_ref, dum_ref, dyp, dyc, dys, dwb_ref, acc):
        i = pl.program_id(0)

        @pl.when(i == 0)
        def _():
            acc[...] = jnp.zeros_like(acc)
        dm = dm_ref[...]
        for n, (y_ref, m_ref, dy_ref) in enumerate(((yp, m0, dyp), (yc, m1, dyc), (ys, m2, dys))):
            yv = y_ref[...]
            wb = wb_ref[n]
            gate = _sigmoid(m_ref[...])
            proj = _dot(yv, wb)
            dum_ref[:, n * D_MODEL:(n + 1) * D_MODEL] = (dm * proj * gate * (1.0 - gate)).astype(BF16)
            dproj = (dm * gate).astype(BF16)
            acc[n] += _dot_tn(yv, dproj)
            dy_ref[...] = _dot_nt(dproj, wb)

        @pl.when(i == n_tiles - 1)
        def _():
            for j in range(N_DEV):
                for n in range(3):
                    dwb_ref[j, n] = acc[n, :, j * cols:(j + 1) * cols].astype(BF16)

    rows = lambda w: pl.BlockSpec((tm, w), lambda i: (i, 0))
    merge = lambda n: pl.BlockSpec((tm, D_MODEL), lambda i: (i, MERGE_BLOCK_1024 + n))
    return _pcall(
        body, name=name, grid=(n_tiles,),
        in_specs=[rows(D_MODEL), rows(WIDTH), rows(WIDTH), rows(WIDTH), merge(0), merge(1), merge(2),
                  pl.BlockSpec((None, 3, WIDTH, D_MODEL), lambda i: (layer, 0, 0, 0))],
        out_specs=(rows(3 * D_MODEL), rows(WIDTH), rows(WIDTH), rows(WIDTH),
                   pl.BlockSpec((N_DEV, 3, WIDTH, cols), lambda i: (0, 0, 0, 0))),
        out_shape=(jax.ShapeDtypeStruct((s, 3 * D_MODEL), BF16),
                   jax.ShapeDtypeStruct((s, WIDTH), F32), jax.ShapeDtypeStruct((s, WIDTH), F32),
                   jax.ShapeDtypeStruct((s, WIDTH), F32),
                   jax.ShapeDtypeStruct((N_DEV, 3, WIDTH, cols), BF16)),
        scratch_shapes=[pltpu.VMEM((3, WIDTH, D_MODEL), F32)],
        compiler_params=_params(("arbitrary",)),
    )(dmerged, y_pool, y_conv, y_sb, u, u, u, wb_all)


def _sb_bwd(u, o, dys, name, xchg=None):
    s = u.shape[0]
    tq = tk = min(128, s)
    pairs = SB_PAIRS
    width = pairs * LANES
    rows = 2 * pairs * tq
    pair_rows = lambda a: slice(a * 2 * tq, (a + 1) * 2 * tq)

    x_arrs, x_gather = xchg if xchg else ((), True)
    n_x = len(x_arrs)
    grid = (4 // pairs, s // tq)

    def body(*refs):
        q_ref, k_ref, v_ref, g_ref, o_ref, dys_ref = refs[:6]
        x_in, refs = refs[6:6 + n_x], refs[6 + n_x:]
        dq_ref, dk_ref, dv_ref, dg_ref = refs[:4]
        x_out, refs = refs[4:4 + n_x], refs[4 + n_x:]
        kbf, vbf, kst, z_s, ell_s, ls_s, cl_s, wb_s, g_s, bef_s, cg_s = refs[:11]
        x_sems = refs[11:]
        i = pl.program_id(1)
        if n_x:
            @pl.when((pl.program_id(0) == 0) & (i == 0))
            def _():
                _exchange_start(x_in, x_out, x_sems, x_gather)

        @pl.when(i == 0)
        def _():
            dk_ref[...] = jnp.zeros_like(dk_ref)
            dv_ref[...] = jnp.zeros_like(dv_ref)
            kf = k_ref[...]
            kbf[...] = kf.astype(BF16)
            vbf[...] = v_ref[...].astype(BF16)
            first_s = _first_head_lanes(s, width)
            kst[0] = jnp.where(first_s, kf, 0.0).astype(BF16)
            kst[1] = jnp.where(first_s, 0.0, kf).astype(BF16)

        first = _first_head_lanes(tq)
        mask = _causal_mask(tq, tk, 2 * pairs)
        m_strict = _suffix_matrix(tk, False)
        m_incl = _suffix_matrix(tk, True)

        gate = g_ref[...]
        sg = _sigmoid(gate)
        dy = dys_ref[...]
        ov = o_ref[...]
        dg_ref[...] = (dy * ov * (sg * (1.0 + gate * (1.0 - sg)))).astype(BF16)
        do = (dy * (gate * sg)).astype(BF16)
        prod = do.astype(F32) * ov
        row_sum = lambda v: jnp.broadcast_to(jnp.sum(v, axis=1, keepdims=True), (tq, tk))
        dsum, docat, qcat = [], [], []
        for a in range(pairs):
            pa = prod[:, _pair_lanes(a)]
            dsum += [row_sum(jnp.where(first, pa, 0.0)), row_sum(jnp.where(first, 0.0, pa))]
            docat.append(_stack_heads(do[:, _pair_lanes(a)], first))
            qcat.append(_stack_heads(q_ref[:, _pair_lanes(a)] * SB_SCALE, first))
        dsum = jnp.concatenate(dsum, axis=0)

        def block_start(b):
            return pl.multiple_of(jnp.maximum(b, 0) * tk, tk)

        def scores(b):
            off = block_start(b)
            z_s[...] = jnp.concatenate([_dot_nt(qcat[a], kbf[pl.ds(off, tk), _pair_lanes(a)]) for a in range(pairs)],
                                       axis=0)

        def log_weights(slot, m):
            ls, cs = _sb_log_terms(z_s[...], m, m_strict)
            cl = cl_s[...]
            ell_s[...] = ls + cs[:, :tk] + cl
            cl_s[...] = cl + cs[:, tk:]
            ls_s[slot] = ls

        def weights(b, m):
            off = block_start(b)
            dwt = jnp.concatenate([_dot_nt(docat[a], vbf[pl.ds(off, tk), _pair_lanes(a)]) for a in range(pairs)],
                                  axis=0)
            w = jnp.exp(ell_s[...])
            if m is not None:
                w = jnp.where(m, w, 0.0)
            wb = w.astype(BF16)
            g = dwt * wb.astype(F32)
            gs = _suffix_sums(g, m_incl)
            cg = cg_s[...]
            wb_s[...] = wb
            g_s[...] = g
            bef_s[...] = gs[:, :tk] + cg
            cg_s[...] = cg + gs[:, tk:]

        def grads(b, slot, dqs, m):
            g = g_s[...]
            before = dsum - bef_s[...]
            dz = g - jnp.exp(ls_s[slot]) * (g + before)
            if m is not None:
                dz = jnp.where(m, dz, 0.0)
            dzb = dz.astype(BF16)
            wb = wb_s[...]
            off = pl.multiple_of(b * tk, tk)
            new = []
            for a in range(pairs):
                r0 = a * 2 * tq
                kcat = jnp.concatenate([kst[0, pl.ds(off, tk), _pair_lanes(a)], kst[1, pl.ds(off, tk), _pair_lanes(a)]],
                                       axis=0)
                new.append(dqs[a] + _dot(jnp.concatenate([dzb[r0:r0 + tq], dzb[r0 + tq:r0 + 2 * tq]], axis=1), kcat))
                dk_ref[pl.ds(off, tk), _pair_lanes(a)] += _dot_tn(dzb[pair_rows(a)], qcat[a])
                dv_ref[pl.ds(off, tk), _pair_lanes(a)] += _dot_tn(wb[pair_rows(a)], docat[a])
            return tuple(new)

        zero = jnp.zeros((rows, tk), F32)
        cl_s[...] = zero
        cg_s[...] = zero
        scores(i)
        log_weights(0, mask)
        scores(i - 1)
        weights(i, mask)
        log_weights(1, None)
        scores(i - 2)
        dqs = grads(i, 0, tuple(jnp.zeros((tq, LANES), F32) for _ in range(pairs)), mask)
        weights(i - 1, None)
        log_weights(0, None)
        scores(i - 3)

        def step(n, dqs):
            slot = jnp.bitwise_and(n, 1)
            dqs = grads(i - n, slot, dqs, None)
            weights(i - n - 1, None)
            log_weights(slot, None)
            scores(i - n - 3)
            return dqs

        dqs = lax.fori_loop(1, i + 1, step, dqs)
        dq_ref[...] = (jnp.concatenate(dqs, axis=1) * SB_SCALE).astype(BF16)
        if n_x:
            @pl.when((pl.program_id(0) == grid[0] - 1) & (i == grid[1] - 1))
            def _():
                _exchange_wait(x_in, x_out, x_sems, x_gather)

    base = lambda cb: cb // pairs
    qblk = lambda cb: pl.BlockSpec((tq, width), lambda p, i: (i, base(cb) + p))
    full = lambda cb: pl.BlockSpec((s, width), lambda p, i: (0, base(cb) + p))
    state = pltpu.VMEM((rows, tk), F32)
    return _pcall(
        body, name=name, grid=grid,
        in_specs=[qblk(CB_SB_Q), full(CB_SB_K), full(CB_SB_V), qblk(CB_SB_G), qblk(0), qblk(0)] + [ANY_SPEC] * n_x,
        out_specs=(qblk(0), full(0), full(0), qblk(0)) + (ANY_SPEC,) * n_x,
        out_shape=(jax.ShapeDtypeStruct((s, WIDTH), BF16), jax.ShapeDtypeStruct((s, WIDTH), F32),
                   jax.ShapeDtypeStruct((s, WIDTH), F32), jax.ShapeDtypeStruct((s, WIDTH), BF16))
        + tuple(_exchange_out_shapes(x_arrs, x_gather)),
        scratch_shapes=[pltpu.VMEM((s, width), BF16), pltpu.VMEM((s, width), BF16), pltpu.VMEM((2, s, width), BF16),
                        state, state, pltpu.VMEM((2, rows, tk), F32), state, pltpu.VMEM((rows, tk), BF16),
                        state, state, state] + (_exchange_sems(n_x) if n_x else []),
        compiler_params=_params(("arbitrary", "arbitrary")),
    )(u, u, u, u, o, dys, *x_arrs)


def _conv_bwd(u, conv_w, conv_b, dyc, name):
    s = u.shape[0]
    t = min(256, s)
    n_tiles = s // t

    def body(xc_ref, gb_ref, gc_ref, cg_ref, w_ref, b_ref, dy_ref,
             dxc_ref, dgb_ref, dgc_ref, dcg_ref, dw_ref, db_ref, zs, ds):
        zs[0:CONV_HALO, :] = jnp.zeros((CONV_HALO, LANES), F32)
        zs[CONV_HALO:, :] = gc_ref[...] * xc_ref[...]
        ds[s:, :] = jnp.zeros((CONV_HALO, LANES), F32)
        w0, w1, w2 = w_ref[0:1, :], w_ref[1:2, :], w_ref[2:3, :]
        bias = b_ref[...]

        def first(i, sums):
            t0 = pl.multiple_of(i * t, t)
            z0, z1, z2 = _conv_taps(zs, t0, t)
            pre = w0 * z2 + w1 * z1 + w2 * z0 + bias
            gate = cg_ref[pl.ds(t0, t), :]
            sg = _sigmoid(gate)
            gb = gb_ref[pl.ds(t0, t), :]
            dy = dy_ref[pl.ds(t0, t), :]
            dcg_ref[pl.ds(t0, t), :] = (dy * gb * pre * (sg * (1.0 + gate * (1.0 - sg)))).astype(BF16)
            dgb_ref[pl.ds(t0, t), :] = (dy * pre * (gate * sg)).astype(BF16)
            dc = dy * gb * (gate * sg)
            ds[pl.ds(t0, t), :] = dc
            red = lambda v: jnp.sum(v, axis=0, keepdims=True)
            return (sums[0] + red(dc * z2), sums[1] + red(dc * z1), sums[2] + red(dc * z0), sums[3] + red(dc))

        zrow = jnp.zeros((1, LANES), F32)
        sw0, sw1, sw2, sb = lax.fori_loop(0, n_tiles, first, (zrow, zrow, zrow, zrow))
        dw_ref[0:1, :] = sw0
        dw_ref[1:2, :] = sw1
        dw_ref[2:3, :] = sw2
        db_ref[...] = sb

        def second(i, carry):
            t0 = pl.multiple_of(i * t, t)
            ext = ds[pl.ds(t0, t + CONV_HALO), :]
            n = t + CONV_HALO
            d0 = ext[:t, :]
            d1 = pltpu.roll(ext, n - 1, 0)[:t, :]
            d2 = pltpu.roll(ext, n - 2, 0)[:t, :]
            dz = w2 * d0 + w1 * d1 + w0 * d2
            dgc_ref[pl.ds(t0, t), :] = (dz * xc_ref[pl.ds(t0, t), :]).astype(BF16)
            dxc_ref[pl.ds(t0, t), :] = (dz * gc_ref[pl.ds(t0, t), :]).astype(BF16)
            return carry

        lax.fori_loop(0, n_tiles, second, 0)

    col = lambda base: pl.BlockSpec((s, LANES), lambda j: (0, base + j))
    dcol = jax.ShapeDtypeStruct((s, WIDTH), BF16)
    return _pcall(
        body, name=name, grid=(4,),
        in_specs=[col(CB_CONV_X), col(CB_CONV_GB), col(CB_CONV_GC), col(CB_CONV_G),
                  pl.BlockSpec((3, LANES), lambda j: (0, j)), pl.BlockSpec((1, LANES), lambda j: (0, j)), col(0)],
        out_specs=(col(0), col(0), col(0), col(0),
                   pl.BlockSpec((3, LANES), lambda j: (0, j)), pl.BlockSpec((1, LANES), lambda j: (0, j))),
        out_shape=(dcol, dcol, dcol, dcol,
                   jax.ShapeDtypeStruct((3, WIDTH), F32), jax.ShapeDtypeStruct((1, WIDTH), F32)),
        scratch_shapes=[pltpu.VMEM((CONV_HALO + s, LANES), F32), pltpu.VMEM((s + CONV_HALO, LANES), F32)],
        compiler_params=_params(("arbitrary",)),
    )(u, u, u, u, conv_w, conv_b, dyc)


def _pool_bwd(u, pool_w, pool_scale, dyp, name):
    s = u.shape[0]
    t = min(256, s)
    n_tiles = s // t

    def body(pv_ref, pg_ref, w_ref, sc_ref, dy_ref, dpv_ref, dpg_ref, dw_ref, dsc_ref, vs, es, dps):
        grp = pl.program_id(0)
        vs[0:POOL_HALO, :] = jnp.zeros((POOL_HALO, LANES), F32)
        vs[POOL_HALO:, :] = pv_ref[...]
        es[s:, :] = jnp.zeros((POOL_HALO, LANES), F32)
        wb = w_ref[...].astype(BF16)
        scale = sc_ref[...]

        def first(i, sums):
            dw, dsc = sums
            t0 = pl.multiple_of(i * t, t)
            win, v = _pool_window(vs, t0, t, grp)
            cnt = _pool_count(t0, t, grp)
            pb = (win / cnt - v).astype(BF16)
            mixed = _dot(pb, wb)
            gate = pg_ref[pl.ds(t0, t), :]
            sg = _sigmoid(gate)
            dy = dy_ref[pl.ds(t0, t), :]
            dpg_ref[pl.ds(t0, t), :] = (dy * (mixed * scale) * (sg * (1.0 + gate * (1.0 - sg)))).astype(BF16)
            dms = dy * (gate * sg)
            dsc = dsc + jnp.sum(dms * mixed, axis=0, keepdims=True)
            dmb = (dms * scale).astype(BF16)
            dw = dw + _dot_tn(pb, dmb)
            dpooled = _dot_nt(dmb, wb)
            dps[pl.ds(t0, t), :] = dpooled
            es[pl.ds(t0, t), :] = dpooled / cnt
            return dw, dsc

        dw, dsc = lax.fori_loop(0, n_tiles, first, (jnp.zeros((LANES, LANES), F32), jnp.zeros((1, LANES), F32)))
        dw_ref[...] = dw
        dsc_ref[...] = dsc

        def second(i, carry):
            t0 = pl.multiple_of(i * t, t)
            ext = es[pl.ds(t0, t + POOL_HALO), :]
            n = t + POOL_HALO
            f2 = ext + pltpu.roll(ext, n - 1, 0)
            f4 = f2 + pltpu.roll(f2, n - 2, 0)
            f8 = f4 + pltpu.roll(f4, n - 4, 0)
            f16 = f8 + pltpu.roll(f8, n - 8, 0)
            sel = jnp.where(grp == 0, f2, jnp.where(grp == 1, f4, jnp.where(grp == 2, f8, f16)))
            dpv_ref[pl.ds(t0, t), :] = (sel[:t, :] - dps[pl.ds(t0, t), :]).astype(BF16)
            return carry

        lax.fori_loop(0, n_tiles, second, 0)

    col = lambda base: pl.BlockSpec((s, LANES), lambda g: (0, base + g))
    dcol = jax.ShapeDtypeStruct((s, WIDTH), BF16)
    return _pcall(
        body, name=name, grid=(4,),
        in_specs=[col(CB_POOL_V), col(CB_POOL_G), pl.BlockSpec((None, LANES, LANES), lambda g: (g, 0, 0)),
                  pl.BlockSpec((1, LANES), lambda g: (0, g)), col(0)],
        out_specs=(col(0), col(0), pl.BlockSpec((None, LANES, LANES), lambda g: (g, 0, 0)),
                   pl.BlockSpec((1, LANES), lambda g: (0, g))),
        out_shape=(dcol, dcol, jax.ShapeDtypeStruct((4, LANES, LANES), F32), jax.ShapeDtypeStruct((1, WIDTH), F32)),
        scratch_shapes=[pltpu.VMEM((POOL_HALO + s, LANES), F32), pltpu.VMEM((s + POOL_HALO, LANES), F32),
                        pltpu.VMEM((s, LANES), F32)],
        compiler_params=_params(("arbitrary",)),
    )(u, u, pool_w, pool_scale, dyp)


def _in_proj_bwd_x(du, w_all, x, g_pre, dy, name):
    s = x.shape[0]
    tm = min(512, s)

    def body(du_ref, w_ref, x_ref, g_ref, dy_ref, dx_ref, dg_ref, acc):
        i, k = pl.program_id(0), pl.program_id(1)

        @pl.when(k == 0)
        def _():
            acc[...] = jnp.zeros_like(acc)

        @pl.when((k == 0) & (i == 0))
        def _():
            dg_ref[...] = jnp.zeros_like(dg_ref)
        acc[...] += _dot_nt(du_ref[...], w_ref[...])

        @pl.when(k == N_DEV - 1)
        def _():
            dh, xv = acc[...], x_ref[...]
            r = lax.rsqrt(jnp.mean(xv * xv, axis=-1, keepdims=True) + RMS_EPS)
            dg_ref[...] += jnp.sum(dh * xv * r, axis=0, keepdims=True)
            a = dh * g_ref[...]
            dx_ref[...] = dy_ref[...] + r * a - xv * (r * r * r) * jnp.mean(a * xv, axis=-1, keepdims=True)

    rows = lambda: pl.BlockSpec((tm, D_MODEL), lambda i, k: (i, 0))
    vec = lambda: pl.BlockSpec((1, D_MODEL), lambda i, k: (0, 0))
    return _pcall(
        body, name=name, grid=(s // tm, N_DEV),
        in_specs=[pl.BlockSpec((tm, COLS_PER_DEV), lambda i, k: (i, k)),
                  pl.BlockSpec((None, D_MODEL, COLS_PER_DEV), lambda i, k: (k, 0, 0)),
                  rows(), vec(), rows()],
        out_specs=(rows(), vec()),
        out_shape=(jax.ShapeDtypeStruct((s, D_MODEL), F32), jax.ShapeDtypeStruct((1, D_MODEL), F32)),
        scratch_shapes=[pltpu.VMEM((tm, D_MODEL), F32)],
        compiler_params=_params(("arbitrary", "arbitrary")),
    )(du, w_all, x, g_pre, dy)


def _in_proj_bwd_w(h, du, name):
    s = h.shape[0]
    tk = min(512, s)
    n_k = s // tk

    def body(h_ref, du_ref, out_ref, acc):
        k = pl.program_id(1)

        @pl.when(k == 0)
        def _():
            acc[...] = jnp.zeros_like(acc)
        acc[...] += _dot_tn(h_ref[...], du_ref[...])

        @pl.when(k == n_k - 1)
        def _():
            out_ref[...] = acc[...].astype(BF16)

    return _pcall(
        body, name=name, grid=(N_DEV, n_k),
        in_specs=[pl.BlockSpec((tk, D_MODEL), lambda j, k: (k, 0)),
                  pl.BlockSpec((tk, COLS_PER_DEV), lambda j, k: (k, j))],
        out_specs=pl.BlockSpec((None, D_MODEL, COLS_PER_DEV), lambda j, k: (j, 0, 0)),
        out_shape=jax.ShapeDtypeStruct((N_DEV, D_MODEL, COLS_PER_DEV), BF16),
        scratch_shapes=[pltpu.VMEM((D_MODEL, COLS_PER_DEV), F32)],
        compiler_params=_params(("parallel", "arbitrary")),
    )(h, du)


def _adamw_math(g, w, m, v):
    m_new = ADAM_B1 * m + (1.0 - ADAM_B1) * g
    v_new = ADAM_B2 * v + (1.0 - ADAM_B2) * (g * g)
    m_hat = m_new / (1.0 - ADAM_B1 ** ADAM_STEP)
    v_hat = v_new / (1.0 - ADAM_B2 ** ADAM_STEP)
    delta = -ADAM_LR * (m_hat / (jnp.sqrt(v_hat) + ADAM_EPS) + ADAM_WD * w)
    return delta, m_new, v_new


def _sum_partials(p_ref):
    total = p_ref[0].astype(F32)
    for d in range(1, N_DEV):
        total = total + p_ref[d].astype(F32)
    return total


def _adamw_layers(parts0, parts1, w, m, v, name):
    _, r, c = w.shape
    tr = min(128, r)
    n_r = r // tr

    def body(p0_ref, p1_ref, w_ref, m_ref, v_ref, g_ref, d_ref, mo_ref, vo_ref):
        layer = pl.program_id(0)

        @pl.when(layer == 0)
        def _():
            g_ref[...] = _sum_partials(p0_ref)

        @pl.when(layer == 1)
        def _():
            g_ref[...] = _sum_partials(p1_ref)
        d_ref[...], mo_ref[...], vo_ref[...] = _adamw_math(g_ref[...], w_ref[...], m_ref[...], v_ref[...])

    part = lambda which: pl.BlockSpec((N_DEV, tr, c), lambda l, i: (0, jnp.where(l == which, i, 0), 0))
    par = lambda: pl.BlockSpec((None, tr, c), lambda l, i: (l, i, 0))
    out = jax.ShapeDtypeStruct(w.shape, F32)
    return _pcall(
        body, name=name, grid=(2, n_r),
        in_specs=[part(0), part(1), par(), par(), par()],
        out_specs=(par(), par(), par(), par()),
        out_shape=(out, out, out, out),
        compiler_params=_params(("arbitrary", "arbitrary")),
    )(parts0, parts1, w, m, v)


def _adamw_small(parts, w, m, v, name):
    def body(p_ref, w_ref, m_ref, v_ref, g_ref, d_ref, mo_ref, vo_ref):
        g = _sum_partials(p_ref)
        g_ref[...] = g
        d_ref[...], mo_ref[...], vo_ref[...] = _adamw_math(g, w_ref[...], m_ref[...], v_ref[...])

    out = jax.ShapeDtypeStruct(w.shape, F32)
    return _pcall(body, name=name, out_shape=(out, out, out, out), compiler_params=_params())(parts, w, m, v)


def _adamw_plain(g, w, m, v, name):
    def body(g_ref, w_ref, m_ref, v_ref, d_ref, mo_ref, vo_ref):
        d_ref[...], mo_ref[...], vo_ref[...] = _adamw_math(g_ref[...], w_ref[...], m_ref[...], v_ref[...])

    out = jax.ShapeDtypeStruct(w.shape, F32)
    return _pcall(body, name=name, out_shape=(out, out, out), compiler_params=_params())(g, w, m, v)


def _rows128(a):
    return a.reshape(-1, LANES)


SMALL_NAMES = ("pre_norm_g", "pool_w", "pool_scale", "conv_w", "conv_b", "post_norm_g")


def kernel(x, pre_norm_g, w_in, pool_w, pool_scale, conv_w, conv_b, w_branch, w_out, post_norm_g, loss_target, m_pre_norm_g, m_w_in, m_pool_w, m_pool_scale, m_conv_w, m_conv_b, m_w_branch, m_w_out, m_post_norm_g, v_pre_norm_g, v_w_in, v_pool_w, v_pool_scale, v_conv_w, v_conv_b, v_w_branch, v_w_out, v_post_norm_g):
    s = x.shape[1]
    me = 4 * lax.axis_index("x") + 2 * lax.axis_index("y") + lax.axis_index("c")
    x0 = x[0]
    target = loss_target[0]
    conv_cols = conv_w.shape[-1]

    conv_w_pad = jnp.pad(conv_w.reshape(2 * 3, conv_cols), ((0, 2), (0, LANES - conv_cols)))
    w_in_all = [None, None]
    w_in_all[0], cw_g = _exchange([w_in[0].astype(BF16), conv_w_pad], True, "gather_w_in_0")
    conv_w_full = cw_g[:, :6, :conv_cols].reshape(N_DEV, 2, 3, conv_cols).transpose(1, 2, 0, 3).reshape(2, 3, WIDTH)
    later_weights = ([w_in[1].astype(BF16), w_branch.astype(BF16), w_out.astype(BF16)], True)

    saved = []
    xin = x0
    for l in range(2):
        u, h = _in_proj_fwd(xin, pre_norm_g[l:l + 1], w_in_all[l], f"in_proj_fwd_{l}")
        y_pool = _pool_fwd(u, pool_w[l], pool_scale[l:l + 1], f"pool_fwd_{l}")
        y_conv = _conv_fwd(u, conv_w_full[l], conv_b[l:l + 1], f"conv_fwd_{l}")
        if l == 0:
            o_sb, y_sb, w_in_all[1], wb_g, wo_all = _sb_fwd(u, f"sb_fwd_{l}", later_weights)
            wb_all = wb_g.transpose(1, 2, 3, 0, 4).reshape(2, 3, WIDTH, D_MODEL)
        else:
            o_sb, y_sb = _sb_fwd(u, f"sb_fwd_{l}")
        xout, merged, pre = _merge_out_fwd(y_pool, y_conv, y_sb, u, wb_all, wo_all, xin, post_norm_g[l:l + 1], l,
                                           f"merge_out_fwd_{l}")
        saved.append((xin, u, h, y_pool, y_conv, y_sb, o_sb, merged, pre))
        xin = xout

    dy, loss_row = _loss_and_grad(xin, target, "loss")

    small = [None, None]
    recv = [None, None]
    ready = []
    for l in (1, 0):
        xl, u, h, y_pool, y_conv, y_sb, o_sb, merged, pre = saved[l]
        dmerged, dwo, dg_post = _out_proj_bwd(dy, pre, post_norm_g[l:l + 1], merged, wo_all, l, f"out_proj_bwd_{l}")
        du_merge, dyp, dyc, dys, dwb = _merge_bwd(dmerged, y_pool, y_conv, y_sb, u, wb_all, l, f"merge_bwd_{l}")
        dwb = dwb.reshape(N_DEV, 3 * WIDTH, D_MODEL // N_DEV)
        dwo = dwo.reshape(N_DEV, D_MODEL // N_DEV, D_MODEL)
        if l == 1:
            dq, dk, dv, dsg = _sb_bwd(u, o_sb, dys, f"sb_bwd_{l}")
        else:
            dq, dk, dv, dsg, *got = _sb_bwd(u, o_sb, dys, f"sb_bwd_{l}", (ready + [dwb, dwo], False))
            recv[1] = got[:3]
        dxc, dgb, dgc, dcg, dcw, dcb = _conv_bwd(u, conv_w_full[l], conv_b[l:l + 1], dyc, f"conv_bwd_{l}")
        dpv, dpg, dpw, dps = _pool_bwd(u, pool_w[l], pool_scale[l:l + 1], dyp, f"pool_bwd_{l}")
        du = jnp.concatenate([dpv, dpg, dxc, dgb, dgc, dcg, dq, dk.astype(BF16), dv.astype(BF16), dsg, du_merge],
                             axis=1)
        dwi = _in_proj_bwd_w(h, du, f"in_proj_bwd_w_{l}")
        if l == 1:
            ready = [dwi, dwb, dwo]
        else:
            recv[0] = list(_exchange([dwi], False, "scatter_w_in_0")) + got[3:]
        dx, dg_pre = _in_proj_bwd_x(du, w_in_all[l], xl, pre_norm_g[l:l + 1], dy, f"in_proj_bwd_x_{l}")
        small[l] = dict(pre_norm_g=dg_pre, pool_w=dpw, pool_scale=dps, conv_w=dcw, conv_b=dcb, post_norm_g=dg_post)
        dy = dx
    grad_x = dy[None]

    packed = jnp.concatenate(
        [_rows128(jnp.stack([small[0][n], small[1][n]])) for n in SMALL_NAMES]
        + [jnp.pad(loss_row, ((0, 7), (0, 0)))], axis=0)
    (packed_all,) = _exchange([packed], True, "gather_small")
    sizes = dict(pre_norm_g=16, pool_w=1024, pool_scale=8, conv_w=24, conv_b=8, post_norm_g=16)
    n_rows = sum(sizes.values())
    loss = jnp.sum(packed_all[:, n_rows, 0])

    given = dict(pre_norm_g=(pre_norm_g, m_pre_norm_g, v_pre_norm_g), pool_w=(pool_w, m_pool_w, v_pool_w),
                 pool_scale=(pool_scale, m_pool_scale, v_pool_scale), conv_b=(conv_b, m_conv_b, v_conv_b),
                 post_norm_g=(post_norm_g, m_post_norm_g, v_post_norm_g))
    zeros_cw = jnp.zeros((sizes["conv_w"], LANES), F32)
    pack3 = [jnp.concatenate([zeros_cw if n == "conv_w" else _rows128(given[n][k]) for n in SMALL_NAMES], axis=0)
             for k in range(3)]
    sg, sd, sm, sv = _adamw_small(packed_all[:, :n_rows], pack3[0], pack3[1], pack3[2], "adamw_small")

    def unpack(buf, name, shape):
        start = 0
        for n in SMALL_NAMES:
            if n == name:
                return buf[start:start + sizes[n]].reshape(shape)
            start += sizes[n]

    out = {}
    for n in ("pre_norm_g", "pool_w", "pool_scale", "conv_b", "post_norm_g"):
        shape = given[n][0].shape
        out[n] = tuple(unpack(b, n, shape) for b in (sg, sd, sm, sv))
    g_cw = lax.dynamic_slice_in_dim(unpack(sg, "conv_w", (2, 3, WIDTH)), me * conv_cols, conv_cols, axis=2)
    cw2 = lambda a: a.reshape(6, conv_cols)
    d_cw, m_cw, v_cw = _adamw_plain(cw2(g_cw), cw2(conv_w), cw2(m_conv_w), cw2(v_conv_w), "adamw_conv_w")
    out["conv_w"] = (g_cw,) + tuple(a.reshape(2, 3, conv_cols) for a in (d_cw, m_cw, v_cw))

    out["w_in"] = _adamw_layers(recv[0][0], recv[1][0], w_in, m_w_in, v_w_in, "adamw_w_in")
    cols = D_MODEL // N_DEV
    wb3 = lambda a: a.reshape(2, 3 * WIDTH, cols)
    out["w_branch"] = tuple(a.reshape(2, 3, WIDTH, cols) for a in _adamw_layers(
        recv[0][1], recv[1][1], wb3(w_branch), wb3(m_w_branch), wb3(v_w_branch), "adamw_w_branch"))
    out["w_out"] = _adamw_layers(recv[0][2], recv[1][2], w_out, m_w_out, v_w_out, "adamw_w_out")

    order = ("pre_norm_g", "w_in", "pool_w", "pool_scale", "conv_w", "conv_b", "w_branch", "w_out", "post_norm_g")
    return (loss, grad_x) + tuple(out[n][k] for k in range(4) for n in order)
```

```python
import functools

import jax
import jax.numpy as jnp
from jax import lax
from jax.experimental import pallas as pl
from jax.experimental.pallas import tpu as pltpu

F32 = jnp.float32
BF16 = jnp.bfloat16

N_DEV = 8
D_MODEL = 1024
WIDTH = 512
N_IN = 8192
COLS_PER_DEV = N_IN // N_DEV
HEAD_DIM = 64
LANES = 128
SB_SCALE = HEAD_DIM ** -0.5
RMS_EPS = 1e-6
POOL_HALO = 16
CONV_HALO = 8
ADAM_LR, ADAM_B1, ADAM_B2, ADAM_EPS, ADAM_WD, ADAM_STEP = 0.001, 0.9, 0.999, 1e-08, 0.01, 10
VMEM_LIMIT = 60 * 1024 * 1024

CB_POOL_V, CB_POOL_G = 0, 4
CB_CONV_X, CB_CONV_GB, CB_CONV_GC, CB_CONV_G = 8, 12, 16, 20
CB_SB_Q, CB_SB_K, CB_SB_V, CB_SB_G = 24, 28, 32, 36
MERGE_BLOCK_1024 = 5


def _pcall(body, **kw):
    return pl.pallas_call(body, **kw)


def _params(sem=None):
    if sem is None:
        return pltpu.CompilerParams(vmem_limit_bytes=VMEM_LIMIT)
    return pltpu.CompilerParams(dimension_semantics=sem, vmem_limit_bytes=VMEM_LIMIT)


def _sigmoid(x):
    return 1.0 / (1.0 + jnp.exp(-x))


def _dot(a, b):
    return jnp.dot(a, b, preferred_element_type=F32)


def _dot_nt(a, b):
    return lax.dot_general(a, b, (((1,), (1,)), ((), ())), preferred_element_type=F32)


def _dot_tn(a, b):
    return lax.dot_general(a, b, (((0,), (0,)), ((), ())), preferred_element_type=F32)


def _split_bf16(x):
    hi = x.astype(BF16)
    lo = (x - hi.astype(F32)).astype(BF16)
    return hi, lo


N_PEER = N_DEV - 1
ANY_SPEC = pl.BlockSpec(memory_space=pl.ANY)


def _exchange_copies(ins, outs, send_sems, recv_sems, local_sems, gather, with_recvs=True):
    n = len(ins)
    x, y, c = lax.axis_index("x"), lax.axis_index("y"), lax.axis_index("c")
    me = 4 * x + 2 * y + c
    flip = lambda v, bit: 1 - v if bit else v
    local, sends, recvs = [], [], []
    for a in range(n):
        src = ins[a] if gather else ins[a].at[me]
        local.append(pltpu.make_async_copy(src, outs[a].at[me], local_sems.at[a]))
    for k in range(N_PEER):
        px, py, pc = flip(x, ((k + 1) >> 2) & 1), flip(y, ((k + 1) >> 1) & 1), flip(c, (k + 1) & 1)
        peer_id = 4 * px + 2 * py + pc
        for a in range(n):
            src = ins[a] if gather else ins[a].at[peer_id]
            common = dict(src_ref=src, send_sem=send_sems.at[a * N_PEER + k], recv_sem=recv_sems.at[a * N_PEER + k],
                          device_id=(px, py, pc), device_id_type=pl.DeviceIdType.MESH)
            sends.append(pltpu.make_async_remote_copy(dst_ref=outs[a].at[me], **common))
            if with_recvs:
                recvs.append(pltpu.make_async_remote_copy(dst_ref=outs[a].at[peer_id], **common))
    return local, sends, recvs


def _exchange_start(ins, outs, sems, gather):
    local, sends, _ = _exchange_copies(ins, outs, *sems, gather, with_recvs=False)
    for cp in local + sends:
        cp.start()


def _exchange_wait(ins, outs, sems, gather):
    local, sends, recvs = _exchange_copies(ins, outs, *sems, gather)
    for cp in recvs:
        cp.wait_recv()
    for cp in sends:
        cp.wait_send()
    for cp in local:
        cp.wait()


def _exchange_out_shapes(arrs, gather):
    return [jax.ShapeDtypeStruct((N_DEV,) + tuple(a.shape if gather else a.shape[1:]), a.dtype) for a in arrs]


def _exchange_sems(n):
    return [pltpu.SemaphoreType.DMA((n * N_PEER,)), pltpu.SemaphoreType.DMA((n * N_PEER,)),
            pltpu.SemaphoreType.DMA((n,))]


def _exchange(arrs, gather, name):
    n = len(arrs)

    def body(*refs):
        ins, outs, sems = refs[:n], refs[n:2 * n], refs[2 * n:]
        _exchange_start(ins, outs, sems, gather)
        _exchange_wait(ins, outs, sems, gather)

    return _pcall(
        body, name=name,
        out_shape=tuple(_exchange_out_shapes(arrs, gather)),
        in_specs=[ANY_SPEC] * n, out_specs=tuple([ANY_SPEC] * n),
        scratch_shapes=_exchange_sems(n),
    )(*arrs)


def _in_proj_fwd(x, g, w_all, name):
    s = x.shape[0]
    tm = min(512, s)

    def body(x_ref, g_ref, w_ref, u_ref, h_ref, hs):
        @pl.when(pl.program_id(1) == 0)
        def _():
            xv = x_ref[...]
            r = lax.rsqrt(jnp.mean(xv * xv, axis=-1, keepdims=True) + RMS_EPS)
            hv = (xv * r * g_ref[...]).astype(BF16)
            hs[...] = hv
            h_ref[...] = hv
        u_ref[...] = _dot(hs[...], w_ref[...])

    return _pcall(
        body, name=name, grid=(s // tm, N_DEV),
        in_specs=[pl.BlockSpec((tm, D_MODEL), lambda i, j: (i, 0)),
                  pl.BlockSpec((1, D_MODEL), lambda i, j: (0, 0)),
                  pl.BlockSpec((None, D_MODEL, COLS_PER_DEV), lambda i, j: (j, 0, 0))],
        out_specs=(pl.BlockSpec((tm, COLS_PER_DEV), lambda i, j: (i, j)),
                   pl.BlockSpec((tm, D_MODEL), lambda i, j: (i, 0))),
        out_shape=(jax.ShapeDtypeStruct((s, N_IN), F32), jax.ShapeDtypeStruct((s, D_MODEL), BF16)),
        scratch_shapes=[pltpu.VMEM((tm, D_MODEL), BF16)],
        compiler_params=_params(("parallel", "arbitrary")),
    )(x, g, w_all)


def _pool_window(vs, t0, t, grp):
    ext = vs[pl.ds(t0, t + POOL_HALO), :]
    s2 = ext + pltpu.roll(ext, 1, 0)
    s4 = s2 + pltpu.roll(s2, 2, 0)
    s8 = s4 + pltpu.roll(s4, 4, 0)
    s16 = s8 + pltpu.roll(s8, 8, 0)
    sel = jnp.where(grp == 0, s2, jnp.where(grp == 1, s4, jnp.where(grp == 2, s8, s16)))
    return sel[POOL_HALO:, :], ext[POOL_HALO:, :]


def _pool_count(t0, t, grp):
    pos = t0 + lax.broadcasted_iota(jnp.int32, (t, 1), 0)
    return jnp.minimum(pos + 1, jnp.left_shift(2, grp)).astype(F32)


def _pool_fwd(u, pool_w, pool_scale, name):
    s = u.shape[0]
    t = min(256, s)

    def body(pv_ref, pg_ref, w_ref, sc_ref, y_ref, vs):
        grp = pl.program_id(0)
        vs[0:POOL_HALO, :] = jnp.zeros((POOL_HALO, LANES), F32)
        vs[POOL_HALO:, :] = pv_ref[...]
        wb = w_ref[...].astype(BF16)
        scale = sc_ref[...]

        def tile(i, carry):
            t0 = pl.multiple_of(i * t, t)
            win, v = _pool_window(vs, t0, t, grp)
            pooled = win / _pool_count(t0, t, grp) - v
            mixed = _dot(pooled.astype(BF16), wb)
            gate = pg_ref[pl.ds(t0, t), :]
            y_ref[pl.ds(t0, t), :] = (mixed * scale * (gate * _sigmoid(gate))).astype(BF16)
            return carry

        lax.fori_loop(0, s // t, tile, 0)

    return _pcall(
        body, name=name, grid=(4,),
        in_specs=[pl.BlockSpec((s, LANES), lambda g: (0, CB_POOL_V + g)),
                  pl.BlockSpec((s, LANES), lambda g: (0, CB_POOL_G + g)),
                  pl.BlockSpec((None, LANES, LANES), lambda g: (g, 0, 0)),
                  pl.BlockSpec((1, LANES), lambda g: (0, g))],
        out_specs=pl.BlockSpec((s, LANES), lambda g: (0, g)),
        out_shape=jax.ShapeDtypeStruct((s, WIDTH), BF16),
        scratch_shapes=[pltpu.VMEM((POOL_HALO + s, LANES), F32)],
        compiler_params=_params(("arbitrary",)),
    )(u, u, pool_w, pool_scale)


def _conv_taps(zs, t0, t):
    ext = zs[pl.ds(t0, t + CONV_HALO), :]
    z0 = ext[CONV_HALO:, :]
    z1 = pltpu.roll(ext, 1, 0)[CONV_HALO:, :]
    z2 = pltpu.roll(ext, 2, 0)[CONV_HALO:, :]
    return z0, z1, z2


def _conv_fwd(u, conv_w, conv_b, name):
    s = u.shape[0]
    t = min(256, s)

    def body(xc_ref, gb_ref, gc_ref, cg_ref, w_ref, b_ref, y_ref, zs):
        zs[0:CONV_HALO, :] = jnp.zeros((CONV_HALO, LANES), F32)
        zs[CONV_HALO:, :] = gc_ref[...] * xc_ref[...]
        w0, w1, w2 = w_ref[0:1, :], w_ref[1:2, :], w_ref[2:3, :]
        bias = b_ref[...]

        def tile(i, carry):
            t0 = pl.multiple_of(i * t, t)
            z0, z1, z2 = _conv_taps(zs, t0, t)
            conv = w0 * z2 + w1 * z1 + w2 * z0
            gate = cg_ref[pl.ds(t0, t), :]
            y = gb_ref[pl.ds(t0, t), :] * (conv + bias) * (gate * _sigmoid(gate))
            y_ref[pl.ds(t0, t), :] = y.astype(BF16)
            return carry

        lax.fori_loop(0, s // t, tile, 0)

    col = lambda base: pl.BlockSpec((s, LANES), lambda j: (0, base + j))
    return _pcall(
        body, name=name, grid=(4,),
        in_specs=[col(CB_CONV_X), col(CB_CONV_GB), col(CB_CONV_GC), col(CB_CONV_G),
                  pl.BlockSpec((3, LANES), lambda j: (0, j)),
                  pl.BlockSpec((1, LANES), lambda j: (0, j))],
        out_specs=pl.BlockSpec((s, LANES), lambda j: (0, j)),
        out_shape=jax.ShapeDtypeStruct((s, WIDTH), BF16),
        scratch_shapes=[pltpu.VMEM((CONV_HALO + s, LANES), F32)],
        compiler_params=_params(("arbitrary",)),
    )(u, u, u, u, conv_w, conv_b)


def _first_head_lanes(rows, width=LANES):
    lane = lax.broadcasted_iota(jnp.int32, (rows, width), 1)
    return jnp.bitwise_and(lane, LANES - 1) < HEAD_DIM


def _stack_heads(x, first):
    zero = jnp.zeros_like(x)
    return jnp.concatenate([jnp.where(first, x, zero), jnp.where(first, zero, x)], axis=0).astype(BF16)


def _causal_mask(tq, tk, copies):
    row = lax.broadcasted_iota(jnp.int32, (tq, tk), 0)
    col = lax.broadcasted_iota(jnp.int32, (tq, tk), 1)
    return jnp.concatenate([col < row] * copies, axis=0)


def _suffix_matrix(tk, inclusive):
    r = lax.broadcasted_iota(jnp.int32, (2 * tk, 2 * tk), 0)
    c = lax.broadcasted_iota(jnp.int32, (2 * tk, 2 * tk), 1)
    r = jnp.where(r >= tk, r - tk, r)
    tri = (r >= c) if inclusive else (r > c)
    return jnp.where(c >= tk, 1.0, jnp.where(tri, 1.0, 0.0)).astype(BF16)


def _suffix_sums(x, m):
    hi, lo = _split_bf16(x)
    return _dot(jnp.concatenate([hi, lo], axis=1), m)


def _sb_log_terms(z, mask, m_strict):
    ls = jnp.minimum(z, 0.0) - jnp.log(1.0 + jnp.exp(-jnp.abs(z)))
    lk = ls - z
    if mask is not None:
        lk = jnp.where(mask, lk, 0.0)
    return ls, _suffix_sums(lk, m_strict)


SB_PAIRS = 4


def _pair_lanes(a):
    return slice(a * LANES, (a + 1) * LANES)


def _sb_fwd(u, name, xchg=None):
    s = u.shape[0]
    tq = tk = min(128, s)
    pairs = SB_PAIRS
    width = pairs * LANES
    rows = 2 * pairs * tq
    x_arrs, x_gather = xchg if xchg else ((), True)
    n_x = len(x_arrs)
    grid = (4 // pairs, s // tq)

    def body(*refs):
        q_ref, k_ref, v_ref, g_ref = refs[:4]
        x_in, refs = refs[4:4 + n_x], refs[4 + n_x:]
        o_ref, y_ref = refs[:2]
        x_out, refs = refs[2:2 + n_x], refs[2 + n_x:]
        kbf, vst, z_s, ell_s, carry_s = refs[:5]
        x_sems = refs[5:]
        i = pl.program_id(1)
        if n_x:
            @pl.when((pl.program_id(0) == 0) & (i == 0))
            def _():
                _exchange_start(x_in, x_out, x_sems, x_gather)

        @pl.when(i == 0)
        def _():
            kbf[...] = k_ref[...].astype(BF16)
            first_s = _first_head_lanes(s, width)
            vf = v_ref[...]
            vst[0] = jnp.where(first_s, vf, 0.0).astype(BF16)
            vst[1] = jnp.where(first_s, 0.0, vf).astype(BF16)

        first = _first_head_lanes(tq)
        mask = _causal_mask(tq, tk, 2 * pairs)
        m_strict = _suffix_matrix(tk, False)
        qcat = jnp.concatenate([_stack_heads(q_ref[:, _pair_lanes(a)] * SB_SCALE, first) for a in range(pairs)],
                               axis=0)

        def scores(b):
            off = pl.multiple_of(jnp.maximum(b, 0) * tk, tk)
            z_s[...] = jnp.concatenate(
                [_dot_nt(qcat[a * 2 * tq:(a + 1) * 2 * tq], kbf[pl.ds(off, tk), _pair_lanes(a)])
                 for a in range(pairs)], axis=0)

        def log_weights(m):
            ls, cs = _sb_log_terms(z_s[...], m, m_strict)
            carry = carry_s[...]
            ell_s[...] = ls + cs[:, :tk] + carry
            carry_s[...] = carry + cs[:, tk:]

        def consume(b, accs, m):
            w = jnp.exp(ell_s[...])
            if m is not None:
                w = jnp.where(m, w, 0.0)
            wb = w.astype(BF16)
            off = pl.multiple_of(b * tk, tk)
            new = []
            for a in range(pairs):
                r0 = a * 2 * tq
                wcat = jnp.concatenate([wb[r0:r0 + tq], wb[r0 + tq:r0 + 2 * tq]], axis=1)
                vcat = jnp.concatenate([vst[0, pl.ds(off, tk), _pair_lanes(a)], vst[1, pl.ds(off, tk), _pair_lanes(a)]],
                                       axis=0)
                new.append(accs[a] + _dot(wcat, vcat))
            return tuple(new)

        carry_s[...] = jnp.zeros((rows, tk), F32)
        scores(i)
        log_weights(mask)
        scores(i - 1)
        accs = consume(i, tuple(jnp.zeros((tq, LANES), F32) for _ in range(pairs)), mask)
        log_weights(None)
        scores(i - 2)

        def step(n, accs):
            accs = consume(i - n, accs, None)
            log_weights(None)
            scores(i - n - 2)
            return accs

        accs = lax.fori_loop(1, i + 1, step, accs)
        o = jnp.concatenate(accs, axis=1)
        o_ref[...] = o
        gate = g_ref[...]
        y_ref[...] = (o * (gate * _sigmoid(gate))).astype(BF16)
        if n_x:
            @pl.when((pl.program_id(0) == grid[0] - 1) & (i == grid[1] - 1))
            def _():
                _exchange_wait(x_in, x_out, x_sems, x_gather)

    base = lambda cb: cb // pairs
    qblk = lambda cb: pl.BlockSpec((tq, width), lambda p, i: (i, base(cb) + p))
    full = lambda cb: pl.BlockSpec((s, width), lambda p, i: (0, base(cb) + p), pipeline_mode=pl.Buffered(1))
    state = pltpu.VMEM((rows, tk), F32)
    return _pcall(
        body, name=name, grid=grid,
        in_specs=[qblk(CB_SB_Q), full(CB_SB_K), full(CB_SB_V), qblk(CB_SB_G)] + [ANY_SPEC] * n_x,
        out_specs=(qblk(0), qblk(0)) + (ANY_SPEC,) * n_x,
        out_shape=(jax.ShapeDtypeStruct((s, WIDTH), F32), jax.ShapeDtypeStruct((s, WIDTH), BF16))
        + tuple(_exchange_out_shapes(x_arrs, x_gather)),
        scratch_shapes=[pltpu.VMEM((s, width), BF16), pltpu.VMEM((2, s, width), BF16), state, state, state]
        + (_exchange_sems(n_x) if n_x else []),
        compiler_params=_params(("arbitrary", "arbitrary")),
    )(u, u, u, u, *x_arrs)


def _merge_out_fwd(y_pool, y_conv, y_sb, u, wb_all, wo_all, x, g_post, layer, name):
    s = x.shape[0]
    tm = min(256, s)

    def body(yp, yc, ys, m0, m1, m2, wb_ref, wo_ref, x_ref, g_ref, out_ref, merged_ref, pre_ref):
        merged = jnp.zeros((tm, D_MODEL), F32)
        for n, (y_ref, m_ref) in enumerate(((yp, m0), (yc, m1), (ys, m2))):
            merged = merged + _sigmoid(m_ref[...]) * _dot(y_ref[...], wb_ref[n])
        mb = merged.astype(BF16)
        merged_ref[...] = mb
        pre = _dot(mb, wo_ref[...].reshape(D_MODEL, D_MODEL))
        pre_ref[...] = pre
        r = lax.rsqrt(jnp.mean(pre * pre, axis=-1, keepdims=True) + RMS_EPS)
        out_ref[...] = x_ref[...] + pre * r * g_ref[...]

    rows = lambda w: pl.BlockSpec((tm, w), lambda i: (i, 0))
    merge = lambda n: pl.BlockSpec((tm, D_MODEL), lambda i: (i, MERGE_BLOCK_1024 + n))
    return _pcall(
        body, name=name, grid=(s // tm,),
        in_specs=[rows(WIDTH), rows(WIDTH), rows(WIDTH), merge(0), merge(1), merge(2),
                  pl.BlockSpec((None, 3, WIDTH, D_MODEL), lambda i: (layer, 0, 0, 0)),
                  pl.BlockSpec((N_DEV, None, D_MODEL // N_DEV, D_MODEL), lambda i: (0, layer, 0, 0)),
                  rows(D_MODEL), pl.BlockSpec((1, D_MODEL), lambda i: (0, 0))],
        out_specs=(rows(D_MODEL), rows(D_MODEL), rows(D_MODEL)),
        out_shape=(jax.ShapeDtypeStruct((s, D_MODEL), F32), jax.ShapeDtypeStruct((s, D_MODEL), BF16),
                   jax.ShapeDtypeStruct((s, D_MODEL), F32)),
        compiler_params=_params(("arbitrary",)),
    )(y_pool, y_conv, y_sb, u, u, u, wb_all, wo_all, x, g_post)


def _loss_and_grad(y, target, name):
    s = y.shape[0]
    tm = min(512, s)

    def body(y_ref, t_ref, dy_ref, loss_ref, acc):
        i = pl.program_id(0)

        @pl.when(i == 0)
        def _():
            acc[...] = jnp.zeros_like(acc)
        err = y_ref[...] - t_ref[...]
        dy_ref[...] = err / D_MODEL
        acc[...] += jnp.sum(err * err, axis=0, keepdims=True)

        @pl.when(i == pl.num_programs(0) - 1)
        def _():
            total = jnp.sum(acc[...], axis=1, keepdims=True) * (0.5 / D_MODEL)
            loss_ref[...] = jnp.broadcast_to(total, (1, LANES))

    return _pcall(
        body, name=name, grid=(s // tm,),
        in_specs=[pl.BlockSpec((tm, D_MODEL), lambda i: (i, 0)), pl.BlockSpec((tm, D_MODEL), lambda i: (i, 0))],
        out_specs=(pl.BlockSpec((tm, D_MODEL), lambda i: (i, 0)), pl.BlockSpec((1, LANES), lambda i: (0, 0))),
        out_shape=(jax.ShapeDtypeStruct((s, D_MODEL), F32), jax.ShapeDtypeStruct((1, LANES), F32)),
        scratch_shapes=[pltpu.VMEM((1, D_MODEL), F32)],
        compiler_params=_params(("arbitrary",)),
    )(y, target)


def _out_proj_bwd(dy, pre, g_post, merged, wo_all, layer, name):
    s = dy.shape[0]
    tm = min(256, s)
    n_tiles = s // tm

    def body(dy_ref, pre_ref, g_ref, mg_ref, wo_ref, dm_ref, dwo_ref, dg_ref, acc):
        i = pl.program_id(0)

        @pl.when(i == 0)
        def _():
            acc[...] = jnp.zeros_like(acc)
            dg_ref[...] = jnp.zeros_like(dg_ref)
        dyv, pre_v = dy_ref[...], pre_ref[...]
        r = lax.rsqrt(jnp.mean(pre_v * pre_v, axis=-1, keepdims=True) + RMS_EPS)
        dg_ref[...] += jnp.sum(dyv * pre_v * r, axis=0, keepdims=True)
        a = dyv * g_ref[...]
        dpre = r * a - pre_v * (r * r * r) * jnp.mean(a * pre_v, axis=-1, keepdims=True)
        db = dpre.astype(BF16)
        acc[...] += _dot_tn(mg_ref[...], db)
        dm_ref[...] = _dot_nt(db, wo_ref[...].reshape(D_MODEL, D_MODEL))

        @pl.when(i == n_tiles - 1)
        def _():
            dwo_ref[...] = acc[...].astype(BF16)

    rows = lambda: pl.BlockSpec((tm, D_MODEL), lambda i: (i, 0))
    return _pcall(
        body, name=name, grid=(n_tiles,),
        in_specs=[rows(), rows(), pl.BlockSpec((1, D_MODEL), lambda i: (0, 0)), rows(),
                  pl.BlockSpec((N_DEV, None, D_MODEL // N_DEV, D_MODEL), lambda i: (0, layer, 0, 0))],
        out_specs=(rows(), pl.BlockSpec((D_MODEL, D_MODEL), lambda i: (0, 0)),
                   pl.BlockSpec((1, D_MODEL), lambda i: (0, 0))),
        out_shape=(jax.ShapeDtypeStruct((s, D_MODEL), F32), jax.ShapeDtypeStruct((D_MODEL, D_MODEL), BF16),
                   jax.ShapeDtypeStruct((1, D_MODEL), F32)),
        scratch_shapes=[pltpu.VMEM((D_MODEL, D_MODEL), F32)],
        compiler_params=_params(("arbitrary",)),
    )(dy, pre, g_post, merged, wo_all)


def _merge_bwd(dmerged, y_pool, y_conv, y_sb, u, wb_all, layer, name):
    s = dmerged.shape[0]
    tm = min(256, s)
    n_tiles = s // tm
    cols = D_MODEL // N_DEV

    def body(dm_ref, yp, yc, ys, m0, m1, m2, wb_ref, dum_ref, dyp, dyc, dys, dwb_ref, acc):
        i = pl.program_id(0)

        @pl.when(i == 0)
        def _():
            acc[...] = jnp.zeros_like(acc)
        dm = dm_ref[...]
        for n, (y_ref, m_ref, dy_ref) in enumerate(((yp, m0, dyp), (yc, m1, dyc), (ys, m2, dys))):
            yv = y_ref[...]
            wb = wb_ref[n]
            gate = _sigmoid(m_ref[...])
            proj = _dot(yv, wb)
            dum_ref[:, n * D_MODEL:(n + 1) * D_MODEL] = (dm * proj * gate * (1.0 - gate)).astype(BF16)
            dproj = (dm * gate).astype(BF16)
            acc[n] += _dot_tn(yv, dproj)
            dy_ref[...] = _dot_nt(dproj, wb)

        @pl.when(i == n_tiles - 1)
        def _():
            for j in range(N_DEV):
                for n in range(3):
                    dwb_ref[j, n] = acc[n, :, j * cols:(j + 1) * cols].astype(BF16)

    rows = lambda w: pl.BlockSpec((tm, w), lambda i: (i, 0))
    merge = lambda n: pl.BlockSpec((tm, D_MODEL), lambda i: (i, MERGE_BLOCK_1024 + n))
    return _pcall(
        body, name=name, grid=(n_tiles,),
        in_specs=[rows(D_MODEL), rows(WIDTH), rows(WIDTH), rows(WIDTH), merge(0), merge(1), merge(2),
                  pl.BlockSpec((None, 3, WIDTH, D_MODEL), lambda i: (layer, 0, 0, 0))],
        out_specs=(rows(3 * D_MODEL), rows(WIDTH), rows(WIDTH), rows(WIDTH),
                   pl.BlockSpec((N_DEV, 3, WIDTH, cols), lambda i: (0, 0, 0, 0))),
        out_shape=(jax.ShapeDtypeStruct((s, 3 * D_MODEL), BF16),
                   jax.ShapeDtypeStruct((s, WIDTH), F32), jax.ShapeDtypeStruct((s, WIDTH), F32),
                   jax.ShapeDtypeStruct((s, WIDTH), F32),
                   jax.ShapeDtypeStruct((N_DEV, 3, WIDTH, cols), BF16)),
        scratch_shapes=[pltpu.VMEM((3, WIDTH, D_MODEL), F32)],
        compiler_params=_params(("arbitrary",)),
    )(dmerged, y_pool, y_conv, y_sb, u, u, u, wb_all)


def _sb_bwd(u, o, dys, name, xchg=None):
    s = u.shape[0]
    tq = tk = min(128, s)
    pairs = SB_PAIRS
    width = pairs * LANES
    rows = 2 * pairs * tq
    pair_rows = lambda a: slice(a * 2 * tq, (a + 1) * 2 * tq)

    x_arrs, x_gather = xchg if xchg else ((), True)
    n_x = len(x_arrs)
    grid = (4 // pairs, s // tq)

    def body(*refs):
        q_ref, k_ref, v_ref, g_ref, o_ref, dys_ref = refs[:6]
        x_in, refs = refs[6:6 + n_x], refs[6 + n_x:]
        dq_ref, dk_ref, dv_ref, dg_ref = refs[:4]
        x_out, refs = refs[4:4 + n_x], refs[4 + n_x:]
        kbf, vbf, kst, z_s, ell_s, ls_s, cl_s, wb_s, g_s, bef_s, cg_s, beta_s = refs[:12]
        x_sems = refs[12:]
        i = pl.program_id(1)
        if n_x:
            @pl.when((pl.program_id(0) == 0) & (i == 0))
            def _():
                _exchange_start(x_in, x_out, x_sems, x_gather)

        @pl.when(i == 0)
        def _():
            dk_ref[...] = jnp.zeros_like(dk_ref)
            dv_ref[...] = jnp.zeros_like(dv_ref)
            kf = k_ref[...]
            kbf[...] = kf.astype(BF16)
            vbf[...] = v_ref[...].astype(BF16)
            first_s = _first_head_lanes(s, width)
            kst[0] = jnp.where(first_s, kf, 0.0).astype(BF16)
            kst[1] = jnp.where(first_s, 0.0, kf).astype(BF16)

        first = _first_head_lanes(tq)
        mask = _causal_mask(tq, tk, 2 * pairs)
        m_strict = _suffix_matrix(tk, False)
        m_incl = _suffix_matrix(tk, True)

        gate = g_ref[...]
        sg = _sigmoid(gate)
        dy = dys_ref[...]
        ov = o_ref[...]
        dg_ref[...] = (dy * ov * (sg * (1.0 + gate * (1.0 - sg)))).astype(BF16)
        do = (dy * (gate * sg)).astype(BF16)
        prod = do.astype(F32) * ov
        row_sum = lambda v: jnp.broadcast_to(jnp.sum(v, axis=1, keepdims=True), (tq, tk))
        dsum, docat, qcat = [], [], []
        for a in range(pairs):
            pa = prod[:, _pair_lanes(a)]
            dsum += [row_sum(jnp.where(first, pa, 0.0)), row_sum(jnp.where(first, 0.0, pa))]
            docat.append(_stack_heads(do[:, _pair_lanes(a)], first))
            qcat.append(_stack_heads(q_ref[:, _pair_lanes(a)] * SB_SCALE, first))
        dsum = jnp.concatenate(dsum, axis=0)

        def block_start(b):
            return pl.multiple_of(jnp.maximum(b, 0) * tk, tk)

        def scores(b):
            off = block_start(b)
            z_s[...] = jnp.concatenate([_dot_nt(qcat[a], kbf[pl.ds(off, tk), _pair_lanes(a)]) for a in range(pairs)],
                                       axis=0)

        def log_weights(m):
            ls, cs = _sb_log_terms(z_s[...], m, m_strict)
            cl = cl_s[...]
            ell_s[...] = ls + cs[:, :tk] + cl
            cl_s[...] = cl + cs[:, tk:]
            ls_s[...] = ls

        def weights(b, m):
            off = block_start(b)
            dwt = jnp.concatenate([_dot_nt(docat[a], vbf[pl.ds(off, tk), _pair_lanes(a)]) for a in range(pairs)],
                                  axis=0)
            w = jnp.exp(ell_s[...])
            if m is not None:
                w = jnp.where(m, w, 0.0)
            wb = w.astype(BF16)
            g = dwt * wb.astype(F32)
            gs = _suffix_sums(g, m_incl)
            cg = cg_s[...]
            beta = jnp.exp(ls_s[...])
            wb_s[...] = wb
            beta_s[...] = beta
            g_s[...] = g * (1.0 - beta)
            bef_s[...] = gs[:, :tk] + cg
            cg_s[...] = cg + gs[:, tk:]

        def grads(b, dqs, m):
            dz = g_s[...] - beta_s[...] * (dsum - bef_s[...])
            if m is not None:
                dz = jnp.where(m, dz, 0.0)
            dzb = dz.astype(BF16)
            wb = wb_s[...]
            off = pl.multiple_of(b * tk, tk)
            new = []
            for a in range(pairs):
                r0 = a * 2 * tq
                kcat = jnp.concatenate([kst[0, pl.ds(off, tk), _pair_lanes(a)], kst[1, pl.ds(off, tk), _pair_lanes(a)]],
                                       axis=0)
                new.append(dqs[a] + _dot(jnp.concatenate([dzb[r0:r0 + tq], dzb[r0 + tq:r0 + 2 * tq]], axis=1), kcat))
                dk_ref[pl.ds(off, tk), _pair_lanes(a)] += _dot_tn(dzb[pair_rows(a)], qcat[a])
                dv_ref[pl.ds(off, tk), _pair_lanes(a)] += _dot_tn(wb[pair_rows(a)], docat[a])
            return tuple(new)

        zero = jnp.zeros((rows, tk), F32)
        cl_s[...] = zero
        cg_s[...] = zero
        scores(i)
        log_weights(mask)
        scores(i - 1)
        weights(i, mask)
        log_weights(None)
        scores(i - 2)
        dqs = grads(i, tuple(jnp.zeros((tq, LANES), F32) for _ in range(pairs)), mask)
        weights(i - 1, None)
        log_weights(None)
        scores(i - 3)

        def step(n, dqs):
            dqs = grads(i - n, dqs, None)
            weights(i - n - 1, None)
            log_weights(None)
            scores(i - n - 3)
            return dqs

        dqs = lax.fori_loop(1, i + 1, step, dqs)
        dq_ref[...] = (jnp.concatenate(dqs, axis=1) * SB_SCALE).astype(BF16)
        if n_x:
            @pl.when((pl.program_id(0) == grid[0] - 1) & (i == grid[1] - 1))
            def _():
                _exchange_wait(x_in, x_out, x_sems, x_gather)

    base = lambda cb: cb // pairs
    qblk = lambda cb: pl.BlockSpec((tq, width), lambda p, i: (i, base(cb) + p))
    full = lambda cb: pl.BlockSpec((s, width), lambda p, i: (0, base(cb) + p), pipeline_mode=pl.Buffered(1))
    state = pltpu.VMEM((rows, tk), F32)
    return _pcall(
        body, name=name, grid=grid,
        in_specs=[qblk(CB_SB_Q), full(CB_SB_K), full(CB_SB_V), qblk(CB_SB_G), qblk(0), qblk(0)] + [ANY_SPEC] * n_x,
        out_specs=(qblk(0), full(0), full(0), qblk(0)) + (ANY_SPEC,) * n_x,
        out_shape=(jax.ShapeDtypeStruct((s, WIDTH), BF16), jax.ShapeDtypeStruct((s, WIDTH), F32),
                   jax.ShapeDtypeStruct((s, WIDTH), F32), jax.ShapeDtypeStruct((s, WIDTH), BF16))
        + tuple(_exchange_out_shapes(x_arrs, x_gather)),
        scratch_shapes=[pltpu.VMEM((s, width), BF16), pltpu.VMEM((s, width), BF16), pltpu.VMEM((2, s, width), BF16),
                        state, state, state, state, pltpu.VMEM((rows, tk), BF16),
                        state, state, state, state] + (_exchange_sems(n_x) if n_x else []),
        compiler_params=_params(("arbitrary", "arbitrary")),
    )(u, u, u, u, o, dys, *x_arrs)


def _conv_bwd(u, conv_w, conv_b, dyc, name):
    s = u.shape[0]
    t = min(256, s)
    n_tiles = s // t

    def body(xc_ref, gb_ref, gc_ref, cg_ref, w_ref, b_ref, dy_ref,
             dxc_ref, dgb_ref, dgc_ref, dcg_ref, dw_ref, db_ref, zs, ds):
        zs[0:CONV_HALO, :] = jnp.zeros((CONV_HALO, LANES), F32)
        zs[CONV_HALO:, :] = gc_ref[...] * xc_ref[...]
        ds[s:, :] = jnp.zeros((CONV_HALO, LANES), F32)
        w0, w1, w2 = w_ref[0:1, :], w_ref[1:2, :], w_ref[2:3, :]
        bias = b_ref[...]

        def first(i, sums):
            t0 = pl.multiple_of(i * t, t)
            z0, z1, z2 = _conv_taps(zs, t0, t)
            pre = w0 * z2 + w1 * z1 + w2 * z0 + bias
            gate = cg_ref[pl.ds(t0, t), :]
            sg = _sigmoid(gate)
            gb = gb_ref[pl.ds(t0, t), :]
            dy = dy_ref[pl.ds(t0, t), :]
            dcg_ref[pl.ds(t0, t), :] = (dy * gb * pre * (sg * (1.0 + gate * (1.0 - sg)))).astype(BF16)
            dgb_ref[pl.ds(t0, t), :] = (dy * pre * (gate * sg)).astype(BF16)
            dc = dy * gb * (gate * sg)
            ds[pl.ds(t0, t), :] = dc
            red = lambda v: jnp.sum(v, axis=0, keepdims=True)
            return (sums[0] + red(dc * z2), sums[1] + red(dc * z1), sums[2] + red(dc * z0), sums[3] + red(dc))

        zrow = jnp.zeros((1, LANES), F32)
        sw0, sw1, sw2, sb = lax.fori_loop(0, n_tiles, first, (zrow, zrow, zrow, zrow))
        dw_ref[0:1, :] = sw0
        dw_ref[1:2, :] = sw1
        dw_ref[2:3, :] = sw2
        db_ref[...] = sb

        def second(i, carry):
            t0 = pl.multiple_of(i * t, t)
            ext = ds[pl.ds(t0, t + CONV_HALO), :]
            n = t + CONV_HALO
            d0 = ext[:t, :]
            d1 = pltpu.roll(ext, n - 1, 0)[:t, :]
            d2 = pltpu.roll(ext, n - 2, 0)[:t, :]
            dz = w2 * d0 + w1 * d1 + w0 * d2
            dgc_ref[pl.ds(t0, t), :] = (dz * xc_ref[pl.ds(t0, t), :]).astype(BF16)
            dxc_ref[pl.ds(t0, t), :] = (dz * gc_ref[pl.ds(t0, t), :]).astype(BF16)
            return carry

        lax.fori_loop(0, n_tiles, second, 0)

    col = lambda base: pl.BlockSpec((s, LANES), lambda j: (0, base + j))
    dcol = jax.ShapeDtypeStruct((s, WIDTH), BF16)
    return _pcall(
        body, name=name, grid=(4,),
        in_specs=[col(CB_CONV_X), col(CB_CONV_GB), col(CB_CONV_GC), col(CB_CONV_G),
                  pl.BlockSpec((3, LANES), lambda j: (0, j)), pl.BlockSpec((1, LANES), lambda j: (0, j)), col(0)],
        out_specs=(col(0), col(0), col(0), col(0),
                   pl.BlockSpec((3, LANES), lambda j: (0, j)), pl.BlockSpec((1, LANES), lambda j: (0, j))),
        out_shape=(dcol, dcol, dcol, dcol,
                   jax.ShapeDtypeStruct((3, WIDTH), F32), jax.ShapeDtypeStruct((1, WIDTH), F32)),
        scratch_shapes=[pltpu.VMEM((CONV_HALO + s, LANES), F32), pltpu.VMEM((s + CONV_HALO, LANES), F32)],
        compiler_params=_params(("arbitrary",)),
    )(u, u, u, u, conv_w, conv_b, dyc)


def _pool_bwd(u, pool_w, pool_scale, dyp, name):
    s = u.shape[0]
    t = min(256, s)
    n_tiles = s // t

    def body(pv_ref, pg_ref, w_ref, sc_ref, dy_ref, dpv_ref, dpg_ref, dw_ref, dsc_ref, vs, es, dps):
        grp = pl.program_id(0)
        vs[0:POOL_HALO, :] = jnp.zeros((POOL_HALO, LANES), F32)
        vs[POOL_HALO:, :] = pv_ref[...]
        es[s:, :] = jnp.zeros((POOL_HALO, LANES), F32)
        wb = w_ref[...].astype(BF16)
        scale = sc_ref[...]

        def first(i, sums):
            dw, dsc = sums
            t0 = pl.multiple_of(i * t, t)
            win, v = _pool_window(vs, t0, t, grp)
            cnt = _pool_count(t0, t, grp)
            pb = (win / cnt - v).astype(BF16)
            mixed = _dot(pb, wb)
            gate = pg_ref[pl.ds(t0, t), :]
            sg = _sigmoid(gate)
            dy = dy_ref[pl.ds(t0, t), :]
            dpg_ref[pl.ds(t0, t), :] = (dy * (mixed * scale) * (sg * (1.0 + gate * (1.0 - sg)))).astype(BF16)
            dms = dy * (gate * sg)
            dsc = dsc + jnp.sum(dms * mixed, axis=0, keepdims=True)
            dmb = (dms * scale).astype(BF16)
            dw = dw + _dot_tn(pb, dmb)
            dpooled = _dot_nt(dmb, wb)
            dps[pl.ds(t0, t), :] = dpooled
            es[pl.ds(t0, t), :] = dpooled / cnt
            return dw, dsc

        dw, dsc = lax.fori_loop(0, n_tiles, first, (jnp.zeros((LANES, LANES), F32), jnp.zeros((1, LANES), F32)))
        dw_ref[...] = dw
        dsc_ref[...] = dsc

        def second(i, carry):
            t0 = pl.multiple_of(i * t, t)
            ext = es[pl.ds(t0, t + POOL_HALO), :]
            n = t + POOL_HALO
            f2 = ext + pltpu.roll(ext, n - 1, 0)
            f4 = f2 + pltpu.roll(f2, n - 2, 0)
            f8 = f4 + pltpu.roll(f4, n - 4, 0)
            f16 = f8 + pltpu.roll(f8, n - 8, 0)
            sel = jnp.where(grp == 0, f2, jnp.where(grp == 1, f4, jnp.where(grp == 2, f8, f16)))
            dpv_ref[pl.ds(t0, t), :] = (sel[:t, :] - dps[pl.ds(t0, t), :]).astype(BF16)
            return carry

        lax.fori_loop(0, n_tiles, second, 0)

    col = lambda base: pl.BlockSpec((s, LANES), lambda g: (0, base + g))
    dcol = jax.ShapeDtypeStruct((s, WIDTH), BF16)
    return _pcall(
        body, name=name, grid=(4,),
        in_specs=[col(CB_POOL_V), col(CB_POOL_G), pl.BlockSpec((None, LANES, LANES), lambda g: (g, 0, 0)),
                  pl.BlockSpec((1, LANES), lambda g: (0, g)), col(0)],
        out_specs=(col(0), col(0), pl.BlockSpec((None, LANES, LANES), lambda g: (g, 0, 0)),
                   pl.BlockSpec((1, LANES), lambda g: (0, g))),
        out_shape=(dcol, dcol, jax.ShapeDtypeStruct((4, LANES, LANES), F32), jax.ShapeDtypeStruct((1, WIDTH), F32)),
        scratch_shapes=[pltpu.VMEM((POOL_HALO + s, LANES), F32), pltpu.VMEM((s + POOL_HALO, LANES), F32),
                        pltpu.VMEM((s, LANES), F32)],
        compiler_params=_params(("arbitrary",)),
    )(u, u, pool_w, pool_scale, dyp)


def _in_proj_bwd_x(du, w_all, x, g_pre, dy, name, xchg=None):
    s = x.shape[0]
    tm = min(512, s)
    x_arrs, x_gather = xchg if xchg else ((), True)
    n_x = len(x_arrs)
    grid = (s // tm, N_DEV)

    def body(*refs):
        du_ref, w_ref, x_ref, g_ref, dy_ref = refs[:5]
        x_in, refs = refs[5:5 + n_x], refs[5 + n_x:]
        dx_ref, dg_ref = refs[:2]
        x_out, refs = refs[2:2 + n_x], refs[2 + n_x:]
        acc, x_sems = refs[0], refs[1:]
        i, k = pl.program_id(0), pl.program_id(1)
        if n_x:
            @pl.when((i == 0) & (k == 0))
            def _():
                _exchange_start(x_in, x_out, x_sems, x_gather)

        @pl.when(k == 0)
        def _():
            acc[...] = jnp.zeros_like(acc)

        @pl.when((k == 0) & (i == 0))
        def _():
            dg_ref[...] = jnp.zeros_like(dg_ref)
        acc[...] += _dot_nt(du_ref[...], w_ref[...])

        @pl.when(k == N_DEV - 1)
        def _():
            dh, xv = acc[...], x_ref[...]
            r = lax.rsqrt(jnp.mean(xv * xv, axis=-1, keepdims=True) + RMS_EPS)
            dg_ref[...] += jnp.sum(dh * xv * r, axis=0, keepdims=True)
            a = dh * g_ref[...]
            dx_ref[...] = dy_ref[...] + r * a - xv * (r * r * r) * jnp.mean(a * xv, axis=-1, keepdims=True)

        if n_x:
            @pl.when((i == grid[0] - 1) & (k == grid[1] - 1))
            def _():
                _exchange_wait(x_in, x_out, x_sems, x_gather)

    rows = lambda: pl.BlockSpec((tm, D_MODEL), lambda i, k: (i, 0))
    vec = lambda: pl.BlockSpec((1, D_MODEL), lambda i, k: (0, 0))
    return _pcall(
        body, name=name, grid=grid,
        in_specs=[pl.BlockSpec((tm, COLS_PER_DEV), lambda i, k: (i, k)),
                  pl.BlockSpec((None, D_MODEL, COLS_PER_DEV), lambda i, k: (k, 0, 0)),
                  rows(), vec(), rows()] + [ANY_SPEC] * n_x,
        out_specs=(rows(), vec()) + (ANY_SPEC,) * n_x,
        out_shape=(jax.ShapeDtypeStruct((s, D_MODEL), F32), jax.ShapeDtypeStruct((1, D_MODEL), F32))
        + tuple(_exchange_out_shapes(x_arrs, x_gather)),
        scratch_shapes=[pltpu.VMEM((tm, D_MODEL), F32)] + (_exchange_sems(n_x) if n_x else []),
        compiler_params=_params(("arbitrary", "arbitrary")),
    )(du, w_all, x, g_pre, dy, *x_arrs)


def _in_proj_bwd_w(h, du, name):
    s = h.shape[0]
    tk = min(512, s)
    n_k = s // tk

    def body(h_ref, du_ref, out_ref, acc):
        k = pl.program_id(1)

        @pl.when(k == 0)
        def _():
            acc[...] = jnp.zeros_like(acc)
        acc[...] += _dot_tn(h_ref[...], du_ref[...])

        @pl.when(k == n_k - 1)
        def _():
            out_ref[...] = acc[...].astype(BF16)

    return _pcall(
        body, name=name, grid=(N_DEV, n_k),
        in_specs=[pl.BlockSpec((tk, D_MODEL), lambda j, k: (k, 0)),
                  pl.BlockSpec((tk, COLS_PER_DEV), lambda j, k: (k, j))],
        out_specs=pl.BlockSpec((None, D_MODEL, COLS_PER_DEV), lambda j, k: (j, 0, 0)),
        out_shape=jax.ShapeDtypeStruct((N_DEV, D_MODEL, COLS_PER_DEV), BF16),
        scratch_shapes=[pltpu.VMEM((D_MODEL, COLS_PER_DEV), F32)],
        compiler_params=_params(("parallel", "arbitrary")),
    )(h, du)


def _adamw_math(g, w, m, v):
    m_new = ADAM_B1 * m + (1.0 - ADAM_B1) * g
    v_new = ADAM_B2 * v + (1.0 - ADAM_B2) * (g * g)
    m_hat = m_new / (1.0 - ADAM_B1 ** ADAM_STEP)
    v_hat = v_new / (1.0 - ADAM_B2 ** ADAM_STEP)
    delta = -ADAM_LR * (m_hat / (jnp.sqrt(v_hat) + ADAM_EPS) + ADAM_WD * w)
    return delta, m_new, v_new


def _sum_partials(p_ref):
    total = p_ref[0].astype(F32)
    for d in range(1, N_DEV):
        total = total + p_ref[d].astype(F32)
    return total


def _adamw_layers(parts0, parts1, w, m, v, name):
    _, r, c = w.shape
    tr = min(128, r)
    n_r = r // tr

    def body(p0_ref, p1_ref, w_ref, m_ref, v_ref, g_ref, d_ref, mo_ref, vo_ref):
        layer = pl.program_id(0)

        @pl.when(layer == 0)
        def _():
            g_ref[...] = _sum_partials(p0_ref)

        @pl.when(layer == 1)
        def _():
            g_ref[...] = _sum_partials(p1_ref)
        d_ref[...], mo_ref[...], vo_ref[...] = _adamw_math(g_ref[...], w_ref[...], m_ref[...], v_ref[...])

    part = lambda which: pl.BlockSpec((N_DEV, tr, c), lambda l, i: (0, jnp.where(l == which, i, 0), 0))
    par = lambda: pl.BlockSpec((None, tr, c), lambda l, i: (l, i, 0))
    out = jax.ShapeDtypeStruct(w.shape, F32)
    return _pcall(
        body, name=name, grid=(2, n_r),
        in_specs=[part(0), part(1), par(), par(), par()],
        out_specs=(par(), par(), par(), par()),
        out_shape=(out, out, out, out),
        compiler_params=_params(("arbitrary", "arbitrary")),
    )(parts0, parts1, w, m, v)


def _adamw_small(parts, w, m, v, name):
    def body(p_ref, w_ref, m_ref, v_ref, g_ref, d_ref, mo_ref, vo_ref):
        g = _sum_partials(p_ref)
        g_ref[...] = g
        d_ref[...], mo_ref[...], vo_ref[...] = _adamw_math(g, w_ref[...], m_ref[...], v_ref[...])

    out = jax.ShapeDtypeStruct(w.shape, F32)
    return _pcall(body, name=name, out_shape=(out, out, out, out), compiler_params=_params())(parts, w, m, v)


def _adamw_plain(g, w, m, v, name):
    def body(g_ref, w_ref, m_ref, v_ref, d_ref, mo_ref, vo_ref):
        d_ref[...], mo_ref[...], vo_ref[...] = _adamw_math(g_ref[...], w_ref[...], m_ref[...], v_ref[...])

    out = jax.ShapeDtypeStruct(w.shape, F32)
    return _pcall(body, name=name, out_shape=(out, out, out), compiler_params=_params())(g, w, m, v)


def _rows128(a):
    return a.reshape(-1, LANES)


SMALL_NAMES = ("pre_norm_g", "pool_w", "pool_scale", "conv_w", "conv_b", "post_norm_g")


def kernel(x, pre_norm_g, w_in, pool_w, pool_scale, conv_w, conv_b, w_branch, w_out, post_norm_g, loss_target, m_pre_norm_g, m_w_in, m_pool_w, m_pool_scale, m_conv_w, m_conv_b, m_w_branch, m_w_out, m_post_norm_g, v_pre_norm_g, v_w_in, v_pool_w, v_pool_scale, v_conv_w, v_conv_b, v_w_branch, v_w_out, v_post_norm_g):
    s = x.shape[1]
    me = 4 * lax.axis_index("x") + 2 * lax.axis_index("y") + lax.axis_index("c")
    x0 = x[0]
    target = loss_target[0]
    conv_cols = conv_w.shape[-1]

    conv_w_pad = jnp.pad(conv_w.reshape(2 * 3, conv_cols), ((0, 2), (0, LANES - conv_cols)))
    w_in_all = [None, None]
    w_in_all[0], cw_g = _exchange([w_in[0].astype(BF16), conv_w_pad], True, "gather_w_in_0")
    conv_w_full = cw_g[:, :6, :conv_cols].reshape(N_DEV, 2, 3, conv_cols).transpose(1, 2, 0, 3).reshape(2, 3, WIDTH)
    later_weights = ([w_in[1].astype(BF16), w_branch.astype(BF16), w_out.astype(BF16)], True)

    saved = []
    xin = x0
    for l in range(2):
        u, h = _in_proj_fwd(xin, pre_norm_g[l:l + 1], w_in_all[l], f"in_proj_fwd_{l}")
        y_pool = _pool_fwd(u, pool_w[l], pool_scale[l:l + 1], f"pool_fwd_{l}")
        y_conv = _conv_fwd(u, conv_w_full[l], conv_b[l:l + 1], f"conv_fwd_{l}")
        if l == 0:
            o_sb, y_sb, w_in_all[1], wb_g, wo_all = _sb_fwd(u, f"sb_fwd_{l}", later_weights)
            wb_all = wb_g.transpose(1, 2, 3, 0, 4).reshape(2, 3, WIDTH, D_MODEL)
        else:
            o_sb, y_sb = _sb_fwd(u, f"sb_fwd_{l}")
        xout, merged, pre = _merge_out_fwd(y_pool, y_conv, y_sb, u, wb_all, wo_all, xin, post_norm_g[l:l + 1], l,
                                           f"merge_out_fwd_{l}")
        saved.append((xin, u, h, y_pool, y_conv, y_sb, o_sb, merged, pre))
        xin = xout

    dy, loss_row = _loss_and_grad(xin, target, "loss")

    small = [None, None]
    recv = [None, None]
    ready = []
    for l in (1, 0):
        xl, u, h, y_pool, y_conv, y_sb, o_sb, merged, pre = saved[l]
        dmerged, dwo, dg_post = _out_proj_bwd(dy, pre, post_norm_g[l:l + 1], merged, wo_all, l, f"out_proj_bwd_{l}")
        du_merge, dyp, dyc, dys, dwb = _merge_bwd(dmerged, y_pool, y_conv, y_sb, u, wb_all, l, f"merge_bwd_{l}")
        dwb = dwb.reshape(N_DEV, 3 * WIDTH, D_MODEL // N_DEV)
        dwo = dwo.reshape(N_DEV, D_MODEL // N_DEV, D_MODEL)
        if l == 1:
            dq, dk, dv, dsg = _sb_bwd(u, o_sb, dys, f"sb_bwd_{l}")
        else:
            dq, dk, dv, dsg, *got = _sb_bwd(u, o_sb, dys, f"sb_bwd_{l}", (ready + [dwb, dwo], False))
            recv[1] = got[:3]
        dxc, dgb, dgc, dcg, dcw, dcb = _conv_bwd(u, conv_w_full[l], conv_b[l:l + 1], dyc, f"conv_bwd_{l}")
        dpv, dpg, dpw, dps = _pool_bwd(u, pool_w[l], pool_scale[l:l + 1], dyp, f"pool_bwd_{l}")
        du = jnp.concatenate([dpv, dpg, dxc, dgb, dgc, dcg, dq, dk.astype(BF16), dv.astype(BF16), dsg, du_merge],
                             axis=1)
        dwi = _in_proj_bwd_w(h, du, f"in_proj_bwd_w_{l}")
        if l == 1:
            ready = [dwi, dwb, dwo]
            dx, dg_pre = _in_proj_bwd_x(du, w_in_all[l], xl, pre_norm_g[l:l + 1], dy, f"in_proj_bwd_x_{l}")
        else:
            dx, dg_pre, got_dwi = _in_proj_bwd_x(du, w_in_all[l], xl, pre_norm_g[l:l + 1], dy, f"in_proj_bwd_x_{l}",
                                                 ([dwi], False))
            recv[0] = [got_dwi] + got[3:]
        small[l] = dict(pre_norm_g=dg_pre, pool_w=dpw, pool_scale=dps, conv_w=dcw, conv_b=dcb, post_norm_g=dg_post)
        dy = dx
    grad_x = dy[None]

    packed = jnp.concatenate(
        [_rows128(jnp.stack([small[0][n], small[1][n]])) for n in SMALL_NAMES]
        + [jnp.pad(loss_row, ((0, 7), (0, 0)))], axis=0)
    (packed_all,) = _exchange([packed], True, "gather_small")
    sizes = dict(pre_norm_g=16, pool_w=1024, pool_scale=8, conv_w=24, conv_b=8, post_norm_g=16)
    n_rows = sum(sizes.values())
    loss = jnp.sum(packed_all[:, n_rows, 0])

    given = dict(pre_norm_g=(pre_norm_g, m_pre_norm_g, v_pre_norm_g), pool_w=(pool_w, m_pool_w, v_pool_w),
                 pool_scale=(pool_scale, m_pool_scale, v_pool_scale), conv_b=(conv_b, m_conv_b, v_conv_b),
                 post_norm_g=(post_norm_g, m_post_norm_g, v_post_norm_g))
    zeros_cw = jnp.zeros((sizes["conv_w"], LANES), F32)
    pack3 = [jnp.concatenate([zeros_cw if n == "conv_w" else _rows128(given[n][k]) for n in SMALL_NAMES], axis=0)
             for k in range(3)]
    sg, sd, sm, sv = _adamw_small(packed_all[:, :n_rows], pack3[0], pack3[1], pack3[2], "adamw_small")

    def unpack(buf, name, shape):
        start = 0
        for n in SMALL_NAMES:
            if n == name:
                return buf[start:start + sizes[n]].reshape(shape)
            start += sizes[n]

    out = {}
    for n in ("pre_norm_g", "pool_w", "pool_scale", "conv_b", "post_norm_g"):
        shape = given[n][0].shape
        out[n] = tuple(unpack(b, n, shape) for b in (sg, sd, sm, sv))
    g_cw = lax.dynamic_slice_in_dim(unpack(sg, "conv_w", (2, 3, WIDTH)), me * conv_cols, conv_cols, axis=2)
    cw2 = lambda a: a.reshape(6, conv_cols)
    d_cw, m_cw, v_cw = _adamw_plain(cw2(g_cw), cw2(conv_w), cw2(m_conv_w), cw2(v_conv_w), "adamw_conv_w")
    out["conv_w"] = (g_cw,) + tuple(a.reshape(2, 3, conv_cols) for a in (d_cw, m_cw, v_cw))

    out["w_in"] = _adamw_layers(recv[0][0], recv[1][0], w_in, m_w_in, v_w_in, "adamw_w_in")
    cols = D_MODEL // N_DEV
    wb3 = lambda a: a.reshape(2, 3 * WIDTH, cols)
    out["w_branch"] = tuple(a.reshape(2, 3, WIDTH, cols) for a in _adamw_layers(
        recv[0][1], recv[1][1], wb3(w_branch), wb3(m_w_branch), wb3(v_w_branch), "adamw_w_branch"))
    out["w_out"] = _adamw_layers(recv[0][2], recv[1][2], w_out, m_w_out, v_w_out, "adamw_w_out")

    order = ("pre_norm_g", "w_in", "pool_w", "pool_scale", "conv_w", "conv_b", "w_branch", "w_out", "post_norm_g")
    return (loss, grad_x) + tuple(out[n][k] for k in range(4) for n in order)
```

```python
import functools

import jax
import jax.numpy as jnp
from jax import lax
from jax.experimental import pallas as pl
from jax.experimental.pallas import tpu as pltpu

F32 = jnp.float32
BF16 = jnp.bfloat16

N_DEV = 8
D_MODEL = 1024
WIDTH = 512
N_IN = 8192
COLS_PER_DEV = N_IN // N_DEV
HEAD_DIM = 64
LANES = 128
SB_SCALE = HEAD_DIM ** -0.5
RMS_EPS = 1e-6
POOL_HALO = 16
CONV_HALO = 8
ADAM_LR, ADAM_B1, ADAM_B2, ADAM_EPS, ADAM_WD, ADAM_STEP = 0.001, 0.9, 0.999, 1e-08, 0.01, 10
VMEM_LIMIT = 60 * 1024 * 1024

CB_POOL_V, CB_POOL_G = 0, 4
CB_CONV_X, CB_CONV_GB, CB_CONV_GC, CB_CONV_G = 8, 12, 16, 20
CB_SB_Q, CB_SB_K, CB_SB_V, CB_SB_G = 24, 28, 32, 36
MERGE_BLOCK_1024 = 5


def _pcall(body, **kw):
    return pl.pallas_call(body, **kw)


def _params(sem=None):
    if sem is None:
        return pltpu.CompilerParams(vmem_limit_bytes=VMEM_LIMIT)
    return pltpu.CompilerParams(dimension_semantics=sem, vmem_limit_bytes=VMEM_LIMIT)


def _sigmoid(x):
    return 1.0 / (1.0 + jnp.exp(-x))


def _dot(a, b):
    return jnp.dot(a, b, preferred_element_type=F32)


def _dot_nt(a, b):
    return lax.dot_general(a, b, (((1,), (1,)), ((), ())), preferred_element_type=F32)


def _dot_tn(a, b):
    return lax.dot_general(a, b, (((0,), (0,)), ((), ())), preferred_element_type=F32)


def _split_bf16(x):
    hi = x.astype(BF16)
    lo = (x - hi.astype(F32)).astype(BF16)
    return hi, lo


N_PEER = N_DEV - 1
ANY_SPEC = pl.BlockSpec(memory_space=pl.ANY)


def _exchange_copies(ins, outs, send_sems, recv_sems, local_sems, gather, with_recvs=True):
    n = len(ins)
    gathers = _per_array(gather, n)
    x, y, c = lax.axis_index("x"), lax.axis_index("y"), lax.axis_index("c")
    me = 4 * x + 2 * y + c
    flip = lambda v, bit: 1 - v if bit else v
    local, sends, recvs = [], [], []
    for a in range(n):
        src = ins[a] if gathers[a] else ins[a].at[me]
        local.append(pltpu.make_async_copy(src, outs[a].at[me], local_sems.at[a]))
    for k in range(N_PEER):
        px, py, pc = flip(x, ((k + 1) >> 2) & 1), flip(y, ((k + 1) >> 1) & 1), flip(c, (k + 1) & 1)
        peer_id = 4 * px + 2 * py + pc
        for a in range(n):
            src = ins[a] if gathers[a] else ins[a].at[peer_id]
            common = dict(src_ref=src, send_sem=send_sems.at[a * N_PEER + k], recv_sem=recv_sems.at[a * N_PEER + k],
                          device_id=(px, py, pc), device_id_type=pl.DeviceIdType.MESH)
            sends.append(pltpu.make_async_remote_copy(dst_ref=outs[a].at[me], **common))
            if with_recvs:
                recvs.append(pltpu.make_async_remote_copy(dst_ref=outs[a].at[peer_id], **common))
    return local, sends, recvs


def _exchange_start(ins, outs, sems, gather):
    local, sends, _ = _exchange_copies(ins, outs, *sems, gather, with_recvs=False)
    for cp in local + sends:
        cp.start()


def _exchange_wait(ins, outs, sems, gather):
    local, sends, recvs = _exchange_copies(ins, outs, *sems, gather)
    for cp in recvs:
        cp.wait_recv()
    for cp in sends:
        cp.wait_send()
    for cp in local:
        cp.wait()


def _per_array(gather, n):
    return tuple(gather) if isinstance(gather, (tuple, list)) else (gather,) * n


def _exchange_out_shapes(arrs, gather):
    return [jax.ShapeDtypeStruct((N_DEV,) + tuple(a.shape if g else a.shape[1:]), a.dtype)
            for a, g in zip(arrs, _per_array(gather, len(arrs)))]


def _gather_two_level(arrs, name):
    n = len(arrs)

    def body(*refs):
        ins, outs = refs[:n], refs[n:2 * n]
        send_sems, recv_sems, local_sems = refs[2 * n:]
        x, y, c = lax.axis_index("x"), lax.axis_index("y"), lax.axis_index("c")
        me, sibling = (x, y, c), (x, y, 1 - c)
        chips = [(1 - x, y), (x, 1 - y), (1 - x, 1 - y)]
        slot = lambda dev: 4 * dev[0] + 2 * dev[1] + dev[2]

        def copy(a, k, block, to, src=None):
            return pltpu.make_async_remote_copy(
                src_ref=outs[a].at[slot(block)] if src is None else src, dst_ref=outs[a].at[slot(block)],
                send_sem=send_sems.at[a * N_PEER + k], recv_sem=recv_sems.at[a * N_PEER + k],
                device_id=to, device_id_type=pl.DeviceIdType.MESH)

        local = [pltpu.make_async_copy(ins[a], outs[a].at[slot(me)], local_sems.at[a]) for a in range(n)]
        first = []
        for a in range(n):
            first.append(copy(a, 0, me, sibling, src=ins[a]))
            first += [copy(a, 1 + j, me, (*chip, c), src=ins[a]) for j, chip in enumerate(chips)]
        for cp in local + first:
            cp.start()
        passed = []
        for j, chip in enumerate(chips):
            for a in range(n):
                copy(a, 1 + j, (*chip, c), me).wait_recv()
                passed.append(copy(a, 4 + j, (*chip, c), sibling))
                passed[-1].start()
        for a in range(n):
            copy(a, 0, sibling, me).wait_recv()
        for j, chip in enumerate(chips):
            for a in range(n):
                copy(a, 4 + j, (*chip, 1 - c), me).wait_recv()
        for cp in first + passed:
            cp.wait_send()
        for cp in local:
            cp.wait()

    return _pcall(
        body, name=name,
        out_shape=tuple(_exchange_out_shapes(arrs, True)),
        in_specs=[ANY_SPEC] * n, out_specs=tuple([ANY_SPEC] * n),
        scratch_shapes=_exchange_sems(n),
    )(*arrs)


def _exchange_sems(n):
    return [pltpu.SemaphoreType.DMA((n * N_PEER,)), pltpu.SemaphoreType.DMA((n * N_PEER,)),
            pltpu.SemaphoreType.DMA((n,))]


def _exchange(arrs, gather, name):
    n = len(arrs)

    def body(*refs):
        ins, outs, sems = refs[:n], refs[n:2 * n], refs[2 * n:]
        _exchange_start(ins, outs, sems, gather)
        _exchange_wait(ins, outs, sems, gather)

    return _pcall(
        body, name=name,
        out_shape=tuple(_exchange_out_shapes(arrs, gather)),
        in_specs=[ANY_SPEC] * n, out_specs=tuple([ANY_SPEC] * n),
        scratch_shapes=_exchange_sems(n),
    )(*arrs)


def _in_proj_fwd(x, g, w_all, name):
    s = x.shape[0]
    tm = min(1024, s)

    def body(x_ref, g_ref, w_ref, u_ref, h_ref, hs):
        @pl.when(pl.program_id(1) == 0)
        def _():
            xv = x_ref[...]
            r = lax.rsqrt(jnp.mean(xv * xv, axis=-1, keepdims=True) + RMS_EPS)
            hv = (xv * r * g_ref[...]).astype(BF16)
            hs[...] = hv
            h_ref[...] = hv
        u_ref[...] = _dot(hs[...], w_ref[...])

    return _pcall(
        body, name=name, grid=(s // tm, N_DEV),
        in_specs=[pl.BlockSpec((tm, D_MODEL), lambda i, j: (i, 0)),
                  pl.BlockSpec((1, D_MODEL), lambda i, j: (0, 0)),
                  pl.BlockSpec((None, D_MODEL, COLS_PER_DEV), lambda i, j: (j, 0, 0))],
        out_specs=(pl.BlockSpec((tm, COLS_PER_DEV), lambda i, j: (i, j)),
                   pl.BlockSpec((tm, D_MODEL), lambda i, j: (i, 0))),
        out_shape=(jax.ShapeDtypeStruct((s, N_IN), F32), jax.ShapeDtypeStruct((s, D_MODEL), BF16)),
        scratch_shapes=[pltpu.VMEM((tm, D_MODEL), BF16)],
        compiler_params=_params(("parallel", "arbitrary")),
    )(x, g, w_all)


def _pool_window(vs, t0, t, grp):
    ext = vs[pl.ds(t0, t + POOL_HALO), :]
    s2 = ext + pltpu.roll(ext, 1, 0)
    s4 = s2 + pltpu.roll(s2, 2, 0)
    s8 = s4 + pltpu.roll(s4, 4, 0)
    s16 = s8 + pltpu.roll(s8, 8, 0)
    sel = jnp.where(grp == 0, s2, jnp.where(grp == 1, s4, jnp.where(grp == 2, s8, s16)))
    return sel[POOL_HALO:, :], ext[POOL_HALO:, :]


def _pool_count(t0, t, grp):
    pos = t0 + lax.broadcasted_iota(jnp.int32, (t, 1), 0)
    return jnp.minimum(pos + 1, jnp.left_shift(2, grp)).astype(F32)


def _pool_fwd(u, pool_w, pool_scale, name):
    s = u.shape[0]
    t = min(256, s)

    def body(pv_ref, pg_ref, w_ref, sc_ref, y_ref, vs):
        grp = pl.program_id(0)
        vs[0:POOL_HALO, :] = jnp.zeros((POOL_HALO, LANES), F32)
        vs[POOL_HALO:, :] = pv_ref[...]
        wb = w_ref[...].astype(BF16)
        scale = sc_ref[...]

        def tile(i, carry):
            t0 = pl.multiple_of(i * t, t)
            win, v = _pool_window(vs, t0, t, grp)
            pooled = win / _pool_count(t0, t, grp) - v
            mixed = _dot(pooled.astype(BF16), wb)
            gate = pg_ref[pl.ds(t0, t), :]
            y_ref[pl.ds(t0, t), :] = (mixed * scale * (gate * _sigmoid(gate))).astype(BF16)
            return carry

        lax.fori_loop(0, s // t, tile, 0)

    return _pcall(
        body, name=name, grid=(4,),
        in_specs=[pl.BlockSpec((s, LANES), lambda g: (0, CB_POOL_V + g)),
                  pl.BlockSpec((s, LANES), lambda g: (0, CB_POOL_G + g)),
                  pl.BlockSpec((None, LANES, LANES), lambda g: (g, 0, 0)),
                  pl.BlockSpec((1, LANES), lambda g: (0, g))],
        out_specs=pl.BlockSpec((s, LANES), lambda g: (0, g)),
        out_shape=jax.ShapeDtypeStruct((s, WIDTH), BF16),
        scratch_shapes=[pltpu.VMEM((POOL_HALO + s, LANES), F32)],
        compiler_params=_params(("arbitrary",)),
    )(u, u, pool_w, pool_scale)


def _conv_taps(zs, t0, t):
    ext = zs[pl.ds(t0, t + CONV_HALO), :]
    z0 = ext[CONV_HALO:, :]
    z1 = pltpu.roll(ext, 1, 0)[CONV_HALO:, :]
    z2 = pltpu.roll(ext, 2, 0)[CONV_HALO:, :]
    return z0, z1, z2


def _conv_fwd(u, conv_w, conv_b, name):
    s = u.shape[0]
    t = min(256, s)

    def body(xc_ref, gb_ref, gc_ref, cg_ref, w_ref, b_ref, y_ref, zs):
        zs[0:CONV_HALO, :] = jnp.zeros((CONV_HALO, LANES), F32)
        zs[CONV_HALO:, :] = gc_ref[...] * xc_ref[...]
        w0, w1, w2 = w_ref[0:1, :], w_ref[1:2, :], w_ref[2:3, :]
        bias = b_ref[...]

        def tile(i, carry):
            t0 = pl.multiple_of(i * t, t)
            z0, z1, z2 = _conv_taps(zs, t0, t)
            conv = w0 * z2 + w1 * z1 + w2 * z0
            gate = cg_ref[pl.ds(t0, t), :]
            y = gb_ref[pl.ds(t0, t), :] * (conv + bias) * (gate * _sigmoid(gate))
            y_ref[pl.ds(t0, t), :] = y.astype(BF16)
            return carry

        lax.fori_loop(0, s // t, tile, 0)

    col = lambda base: pl.BlockSpec((s, LANES), lambda j: (0, base + j))
    return _pcall(
        body, name=name, grid=(4,),
        in_specs=[col(CB_CONV_X), col(CB_CONV_GB), col(CB_CONV_GC), col(CB_CONV_G),
                  pl.BlockSpec((3, LANES), lambda j: (0, j)),
                  pl.BlockSpec((1, LANES), lambda j: (0, j))],
        out_specs=pl.BlockSpec((s, LANES), lambda j: (0, j)),
        out_shape=jax.ShapeDtypeStruct((s, WIDTH), BF16),
        scratch_shapes=[pltpu.VMEM((CONV_HALO + s, LANES), F32)],
        compiler_params=_params(("arbitrary",)),
    )(u, u, u, u, conv_w, conv_b)


def _first_head_lanes(rows, width=LANES):
    lane = lax.broadcasted_iota(jnp.int32, (rows, width), 1)
    return jnp.bitwise_and(lane, LANES - 1) < HEAD_DIM


def _stack_heads(x, first):
    zero = jnp.zeros_like(x)
    return jnp.concatenate([jnp.where(first, x, zero), jnp.where(first, zero, x)], axis=0).astype(BF16)


def _causal_mask(tq, tk, copies):
    row = lax.broadcasted_iota(jnp.int32, (tq, tk), 0)
    col = lax.broadcasted_iota(jnp.int32, (tq, tk), 1)
    return jnp.concatenate([col < row] * copies, axis=0)


def _suffix_matrix(tk, inclusive):
    r = lax.broadcasted_iota(jnp.int32, (2 * tk, 2 * tk), 0)
    c = lax.broadcasted_iota(jnp.int32, (2 * tk, 2 * tk), 1)
    r = jnp.where(r >= tk, r - tk, r)
    tri = (r >= c) if inclusive else (r > c)
    return jnp.where(c >= tk, 1.0, jnp.where(tri, 1.0, 0.0)).astype(BF16)


def _suffix_sums(x, m):
    hi, lo = _split_bf16(x)
    return _dot(jnp.concatenate([hi, lo], axis=1), m)


def _sb_log_terms(z, mask, m_strict):
    ls = jnp.minimum(z, 0.0) - jnp.log(1.0 + jnp.exp(-jnp.abs(z)))
    lk = ls - z
    if mask is not None:
        lk = jnp.where(mask, lk, 0.0)
    return ls, _suffix_sums(lk, m_strict)


SB_PAIRS = 4


def _pair_lanes(a):
    return slice(a * LANES, (a + 1) * LANES)


def _sb_fwd(u, name, xchg=None):
    s = u.shape[0]
    tq = tk = min(128, s)
    pairs = SB_PAIRS
    width = pairs * LANES
    rows = 2 * pairs * tq
    x_arrs, x_gather = xchg if xchg else ((), True)
    n_x = len(x_arrs)
    grid = (4 // pairs, s // tq)

    def body(*refs):
        q_ref, k_ref, v_ref, g_ref = refs[:4]
        x_in, refs = refs[4:4 + n_x], refs[4 + n_x:]
        o_ref, y_ref = refs[:2]
        x_out, refs = refs[2:2 + n_x], refs[2 + n_x:]
        kbf, vst, z_s, ell_s, carry_s = refs[:5]
        x_sems = refs[5:]
        i = pl.program_id(1)
        if n_x:
            @pl.when((pl.program_id(0) == 0) & (i == 0))
            def _():
                _exchange_start(x_in, x_out, x_sems, x_gather)

        @pl.when(i == 0)
        def _():
            kbf[...] = k_ref[...].astype(BF16)
            first_s = _first_head_lanes(s, width)
            vf = v_ref[...]
            vst[0] = jnp.where(first_s, vf, 0.0).astype(BF16)
            vst[1] = jnp.where(first_s, 0.0, vf).astype(BF16)

        first = _first_head_lanes(tq)
        mask = _causal_mask(tq, tk, 2 * pairs)
        m_strict = _suffix_matrix(tk, False)
        qcat = jnp.concatenate([_stack_heads(q_ref[:, _pair_lanes(a)] * SB_SCALE, first) for a in range(pairs)],
                               axis=0)

        def scores(b):
            off = pl.multiple_of(jnp.maximum(b, 0) * tk, tk)
            z_s[...] = jnp.concatenate(
                [_dot_nt(qcat[a * 2 * tq:(a + 1) * 2 * tq], kbf[pl.ds(off, tk), _pair_lanes(a)])
                 for a in range(pairs)], axis=0)

        def log_weights(m):
            ls, cs = _sb_log_terms(z_s[...], m, m_strict)
            carry = carry_s[...]
            ell_s[...] = ls + cs[:, :tk] + carry
            carry_s[...] = carry + cs[:, tk:]

        def consume(b, accs, m):
            w = jnp.exp(ell_s[...])
            if m is not None:
                w = jnp.where(m, w, 0.0)
            wb = w.astype(BF16)
            off = pl.multiple_of(b * tk, tk)
            new = []
            for a in range(pairs):
                r0 = a * 2 * tq
                wcat = jnp.concatenate([wb[r0:r0 + tq], wb[r0 + tq:r0 + 2 * tq]], axis=1)
                vcat = jnp.concatenate([vst[0, pl.ds(off, tk), _pair_lanes(a)], vst[1, pl.ds(off, tk), _pair_lanes(a)]],
                                       axis=0)
                new.append(accs[a] + _dot(wcat, vcat))
            return tuple(new)

        carry_s[...] = jnp.zeros((rows, tk), F32)
        scores(i)
        log_weights(mask)
        scores(i - 1)
        accs = consume(i, tuple(jnp.zeros((tq, LANES), F32) for _ in range(pairs)), mask)
        log_weights(None)
        scores(i - 2)

        def step(n, accs):
            accs = consume(i - n, accs, None)
            log_weights(None)
            scores(i - n - 2)
            return accs

        accs = lax.fori_loop(1, i + 1, step, accs)
        o = jnp.concatenate(accs, axis=1)
        o_ref[...] = o
        gate = g_ref[...]
        y_ref[...] = (o * (gate * _sigmoid(gate))).astype(BF16)
        if n_x:
            @pl.when((pl.program_id(0) == grid[0] - 1) & (i == grid[1] - 1))
            def _():
                _exchange_wait(x_in, x_out, x_sems, x_gather)

    base = lambda cb: cb // pairs
    qblk = lambda cb: pl.BlockSpec((tq, width), lambda p, i: (i, base(cb) + p))
    full = lambda cb: pl.BlockSpec((s, width), lambda p, i: (0, base(cb) + p), pipeline_mode=pl.Buffered(1))
    state = pltpu.VMEM((rows, tk), F32)
    return _pcall(
        body, name=name, grid=grid,
        in_specs=[qblk(CB_SB_Q), full(CB_SB_K), full(CB_SB_V), qblk(CB_SB_G)] + [ANY_SPEC] * n_x,
        out_specs=(qblk(0), qblk(0)) + (ANY_SPEC,) * n_x,
        out_shape=(jax.ShapeDtypeStruct((s, WIDTH), F32), jax.ShapeDtypeStruct((s, WIDTH), BF16))
        + tuple(_exchange_out_shapes(x_arrs, x_gather)),
        scratch_shapes=[pltpu.VMEM((s, width), BF16), pltpu.VMEM((2, s, width), BF16), state, state, state]
        + (_exchange_sems(n_x) if n_x else []),
        compiler_params=_params(("arbitrary", "arbitrary")),
    )(u, u, u, u, *x_arrs)


def _merge_out_fwd(y_pool, y_conv, y_sb, u, wb_all, wo_all, x, g_post, layer, name):
    s = x.shape[0]
    tm = min(256, s)

    def body(yp, yc, ys, m0, m1, m2, wb_ref, wo_ref, x_ref, g_ref, out_ref, merged_ref, pre_ref):
        merged = jnp.zeros((tm, D_MODEL), F32)
        for n, (y_ref, m_ref) in enumerate(((yp, m0), (yc, m1), (ys, m2))):
            merged = merged + _sigmoid(m_ref[...]) * _dot(y_ref[...], wb_ref[n])
        mb = merged.astype(BF16)
        merged_ref[...] = mb
        pre = _dot(mb, wo_ref[...].reshape(D_MODEL, D_MODEL))
        pre_ref[...] = pre
        r = lax.rsqrt(jnp.mean(pre * pre, axis=-1, keepdims=True) + RMS_EPS)
        out_ref[...] = x_ref[...] + pre * r * g_ref[...]

    rows = lambda w: pl.BlockSpec((tm, w), lambda i: (i, 0))
    merge = lambda n: pl.BlockSpec((tm, D_MODEL), lambda i: (i, MERGE_BLOCK_1024 + n))
    return _pcall(
        body, name=name, grid=(s // tm,),
        in_specs=[rows(WIDTH), rows(WIDTH), rows(WIDTH), merge(0), merge(1), merge(2),
                  pl.BlockSpec((None, 3, WIDTH, D_MODEL), lambda i: (layer, 0, 0, 0)),
                  pl.BlockSpec((N_DEV, None, D_MODEL // N_DEV, D_MODEL), lambda i: (0, layer, 0, 0)),
                  rows(D_MODEL), pl.BlockSpec((1, D_MODEL), lambda i: (0, 0))],
        out_specs=(rows(D_MODEL), rows(D_MODEL), rows(D_MODEL)),
        out_shape=(jax.ShapeDtypeStruct((s, D_MODEL), F32), jax.ShapeDtypeStruct((s, D_MODEL), BF16),
                   jax.ShapeDtypeStruct((s, D_MODEL), F32)),
        compiler_params=_params(("arbitrary",)),
    )(y_pool, y_conv, y_sb, u, u, u, wb_all, wo_all, x, g_post)


def _loss_and_grad(y, target, name):
    s = y.shape[0]
    tm = min(512, s)

    def body(y_ref, t_ref, dy_ref, loss_ref, acc):
        i = pl.program_id(0)

        @pl.when(i == 0)
        def _():
            acc[...] = jnp.zeros_like(acc)
        err = y_ref[...] - t_ref[...]
        dy_ref[...] = err / D_MODEL
        acc[...] += jnp.sum(err * err, axis=0, keepdims=True)

        @pl.when(i == pl.num_programs(0) - 1)
        def _():
            total = jnp.sum(acc[...], axis=1, keepdims=True) * (0.5 / D_MODEL)
            loss_ref[...] = jnp.broadcast_to(total, (1, LANES))

    return _pcall(
        body, name=name, grid=(s // tm,),
        in_specs=[pl.BlockSpec((tm, D_MODEL), lambda i: (i, 0)), pl.BlockSpec((tm, D_MODEL), lambda i: (i, 0))],
        out_specs=(pl.BlockSpec((tm, D_MODEL), lambda i: (i, 0)), pl.BlockSpec((1, LANES), lambda i: (0, 0))),
        out_shape=(jax.ShapeDtypeStruct((s, D_MODEL), F32), jax.ShapeDtypeStruct((1, LANES), F32)),
        scratch_shapes=[pltpu.VMEM((1, D_MODEL), F32)],
        compiler_params=_params(("arbitrary",)),
    )(y, target)


def _out_proj_bwd(dy, pre, g_post, merged, wo_all, layer, name):
    s = dy.shape[0]
    tm = min(256, s)
    n_tiles = s // tm

    def body(dy_ref, pre_ref, g_ref, mg_ref, wo_ref, dm_ref, dwo_ref, dg_ref, acc):
        i = pl.program_id(0)

        @pl.when(i == 0)
        def _():
            acc[...] = jnp.zeros_like(acc)
            dg_ref[...] = jnp.zeros_like(dg_ref)
        dyv, pre_v = dy_ref[...], pre_ref[...]
        r = lax.rsqrt(jnp.mean(pre_v * pre_v, axis=-1, keepdims=True) + RMS_EPS)
        dg_ref[...] += jnp.sum(dyv * pre_v * r, axis=0, keepdims=True)
        a = dyv * g_ref[...]
        dpre = r * a - pre_v * (r * r * r) * jnp.mean(a * pre_v, axis=-1, keepdims=True)
        db = dpre.astype(BF16)
        acc[...] += _dot_tn(mg_ref[...], db)
        dm_ref[...] = _dot_nt(db, wo_ref[...].reshape(D_MODEL, D_MODEL))

        @pl.when(i == n_tiles - 1)
        def _():
            dwo_ref[...] = acc[...].astype(BF16)

    rows = lambda: pl.BlockSpec((tm, D_MODEL), lambda i: (i, 0))
    return _pcall(
        body, name=name, grid=(n_tiles,),
        in_specs=[rows(), rows(), pl.BlockSpec((1, D_MODEL), lambda i: (0, 0)), rows(),
                  pl.BlockSpec((N_DEV, None, D_MODEL // N_DEV, D_MODEL), lambda i: (0, layer, 0, 0))],
        out_specs=(rows(), pl.BlockSpec((D_MODEL, D_MODEL), lambda i: (0, 0)),
                   pl.BlockSpec((1, D_MODEL), lambda i: (0, 0))),
        out_shape=(jax.ShapeDtypeStruct((s, D_MODEL), F32), jax.ShapeDtypeStruct((D_MODEL, D_MODEL), BF16),
                   jax.ShapeDtypeStruct((1, D_MODEL), F32)),
        scratch_shapes=[pltpu.VMEM((D_MODEL, D_MODEL), F32)],
        compiler_params=_params(("arbitrary",)),
    )(dy, pre, g_post, merged, wo_all)


def _merge_bwd(dmerged, y_pool, y_conv, y_sb, u, wb_all, layer, name):
    s = dmerged.shape[0]
    tm = min(256, s)
    n_tiles = s // tm
    cols = D_MODEL // N_DEV

    def body(dm_ref, yp, yc, ys, m0, m1, m2, wb_ref, dum_ref, dyp, dyc, dys, dwb_ref, acc):
        i = pl.program_id(0)

        @pl.when(i == 0)
        def _():
            acc[...] = jnp.zeros_like(acc)
        dm = dm_ref[...]
        for n, (y_ref, m_ref, dy_ref) in enumerate(((yp, m0, dyp), (yc, m1, dyc), (ys, m2, dys))):
            yv = y_ref[...]
            wb = wb_ref[n]
            gate = _sigmoid(m_ref[...])
            proj = _dot(yv, wb)
            dum_ref[:, n * D_MODEL:(n + 1) * D_MODEL] = (dm * proj * gate * (1.0 - gate)).astype(BF16)
            dproj = (dm * gate).astype(BF16)
            acc[n] += _dot_tn(yv, dproj)
            dy_ref[...] = _dot_nt(dproj, wb)

        @pl.when(i == n_tiles - 1)
        def _():
            for j in range(N_DEV):
                for n in range(3):
                    dwb_ref[j, n] = acc[n, :, j * cols:(j + 1) * cols].astype(BF16)

    rows = lambda w: pl.BlockSpec((tm, w), lambda i: (i, 0))
    merge = lambda n: pl.BlockSpec((tm, D_MODEL), lambda i: (i, MERGE_BLOCK_1024 + n))
    return _pcall(
        body, name=name, grid=(n_tiles,),
        in_specs=[rows(D_MODEL), rows(WIDTH), rows(WIDTH), rows(WIDTH), merge(0), merge(1), merge(2),
                  pl.BlockSpec((None, 3, WIDTH, D_MODEL), lambda i: (layer, 0, 0, 0))],
        out_specs=(rows(3 * D_MODEL), rows(WIDTH), rows(WIDTH), rows(WIDTH),
                   pl.BlockSpec((N_DEV, 3, WIDTH, cols), lambda i: (0, 0, 0, 0))),
        out_shape=(jax.ShapeDtypeStruct((s, 3 * D_MODEL), BF16),
                   jax.ShapeDtypeStruct((s, WIDTH), F32), jax.ShapeDtypeStruct((s, WIDTH), F32),
                   jax.ShapeDtypeStruct((s, WIDTH), F32),
                   jax.ShapeDtypeStruct((N_DEV, 3, WIDTH, cols), BF16)),
        scratch_shapes=[pltpu.VMEM((3, WIDTH, D_MODEL), F32)],
        compiler_params=_params(("arbitrary",)),
    )(dmerged, y_pool, y_conv, y_sb, u, u, u, wb_all)


def _sb_bwd(u, o, dys, name, xchg=None):
    s = u.shape[0]
    tq = tk = min(128, s)
    pairs = SB_PAIRS
    width = pairs * LANES
    rows = 2 * pairs * tq
    pair_rows = lambda a: slice(a * 2 * tq, (a + 1) * 2 * tq)

    x_arrs, x_gather = xchg if xchg else ((), True)
    n_x = len(x_arrs)
    grid = (4 // pairs, s // tq)

    def body(*refs):
        q_ref, k_ref, v_ref, g_ref, o_ref, dys_ref = refs[:6]
        x_in, refs = refs[6:6 + n_x], refs[6 + n_x:]
        dq_ref, dk_ref, dv_ref, dg_ref = refs[:4]
        x_out, refs = refs[4:4 + n_x], refs[4 + n_x:]
        kbf, vbf, kst, z_s, ell_s, ls_s, cl_s, wb_s, g_s, bef_s, cg_s, beta_s = refs[:12]
        x_sems = refs[12:]
        i = pl.program_id(1)
        if n_x:
            @pl.when((pl.program_id(0) == 0) & (i == 0))
            def _():
                _exchange_start(x_in, x_out, x_sems, x_gather)

        @pl.when(i == 0)
        def _():
            dk_ref[...] = jnp.zeros_like(dk_ref)
            dv_ref[...] = jnp.zeros_like(dv_ref)
            kf = k_ref[...]
            kbf[...] = kf.astype(BF16)
            vbf[...] = v_ref[...].astype(BF16)
            first_s = _first_head_lanes(s, width)
            kst[0] = jnp.where(first_s, kf, 0.0).astype(BF16)
            kst[1] = jnp.where(first_s, 0.0, kf).astype(BF16)

        first = _first_head_lanes(tq)
        mask = _causal_mask(tq, tk, 2 * pairs)
        m_strict = _suffix_matrix(tk, False)
        m_incl = _suffix_matrix(tk, True)

        gate = g_ref[...]
        sg = _sigmoid(gate)
        dy = dys_ref[...]
        ov = o_ref[...]
        dg_ref[...] = (dy * ov * (sg * (1.0 + gate * (1.0 - sg)))).astype(BF16)
        do = (dy * (gate * sg)).astype(BF16)
        prod = do.astype(F32) * ov
        row_sum = lambda v: jnp.broadcast_to(jnp.sum(v, axis=1, keepdims=True), (tq, tk))
        dsum, docat, qcat = [], [], []
        for a in range(pairs):
            pa = prod[:, _pair_lanes(a)]
            dsum += [row_sum(jnp.where(first, pa, 0.0)), row_sum(jnp.where(first, 0.0, pa))]
            docat.append(_stack_heads(do[:, _pair_lanes(a)], first))
            qcat.append(_stack_heads(q_ref[:, _pair_lanes(a)] * SB_SCALE, first))
        dsum = jnp.concatenate(dsum, axis=0)

        def block_start(b):
            return pl.multiple_of(jnp.maximum(b, 0) * tk, tk)

        def scores(b):
            off = block_start(b)
            z_s[...] = jnp.concatenate([_dot_nt(qcat[a], kbf[pl.ds(off, tk), _pair_lanes(a)]) for a in range(pairs)],
                                       axis=0)

        def log_weights(m):
            ls, cs = _sb_log_terms(z_s[...], m, m_strict)
            cl = cl_s[...]
            ell_s[...] = ls + cs[:, :tk] + cl
            cl_s[...] = cl + cs[:, tk:]
            ls_s[...] = ls

        def weights(b, m):
            off = block_start(b)
            dwt = jnp.concatenate([_dot_nt(docat[a], vbf[pl.ds(off, tk), _pair_lanes(a)]) for a in range(pairs)],
                                  axis=0)
            w = jnp.exp(ell_s[...])
            if m is not None:
                w = jnp.where(m, w, 0.0)
            wb = w.astype(BF16)
            g = dwt * wb.astype(F32)
            gs = _suffix_sums(g, m_incl)
            cg = cg_s[...]
            beta = jnp.exp(ls_s[...])
            wb_s[...] = wb
            beta_s[...] = beta
            g_s[...] = g * (1.0 - beta)
            bef_s[...] = gs[:, :tk] + cg
            cg_s[...] = cg + gs[:, tk:]

        def grads(b, dqs, m):
            dz = g_s[...] - beta_s[...] * (dsum - bef_s[...])
            if m is not None:
                dz = jnp.where(m, dz, 0.0)
            dzb = dz.astype(BF16)
            wb = wb_s[...]
            off = pl.multiple_of(b * tk, tk)
            new = []
            for a in range(pairs):
                r0 = a * 2 * tq
                kcat = jnp.concatenate([kst[0, pl.ds(off, tk), _pair_lanes(a)], kst[1, pl.ds(off, tk), _pair_lanes(a)]],
                                       axis=0)
                new.append(dqs[a] + _dot(jnp.concatenate([dzb[r0:r0 + tq], dzb[r0 + tq:r0 + 2 * tq]], axis=1), kcat))
                dk_ref[pl.ds(off, tk), _pair_lanes(a)] += _dot_tn(dzb[pair_rows(a)], qcat[a])
                dv_ref[pl.ds(off, tk), _pair_lanes(a)] += _dot_tn(wb[pair_rows(a)], docat[a])
            return tuple(new)

        zero = jnp.zeros((rows, tk), F32)
        cl_s[...] = zero
        cg_s[...] = zero
        scores(i)
        log_weights(mask)
        scores(i - 1)
        weights(i, mask)
        log_weights(None)
        scores(i - 2)
        dqs = grads(i, tuple(jnp.zeros((tq, LANES), F32) for _ in range(pairs)), mask)
        weights(i - 1, None)
        log_weights(None)
        scores(i - 3)

        def step(n, dqs):
            dqs = grads(i - n, dqs, None)
            weights(i - n - 1, None)
            log_weights(None)
            scores(i - n - 3)
            return dqs

        dqs = lax.fori_loop(1, i + 1, step, dqs)
        dq_ref[...] = (jnp.concatenate(dqs, axis=1) * SB_SCALE).astype(BF16)
        if n_x:
            @pl.when((pl.program_id(0) == grid[0] - 1) & (i == grid[1] - 1))
            def _():
                _exchange_wait(x_in, x_out, x_sems, x_gather)

    base = lambda cb: cb // pairs
    qblk = lambda cb: pl.BlockSpec((tq, width), lambda p, i: (i, base(cb) + p))
    full = lambda cb: pl.BlockSpec((s, width), lambda p, i: (0, base(cb) + p), pipeline_mode=pl.Buffered(1))
    state = pltpu.VMEM((rows, tk), F32)
    return _pcall(
        body, name=name, grid=grid,
        in_specs=[qblk(CB_SB_Q), full(CB_SB_K), full(CB_SB_V), qblk(CB_SB_G), qblk(0), qblk(0)] + [ANY_SPEC] * n_x,
        out_specs=(qblk(0), full(0), full(0), qblk(0)) + (ANY_SPEC,) * n_x,
        out_shape=(jax.ShapeDtypeStruct((s, WIDTH), BF16), jax.ShapeDtypeStruct((s, WIDTH), F32),
                   jax.ShapeDtypeStruct((s, WIDTH), F32), jax.ShapeDtypeStruct((s, WIDTH), BF16))
        + tuple(_exchange_out_shapes(x_arrs, x_gather)),
        scratch_shapes=[pltpu.VMEM((s, width), BF16), pltpu.VMEM((s, width), BF16), pltpu.VMEM((2, s, width), BF16),
                        state, state, state, state, pltpu.VMEM((rows, tk), BF16),
                        state, state, state, state] + (_exchange_sems(n_x) if n_x else []),
        compiler_params=_params(("arbitrary", "arbitrary")),
    )(u, u, u, u, o, dys, *x_arrs)


def _conv_bwd(u, conv_w, conv_b, dyc, name):
    s = u.shape[0]
    t = min(256, s)
    n_tiles = s // t

    def body(xc_ref, gb_ref, gc_ref, cg_ref, w_ref, b_ref, dy_ref,
             dxc_ref, dgb_ref, dgc_ref, dcg_ref, dw_ref, db_ref, zs, ds):
        zs[0:CONV_HALO, :] = jnp.zeros((CONV_HALO, LANES), F32)
        zs[CONV_HALO:, :] = gc_ref[...] * xc_ref[...]
        ds[s:, :] = jnp.zeros((CONV_HALO, LANES), F32)
        w0, w1, w2 = w_ref[0:1, :], w_ref[1:2, :], w_ref[2:3, :]
        bias = b_ref[...]

        def first(i, sums):
            t0 = pl.multiple_of(i * t, t)
            z0, z1, z2 = _conv_taps(zs, t0, t)
            pre = w0 * z2 + w1 * z1 + w2 * z0 + bias
            gate = cg_ref[pl.ds(t0, t), :]
            sg = _sigmoid(gate)
            gb = gb_ref[pl.ds(t0, t), :]
            dy = dy_ref[pl.ds(t0, t), :]
            dcg_ref[pl.ds(t0, t), :] = (dy * gb * pre * (sg * (1.0 + gate * (1.0 - sg)))).astype(BF16)
            dgb_ref[pl.ds(t0, t), :] = (dy * pre * (gate * sg)).astype(BF16)
            dc = dy * gb * (gate * sg)
            ds[pl.ds(t0, t), :] = dc
            red = lambda v: jnp.sum(v, axis=0, keepdims=True)
            return (sums[0] + red(dc * z2), sums[1] + red(dc * z1), sums[2] + red(dc * z0), sums[3] + red(dc))

        zrow = jnp.zeros((1, LANES), F32)
        sw0, sw1, sw2, sb = lax.fori_loop(0, n_tiles, first, (zrow, zrow, zrow, zrow))
        dw_ref[0:1, :] = sw0
        dw_ref[1:2, :] = sw1
        dw_ref[2:3, :] = sw2
        db_ref[...] = sb

        def second(i, carry):
            t0 = pl.multiple_of(i * t, t)
            ext = ds[pl.ds(t0, t + CONV_HALO), :]
            n = t + CONV_HALO
            d0 = ext[:t, :]
            d1 = pltpu.roll(ext, n - 1, 0)[:t, :]
            d2 = pltpu.roll(ext, n - 2, 0)[:t, :]
            dz = w2 * d0 + w1 * d1 + w0 * d2
            dgc_ref[pl.ds(t0, t), :] = (dz * xc_ref[pl.ds(t0, t), :]).astype(BF16)
            dxc_ref[pl.ds(t0, t), :] = (dz * gc_ref[pl.ds(t0, t), :]).astype(BF16)
            return carry

        lax.fori_loop(0, n_tiles, second, 0)

    col = lambda base: pl.BlockSpec((s, LANES), lambda j: (0, base + j))
    dcol = jax.ShapeDtypeStruct((s, WIDTH), BF16)
    return _pcall(
        body, name=name, grid=(4,),
        in_specs=[col(CB_CONV_X), col(CB_CONV_GB), col(CB_CONV_GC), col(CB_CONV_G),
                  pl.BlockSpec((3, LANES), lambda j: (0, j)), pl.BlockSpec((1, LANES), lambda j: (0, j)), col(0)],
        out_specs=(col(0), col(0), col(0), col(0),
                   pl.BlockSpec((3, LANES), lambda j: (0, j)), pl.BlockSpec((1, LANES), lambda j: (0, j))),
        out_shape=(dcol, dcol, dcol, dcol,
                   jax.ShapeDtypeStruct((3, WIDTH), F32), jax.ShapeDtypeStruct((1, WIDTH), F32)),
        scratch_shapes=[pltpu.VMEM((CONV_HALO + s, LANES), F32), pltpu.VMEM((s + CONV_HALO, LANES), F32)],
        compiler_params=_params(("arbitrary",)),
    )(u, u, u, u, conv_w, conv_b, dyc)


def _pool_bwd(u, pool_w, pool_scale, dyp, name):
    s = u.shape[0]
    t = min(256, s)
    n_tiles = s // t

    def body(pv_ref, pg_ref, w_ref, sc_ref, dy_ref, dpv_ref, dpg_ref, dw_ref, dsc_ref, vs, es, dps):
        grp = pl.program_id(0)
        vs[0:POOL_HALO, :] = jnp.zeros((POOL_HALO, LANES), F32)
        vs[POOL_HALO:, :] = pv_ref[...]
        es[s:, :] = jnp.zeros((POOL_HALO, LANES), F32)
        wb = w_ref[...].astype(BF16)
        scale = sc_ref[...]

        def first(i, sums):
            dw, dsc = sums
            t0 = pl.multiple_of(i * t, t)
            win, v = _pool_window(vs, t0, t, grp)
            cnt = _pool_count(t0, t, grp)
            pb = (win / cnt - v).astype(BF16)
            mixed = _dot(pb, wb)
            gate = pg_ref[pl.ds(t0, t), :]
            sg = _sigmoid(gate)
            dy = dy_ref[pl.ds(t0, t), :]
            dpg_ref[pl.ds(t0, t), :] = (dy * (mixed * scale) * (sg * (1.0 + gate * (1.0 - sg)))).astype(BF16)
            dms = dy * (gate * sg)
            dsc = dsc + jnp.sum(dms * mixed, axis=0, keepdims=True)
            dmb = (dms * scale).astype(BF16)
            dw = dw + _dot_tn(pb, dmb)
            dpooled = _dot_nt(dmb, wb)
            dps[pl.ds(t0, t), :] = dpooled
            es[pl.ds(t0, t), :] = dpooled / cnt
            return dw, dsc

        dw, dsc = lax.fori_loop(0, n_tiles, first, (jnp.zeros((LANES, LANES), F32), jnp.zeros((1, LANES), F32)))
        dw_ref[...] = dw
        dsc_ref[...] = dsc

        def second(i, carry):
            t0 = pl.multiple_of(i * t, t)
            ext = es[pl.ds(t0, t + POOL_HALO), :]
            n = t + POOL_HALO
            f2 = ext + pltpu.roll(ext, n - 1, 0)
            f4 = f2 + pltpu.roll(f2, n - 2, 0)
            f8 = f4 + pltpu.roll(f4, n - 4, 0)
            f16 = f8 + pltpu.roll(f8, n - 8, 0)
            sel = jnp.where(grp == 0, f2, jnp.where(grp == 1, f4, jnp.where(grp == 2, f8, f16)))
            dpv_ref[pl.ds(t0, t), :] = (sel[:t, :] - dps[pl.ds(t0, t), :]).astype(BF16)
            return carry

        lax.fori_loop(0, n_tiles, second, 0)

    col = lambda base: pl.BlockSpec((s, LANES), lambda g: (0, base + g))
    dcol = jax.ShapeDtypeStruct((s, WIDTH), BF16)
    return _pcall(
        body, name=name, grid=(4,),
        in_specs=[col(CB_POOL_V), col(CB_POOL_G), pl.BlockSpec((None, LANES, LANES), lambda g: (g, 0, 0)),
                  pl.BlockSpec((1, LANES), lambda g: (0, g)), col(0)],
        out_specs=(col(0), col(0), pl.BlockSpec((None, LANES, LANES), lambda g: (g, 0, 0)),
                   pl.BlockSpec((1, LANES), lambda g: (0, g))),
        out_shape=(dcol, dcol, jax.ShapeDtypeStruct((4, LANES, LANES), F32), jax.ShapeDtypeStruct((1, WIDTH), F32)),
        scratch_shapes=[pltpu.VMEM((POOL_HALO + s, LANES), F32), pltpu.VMEM((s + POOL_HALO, LANES), F32),
                        pltpu.VMEM((s, LANES), F32)],
        compiler_params=_params(("arbitrary",)),
    )(u, u, pool_w, pool_scale, dyp)


def _in_proj_bwd_x(du, w_all, x, g_pre, dy, name, xchg=None):
    s = x.shape[0]
    tm = min(1024, s)
    x_arrs, x_gather = xchg if xchg else ((), True)
    n_x = len(x_arrs)
    grid = (s // tm, N_DEV)

    def body(*refs):
        du_ref, w_ref, x_ref, g_ref, dy_ref = refs[:5]
        x_in, refs = refs[5:5 + n_x], refs[5 + n_x:]
        dx_ref, dg_ref = refs[:2]
        x_out, refs = refs[2:2 + n_x], refs[2 + n_x:]
        acc, x_sems = refs[0], refs[1:]
        i, k = pl.program_id(0), pl.program_id(1)
        if n_x:
            @pl.when((i == 0) & (k == 0))
            def _():
                _exchange_start(x_in, x_out, x_sems, x_gather)

        @pl.when(k == 0)
        def _():
            acc[...] = jnp.zeros_like(acc)

        @pl.when((k == 0) & (i == 0))
        def _():
            dg_ref[...] = jnp.zeros_like(dg_ref)
        acc[...] += _dot_nt(du_ref[...], w_ref[...])

        @pl.when(k == N_DEV - 1)
        def _():
            dh, xv = acc[...], x_ref[...]
            r = lax.rsqrt(jnp.mean(xv * xv, axis=-1, keepdims=True) + RMS_EPS)
            dg_ref[...] += jnp.sum(dh * xv * r, axis=0, keepdims=True)
            a = dh * g_ref[...]
            dx_ref[...] = dy_ref[...] + r * a - xv * (r * r * r) * jnp.mean(a * xv, axis=-1, keepdims=True)

        if n_x:
            @pl.when((i == grid[0] - 1) & (k == grid[1] - 1))
            def _():
                _exchange_wait(x_in, x_out, x_sems, x_gather)

    rows = lambda: pl.BlockSpec((tm, D_MODEL), lambda i, k: (i, 0))
    vec = lambda: pl.BlockSpec((1, D_MODEL), lambda i, k: (0, 0))
    return _pcall(
        body, name=name, grid=grid,
        in_specs=[pl.BlockSpec((tm, COLS_PER_DEV), lambda i, k: (i, k)),
                  pl.BlockSpec((None, D_MODEL, COLS_PER_DEV), lambda i, k: (k, 0, 0)),
                  rows(), vec(), rows()] + [ANY_SPEC] * n_x,
        out_specs=(rows(), vec()) + (ANY_SPEC,) * n_x,
        out_shape=(jax.ShapeDtypeStruct((s, D_MODEL), F32), jax.ShapeDtypeStruct((1, D_MODEL), F32))
        + tuple(_exchange_out_shapes(x_arrs, x_gather)),
        scratch_shapes=[pltpu.VMEM((tm, D_MODEL), F32)] + (_exchange_sems(n_x) if n_x else []),
        compiler_params=_params(("arbitrary", "arbitrary")),
    )(du, w_all, x, g_pre, dy, *x_arrs)


def _in_proj_bwd_w(h, du, name):
    s = h.shape[0]
    tk = min(512, s)
    n_k = s // tk

    def body(h_ref, du_ref, out_ref, acc):
        k = pl.program_id(1)

        @pl.when(k == 0)
        def _():
            acc[...] = jnp.zeros_like(acc)
        acc[...] += _dot_tn(h_ref[...], du_ref[...])

        @pl.when(k == n_k - 1)
        def _():
            out_ref[...] = acc[...].astype(BF16)

    return _pcall(
        body, name=name, grid=(N_DEV, n_k),
        in_specs=[pl.BlockSpec((tk, D_MODEL), lambda j, k: (k, 0)),
                  pl.BlockSpec((tk, COLS_PER_DEV), lambda j, k: (k, j))],
        out_specs=pl.BlockSpec((None, D_MODEL, COLS_PER_DEV), lambda j, k: (j, 0, 0)),
        out_shape=jax.ShapeDtypeStruct((N_DEV, D_MODEL, COLS_PER_DEV), BF16),
        scratch_shapes=[pltpu.VMEM((D_MODEL, COLS_PER_DEV), F32)],
        compiler_params=_params(("parallel", "arbitrary")),
    )(h, du)


def _adamw_math(g, w, m, v):
    m_new = ADAM_B1 * m + (1.0 - ADAM_B1) * g
    v_new = ADAM_B2 * v + (1.0 - ADAM_B2) * (g * g)
    m_hat = m_new / (1.0 - ADAM_B1 ** ADAM_STEP)
    v_hat = v_new / (1.0 - ADAM_B2 ** ADAM_STEP)
    delta = -ADAM_LR * (m_hat / (jnp.sqrt(v_hat) + ADAM_EPS) + ADAM_WD * w)
    return delta, m_new, v_new


def _sum_partials(p_ref):
    total = p_ref[0].astype(F32)
    for d in range(1, N_DEV):
        total = total + p_ref[d].astype(F32)
    return total


def _adamw_layers(parts0, parts1, w, m, v, name):
    _, r, c = w.shape
    tr = min(128, r)
    n_r = r // tr

    def body(p0_ref, p1_ref, w_ref, m_ref, v_ref, g_ref, d_ref, mo_ref, vo_ref):
        layer = pl.program_id(0)

        @pl.when(layer == 0)
        def _():
            g_ref[...] = _sum_partials(p0_ref)

        @pl.when(layer == 1)
        def _():
            g_ref[...] = _sum_partials(p1_ref)
        d_ref[...], mo_ref[...], vo_ref[...] = _adamw_math(g_ref[...], w_ref[...], m_ref[...], v_ref[...])

    part = lambda which: pl.BlockSpec((N_DEV, tr, c), lambda l, i: (0, jnp.where(l == which, i, 0), 0))
    par = lambda: pl.BlockSpec((None, tr, c), lambda l, i: (l, i, 0))
    out = jax.ShapeDtypeStruct(w.shape, F32)
    return _pcall(
        body, name=name, grid=(2, n_r),
        in_specs=[part(0), part(1), par(), par(), par()],
        out_specs=(par(), par(), par(), par()),
        out_shape=(out, out, out, out),
        compiler_params=_params(("arbitrary", "arbitrary")),
    )(parts0, parts1, w, m, v)


def _adamw_small(parts, w, m, v, name):
    def body(p_ref, w_ref, m_ref, v_ref, g_ref, d_ref, mo_ref, vo_ref):
        g = _sum_partials(p_ref)
        g_ref[...] = g
        d_ref[...], mo_ref[...], vo_ref[...] = _adamw_math(g, w_ref[...], m_ref[...], v_ref[...])

    out = jax.ShapeDtypeStruct(w.shape, F32)
    return _pcall(body, name=name, out_shape=(out, out, out, out), compiler_params=_params())(parts, w, m, v)


def _adamw_plain(g, w, m, v, name):
    def body(g_ref, w_ref, m_ref, v_ref, d_ref, mo_ref, vo_ref):
        d_ref[...], mo_ref[...], vo_ref[...] = _adamw_math(g_ref[...], w_ref[...], m_ref[...], v_ref[...])

    out = jax.ShapeDtypeStruct(w.shape, F32)
    return _pcall(body, name=name, out_shape=(out, out, out), compiler_params=_params())(g, w, m, v)


def _rows128(a):
    return a.reshape(-1, LANES)


SMALL_NAMES = ("pre_norm_g", "pool_w", "pool_scale", "conv_w", "conv_b", "post_norm_g")


def kernel(x, pre_norm_g, w_in, pool_w, pool_scale, conv_w, conv_b, w_branch, w_out, post_norm_g, loss_target, m_pre_norm_g, m_w_in, m_pool_w, m_pool_scale, m_conv_w, m_conv_b, m_w_branch, m_w_out, m_post_norm_g, v_pre_norm_g, v_w_in, v_pool_w, v_pool_scale, v_conv_w, v_conv_b, v_w_branch, v_w_out, v_post_norm_g):
    s = x.shape[1]
    me = 4 * lax.axis_index("x") + 2 * lax.axis_index("y") + lax.axis_index("c")
    x0 = x[0]
    target = loss_target[0]
    conv_cols = conv_w.shape[-1]

    conv_w_pad = jnp.pad(conv_w.reshape(2 * 3, conv_cols), ((0, 2), (0, LANES - conv_cols)))
    w_in_all = [None, None]
    w_in_all[0], cw_g = _gather_two_level([w_in[0].astype(BF16), conv_w_pad], "gather_w_in_0")
    conv_w_full = cw_g[:, :6, :conv_cols].reshape(N_DEV, 2, 3, conv_cols).transpose(1, 2, 0, 3).reshape(2, 3, WIDTH)
    later_weights = ([w_in[1].astype(BF16), w_branch.astype(BF16), w_out.astype(BF16)], True)

    saved = []
    xin = x0
    for l in range(2):
        u, h = _in_proj_fwd(xin, pre_norm_g[l:l + 1], w_in_all[l], f"in_proj_fwd_{l}")
        y_pool = _pool_fwd(u, pool_w[l], pool_scale[l:l + 1], f"pool_fwd_{l}")
        y_conv = _conv_fwd(u, conv_w_full[l], conv_b[l:l + 1], f"conv_fwd_{l}")
        if l == 0:
            o_sb, y_sb, w_in_all[1], wb_g, wo_all = _sb_fwd(u, f"sb_fwd_{l}", later_weights)
            wb_all = wb_g.transpose(1, 2, 3, 0, 4).reshape(2, 3, WIDTH, D_MODEL)
        else:
            o_sb, y_sb = _sb_fwd(u, f"sb_fwd_{l}")
        xout, merged, pre = _merge_out_fwd(y_pool, y_conv, y_sb, u, wb_all, wo_all, xin, post_norm_g[l:l + 1], l,
                                           f"merge_out_fwd_{l}")
        saved.append((xin, u, h, y_pool, y_conv, y_sb, o_sb, merged, pre))
        xin = xout

    dy, loss_row = _loss_and_grad(xin, target, "loss")

    small = [None, None]
    recv = [None, None]
    ready = []
    for l in (1, 0):
        xl, u, h, y_pool, y_conv, y_sb, o_sb, merged, pre = saved[l]
        dmerged, dwo, dg_post = _out_proj_bwd(dy, pre, post_norm_g[l:l + 1], merged, wo_all, l, f"out_proj_bwd_{l}")
        du_merge, dyp, dyc, dys, dwb = _merge_bwd(dmerged, y_pool, y_conv, y_sb, u, wb_all, l, f"merge_bwd_{l}")
        dwb = dwb.reshape(N_DEV, 3 * WIDTH, D_MODEL // N_DEV)
        dwo = dwo.reshape(N_DEV, D_MODEL // N_DEV, D_MODEL)
        if l == 1:
            dq, dk, dv, dsg = _sb_bwd(u, o_sb, dys, f"sb_bwd_{l}")
        else:
            dq, dk, dv, dsg, *got = _sb_bwd(u, o_sb, dys, f"sb_bwd_{l}", (ready + [dwb, dwo], False))
            recv[1] = got[:3]
        dxc, dgb, dgc, dcg, dcw, dcb = _conv_bwd(u, conv_w_full[l], conv_b[l:l + 1], dyc, f"conv_bwd_{l}")
        dpv, dpg, dpw, dps = _pool_bwd(u, pool_w[l], pool_scale[l:l + 1], dyp, f"pool_bwd_{l}")
        du = jnp.concatenate([dpv, dpg, dxc, dgb, dgc, dcg, dq, dk.astype(BF16), dv.astype(BF16), dsg, du_merge],
                             axis=1)
        dwi = _in_proj_bwd_w(h, du, f"in_proj_bwd_w_{l}")
        small[l] = dict(pool_w=dpw, pool_scale=dps, conv_w=dcw, conv_b=dcb, post_norm_g=dg_post)
        if l == 1:
            ready = [dwi, dwb, dwo]
            dx, dg_pre = _in_proj_bwd_x(du, w_in_all[l], xl, pre_norm_g[l:l + 1], dy, f"in_proj_bwd_x_{l}")
            small[l]["pre_norm_g"] = dg_pre
        else:
            small[l]["pre_norm_g"] = jnp.zeros((1, D_MODEL), F32)
            packed = jnp.concatenate(
                [_rows128(jnp.stack([small[0][n], small[1][n]])) for n in SMALL_NAMES]
                + [jnp.pad(loss_row, ((0, 7), (0, 0)))], axis=0)
            dx, dg_pre, got_dwi, packed_all = _in_proj_bwd_x(
                du, w_in_all[l], xl, pre_norm_g[l:l + 1], dy, f"in_proj_bwd_x_{l}", ([dwi, packed], (False, True)))
            recv[0] = [got_dwi] + got[3:]
        dy = dx
    grad_x = dy[None]

    (g_pre_0_all,) = _exchange([_rows128(dg_pre)], True, "gather_g_pre_0")
    packed_all = lax.dynamic_update_slice(packed_all, g_pre_0_all, (0, 0, 0))
    sizes = dict(pre_norm_g=16, pool_w=1024, pool_scale=8, conv_w=24, conv_b=8, post_norm_g=16)
    n_rows = sum(sizes.values())
    loss = jnp.sum(packed_all[:, n_rows, 0])

    given = dict(pre_norm_g=(pre_norm_g, m_pre_norm_g, v_pre_norm_g), pool_w=(pool_w, m_pool_w, v_pool_w),
                 pool_scale=(pool_scale, m_pool_scale, v_pool_scale), conv_b=(conv_b, m_conv_b, v_conv_b),
                 post_norm_g=(post_norm_g, m_post_norm_g, v_post_norm_g))
    zeros_cw = jnp.zeros((sizes["conv_w"], LANES), F32)
    pack3 = [jnp.concatenate([zeros_cw if n == "conv_w" else _rows128(given[n][k]) for n in SMALL_NAMES], axis=0)
             for k in range(3)]
    sg, sd, sm, sv = _adamw_small(packed_all[:, :n_rows], pack3[0], pack3[1], pack3[2], "adamw_small")

    def unpack(buf, name, shape):
        start = 0
        for n in SMALL_NAMES:
            if n == name:
                return buf[start:start + sizes[n]].reshape(shape)
            start += sizes[n]

    out = {}
    for n in ("pre_norm_g", "pool_w", "pool_scale", "conv_b", "post_norm_g"):
        shape = given[n][0].shape
        out[n] = tuple(unpack(b, n, shape) for b in (sg, sd, sm, sv))
    g_cw = lax.dynamic_slice_in_dim(unpack(sg, "conv_w", (2, 3, WIDTH)), me * conv_cols, conv_cols, axis=2)
    cw2 = lambda a: a.reshape(6, conv_cols)
    d_cw, m_cw, v_cw = _adamw_plain(cw2(g_cw), cw2(conv_w), cw2(m_conv_w), cw2(v_conv_w), "adamw_conv_w")
    out["conv_w"] = (g_cw,) + tuple(a.reshape(2, 3, conv_cols) for a in (d_cw, m_cw, v_cw))

    out["w_in"] = _adamw_layers(recv[0][0], recv[1][0], w_in, m_w_in, v_w_in, "adamw_w_in")
    cols = D_MODEL // N_DEV
    wb3 = lambda a: a.reshape(2, 3 * WIDTH, cols)
    out["w_branch"] = tuple(a.reshape(2, 3, WIDTH, cols) for a in _adamw_layers(
        recv[0][1], recv[1][1], wb3(w_branch), wb3(m_w_branch), wb3(v_w_branch), "adamw_w_branch"))
    out["w_out"] = _adamw_layers(recv[0][2], recv[1][2], w_out, m_w_out, v_w_out, "adamw_w_out")

    order = ("pre_norm_g", "w_in", "pool_w", "pool_scale", "conv_w", "conv_b", "w_branch", "w_out", "post_norm_g")
    return (loss, grad_x) + tuple(out[n][k] for k in range(4) for n in order)
```

```python
import functools

import jax
import jax.numpy as jnp
from jax import lax
from jax.experimental import pallas as pl
from jax.experimental.pallas import tpu as pltpu

F32 = jnp.float32
BF16 = jnp.bfloat16

N_DEV = 8
D_MODEL = 1024
WIDTH = 512
N_IN = 8192
COLS_PER_DEV = N_IN // N_DEV
HEAD_DIM = 64
LANES = 128
SB_SCALE = HEAD_DIM ** -0.5
RMS_EPS = 1e-6
POOL_HALO = 16
CONV_HALO = 8
ADAM_LR, ADAM_B1, ADAM_B2, ADAM_EPS, ADAM_WD, ADAM_STEP = 0.001, 0.9, 0.999, 1e-08, 0.01, 10
VMEM_LIMIT = 60 * 1024 * 1024

CB_POOL_V, CB_POOL_G = 0, 4
CB_CONV_X, CB_CONV_GB, CB_CONV_GC, CB_CONV_G = 8, 12, 16, 20
CB_SB_Q, CB_SB_K, CB_SB_V, CB_SB_G = 24, 28, 32, 36
MERGE_BLOCK_1024 = 5


def _pcall(body, **kw):
    return pl.pallas_call(body, **kw)


def _params(sem=None):
    if sem is None:
        return pltpu.CompilerParams(vmem_limit_bytes=VMEM_LIMIT)
    return pltpu.CompilerParams(dimension_semantics=sem, vmem_limit_bytes=VMEM_LIMIT)


def _sigmoid(x):
    return 1.0 / (1.0 + jnp.exp(-x))


def _dot(a, b):
    return jnp.dot(a, b, preferred_element_type=F32)


def _dot_nt(a, b):
    return lax.dot_general(a, b, (((1,), (1,)), ((), ())), preferred_element_type=F32)


def _dot_tn(a, b):
    return lax.dot_general(a, b, (((0,), (0,)), ((), ())), preferred_element_type=F32)


def _split_bf16(x):
    hi = x.astype(BF16)
    lo = (x - hi.astype(F32)).astype(BF16)
    return hi, lo


N_PEER = N_DEV - 1
ANY_SPEC = pl.BlockSpec(memory_space=pl.ANY)


def _exchange_copies(ins, outs, send_sems, recv_sems, local_sems, gather, with_recvs=True):
    n = len(ins)
    gathers = _per_array(gather, n)
    x, y, c = lax.axis_index("x"), lax.axis_index("y"), lax.axis_index("c")
    me = 4 * x + 2 * y + c
    flip = lambda v, bit: 1 - v if bit else v
    local, sends, recvs = [], [], []
    for a in range(n):
        src = ins[a] if gathers[a] else ins[a].at[me]
        local.append(pltpu.make_async_copy(src, outs[a].at[me], local_sems.at[a]))
    for k in range(N_PEER):
        px, py, pc = flip(x, ((k + 1) >> 2) & 1), flip(y, ((k + 1) >> 1) & 1), flip(c, (k + 1) & 1)
        peer_id = 4 * px + 2 * py + pc
        for a in range(n):
            src = ins[a] if gathers[a] else ins[a].at[peer_id]
            common = dict(src_ref=src, send_sem=send_sems.at[a * N_PEER + k], recv_sem=recv_sems.at[a * N_PEER + k],
                          device_id=(px, py, pc), device_id_type=pl.DeviceIdType.MESH)
            sends.append(pltpu.make_async_remote_copy(dst_ref=outs[a].at[me], **common))
            if with_recvs:
                recvs.append(pltpu.make_async_remote_copy(dst_ref=outs[a].at[peer_id], **common))
    return local, sends, recvs


def _exchange_start(ins, outs, sems, gather):
    local, sends, _ = _exchange_copies(ins, outs, *sems, gather, with_recvs=False)
    for cp in local + sends:
        cp.start()


def _exchange_wait(ins, outs, sems, gather):
    local, sends, recvs = _exchange_copies(ins, outs, *sems, gather)
    for cp in recvs:
        cp.wait_recv()
    for cp in sends:
        cp.wait_send()
    for cp in local:
        cp.wait()


def _per_array(gather, n):
    return tuple(gather) if isinstance(gather, (tuple, list)) else (gather,) * n


def _exchange_out_shapes(arrs, gather):
    return [jax.ShapeDtypeStruct((N_DEV,) + tuple(a.shape if g else a.shape[1:]), a.dtype)
            for a, g in zip(arrs, _per_array(gather, len(arrs)))]


def _gather_two_level(arrs, name):
    n = len(arrs)

    def body(*refs):
        ins, outs = refs[:n], refs[n:2 * n]
        send_sems, recv_sems, local_sems = refs[2 * n:]
        x, y, c = lax.axis_index("x"), lax.axis_index("y"), lax.axis_index("c")
        me, sibling = (x, y, c), (x, y, 1 - c)
        chips = [(1 - x, y), (x, 1 - y), (1 - x, 1 - y)]
        slot = lambda dev: 4 * dev[0] + 2 * dev[1] + dev[2]

        def copy(a, k, block, to, src=None):
            return pltpu.make_async_remote_copy(
                src_ref=outs[a].at[slot(block)] if src is None else src, dst_ref=outs[a].at[slot(block)],
                send_sem=send_sems.at[a * N_PEER + k], recv_sem=recv_sems.at[a * N_PEER + k],
                device_id=to, device_id_type=pl.DeviceIdType.MESH)

        local = [pltpu.make_async_copy(ins[a], outs[a].at[slot(me)], local_sems.at[a]) for a in range(n)]
        first = []
        for a in range(n):
            first.append(copy(a, 0, me, sibling, src=ins[a]))
            first += [copy(a, 1 + j, me, (*chip, c), src=ins[a]) for j, chip in enumerate(chips)]
        for cp in local + first:
            cp.start()
        passed = []
        for j, chip in enumerate(chips):
            for a in range(n):
                copy(a, 1 + j, (*chip, c), me).wait_recv()
                passed.append(copy(a, 4 + j, (*chip, c), sibling))
                passed[-1].start()
        for a in range(n):
            copy(a, 0, sibling, me).wait_recv()
        for j, chip in enumerate(chips):
            for a in range(n):
                copy(a, 4 + j, (*chip, 1 - c), me).wait_recv()
        for cp in first + passed:
            cp.wait_send()
        for cp in local:
            cp.wait()

    return _pcall(
        body, name=name,
        out_shape=tuple(_exchange_out_shapes(arrs, True)),
        in_specs=[ANY_SPEC] * n, out_specs=tuple([ANY_SPEC] * n),
        scratch_shapes=_exchange_sems(n),
    )(*arrs)


def _exchange_sems(n):
    return [pltpu.SemaphoreType.DMA((n * N_PEER,)), pltpu.SemaphoreType.DMA((n * N_PEER,)),
            pltpu.SemaphoreType.DMA((n,))]


def _exchange(arrs, gather, name):
    n = len(arrs)

    def body(*refs):
        ins, outs, sems = refs[:n], refs[n:2 * n], refs[2 * n:]
        _exchange_start(ins, outs, sems, gather)
        _exchange_wait(ins, outs, sems, gather)

    return _pcall(
        body, name=name,
        out_shape=tuple(_exchange_out_shapes(arrs, gather)),
        in_specs=[ANY_SPEC] * n, out_specs=tuple([ANY_SPEC] * n),
        scratch_shapes=_exchange_sems(n),
    )(*arrs)


def _in_proj_fwd(x, g, w_all, name):
    s = x.shape[0]
    tm = min(1024, s)

    def body(x_ref, g_ref, w_ref, u_ref, h_ref, hs):
        @pl.when(pl.program_id(1) == 0)
        def _():
            xv = x_ref[...]
            r = lax.rsqrt(jnp.mean(xv * xv, axis=-1, keepdims=True) + RMS_EPS)
            hv = (xv * r * g_ref[...]).astype(BF16)
            hs[...] = hv
            h_ref[...] = hv
        u_ref[...] = _dot(hs[...], w_ref[...])

    return _pcall(
        body, name=name, grid=(s // tm, N_DEV),
        in_specs=[pl.BlockSpec((tm, D_MODEL), lambda i, j: (i, 0)),
                  pl.BlockSpec((1, D_MODEL), lambda i, j: (0, 0)),
                  pl.BlockSpec((None, D_MODEL, COLS_PER_DEV), lambda i, j: (j, 0, 0))],
        out_specs=(pl.BlockSpec((tm, COLS_PER_DEV), lambda i, j: (i, j)),
                   pl.BlockSpec((tm, D_MODEL), lambda i, j: (i, 0))),
        out_shape=(jax.ShapeDtypeStruct((s, N_IN), F32), jax.ShapeDtypeStruct((s, D_MODEL), BF16)),
        scratch_shapes=[pltpu.VMEM((tm, D_MODEL), BF16)],
        compiler_params=_params(("parallel", "arbitrary")),
    )(x, g, w_all)


def _pool_window(vs, t0, t, grp):
    ext = vs[pl.ds(t0, t + POOL_HALO), :]
    s2 = ext + pltpu.roll(ext, 1, 0)
    s4 = s2 + pltpu.roll(s2, 2, 0)
    s8 = s4 + pltpu.roll(s4, 4, 0)
    s16 = s8 + pltpu.roll(s8, 8, 0)
    sel = jnp.where(grp == 0, s2, jnp.where(grp == 1, s4, jnp.where(grp == 2, s8, s16)))
    return sel[POOL_HALO:, :], ext[POOL_HALO:, :]


def _pool_count(t0, t, grp):
    pos = t0 + lax.broadcasted_iota(jnp.int32, (t, 1), 0)
    return jnp.minimum(pos + 1, jnp.left_shift(2, grp)).astype(F32)


def _pool_fwd(u, pool_w, pool_scale, name):
    s = u.shape[0]
    t = min(256, s)

    def body(pv_ref, pg_ref, w_ref, sc_ref, y_ref, vs):
        grp = pl.program_id(0)
        vs[0:POOL_HALO, :] = jnp.zeros((POOL_HALO, LANES), F32)
        vs[POOL_HALO:, :] = pv_ref[...]
        wb = w_ref[...].astype(BF16)
        scale = sc_ref[...]

        def tile(i, carry):
            t0 = pl.multiple_of(i * t, t)
            win, v = _pool_window(vs, t0, t, grp)
            pooled = win / _pool_count(t0, t, grp) - v
            mixed = _dot(pooled.astype(BF16), wb)
            gate = pg_ref[pl.ds(t0, t), :]
            y_ref[pl.ds(t0, t), :] = (mixed * scale * (gate * _sigmoid(gate))).astype(BF16)
            return carry

        lax.fori_loop(0, s // t, tile, 0)

    return _pcall(
        body, name=name, grid=(4,),
        in_specs=[pl.BlockSpec((s, LANES), lambda g: (0, CB_POOL_V + g)),
                  pl.BlockSpec((s, LANES), lambda g: (0, CB_POOL_G + g)),
                  pl.BlockSpec((None, LANES, LANES), lambda g: (g, 0, 0)),
                  pl.BlockSpec((1, LANES), lambda g: (0, g))],
        out_specs=pl.BlockSpec((s, LANES), lambda g: (0, g)),
        out_shape=jax.ShapeDtypeStruct((s, WIDTH), BF16),
        scratch_shapes=[pltpu.VMEM((POOL_HALO + s, LANES), F32)],
        compiler_params=_params(("arbitrary",)),
    )(u, u, pool_w, pool_scale)


def _conv_taps(zs, t0, t):
    ext = zs[pl.ds(t0, t + CONV_HALO), :]
    z0 = ext[CONV_HALO:, :]
    z1 = pltpu.roll(ext, 1, 0)[CONV_HALO:, :]
    z2 = pltpu.roll(ext, 2, 0)[CONV_HALO:, :]
    return z0, z1, z2


def _conv_fwd(u, conv_w, conv_b, name):
    s = u.shape[0]
    t = min(256, s)

    def body(xc_ref, gb_ref, gc_ref, cg_ref, w_ref, b_ref, y_ref, zs):
        zs[0:CONV_HALO, :] = jnp.zeros((CONV_HALO, LANES), F32)
        zs[CONV_HALO:, :] = gc_ref[...] * xc_ref[...]
        w0, w1, w2 = w_ref[0:1, :], w_ref[1:2, :], w_ref[2:3, :]
        bias = b_ref[...]

        def tile(i, carry):
            t0 = pl.multiple_of(i * t, t)
            z0, z1, z2 = _conv_taps(zs, t0, t)
            conv = w0 * z2 + w1 * z1 + w2 * z0
            gate = cg_ref[pl.ds(t0, t), :]
            y = gb_ref[pl.ds(t0, t), :] * (conv + bias) * (gate * _sigmoid(gate))
            y_ref[pl.ds(t0, t), :] = y.astype(BF16)
            return carry

        lax.fori_loop(0, s // t, tile, 0)

    col = lambda base: pl.BlockSpec((s, LANES), lambda j: (0, base + j))
    return _pcall(
        body, name=name, grid=(4,),
        in_specs=[col(CB_CONV_X), col(CB_CONV_GB), col(CB_CONV_GC), col(CB_CONV_G),
                  pl.BlockSpec((3, LANES), lambda j: (0, j)),
                  pl.BlockSpec((1, LANES), lambda j: (0, j))],
        out_specs=pl.BlockSpec((s, LANES), lambda j: (0, j)),
        out_shape=jax.ShapeDtypeStruct((s, WIDTH), BF16),
        scratch_shapes=[pltpu.VMEM((CONV_HALO + s, LANES), F32)],
        compiler_params=_params(("arbitrary",)),
    )(u, u, u, u, conv_w, conv_b)


def _first_head_lanes(rows, width=LANES):
    lane = lax.broadcasted_iota(jnp.int32, (rows, width), 1)
    return jnp.bitwise_and(lane, LANES - 1) < HEAD_DIM


def _stack_heads(x, first):
    zero = jnp.zeros_like(x)
    return jnp.concatenate([jnp.where(first, x, zero), jnp.where(first, zero, x)], axis=0).astype(BF16)


def _causal_mask(tq, tk, copies):
    row = lax.broadcasted_iota(jnp.int32, (tq, tk), 0)
    col = lax.broadcasted_iota(jnp.int32, (tq, tk), 1)
    return jnp.concatenate([col < row] * copies, axis=0)


def _suffix_matrix(tk, inclusive):
    r = lax.broadcasted_iota(jnp.int32, (2 * tk, 2 * tk), 0)
    c = lax.broadcasted_iota(jnp.int32, (2 * tk, 2 * tk), 1)
    r = jnp.where(r >= tk, r - tk, r)
    tri = (r >= c) if inclusive else (r > c)
    return jnp.where(c >= tk, 1.0, jnp.where(tri, 1.0, 0.0)).astype(BF16)


def _suffix_sums(x, m):
    hi, lo = _split_bf16(x)
    return _dot(jnp.concatenate([hi, lo], axis=1), m)


def _sb_log_terms(z, mask, m_strict):
    ls = jnp.minimum(z, 0.0) - jnp.log(1.0 + jnp.exp(-jnp.abs(z)))
    lk = ls - z
    if mask is not None:
        lk = jnp.where(mask, lk, 0.0)
    return ls, _suffix_sums(lk, m_strict)


SB_PAIRS = 4


def _pair_lanes(a):
    return slice(a * LANES, (a + 1) * LANES)


def _sb_fwd(u, name, xchg=None):
    s = u.shape[0]
    tq = tk = min(128, s)
    pairs = SB_PAIRS
    width = pairs * LANES
    rows = 2 * pairs * tq
    x_arrs, x_gather = xchg if xchg else ((), True)
    n_x = len(x_arrs)
    grid = (4 // pairs, s // tq)

    def body(*refs):
        q_ref, k_ref, v_ref, g_ref = refs[:4]
        x_in, refs = refs[4:4 + n_x], refs[4 + n_x:]
        o_ref, y_ref = refs[:2]
        x_out, refs = refs[2:2 + n_x], refs[2 + n_x:]
        kbf, vst, z_s, ell_s, carry_s = refs[:5]
        x_sems = refs[5:]
        i = pl.program_id(1)
        if n_x:
            @pl.when((pl.program_id(0) == 0) & (i == 0))
            def _():
                _exchange_start(x_in, x_out, x_sems, x_gather)

        @pl.when(i == 0)
        def _():
            kbf[...] = k_ref[...].astype(BF16)
            first_s = _first_head_lanes(s, width)
            vf = v_ref[...]
            vst[0] = jnp.where(first_s, vf, 0.0).astype(BF16)
            vst[1] = jnp.where(first_s, 0.0, vf).astype(BF16)

        first = _first_head_lanes(tq)
        mask = _causal_mask(tq, tk, 2 * pairs)
        m_strict = _suffix_matrix(tk, False)
        qcat = jnp.concatenate([_stack_heads(q_ref[:, _pair_lanes(a)] * SB_SCALE, first) for a in range(pairs)],
                               axis=0)

        def scores(b):
            off = pl.multiple_of(jnp.maximum(b, 0) * tk, tk)
            z_s[...] = jnp.concatenate(
                [_dot_nt(qcat[a * 2 * tq:(a + 1) * 2 * tq], kbf[pl.ds(off, tk), _pair_lanes(a)])
                 for a in range(pairs)], axis=0)

        def log_weights(m):
            ls, cs = _sb_log_terms(z_s[...], m, m_strict)
            carry = carry_s[...]
            ell_s[...] = ls + cs[:, :tk] + carry
            carry_s[...] = carry + cs[:, tk:]

        def consume(b, accs, m):
            w = jnp.exp(ell_s[...])
            if m is not None:
                w = jnp.where(m, w, 0.0)
            wb = w.astype(BF16)
            off = pl.multiple_of(b * tk, tk)
            new = []
            for a in range(pairs):
                r0 = a * 2 * tq
                wcat = jnp.concatenate([wb[r0:r0 + tq], wb[r0 + tq:r0 + 2 * tq]], axis=1)
                vcat = jnp.concatenate([vst[0, pl.ds(off, tk), _pair_lanes(a)], vst[1, pl.ds(off, tk), _pair_lanes(a)]],
                                       axis=0)
                new.append(accs[a] + _dot(wcat, vcat))
            return tuple(new)

        carry_s[...] = jnp.zeros((rows, tk), F32)
        scores(i)
        log_weights(mask)
        scores(i - 1)
        accs = consume(i, tuple(jnp.zeros((tq, LANES), F32) for _ in range(pairs)), mask)
        log_weights(None)
        scores(i - 2)

        def step(n, accs):
            accs = consume(i - n, accs, None)
            log_weights(None)
            scores(i - n - 2)
            return accs

        accs = lax.fori_loop(1, i + 1, step, accs)
        o = jnp.concatenate(accs, axis=1)
        o_ref[...] = o
        gate = g_ref[...]
        y_ref[...] = (o * (gate * _sigmoid(gate))).astype(BF16)
        if n_x:
            @pl.when((pl.program_id(0) == grid[0] - 1) & (i == grid[1] - 1))
            def _():
                _exchange_wait(x_in, x_out, x_sems, x_gather)

    base = lambda cb: cb // pairs
    qblk = lambda cb: pl.BlockSpec((tq, width), lambda p, i: (i, base(cb) + p))
    full = lambda cb: pl.BlockSpec((s, width), lambda p, i: (0, base(cb) + p), pipeline_mode=pl.Buffered(1))
    state = pltpu.VMEM((rows, tk), F32)
    return _pcall(
        body, name=name, grid=grid,
        in_specs=[qblk(CB_SB_Q), full(CB_SB_K), full(CB_SB_V), qblk(CB_SB_G)] + [ANY_SPEC] * n_x,
        out_specs=(qblk(0), qblk(0)) + (ANY_SPEC,) * n_x,
        out_shape=(jax.ShapeDtypeStruct((s, WIDTH), F32), jax.ShapeDtypeStruct((s, WIDTH), BF16))
        + tuple(_exchange_out_shapes(x_arrs, x_gather)),
        scratch_shapes=[pltpu.VMEM((s, width), BF16), pltpu.VMEM((2, s, width), BF16), state, state, state]
        + (_exchange_sems(n_x) if n_x else []),
        compiler_params=_params(("arbitrary", "arbitrary")),
    )(u, u, u, u, *x_arrs)


def _merge_out_fwd(y_pool, y_conv, y_sb, u, wb_all, wo_all, x, g_post, layer, name):
    s = x.shape[0]
    tm = min(256, s)

    def body(yp, yc, ys, m0, m1, m2, wb_ref, wo_ref, x_ref, g_ref, out_ref, merged_ref, pre_ref):
        merged = jnp.zeros((tm, D_MODEL), F32)
        for n, (y_ref, m_ref) in enumerate(((yp, m0), (yc, m1), (ys, m2))):
            merged = merged + _sigmoid(m_ref[...]) * _dot(y_ref[...], wb_ref[n])
        mb = merged.astype(BF16)
        merged_ref[...] = mb
        pre = _dot(mb, wo_ref[...].reshape(D_MODEL, D_MODEL))
        pre_ref[...] = pre
        r = lax.rsqrt(jnp.mean(pre * pre, axis=-1, keepdims=True) + RMS_EPS)
        out_ref[...] = x_ref[...] + pre * r * g_ref[...]

    rows = lambda w: pl.BlockSpec((tm, w), lambda i: (i, 0))
    merge = lambda n: pl.BlockSpec((tm, D_MODEL), lambda i: (i, MERGE_BLOCK_1024 + n))
    return _pcall(
        body, name=name, grid=(s // tm,),
        in_specs=[rows(WIDTH), rows(WIDTH), rows(WIDTH), merge(0), merge(1), merge(2),
                  pl.BlockSpec((None, 3, WIDTH, D_MODEL), lambda i: (layer, 0, 0, 0)),
                  pl.BlockSpec((N_DEV, None, D_MODEL // N_DEV, D_MODEL), lambda i: (0, layer, 0, 0)),
                  rows(D_MODEL), pl.BlockSpec((1, D_MODEL), lambda i: (0, 0))],
        out_specs=(rows(D_MODEL), rows(D_MODEL), rows(D_MODEL)),
        out_shape=(jax.ShapeDtypeStruct((s, D_MODEL), F32), jax.ShapeDtypeStruct((s, D_MODEL), BF16),
                   jax.ShapeDtypeStruct((s, D_MODEL), F32)),
        compiler_params=_params(("arbitrary",)),
    )(y_pool, y_conv, y_sb, u, u, u, wb_all, wo_all, x, g_post)


def _loss_and_grad(y, target, name):
    s = y.shape[0]
    tm = min(512, s)

    def body(y_ref, t_ref, dy_ref, loss_ref, acc):
        i = pl.program_id(0)

        @pl.when(i == 0)
        def _():
            acc[...] = jnp.zeros_like(acc)
        err = y_ref[...] - t_ref[...]
        dy_ref[...] = err / D_MODEL
        acc[...] += jnp.sum(err * err, axis=0, keepdims=True)

        @pl.when(i == pl.num_programs(0) - 1)
        def _():
            total = jnp.sum(acc[...], axis=1, keepdims=True) * (0.5 / D_MODEL)
            loss_ref[...] = jnp.broadcast_to(total, (1, LANES))

    return _pcall(
        body, name=name, grid=(s // tm,),
        in_specs=[pl.BlockSpec((tm, D_MODEL), lambda i: (i, 0)), pl.BlockSpec((tm, D_MODEL), lambda i: (i, 0))],
        out_specs=(pl.BlockSpec((tm, D_MODEL), lambda i: (i, 0)), pl.BlockSpec((1, LANES), lambda i: (0, 0))),
        out_shape=(jax.ShapeDtypeStruct((s, D_MODEL), F32), jax.ShapeDtypeStruct((1, LANES), F32)),
        scratch_shapes=[pltpu.VMEM((1, D_MODEL), F32)],
        compiler_params=_params(("arbitrary",)),
    )(y, target)


def _out_proj_bwd(dy, pre, g_post, merged, wo_all, layer, name):
    s = dy.shape[0]
    tm = min(256, s)
    n_tiles = s // tm

    def body(dy_ref, pre_ref, g_ref, mg_ref, wo_ref, dm_ref, dwo_ref, dg_ref, acc):
        i = pl.program_id(0)

        @pl.when(i == 0)
        def _():
            acc[...] = jnp.zeros_like(acc)
            dg_ref[...] = jnp.zeros_like(dg_ref)
        dyv, pre_v = dy_ref[...], pre_ref[...]
        r = lax.rsqrt(jnp.mean(pre_v * pre_v, axis=-1, keepdims=True) + RMS_EPS)
        dg_ref[...] += jnp.sum(dyv * pre_v * r, axis=0, keepdims=True)
        a = dyv * g_ref[...]
        dpre = r * a - pre_v * (r * r * r) * jnp.mean(a * pre_v, axis=-1, keepdims=True)
        db = dpre.astype(BF16)
        acc[...] += _dot_tn(mg_ref[...], db)
        dm_ref[...] = _dot_nt(db, wo_ref[...].reshape(D_MODEL, D_MODEL))

        @pl.when(i == n_tiles - 1)
        def _():
            dwo_ref[...] = acc[...].astype(BF16)

    rows = lambda: pl.BlockSpec((tm, D_MODEL), lambda i: (i, 0))
    return _pcall(
        body, name=name, grid=(n_tiles,),
        in_specs=[rows(), rows(), pl.BlockSpec((1, D_MODEL), lambda i: (0, 0)), rows(),
                  pl.BlockSpec((N_DEV, None, D_MODEL // N_DEV, D_MODEL), lambda i: (0, layer, 0, 0))],
        out_specs=(rows(), pl.BlockSpec((D_MODEL, D_MODEL), lambda i: (0, 0)),
                   pl.BlockSpec((1, D_MODEL), lambda i: (0, 0))),
        out_shape=(jax.ShapeDtypeStruct((s, D_MODEL), F32), jax.ShapeDtypeStruct((D_MODEL, D_MODEL), BF16),
                   jax.ShapeDtypeStruct((1, D_MODEL), F32)),
        scratch_shapes=[pltpu.VMEM((D_MODEL, D_MODEL), F32)],
        compiler_params=_params(("arbitrary",)),
    )(dy, pre, g_post, merged, wo_all)


def _merge_bwd(dmerged, y_pool, y_conv, y_sb, u, wb_all, layer, name):
    s = dmerged.shape[0]
    tm = min(256, s)
    n_tiles = s // tm
    cols = D_MODEL // N_DEV

    def body(dm_ref, yp, yc, ys, m0, m1, m2, wb_ref, dum_ref, dyp, dyc, dys, dwb_ref, acc):
        i = pl.program_id(0)

        @pl.when(i == 0)
        def _():
            acc[...] = jnp.zeros_like(acc)
        dm = dm_ref[...]
        for n, (y_ref, m_ref, dy_ref) in enumerate(((yp, m0, dyp), (yc, m1, dyc), (ys, m2, dys))):
            yv = y_ref[...]
            wb = wb_ref[n]
            gate = _sigmoid(m_ref[...])
            proj = _dot(yv, wb)
            dum_ref[:, n * D_MODEL:(n + 1) * D_MODEL] = (dm * proj * gate * (1.0 - gate)).astype(BF16)
            dproj = (dm * gate).astype(BF16)
            acc[n] += _dot_tn(yv, dproj)
            dy_ref[...] = _dot_nt(dproj, wb)

        @pl.when(i == n_tiles - 1)
        def _():
            for j in range(N_DEV):
                for n in range(3):
                    dwb_ref[j, n] = acc[n, :, j * cols:(j + 1) * cols].astype(BF16)

    rows = lambda w: pl.BlockSpec((tm, w), lambda i: (i, 0))
    merge = lambda n: pl.BlockSpec((tm, D_MODEL), lambda i: (i, MERGE_BLOCK_1024 + n))
    return _pcall(
        body, name=name, grid=(n_tiles,),
        in_specs=[rows(D_MODEL), rows(WIDTH), rows(WIDTH), rows(WIDTH), merge(0), merge(1), merge(2),
                  pl.BlockSpec((None, 3, WIDTH, D_MODEL), lambda i: (layer, 0, 0, 0))],
        out_specs=(rows(3 * D_MODEL), rows(WIDTH), rows(WIDTH), rows(WIDTH),
                   pl.BlockSpec((N_DEV, 3, WIDTH, cols), lambda i: (0, 0, 0, 0))),
        out_shape=(jax.ShapeDtypeStruct((s, 3 * D_MODEL), BF16),
                   jax.ShapeDtypeStruct((s, WIDTH), F32), jax.ShapeDtypeStruct((s, WIDTH), F32),
                   jax.ShapeDtypeStruct((s, WIDTH), F32),
                   jax.ShapeDtypeStruct((N_DEV, 3, WIDTH, cols), BF16)),
        scratch_shapes=[pltpu.VMEM((3, WIDTH, D_MODEL), F32)],
        compiler_params=_params(("arbitrary",)),
    )(dmerged, y_pool, y_conv, y_sb, u, u, u, wb_all)


def _sb_bwd(u, o, dys, name, xchg=None):
    s = u.shape[0]
    tq = tk = min(128, s)
    pairs = SB_PAIRS
    width = pairs * LANES
    rows = 2 * pairs * tq
    pair_rows = lambda a: slice(a * 2 * tq, (a + 1) * 2 * tq)

    x_arrs, x_gather = xchg if xchg else ((), True)
    n_x = len(x_arrs)
    grid = (4 // pairs, s // tq)

    def body(*refs):
        q_ref, k_ref, v_ref, g_ref, o_ref, dys_ref = refs[:6]
        x_in, refs = refs[6:6 + n_x], refs[6 + n_x:]
        dq_ref, dk_ref, dv_ref, dg_ref = refs[:4]
        x_out, refs = refs[4:4 + n_x], refs[4 + n_x:]
        kbf, vbf, kst, z_s, ell_s, ls_s, cl_s, wb_s, g_s, bef_s, cg_s, beta_s = refs[:12]
        x_sems = refs[12:]
        i = pl.program_id(1)
        if n_x:
            @pl.when((pl.program_id(0) == 0) & (i == 0))
            def _():
                _exchange_start(x_in, x_out, x_sems, x_gather)

        @pl.when(i == 0)
        def _():
            dk_ref[...] = jnp.zeros_like(dk_ref)
            dv_ref[...] = jnp.zeros_like(dv_ref)
            kf = k_ref[...]
            kbf[...] = kf.astype(BF16)
            vbf[...] = v_ref[...].astype(BF16)
            first_s = _first_head_lanes(s, width)
            kst[0] = jnp.where(first_s, kf, 0.0).astype(BF16)
            kst[1] = jnp.where(first_s, 0.0, kf).astype(BF16)

        first = _first_head_lanes(tq)
        mask = _causal_mask(tq, tk, 2 * pairs)
        m_strict = _suffix_matrix(tk, False)
        m_incl = _suffix_matrix(tk, True)

        gate = g_ref[...]
        sg = _sigmoid(gate)
        dy = dys_ref[...]
        ov = o_ref[...]
        dg_ref[...] = (dy * ov * (sg * (1.0 + gate * (1.0 - sg)))).astype(BF16)
        do = (dy * (gate * sg)).astype(BF16)
        prod = do.astype(F32) * ov
        row_sum = lambda v: jnp.broadcast_to(jnp.sum(v, axis=1, keepdims=True), (tq, tk))
        dsum, docat, qcat = [], [], []
        for a in range(pairs):
            pa = prod[:, _pair_lanes(a)]
            dsum += [row_sum(jnp.where(first, pa, 0.0)), row_sum(jnp.where(first, 0.0, pa))]
            docat.append(_stack_heads(do[:, _pair_lanes(a)], first))
            qcat.append(_stack_heads(q_ref[:, _pair_lanes(a)] * SB_SCALE, first))
        dsum = jnp.concatenate(dsum, axis=0)

        def block_start(b):
            return pl.multiple_of(jnp.maximum(b, 0) * tk, tk)

        def scores(b):
            off = block_start(b)
            z_s[...] = jnp.concatenate([_dot_nt(qcat[a], kbf[pl.ds(off, tk), _pair_lanes(a)]) for a in range(pairs)],
                                       axis=0)

        def log_weights(m):
            ls, cs = _sb_log_terms(z_s[...], m, m_strict)
            cl = cl_s[...]
            ell_s[...] = ls + cs[:, :tk] + cl
            cl_s[...] = cl + cs[:, tk:]
            ls_s[...] = ls

        def weights(b, m):
            off = block_start(b)
            dwt = jnp.concatenate([_dot_nt(docat[a], vbf[pl.ds(off, tk), _pair_lanes(a)]) for a in range(pairs)],
                                  axis=0)
            w = jnp.exp(ell_s[...])
            if m is not None:
                w = jnp.where(m, w, 0.0)
            wb = w.astype(BF16)
            g = dwt * wb.astype(F32)
            gs = _suffix_sums(g, m_incl)
            cg = cg_s[...]
            beta = jnp.exp(ls_s[...])
            wb_s[...] = wb
            beta_s[...] = beta
            g_s[...] = g * (1.0 - beta)
            bef_s[...] = gs[:, :tk] + cg
            cg_s[...] = cg + gs[:, tk:]

        def grads(b, dqs, m):
            dz = g_s[...] - beta_s[...] * (dsum - bef_s[...])
            if m is not None:
                dz = jnp.where(m, dz, 0.0)
            dzb = dz.astype(BF16)
            wb = wb_s[...]
            off = pl.multiple_of(b * tk, tk)
            new = []
            for a in range(pairs):
                r0 = a * 2 * tq
                kcat = jnp.concatenate([kst[0, pl.ds(off, tk), _pair_lanes(a)], kst[1, pl.ds(off, tk), _pair_lanes(a)]],
                                       axis=0)
                new.append(dqs[a] + _dot(jnp.concatenate([dzb[r0:r0 + tq], dzb[r0 + tq:r0 + 2 * tq]], axis=1), kcat))
                dk_ref[pl.ds(off, tk), _pair_lanes(a)] += _dot_tn(dzb[pair_rows(a)], qcat[a])
                dv_ref[pl.ds(off, tk), _pair_lanes(a)] += _dot_tn(wb[pair_rows(a)], docat[a])
            return tuple(new)

        zero = jnp.zeros((rows, tk), F32)
        cl_s[...] = zero
        cg_s[...] = zero
        scores(i)
        log_weights(mask)
        scores(i - 1)
        weights(i, mask)
        log_weights(None)
        scores(i - 2)
        dqs = grads(i, tuple(jnp.zeros((tq, LANES), F32) for _ in range(pairs)), mask)
        weights(i - 1, None)
        log_weights(None)
        scores(i - 3)

        def step(n, dqs):
            dqs = grads(i - n, dqs, None)
            weights(i - n - 1, None)
            log_weights(None)
            scores(i - n - 3)
            return dqs

        dqs = lax.fori_loop(1, i + 1, step, dqs)
        dq_ref[...] = (jnp.concatenate(dqs, axis=1) * SB_SCALE).astype(BF16)
        if n_x:
            @pl.when((pl.program_id(0) == grid[0] - 1) & (i == grid[1] - 1))
            def _():
                _exchange_wait(x_in, x_out, x_sems, x_gather)

    base = lambda cb: cb // pairs
    qblk = lambda cb: pl.BlockSpec((tq, width), lambda p, i: (i, base(cb) + p))
    full = lambda cb: pl.BlockSpec((s, width), lambda p, i: (0, base(cb) + p), pipeline_mode=pl.Buffered(1))
    state = pltpu.VMEM((rows, tk), F32)
    return _pcall(
        body, name=name, grid=grid,
        in_specs=[qblk(CB_SB_Q), full(CB_SB_K), full(CB_SB_V), qblk(CB_SB_G), qblk(0), qblk(0)] + [ANY_SPEC] * n_x,
        out_specs=(qblk(0), full(0), full(0), qblk(0)) + (ANY_SPEC,) * n_x,
        out_shape=(jax.ShapeDtypeStruct((s, WIDTH), BF16), jax.ShapeDtypeStruct((s, WIDTH), F32),
                   jax.ShapeDtypeStruct((s, WIDTH), F32), jax.ShapeDtypeStruct((s, WIDTH), BF16))
        + tuple(_exchange_out_shapes(x_arrs, x_gather)),
        scratch_shapes=[pltpu.VMEM((s, width), BF16), pltpu.VMEM((s, width), BF16), pltpu.VMEM((2, s, width), BF16),
                        state, state, state, state, pltpu.VMEM((rows, tk), BF16),
                        state, state, state, state] + (_exchange_sems(n_x) if n_x else []),
        compiler_params=_params(("arbitrary", "arbitrary")),
    )(u, u, u, u, o, dys, *x_arrs)


def _conv_bwd(u, conv_w, conv_b, dyc, name):
    s = u.shape[0]
    t = min(256, s)
    n_tiles = s // t

    def body(xc_ref, gb_ref, gc_ref, cg_ref, w_ref, b_ref, dy_ref,
             dxc_ref, dgb_ref, dgc_ref, dcg_ref, dw_ref, db_ref, zs, ds):
        zs[0:CONV_HALO, :] = jnp.zeros((CONV_HALO, LANES), F32)
        zs[CONV_HALO:, :] = gc_ref[...] * xc_ref[...]
        ds[s:, :] = jnp.zeros((CONV_HALO, LANES), F32)
        w0, w1, w2 = w_ref[0:1, :], w_ref[1:2, :], w_ref[2:3, :]
        bias = b_ref[...]

        def first(i, sums):
            t0 = pl.multiple_of(i * t, t)
            z0, z1, z2 = _conv_taps(zs, t0, t)
            pre = w0 * z2 + w1 * z1 + w2 * z0 + bias
            gate = cg_ref[pl.ds(t0, t), :]
            sg = _sigmoid(gate)
            gb = gb_ref[pl.ds(t0, t), :]
            dy = dy_ref[pl.ds(t0, t), :]
            dcg_ref[pl.ds(t0, t), :] = (dy * gb * pre * (sg * (1.0 + gate * (1.0 - sg)))).astype(BF16)
            dgb_ref[pl.ds(t0, t), :] = (dy * pre * (gate * sg)).astype(BF16)
            dc = dy * gb * (gate * sg)
            ds[pl.ds(t0, t), :] = dc
            red = lambda v: jnp.sum(v, axis=0, keepdims=True)
            return (sums[0] + red(dc * z2), sums[1] + red(dc * z1), sums[2] + red(dc * z0), sums[3] + red(dc))

        zrow = jnp.zeros((1, LANES), F32)
        sw0, sw1, sw2, sb = lax.fori_loop(0, n_tiles, first, (zrow, zrow, zrow, zrow))
        dw_ref[0:1, :] = sw0
        dw_ref[1:2, :] = sw1
        dw_ref[2:3, :] = sw2
        db_ref[...] = sb

        def second(i, carry):
            t0 = pl.multiple_of(i * t, t)
            ext = ds[pl.ds(t0, t + CONV_HALO), :]
            n = t + CONV_HALO
            d0 = ext[:t, :]
            d1 = pltpu.roll(ext, n - 1, 0)[:t, :]
            d2 = pltpu.roll(ext, n - 2, 0)[:t, :]
            dz = w2 * d0 + w1 * d1 + w0 * d2
            dgc_ref[pl.ds(t0, t), :] = (dz * xc_ref[pl.ds(t0, t), :]).astype(BF16)
            dxc_ref[pl.ds(t0, t), :] = (dz * gc_ref[pl.ds(t0, t), :]).astype(BF16)
            return carry

        lax.fori_loop(0, n_tiles, second, 0)

    col = lambda base: pl.BlockSpec((s, LANES), lambda j: (0, base + j))
    dcol = jax.ShapeDtypeStruct((s, WIDTH), BF16)
    return _pcall(
        body, name=name, grid=(4,),
        in_specs=[col(CB_CONV_X), col(CB_CONV_GB), col(CB_CONV_GC), col(CB_CONV_G),
                  pl.BlockSpec((3, LANES), lambda j: (0, j)), pl.BlockSpec((1, LANES), lambda j: (0, j)), col(0)],
        out_specs=(col(0), col(0), col(0), col(0),
                   pl.BlockSpec((3, LANES), lambda j: (0, j)), pl.BlockSpec((1, LANES), lambda j: (0, j))),
        out_shape=(dcol, dcol, dcol, dcol,
                   jax.ShapeDtypeStruct((3, WIDTH), F32), jax.ShapeDtypeStruct((1, WIDTH), F32)),
        scratch_shapes=[pltpu.VMEM((CONV_HALO + s, LANES), F32), pltpu.VMEM((s + CONV_HALO, LANES), F32)],
        compiler_params=_params(("arbitrary",)),
    )(u, u, u, u, conv_w, conv_b, dyc)


def _pool_bwd(u, pool_w, pool_scale, dyp, name):
    s = u.shape[0]
    t = min(256, s)
    n_tiles = s // t

    def body(pv_ref, pg_ref, w_ref, sc_ref, dy_ref, dpv_ref, dpg_ref, dw_ref, dsc_ref, vs, es, dps):
        grp = pl.program_id(0)
        vs[0:POOL_HALO, :] = jnp.zeros((POOL_HALO, LANES), F32)
        vs[POOL_HALO:, :] = pv_ref[...]
        es[s:, :] = jnp.zeros((POOL_HALO, LANES), F32)
        wb = w_ref[...].astype(BF16)
        scale = sc_ref[...]

        def first(i, sums):
            dw, dsc = sums
            t0 = pl.multiple_of(i * t, t)
            win, v = _pool_window(vs, t0, t, grp)
            cnt = _pool_count(t0, t, grp)
            pb = (win / cnt - v).astype(BF16)
            mixed = _dot(pb, wb)
            gate = pg_ref[pl.ds(t0, t), :]
            sg = _sigmoid(gate)
            dy = dy_ref[pl.ds(t0, t), :]
            dpg_ref[pl.ds(t0, t), :] = (dy * (mixed * scale) * (sg * (1.0 + gate * (1.0 - sg)))).astype(BF16)
            dms = dy * (gate * sg)
            dsc = dsc + jnp.sum(dms * mixed, axis=0, keepdims=True)
            dmb = (dms * scale).astype(BF16)
            dw = dw + _dot_tn(pb, dmb)
            dpooled = _dot_nt(dmb, wb)
            dps[pl.ds(t0, t), :] = dpooled
            es[pl.ds(t0, t), :] = dpooled / cnt
            return dw, dsc

        dw, dsc = lax.fori_loop(0, n_tiles, first, (jnp.zeros((LANES, LANES), F32), jnp.zeros((1, LANES), F32)))
        dw_ref[...] = dw
        dsc_ref[...] = dsc

        def second(i, carry):
            t0 = pl.multiple_of(i * t, t)
            ext = es[pl.ds(t0, t + POOL_HALO), :]
            n = t + POOL_HALO
            f2 = ext + pltpu.roll(ext, n - 1, 0)
            f4 = f2 + pltpu.roll(f2, n - 2, 0)
            f8 = f4 + pltpu.roll(f4, n - 4, 0)
            f16 = f8 + pltpu.roll(f8, n - 8, 0)
            sel = jnp.where(grp == 0, f2, jnp.where(grp == 1, f4, jnp.where(grp == 2, f8, f16)))
            dpv_ref[pl.ds(t0, t), :] = (sel[:t, :] - dps[pl.ds(t0, t), :]).astype(BF16)
            return carry

        lax.fori_loop(0, n_tiles, second, 0)

    col = lambda base: pl.BlockSpec((s, LANES), lambda g: (0, base + g))
    dcol = jax.ShapeDtypeStruct((s, WIDTH), BF16)
    return _pcall(
        body, name=name, grid=(4,),
        in_specs=[col(CB_POOL_V), col(CB_POOL_G), pl.BlockSpec((None, LANES, LANES), lambda g: (g, 0, 0)),
                  pl.BlockSpec((1, LANES), lambda g: (0, g)), col(0)],
        out_specs=(col(0), col(0), pl.BlockSpec((None, LANES, LANES), lambda g: (g, 0, 0)),
                   pl.BlockSpec((1, LANES), lambda g: (0, g))),
        out_shape=(dcol, dcol, jax.ShapeDtypeStruct((4, LANES, LANES), F32), jax.ShapeDtypeStruct((1, WIDTH), F32)),
        scratch_shapes=[pltpu.VMEM((POOL_HALO + s, LANES), F32), pltpu.VMEM((s + POOL_HALO, LANES), F32),
                        pltpu.VMEM((s, LANES), F32)],
        compiler_params=_params(("arbitrary",)),
    )(u, u, pool_w, pool_scale, dyp)


def _in_proj_bwd_x(du, w_all, x, g_pre, dy, name, xchg=None):
    s = x.shape[0]
    tm = min(1024, s)
    x_arrs, x_gather = xchg if xchg else ((), True)
    n_x = len(x_arrs)
    grid = (s // tm, N_DEV)

    def body(*refs):
        du_ref, w_ref, x_ref, g_ref, dy_ref = refs[:5]
        x_in, refs = refs[5:5 + n_x], refs[5 + n_x:]
        dx_ref, dg_ref = refs[:2]
        x_out, refs = refs[2:2 + n_x], refs[2 + n_x:]
        acc, x_sems = refs[0], refs[1:]
        i, k = pl.program_id(0), pl.program_id(1)
        if n_x:
            @pl.when((i == 0) & (k == 0))
            def _():
                _exchange_start(x_in, x_out, x_sems, x_gather)

        @pl.when(k == 0)
        def _():
            acc[...] = jnp.zeros_like(acc)

        @pl.when((k == 0) & (i == 0))
        def _():
            dg_ref[...] = jnp.zeros_like(dg_ref)
        acc[...] += _dot_nt(du_ref[...], w_ref[...])

        @pl.when(k == N_DEV - 1)
        def _():
            dh, xv = acc[...], x_ref[...]
            r = lax.rsqrt(jnp.mean(xv * xv, axis=-1, keepdims=True) + RMS_EPS)
            dg_ref[...] += jnp.sum(dh * xv * r, axis=0, keepdims=True)
            a = dh * g_ref[...]
            dx_ref[...] = dy_ref[...] + r * a - xv * (r * r * r) * jnp.mean(a * xv, axis=-1, keepdims=True)

        if n_x:
            @pl.when((i == grid[0] - 1) & (k == grid[1] - 1))
            def _():
                _exchange_wait(x_in, x_out, x_sems, x_gather)

    rows = lambda: pl.BlockSpec((tm, D_MODEL), lambda i, k: (i, 0))
    vec = lambda: pl.BlockSpec((1, D_MODEL), lambda i, k: (0, 0))
    return _pcall(
        body, name=name, grid=grid,
        in_specs=[pl.BlockSpec((tm, COLS_PER_DEV), lambda i, k: (i, k)),
                  pl.BlockSpec((None, D_MODEL, COLS_PER_DEV), lambda i, k: (k, 0, 0)),
                  rows(), vec(), rows()] + [ANY_SPEC] * n_x,
        out_specs=(rows(), vec()) + (ANY_SPEC,) * n_x,
        out_shape=(jax.ShapeDtypeStruct((s, D_MODEL), F32), jax.ShapeDtypeStruct((1, D_MODEL), F32))
        + tuple(_exchange_out_shapes(x_arrs, x_gather)),
        scratch_shapes=[pltpu.VMEM((tm, D_MODEL), F32)] + (_exchange_sems(n_x) if n_x else []),
        compiler_params=_params(("arbitrary", "arbitrary")),
    )(du, w_all, x, g_pre, dy, *x_arrs)


def _in_proj_bwd_send(h, du, w_all, x, g_pre, dy, name):
    s = x.shape[0]
    tk = s // N_DEV
    tm = min(1024, s)
    n_i = s // tm
    grid = (N_DEV + n_i, N_DEV)
    last = N_DEV - 1
    dev = lambda d: ((d >> 2) & 1, (d >> 1) & 1, d & 1)

    def body(h_ref, duw_ref, dux_ref, w_ref, x_ref, g_ref, dy_ref, dx_ref, dg_ref, recv_ref,
             acc_w, stage, acc_x, send_sems, recv_sems, local_sem):
        r, k = pl.program_id(0), pl.program_id(1)
        me = 4 * lax.axis_index("x") + 2 * lax.axis_index("y") + lax.axis_index("c")

        def send(j):
            return pltpu.make_async_remote_copy(
                src_ref=stage.at[j % 2], dst_ref=recv_ref.at[me], send_sem=send_sems.at[j], recv_sem=recv_sems.at[me],
                device_id=dev(j), device_id_type=pl.DeviceIdType.MESH)

        def keep(j):
            return pltpu.make_async_copy(stage.at[j % 2], recv_ref.at[me], local_sem)

        def start(j):
            pl.when(me != j)(lambda: send(j).start())
            pl.when(me == j)(lambda: keep(j).start())

        def wait_sent(j):
            pl.when(me != j)(lambda: send(j).wait_send())
            pl.when(me == j)(lambda: keep(j).wait())

        @pl.when(r < N_DEV)
        def _():
            @pl.when(k == 0)
            def _():
                acc_w[...] = jnp.zeros_like(acc_w)
            acc_w[...] += _dot_tn(h_ref[...], duw_ref[...])

            for j in range(N_DEV):
                @pl.when((k == last) & (r == j))
                def _():
                    if j >= 2:
                        wait_sent(j - 2)
                    stage[j % 2] = acc_w[...].astype(BF16)
                    start(j)

        @pl.when(r >= N_DEV)
        def _():
            @pl.when(k == 0)
            def _():
                acc_x[...] = jnp.zeros_like(acc_x)

            @pl.when((k == 0) & (r == N_DEV))
            def _():
                dg_ref[...] = jnp.zeros_like(dg_ref)
            acc_x[...] += _dot_nt(dux_ref[...], w_ref[...])

            @pl.when(k == last)
            def _():
                dh, xv = acc_x[...], x_ref[...]
                rs = lax.rsqrt(jnp.mean(xv * xv, axis=-1, keepdims=True) + RMS_EPS)
                dg_ref[...] += jnp.sum(dh * xv * rs, axis=0, keepdims=True)
                a = dh * g_ref[...]
                dx_ref[...] = dy_ref[...] + rs * a - xv * (rs * rs * rs) * jnp.mean(a * xv, axis=-1, keepdims=True)

        @pl.when((r == grid[0] - 1) & (k == last))
        def _():
            wait_sent(N_DEV - 2)
            wait_sent(N_DEV - 1)
            for d in range(N_DEV):
                @pl.when(me != d)
                def _():
                    pltpu.make_async_remote_copy(
                        src_ref=stage.at[0], dst_ref=recv_ref.at[d], send_sem=send_sems.at[d],
                        recv_sem=recv_sems.at[d], device_id=dev(d), device_id_type=pl.DeviceIdType.MESH).wait_recv()

    in_w = lambda r: r < N_DEV
    row_x = lambda r: jnp.maximum(r - N_DEV, 0)
    rows = lambda: pl.BlockSpec((tm, D_MODEL), lambda r, k: (row_x(r), 0))
    vec = lambda: pl.BlockSpec((1, D_MODEL), lambda r, k: (0, 0))
    return _pcall(
        body, name=name, grid=grid,
        in_specs=[pl.BlockSpec((tk, D_MODEL), lambda r, k: (jnp.where(in_w(r), k, last), 0)),
                  pl.BlockSpec((tk, COLS_PER_DEV), lambda r, k: (jnp.where(in_w(r), k, last), jnp.minimum(r, last))),
                  pl.BlockSpec((tm, COLS_PER_DEV), lambda r, k: (row_x(r), jnp.where(in_w(r), 0, k))),
                  pl.BlockSpec((None, D_MODEL, COLS_PER_DEV), lambda r, k: (jnp.where(in_w(r), 0, k), 0, 0)),
                  rows(), vec(), rows()],
        out_specs=(rows(), vec(), ANY_SPEC),
        out_shape=(jax.ShapeDtypeStruct((s, D_MODEL), F32), jax.ShapeDtypeStruct((1, D_MODEL), F32),
                   jax.ShapeDtypeStruct((N_DEV, D_MODEL, COLS_PER_DEV), BF16)),
        scratch_shapes=[pltpu.VMEM((D_MODEL, COLS_PER_DEV), F32), pltpu.VMEM((2, D_MODEL, COLS_PER_DEV), BF16),
                        pltpu.VMEM((tm, D_MODEL), F32), pltpu.SemaphoreType.DMA((N_DEV,)),
                        pltpu.SemaphoreType.DMA((N_DEV,)), pltpu.SemaphoreType.DMA],
        compiler_params=_params(("arbitrary", "arbitrary")),
    )(h, du, du, w_all, x, g_pre, dy)


def _in_proj_bwd_w(h, du, name):
    s = h.shape[0]
    tk = min(512, s)
    n_k = s // tk

    def body(h_ref, du_ref, out_ref, acc):
        k = pl.program_id(1)

        @pl.when(k == 0)
        def _():
            acc[...] = jnp.zeros_like(acc)
        acc[...] += _dot_tn(h_ref[...], du_ref[...])

        @pl.when(k == n_k - 1)
        def _():
            out_ref[...] = acc[...].astype(BF16)

    return _pcall(
        body, name=name, grid=(N_DEV, n_k),
        in_specs=[pl.BlockSpec((tk, D_MODEL), lambda j, k: (k, 0)),
                  pl.BlockSpec((tk, COLS_PER_DEV), lambda j, k: (k, j))],
        out_specs=pl.BlockSpec((None, D_MODEL, COLS_PER_DEV), lambda j, k: (j, 0, 0)),
        out_shape=jax.ShapeDtypeStruct((N_DEV, D_MODEL, COLS_PER_DEV), BF16),
        scratch_shapes=[pltpu.VMEM((D_MODEL, COLS_PER_DEV), F32)],
        compiler_params=_params(("parallel", "arbitrary")),
    )(h, du)


def _adamw_math(g, w, m, v):
    m_new = ADAM_B1 * m + (1.0 - ADAM_B1) * g
    v_new = ADAM_B2 * v + (1.0 - ADAM_B2) * (g * g)
    m_hat = m_new / (1.0 - ADAM_B1 ** ADAM_STEP)
    v_hat = v_new / (1.0 - ADAM_B2 ** ADAM_STEP)
    delta = -ADAM_LR * (m_hat / (jnp.sqrt(v_hat) + ADAM_EPS) + ADAM_WD * w)
    return delta, m_new, v_new


def _sum_partials(p_ref):
    total = p_ref[0].astype(F32)
    for d in range(1, N_DEV):
        total = total + p_ref[d].astype(F32)
    return total


def _adamw_layers(parts0, parts1, w, m, v, name):
    _, r, c = w.shape
    tr = min(128, r)
    n_r = r // tr

    def body(p0_ref, p1_ref, w_ref, m_ref, v_ref, g_ref, d_ref, mo_ref, vo_ref):
        layer = pl.program_id(0)

        @pl.when(layer == 0)
        def _():
            g_ref[...] = _sum_partials(p0_ref)

        @pl.when(layer == 1)
        def _():
            g_ref[...] = _sum_partials(p1_ref)
        d_ref[...], mo_ref[...], vo_ref[...] = _adamw_math(g_ref[...], w_ref[...], m_ref[...], v_ref[...])

    part = lambda which: pl.BlockSpec((N_DEV, tr, c), lambda l, i: (0, jnp.where(l == which, i, 0), 0))
    par = lambda: pl.BlockSpec((None, tr, c), lambda l, i: (l, i, 0))
    out = jax.ShapeDtypeStruct(w.shape, F32)
    return _pcall(
        body, name=name, grid=(2, n_r),
        in_specs=[part(0), part(1), par(), par(), par()],
        out_specs=(par(), par(), par(), par()),
        out_shape=(out, out, out, out),
        compiler_params=_params(("arbitrary", "arbitrary")),
    )(parts0, parts1, w, m, v)


def _adamw_small(parts, w, m, v, name):
    def body(p_ref, w_ref, m_ref, v_ref, g_ref, d_ref, mo_ref, vo_ref):
        g = _sum_partials(p_ref)
        g_ref[...] = g
        d_ref[...], mo_ref[...], vo_ref[...] = _adamw_math(g, w_ref[...], m_ref[...], v_ref[...])

    out = jax.ShapeDtypeStruct(w.shape, F32)
    return _pcall(body, name=name, out_shape=(out, out, out, out), compiler_params=_params())(parts, w, m, v)


def _adamw_plain(g, w, m, v, name):
    def body(g_ref, w_ref, m_ref, v_ref, d_ref, mo_ref, vo_ref):
        d_ref[...], mo_ref[...], vo_ref[...] = _adamw_math(g_ref[...], w_ref[...], m_ref[...], v_ref[...])

    out = jax.ShapeDtypeStruct(w.shape, F32)
    return _pcall(body, name=name, out_shape=(out, out, out), compiler_params=_params())(g, w, m, v)


def _rows128(a):
    return a.reshape(-1, LANES)


SMALL_NAMES = ("pre_norm_g", "pool_w", "pool_scale", "conv_w", "conv_b", "post_norm_g")


def kernel(x, pre_norm_g, w_in, pool_w, pool_scale, conv_w, conv_b, w_branch, w_out, post_norm_g, loss_target, m_pre_norm_g, m_w_in, m_pool_w, m_pool_scale, m_conv_w, m_conv_b, m_w_branch, m_w_out, m_post_norm_g, v_pre_norm_g, v_w_in, v_pool_w, v_pool_scale, v_conv_w, v_conv_b, v_w_branch, v_w_out, v_post_norm_g):
    s = x.shape[1]
    me = 4 * lax.axis_index("x") + 2 * lax.axis_index("y") + lax.axis_index("c")
    x0 = x[0]
    target = loss_target[0]
    conv_cols = conv_w.shape[-1]

    conv_w_pad = jnp.pad(conv_w.reshape(2 * 3, conv_cols), ((0, 2), (0, LANES - conv_cols)))
    w_in_all = [None, None]
    w_in_all[0], cw_g = _gather_two_level([w_in[0].astype(BF16), conv_w_pad], "gather_w_in_0")
    conv_w_full = cw_g[:, :6, :conv_cols].reshape(N_DEV, 2, 3, conv_cols).transpose(1, 2, 0, 3).reshape(2, 3, WIDTH)
    later_weights = ([w_in[1].astype(BF16), w_branch.astype(BF16), w_out.astype(BF16)], True)

    saved = []
    xin = x0
    for l in range(2):
        u, h = _in_proj_fwd(xin, pre_norm_g[l:l + 1], w_in_all[l], f"in_proj_fwd_{l}")
        y_pool = _pool_fwd(u, pool_w[l], pool_scale[l:l + 1], f"pool_fwd_{l}")
        y_conv = _conv_fwd(u, conv_w_full[l], conv_b[l:l + 1], f"conv_fwd_{l}")
        if l == 0:
            o_sb, y_sb, w_in_all[1], wb_g, wo_all = _sb_fwd(u, f"sb_fwd_{l}", later_weights)
            wb_all = wb_g.transpose(1, 2, 3, 0, 4).reshape(2, 3, WIDTH, D_MODEL)
        else:
            o_sb, y_sb = _sb_fwd(u, f"sb_fwd_{l}")
        xout, merged, pre = _merge_out_fwd(y_pool, y_conv, y_sb, u, wb_all, wo_all, xin, post_norm_g[l:l + 1], l,
                                           f"merge_out_fwd_{l}")
        saved.append((xin, u, h, y_pool, y_conv, y_sb, o_sb, merged, pre))
        xin = xout

    dy, loss_row = _loss_and_grad(xin, target, "loss")

    small = [None, None]
    recv = [None, None]
    ready = []
    for l in (1, 0):
        xl, u, h, y_pool, y_conv, y_sb, o_sb, merged, pre = saved[l]
        dmerged, dwo, dg_post = _out_proj_bwd(dy, pre, post_norm_g[l:l + 1], merged, wo_all, l, f"out_proj_bwd_{l}")
        du_merge, dyp, dyc, dys, dwb = _merge_bwd(dmerged, y_pool, y_conv, y_sb, u, wb_all, l, f"merge_bwd_{l}")
        dwb = dwb.reshape(N_DEV, 3 * WIDTH, D_MODEL // N_DEV)
        dwo = dwo.reshape(N_DEV, D_MODEL // N_DEV, D_MODEL)
        dxc, dgb, dgc, dcg, dcw, dcb = _conv_bwd(u, conv_w_full[l], conv_b[l:l + 1], dyc, f"conv_bwd_{l}")
        dpv, dpg, dpw, dps = _pool_bwd(u, pool_w[l], pool_scale[l:l + 1], dyp, f"pool_bwd_{l}")
        small[l] = dict(pool_w=dpw, pool_scale=dps, conv_w=dcw, conv_b=dcb, post_norm_g=dg_post)
        if l == 1:
            dq, dk, dv, dsg = _sb_bwd(u, o_sb, dys, f"sb_bwd_{l}")
        else:
            small[l]["pre_norm_g"] = jnp.zeros((1, D_MODEL), F32)
            packed = jnp.concatenate(
                [_rows128(jnp.stack([small[0][n], small[1][n]])) for n in SMALL_NAMES]
                + [jnp.pad(loss_row, ((0, 7), (0, 0)))], axis=0)
            dq, dk, dv, dsg, *got, packed_all = _sb_bwd(
                u, o_sb, dys, f"sb_bwd_{l}", (ready + [dwb, dwo, packed], (False,) * 5 + (True,)))
            recv[1] = got[:3]
        du = jnp.concatenate([dpv, dpg, dxc, dgb, dgc, dcg, dq, dk.astype(BF16), dv.astype(BF16), dsg, du_merge],
                             axis=1)
        if l == 1:
            dwi = _in_proj_bwd_w(h, du, f"in_proj_bwd_w_{l}")
            ready = [dwi, dwb, dwo]
            dx, dg_pre = _in_proj_bwd_x(du, w_in_all[l], xl, pre_norm_g[l:l + 1], dy, f"in_proj_bwd_x_{l}")
            small[l]["pre_norm_g"] = dg_pre
        else:
            dx, dg_pre, got_dwi = _in_proj_bwd_send(h, du, w_in_all[l], xl, pre_norm_g[l:l + 1], dy,
                                                    f"in_proj_bwd_{l}")
            recv[0] = [got_dwi] + got[3:]
        dy = dx
    grad_x = dy[None]

    (g_pre_0_all,) = _exchange([_rows128(dg_pre)], True, "gather_g_pre_0")
    packed_all = lax.dynamic_update_slice(packed_all, g_pre_0_all, (0, 0, 0))
    sizes = dict(pre_norm_g=16, pool_w=1024, pool_scale=8, conv_w=24, conv_b=8, post_norm_g=16)
    n_rows = sum(sizes.values())
    loss = jnp.sum(packed_all[:, n_rows, 0])

    given = dict(pre_norm_g=(pre_norm_g, m_pre_norm_g, v_pre_norm_g), pool_w=(pool_w, m_pool_w, v_pool_w),
                 pool_scale=(pool_scale, m_pool_scale, v_pool_scale), conv_b=(conv_b, m_conv_b, v_conv_b),
                 post_norm_g=(post_norm_g, m_post_norm_g, v_post_norm_g))
    zeros_cw = jnp.zeros((sizes["conv_w"], LANES), F32)
    pack3 = [jnp.concatenate([zeros_cw if n == "conv_w" else _rows128(given[n][k]) for n in SMALL_NAMES], axis=0)
             for k in range(3)]
    sg, sd, sm, sv = _adamw_small(packed_all[:, :n_rows], pack3[0], pack3[1], pack3[2], "adamw_small")

    def unpack(buf, name, shape):
        start = 0
        for n in SMALL_NAMES:
            if n == name:
                return buf[start:start + sizes[n]].reshape(shape)
            start += sizes[n]

    out = {}
    for n in ("pre_norm_g", "pool_w", "pool_scale", "conv_b", "post_norm_g"):
        shape = given[n][0].shape
        out[n] = tuple(unpack(b, n, shape) for b in (sg, sd, sm, sv))
    g_cw = lax.dynamic_slice_in_dim(unpack(sg, "conv_w", (2, 3, WIDTH)), me * conv_cols, conv_cols, axis=2)
    cw2 = lambda a: a.reshape(6, conv_cols)
    d_cw, m_cw, v_cw = _adamw_plain(cw2(g_cw), cw2(conv_w), cw2(m_conv_w), cw2(v_conv_w), "adamw_conv_w")
    out["conv_w"] = (g_cw,) + tuple(a.reshape(2, 3, conv_cols) for a in (d_cw, m_cw, v_cw))

    out["w_in"] = _adamw_layers(recv[0][0], recv[1][0], w_in, m_w_in, v_w_in, "adamw_w_in")
    cols = D_MODEL // N_DEV
    wb3 = lambda a: a.reshape(2, 3 * WIDTH, cols)
    out["w_branch"] = tuple(a.reshape(2, 3, WIDTH, cols) for a in _adamw_layers(
        recv[0][1], recv[1][1], wb3(w_branch), wb3(m_w_branch), wb3(v_w_branch), "adamw_w_branch"))
    out["w_out"] = _adamw_layers(recv[0][2], recv[1][2], w_out, m_w_out, v_w_out, "adamw_w_out")

    order = ("pre_norm_g", "w_in", "pool_w", "pool_scale", "conv_w", "conv_b", "w_branch", "w_out", "post_norm_g")
    return (loss, grad_x) + tuple(out[n][k] for k in range(4) for n in order)
```

```python
import functools

import jax
import jax.numpy as jnp
from jax import lax
from jax.experimental import pallas as pl
from jax.experimental.pallas import tpu as pltpu

F32 = jnp.float32
BF16 = jnp.bfloat16

N_DEV = 8
D_MODEL = 1024
WIDTH = 512
N_IN = 8192
COLS_PER_DEV = N_IN // N_DEV
HEAD_DIM = 64
LANES = 128
SB_SCALE = HEAD_DIM ** -0.5
RMS_EPS = 1e-6
POOL_HALO = 16
CONV_HALO = 8
ADAM_LR, ADAM_B1, ADAM_B2, ADAM_EPS, ADAM_WD, ADAM_STEP = 0.001, 0.9, 0.999, 1e-08, 0.01, 10
VMEM_LIMIT = 60 * 1024 * 1024

CB_POOL_V, CB_POOL_G = 0, 4
CB_CONV_X, CB_CONV_GB, CB_CONV_GC, CB_CONV_G = 8, 12, 16, 20
CB_SB_Q, CB_SB_K, CB_SB_V, CB_SB_G = 24, 28, 32, 36
MERGE_BLOCK_1024 = 5


def _pcall(body, **kw):
    return pl.pallas_call(body, **kw)


def _params(sem=None):
    if sem is None:
        return pltpu.CompilerParams(vmem_limit_bytes=VMEM_LIMIT)
    return pltpu.CompilerParams(dimension_semantics=sem, vmem_limit_bytes=VMEM_LIMIT)


def _sigmoid(x):
    return 1.0 / (1.0 + jnp.exp(-x))


def _dot(a, b):
    return jnp.dot(a, b, preferred_element_type=F32)


def _dot_nt(a, b):
    return lax.dot_general(a, b, (((1,), (1,)), ((), ())), preferred_element_type=F32)


def _dot_tn(a, b):
    return lax.dot_general(a, b, (((0,), (0,)), ((), ())), preferred_element_type=F32)


def _split_bf16(x):
    hi = x.astype(BF16)
    lo = (x - hi.astype(F32)).astype(BF16)
    return hi, lo


N_PEER = N_DEV - 1
ANY_SPEC = pl.BlockSpec(memory_space=pl.ANY)


def _exchange_copies(ins, outs, send_sems, recv_sems, local_sems, gather, with_recvs=True):
    n = len(ins)
    gathers = _per_array(gather, n)
    x, y, c = lax.axis_index("x"), lax.axis_index("y"), lax.axis_index("c")
    me = 4 * x + 2 * y + c
    flip = lambda v, bit: 1 - v if bit else v
    local, sends, recvs = [], [], []
    for a in range(n):
        src = ins[a] if gathers[a] else ins[a].at[me]
        local.append(pltpu.make_async_copy(src, outs[a].at[me], local_sems.at[a]))
    for k in range(N_PEER):
        px, py, pc = flip(x, ((k + 1) >> 2) & 1), flip(y, ((k + 1) >> 1) & 1), flip(c, (k + 1) & 1)
        peer_id = 4 * px + 2 * py + pc
        for a in range(n):
            src = ins[a] if gathers[a] else ins[a].at[peer_id]
            common = dict(src_ref=src, send_sem=send_sems.at[a * N_PEER + k], recv_sem=recv_sems.at[a * N_PEER + k],
                          device_id=(px, py, pc), device_id_type=pl.DeviceIdType.MESH)
            sends.append(pltpu.make_async_remote_copy(dst_ref=outs[a].at[me], **common))
            if with_recvs:
                recvs.append(pltpu.make_async_remote_copy(dst_ref=outs[a].at[peer_id], **common))
    return local, sends, recvs


def _exchange_start(ins, outs, sems, gather):
    local, sends, _ = _exchange_copies(ins, outs, *sems, gather, with_recvs=False)
    for cp in local + sends:
        cp.start()


def _exchange_wait(ins, outs, sems, gather):
    local, sends, recvs = _exchange_copies(ins, outs, *sems, gather)
    for cp in recvs:
        cp.wait_recv()
    for cp in sends:
        cp.wait_send()
    for cp in local:
        cp.wait()


def _per_array(gather, n):
    return tuple(gather) if isinstance(gather, (tuple, list)) else (gather,) * n


def _exchange_out_shapes(arrs, gather):
    return [jax.ShapeDtypeStruct((N_DEV,) + tuple(a.shape if g else a.shape[1:]), a.dtype)
            for a, g in zip(arrs, _per_array(gather, len(arrs)))]


def _gather_two_level(arrs, name):
    n = len(arrs)

    def body(*refs):
        ins, outs = refs[:n], refs[n:2 * n]
        send_sems, recv_sems, local_sems = refs[2 * n:]
        x, y, c = lax.axis_index("x"), lax.axis_index("y"), lax.axis_index("c")
        me, sibling = (x, y, c), (x, y, 1 - c)
        chips = [(1 - x, y), (x, 1 - y), (1 - x, 1 - y)]
        slot = lambda dev: 4 * dev[0] + 2 * dev[1] + dev[2]

        def copy(a, k, block, to, src=None):
            return pltpu.make_async_remote_copy(
                src_ref=outs[a].at[slot(block)] if src is None else src, dst_ref=outs[a].at[slot(block)],
                send_sem=send_sems.at[a * N_PEER + k], recv_sem=recv_sems.at[a * N_PEER + k],
                device_id=to, device_id_type=pl.DeviceIdType.MESH)

        local = [pltpu.make_async_copy(ins[a], outs[a].at[slot(me)], local_sems.at[a]) for a in range(n)]
        first = []
        for a in range(n):
            first.append(copy(a, 0, me, sibling, src=ins[a]))
            first += [copy(a, 1 + j, me, (*chip, c), src=ins[a]) for j, chip in enumerate(chips)]
        for cp in local + first:
            cp.start()
        passed = []
        for j, chip in enumerate(chips):
            for a in range(n):
                copy(a, 1 + j, (*chip, c), me).wait_recv()
                passed.append(copy(a, 4 + j, (*chip, c), sibling))
                passed[-1].start()
        for a in range(n):
            copy(a, 0, sibling, me).wait_recv()
        for j, chip in enumerate(chips):
            for a in range(n):
                copy(a, 4 + j, (*chip, 1 - c), me).wait_recv()
        for cp in first + passed:
            cp.wait_send()
        for cp in local:
            cp.wait()

    return _pcall(
        body, name=name,
        out_shape=tuple(_exchange_out_shapes(arrs, True)),
        in_specs=[ANY_SPEC] * n, out_specs=tuple([ANY_SPEC] * n),
        scratch_shapes=_exchange_sems(n),
    )(*arrs)


def _exchange_sems(n):
    return [pltpu.SemaphoreType.DMA((n * N_PEER,)), pltpu.SemaphoreType.DMA((n * N_PEER,)),
            pltpu.SemaphoreType.DMA((n,))]


def _exchange(arrs, gather, name):
    n = len(arrs)

    def body(*refs):
        ins, outs, sems = refs[:n], refs[n:2 * n], refs[2 * n:]
        _exchange_start(ins, outs, sems, gather)
        _exchange_wait(ins, outs, sems, gather)

    return _pcall(
        body, name=name,
        out_shape=tuple(_exchange_out_shapes(arrs, gather)),
        in_specs=[ANY_SPEC] * n, out_specs=tuple([ANY_SPEC] * n),
        scratch_shapes=_exchange_sems(n),
    )(*arrs)


def _in_proj_fwd(x, g, w_all, name):
    s = x.shape[0]
    tm = min(1024, s)

    def body(x_ref, g_ref, w_ref, u_ref, h_ref, hs):
        @pl.when(pl.program_id(1) == 0)
        def _():
            xv = x_ref[...]
            r = lax.rsqrt(jnp.mean(xv * xv, axis=-1, keepdims=True) + RMS_EPS)
            hv = (xv * r * g_ref[...]).astype(BF16)
            hs[...] = hv
            h_ref[...] = hv
        u_ref[...] = _dot(hs[...], w_ref[...])

    return _pcall(
        body, name=name, grid=(s // tm, N_DEV),
        in_specs=[pl.BlockSpec((tm, D_MODEL), lambda i, j: (i, 0)),
                  pl.BlockSpec((1, D_MODEL), lambda i, j: (0, 0)),
                  pl.BlockSpec((None, D_MODEL, COLS_PER_DEV), lambda i, j: (j, 0, 0))],
        out_specs=(pl.BlockSpec((tm, COLS_PER_DEV), lambda i, j: (i, j)),
                   pl.BlockSpec((tm, D_MODEL), lambda i, j: (i, 0))),
        out_shape=(jax.ShapeDtypeStruct((s, N_IN), F32), jax.ShapeDtypeStruct((s, D_MODEL), BF16)),
        scratch_shapes=[pltpu.VMEM((tm, D_MODEL), BF16)],
        compiler_params=_params(("parallel", "arbitrary")),
    )(x, g, w_all)


def _pool_window(vs, t0, t, grp):
    ext = vs[pl.ds(t0, t + POOL_HALO), :]
    s2 = ext + pltpu.roll(ext, 1, 0)
    s4 = s2 + pltpu.roll(s2, 2, 0)
    s8 = s4 + pltpu.roll(s4, 4, 0)
    s16 = s8 + pltpu.roll(s8, 8, 0)
    sel = jnp.where(grp == 0, s2, jnp.where(grp == 1, s4, jnp.where(grp == 2, s8, s16)))
    return sel[POOL_HALO:, :], ext[POOL_HALO:, :]


def _pool_count(t0, t, grp):
    pos = t0 + lax.broadcasted_iota(jnp.int32, (t, 1), 0)
    return jnp.minimum(pos + 1, jnp.left_shift(2, grp)).astype(F32)


def _pool_fwd(u, pool_w, pool_scale, name):
    s = u.shape[0]
    t = min(256, s)

    def body(pv_ref, pg_ref, w_ref, sc_ref, y_ref, vs):
        grp = pl.program_id(0)
        vs[0:POOL_HALO, :] = jnp.zeros((POOL_HALO, LANES), F32)
        vs[POOL_HALO:, :] = pv_ref[...]
        wb = w_ref[...].astype(BF16)
        scale = sc_ref[...]

        def tile(i, carry):
            t0 = pl.multiple_of(i * t, t)
            win, v = _pool_window(vs, t0, t, grp)
            pooled = win / _pool_count(t0, t, grp) - v
            mixed = _dot(pooled.astype(BF16), wb)
            gate = pg_ref[pl.ds(t0, t), :]
            y_ref[pl.ds(t0, t), :] = (mixed * scale * (gate * _sigmoid(gate))).astype(BF16)
            return carry

        lax.fori_loop(0, s // t, tile, 0)

    return _pcall(
        body, name=name, grid=(4,),
        in_specs=[pl.BlockSpec((s, LANES), lambda g: (0, CB_POOL_V + g)),
                  pl.BlockSpec((s, LANES), lambda g: (0, CB_POOL_G + g)),
                  pl.BlockSpec((None, LANES, LANES), lambda g: (g, 0, 0)),
                  pl.BlockSpec((1, LANES), lambda g: (0, g))],
        out_specs=pl.BlockSpec((s, LANES), lambda g: (0, g)),
        out_shape=jax.ShapeDtypeStruct((s, WIDTH), BF16),
        scratch_shapes=[pltpu.VMEM((POOL_HALO + s, LANES), F32)],
        compiler_params=_params(("arbitrary",)),
    )(u, u, pool_w, pool_scale)


def _conv_taps(zs, t0, t):
    ext = zs[pl.ds(t0, t + CONV_HALO), :]
    z0 = ext[CONV_HALO:, :]
    z1 = pltpu.roll(ext, 1, 0)[CONV_HALO:, :]
    z2 = pltpu.roll(ext, 2, 0)[CONV_HALO:, :]
    return z0, z1, z2


def _conv_fwd(u, conv_w, conv_b, name):
    s = u.shape[0]
    t = min(256, s)

    def body(xc_ref, gb_ref, gc_ref, cg_ref, w_ref, b_ref, y_ref, zs):
        zs[0:CONV_HALO, :] = jnp.zeros((CONV_HALO, LANES), F32)
        zs[CONV_HALO:, :] = gc_ref[...] * xc_ref[...]
        w0, w1, w2 = w_ref[0:1, :], w_ref[1:2, :], w_ref[2:3, :]
        bias = b_ref[...]

        def tile(i, carry):
            t0 = pl.multiple_of(i * t, t)
            z0, z1, z2 = _conv_taps(zs, t0, t)
            conv = w0 * z2 + w1 * z1 + w2 * z0
            gate = cg_ref[pl.ds(t0, t), :]
            y = gb_ref[pl.ds(t0, t), :] * (conv + bias) * (gate * _sigmoid(gate))
            y_ref[pl.ds(t0, t), :] = y.astype(BF16)
            return carry

        lax.fori_loop(0, s // t, tile, 0)

    col = lambda base: pl.BlockSpec((s, LANES), lambda j: (0, base + j))
    return _pcall(
        body, name=name, grid=(4,),
        in_specs=[col(CB_CONV_X), col(CB_CONV_GB), col(CB_CONV_GC), col(CB_CONV_G),
                  pl.BlockSpec((3, LANES), lambda j: (0, j)),
                  pl.BlockSpec((1, LANES), lambda j: (0, j))],
        out_specs=pl.BlockSpec((s, LANES), lambda j: (0, j)),
        out_shape=jax.ShapeDtypeStruct((s, WIDTH), BF16),
        scratch_shapes=[pltpu.VMEM((CONV_HALO + s, LANES), F32)],
        compiler_params=_params(("arbitrary",)),
    )(u, u, u, u, conv_w, conv_b)


def _first_head_lanes(rows, width=LANES):
    lane = lax.broadcasted_iota(jnp.int32, (rows, width), 1)
    return jnp.bitwise_and(lane, LANES - 1) < HEAD_DIM


def _stack_heads(x, first):
    zero = jnp.zeros_like(x)
    return jnp.concatenate([jnp.where(first, x, zero), jnp.where(first, zero, x)], axis=0).astype(BF16)


def _causal_mask(tq, tk, copies):
    row = lax.broadcasted_iota(jnp.int32, (tq, tk), 0)
    col = lax.broadcasted_iota(jnp.int32, (tq, tk), 1)
    return jnp.concatenate([col < row] * copies, axis=0)


def _suffix_matrix(tk, inclusive):
    r = lax.broadcasted_iota(jnp.int32, (2 * tk, 2 * tk), 0)
    c = lax.broadcasted_iota(jnp.int32, (2 * tk, 2 * tk), 1)
    r = jnp.where(r >= tk, r - tk, r)
    tri = (r >= c) if inclusive else (r > c)
    return jnp.where(c >= tk, 1.0, jnp.where(tri, 1.0, 0.0)).astype(BF16)


def _suffix_sums(x, m):
    hi, lo = _split_bf16(x)
    return _dot(jnp.concatenate([hi, lo], axis=1), m)


def _sb_log_terms(z, mask, m_strict):
    ls = jnp.minimum(z, 0.0) - jnp.log(1.0 + jnp.exp(-jnp.abs(z)))
    lk = ls - z
    if mask is not None:
        lk = jnp.where(mask, lk, 0.0)
    return ls, _suffix_sums(lk, m_strict)


SB_PAIRS = 4


def _pair_lanes(a):
    return slice(a * LANES, (a + 1) * LANES)


def _sb_fwd(u, name, xchg=None):
    s = u.shape[0]
    tq = tk = min(128, s)
    pairs = SB_PAIRS
    width = pairs * LANES
    rows = 2 * pairs * tq
    x_arrs, x_gather = xchg if xchg else ((), True)
    n_x = len(x_arrs)
    grid = (4 // pairs, s // tq)

    def body(*refs):
        q_ref, k_ref, v_ref, g_ref = refs[:4]
        x_in, refs = refs[4:4 + n_x], refs[4 + n_x:]
        o_ref, y_ref = refs[:2]
        x_out, refs = refs[2:2 + n_x], refs[2 + n_x:]
        kbf, vst, z_s, ell_s, carry_s = refs[:5]
        x_sems = refs[5:]
        i = pl.program_id(1)
        if n_x:
            @pl.when((pl.program_id(0) == 0) & (i == 0))
            def _():
                _exchange_start(x_in, x_out, x_sems, x_gather)

        @pl.when(i == 0)
        def _():
            kbf[...] = k_ref[...].astype(BF16)
            first_s = _first_head_lanes(s, width)
            vf = v_ref[...]
            vst[0] = jnp.where(first_s, vf, 0.0).astype(BF16)
            vst[1] = jnp.where(first_s, 0.0, vf).astype(BF16)

        first = _first_head_lanes(tq)
        mask = _causal_mask(tq, tk, 2 * pairs)
        m_strict = _suffix_matrix(tk, False)
        qcat = jnp.concatenate([_stack_heads(q_ref[:, _pair_lanes(a)] * SB_SCALE, first) for a in range(pairs)],
                               axis=0)

        def scores(b):
            off = pl.multiple_of(jnp.maximum(b, 0) * tk, tk)
            z_s[...] = jnp.concatenate(
                [_dot_nt(qcat[a * 2 * tq:(a + 1) * 2 * tq], kbf[pl.ds(off, tk), _pair_lanes(a)])
                 for a in range(pairs)], axis=0)

        def log_weights(m):
            ls, cs = _sb_log_terms(z_s[...], m, m_strict)
            carry = carry_s[...]
            ell_s[...] = ls + cs[:, :tk] + carry
            carry_s[...] = carry + cs[:, tk:]

        def consume(b, accs, m):
            w = jnp.exp(ell_s[...])
            if m is not None:
                w = jnp.where(m, w, 0.0)
            wb = w.astype(BF16)
            off = pl.multiple_of(b * tk, tk)
            new = []
            for a in range(pairs):
                r0 = a * 2 * tq
                wcat = jnp.concatenate([wb[r0:r0 + tq], wb[r0 + tq:r0 + 2 * tq]], axis=1)
                vcat = jnp.concatenate([vst[0, pl.ds(off, tk), _pair_lanes(a)], vst[1, pl.ds(off, tk), _pair_lanes(a)]],
                                       axis=0)
                new.append(accs[a] + _dot(wcat, vcat))
            return tuple(new)

        carry_s[...] = jnp.zeros((rows, tk), F32)
        scores(i)
        log_weights(mask)
        scores(i - 1)
        accs = consume(i, tuple(jnp.zeros((tq, LANES), F32) for _ in range(pairs)), mask)
        log_weights(None)
        scores(i - 2)

        def step(n, accs):
            accs = consume(i - n, accs, None)
            log_weights(None)
            scores(i - n - 2)
            return accs

        accs = lax.fori_loop(1, i + 1, step, accs)
        o = jnp.concatenate(accs, axis=1)
        o_ref[...] = o
        gate = g_ref[...]
        y_ref[...] = (o * (gate * _sigmoid(gate))).astype(BF16)
        if n_x:
            @pl.when((pl.program_id(0) == grid[0] - 1) & (i == grid[1] - 1))
            def _():
                _exchange_wait(x_in, x_out, x_sems, x_gather)

    base = lambda cb: cb // pairs
    qblk = lambda cb: pl.BlockSpec((tq, width), lambda p, i: (i, base(cb) + p))
    full = lambda cb: pl.BlockSpec((s, width), lambda p, i: (0, base(cb) + p), pipeline_mode=pl.Buffered(1))
    state = pltpu.VMEM((rows, tk), F32)
    return _pcall(
        body, name=name, grid=grid,
        in_specs=[qblk(CB_SB_Q), full(CB_SB_K), full(CB_SB_V), qblk(CB_SB_G)] + [ANY_SPEC] * n_x,
        out_specs=(qblk(0), qblk(0)) + (ANY_SPEC,) * n_x,
        out_shape=(jax.ShapeDtypeStruct((s, WIDTH), F32), jax.ShapeDtypeStruct((s, WIDTH), BF16))
        + tuple(_exchange_out_shapes(x_arrs, x_gather)),
        scratch_shapes=[pltpu.VMEM((s, width), BF16), pltpu.VMEM((2, s, width), BF16), state, state, state]
        + (_exchange_sems(n_x) if n_x else []),
        compiler_params=_params(("arbitrary", "arbitrary")),
    )(u, u, u, u, *x_arrs)


def _merge_out_fwd(y_pool, y_conv, y_sb, u, wb_all, wo_all, x, g_post, layer, name):
    s = x.shape[0]
    tm = min(256, s)

    def body(yp, yc, ys, m0, m1, m2, wb_ref, wo_ref, x_ref, g_ref, out_ref, merged_ref, pre_ref):
        merged = jnp.zeros((tm, D_MODEL), F32)
        for n, (y_ref, m_ref) in enumerate(((yp, m0), (yc, m1), (ys, m2))):
            merged = merged + _sigmoid(m_ref[...]) * _dot(y_ref[...], wb_ref[n])
        mb = merged.astype(BF16)
        merged_ref[...] = mb
        pre = _dot(mb, wo_ref[...].reshape(D_MODEL, D_MODEL))
        pre_ref[...] = pre
        r = lax.rsqrt(jnp.mean(pre * pre, axis=-1, keepdims=True) + RMS_EPS)
        out_ref[...] = x_ref[...] + pre * r * g_ref[...]

    rows = lambda w: pl.BlockSpec((tm, w), lambda i: (i, 0))
    merge = lambda n: pl.BlockSpec((tm, D_MODEL), lambda i: (i, MERGE_BLOCK_1024 + n))
    return _pcall(
        body, name=name, grid=(s // tm,),
        in_specs=[rows(WIDTH), rows(WIDTH), rows(WIDTH), merge(0), merge(1), merge(2),
                  pl.BlockSpec((None, 3, WIDTH, D_MODEL), lambda i: (layer, 0, 0, 0)),
                  pl.BlockSpec((N_DEV, None, D_MODEL // N_DEV, D_MODEL), lambda i: (0, layer, 0, 0)),
                  rows(D_MODEL), pl.BlockSpec((1, D_MODEL), lambda i: (0, 0))],
        out_specs=(rows(D_MODEL), rows(D_MODEL), rows(D_MODEL)),
        out_shape=(jax.ShapeDtypeStruct((s, D_MODEL), F32), jax.ShapeDtypeStruct((s, D_MODEL), BF16),
                   jax.ShapeDtypeStruct((s, D_MODEL), F32)),
        compiler_params=_params(("arbitrary",)),
    )(y_pool, y_conv, y_sb, u, u, u, wb_all, wo_all, x, g_post)


def _loss_and_grad(y, target, name):
    s = y.shape[0]
    tm = min(512, s)

    def body(y_ref, t_ref, dy_ref, loss_ref, acc):
        i = pl.program_id(0)

        @pl.when(i == 0)
        def _():
            acc[...] = jnp.zeros_like(acc)
        err = y_ref[...] - t_ref[...]
        dy_ref[...] = err / D_MODEL
        acc[...] += jnp.sum(err * err, axis=0, keepdims=True)

        @pl.when(i == pl.num_programs(0) - 1)
        def _():
            total = jnp.sum(acc[...], axis=1, keepdims=True) * (0.5 / D_MODEL)
            loss_ref[...] = jnp.broadcast_to(total, (1, LANES))

    return _pcall(
        body, name=name, grid=(s // tm,),
        in_specs=[pl.BlockSpec((tm, D_MODEL), lambda i: (i, 0)), pl.BlockSpec((tm, D_MODEL), lambda i: (i, 0))],
        out_specs=(pl.BlockSpec((tm, D_MODEL), lambda i: (i, 0)), pl.BlockSpec((1, LANES), lambda i: (0, 0))),
        out_shape=(jax.ShapeDtypeStruct((s, D_MODEL), F32), jax.ShapeDtypeStruct((1, LANES), F32)),
        scratch_shapes=[pltpu.VMEM((1, D_MODEL), F32)],
        compiler_params=_params(("arbitrary",)),
    )(y, target)


def _out_proj_bwd(dy, pre, g_post, merged, wo_all, layer, name):
    s = dy.shape[0]
    tm = min(256, s)
    n_tiles = s // tm

    def body(dy_ref, pre_ref, g_ref, mg_ref, wo_ref, dm_ref, dwo_ref, dg_ref, acc):
        i = pl.program_id(0)

        @pl.when(i == 0)
        def _():
            acc[...] = jnp.zeros_like(acc)
            dg_ref[...] = jnp.zeros_like(dg_ref)
        dyv, pre_v = dy_ref[...], pre_ref[...]
        r = lax.rsqrt(jnp.mean(pre_v * pre_v, axis=-1, keepdims=True) + RMS_EPS)
        dg_ref[...] += jnp.sum(dyv * pre_v * r, axis=0, keepdims=True)
        a = dyv * g_ref[...]
        dpre = r * a - pre_v * (r * r * r) * jnp.mean(a * pre_v, axis=-1, keepdims=True)
        db = dpre.astype(BF16)
        acc[...] += _dot_tn(mg_ref[...], db)
        dm_ref[...] = _dot_nt(db, wo_ref[...].reshape(D_MODEL, D_MODEL))

        @pl.when(i == n_tiles - 1)
        def _():
            dwo_ref[...] = acc[...].astype(BF16)

    rows = lambda: pl.BlockSpec((tm, D_MODEL), lambda i: (i, 0))
    return _pcall(
        body, name=name, grid=(n_tiles,),
        in_specs=[rows(), rows(), pl.BlockSpec((1, D_MODEL), lambda i: (0, 0)), rows(),
                  pl.BlockSpec((N_DEV, None, D_MODEL // N_DEV, D_MODEL), lambda i: (0, layer, 0, 0))],
        out_specs=(rows(), pl.BlockSpec((D_MODEL, D_MODEL), lambda i: (0, 0)),
                   pl.BlockSpec((1, D_MODEL), lambda i: (0, 0))),
        out_shape=(jax.ShapeDtypeStruct((s, D_MODEL), F32), jax.ShapeDtypeStruct((D_MODEL, D_MODEL), BF16),
                   jax.ShapeDtypeStruct((1, D_MODEL), F32)),
        scratch_shapes=[pltpu.VMEM((D_MODEL, D_MODEL), F32)],
        compiler_params=_params(("arbitrary",)),
    )(dy, pre, g_post, merged, wo_all)


def _merge_bwd(dmerged, y_pool, y_conv, y_sb, u, wb_all, layer, name):
    s = dmerged.shape[0]
    tm = min(256, s)
    n_tiles = s // tm
    cols = D_MODEL // N_DEV

    def body(dm_ref, yp, yc, ys, m0, m1, m2, wb_ref, dum_ref, dyp, dyc, dys, dwb_ref, acc):
        i = pl.program_id(0)

        @pl.when(i == 0)
        def _():
            acc[...] = jnp.zeros_like(acc)
        dm = dm_ref[...]
        for n, (y_ref, m_ref, dy_ref) in enumerate(((yp, m0, dyp), (yc, m1, dyc), (ys, m2, dys))):
            yv = y_ref[...]
            wb = wb_ref[n]
            gate = _sigmoid(m_ref[...])
            proj = _dot(yv, wb)
            dum_ref[:, n * D_MODEL:(n + 1) * D_MODEL] = (dm * proj * gate * (1.0 - gate)).astype(BF16)
            dproj = (dm * gate).astype(BF16)
            acc[n] += _dot_tn(yv, dproj)
            dy_ref[...] = _dot_nt(dproj, wb)

        @pl.when(i == n_tiles - 1)
        def _():
            for j in range(N_DEV):
                for n in range(3):
                    dwb_ref[j, n] = acc[n, :, j * cols:(j + 1) * cols].astype(BF16)

    rows = lambda w: pl.BlockSpec((tm, w), lambda i: (i, 0))
    merge = lambda n: pl.BlockSpec((tm, D_MODEL), lambda i: (i, MERGE_BLOCK_1024 + n))
    return _pcall(
        body, name=name, grid=(n_tiles,),
        in_specs=[rows(D_MODEL), rows(WIDTH), rows(WIDTH), rows(WIDTH), merge(0), merge(1), merge(2),
                  pl.BlockSpec((None, 3, WIDTH, D_MODEL), lambda i: (layer, 0, 0, 0))],
        out_specs=(rows(3 * D_MODEL), rows(WIDTH), rows(WIDTH), rows(WIDTH),
                   pl.BlockSpec((N_DEV, 3, WIDTH, cols), lambda i: (0, 0, 0, 0))),
        out_shape=(jax.ShapeDtypeStruct((s, 3 * D_MODEL), BF16),
                   jax.ShapeDtypeStruct((s, WIDTH), F32), jax.ShapeDtypeStruct((s, WIDTH), F32),
                   jax.ShapeDtypeStruct((s, WIDTH), F32),
                   jax.ShapeDtypeStruct((N_DEV, 3, WIDTH, cols), BF16)),
        scratch_shapes=[pltpu.VMEM((3, WIDTH, D_MODEL), F32)],
        compiler_params=_params(("arbitrary",)),
    )(dmerged, y_pool, y_conv, y_sb, u, u, u, wb_all)


def _sb_bwd(u, o, dys, name, xchg=None):
    s = u.shape[0]
    tq = tk = min(128, s)
    pairs = SB_PAIRS
    width = pairs * LANES
    rows = 2 * pairs * tq
    pair_rows = lambda a: slice(a * 2 * tq, (a + 1) * 2 * tq)

    x_arrs, x_gather = xchg if xchg else ((), True)
    n_x = len(x_arrs)
    grid = (4 // pairs, s // tq)

    def body(*refs):
        q_ref, k_ref, v_ref, g_ref, o_ref, dys_ref = refs[:6]
        x_in, refs = refs[6:6 + n_x], refs[6 + n_x:]
        dq_ref, dk_ref, dv_ref, dg_ref = refs[:4]
        x_out, refs = refs[4:4 + n_x], refs[4 + n_x:]
        kbf, vbf, kst, z_s, ell_s, ls_s, cl_s, wb_s, g_s, bef_s, cg_s, beta_s = refs[:12]
        x_sems = refs[12:]
        i = pl.program_id(1)
        if n_x:
            @pl.when((pl.program_id(0) == 0) & (i == 0))
            def _():
                _exchange_start(x_in, x_out, x_sems, x_gather)

        @pl.when(i == 0)
        def _():
            dk_ref[...] = jnp.zeros_like(dk_ref)
            dv_ref[...] = jnp.zeros_like(dv_ref)
            kf = k_ref[...]
            kbf[...] = kf.astype(BF16)
            vbf[...] = v_ref[...].astype(BF16)
            first_s = _first_head_lanes(s, width)
            kst[0] = jnp.where(first_s, kf, 0.0).astype(BF16)
            kst[1] = jnp.where(first_s, 0.0, kf).astype(BF16)

        first = _first_head_lanes(tq)
        mask = _causal_mask(tq, tk, 2 * pairs)
        m_strict = _suffix_matrix(tk, False)
        m_incl = _suffix_matrix(tk, True)

        gate = g_ref[...]
        sg = _sigmoid(gate)
        dy = dys_ref[...]
        ov = o_ref[...]
        dg_ref[...] = (dy * ov * (sg * (1.0 + gate * (1.0 - sg)))).astype(BF16)
        do = (dy * (gate * sg)).astype(BF16)
        prod = do.astype(F32) * ov
        row_sum = lambda v: jnp.broadcast_to(jnp.sum(v, axis=1, keepdims=True), (tq, tk))
        dsum, docat, qcat = [], [], []
        for a in range(pairs):
            pa = prod[:, _pair_lanes(a)]
            dsum += [row_sum(jnp.where(first, pa, 0.0)), row_sum(jnp.where(first, 0.0, pa))]
            docat.append(_stack_heads(do[:, _pair_lanes(a)], first))
            qcat.append(_stack_heads(q_ref[:, _pair_lanes(a)] * SB_SCALE, first))
        dsum = jnp.concatenate(dsum, axis=0)

        def block_start(b):
            return pl.multiple_of(jnp.maximum(b, 0) * tk, tk)

        def scores(b):
            off = block_start(b)
            z_s[...] = jnp.concatenate([_dot_nt(qcat[a], kbf[pl.ds(off, tk), _pair_lanes(a)]) for a in range(pairs)],
                                       axis=0)

        def log_weights(m):
            ls, cs = _sb_log_terms(z_s[...], m, m_strict)
            cl = cl_s[...]
            ell_s[...] = ls + cs[:, :tk] + cl
            cl_s[...] = cl + cs[:, tk:]
            ls_s[...] = ls

        def weights(b, m):
            off = block_start(b)
            dwt = jnp.concatenate([_dot_nt(docat[a], vbf[pl.ds(off, tk), _pair_lanes(a)]) for a in range(pairs)],
                                  axis=0)
            w = jnp.exp(ell_s[...])
            if m is not None:
                w = jnp.where(m, w, 0.0)
            wb = w.astype(BF16)
            g = dwt * wb.astype(F32)
            gs = _suffix_sums(g, m_incl)
            cg = cg_s[...]
            beta = jnp.exp(ls_s[...])
            wb_s[...] = wb
            beta_s[...] = beta
            g_s[...] = g * (1.0 - beta)
            bef_s[...] = gs[:, :tk] + cg
            cg_s[...] = cg + gs[:, tk:]

        def grads(b, dqs, m):
            dz = g_s[...] - beta_s[...] * (dsum - bef_s[...])
            if m is not None:
                dz = jnp.where(m, dz, 0.0)
            dzb = dz.astype(BF16)
            wb = wb_s[...]
            off = pl.multiple_of(b * tk, tk)
            new = []
            for a in range(pairs):
                r0 = a * 2 * tq
                kcat = jnp.concatenate([kst[0, pl.ds(off, tk), _pair_lanes(a)], kst[1, pl.ds(off, tk), _pair_lanes(a)]],
                                       axis=0)
                new.append(dqs[a] + _dot(jnp.concatenate([dzb[r0:r0 + tq], dzb[r0 + tq:r0 + 2 * tq]], axis=1), kcat))
                dk_ref[pl.ds(off, tk), _pair_lanes(a)] += _dot_tn(dzb[pair_rows(a)], qcat[a])
                dv_ref[pl.ds(off, tk), _pair_lanes(a)] += _dot_tn(wb[pair_rows(a)], docat[a])
            return tuple(new)

        zero = jnp.zeros((rows, tk), F32)
        cl_s[...] = zero
        cg_s[...] = zero
        scores(i)
        log_weights(mask)
        scores(i - 1)
        weights(i, mask)
        log_weights(None)
        scores(i - 2)
        dqs = grads(i, tuple(jnp.zeros((tq, LANES), F32) for _ in range(pairs)), mask)
        weights(i - 1, None)
        log_weights(None)
        scores(i - 3)

        def step(n, dqs):
            dqs = grads(i - n, dqs, None)
            weights(i - n - 1, None)
            log_weights(None)
            scores(i - n - 3)
            return dqs

        dqs = lax.fori_loop(1, i + 1, step, dqs)
        dq_ref[...] = (jnp.concatenate(dqs, axis=1) * SB_SCALE).astype(BF16)
        if n_x:
            @pl.when((pl.program_id(0) == grid[0] - 1) & (i == grid[1] - 1))
            def _():
                _exchange_wait(x_in, x_out, x_sems, x_gather)

    base = lambda cb: cb // pairs
    qblk = lambda cb: pl.BlockSpec((tq, width), lambda p, i: (i, base(cb) + p))
    full = lambda cb: pl.BlockSpec((s, width), lambda p, i: (0, base(cb) + p), pipeline_mode=pl.Buffered(1))
    state = pltpu.VMEM((rows, tk), F32)
    return _pcall(
        body, name=name, grid=grid,
        in_specs=[qblk(CB_SB_Q), full(CB_SB_K), full(CB_SB_V), qblk(CB_SB_G), qblk(0), qblk(0)] + [ANY_SPEC] * n_x,
        out_specs=(qblk(0), full(0), full(0), qblk(0)) + (ANY_SPEC,) * n_x,
        out_shape=(jax.ShapeDtypeStruct((s, WIDTH), BF16), jax.ShapeDtypeStruct((s, WIDTH), F32),
                   jax.ShapeDtypeStruct((s, WIDTH), F32), jax.ShapeDtypeStruct((s, WIDTH), BF16))
        + tuple(_exchange_out_shapes(x_arrs, x_gather)),
        scratch_shapes=[pltpu.VMEM((s, width), BF16), pltpu.VMEM((s, width), BF16), pltpu.VMEM((2, s, width), BF16),
                        state, state, state, state, pltpu.VMEM((rows, tk), BF16),
                        state, state, state, state] + (_exchange_sems(n_x) if n_x else []),
        compiler_params=_params(("arbitrary", "arbitrary")),
    )(u, u, u, u, o, dys, *x_arrs)


def _conv_bwd(u, conv_w, conv_b, dyc, name):
    s = u.shape[0]
    t = min(256, s)
    n_tiles = s // t

    def body(xc_ref, gb_ref, gc_ref, cg_ref, w_ref, b_ref, dy_ref,
             dxc_ref, dgb_ref, dgc_ref, dcg_ref, dw_ref, db_ref, zs, ds):
        zs[0:CONV_HALO, :] = jnp.zeros((CONV_HALO, LANES), F32)
        zs[CONV_HALO:, :] = gc_ref[...] * xc_ref[...]
        ds[s:, :] = jnp.zeros((CONV_HALO, LANES), F32)
        w0, w1, w2 = w_ref[0:1, :], w_ref[1:2, :], w_ref[2:3, :]
        bias = b_ref[...]

        def first(i, sums):
            t0 = pl.multiple_of(i * t, t)
            z0, z1, z2 = _conv_taps(zs, t0, t)
            pre = w0 * z2 + w1 * z1 + w2 * z0 + bias
            gate = cg_ref[pl.ds(t0, t), :]
            sg = _sigmoid(gate)
            gb = gb_ref[pl.ds(t0, t), :]
            dy = dy_ref[pl.ds(t0, t), :]
            dcg_ref[pl.ds(t0, t), :] = (dy * gb * pre * (sg * (1.0 + gate * (1.0 - sg)))).astype(BF16)
            dgb_ref[pl.ds(t0, t), :] = (dy * pre * (gate * sg)).astype(BF16)
            dc = dy * gb * (gate * sg)
            ds[pl.ds(t0, t), :] = dc
            red = lambda v: jnp.sum(v, axis=0, keepdims=True)
            return (sums[0] + red(dc * z2), sums[1] + red(dc * z1), sums[2] + red(dc * z0), sums[3] + red(dc))

        zrow = jnp.zeros((1, LANES), F32)
        sw0, sw1, sw2, sb = lax.fori_loop(0, n_tiles, first, (zrow, zrow, zrow, zrow))
        dw_ref[0:1, :] = sw0
        dw_ref[1:2, :] = sw1
        dw_ref[2:3, :] = sw2
        db_ref[...] = sb

        def second(i, carry):
            t0 = pl.multiple_of(i * t, t)
            ext = ds[pl.ds(t0, t + CONV_HALO), :]
            n = t + CONV_HALO
            d0 = ext[:t, :]
            d1 = pltpu.roll(ext, n - 1, 0)[:t, :]
            d2 = pltpu.roll(ext, n - 2, 0)[:t, :]
            dz = w2 * d0 + w1 * d1 + w0 * d2
            dgc_ref[pl.ds(t0, t), :] = (dz * xc_ref[pl.ds(t0, t), :]).astype(BF16)
            dxc_ref[pl.ds(t0, t), :] = (dz * gc_ref[pl.ds(t0, t), :]).astype(BF16)
            return carry

        lax.fori_loop(0, n_tiles, second, 0)

    col = lambda base: pl.BlockSpec((s, LANES), lambda j: (0, base + j))
    dcol = jax.ShapeDtypeStruct((s, WIDTH), BF16)
    return _pcall(
        body, name=name, grid=(4,),
        in_specs=[col(CB_CONV_X), col(CB_CONV_GB), col(CB_CONV_GC), col(CB_CONV_G),
                  pl.BlockSpec((3, LANES), lambda j: (0, j)), pl.BlockSpec((1, LANES), lambda j: (0, j)), col(0)],
        out_specs=(col(0), col(0), col(0), col(0),
                   pl.BlockSpec((3, LANES), lambda j: (0, j)), pl.BlockSpec((1, LANES), lambda j: (0, j))),
        out_shape=(dcol, dcol, dcol, dcol,
                   jax.ShapeDtypeStruct((3, WIDTH), F32), jax.ShapeDtypeStruct((1, WIDTH), F32)),
        scratch_shapes=[pltpu.VMEM((CONV_HALO + s, LANES), F32), pltpu.VMEM((s + CONV_HALO, LANES), F32)],
        compiler_params=_params(("arbitrary",)),
    )(u, u, u, u, conv_w, conv_b, dyc)


def _pool_bwd(u, pool_w, pool_scale, dyp, name):
    s = u.shape[0]
    t = min(256, s)
    n_tiles = s // t

    def body(pv_ref, pg_ref, w_ref, sc_ref, dy_ref, dpv_ref, dpg_ref, dw_ref, dsc_ref, vs, es, dps):
        grp = pl.program_id(0)
        vs[0:POOL_HALO, :] = jnp.zeros((POOL_HALO, LANES), F32)
        vs[POOL_HALO:, :] = pv_ref[...]
        es[s:, :] = jnp.zeros((POOL_HALO, LANES), F32)
        wb = w_ref[...].astype(BF16)
        scale = sc_ref[...]

        def first(i, sums):
            dw, dsc = sums
            t0 = pl.multiple_of(i * t, t)
            win, v = _pool_window(vs, t0, t, grp)
            cnt = _pool_count(t0, t, grp)
            pb = (win / cnt - v).astype(BF16)
            mixed = _dot(pb, wb)
            gate = pg_ref[pl.ds(t0, t), :]
            sg = _sigmoid(gate)
            dy = dy_ref[pl.ds(t0, t), :]
            dpg_ref[pl.ds(t0, t), :] = (dy * (mixed * scale) * (sg * (1.0 + gate * (1.0 - sg)))).astype(BF16)
            dms = dy * (gate * sg)
            dsc = dsc + jnp.sum(dms * mixed, axis=0, keepdims=True)
            dmb = (dms * scale).astype(BF16)
            dw = dw + _dot_tn(pb, dmb)
            dpooled = _dot_nt(dmb, wb)
            dps[pl.ds(t0, t), :] = dpooled
            es[pl.ds(t0, t), :] = dpooled / cnt
            return dw, dsc

        dw, dsc = lax.fori_loop(0, n_tiles, first, (jnp.zeros((LANES, LANES), F32), jnp.zeros((1, LANES), F32)))
        dw_ref[...] = dw
        dsc_ref[...] = dsc

        def second(i, carry):
            t0 = pl.multiple_of(i * t, t)
            ext = es[pl.ds(t0, t + POOL_HALO), :]
            n = t + POOL_HALO
            f2 = ext + pltpu.roll(ext, n - 1, 0)
            f4 = f2 + pltpu.roll(f2, n - 2, 0)
            f8 = f4 + pltpu.roll(f4, n - 4, 0)
            f16 = f8 + pltpu.roll(f8, n - 8, 0)
            sel = jnp.where(grp == 0, f2, jnp.where(grp == 1, f4, jnp.where(grp == 2, f8, f16)))
            dpv_ref[pl.ds(t0, t), :] = (sel[:t, :] - dps[pl.ds(t0, t), :]).astype(BF16)
            return carry

        lax.fori_loop(0, n_tiles, second, 0)

    col = lambda base: pl.BlockSpec((s, LANES), lambda g: (0, base + g))
    dcol = jax.ShapeDtypeStruct((s, WIDTH), BF16)
    return _pcall(
        body, name=name, grid=(4,),
        in_specs=[col(CB_POOL_V), col(CB_POOL_G), pl.BlockSpec((None, LANES, LANES), lambda g: (g, 0, 0)),
                  pl.BlockSpec((1, LANES), lambda g: (0, g)), col(0)],
        out_specs=(col(0), col(0), pl.BlockSpec((None, LANES, LANES), lambda g: (g, 0, 0)),
                   pl.BlockSpec((1, LANES), lambda g: (0, g))),
        out_shape=(dcol, dcol, jax.ShapeDtypeStruct((4, LANES, LANES), F32), jax.ShapeDtypeStruct((1, WIDTH), F32)),
        scratch_shapes=[pltpu.VMEM((POOL_HALO + s, LANES), F32), pltpu.VMEM((s + POOL_HALO, LANES), F32),
                        pltpu.VMEM((s, LANES), F32)],
        compiler_params=_params(("arbitrary",)),
    )(u, u, pool_w, pool_scale, dyp)


def _in_proj_bwd_x(du, w_all, x, g_pre, dy, name, xchg=None):
    s = x.shape[0]
    tm = min(1024, s)
    x_arrs, x_gather = xchg if xchg else ((), True)
    n_x = len(x_arrs)
    grid = (s // tm, N_DEV)

    def body(*refs):
        du_ref, w_ref, x_ref, g_ref, dy_ref = refs[:5]
        x_in, refs = refs[5:5 + n_x], refs[5 + n_x:]
        dx_ref, dg_ref = refs[:2]
        x_out, refs = refs[2:2 + n_x], refs[2 + n_x:]
        acc, x_sems = refs[0], refs[1:]
        i, k = pl.program_id(0), pl.program_id(1)
        if n_x:
            @pl.when((i == 0) & (k == 0))
            def _():
                _exchange_start(x_in, x_out, x_sems, x_gather)

        @pl.when(k == 0)
        def _():
            acc[...] = jnp.zeros_like(acc)

        @pl.when((k == 0) & (i == 0))
        def _():
            dg_ref[...] = jnp.zeros_like(dg_ref)
        acc[...] += _dot_nt(du_ref[...], w_ref[...])

        @pl.when(k == N_DEV - 1)
        def _():
            dh, xv = acc[...], x_ref[...]
            r = lax.rsqrt(jnp.mean(xv * xv, axis=-1, keepdims=True) + RMS_EPS)
            dg_ref[...] += jnp.sum(dh * xv * r, axis=0, keepdims=True)
            a = dh * g_ref[...]
            dx_ref[...] = dy_ref[...] + r * a - xv * (r * r * r) * jnp.mean(a * xv, axis=-1, keepdims=True)

        if n_x:
            @pl.when((i == grid[0] - 1) & (k == grid[1] - 1))
            def _():
                _exchange_wait(x_in, x_out, x_sems, x_gather)

    rows = lambda: pl.BlockSpec((tm, D_MODEL), lambda i, k: (i, 0))
    vec = lambda: pl.BlockSpec((1, D_MODEL), lambda i, k: (0, 0))
    return _pcall(
        body, name=name, grid=grid,
        in_specs=[pl.BlockSpec((tm, COLS_PER_DEV), lambda i, k: (i, k)),
                  pl.BlockSpec((None, D_MODEL, COLS_PER_DEV), lambda i, k: (k, 0, 0)),
                  rows(), vec(), rows()] + [ANY_SPEC] * n_x,
        out_specs=(rows(), vec()) + (ANY_SPEC,) * n_x,
        out_shape=(jax.ShapeDtypeStruct((s, D_MODEL), F32), jax.ShapeDtypeStruct((1, D_MODEL), F32))
        + tuple(_exchange_out_shapes(x_arrs, x_gather)),
        scratch_shapes=[pltpu.VMEM((tm, D_MODEL), F32)] + (_exchange_sems(n_x) if n_x else []),
        compiler_params=_params(("arbitrary", "arbitrary")),
    )(du, w_all, x, g_pre, dy, *x_arrs)


def _in_proj_bwd_send(h, du, w_all, x, g_pre, dy, name):
    s = x.shape[0]
    tk = s // N_DEV
    tm = min(1024, s)
    n_i = s // tm
    grid = (N_DEV + n_i, N_DEV)
    last = N_DEV - 1

    def body(me_ref, h_ref, duw_ref, dux_ref, w_ref, x_ref, g_ref, dy_ref, dx_ref, dg_ref, recv_ref,
             acc_w, stage, acc_x, send_sems, recv_sems, local_sem):
        r, k = pl.program_id(0), pl.program_id(1)
        x_, y_, c_ = lax.axis_index("x"), lax.axis_index("y"), lax.axis_index("c")
        me = 4 * x_ + 2 * y_ + c_
        flip = lambda v, bit: 1 - v if bit else v
        peer = lambda n: (flip(x_, (n >> 2) & 1), flip(y_, (n >> 1) & 1), flip(c_, n & 1))

        def out_copy(n, landing=None):
            if n == 0:
                return pltpu.make_async_copy(stage.at[0], recv_ref.at[me], local_sem)
            px, py, pc = peer(n)
            dst = recv_ref.at[me] if landing is None else recv_ref.at[4 * px + 2 * py + pc]
            return pltpu.make_async_remote_copy(
                src_ref=stage.at[n % 2], dst_ref=dst, send_sem=send_sems.at[n], recv_sem=recv_sems.at[n],
                device_id=(px, py, pc), device_id_type=pl.DeviceIdType.MESH)

        def wait_sent(n):
            if n == 0:
                out_copy(0).wait()
            else:
                out_copy(n).wait_send()

        @pl.when(r < N_DEV)
        def _():
            @pl.when(k == 0)
            def _():
                acc_w[...] = jnp.zeros_like(acc_w)
            acc_w[...] += _dot_tn(h_ref[...], duw_ref[...])

            for n in range(N_DEV):
                @pl.when((k == last) & (r == n))
                def _():
                    if n >= 2:
                        wait_sent(n - 2)
                    stage[n % 2] = acc_w[...].astype(BF16)
                    out_copy(n).start()

        @pl.when(r >= N_DEV)
        def _():
            @pl.when(k == 0)
            def _():
                acc_x[...] = jnp.zeros_like(acc_x)

            @pl.when((k == 0) & (r == N_DEV))
            def _():
                dg_ref[...] = jnp.zeros_like(dg_ref)
            acc_x[...] += _dot_nt(dux_ref[...], w_ref[...])

            @pl.when(k == last)
            def _():
                dh, xv = acc_x[...], x_ref[...]
                rs = lax.rsqrt(jnp.mean(xv * xv, axis=-1, keepdims=True) + RMS_EPS)
                dg_ref[...] += jnp.sum(dh * xv * rs, axis=0, keepdims=True)
                a = dh * g_ref[...]
                dx_ref[...] = dy_ref[...] + rs * a - xv * (rs * rs * rs) * jnp.mean(a * xv, axis=-1, keepdims=True)

        @pl.when((r == grid[0] - 1) & (k == last))
        def _():
            wait_sent(N_DEV - 2)
            wait_sent(N_DEV - 1)
            for n in range(1, N_DEV):
                out_copy(n, landing=True).wait_recv()

    in_w = lambda r: r < N_DEV
    row_x = lambda r: jnp.maximum(r - N_DEV, 0)
    rows = lambda: pl.BlockSpec((tm, D_MODEL), lambda r, k, me: (row_x(r), 0))
    vec = lambda: pl.BlockSpec((1, D_MODEL), lambda r, k, me: (0, 0))
    grid_spec = pltpu.PrefetchScalarGridSpec(
        num_scalar_prefetch=1, grid=grid,
        in_specs=[pl.BlockSpec((tk, D_MODEL), lambda r, k, me: (jnp.where(in_w(r), k, last), 0)),
                  pl.BlockSpec((tk, COLS_PER_DEV), lambda r, k, me: (jnp.where(in_w(r), k, last),
                                                                     jnp.bitwise_xor(me[0], jnp.minimum(r, last)))),
                  pl.BlockSpec((tm, COLS_PER_DEV), lambda r, k, me: (row_x(r), jnp.where(in_w(r), 0, k))),
                  pl.BlockSpec((None, D_MODEL, COLS_PER_DEV), lambda r, k, me: (jnp.where(in_w(r), 0, k), 0, 0)),
                  rows(), vec(), rows()],
        out_specs=(rows(), vec(), ANY_SPEC),
        scratch_shapes=[pltpu.VMEM((D_MODEL, COLS_PER_DEV), F32), pltpu.VMEM((2, D_MODEL, COLS_PER_DEV), BF16),
                        pltpu.VMEM((tm, D_MODEL), F32), pltpu.SemaphoreType.DMA((N_DEV,)),
                        pltpu.SemaphoreType.DMA((N_DEV,)), pltpu.SemaphoreType.DMA])
    me = 4 * lax.axis_index("x") + 2 * lax.axis_index("y") + lax.axis_index("c")
    return _pcall(
        body, name=name, grid_spec=grid_spec,
        out_shape=(jax.ShapeDtypeStruct((s, D_MODEL), F32), jax.ShapeDtypeStruct((1, D_MODEL), F32),
                   jax.ShapeDtypeStruct((N_DEV, D_MODEL, COLS_PER_DEV), BF16)),
        compiler_params=_params(("arbitrary", "arbitrary")),
    )(jnp.reshape(me, (1,)).astype(jnp.int32), h, du, du, w_all, x, g_pre, dy)


def _in_proj_bwd_w(h, du, name):
    s = h.shape[0]
    tk = min(512, s)
    n_k = s // tk

    def body(h_ref, du_ref, out_ref, acc):
        k = pl.program_id(1)

        @pl.when(k == 0)
        def _():
            acc[...] = jnp.zeros_like(acc)
        acc[...] += _dot_tn(h_ref[...], du_ref[...])

        @pl.when(k == n_k - 1)
        def _():
            out_ref[...] = acc[...].astype(BF16)

    return _pcall(
        body, name=name, grid=(N_DEV, n_k),
        in_specs=[pl.BlockSpec((tk, D_MODEL), lambda j, k: (k, 0)),
                  pl.BlockSpec((tk, COLS_PER_DEV), lambda j, k: (k, j))],
        out_specs=pl.BlockSpec((None, D_MODEL, COLS_PER_DEV), lambda j, k: (j, 0, 0)),
        out_shape=jax.ShapeDtypeStruct((N_DEV, D_MODEL, COLS_PER_DEV), BF16),
        scratch_shapes=[pltpu.VMEM((D_MODEL, COLS_PER_DEV), F32)],
        compiler_params=_params(("parallel", "arbitrary")),
    )(h, du)


def _adamw_math(g, w, m, v):
    m_new = ADAM_B1 * m + (1.0 - ADAM_B1) * g
    v_new = ADAM_B2 * v + (1.0 - ADAM_B2) * (g * g)
    m_hat = m_new / (1.0 - ADAM_B1 ** ADAM_STEP)
    v_hat = v_new / (1.0 - ADAM_B2 ** ADAM_STEP)
    delta = -ADAM_LR * (m_hat / (jnp.sqrt(v_hat) + ADAM_EPS) + ADAM_WD * w)
    return delta, m_new, v_new


def _sum_partials(p_ref):
    total = p_ref[0].astype(F32)
    for d in range(1, N_DEV):
        total = total + p_ref[d].astype(F32)
    return total


def _adamw_layers(parts0, parts1, w, m, v, name):
    _, r, c = w.shape
    tr = min(128, r)
    n_r = r // tr

    def body(p0_ref, p1_ref, w_ref, m_ref, v_ref, g_ref, d_ref, mo_ref, vo_ref):
        layer = pl.program_id(0)

        @pl.when(layer == 0)
        def _():
            g_ref[...] = _sum_partials(p0_ref)

        @pl.when(layer == 1)
        def _():
            g_ref[...] = _sum_partials(p1_ref)
        d_ref[...], mo_ref[...], vo_ref[...] = _adamw_math(g_ref[...], w_ref[...], m_ref[...], v_ref[...])

    part = lambda which: pl.BlockSpec((N_DEV, tr, c), lambda l, i: (0, jnp.where(l == which, i, 0), 0))
    par = lambda: pl.BlockSpec((None, tr, c), lambda l, i: (l, i, 0))
    out = jax.ShapeDtypeStruct(w.shape, F32)
    return _pcall(
        body, name=name, grid=(2, n_r),
        in_specs=[part(0), part(1), par(), par(), par()],
        out_specs=(par(), par(), par(), par()),
        out_shape=(out, out, out, out),
        compiler_params=_params(("arbitrary", "arbitrary")),
    )(parts0, parts1, w, m, v)


def _adamw_small(parts, w, m, v, name):
    def body(p_ref, w_ref, m_ref, v_ref, g_ref, d_ref, mo_ref, vo_ref):
        g = _sum_partials(p_ref)
        g_ref[...] = g
        d_ref[...], mo_ref[...], vo_ref[...] = _adamw_math(g, w_ref[...], m_ref[...], v_ref[...])

    out = jax.ShapeDtypeStruct(w.shape, F32)
    return _pcall(body, name=name, out_shape=(out, out, out, out), compiler_params=_params())(parts, w, m, v)


def _adamw_plain(g, w, m, v, name):
    def body(g_ref, w_ref, m_ref, v_ref, d_ref, mo_ref, vo_ref):
        d_ref[...], mo_ref[...], vo_ref[...] = _adamw_math(g_ref[...], w_ref[...], m_ref[...], v_ref[...])

    out = jax.ShapeDtypeStruct(w.shape, F32)
    return _pcall(body, name=name, out_shape=(out, out, out), compiler_params=_params())(g, w, m, v)


def _rows128(a):
    return a.reshape(-1, LANES)


SMALL_NAMES = ("pre_norm_g", "pool_w", "pool_scale", "conv_w", "conv_b", "post_norm_g")


def kernel(x, pre_norm_g, w_in, pool_w, pool_scale, conv_w, conv_b, w_branch, w_out, post_norm_g, loss_target, m_pre_norm_g, m_w_in, m_pool_w, m_pool_scale, m_conv_w, m_conv_b, m_w_branch, m_w_out, m_post_norm_g, v_pre_norm_g, v_w_in, v_pool_w, v_pool_scale, v_conv_w, v_conv_b, v_w_branch, v_w_out, v_post_norm_g):
    s = x.shape[1]
    me = 4 * lax.axis_index("x") + 2 * lax.axis_index("y") + lax.axis_index("c")
    x0 = x[0]
    target = loss_target[0]
    conv_cols = conv_w.shape[-1]

    conv_w_pad = jnp.pad(conv_w.reshape(2 * 3, conv_cols), ((0, 2), (0, LANES - conv_cols)))
    w_in_all = [None, None]
    w_in_all[0], cw_g = _gather_two_level([w_in[0].astype(BF16), conv_w_pad], "gather_w_in_0")
    conv_w_full = cw_g[:, :6, :conv_cols].reshape(N_DEV, 2, 3, conv_cols).transpose(1, 2, 0, 3).reshape(2, 3, WIDTH)
    later_weights = ([w_in[1].astype(BF16), w_branch.astype(BF16), w_out.astype(BF16)], True)

    saved = []
    xin = x0
    for l in range(2):
        u, h = _in_proj_fwd(xin, pre_norm_g[l:l + 1], w_in_all[l], f"in_proj_fwd_{l}")
        y_pool = _pool_fwd(u, pool_w[l], pool_scale[l:l + 1], f"pool_fwd_{l}")
        y_conv = _conv_fwd(u, conv_w_full[l], conv_b[l:l + 1], f"conv_fwd_{l}")
        if l == 0:
            o_sb, y_sb, w_in_all[1], wb_g, wo_all = _sb_fwd(u, f"sb_fwd_{l}", later_weights)
            wb_all = wb_g.transpose(1, 2, 3, 0, 4).reshape(2, 3, WIDTH, D_MODEL)
        else:
            o_sb, y_sb = _sb_fwd(u, f"sb_fwd_{l}")
        xout, merged, pre = _merge_out_fwd(y_pool, y_conv, y_sb, u, wb_all, wo_all, xin, post_norm_g[l:l + 1], l,
                                           f"merge_out_fwd_{l}")
        saved.append((xin, u, h, y_pool, y_conv, y_sb, o_sb, merged, pre))
        xin = xout

    dy, loss_row = _loss_and_grad(xin, target, "loss")

    small = [None, None]
    recv = [None, None]
    ready = []
    for l in (1, 0):
        xl, u, h, y_pool, y_conv, y_sb, o_sb, merged, pre = saved[l]
        dmerged, dwo, dg_post = _out_proj_bwd(dy, pre, post_norm_g[l:l + 1], merged, wo_all, l, f"out_proj_bwd_{l}")
        du_merge, dyp, dyc, dys, dwb = _merge_bwd(dmerged, y_pool, y_conv, y_sb, u, wb_all, l, f"merge_bwd_{l}")
        dwb = dwb.reshape(N_DEV, 3 * WIDTH, D_MODEL // N_DEV)
        dwo = dwo.reshape(N_DEV, D_MODEL // N_DEV, D_MODEL)
        dxc, dgb, dgc, dcg, dcw, dcb = _conv_bwd(u, conv_w_full[l], conv_b[l:l + 1], dyc, f"conv_bwd_{l}")
        dpv, dpg, dpw, dps = _pool_bwd(u, pool_w[l], pool_scale[l:l + 1], dyp, f"pool_bwd_{l}")
        small[l] = dict(pool_w=dpw, pool_scale=dps, conv_w=dcw, conv_b=dcb, post_norm_g=dg_post)
        if l == 1:
            dq, dk, dv, dsg = _sb_bwd(u, o_sb, dys, f"sb_bwd_{l}")
        else:
            small[l]["pre_norm_g"] = jnp.zeros((1, D_MODEL), F32)
            packed = jnp.concatenate(
                [_rows128(jnp.stack([small[0][n], small[1][n]])) for n in SMALL_NAMES]
                + [jnp.pad(loss_row, ((0, 7), (0, 0)))], axis=0)
            dq, dk, dv, dsg, *got, packed_all = _sb_bwd(
                u, o_sb, dys, f"sb_bwd_{l}", (ready + [dwb, dwo, packed], (False,) * 5 + (True,)))
            recv[1] = got[:3]
        du = jnp.concatenate([dpv, dpg, dxc, dgb, dgc, dcg, dq, dk.astype(BF16), dv.astype(BF16), dsg, du_merge],
                             axis=1)
        if l == 1:
            dwi = _in_proj_bwd_w(h, du, f"in_proj_bwd_w_{l}")
            ready = [dwi, dwb, dwo]
            dx, dg_pre = _in_proj_bwd_x(du, w_in_all[l], xl, pre_norm_g[l:l + 1], dy, f"in_proj_bwd_x_{l}")
            small[l]["pre_norm_g"] = dg_pre
        else:
            dx, dg_pre, got_dwi = _in_proj_bwd_send(h, du, w_in_all[l], xl, pre_norm_g[l:l + 1], dy,
                                                    f"in_proj_bwd_{l}")
            recv[0] = [got_dwi] + got[3:]
        dy = dx
    grad_x = dy[None]

    (g_pre_0_all,) = _exchange([_rows128(dg_pre)], True, "gather_g_pre_0")
    packed_all = lax.dynamic_update_slice(packed_all, g_pre_0_all, (0, 0, 0))
    sizes = dict(pre_norm_g=16, pool_w=1024, pool_scale=8, conv_w=24, conv_b=8, post_norm_g=16)
    n_rows = sum(sizes.values())
    loss = jnp.sum(packed_all[:, n_rows, 0])

    given = dict(pre_norm_g=(pre_norm_g, m_pre_norm_g, v_pre_norm_g), pool_w=(pool_w, m_pool_w, v_pool_w),
                 pool_scale=(pool_scale, m_pool_scale, v_pool_scale), conv_b=(conv_b, m_conv_b, v_conv_b),
                 post_norm_g=(post_norm_g, m_post_norm_g, v_post_norm_g))
    zeros_cw = jnp.zeros((sizes["conv_w"], LANES), F32)
    pack3 = [jnp.concatenate([zeros_cw if n == "conv_w" else _rows128(given[n][k]) for n in SMALL_NAMES], axis=0)
             for k in range(3)]
    sg, sd, sm, sv = _adamw_small(packed_all[:, :n_rows], pack3[0], pack3[1], pack3[2], "adamw_small")

    def unpack(buf, name, shape):
        start = 0
        for n in SMALL_NAMES:
            if n == name:
                return buf[start:start + sizes[n]].reshape(shape)
            start += sizes[n]

    out = {}
    for n in ("pre_norm_g", "pool_w", "pool_scale", "conv_b", "post_norm_g"):
        shape = given[n][0].shape
        out[n] = tuple(unpack(b, n, shape) for b in (sg, sd, sm, sv))
    g_cw = lax.dynamic_slice_in_dim(unpack(sg, "conv_w", (2, 3, WIDTH)), me * conv_cols, conv_cols, axis=2)
    cw2 = lambda a: a.reshape(6, conv_cols)
    d_cw, m_cw, v_cw = _adamw_plain(cw2(g_cw), cw2(conv_w), cw2(m_conv_w), cw2(v_conv_w), "adamw_conv_w")
    out["conv_w"] = (g_cw,) + tuple(a.reshape(2, 3, conv_cols) for a in (d_cw, m_cw, v_cw))

    out["w_in"] = _adamw_layers(recv[0][0], recv[1][0], w_in, m_w_in, v_w_in, "adamw_w_in")
    cols = D_MODEL // N_DEV
    wb3 = lambda a: a.reshape(2, 3 * WIDTH, cols)
    out["w_branch"] = tuple(a.reshape(2, 3, WIDTH, cols) for a in _adamw_layers(
        recv[0][1], recv[1][1], wb3(w_branch), wb3(m_w_branch), wb3(v_w_branch), "adamw_w_branch"))
    out["w_out"] = _adamw_layers(recv[0][2], recv[1][2], w_out, m_w_out, v_w_out, "adamw_w_out")

    order = ("pre_norm_g", "w_in", "pool_w", "pool_scale", "conv_w", "conv_b", "w_branch", "w_out", "post_norm_g")
    return (loss, grad_x) + tuple(out[n][k] for k in range(4) for n in order)
```

```python
import functools

import jax
import jax.numpy as jnp
from jax import lax
from jax.experimental import pallas as pl
from jax.experimental.pallas import tpu as pltpu

F32 = jnp.float32
BF16 = jnp.bfloat16

N_DEV = 8
D_MODEL = 1024
WIDTH = 512
N_IN = 8192
COLS_PER_DEV = N_IN // N_DEV
HEAD_DIM = 64
LANES = 128
SB_SCALE = HEAD_DIM ** -0.5
LOG2E = 1.4426950408889634
RMS_EPS = 1e-6
POOL_HALO = 16
CONV_HALO = 8
ADAM_LR, ADAM_B1, ADAM_B2, ADAM_EPS, ADAM_WD, ADAM_STEP = 0.001, 0.9, 0.999, 1e-08, 0.01, 10
VMEM_LIMIT = 60 * 1024 * 1024

CB_POOL_V, CB_POOL_G = 0, 4
CB_CONV_X, CB_CONV_GB, CB_CONV_GC, CB_CONV_G = 8, 12, 16, 20
CB_SB_Q, CB_SB_K, CB_SB_V, CB_SB_G = 24, 28, 32, 36
MERGE_BLOCK_1024 = 5


def _pcall(body, **kw):
    return pl.pallas_call(body, **kw)


def _params(sem=None):
    if sem is None:
        return pltpu.CompilerParams(vmem_limit_bytes=VMEM_LIMIT)
    return pltpu.CompilerParams(dimension_semantics=sem, vmem_limit_bytes=VMEM_LIMIT)


def _sigmoid(x):
    return 1.0 / (1.0 + jnp.exp(-x))


def _dot(a, b):
    return jnp.dot(a, b, preferred_element_type=F32)


def _dot_nt(a, b):
    return lax.dot_general(a, b, (((1,), (1,)), ((), ())), preferred_element_type=F32)


def _dot_tn(a, b):
    return lax.dot_general(a, b, (((0,), (0,)), ((), ())), preferred_element_type=F32)


def _split_bf16(x):
    hi = x.astype(BF16)
    lo = (x - hi.astype(F32)).astype(BF16)
    return hi, lo


N_PEER = N_DEV - 1
ANY_SPEC = pl.BlockSpec(memory_space=pl.ANY)


def _exchange_copies(ins, outs, send_sems, recv_sems, local_sems, gather, with_recvs=True):
    n = len(ins)
    gathers = _per_array(gather, n)
    x, y, c = lax.axis_index("x"), lax.axis_index("y"), lax.axis_index("c")
    me = 4 * x + 2 * y + c
    flip = lambda v, bit: 1 - v if bit else v
    local, sends, recvs = [], [], []
    for a in range(n):
        src = ins[a] if gathers[a] else ins[a].at[me]
        local.append(pltpu.make_async_copy(src, outs[a].at[me], local_sems.at[a]))
    for k in range(N_PEER):
        px, py, pc = flip(x, ((k + 1) >> 2) & 1), flip(y, ((k + 1) >> 1) & 1), flip(c, (k + 1) & 1)
        peer_id = 4 * px + 2 * py + pc
        for a in range(n):
            src = ins[a] if gathers[a] else ins[a].at[peer_id]
            common = dict(src_ref=src, send_sem=send_sems.at[a * N_PEER + k], recv_sem=recv_sems.at[a * N_PEER + k],
                          device_id=(px, py, pc), device_id_type=pl.DeviceIdType.MESH)
            sends.append(pltpu.make_async_remote_copy(dst_ref=outs[a].at[me], **common))
            if with_recvs:
                recvs.append(pltpu.make_async_remote_copy(dst_ref=outs[a].at[peer_id], **common))
    return local, sends, recvs


def _exchange_start(ins, outs, sems, gather):
    local, sends, _ = _exchange_copies(ins, outs, *sems, gather, with_recvs=False)
    for cp in local + sends:
        cp.start()


def _exchange_wait(ins, outs, sems, gather):
    local, sends, recvs = _exchange_copies(ins, outs, *sems, gather)
    for cp in recvs:
        cp.wait_recv()
    for cp in sends:
        cp.wait_send()
    for cp in local:
        cp.wait()


def _per_array(gather, n):
    return tuple(gather) if isinstance(gather, (tuple, list)) else (gather,) * n


def _exchange_out_shapes(arrs, gather):
    return [jax.ShapeDtypeStruct((N_DEV,) + tuple(a.shape if g else a.shape[1:]), a.dtype)
            for a, g in zip(arrs, _per_array(gather, len(arrs)))]


def _gather_two_level(arrs, name):
    n = len(arrs)

    def body(*refs):
        ins, outs = refs[:n], refs[n:2 * n]
        send_sems, recv_sems, local_sems = refs[2 * n:]
        x, y, c = lax.axis_index("x"), lax.axis_index("y"), lax.axis_index("c")
        me, sibling = (x, y, c), (x, y, 1 - c)
        chips = [(1 - x, y), (x, 1 - y), (1 - x, 1 - y)]
        slot = lambda dev: 4 * dev[0] + 2 * dev[1] + dev[2]

        def copy(a, k, block, to, src=None):
            return pltpu.make_async_remote_copy(
                src_ref=outs[a].at[slot(block)] if src is None else src, dst_ref=outs[a].at[slot(block)],
                send_sem=send_sems.at[a * N_PEER + k], recv_sem=recv_sems.at[a * N_PEER + k],
                device_id=to, device_id_type=pl.DeviceIdType.MESH)

        local = [pltpu.make_async_copy(ins[a], outs[a].at[slot(me)], local_sems.at[a]) for a in range(n)]
        first = []
        for a in range(n):
            first.append(copy(a, 0, me, sibling, src=ins[a]))
            first += [copy(a, 1 + j, me, (*chip, c), src=ins[a]) for j, chip in enumerate(chips)]
        for cp in local + first:
            cp.start()
        passed = []
        for j, chip in enumerate(chips):
            for a in range(n):
                copy(a, 1 + j, (*chip, c), me).wait_recv()
                passed.append(copy(a, 4 + j, (*chip, c), sibling))
                passed[-1].start()
        for a in range(n):
            copy(a, 0, sibling, me).wait_recv()
        for j, chip in enumerate(chips):
            for a in range(n):
                copy(a, 4 + j, (*chip, 1 - c), me).wait_recv()
        for cp in first + passed:
            cp.wait_send()
        for cp in local:
            cp.wait()

    return _pcall(
        body, name=name,
        out_shape=tuple(_exchange_out_shapes(arrs, True)),
        in_specs=[ANY_SPEC] * n, out_specs=tuple([ANY_SPEC] * n),
        scratch_shapes=_exchange_sems(n),
    )(*arrs)


def _exchange_sems(n):
    return [pltpu.SemaphoreType.DMA((n * N_PEER,)), pltpu.SemaphoreType.DMA((n * N_PEER,)),
            pltpu.SemaphoreType.DMA((n,))]


def _exchange(arrs, gather, name):
    n = len(arrs)

    def body(*refs):
        ins, outs, sems = refs[:n], refs[n:2 * n], refs[2 * n:]
        _exchange_start(ins, outs, sems, gather)
        _exchange_wait(ins, outs, sems, gather)

    return _pcall(
        body, name=name,
        out_shape=tuple(_exchange_out_shapes(arrs, gather)),
        in_specs=[ANY_SPEC] * n, out_specs=tuple([ANY_SPEC] * n),
        scratch_shapes=_exchange_sems(n),
    )(*arrs)


def _in_proj_fwd(x, g, w_all, name):
    s = x.shape[0]
    tm = min(1024, s)

    def body(x_ref, g_ref, w_ref, u_ref, h_ref, hs):
        @pl.when(pl.program_id(1) == 0)
        def _():
            xv = x_ref[...]
            r = lax.rsqrt(jnp.mean(xv * xv, axis=-1, keepdims=True) + RMS_EPS)
            hv = (xv * r * g_ref[...]).astype(BF16)
            hs[...] = hv
            h_ref[...] = hv
        u_ref[...] = _dot(hs[...], w_ref[...])

    return _pcall(
        body, name=name, grid=(s // tm, N_DEV),
        in_specs=[pl.BlockSpec((tm, D_MODEL), lambda i, j: (i, 0)),
                  pl.BlockSpec((1, D_MODEL), lambda i, j: (0, 0)),
                  pl.BlockSpec((None, D_MODEL, COLS_PER_DEV), lambda i, j: (j, 0, 0))],
        out_specs=(pl.BlockSpec((tm, COLS_PER_DEV), lambda i, j: (i, j)),
                   pl.BlockSpec((tm, D_MODEL), lambda i, j: (i, 0))),
        out_shape=(jax.ShapeDtypeStruct((s, N_IN), F32), jax.ShapeDtypeStruct((s, D_MODEL), BF16)),
        scratch_shapes=[pltpu.VMEM((tm, D_MODEL), BF16)],
        compiler_params=_params(("parallel", "arbitrary")),
    )(x, g, w_all)


def _pool_window(vs, t0, t, grp):
    ext = vs[pl.ds(t0, t + POOL_HALO), :]
    s2 = ext + pltpu.roll(ext, 1, 0)
    s4 = s2 + pltpu.roll(s2, 2, 0)
    s8 = s4 + pltpu.roll(s4, 4, 0)
    s16 = s8 + pltpu.roll(s8, 8, 0)
    sel = jnp.where(grp == 0, s2, jnp.where(grp == 1, s4, jnp.where(grp == 2, s8, s16)))
    return sel[POOL_HALO:, :], ext[POOL_HALO:, :]


def _pool_count(t0, t, grp):
    pos = t0 + lax.broadcasted_iota(jnp.int32, (t, 1), 0)
    return jnp.minimum(pos + 1, jnp.left_shift(2, grp)).astype(F32)


def _pool_fwd(u, pool_w, pool_scale, name):
    s = u.shape[0]
    t = min(256, s)

    def body(pv_ref, pg_ref, w_ref, sc_ref, y_ref, vs):
        grp = pl.program_id(0)
        vs[0:POOL_HALO, :] = jnp.zeros((POOL_HALO, LANES), F32)
        vs[POOL_HALO:, :] = pv_ref[...]
        wb = w_ref[...].astype(BF16)
        scale = sc_ref[...]

        def tile(i, carry):
            t0 = pl.multiple_of(i * t, t)
            win, v = _pool_window(vs, t0, t, grp)
            pooled = win / _pool_count(t0, t, grp) - v
            mixed = _dot(pooled.astype(BF16), wb)
            gate = pg_ref[pl.ds(t0, t), :]
            y_ref[pl.ds(t0, t), :] = (mixed * scale * (gate * _sigmoid(gate))).astype(BF16)
            return carry

        lax.fori_loop(0, s // t, tile, 0)

    return _pcall(
        body, name=name, grid=(4,),
        in_specs=[pl.BlockSpec((s, LANES), lambda g: (0, CB_POOL_V + g)),
                  pl.BlockSpec((s, LANES), lambda g: (0, CB_POOL_G + g)),
                  pl.BlockSpec((None, LANES, LANES), lambda g: (g, 0, 0)),
                  pl.BlockSpec((1, LANES), lambda g: (0, g))],
        out_specs=pl.BlockSpec((s, LANES), lambda g: (0, g)),
        out_shape=jax.ShapeDtypeStruct((s, WIDTH), BF16),
        scratch_shapes=[pltpu.VMEM((POOL_HALO + s, LANES), F32)],
        compiler_params=_params(("arbitrary",)),
    )(u, u, pool_w, pool_scale)


def _conv_taps(zs, t0, t):
    ext = zs[pl.ds(t0, t + CONV_HALO), :]
    z0 = ext[CONV_HALO:, :]
    z1 = pltpu.roll(ext, 1, 0)[CONV_HALO:, :]
    z2 = pltpu.roll(ext, 2, 0)[CONV_HALO:, :]
    return z0, z1, z2


def _conv_fwd(u, conv_w, conv_b, name):
    s = u.shape[0]
    t = min(256, s)

    def body(xc_ref, gb_ref, gc_ref, cg_ref, w_ref, b_ref, y_ref, zs):
        zs[0:CONV_HALO, :] = jnp.zeros((CONV_HALO, LANES), F32)
        zs[CONV_HALO:, :] = gc_ref[...] * xc_ref[...]
        w0, w1, w2 = w_ref[0:1, :], w_ref[1:2, :], w_ref[2:3, :]
        bias = b_ref[...]

        def tile(i, carry):
            t0 = pl.multiple_of(i * t, t)
            z0, z1, z2 = _conv_taps(zs, t0, t)
            conv = w0 * z2 + w1 * z1 + w2 * z0
            gate = cg_ref[pl.ds(t0, t), :]
            y = gb_ref[pl.ds(t0, t), :] * (conv + bias) * (gate * _sigmoid(gate))
            y_ref[pl.ds(t0, t), :] = y.astype(BF16)
            return carry

        lax.fori_loop(0, s // t, tile, 0)

    col = lambda base: pl.BlockSpec((s, LANES), lambda j: (0, base + j))
    return _pcall(
        body, name=name, grid=(4,),
        in_specs=[col(CB_CONV_X), col(CB_CONV_GB), col(CB_CONV_GC), col(CB_CONV_G),
                  pl.BlockSpec((3, LANES), lambda j: (0, j)),
                  pl.BlockSpec((1, LANES), lambda j: (0, j))],
        out_specs=pl.BlockSpec((s, LANES), lambda j: (0, j)),
        out_shape=jax.ShapeDtypeStruct((s, WIDTH), BF16),
        scratch_shapes=[pltpu.VMEM((CONV_HALO + s, LANES), F32)],
        compiler_params=_params(("arbitrary",)),
    )(u, u, u, u, conv_w, conv_b)


def _first_head_lanes(rows, width=LANES):
    lane = lax.broadcasted_iota(jnp.int32, (rows, width), 1)
    return jnp.bitwise_and(lane, LANES - 1) < HEAD_DIM


def _stack_heads(x, first):
    zero = jnp.zeros_like(x)
    return jnp.concatenate([jnp.where(first, x, zero), jnp.where(first, zero, x)], axis=0).astype(BF16)


def _causal_mask(tq, tk, copies):
    row = lax.broadcasted_iota(jnp.int32, (tq, tk), 0)
    col = lax.broadcasted_iota(jnp.int32, (tq, tk), 1)
    return jnp.concatenate([col < row] * copies, axis=0)


def _suffix_matrix(tk, inclusive, parts):
    r = lax.broadcasted_iota(jnp.int32, (parts * tk, 2 * tk), 0)
    c = lax.broadcasted_iota(jnp.int32, (parts * tk, 2 * tk), 1)
    r = jnp.bitwise_and(r, tk - 1)
    tri = (r >= c) if inclusive else (r > c)
    return jnp.where(c >= tk, 1.0, jnp.where(tri, 1.0, 0.0)).astype(BF16)


def _suffix_sums(x, m):
    hi, lo = _split_bf16(x)
    return _dot(jnp.concatenate([hi, lo], axis=1), m)


def _sb_log_terms(z, mask, m_strict):
    ls = jnp.minimum(z, 0.0) - jnp.log(1.0 + jnp.exp2(jnp.abs(z) * -LOG2E))
    lk = ls - z
    if mask is not None:
        lk = jnp.where(mask, lk, 0.0)
    return ls, _dot(lk.astype(BF16), m_strict)


SB_PAIRS = 4


def _pair_lanes(a):
    return slice(a * LANES, (a + 1) * LANES)


def _sb_fwd(u, name, xchg=None):
    s = u.shape[0]
    tq = tk = min(128, s)
    pairs = SB_PAIRS
    width = pairs * LANES
    rows = 2 * pairs * tq
    x_arrs, x_gather = xchg if xchg else ((), True)
    n_x = len(x_arrs)
    grid = (4 // pairs, s // tq)

    def body(*refs):
        q_ref, k_ref, v_ref, g_ref = refs[:4]
        x_in, refs = refs[4:4 + n_x], refs[4 + n_x:]
        o_ref, y_ref = refs[:2]
        x_out, refs = refs[2:2 + n_x], refs[2 + n_x:]
        kbf, vst, z_s, ell_s, carry_s = refs[:5]
        x_sems = refs[5:]
        i = pl.program_id(1)
        if n_x:
            @pl.when((pl.program_id(0) == 0) & (i == 0))
            def _():
                _exchange_start(x_in, x_out, x_sems, x_gather)

        @pl.when(i == 0)
        def _():
            kbf[...] = k_ref[...].astype(BF16)
            first_s = _first_head_lanes(s, width)
            vf = v_ref[...]
            vst[0] = jnp.where(first_s, vf, 0.0).astype(BF16)
            vst[1] = jnp.where(first_s, 0.0, vf).astype(BF16)

        first = _first_head_lanes(tq)
        mask = _causal_mask(tq, tk, 2 * pairs)
        m_strict = _suffix_matrix(tk, False, 1)
        qcat = jnp.concatenate([_stack_heads(q_ref[:, _pair_lanes(a)] * SB_SCALE, first) for a in range(pairs)],
                               axis=0)

        def scores(b):
            off = pl.multiple_of(jnp.maximum(b, 0) * tk, tk)
            z_s[...] = jnp.concatenate(
                [_dot_nt(qcat[a * 2 * tq:(a + 1) * 2 * tq], kbf[pl.ds(off, tk), _pair_lanes(a)])
                 for a in range(pairs)], axis=0)

        def log_weights(m):
            ls, cs = _sb_log_terms(z_s[...], m, m_strict)
            carry = carry_s[...]
            ell_s[...] = ls + cs[:, :tk] + carry
            carry_s[...] = carry + cs[:, tk:]

        def consume(b, accs, m):
            w = jnp.exp(ell_s[...])
            if m is not None:
                w = jnp.where(m, w, 0.0)
            wb = w.astype(BF16)
            off = pl.multiple_of(b * tk, tk)
            new = []
            for a in range(pairs):
                r0 = a * 2 * tq
                wcat = jnp.concatenate([wb[r0:r0 + tq], wb[r0 + tq:r0 + 2 * tq]], axis=1)
                vcat = jnp.concatenate([vst[0, pl.ds(off, tk), _pair_lanes(a)], vst[1, pl.ds(off, tk), _pair_lanes(a)]],
                                       axis=0)
                new.append(accs[a] + _dot(wcat, vcat))
            return tuple(new)

        carry_s[...] = jnp.zeros((rows, tk), F32)
        scores(i)
        log_weights(mask)
        scores(i - 1)
        accs = consume(i, tuple(jnp.zeros((tq, LANES), F32) for _ in range(pairs)), mask)
        log_weights(None)
        scores(i - 2)

        def step(n, accs):
            accs = consume(i - n, accs, None)
            log_weights(None)
            scores(i - n - 2)
            return accs

        accs = lax.fori_loop(1, i + 1, step, accs)
        o = jnp.concatenate(accs, axis=1)
        o_ref[...] = o
        gate = g_ref[...]
        y_ref[...] = (o * (gate * _sigmoid(gate))).astype(BF16)
        if n_x:
            @pl.when((pl.program_id(0) == grid[0] - 1) & (i == grid[1] - 1))
            def _():
                _exchange_wait(x_in, x_out, x_sems, x_gather)

    base = lambda cb: cb // pairs
    qblk = lambda cb: pl.BlockSpec((tq, width), lambda p, i: (i, base(cb) + p))
    full = lambda cb: pl.BlockSpec((s, width), lambda p, i: (0, base(cb) + p), pipeline_mode=pl.Buffered(1))
    state = pltpu.VMEM((rows, tk), F32)
    return _pcall(
        body, name=name, grid=grid,
        in_specs=[qblk(CB_SB_Q), full(CB_SB_K), full(CB_SB_V), qblk(CB_SB_G)] + [ANY_SPEC] * n_x,
        out_specs=(qblk(0), qblk(0)) + (ANY_SPEC,) * n_x,
        out_shape=(jax.ShapeDtypeStruct((s, WIDTH), F32), jax.ShapeDtypeStruct((s, WIDTH), BF16))
        + tuple(_exchange_out_shapes(x_arrs, x_gather)),
        scratch_shapes=[pltpu.VMEM((s, width), BF16), pltpu.VMEM((2, s, width), BF16), state, state, state]
        + (_exchange_sems(n_x) if n_x else []),
        compiler_params=_params(("arbitrary", "arbitrary")),
    )(u, u, u, u, *x_arrs)


def _merge_out_fwd(y_pool, y_conv, y_sb, u, wb_all, wo_all, x, g_post, layer, name):
    s = x.shape[0]
    tm = min(512, s)

    def body(yp, yc, ys, m0, m1, m2, wb_ref, wo_ref, x_ref, g_ref, out_ref, merged_ref, pre_ref):
        merged = jnp.zeros((tm, D_MODEL), F32)
        for n, (y_ref, m_ref) in enumerate(((yp, m0), (yc, m1), (ys, m2))):
            merged = merged + _sigmoid(m_ref[...]) * _dot(y_ref[...], wb_ref[n])
        mb = merged.astype(BF16)
        merged_ref[...] = mb
        pre = _dot(mb, wo_ref[...].reshape(D_MODEL, D_MODEL))
        pre_ref[...] = pre
        r = lax.rsqrt(jnp.mean(pre * pre, axis=-1, keepdims=True) + RMS_EPS)
        out_ref[...] = x_ref[...] + pre * r * g_ref[...]

    rows = lambda w: pl.BlockSpec((tm, w), lambda i: (i, 0))
    merge = lambda n: pl.BlockSpec((tm, D_MODEL), lambda i: (i, MERGE_BLOCK_1024 + n))
    return _pcall(
        body, name=name, grid=(s // tm,),
        in_specs=[rows(WIDTH), rows(WIDTH), rows(WIDTH), merge(0), merge(1), merge(2),
                  pl.BlockSpec((None, 3, WIDTH, D_MODEL), lambda i: (layer, 0, 0, 0)),
                  pl.BlockSpec((N_DEV, None, D_MODEL // N_DEV, D_MODEL), lambda i: (0, layer, 0, 0)),
                  rows(D_MODEL), pl.BlockSpec((1, D_MODEL), lambda i: (0, 0))],
        out_specs=(rows(D_MODEL), rows(D_MODEL), rows(D_MODEL)),
        out_shape=(jax.ShapeDtypeStruct((s, D_MODEL), F32), jax.ShapeDtypeStruct((s, D_MODEL), BF16),
                   jax.ShapeDtypeStruct((s, D_MODEL), F32)),
        compiler_params=_params(("arbitrary",)),
    )(y_pool, y_conv, y_sb, u, u, u, wb_all, wo_all, x, g_post)


def _loss_and_grad(y, target, name):
    s = y.shape[0]
    tm = min(512, s)

    def body(y_ref, t_ref, dy_ref, loss_ref, acc):
        i = pl.program_id(0)

        @pl.when(i == 0)
        def _():
            acc[...] = jnp.zeros_like(acc)
        err = y_ref[...] - t_ref[...]
        dy_ref[...] = err / D_MODEL
        acc[...] += jnp.sum(err * err, axis=0, keepdims=True)

        @pl.when(i == pl.num_programs(0) - 1)
        def _():
            total = jnp.sum(acc[...], axis=1, keepdims=True) * (0.5 / D_MODEL)
            loss_ref[...] = jnp.broadcast_to(total, (1, LANES))

    return _pcall(
        body, name=name, grid=(s // tm,),
        in_specs=[pl.BlockSpec((tm, D_MODEL), lambda i: (i, 0)), pl.BlockSpec((tm, D_MODEL), lambda i: (i, 0))],
        out_specs=(pl.BlockSpec((tm, D_MODEL), lambda i: (i, 0)), pl.BlockSpec((1, LANES), lambda i: (0, 0))),
        out_shape=(jax.ShapeDtypeStruct((s, D_MODEL), F32), jax.ShapeDtypeStruct((1, LANES), F32)),
        scratch_shapes=[pltpu.VMEM((1, D_MODEL), F32)],
        compiler_params=_params(("arbitrary",)),
    )(y, target)


def _out_proj_bwd(dy, pre, g_post, merged, wo_all, layer, name):
    s = dy.shape[0]
    tm = min(512, s)
    n_tiles = s // tm

    def body(dy_ref, pre_ref, g_ref, mg_ref, wo_ref, dm_ref, dwo_ref, dg_ref, acc):
        i = pl.program_id(0)

        @pl.when(i == 0)
        def _():
            acc[...] = jnp.zeros_like(acc)
            dg_ref[...] = jnp.zeros_like(dg_ref)
        dyv, pre_v = dy_ref[...], pre_ref[...]
        r = lax.rsqrt(jnp.mean(pre_v * pre_v, axis=-1, keepdims=True) + RMS_EPS)
        dg_ref[...] += jnp.sum(dyv * pre_v * r, axis=0, keepdims=True)
        a = dyv * g_ref[...]
        dpre = r * a - pre_v * (r * r * r) * jnp.mean(a * pre_v, axis=-1, keepdims=True)
        db = dpre.astype(BF16)
        acc[...] += _dot_tn(mg_ref[...], db)
        dm_ref[...] = _dot_nt(db, wo_ref[...].reshape(D_MODEL, D_MODEL))

        @pl.when(i == n_tiles - 1)
        def _():
            dwo_ref[...] = acc[...].astype(BF16)

    rows = lambda: pl.BlockSpec((tm, D_MODEL), lambda i: (i, 0))
    return _pcall(
        body, name=name, grid=(n_tiles,),
        in_specs=[rows(), rows(), pl.BlockSpec((1, D_MODEL), lambda i: (0, 0)), rows(),
                  pl.BlockSpec((N_DEV, None, D_MODEL // N_DEV, D_MODEL), lambda i: (0, layer, 0, 0))],
        out_specs=(rows(), pl.BlockSpec((D_MODEL, D_MODEL), lambda i: (0, 0)),
                   pl.BlockSpec((1, D_MODEL), lambda i: (0, 0))),
        out_shape=(jax.ShapeDtypeStruct((s, D_MODEL), F32), jax.ShapeDtypeStruct((D_MODEL, D_MODEL), BF16),
                   jax.ShapeDtypeStruct((1, D_MODEL), F32)),
        scratch_shapes=[pltpu.VMEM((D_MODEL, D_MODEL), F32)],
        compiler_params=_params(("arbitrary",)),
    )(dy, pre, g_post, merged, wo_all)


def _merge_bwd(dmerged, y_pool, y_conv, y_sb, u, wb_all, layer, name):
    s = dmerged.shape[0]
    tm = min(512, s)
    n_tiles = s // tm
    cols = D_MODEL // N_DEV

    def body(dm_ref, yp, yc, ys, m0, m1, m2, wb_ref, dum_ref, dyp, dyc, dys, dwb_ref, acc):
        i = pl.program_id(0)

        @pl.when(i == 0)
        def _():
            acc[...] = jnp.zeros_like(acc)
        dm = dm_ref[...]
        for n, (y_ref, m_ref, dy_ref) in enumerate(((yp, m0, dyp), (yc, m1, dyc), (ys, m2, dys))):
            yv = y_ref[...]
            wb = wb_ref[n]
            gate = _sigmoid(m_ref[...])
            proj = _dot(yv, wb)
            dum_ref[:, n * D_MODEL:(n + 1) * D_MODEL] = (dm * proj * gate * (1.0 - gate)).astype(BF16)
            dproj = (dm * gate).astype(BF16)
            acc[n] += _dot_tn(yv, dproj)
            dy_ref[...] = _dot_nt(dproj, wb)

        @pl.when(i == n_tiles - 1)
        def _():
            for j in range(N_DEV):
                for n in range(3):
                    dwb_ref[j, n] = acc[n, :, j * cols:(j + 1) * cols].astype(BF16)

    rows = lambda w: pl.BlockSpec((tm, w), lambda i: (i, 0))
    merge = lambda n: pl.BlockSpec((tm, D_MODEL), lambda i: (i, MERGE_BLOCK_1024 + n))
    return _pcall(
        body, name=name, grid=(n_tiles,),
        in_specs=[rows(D_MODEL), rows(WIDTH), rows(WIDTH), rows(WIDTH), merge(0), merge(1), merge(2),
                  pl.BlockSpec((None, 3, WIDTH, D_MODEL), lambda i: (layer, 0, 0, 0))],
        out_specs=(rows(3 * D_MODEL), rows(WIDTH), rows(WIDTH), rows(WIDTH),
                   pl.BlockSpec((N_DEV, 3, WIDTH, cols), lambda i: (0, 0, 0, 0))),
        out_shape=(jax.ShapeDtypeStruct((s, 3 * D_MODEL), BF16),
                   jax.ShapeDtypeStruct((s, WIDTH), F32), jax.ShapeDtypeStruct((s, WIDTH), F32),
                   jax.ShapeDtypeStruct((s, WIDTH), F32),
                   jax.ShapeDtypeStruct((N_DEV, 3, WIDTH, cols), BF16)),
        scratch_shapes=[pltpu.VMEM((3, WIDTH, D_MODEL), F32)],
        compiler_params=_params(("arbitrary",)),
    )(dmerged, y_pool, y_conv, y_sb, u, u, u, wb_all)


def _sb_bwd(u, o, dys, name, xchg=None):
    s = u.shape[0]
    tq = tk = min(128, s)
    pairs = SB_PAIRS
    width = pairs * LANES
    rows = 2 * pairs * tq
    pair_rows = lambda a: slice(a * 2 * tq, (a + 1) * 2 * tq)

    x_arrs, x_gather = xchg if xchg else ((), True)
    n_x = len(x_arrs)
    grid = (4 // pairs, s // tq)

    def body(*refs):
        q_ref, k_ref, v_ref, g_ref, o_ref, dys_ref = refs[:6]
        x_in, refs = refs[6:6 + n_x], refs[6 + n_x:]
        dq_ref, dk_ref, dv_ref, dg_ref = refs[:4]
        x_out, refs = refs[4:4 + n_x], refs[4 + n_x:]
        kbf, vbf, kst, z_s, ell_s, ls_s, cl_s, wb_s, g_s, bef_s, cg_s, beta_s = refs[:12]
        x_sems = refs[12:]
        i = pl.program_id(1)
        if n_x:
            @pl.when((pl.program_id(0) == 0) & (i == 0))
            def _():
                _exchange_start(x_in, x_out, x_sems, x_gather)

        @pl.when(i == 0)
        def _():
            dk_ref[...] = jnp.zeros_like(dk_ref)
            dv_ref[...] = jnp.zeros_like(dv_ref)
            kf = k_ref[...]
            kbf[...] = kf.astype(BF16)
            vbf[...] = v_ref[...].astype(BF16)
            first_s = _first_head_lanes(s, width)
            kst[0] = jnp.where(first_s, kf, 0.0).astype(BF16)
            kst[1] = jnp.where(first_s, 0.0, kf).astype(BF16)

        first = _first_head_lanes(tq)
        mask = _causal_mask(tq, tk, 2 * pairs)
        m_strict = _suffix_matrix(tk, False, 1)
        m_incl = _suffix_matrix(tk, True, 2)

        gate = g_ref[...]
        sg = _sigmoid(gate)
        dy = dys_ref[...]
        ov = o_ref[...]
        dg_ref[...] = (dy * ov * (sg * (1.0 + gate * (1.0 - sg)))).astype(BF16)
        do = (dy * (gate * sg)).astype(BF16)
        prod = do.astype(F32) * ov
        row_sum = lambda v: jnp.broadcast_to(jnp.sum(v, axis=1, keepdims=True), (tq, tk))
        dsum, docat, qcat = [], [], []
        for a in range(pairs):
            pa = prod[:, _pair_lanes(a)]
            dsum += [row_sum(jnp.where(first, pa, 0.0)), row_sum(jnp.where(first, 0.0, pa))]
            docat.append(_stack_heads(do[:, _pair_lanes(a)], first))
            qcat.append(_stack_heads(q_ref[:, _pair_lanes(a)] * SB_SCALE, first))
        dsum = jnp.concatenate(dsum, axis=0)

        def block_start(b):
            return pl.multiple_of(jnp.maximum(b, 0) * tk, tk)

        def scores(b):
            off = block_start(b)
            z_s[...] = jnp.concatenate([_dot_nt(qcat[a], kbf[pl.ds(off, tk), _pair_lanes(a)]) for a in range(pairs)],
                                       axis=0)

        def log_weights(m):
            ls, cs = _sb_log_terms(z_s[...], m, m_strict)
            cl = cl_s[...]
            ell_s[...] = ls + cs[:, :tk] + cl
            cl_s[...] = cl + cs[:, tk:]
            ls_s[...] = ls

        def weights(b, m):
            off = block_start(b)
            dwt = jnp.concatenate([_dot_nt(docat[a], vbf[pl.ds(off, tk), _pair_lanes(a)]) for a in range(pairs)],
                                  axis=0)
            w = jnp.exp(ell_s[...])
            if m is not None:
                w = jnp.where(m, w, 0.0)
            wb = w.astype(BF16)
            g = dwt * wb.astype(F32)
            gs = _suffix_sums(g, m_incl)
            cg = cg_s[...]
            beta = jnp.exp(ls_s[...])
            wb_s[...] = wb
            beta_s[...] = beta
            g_s[...] = g * (1.0 - beta)
            bef_s[...] = gs[:, :tk] + cg
            cg_s[...] = cg + gs[:, tk:]

        def grads(b, dqs, m):
            dz = g_s[...] - beta_s[...] * (dsum - bef_s[...])
            if m is not None:
                dz = jnp.where(m, dz, 0.0)
            dzb = dz.astype(BF16)
            wb = wb_s[...]
            off = pl.multiple_of(b * tk, tk)
            new = []
            for a in range(pairs):
                r0 = a * 2 * tq
                kcat = jnp.concatenate([kst[0, pl.ds(off, tk), _pair_lanes(a)], kst[1, pl.ds(off, tk), _pair_lanes(a)]],
                                       axis=0)
                new.append(dqs[a] + _dot(jnp.concatenate([dzb[r0:r0 + tq], dzb[r0 + tq:r0 + 2 * tq]], axis=1), kcat))
                dk_ref[pl.ds(off, tk), _pair_lanes(a)] += _dot_tn(dzb[pair_rows(a)], qcat[a])
                dv_ref[pl.ds(off, tk), _pair_lanes(a)] += _dot_tn(wb[pair_rows(a)], docat[a])
            return tuple(new)

        zero = jnp.zeros((rows, tk), F32)
        cl_s[...] = zero
        cg_s[...] = zero
        scores(i)
        log_weights(mask)
        scores(i - 1)
        weights(i, mask)
        log_weights(None)
        scores(i - 2)
        dqs = grads(i, tuple(jnp.zeros((tq, LANES), F32) for _ in range(pairs)), mask)
        weights(i - 1, None)
        log_weights(None)
        scores(i - 3)

        def step(n, dqs):
            dqs = grads(i - n, dqs, None)
            weights(i - n - 1, None)
            log_weights(None)
            scores(i - n - 3)
            return dqs

        dqs = lax.fori_loop(1, i + 1, step, dqs)
        dq_ref[...] = (jnp.concatenate(dqs, axis=1) * SB_SCALE).astype(BF16)
        if n_x:
            @pl.when((pl.program_id(0) == grid[0] - 1) & (i == grid[1] - 1))
            def _():
                _exchange_wait(x_in, x_out, x_sems, x_gather)

    base = lambda cb: cb // pairs
    qblk = lambda cb: pl.BlockSpec((tq, width), lambda p, i: (i, base(cb) + p))
    full = lambda cb: pl.BlockSpec((s, width), lambda p, i: (0, base(cb) + p), pipeline_mode=pl.Buffered(1))
    state = pltpu.VMEM((rows, tk), F32)
    return _pcall(
        body, name=name, grid=grid,
        in_specs=[qblk(CB_SB_Q), full(CB_SB_K), full(CB_SB_V), qblk(CB_SB_G), qblk(0), qblk(0)] + [ANY_SPEC] * n_x,
        out_specs=(qblk(0), full(0), full(0), qblk(0)) + (ANY_SPEC,) * n_x,
        out_shape=(jax.ShapeDtypeStruct((s, WIDTH), BF16), jax.ShapeDtypeStruct((s, WIDTH), F32),
                   jax.ShapeDtypeStruct((s, WIDTH), F32), jax.ShapeDtypeStruct((s, WIDTH), BF16))
        + tuple(_exchange_out_shapes(x_arrs, x_gather)),
        scratch_shapes=[pltpu.VMEM((s, width), BF16), pltpu.VMEM((s, width), BF16), pltpu.VMEM((2, s, width), BF16),
                        state, state, state, state, pltpu.VMEM((rows, tk), BF16),
                        state, state, state, state] + (_exchange_sems(n_x) if n_x else []),
        compiler_params=_params(("arbitrary", "arbitrary")),
    )(u, u, u, u, o, dys, *x_arrs)


def _conv_bwd(u, conv_w, conv_b, dyc, name):
    s = u.shape[0]
    t = min(256, s)
    n_tiles = s // t

    def body(xc_ref, gb_ref, gc_ref, cg_ref, w_ref, b_ref, dy_ref,
             dxc_ref, dgb_ref, dgc_ref, dcg_ref, dw_ref, db_ref, zs, ds):
        zs[0:CONV_HALO, :] = jnp.zeros((CONV_HALO, LANES), F32)
        zs[CONV_HALO:, :] = gc_ref[...] * xc_ref[...]
        ds[s:, :] = jnp.zeros((CONV_HALO, LANES), F32)
        w0, w1, w2 = w_ref[0:1, :], w_ref[1:2, :], w_ref[2:3, :]
        bias = b_ref[...]

        def first(i, sums):
            t0 = pl.multiple_of(i * t, t)
            z0, z1, z2 = _conv_taps(zs, t0, t)
            pre = w0 * z2 + w1 * z1 + w2 * z0 + bias
            gate = cg_ref[pl.ds(t0, t), :]
            sg = _sigmoid(gate)
            gb = gb_ref[pl.ds(t0, t), :]
            dy = dy_ref[pl.ds(t0, t), :]
            dcg_ref[pl.ds(t0, t), :] = (dy * gb * pre * (sg * (1.0 + gate * (1.0 - sg)))).astype(BF16)
            dgb_ref[pl.ds(t0, t), :] = (dy * pre * (gate * sg)).astype(BF16)
            dc = dy * gb * (gate * sg)
            ds[pl.ds(t0, t), :] = dc
            red = lambda v: jnp.sum(v, axis=0, keepdims=True)
            return (sums[0] + red(dc * z2), sums[1] + red(dc * z1), sums[2] + red(dc * z0), sums[3] + red(dc))

        zrow = jnp.zeros((1, LANES), F32)
        sw0, sw1, sw2, sb = lax.fori_loop(0, n_tiles, first, (zrow, zrow, zrow, zrow))
        dw_ref[0:1, :] = sw0
        dw_ref[1:2, :] = sw1
        dw_ref[2:3, :] = sw2
        db_ref[...] = sb

        def second(i, carry):
            t0 = pl.multiple_of(i * t, t)
            ext = ds[pl.ds(t0, t + CONV_HALO), :]
            n = t + CONV_HALO
            d0 = ext[:t, :]
            d1 = pltpu.roll(ext, n - 1, 0)[:t, :]
            d2 = pltpu.roll(ext, n - 2, 0)[:t, :]
            dz = w2 * d0 + w1 * d1 + w0 * d2
            dgc_ref[pl.ds(t0, t), :] = (dz * xc_ref[pl.ds(t0, t), :]).astype(BF16)
            dxc_ref[pl.ds(t0, t), :] = (dz * gc_ref[pl.ds(t0, t), :]).astype(BF16)
            return carry

        lax.fori_loop(0, n_tiles, second, 0)

    col = lambda base: pl.BlockSpec((s, LANES), lambda j: (0, base + j))
    dcol = jax.ShapeDtypeStruct((s, WIDTH), BF16)
    return _pcall(
        body, name=name, grid=(4,),
        in_specs=[col(CB_CONV_X), col(CB_CONV_GB), col(CB_CONV_GC), col(CB_CONV_G),
                  pl.BlockSpec((3, LANES), lambda j: (0, j)), pl.BlockSpec((1, LANES), lambda j: (0, j)), col(0)],
        out_specs=(col(0), col(0), col(0), col(0),
                   pl.BlockSpec((3, LANES), lambda j: (0, j)), pl.BlockSpec((1, LANES), lambda j: (0, j))),
        out_shape=(dcol, dcol, dcol, dcol,
                   jax.ShapeDtypeStruct((3, WIDTH), F32), jax.ShapeDtypeStruct((1, WIDTH), F32)),
        scratch_shapes=[pltpu.VMEM((CONV_HALO + s, LANES), F32), pltpu.VMEM((s + CONV_HALO, LANES), F32)],
        compiler_params=_params(("arbitrary",)),
    )(u, u, u, u, conv_w, conv_b, dyc)


def _pool_bwd(u, pool_w, pool_scale, dyp, name):
    s = u.shape[0]
    t = min(256, s)
    n_tiles = s // t

    def body(pv_ref, pg_ref, w_ref, sc_ref, dy_ref, dpv_ref, dpg_ref, dw_ref, dsc_ref, vs, es, dps):
        grp = pl.program_id(0)
        vs[0:POOL_HALO, :] = jnp.zeros((POOL_HALO, LANES), F32)
        vs[POOL_HALO:, :] = pv_ref[...]
        es[s:, :] = jnp.zeros((POOL_HALO, LANES), F32)
        wb = w_ref[...].astype(BF16)
        scale = sc_ref[...]

        def first(i, sums):
            dw, dsc = sums
            t0 = pl.multiple_of(i * t, t)
            win, v = _pool_window(vs, t0, t, grp)
            cnt = _pool_count(t0, t, grp)
            pb = (win / cnt - v).astype(BF16)
            mixed = _dot(pb, wb)
            gate = pg_ref[pl.ds(t0, t), :]
            sg = _sigmoid(gate)
            dy = dy_ref[pl.ds(t0, t), :]
            dpg_ref[pl.ds(t0, t), :] = (dy * (mixed * scale) * (sg * (1.0 + gate * (1.0 - sg)))).astype(BF16)
            dms = dy * (gate * sg)
            dsc = dsc + jnp.sum(dms * mixed, axis=0, keepdims=True)
            dmb = (dms * scale).astype(BF16)
            dw = dw + _dot_tn(pb, dmb)
            dpooled = _dot_nt(dmb, wb)
            dps[pl.ds(t0, t), :] = dpooled
            es[pl.ds(t0, t), :] = dpooled / cnt
            return dw, dsc

        dw, dsc = lax.fori_loop(0, n_tiles, first, (jnp.zeros((LANES, LANES), F32), jnp.zeros((1, LANES), F32)))
        dw_ref[...] = dw
        dsc_ref[...] = dsc

        def second(i, carry):
            t0 = pl.multiple_of(i * t, t)
            ext = es[pl.ds(t0, t + POOL_HALO), :]
            n = t + POOL_HALO
            f2 = ext + pltpu.roll(ext, n - 1, 0)
            f4 = f2 + pltpu.roll(f2, n - 2, 0)
            f8 = f4 + pltpu.roll(f4, n - 4, 0)
            f16 = f8 + pltpu.roll(f8, n - 8, 0)
            sel = jnp.where(grp == 0, f2, jnp.where(grp == 1, f4, jnp.where(grp == 2, f8, f16)))
            dpv_ref[pl.ds(t0, t), :] = (sel[:t, :] - dps[pl.ds(t0, t), :]).astype(BF16)
            return carry

        lax.fori_loop(0, n_tiles, second, 0)

    col = lambda base: pl.BlockSpec((s, LANES), lambda g: (0, base + g))
    dcol = jax.ShapeDtypeStruct((s, WIDTH), BF16)
    return _pcall(
        body, name=name, grid=(4,),
        in_specs=[col(CB_POOL_V), col(CB_POOL_G), pl.BlockSpec((None, LANES, LANES), lambda g: (g, 0, 0)),
                  pl.BlockSpec((1, LANES), lambda g: (0, g)), col(0)],
        out_specs=(col(0), col(0), pl.BlockSpec((None, LANES, LANES), lambda g: (g, 0, 0)),
                   pl.BlockSpec((1, LANES), lambda g: (0, g))),
        out_shape=(dcol, dcol, jax.ShapeDtypeStruct((4, LANES, LANES), F32), jax.ShapeDtypeStruct((1, WIDTH), F32)),
        scratch_shapes=[pltpu.VMEM((POOL_HALO + s, LANES), F32), pltpu.VMEM((s + POOL_HALO, LANES), F32),
                        pltpu.VMEM((s, LANES), F32)],
        compiler_params=_params(("arbitrary",)),
    )(u, u, pool_w, pool_scale, dyp)


def _in_proj_bwd_x(du, w_all, x, g_pre, dy, name, xchg=None):
    s = x.shape[0]
    tm = min(1024, s)
    x_arrs, x_gather = xchg if xchg else ((), True)
    n_x = len(x_arrs)
    grid = (s // tm, N_DEV)

    def body(*refs):
        du_ref, w_ref, x_ref, g_ref, dy_ref = refs[:5]
        x_in, refs = refs[5:5 + n_x], refs[5 + n_x:]
        dx_ref, dg_ref = refs[:2]
        x_out, refs = refs[2:2 + n_x], refs[2 + n_x:]
        acc, x_sems = refs[0], refs[1:]
        i, k = pl.program_id(0), pl.program_id(1)
        if n_x:
            @pl.when((i == 0) & (k == 0))
            def _():
                _exchange_start(x_in, x_out, x_sems, x_gather)

        @pl.when(k == 0)
        def _():
            acc[...] = jnp.zeros_like(acc)

        @pl.when((k == 0) & (i == 0))
        def _():
            dg_ref[...] = jnp.zeros_like(dg_ref)
        acc[...] += _dot_nt(du_ref[...], w_ref[...])

        @pl.when(k == N_DEV - 1)
        def _():
            dh, xv = acc[...], x_ref[...]
            r = lax.rsqrt(jnp.mean(xv * xv, axis=-1, keepdims=True) + RMS_EPS)
            dg_ref[...] += jnp.sum(dh * xv * r, axis=0, keepdims=True)
            a = dh * g_ref[...]
            dx_ref[...] = dy_ref[...] + r * a - xv * (r * r * r) * jnp.mean(a * xv, axis=-1, keepdims=True)

        if n_x:
            @pl.when((i == grid[0] - 1) & (k == grid[1] - 1))
            def _():
                _exchange_wait(x_in, x_out, x_sems, x_gather)

    rows = lambda: pl.BlockSpec((tm, D_MODEL), lambda i, k: (i, 0))
    vec = lambda: pl.BlockSpec((1, D_MODEL), lambda i, k: (0, 0))
    return _pcall(
        body, name=name, grid=grid,
        in_specs=[pl.BlockSpec((tm, COLS_PER_DEV), lambda i, k: (i, k)),
                  pl.BlockSpec((None, D_MODEL, COLS_PER_DEV), lambda i, k: (k, 0, 0)),
                  rows(), vec(), rows()] + [ANY_SPEC] * n_x,
        out_specs=(rows(), vec()) + (ANY_SPEC,) * n_x,
        out_shape=(jax.ShapeDtypeStruct((s, D_MODEL), F32), jax.ShapeDtypeStruct((1, D_MODEL), F32))
        + tuple(_exchange_out_shapes(x_arrs, x_gather)),
        scratch_shapes=[pltpu.VMEM((tm, D_MODEL), F32)] + (_exchange_sems(n_x) if n_x else []),
        compiler_params=_params(("arbitrary", "arbitrary")),
    )(du, w_all, x, g_pre, dy, *x_arrs)


def _in_proj_bwd_send(h, du, w_all, x, g_pre, dy, name):
    s = x.shape[0]
    tk = s // N_DEV
    tm = min(1024, s)
    n_i = s // tm
    grid = (N_DEV + n_i, N_DEV)
    last = N_DEV - 1

    def body(me_ref, h_ref, duw_ref, dux_ref, w_ref, x_ref, g_ref, dy_ref, dx_ref, dg_ref, recv_ref,
             acc_w, stage, acc_x, send_sems, recv_sems, local_sem):
        r, k = pl.program_id(0), pl.program_id(1)
        x_, y_, c_ = lax.axis_index("x"), lax.axis_index("y"), lax.axis_index("c")
        me = 4 * x_ + 2 * y_ + c_
        flip = lambda v, bit: 1 - v if bit else v
        peer = lambda n: (flip(x_, (n >> 2) & 1), flip(y_, (n >> 1) & 1), flip(c_, n & 1))

        def out_copy(n, landing=None):
            if n == 0:
                return pltpu.make_async_copy(stage.at[0], recv_ref.at[me], local_sem)
            px, py, pc = peer(n)
            dst = recv_ref.at[me] if landing is None else recv_ref.at[4 * px + 2 * py + pc]
            return pltpu.make_async_remote_copy(
                src_ref=stage.at[n % 2], dst_ref=dst, send_sem=send_sems.at[n], recv_sem=recv_sems.at[n],
                device_id=(px, py, pc), device_id_type=pl.DeviceIdType.MESH)

        def wait_sent(n):
            if n == 0:
                out_copy(0).wait()
            else:
                out_copy(n).wait_send()

        @pl.when(r < N_DEV)
        def _():
            @pl.when(k == 0)
            def _():
                acc_w[...] = jnp.zeros_like(acc_w)
            acc_w[...] += _dot_tn(h_ref[...], duw_ref[...])

            for n in range(N_DEV):
                @pl.when((k == last) & (r == n))
                def _():
                    if n >= 2:
                        wait_sent(n - 2)
                    stage[n % 2] = acc_w[...].astype(BF16)
                    out_copy(n).start()

        @pl.when(r >= N_DEV)
        def _():
            @pl.when(k == 0)
            def _():
                acc_x[...] = jnp.zeros_like(acc_x)

            @pl.when((k == 0) & (r == N_DEV))
            def _():
                dg_ref[...] = jnp.zeros_like(dg_ref)
            acc_x[...] += _dot_nt(dux_ref[...], w_ref[...])

            @pl.when(k == last)
            def _():
                dh, xv = acc_x[...], x_ref[...]
                rs = lax.rsqrt(jnp.mean(xv * xv, axis=-1, keepdims=True) + RMS_EPS)
                dg_ref[...] += jnp.sum(dh * xv * rs, axis=0, keepdims=True)
                a = dh * g_ref[...]
                dx_ref[...] = dy_ref[...] + rs * a - xv * (rs * rs * rs) * jnp.mean(a * xv, axis=-1, keepdims=True)

        @pl.when((r == grid[0] - 1) & (k == last))
        def _():
            wait_sent(N_DEV - 2)
            wait_sent(N_DEV - 1)
            for n in range(1, N_DEV):
                out_copy(n, landing=True).wait_recv()

    in_w = lambda r: r < N_DEV
    row_x = lambda r: jnp.maximum(r - N_DEV, 0)
    rows = lambda: pl.BlockSpec((tm, D_MODEL), lambda r, k, me: (row_x(r), 0))
    vec = lambda: pl.BlockSpec((1, D_MODEL), lambda r, k, me: (0, 0))
    grid_spec = pltpu.PrefetchScalarGridSpec(
        num_scalar_prefetch=1, grid=grid,
        in_specs=[pl.BlockSpec((tk, D_MODEL), lambda r, k, me: (jnp.where(in_w(r), k, last), 0)),
                  pl.BlockSpec((tk, COLS_PER_DEV), lambda r, k, me: (jnp.where(in_w(r), k, last),
                                                                     jnp.bitwise_xor(me[0], jnp.minimum(r, last)))),
                  pl.BlockSpec((tm, COLS_PER_DEV), lambda r, k, me: (row_x(r), jnp.where(in_w(r), 0, k))),
                  pl.BlockSpec((None, D_MODEL, COLS_PER_DEV), lambda r, k, me: (jnp.where(in_w(r), 0, k), 0, 0)),
                  rows(), vec(), rows()],
        out_specs=(rows(), vec(), ANY_SPEC),
        scratch_shapes=[pltpu.VMEM((D_MODEL, COLS_PER_DEV), F32), pltpu.VMEM((2, D_MODEL, COLS_PER_DEV), BF16),
                        pltpu.VMEM((tm, D_MODEL), F32), pltpu.SemaphoreType.DMA((N_DEV,)),
                        pltpu.SemaphoreType.DMA((N_DEV,)), pltpu.SemaphoreType.DMA])
    me = 4 * lax.axis_index("x") + 2 * lax.axis_index("y") + lax.axis_index("c")
    return _pcall(
        body, name=name, grid_spec=grid_spec,
        out_shape=(jax.ShapeDtypeStruct((s, D_MODEL), F32), jax.ShapeDtypeStruct((1, D_MODEL), F32),
                   jax.ShapeDtypeStruct((N_DEV, D_MODEL, COLS_PER_DEV), BF16)),
        compiler_params=_params(("arbitrary", "arbitrary")),
    )(jnp.reshape(me, (1,)).astype(jnp.int32), h, du, du, w_all, x, g_pre, dy)


def _in_proj_bwd_w(h, du, name):
    s = h.shape[0]
    tk = min(512, s)
    n_k = s // tk

    def body(h_ref, du_ref, out_ref, acc):
        k = pl.program_id(1)

        @pl.when(k == 0)
        def _():
            acc[...] = jnp.zeros_like(acc)
        acc[...] += _dot_tn(h_ref[...], du_ref[...])

        @pl.when(k == n_k - 1)
        def _():
            out_ref[...] = acc[...].astype(BF16)

    return _pcall(
        body, name=name, grid=(N_DEV, n_k),
        in_specs=[pl.BlockSpec((tk, D_MODEL), lambda j, k: (k, 0)),
                  pl.BlockSpec((tk, COLS_PER_DEV), lambda j, k: (k, j))],
        out_specs=pl.BlockSpec((None, D_MODEL, COLS_PER_DEV), lambda j, k: (j, 0, 0)),
        out_shape=jax.ShapeDtypeStruct((N_DEV, D_MODEL, COLS_PER_DEV), BF16),
        scratch_shapes=[pltpu.VMEM((D_MODEL, COLS_PER_DEV), F32)],
        compiler_params=_params(("parallel", "arbitrary")),
    )(h, du)


def _adamw_math(g, w, m, v):
    m_new = ADAM_B1 * m + (1.0 - ADAM_B1) * g
    v_new = ADAM_B2 * v + (1.0 - ADAM_B2) * (g * g)
    m_hat = m_new / (1.0 - ADAM_B1 ** ADAM_STEP)
    v_hat = v_new / (1.0 - ADAM_B2 ** ADAM_STEP)
    delta = -ADAM_LR * (m_hat / (jnp.sqrt(v_hat) + ADAM_EPS) + ADAM_WD * w)
    return delta, m_new, v_new


def _sum_partials(p_ref):
    total = p_ref[0].astype(F32)
    for d in range(1, N_DEV):
        total = total + p_ref[d].astype(F32)
    return total


def _adamw_layers(parts0, parts1, w, m, v, name):
    _, r, c = w.shape
    tr = min(128, r)
    n_r = r // tr

    def body(p0_ref, p1_ref, w_ref, m_ref, v_ref, g_ref, d_ref, mo_ref, vo_ref):
        layer = pl.program_id(0)

        @pl.when(layer == 0)
        def _():
            g_ref[...] = _sum_partials(p0_ref)

        @pl.when(layer == 1)
        def _():
            g_ref[...] = _sum_partials(p1_ref)
        d_ref[...], mo_ref[...], vo_ref[...] = _adamw_math(g_ref[...], w_ref[...], m_ref[...], v_ref[...])

    part = lambda which: pl.BlockSpec((N_DEV, tr, c), lambda l, i: (0, jnp.where(l == which, i, 0), 0))
    par = lambda: pl.BlockSpec((None, tr, c), lambda l, i: (l, i, 0))
    out = jax.ShapeDtypeStruct(w.shape, F32)
    return _pcall(
        body, name=name, grid=(2, n_r),
        in_specs=[part(0), part(1), par(), par(), par()],
        out_specs=(par(), par(), par(), par()),
        out_shape=(out, out, out, out),
        compiler_params=_params(("arbitrary", "arbitrary")),
    )(parts0, parts1, w, m, v)


def _adamw_small(parts, w, m, v, name):
    def body(p_ref, w_ref, m_ref, v_ref, g_ref, d_ref, mo_ref, vo_ref):
        g = _sum_partials(p_ref)
        g_ref[...] = g
        d_ref[...], mo_ref[...], vo_ref[...] = _adamw_math(g, w_ref[...], m_ref[...], v_ref[...])

    out = jax.ShapeDtypeStruct(w.shape, F32)
    return _pcall(body, name=name, out_shape=(out, out, out, out), compiler_params=_params())(parts, w, m, v)


def _adamw_plain(g, w, m, v, name):
    def body(g_ref, w_ref, m_ref, v_ref, d_ref, mo_ref, vo_ref):
        d_ref[...], mo_ref[...], vo_ref[...] = _adamw_math(g_ref[...], w_ref[...], m_ref[...], v_ref[...])

    out = jax.ShapeDtypeStruct(w.shape, F32)
    return _pcall(body, name=name, out_shape=(out, out, out), compiler_params=_params())(g, w, m, v)


def _rows128(a):
    return a.reshape(-1, LANES)


SMALL_NAMES = ("pre_norm_g", "pool_w", "pool_scale", "conv_w", "conv_b", "post_norm_g")


def kernel(x, pre_norm_g, w_in, pool_w, pool_scale, conv_w, conv_b, w_branch, w_out, post_norm_g, loss_target, m_pre_norm_g, m_w_in, m_pool_w, m_pool_scale, m_conv_w, m_conv_b, m_w_branch, m_w_out, m_post_norm_g, v_pre_norm_g, v_w_in, v_pool_w, v_pool_scale, v_conv_w, v_conv_b, v_w_branch, v_w_out, v_post_norm_g):
    s = x.shape[1]
    me = 4 * lax.axis_index("x") + 2 * lax.axis_index("y") + lax.axis_index("c")
    x0 = x[0]
    target = loss_target[0]
    conv_cols = conv_w.shape[-1]

    conv_w_pad = jnp.pad(conv_w.reshape(2 * 3, conv_cols), ((0, 2), (0, LANES - conv_cols)))
    w_in_all = [None, None]
    w_in_all[0], cw_g = _gather_two_level([w_in[0].astype(BF16), conv_w_pad], "gather_w_in_0")
    conv_w_full = cw_g[:, :6, :conv_cols].reshape(N_DEV, 2, 3, conv_cols).transpose(1, 2, 0, 3).reshape(2, 3, WIDTH)
    later_weights = ([w_in[1].astype(BF16), w_branch.astype(BF16), w_out.astype(BF16)], True)

    saved = []
    xin = x0
    for l in range(2):
        u, h = _in_proj_fwd(xin, pre_norm_g[l:l + 1], w_in_all[l], f"in_proj_fwd_{l}")
        y_pool = _pool_fwd(u, pool_w[l], pool_scale[l:l + 1], f"pool_fwd_{l}")
        y_conv = _conv_fwd(u, conv_w_full[l], conv_b[l:l + 1], f"conv_fwd_{l}")
        if l == 0:
            o_sb, y_sb, w_in_all[1], wb_g, wo_all = _sb_fwd(u, f"sb_fwd_{l}", later_weights)
            wb_all = wb_g.transpose(1, 2, 3, 0, 4).reshape(2, 3, WIDTH, D_MODEL)
        else:
            o_sb, y_sb = _sb_fwd(u, f"sb_fwd_{l}")
        xout, merged, pre = _merge_out_fwd(y_pool, y_conv, y_sb, u, wb_all, wo_all, xin, post_norm_g[l:l + 1], l,
                                           f"merge_out_fwd_{l}")
        saved.append((xin, u, h, y_pool, y_conv, y_sb, o_sb, merged, pre))
        xin = xout

    dy, loss_row = _loss_and_grad(xin, target, "loss")

    small = [None, None]
    recv = [None, None]
    ready = []
    for l in (1, 0):
        xl, u, h, y_pool, y_conv, y_sb, o_sb, merged, pre = saved[l]
        dmerged, dwo, dg_post = _out_proj_bwd(dy, pre, post_norm_g[l:l + 1], merged, wo_all, l, f"out_proj_bwd_{l}")
        du_merge, dyp, dyc, dys, dwb = _merge_bwd(dmerged, y_pool, y_conv, y_sb, u, wb_all, l, f"merge_bwd_{l}")
        dwb = dwb.reshape(N_DEV, 3 * WIDTH, D_MODEL // N_DEV)
        dwo = dwo.reshape(N_DEV, D_MODEL // N_DEV, D_MODEL)
        dxc, dgb, dgc, dcg, dcw, dcb = _conv_bwd(u, conv_w_full[l], conv_b[l:l + 1], dyc, f"conv_bwd_{l}")
        dpv, dpg, dpw, dps = _pool_bwd(u, pool_w[l], pool_scale[l:l + 1], dyp, f"pool_bwd_{l}")
        small[l] = dict(pool_w=dpw, pool_scale=dps, conv_w=dcw, conv_b=dcb, post_norm_g=dg_post)
        if l == 1:
            dq, dk, dv, dsg = _sb_bwd(u, o_sb, dys, f"sb_bwd_{l}")
        else:
            small[l]["pre_norm_g"] = jnp.zeros((1, D_MODEL), F32)
            packed = jnp.concatenate(
                [_rows128(jnp.stack([small[0][n], small[1][n]])) for n in SMALL_NAMES]
                + [jnp.pad(loss_row, ((0, 7), (0, 0)))], axis=0)
            dq, dk, dv, dsg, *got, packed_all = _sb_bwd(
                u, o_sb, dys, f"sb_bwd_{l}", (ready + [dwb, dwo, packed], (False,) * 5 + (True,)))
            recv[1] = got[:3]
        du = jnp.concatenate([dpv, dpg, dxc, dgb, dgc, dcg, dq, dk.astype(BF16), dv.astype(BF16), dsg, du_merge],
                             axis=1)
        if l == 1:
            dwi = _in_proj_bwd_w(h, du, f"in_proj_bwd_w_{l}")
            ready = [dwi, dwb, dwo]
            dx, dg_pre = _in_proj_bwd_x(du, w_in_all[l], xl, pre_norm_g[l:l + 1], dy, f"in_proj_bwd_x_{l}")
            small[l]["pre_norm_g"] = dg_pre
        else:
            dx, dg_pre, got_dwi = _in_proj_bwd_send(h, du, w_in_all[l], xl, pre_norm_g[l:l + 1], dy,
                                                    f"in_proj_bwd_{l}")
            recv[0] = [got_dwi] + got[3:]
        dy = dx
    grad_x = dy[None]

    (g_pre_0_all,) = _exchange([_rows128(dg_pre)], True, "gather_g_pre_0")
    packed_all = lax.dynamic_update_slice(packed_all, g_pre_0_all, (0, 0, 0))
    sizes = dict(pre_norm_g=16, pool_w=1024, pool_scale=8, conv_w=24, conv_b=8, post_norm_g=16)
    n_rows = sum(sizes.values())
    loss = jnp.sum(packed_all[:, n_rows, 0])

    given = dict(pre_norm_g=(pre_norm_g, m_pre_norm_g, v_pre_norm_g), pool_w=(pool_w, m_pool_w, v_pool_w),
                 pool_scale=(pool_scale, m_pool_scale, v_pool_scale), conv_b=(conv_b, m_conv_b, v_conv_b),
                 post_norm_g=(post_norm_g, m_post_norm_g, v_post_norm_g))
    zeros_cw = jnp.zeros((sizes["conv_w"], LANES), F32)
    pack3 = [jnp.concatenate([zeros_cw if n == "conv_w" else _rows128(given[n][k]) for n in SMALL_NAMES], axis=0)
             for k in range(3)]
    sg, sd, sm, sv = _adamw_small(packed_all[:, :n_rows], pack3[0], pack3[1], pack3[2], "adamw_small")

    def unpack(buf, name, shape):
        start = 0
        for n in SMALL_NAMES:
            if n == name:
                return buf[start:start + sizes[n]].reshape(shape)
            start += sizes[n]

    out = {}
    for n in ("pre_norm_g", "pool_w", "pool_scale", "conv_b", "post_norm_g"):
        shape = given[n][0].shape
        out[n] = tuple(unpack(b, n, shape) for b in (sg, sd, sm, sv))
    g_cw = lax.dynamic_slice_in_dim(unpack(sg, "conv_w", (2, 3, WIDTH)), me * conv_cols, conv_cols, axis=2)
    cw2 = lambda a: a.reshape(6, conv_cols)
    d_cw, m_cw, v_cw = _adamw_plain(cw2(g_cw), cw2(conv_w), cw2(m_conv_w), cw2(v_conv_w), "adamw_conv_w")
    out["conv_w"] = (g_cw,) + tuple(a.reshape(2, 3, conv_cols) for a in (d_cw, m_cw, v_cw))

    out["w_in"] = _adamw_layers(recv[0][0], recv[1][0], w_in, m_w_in, v_w_in, "adamw_w_in")
    cols = D_MODEL // N_DEV
    wb3 = lambda a: a.reshape(2, 3 * WIDTH, cols)
    out["w_branch"] = tuple(a.reshape(2, 3, WIDTH, cols) for a in _adamw_layers(
        recv[0][1], recv[1][1], wb3(w_branch), wb3(m_w_branch), wb3(v_w_branch), "adamw_w_branch"))
    out["w_out"] = _adamw_layers(recv[0][2], recv[1][2], w_out, m_w_out, v_w_out, "adamw_w_out")

    order = ("pre_norm_g", "w_in", "pool_w", "pool_scale", "conv_w", "conv_b", "w_branch", "w_out", "post_norm_g")
    return (loss, grad_x) + tuple(out[n][k] for k in range(4) for n in order)
```

```python
import functools

import jax
import jax.numpy as jnp
from jax import lax
from jax.experimental import pallas as pl
from jax.experimental.pallas import tpu as pltpu

F32 = jnp.float32
BF16 = jnp.bfloat16

N_DEV = 8
D_MODEL = 1024
WIDTH = 512
N_IN = 8192
COLS_PER_DEV = N_IN // N_DEV
HEAD_DIM = 64
LANES = 128
SB_SCALE = HEAD_DIM ** -0.5
LOG2E = 1.4426950408889634
RMS_EPS = 1e-6
POOL_HALO = 16
CONV_HALO = 8
ADAM_LR, ADAM_B1, ADAM_B2, ADAM_EPS, ADAM_WD, ADAM_STEP = 0.001, 0.9, 0.999, 1e-08, 0.01, 10
VMEM_LIMIT = 60 * 1024 * 1024

CB_POOL_V, CB_POOL_G = 0, 4
CB_CONV_X, CB_CONV_GB, CB_CONV_GC, CB_CONV_G = 8, 12, 16, 20
CB_SB_Q, CB_SB_K, CB_SB_V, CB_SB_G = 24, 28, 32, 36
MERGE_BLOCK_1024 = 5


def _pcall(body, **kw):
    return pl.pallas_call(body, **kw)


def _params(sem=None):
    if sem is None:
        return pltpu.CompilerParams(vmem_limit_bytes=VMEM_LIMIT)
    return pltpu.CompilerParams(dimension_semantics=sem, vmem_limit_bytes=VMEM_LIMIT)


def _sigmoid(x):
    return 1.0 / (1.0 + jnp.exp(-x))


def _dot(a, b):
    return jnp.dot(a, b, preferred_element_type=F32)


def _dot_nt(a, b):
    return lax.dot_general(a, b, (((1,), (1,)), ((), ())), preferred_element_type=F32)


def _dot_tn(a, b):
    return lax.dot_general(a, b, (((0,), (0,)), ((), ())), preferred_element_type=F32)


def _split_bf16(x):
    hi = x.astype(BF16)
    lo = (x - hi.astype(F32)).astype(BF16)
    return hi, lo


N_PEER = N_DEV - 1
ANY_SPEC = pl.BlockSpec(memory_space=pl.ANY)


def _exchange_copies(ins, outs, send_sems, recv_sems, local_sems, gather, with_recvs=True):
    n = len(ins)
    gathers = _per_array(gather, n)
    x, y, c = lax.axis_index("x"), lax.axis_index("y"), lax.axis_index("c")
    me = 4 * x + 2 * y + c
    flip = lambda v, bit: 1 - v if bit else v
    local, sends, recvs = [], [], []
    for a in range(n):
        src = ins[a] if gathers[a] else ins[a].at[me]
        local.append(pltpu.make_async_copy(src, outs[a].at[me], local_sems.at[a]))
    for k in range(N_PEER):
        px, py, pc = flip(x, ((k + 1) >> 2) & 1), flip(y, ((k + 1) >> 1) & 1), flip(c, (k + 1) & 1)
        peer_id = 4 * px + 2 * py + pc
        for a in range(n):
            src = ins[a] if gathers[a] else ins[a].at[peer_id]
            common = dict(src_ref=src, send_sem=send_sems.at[a * N_PEER + k], recv_sem=recv_sems.at[a * N_PEER + k],
                          device_id=(px, py, pc), device_id_type=pl.DeviceIdType.MESH)
            sends.append(pltpu.make_async_remote_copy(dst_ref=outs[a].at[me], **common))
            if with_recvs:
                recvs.append(pltpu.make_async_remote_copy(dst_ref=outs[a].at[peer_id], **common))
    return local, sends, recvs


def _exchange_start(ins, outs, sems, gather):
    local, sends, _ = _exchange_copies(ins, outs, *sems, gather, with_recvs=False)
    for cp in local + sends:
        cp.start()


def _exchange_wait(ins, outs, sems, gather):
    local, sends, recvs = _exchange_copies(ins, outs, *sems, gather)
    for cp in recvs:
        cp.wait_recv()
    for cp in sends:
        cp.wait_send()
    for cp in local:
        cp.wait()


def _per_array(gather, n):
    return tuple(gather) if isinstance(gather, (tuple, list)) else (gather,) * n


def _exchange_out_shapes(arrs, gather):
    return [jax.ShapeDtypeStruct((N_DEV,) + tuple(a.shape if g else a.shape[1:]), a.dtype)
            for a, g in zip(arrs, _per_array(gather, len(arrs)))]


def _gather_two_level(arrs, name):
    n = len(arrs)

    def body(*refs):
        ins, outs = refs[:n], refs[n:2 * n]
        send_sems, recv_sems, local_sems = refs[2 * n:]
        x, y, c = lax.axis_index("x"), lax.axis_index("y"), lax.axis_index("c")
        me, sibling = (x, y, c), (x, y, 1 - c)
        chips = [(1 - x, y), (x, 1 - y), (1 - x, 1 - y)]
        slot = lambda dev: 4 * dev[0] + 2 * dev[1] + dev[2]

        def copy(a, k, block, to, src=None):
            return pltpu.make_async_remote_copy(
                src_ref=outs[a].at[slot(block)] if src is None else src, dst_ref=outs[a].at[slot(block)],
                send_sem=send_sems.at[a * N_PEER + k], recv_sem=recv_sems.at[a * N_PEER + k],
                device_id=to, device_id_type=pl.DeviceIdType.MESH)

        local = [pltpu.make_async_copy(ins[a], outs[a].at[slot(me)], local_sems.at[a]) for a in range(n)]
        first = []
        for a in range(n):
            first.append(copy(a, 0, me, sibling, src=ins[a]))
            first += [copy(a, 1 + j, me, (*chip, c), src=ins[a]) for j, chip in enumerate(chips)]
        for cp in local + first:
            cp.start()
        passed = []
        for j, chip in enumerate(chips):
            for a in range(n):
                copy(a, 1 + j, (*chip, c), me).wait_recv()
                passed.append(copy(a, 4 + j, (*chip, c), sibling))
                passed[-1].start()
        for a in range(n):
            copy(a, 0, sibling, me).wait_recv()
        for j, chip in enumerate(chips):
            for a in range(n):
                copy(a, 4 + j, (*chip, 1 - c), me).wait_recv()
        for cp in first + passed:
            cp.wait_send()
        for cp in local:
            cp.wait()

    return _pcall(
        body, name=name,
        out_shape=tuple(_exchange_out_shapes(arrs, True)),
        in_specs=[ANY_SPEC] * n, out_specs=tuple([ANY_SPEC] * n),
        scratch_shapes=_exchange_sems(n),
    )(*arrs)


def _exchange_sems(n):
    return [pltpu.SemaphoreType.DMA((n * N_PEER,)), pltpu.SemaphoreType.DMA((n * N_PEER,)),
            pltpu.SemaphoreType.DMA((n,))]


def _exchange(arrs, gather, name):
    n = len(arrs)

    def body(*refs):
        ins, outs, sems = refs[:n], refs[n:2 * n], refs[2 * n:]
        _exchange_start(ins, outs, sems, gather)
        _exchange_wait(ins, outs, sems, gather)

    return _pcall(
        body, name=name,
        out_shape=tuple(_exchange_out_shapes(arrs, gather)),
        in_specs=[ANY_SPEC] * n, out_specs=tuple([ANY_SPEC] * n),
        scratch_shapes=_exchange_sems(n),
    )(*arrs)


def _in_proj_fwd(x, g, w_all, name):
    s = x.shape[0]
    tm = min(1024, s)

    def body(x_ref, g_ref, w_ref, u_ref, h_ref, hs):
        @pl.when(pl.program_id(1) == 0)
        def _():
            xv = x_ref[...]
            r = lax.rsqrt(jnp.mean(xv * xv, axis=-1, keepdims=True) + RMS_EPS)
            hv = (xv * r * g_ref[...]).astype(BF16)
            hs[...] = hv
            h_ref[...] = hv
        u_ref[...] = _dot(hs[...], w_ref[...])

    return _pcall(
        body, name=name, grid=(s // tm, N_DEV),
        in_specs=[pl.BlockSpec((tm, D_MODEL), lambda i, j: (i, 0)),
                  pl.BlockSpec((1, D_MODEL), lambda i, j: (0, 0)),
                  pl.BlockSpec((None, D_MODEL, COLS_PER_DEV), lambda i, j: (j, 0, 0))],
        out_specs=(pl.BlockSpec((tm, COLS_PER_DEV), lambda i, j: (i, j)),
                   pl.BlockSpec((tm, D_MODEL), lambda i, j: (i, 0))),
        out_shape=(jax.ShapeDtypeStruct((s, N_IN), F32), jax.ShapeDtypeStruct((s, D_MODEL), BF16)),
        scratch_shapes=[pltpu.VMEM((tm, D_MODEL), BF16)],
        compiler_params=_params(("parallel", "arbitrary")),
    )(x, g, w_all)


def _pool_window(vs, t0, t, grp):
    ext = vs[pl.ds(t0, t + POOL_HALO), :]
    s2 = ext + pltpu.roll(ext, 1, 0)
    s4 = s2 + pltpu.roll(s2, 2, 0)
    s8 = s4 + pltpu.roll(s4, 4, 0)
    s16 = s8 + pltpu.roll(s8, 8, 0)
    sel = jnp.where(grp == 0, s2, jnp.where(grp == 1, s4, jnp.where(grp == 2, s8, s16)))
    return sel[POOL_HALO:, :], ext[POOL_HALO:, :]


def _pool_count(t0, t, grp):
    pos = t0 + lax.broadcasted_iota(jnp.int32, (t, 1), 0)
    return jnp.minimum(pos + 1, jnp.left_shift(2, grp)).astype(F32)


def _pool_fwd(u, pool_w, pool_scale, name):
    s = u.shape[0]
    t = min(256, s)

    def body(pv_ref, pg_ref, w_ref, sc_ref, y_ref, vs):
        grp = pl.program_id(0)
        vs[0:POOL_HALO, :] = jnp.zeros((POOL_HALO, LANES), F32)
        vs[POOL_HALO:, :] = pv_ref[...]
        wb = w_ref[...].astype(BF16)
        scale = sc_ref[...]

        def tile(i, carry):
            t0 = pl.multiple_of(i * t, t)
            win, v = _pool_window(vs, t0, t, grp)
            pooled = win / _pool_count(t0, t, grp) - v
            mixed = _dot(pooled.astype(BF16), wb)
            gate = pg_ref[pl.ds(t0, t), :]
            y_ref[pl.ds(t0, t), :] = (mixed * scale * (gate * _sigmoid(gate))).astype(BF16)
            return carry

        lax.fori_loop(0, s // t, tile, 0)

    return _pcall(
        body, name=name, grid=(4,),
        in_specs=[pl.BlockSpec((s, LANES), lambda g: (0, CB_POOL_V + g)),
                  pl.BlockSpec((s, LANES), lambda g: (0, CB_POOL_G + g)),
                  pl.BlockSpec((None, LANES, LANES), lambda g: (g, 0, 0)),
                  pl.BlockSpec((1, LANES), lambda g: (0, g))],
        out_specs=pl.BlockSpec((s, LANES), lambda g: (0, g)),
        out_shape=jax.ShapeDtypeStruct((s, WIDTH), BF16),
        scratch_shapes=[pltpu.VMEM((POOL_HALO + s, LANES), F32)],
        compiler_params=_params(("arbitrary",)),
    )(u, u, pool_w, pool_scale)


def _conv_taps(zs, t0, t):
    ext = zs[pl.ds(t0, t + CONV_HALO), :]
    z0 = ext[CONV_HALO:, :]
    z1 = pltpu.roll(ext, 1, 0)[CONV_HALO:, :]
    z2 = pltpu.roll(ext, 2, 0)[CONV_HALO:, :]
    return z0, z1, z2


def _conv_fwd(u, conv_w, conv_b, name):
    s = u.shape[0]
    t = min(256, s)

    def body(xc_ref, gb_ref, gc_ref, cg_ref, w_ref, b_ref, y_ref, zs):
        zs[0:CONV_HALO, :] = jnp.zeros((CONV_HALO, LANES), F32)
        zs[CONV_HALO:, :] = gc_ref[...] * xc_ref[...]
        w0, w1, w2 = w_ref[0:1, :], w_ref[1:2, :], w_ref[2:3, :]
        bias = b_ref[...]

        def tile(i, carry):
            t0 = pl.multiple_of(i * t, t)
            z0, z1, z2 = _conv_taps(zs, t0, t)
            conv = w0 * z2 + w1 * z1 + w2 * z0
            gate = cg_ref[pl.ds(t0, t), :]
            y = gb_ref[pl.ds(t0, t), :] * (conv + bias) * (gate * _sigmoid(gate))
            y_ref[pl.ds(t0, t), :] = y.astype(BF16)
            return carry

        lax.fori_loop(0, s // t, tile, 0)

    col = lambda base: pl.BlockSpec((s, LANES), lambda j: (0, base + j))
    return _pcall(
        body, name=name, grid=(4,),
        in_specs=[col(CB_CONV_X), col(CB_CONV_GB), col(CB_CONV_GC), col(CB_CONV_G),
                  pl.BlockSpec((3, LANES), lambda j: (0, j)),
                  pl.BlockSpec((1, LANES), lambda j: (0, j))],
        out_specs=pl.BlockSpec((s, LANES), lambda j: (0, j)),
        out_shape=jax.ShapeDtypeStruct((s, WIDTH), BF16),
        scratch_shapes=[pltpu.VMEM((CONV_HALO + s, LANES), F32)],
        compiler_params=_params(("arbitrary",)),
    )(u, u, u, u, conv_w, conv_b)


def _first_head_lanes(rows, width=LANES):
    lane = lax.broadcasted_iota(jnp.int32, (rows, width), 1)
    return jnp.bitwise_and(lane, LANES - 1) < HEAD_DIM


def _stack_heads(x, first):
    zero = jnp.zeros_like(x)
    return jnp.concatenate([jnp.where(first, x, zero), jnp.where(first, zero, x)], axis=0).astype(BF16)


def _causal_mask(tq, tk, copies):
    row = lax.broadcasted_iota(jnp.int32, (tq, tk), 0)
    col = lax.broadcasted_iota(jnp.int32, (tq, tk), 1)
    return jnp.concatenate([col < row] * copies, axis=0)


def _suffix_matrix(tk, inclusive, parts):
    r = lax.broadcasted_iota(jnp.int32, (parts * tk, 2 * tk), 0)
    c = lax.broadcasted_iota(jnp.int32, (parts * tk, 2 * tk), 1)
    r = jnp.bitwise_and(r, tk - 1)
    tri = (r >= c) if inclusive else (r > c)
    return jnp.where(c >= tk, 1.0, jnp.where(tri, 1.0, 0.0)).astype(BF16)


def _suffix_sums(x, m):
    hi, lo = _split_bf16(x)
    return _dot(jnp.concatenate([hi, lo], axis=1), m)


def _sb_log_terms(z, mask, m_strict):
    ls = jnp.minimum(z, 0.0) - jnp.log(1.0 + jnp.exp2(jnp.abs(z) * -LOG2E))
    lk = ls - z
    if mask is not None:
        lk = jnp.where(mask, lk, 0.0)
    return ls, _dot(lk.astype(BF16), m_strict)


SB_PAIRS = 4


def _pair_lanes(a):
    return slice(a * LANES, (a + 1) * LANES)


def _sb_fwd(u, name, xchg=None):
    s = u.shape[0]
    tq = tk = min(128, s)
    pairs = SB_PAIRS
    width = pairs * LANES
    rows = 2 * pairs * tq
    x_arrs, x_gather = xchg if xchg else ((), True)
    n_x = len(x_arrs)
    grid = (4 // pairs, s // tq)

    def body(*refs):
        q_ref, k_ref, v_ref, g_ref = refs[:4]
        x_in, refs = refs[4:4 + n_x], refs[4 + n_x:]
        o_ref, y_ref = refs[:2]
        x_out, refs = refs[2:2 + n_x], refs[2 + n_x:]
        kbf, vst, z_s, ell_s, carry_s = refs[:5]
        x_sems = refs[5:]
        i = pl.program_id(1)
        if n_x:
            @pl.when((pl.program_id(0) == 0) & (i == 0))
            def _():
                _exchange_start(x_in, x_out, x_sems, x_gather)

        @pl.when(i == 0)
        def _():
            kbf[...] = k_ref[...].astype(BF16)
            first_s = _first_head_lanes(s, width)
            vf = v_ref[...]
            vst[0] = jnp.where(first_s, vf, 0.0).astype(BF16)
            vst[1] = jnp.where(first_s, 0.0, vf).astype(BF16)

        first = _first_head_lanes(tq)
        mask = _causal_mask(tq, tk, 2 * pairs)
        m_strict = _suffix_matrix(tk, False, 1)
        qcat = jnp.concatenate([_stack_heads(q_ref[:, _pair_lanes(a)] * SB_SCALE, first) for a in range(pairs)],
                               axis=0)

        def scores(b):
            off = pl.multiple_of(jnp.maximum(b, 0) * tk, tk)
            z_s[...] = jnp.concatenate(
                [_dot_nt(qcat[a * 2 * tq:(a + 1) * 2 * tq], kbf[pl.ds(off, tk), _pair_lanes(a)])
                 for a in range(pairs)], axis=0)

        def log_weights(m):
            ls, cs = _sb_log_terms(z_s[...], m, m_strict)
            carry = carry_s[...]
            ell_s[...] = ls + cs[:, :tk] + carry
            carry_s[...] = carry + cs[:, tk:]

        def consume(b, accs, m):
            w = jnp.exp(ell_s[...])
            if m is not None:
                w = jnp.where(m, w, 0.0)
            wb = w.astype(BF16)
            off = pl.multiple_of(b * tk, tk)
            new = []
            for a in range(pairs):
                r0 = a * 2 * tq
                wcat = jnp.concatenate([wb[r0:r0 + tq], wb[r0 + tq:r0 + 2 * tq]], axis=1)
                vcat = jnp.concatenate([vst[0, pl.ds(off, tk), _pair_lanes(a)], vst[1, pl.ds(off, tk), _pair_lanes(a)]],
                                       axis=0)
                new.append(accs[a] + _dot(wcat, vcat))
            return tuple(new)

        carry_s[...] = jnp.zeros((rows, tk), F32)
        scores(i)
        log_weights(mask)
        pl.when(i >= 1)(lambda: scores(i - 1))
        accs = consume(i, tuple(jnp.zeros((tq, LANES), F32) for _ in range(pairs)), mask)
        pl.when(i >= 1)(lambda: log_weights(None))
        pl.when(i >= 2)(lambda: scores(i - 2))

        def step(n, accs):
            accs = consume(i - n, accs, None)
            log_weights(None)
            scores(i - n - 2)
            return accs

        def next_to_last(accs):
            accs = consume(1, accs, None)
            log_weights(None)
            return accs

        accs = lax.fori_loop(1, i - 1, step, accs)
        accs = lax.cond(i >= 2, next_to_last, lambda a: a, accs)
        accs = lax.cond(i >= 1, lambda a: consume(0, a, None), lambda a: a, accs)
        o = jnp.concatenate(accs, axis=1)
        o_ref[...] = o
        gate = g_ref[...]
        y_ref[...] = (o * (gate * _sigmoid(gate))).astype(BF16)
        if n_x:
            @pl.when((pl.program_id(0) == grid[0] - 1) & (i == grid[1] - 1))
            def _():
                _exchange_wait(x_in, x_out, x_sems, x_gather)

    base = lambda cb: cb // pairs
    qblk = lambda cb: pl.BlockSpec((tq, width), lambda p, i: (i, base(cb) + p))
    full = lambda cb: pl.BlockSpec((s, width), lambda p, i: (0, base(cb) + p), pipeline_mode=pl.Buffered(1))
    state = pltpu.VMEM((rows, tk), F32)
    return _pcall(
        body, name=name, grid=grid,
        in_specs=[qblk(CB_SB_Q), full(CB_SB_K), full(CB_SB_V), qblk(CB_SB_G)] + [ANY_SPEC] * n_x,
        out_specs=(qblk(0), qblk(0)) + (ANY_SPEC,) * n_x,
        out_shape=(jax.ShapeDtypeStruct((s, WIDTH), F32), jax.ShapeDtypeStruct((s, WIDTH), BF16))
        + tuple(_exchange_out_shapes(x_arrs, x_gather)),
        scratch_shapes=[pltpu.VMEM((s, width), BF16), pltpu.VMEM((2, s, width), BF16), state, state, state]
        + (_exchange_sems(n_x) if n_x else []),
        compiler_params=_params(("arbitrary", "arbitrary")),
    )(u, u, u, u, *x_arrs)


def _merge_out_fwd(y_pool, y_conv, y_sb, u, wb_all, wo_all, x, g_post, layer, name):
    s = x.shape[0]
    tm = min(512, s)

    def body(yp, yc, ys, m0, m1, m2, wb_ref, wo_ref, x_ref, g_ref, out_ref, merged_ref, pre_ref):
        merged = jnp.zeros((tm, D_MODEL), F32)
        for n, (y_ref, m_ref) in enumerate(((yp, m0), (yc, m1), (ys, m2))):
            merged = merged + _sigmoid(m_ref[...]) * _dot(y_ref[...], wb_ref[n])
        mb = merged.astype(BF16)
        merged_ref[...] = mb
        pre = _dot(mb, wo_ref[...].reshape(D_MODEL, D_MODEL))
        pre_ref[...] = pre
        r = lax.rsqrt(jnp.mean(pre * pre, axis=-1, keepdims=True) + RMS_EPS)
        out_ref[...] = x_ref[...] + pre * r * g_ref[...]

    rows = lambda w: pl.BlockSpec((tm, w), lambda i: (i, 0))
    merge = lambda n: pl.BlockSpec((tm, D_MODEL), lambda i: (i, MERGE_BLOCK_1024 + n))
    return _pcall(
        body, name=name, grid=(s // tm,),
        in_specs=[rows(WIDTH), rows(WIDTH), rows(WIDTH), merge(0), merge(1), merge(2),
                  pl.BlockSpec((None, 3, WIDTH, D_MODEL), lambda i: (layer, 0, 0, 0)),
                  pl.BlockSpec((N_DEV, None, D_MODEL // N_DEV, D_MODEL), lambda i: (0, layer, 0, 0)),
                  rows(D_MODEL), pl.BlockSpec((1, D_MODEL), lambda i: (0, 0))],
        out_specs=(rows(D_MODEL), rows(D_MODEL), rows(D_MODEL)),
        out_shape=(jax.ShapeDtypeStruct((s, D_MODEL), F32), jax.ShapeDtypeStruct((s, D_MODEL), BF16),
                   jax.ShapeDtypeStruct((s, D_MODEL), F32)),
        compiler_params=_params(("arbitrary",)),
    )(y_pool, y_conv, y_sb, u, u, u, wb_all, wo_all, x, g_post)


def _loss_and_grad(y, target, name):
    s = y.shape[0]
    tm = min(512, s)

    def body(y_ref, t_ref, dy_ref, loss_ref, acc):
        i = pl.program_id(0)

        @pl.when(i == 0)
        def _():
            acc[...] = jnp.zeros_like(acc)
        err = y_ref[...] - t_ref[...]
        dy_ref[...] = err / D_MODEL
        acc[...] += jnp.sum(err * err, axis=0, keepdims=True)

        @pl.when(i == pl.num_programs(0) - 1)
        def _():
            total = jnp.sum(acc[...], axis=1, keepdims=True) * (0.5 / D_MODEL)
            loss_ref[...] = jnp.broadcast_to(total, (1, LANES))

    return _pcall(
        body, name=name, grid=(s // tm,),
        in_specs=[pl.BlockSpec((tm, D_MODEL), lambda i: (i, 0)), pl.BlockSpec((tm, D_MODEL), lambda i: (i, 0))],
        out_specs=(pl.BlockSpec((tm, D_MODEL), lambda i: (i, 0)), pl.BlockSpec((1, LANES), lambda i: (0, 0))),
        out_shape=(jax.ShapeDtypeStruct((s, D_MODEL), F32), jax.ShapeDtypeStruct((1, LANES), F32)),
        scratch_shapes=[pltpu.VMEM((1, D_MODEL), F32)],
        compiler_params=_params(("arbitrary",)),
    )(y, target)


def _out_proj_bwd(dy, pre, g_post, merged, wo_all, layer, name):
    s = dy.shape[0]
    tm = min(512, s)
    n_tiles = s // tm

    def body(dy_ref, pre_ref, g_ref, mg_ref, wo_ref, dm_ref, dwo_ref, dg_ref, acc):
        i = pl.program_id(0)

        @pl.when(i == 0)
        def _():
            acc[...] = jnp.zeros_like(acc)
            dg_ref[...] = jnp.zeros_like(dg_ref)
        dyv, pre_v = dy_ref[...], pre_ref[...]
        r = lax.rsqrt(jnp.mean(pre_v * pre_v, axis=-1, keepdims=True) + RMS_EPS)
        dg_ref[...] += jnp.sum(dyv * pre_v * r, axis=0, keepdims=True)
        a = dyv * g_ref[...]
        dpre = r * a - pre_v * (r * r * r) * jnp.mean(a * pre_v, axis=-1, keepdims=True)
        db = dpre.astype(BF16)
        acc[...] += _dot_tn(mg_ref[...], db)
        dm_ref[...] = _dot_nt(db, wo_ref[...].reshape(D_MODEL, D_MODEL))

        @pl.when(i == n_tiles - 1)
        def _():
            dwo_ref[...] = acc[...].astype(BF16)

    rows = lambda: pl.BlockSpec((tm, D_MODEL), lambda i: (i, 0))
    return _pcall(
        body, name=name, grid=(n_tiles,),
        in_specs=[rows(), rows(), pl.BlockSpec((1, D_MODEL), lambda i: (0, 0)), rows(),
                  pl.BlockSpec((N_DEV, None, D_MODEL // N_DEV, D_MODEL), lambda i: (0, layer, 0, 0))],
        out_specs=(rows(), pl.BlockSpec((D_MODEL, D_MODEL), lambda i: (0, 0)),
                   pl.BlockSpec((1, D_MODEL), lambda i: (0, 0))),
        out_shape=(jax.ShapeDtypeStruct((s, D_MODEL), F32), jax.ShapeDtypeStruct((D_MODEL, D_MODEL), BF16),
                   jax.ShapeDtypeStruct((1, D_MODEL), F32)),
        scratch_shapes=[pltpu.VMEM((D_MODEL, D_MODEL), F32)],
        compiler_params=_params(("arbitrary",)),
    )(dy, pre, g_post, merged, wo_all)


def _merge_bwd(dmerged, y_pool, y_conv, y_sb, u, wb_all, layer, name):
    s = dmerged.shape[0]
    tm = min(512, s)
    n_tiles = s // tm
    cols = D_MODEL // N_DEV

    def body(dm_ref, yp, yc, ys, m0, m1, m2, wb_ref, dum_ref, dyp, dyc, dys, dwb_ref, acc):
        i = pl.program_id(0)

        @pl.when(i == 0)
        def _():
            acc[...] = jnp.zeros_like(acc)
        dm = dm_ref[...]
        for n, (y_ref, m_ref, dy_ref) in enumerate(((yp, m0, dyp), (yc, m1, dyc), (ys, m2, dys))):
            yv = y_ref[...]
            wb = wb_ref[n]
            gate = _sigmoid(m_ref[...])
            proj = _dot(yv, wb)
            dum_ref[:, n * D_MODEL:(n + 1) * D_MODEL] = (dm * proj * gate * (1.0 - gate)).astype(BF16)
            dproj = (dm * gate).astype(BF16)
            acc[n] += _dot_tn(yv, dproj)
            dy_ref[...] = _dot_nt(dproj, wb)

        @pl.when(i == n_tiles - 1)
        def _():
            for j in range(N_DEV):
                for n in range(3):
                    dwb_ref[j, n] = acc[n, :, j * cols:(j + 1) * cols].astype(BF16)

    rows = lambda w: pl.BlockSpec((tm, w), lambda i: (i, 0))
    merge = lambda n: pl.BlockSpec((tm, D_MODEL), lambda i: (i, MERGE_BLOCK_1024 + n))
    return _pcall(
        body, name=name, grid=(n_tiles,),
        in_specs=[rows(D_MODEL), rows(WIDTH), rows(WIDTH), rows(WIDTH), merge(0), merge(1), merge(2),
                  pl.BlockSpec((None, 3, WIDTH, D_MODEL), lambda i: (layer, 0, 0, 0))],
        out_specs=(rows(3 * D_MODEL), rows(WIDTH), rows(WIDTH), rows(WIDTH),
                   pl.BlockSpec((N_DEV, 3, WIDTH, cols), lambda i: (0, 0, 0, 0))),
        out_shape=(jax.ShapeDtypeStruct((s, 3 * D_MODEL), BF16),
                   jax.ShapeDtypeStruct((s, WIDTH), F32), jax.ShapeDtypeStruct((s, WIDTH), F32),
                   jax.ShapeDtypeStruct((s, WIDTH), F32),
                   jax.ShapeDtypeStruct((N_DEV, 3, WIDTH, cols), BF16)),
        scratch_shapes=[pltpu.VMEM((3, WIDTH, D_MODEL), F32)],
        compiler_params=_params(("arbitrary",)),
    )(dmerged, y_pool, y_conv, y_sb, u, u, u, wb_all)


def _sb_bwd(u, o, dys, name, xchg=None):
    s = u.shape[0]
    tq = tk = min(128, s)
    pairs = SB_PAIRS
    width = pairs * LANES
    rows = 2 * pairs * tq
    pair_rows = lambda a: slice(a * 2 * tq, (a + 1) * 2 * tq)

    x_arrs, x_gather = xchg if xchg else ((), True)
    n_x = len(x_arrs)
    grid = (4 // pairs, s // tq)

    def body(*refs):
        q_ref, k_ref, v_ref, g_ref, o_ref, dys_ref = refs[:6]
        x_in, refs = refs[6:6 + n_x], refs[6 + n_x:]
        dq_ref, dk_ref, dv_ref, dg_ref = refs[:4]
        x_out, refs = refs[4:4 + n_x], refs[4 + n_x:]
        kbf, vbf, kst, z_s, ell_s, ls_s, cl_s, wb_s, g_s, bef_s, cg_s, beta_s = refs[:12]
        x_sems = refs[12:]
        i = pl.program_id(1)
        if n_x:
            @pl.when((pl.program_id(0) == 0) & (i == 0))
            def _():
                _exchange_start(x_in, x_out, x_sems, x_gather)

        @pl.when(i == 0)
        def _():
            dk_ref[...] = jnp.zeros_like(dk_ref)
            dv_ref[...] = jnp.zeros_like(dv_ref)
            kf = k_ref[...]
            kbf[...] = kf.astype(BF16)
            vbf[...] = v_ref[...].astype(BF16)
            first_s = _first_head_lanes(s, width)
            kst[0] = jnp.where(first_s, kf, 0.0).astype(BF16)
            kst[1] = jnp.where(first_s, 0.0, kf).astype(BF16)

        first = _first_head_lanes(tq)
        mask = _causal_mask(tq, tk, 2 * pairs)
        m_strict = _suffix_matrix(tk, False, 1)
        m_incl = _suffix_matrix(tk, True, 2)

        gate = g_ref[...]
        sg = _sigmoid(gate)
        dy = dys_ref[...]
        ov = o_ref[...]
        dg_ref[...] = (dy * ov * (sg * (1.0 + gate * (1.0 - sg)))).astype(BF16)
        do = (dy * (gate * sg)).astype(BF16)
        prod = do.astype(F32) * ov
        row_sum = lambda v: jnp.broadcast_to(jnp.sum(v, axis=1, keepdims=True), (tq, tk))
        dsum, docat, qcat = [], [], []
        for a in range(pairs):
            pa = prod[:, _pair_lanes(a)]
            dsum += [row_sum(jnp.where(first, pa, 0.0)), row_sum(jnp.where(first, 0.0, pa))]
            docat.append(_stack_heads(do[:, _pair_lanes(a)], first))
            qcat.append(_stack_heads(q_ref[:, _pair_lanes(a)] * SB_SCALE, first))
        dsum = jnp.concatenate(dsum, axis=0)

        def block_start(b):
            return pl.multiple_of(jnp.maximum(b, 0) * tk, tk)

        def scores(b):
            off = block_start(b)
            z_s[...] = jnp.concatenate([_dot_nt(qcat[a], kbf[pl.ds(off, tk), _pair_lanes(a)]) for a in range(pairs)],
                                       axis=0)

        def log_weights(m):
            ls, cs = _sb_log_terms(z_s[...], m, m_strict)
            cl = cl_s[...]
            ell_s[...] = ls + cs[:, :tk] + cl
            cl_s[...] = cl + cs[:, tk:]
            ls_s[...] = ls

        def weights(b, m):
            off = block_start(b)
            dwt = jnp.concatenate([_dot_nt(docat[a], vbf[pl.ds(off, tk), _pair_lanes(a)]) for a in range(pairs)],
                                  axis=0)
            w = jnp.exp(ell_s[...])
            if m is not None:
                w = jnp.where(m, w, 0.0)
            wb = w.astype(BF16)
            g = dwt * wb.astype(F32)
            gs = _suffix_sums(g, m_incl)
            cg = cg_s[...]
            beta = jnp.exp(ls_s[...])
            wb_s[...] = wb
            beta_s[...] = beta
            g_s[...] = g * (1.0 - beta)
            bef_s[...] = gs[:, :tk] + cg
            cg_s[...] = cg + gs[:, tk:]

        def grads(b, dqs, m):
            dz = g_s[...] - beta_s[...] * (dsum - bef_s[...])
            if m is not None:
                dz = jnp.where(m, dz, 0.0)
            dzb = dz.astype(BF16)
            wb = wb_s[...]
            off = pl.multiple_of(b * tk, tk)
            new = []
            for a in range(pairs):
                r0 = a * 2 * tq
                kcat = jnp.concatenate([kst[0, pl.ds(off, tk), _pair_lanes(a)], kst[1, pl.ds(off, tk), _pair_lanes(a)]],
                                       axis=0)
                new.append(dqs[a] + _dot(jnp.concatenate([dzb[r0:r0 + tq], dzb[r0 + tq:r0 + 2 * tq]], axis=1), kcat))
                dk_ref[pl.ds(off, tk), _pair_lanes(a)] += _dot_tn(dzb[pair_rows(a)], qcat[a])
                dv_ref[pl.ds(off, tk), _pair_lanes(a)] += _dot_tn(wb[pair_rows(a)], docat[a])
            return tuple(new)

        zero = jnp.zeros((rows, tk), F32)
        cl_s[...] = zero
        cg_s[...] = zero
        scores(i)
        log_weights(mask)
        pl.when(i >= 1)(lambda: scores(i - 1))
        weights(i, mask)
        pl.when(i >= 1)(lambda: log_weights(None))
        pl.when(i >= 2)(lambda: scores(i - 2))
        dqs = grads(i, tuple(jnp.zeros((tq, LANES), F32) for _ in range(pairs)), mask)
        pl.when(i >= 1)(lambda: weights(i - 1, None))
        pl.when(i >= 2)(lambda: log_weights(None))
        pl.when(i >= 3)(lambda: scores(i - 3))

        def step(n, dqs):
            dqs = grads(i - n, dqs, None)
            weights(i - n - 1, None)
            log_weights(None)
            scores(i - n - 3)
            return dqs

        def third_to_last(dqs):
            dqs = grads(2, dqs, None)
            weights(1, None)
            log_weights(None)
            return dqs

        def next_to_last(dqs):
            dqs = grads(1, dqs, None)
            weights(0, None)
            return dqs

        dqs = lax.fori_loop(1, i - 2, step, dqs)
        dqs = lax.cond(i >= 3, third_to_last, lambda d: d, dqs)
        dqs = lax.cond(i >= 2, next_to_last, lambda d: d, dqs)
        dqs = lax.cond(i >= 1, lambda d: grads(0, d, None), lambda d: d, dqs)
        dq_ref[...] = (jnp.concatenate(dqs, axis=1) * SB_SCALE).astype(BF16)
        if n_x:
            @pl.when((pl.program_id(0) == grid[0] - 1) & (i == grid[1] - 1))
            def _():
                _exchange_wait(x_in, x_out, x_sems, x_gather)

    base = lambda cb: cb // pairs
    qblk = lambda cb: pl.BlockSpec((tq, width), lambda p, i: (i, base(cb) + p))
    full = lambda cb: pl.BlockSpec((s, width), lambda p, i: (0, base(cb) + p), pipeline_mode=pl.Buffered(1))
    state = pltpu.VMEM((rows, tk), F32)
    return _pcall(
        body, name=name, grid=grid,
        in_specs=[qblk(CB_SB_Q), full(CB_SB_K), full(CB_SB_V), qblk(CB_SB_G), qblk(0), qblk(0)] + [ANY_SPEC] * n_x,
        out_specs=(qblk(0), full(0), full(0), qblk(0)) + (ANY_SPEC,) * n_x,
        out_shape=(jax.ShapeDtypeStruct((s, WIDTH), BF16), jax.ShapeDtypeStruct((s, WIDTH), F32),
                   jax.ShapeDtypeStruct((s, WIDTH), F32), jax.ShapeDtypeStruct((s, WIDTH), BF16))
        + tuple(_exchange_out_shapes(x_arrs, x_gather)),
        scratch_shapes=[pltpu.VMEM((s, width), BF16), pltpu.VMEM((s, width), BF16), pltpu.VMEM((2, s, width), BF16),
                        state, state, state, state, pltpu.VMEM((rows, tk), BF16),
                        state, state, state, state] + (_exchange_sems(n_x) if n_x else []),
        compiler_params=_params(("arbitrary", "arbitrary")),
    )(u, u, u, u, o, dys, *x_arrs)


def _conv_bwd(u, conv_w, conv_b, dyc, name):
    s = u.shape[0]
    t = min(256, s)
    n_tiles = s // t

    def body(xc_ref, gb_ref, gc_ref, cg_ref, w_ref, b_ref, dy_ref,
             dxc_ref, dgb_ref, dgc_ref, dcg_ref, dw_ref, db_ref, zs, ds):
        zs[0:CONV_HALO, :] = jnp.zeros((CONV_HALO, LANES), F32)
        zs[CONV_HALO:, :] = gc_ref[...] * xc_ref[...]
        ds[s:, :] = jnp.zeros((CONV_HALO, LANES), F32)
        w0, w1, w2 = w_ref[0:1, :], w_ref[1:2, :], w_ref[2:3, :]
        bias = b_ref[...]

        def first(i, sums):
            t0 = pl.multiple_of(i * t, t)
            z0, z1, z2 = _conv_taps(zs, t0, t)
            pre = w0 * z2 + w1 * z1 + w2 * z0 + bias
            gate = cg_ref[pl.ds(t0, t), :]
            sg = _sigmoid(gate)
            gb = gb_ref[pl.ds(t0, t), :]
            dy = dy_ref[pl.ds(t0, t), :]
            dcg_ref[pl.ds(t0, t), :] = (dy * gb * pre * (sg * (1.0 + gate * (1.0 - sg)))).astype(BF16)
            dgb_ref[pl.ds(t0, t), :] = (dy * pre * (gate * sg)).astype(BF16)
            dc = dy * gb * (gate * sg)
            ds[pl.ds(t0, t), :] = dc
            red = lambda v: jnp.sum(v, axis=0, keepdims=True)
            return (sums[0] + red(dc * z2), sums[1] + red(dc * z1), sums[2] + red(dc * z0), sums[3] + red(dc))

        zrow = jnp.zeros((1, LANES), F32)
        sw0, sw1, sw2, sb = lax.fori_loop(0, n_tiles, first, (zrow, zrow, zrow, zrow))
        dw_ref[0:1, :] = sw0
        dw_ref[1:2, :] = sw1
        dw_ref[2:3, :] = sw2
        db_ref[...] = sb

        def second(i, carry):
            t0 = pl.multiple_of(i * t, t)
            ext = ds[pl.ds(t0, t + CONV_HALO), :]
            n = t + CONV_HALO
            d0 = ext[:t, :]
            d1 = pltpu.roll(ext, n - 1, 0)[:t, :]
            d2 = pltpu.roll(ext, n - 2, 0)[:t, :]
            dz = w2 * d0 + w1 * d1 + w0 * d2
            dgc_ref[pl.ds(t0, t), :] = (dz * xc_ref[pl.ds(t0, t), :]).astype(BF16)
            dxc_ref[pl.ds(t0, t), :] = (dz * gc_ref[pl.ds(t0, t), :]).astype(BF16)
            return carry

        lax.fori_loop(0, n_tiles, second, 0)

    col = lambda base: pl.BlockSpec((s, LANES), lambda j: (0, base + j))
    dcol = jax.ShapeDtypeStruct((s, WIDTH), BF16)
    return _pcall(
        body, name=name, grid=(4,),
        in_specs=[col(CB_CONV_X), col(CB_CONV_GB), col(CB_CONV_GC), col(CB_CONV_G),
                  pl.BlockSpec((3, LANES), lambda j: (0, j)), pl.BlockSpec((1, LANES), lambda j: (0, j)), col(0)],
        out_specs=(col(0), col(0), col(0), col(0),
                   pl.BlockSpec((3, LANES), lambda j: (0, j)), pl.BlockSpec((1, LANES), lambda j: (0, j))),
        out_shape=(dcol, dcol, dcol, dcol,
                   jax.ShapeDtypeStruct((3, WIDTH), F32), jax.ShapeDtypeStruct((1, WIDTH), F32)),
        scratch_shapes=[pltpu.VMEM((CONV_HALO + s, LANES), F32), pltpu.VMEM((s + CONV_HALO, LANES), F32)],
        compiler_params=_params(("arbitrary",)),
    )(u, u, u, u, conv_w, conv_b, dyc)


def _pool_bwd(u, pool_w, pool_scale, dyp, name):
    s = u.shape[0]
    t = min(256, s)
    n_tiles = s // t

    def body(pv_ref, pg_ref, w_ref, sc_ref, dy_ref, dpv_ref, dpg_ref, dw_ref, dsc_ref, vs, es, dps):
        grp = pl.program_id(0)
        vs[0:POOL_HALO, :] = jnp.zeros((POOL_HALO, LANES), F32)
        vs[POOL_HALO:, :] = pv_ref[...]
        es[s:, :] = jnp.zeros((POOL_HALO, LANES), F32)
        wb = w_ref[...].astype(BF16)
        scale = sc_ref[...]

        def first(i, sums):
            dw, dsc = sums
            t0 = pl.multiple_of(i * t, t)
            win, v = _pool_window(vs, t0, t, grp)
            cnt = _pool_count(t0, t, grp)
            pb = (win / cnt - v).astype(BF16)
            mixed = _dot(pb, wb)
            gate = pg_ref[pl.ds(t0, t), :]
            sg = _sigmoid(gate)
            dy = dy_ref[pl.ds(t0, t), :]
            dpg_ref[pl.ds(t0, t), :] = (dy * (mixed * scale) * (sg * (1.0 + gate * (1.0 - sg)))).astype(BF16)
            dms = dy * (gate * sg)
            dsc = dsc + jnp.sum(dms * mixed, axis=0, keepdims=True)
            dmb = (dms * scale).astype(BF16)
            dw = dw + _dot_tn(pb, dmb)
            dpooled = _dot_nt(dmb, wb)
            dps[pl.ds(t0, t), :] = dpooled
            es[pl.ds(t0, t), :] = dpooled / cnt
            return dw, dsc

        dw, dsc = lax.fori_loop(0, n_tiles, first, (jnp.zeros((LANES, LANES), F32), jnp.zeros((1, LANES), F32)))
        dw_ref[...] = dw
        dsc_ref[...] = dsc

        def second(i, carry):
            t0 = pl.multiple_of(i * t, t)
            ext = es[pl.ds(t0, t + POOL_HALO), :]
            n = t + POOL_HALO
            f2 = ext + pltpu.roll(ext, n - 1, 0)
            f4 = f2 + pltpu.roll(f2, n - 2, 0)
            f8 = f4 + pltpu.roll(f4, n - 4, 0)
            f16 = f8 + pltpu.roll(f8, n - 8, 0)
            sel = jnp.where(grp == 0, f2, jnp.where(grp == 1, f4, jnp.where(grp == 2, f8, f16)))
            dpv_ref[pl.ds(t0, t), :] = (sel[:t, :] - dps[pl.ds(t0, t), :]).astype(BF16)
            return carry

        lax.fori_loop(0, n_tiles, second, 0)

    col = lambda base: pl.BlockSpec((s, LANES), lambda g: (0, base + g))
    dcol = jax.ShapeDtypeStruct((s, WIDTH), BF16)
    return _pcall(
        body, name=name, grid=(4,),
        in_specs=[col(CB_POOL_V), col(CB_POOL_G), pl.BlockSpec((None, LANES, LANES), lambda g: (g, 0, 0)),
                  pl.BlockSpec((1, LANES), lambda g: (0, g)), col(0)],
        out_specs=(col(0), col(0), pl.BlockSpec((None, LANES, LANES), lambda g: (g, 0, 0)),
                   pl.BlockSpec((1, LANES), lambda g: (0, g))),
        out_shape=(dcol, dcol, jax.ShapeDtypeStruct((4, LANES, LANES), F32), jax.ShapeDtypeStruct((1, WIDTH), F32)),
        scratch_shapes=[pltpu.VMEM((POOL_HALO + s, LANES), F32), pltpu.VMEM((s + POOL_HALO, LANES), F32),
                        pltpu.VMEM((s, LANES), F32)],
        compiler_params=_params(("arbitrary",)),
    )(u, u, pool_w, pool_scale, dyp)


def _in_proj_bwd_x(du, w_all, x, g_pre, dy, name, xchg=None):
    s = x.shape[0]
    tm = min(1024, s)
    x_arrs, x_gather = xchg if xchg else ((), True)
    n_x = len(x_arrs)
    grid = (s // tm, N_DEV)

    def body(*refs):
        du_ref, w_ref, x_ref, g_ref, dy_ref = refs[:5]
        x_in, refs = refs[5:5 + n_x], refs[5 + n_x:]
        dx_ref, dg_ref = refs[:2]
        x_out, refs = refs[2:2 + n_x], refs[2 + n_x:]
        acc, x_sems = refs[0], refs[1:]
        i, k = pl.program_id(0), pl.program_id(1)
        if n_x:
            @pl.when((i == 0) & (k == 0))
            def _():
                _exchange_start(x_in, x_out, x_sems, x_gather)

        @pl.when(k == 0)
        def _():
            acc[...] = jnp.zeros_like(acc)

        @pl.when((k == 0) & (i == 0))
        def _():
            dg_ref[...] = jnp.zeros_like(dg_ref)
        acc[...] += _dot_nt(du_ref[...], w_ref[...])

        @pl.when(k == N_DEV - 1)
        def _():
            dh, xv = acc[...], x_ref[...]
            r = lax.rsqrt(jnp.mean(xv * xv, axis=-1, keepdims=True) + RMS_EPS)
            dg_ref[...] += jnp.sum(dh * xv * r, axis=0, keepdims=True)
            a = dh * g_ref[...]
            dx_ref[...] = dy_ref[...] + r * a - xv * (r * r * r) * jnp.mean(a * xv, axis=-1, keepdims=True)

        if n_x:
            @pl.when((i == grid[0] - 1) & (k == grid[1] - 1))
            def _():
                _exchange_wait(x_in, x_out, x_sems, x_gather)

    rows = lambda: pl.BlockSpec((tm, D_MODEL), lambda i, k: (i, 0))
    vec = lambda: pl.BlockSpec((1, D_MODEL), lambda i, k: (0, 0))
    return _pcall(
        body, name=name, grid=grid,
        in_specs=[pl.BlockSpec((tm, COLS_PER_DEV), lambda i, k: (i, k)),
                  pl.BlockSpec((None, D_MODEL, COLS_PER_DEV), lambda i, k: (k, 0, 0)),
                  rows(), vec(), rows()] + [ANY_SPEC] * n_x,
        out_specs=(rows(), vec()) + (ANY_SPEC,) * n_x,
        out_shape=(jax.ShapeDtypeStruct((s, D_MODEL), F32), jax.ShapeDtypeStruct((1, D_MODEL), F32))
        + tuple(_exchange_out_shapes(x_arrs, x_gather)),
        scratch_shapes=[pltpu.VMEM((tm, D_MODEL), F32)] + (_exchange_sems(n_x) if n_x else []),
        compiler_params=_params(("arbitrary", "arbitrary")),
    )(du, w_all, x, g_pre, dy, *x_arrs)


def _in_proj_bwd_send(h, du, w_all, x, g_pre, dy, name):
    s = x.shape[0]
    tk = s // N_DEV
    tm = min(1024, s)
    n_i = s // tm
    grid = (N_DEV + n_i, N_DEV)
    last = N_DEV - 1
    offset = lambda row: (row + 2) & last

    def body(me_ref, h_ref, duw_ref, dux_ref, w_ref, x_ref, g_ref, dy_ref, dx_ref, dg_ref, recv_ref,
             acc_w, stage, acc_x, send_sems, recv_sems, local_sem):
        r, k = pl.program_id(0), pl.program_id(1)
        x_, y_, c_ = lax.axis_index("x"), lax.axis_index("y"), lax.axis_index("c")
        me = 4 * x_ + 2 * y_ + c_
        flip = lambda v, bit: 1 - v if bit else v
        peer = lambda n: (flip(x_, (n >> 2) & 1), flip(y_, (n >> 1) & 1), flip(c_, n & 1))

        def out_copy(n, landing=None):
            if n == 0:
                return pltpu.make_async_copy(stage.at[0], recv_ref.at[me], local_sem)
            px, py, pc = peer(n)
            dst = recv_ref.at[me] if landing is None else recv_ref.at[4 * px + 2 * py + pc]
            return pltpu.make_async_remote_copy(
                src_ref=stage.at[n % 2], dst_ref=dst, send_sem=send_sems.at[n], recv_sem=recv_sems.at[n],
                device_id=(px, py, pc), device_id_type=pl.DeviceIdType.MESH)

        def wait_sent(n):
            if n == 0:
                out_copy(0).wait()
            else:
                out_copy(n).wait_send()

        @pl.when(r < N_DEV)
        def _():
            @pl.when(k == 0)
            def _():
                acc_w[...] = jnp.zeros_like(acc_w)
            acc_w[...] += _dot_tn(h_ref[...], duw_ref[...])

            for row in range(N_DEV):
                @pl.when((k == last) & (r == row))
                def _():
                    if row >= 2:
                        wait_sent(offset(row - 2))
                    stage[row % 2] = acc_w[...].astype(BF16)
                    out_copy(offset(row)).start()

        @pl.when(r >= N_DEV)
        def _():
            @pl.when(k == 0)
            def _():
                acc_x[...] = jnp.zeros_like(acc_x)

            @pl.when((k == 0) & (r == N_DEV))
            def _():
                dg_ref[...] = jnp.zeros_like(dg_ref)
            acc_x[...] += _dot_nt(dux_ref[...], w_ref[...])

            @pl.when(k == last)
            def _():
                dh, xv = acc_x[...], x_ref[...]
                rs = lax.rsqrt(jnp.mean(xv * xv, axis=-1, keepdims=True) + RMS_EPS)
                dg_ref[...] += jnp.sum(dh * xv * rs, axis=0, keepdims=True)
                a = dh * g_ref[...]
                dx_ref[...] = dy_ref[...] + rs * a - xv * (rs * rs * rs) * jnp.mean(a * xv, axis=-1, keepdims=True)

        @pl.when((r == grid[0] - 1) & (k == last))
        def _():
            wait_sent(offset(N_DEV - 2))
            wait_sent(offset(N_DEV - 1))
            for n in range(1, N_DEV):
                out_copy(n, landing=True).wait_recv()

    in_w = lambda r: r < N_DEV
    row_x = lambda r: jnp.maximum(r - N_DEV, 0)
    rows = lambda: pl.BlockSpec((tm, D_MODEL), lambda r, k, me: (row_x(r), 0))
    vec = lambda: pl.BlockSpec((1, D_MODEL), lambda r, k, me: (0, 0))
    grid_spec = pltpu.PrefetchScalarGridSpec(
        num_scalar_prefetch=1, grid=grid,
        in_specs=[pl.BlockSpec((tk, D_MODEL), lambda r, k, me: (jnp.where(in_w(r), k, last), 0)),
                  pl.BlockSpec((tk, COLS_PER_DEV), lambda r, k, me: (jnp.where(in_w(r), k, last),
                                                                     jnp.bitwise_xor(me[0], offset(jnp.minimum(r, last))))),
                  pl.BlockSpec((tm, COLS_PER_DEV), lambda r, k, me: (row_x(r), jnp.where(in_w(r), 0, k))),
                  pl.BlockSpec((None, D_MODEL, COLS_PER_DEV), lambda r, k, me: (jnp.where(in_w(r), 0, k), 0, 0)),
                  rows(), vec(), rows()],
        out_specs=(rows(), vec(), ANY_SPEC),
        scratch_shapes=[pltpu.VMEM((D_MODEL, COLS_PER_DEV), F32), pltpu.VMEM((2, D_MODEL, COLS_PER_DEV), BF16),
                        pltpu.VMEM((tm, D_MODEL), F32), pltpu.SemaphoreType.DMA((N_DEV,)),
                        pltpu.SemaphoreType.DMA((N_DEV,)), pltpu.SemaphoreType.DMA])
    me = 4 * lax.axis_index("x") + 2 * lax.axis_index("y") + lax.axis_index("c")
    return _pcall(
        body, name=name, grid_spec=grid_spec,
        out_shape=(jax.ShapeDtypeStruct((s, D_MODEL), F32), jax.ShapeDtypeStruct((1, D_MODEL), F32),
                   jax.ShapeDtypeStruct((N_DEV, D_MODEL, COLS_PER_DEV), BF16)),
        compiler_params=_params(("arbitrary", "arbitrary")),
    )(jnp.reshape(me, (1,)).astype(jnp.int32), h, du, du, w_all, x, g_pre, dy)


def _in_proj_bwd_w(h, du, name):
    s = h.shape[0]
    tk = min(512, s)
    n_k = s // tk

    def body(h_ref, du_ref, out_ref, acc):
        k = pl.program_id(1)

        @pl.when(k == 0)
        def _():
            acc[...] = jnp.zeros_like(acc)
        acc[...] += _dot_tn(h_ref[...], du_ref[...])

        @pl.when(k == n_k - 1)
        def _():
            out_ref[...] = acc[...].astype(BF16)

    return _pcall(
        body, name=name, grid=(N_DEV, n_k),
        in_specs=[pl.BlockSpec((tk, D_MODEL), lambda j, k: (k, 0)),
                  pl.BlockSpec((tk, COLS_PER_DEV), lambda j, k: (k, j))],
        out_specs=pl.BlockSpec((None, D_MODEL, COLS_PER_DEV), lambda j, k: (j, 0, 0)),
        out_shape=jax.ShapeDtypeStruct((N_DEV, D_MODEL, COLS_PER_DEV), BF16),
        scratch_shapes=[pltpu.VMEM((D_MODEL, COLS_PER_DEV), F32)],
        compiler_params=_params(("parallel", "arbitrary")),
    )(h, du)


def _adamw_math(g, w, m, v):
    m_new = ADAM_B1 * m + (1.0 - ADAM_B1) * g
    v_new = ADAM_B2 * v + (1.0 - ADAM_B2) * (g * g)
    m_hat = m_new / (1.0 - ADAM_B1 ** ADAM_STEP)
    v_hat = v_new / (1.0 - ADAM_B2 ** ADAM_STEP)
    delta = -ADAM_LR * (m_hat / (jnp.sqrt(v_hat) + ADAM_EPS) + ADAM_WD * w)
    return delta, m_new, v_new


def _sum_partials(p_ref):
    total = p_ref[0].astype(F32)
    for d in range(1, N_DEV):
        total = total + p_ref[d].astype(F32)
    return total


def _adamw_layers(parts0, parts1, w, m, v, name):
    _, r, c = w.shape
    tr = min(128, r)
    n_r = r // tr

    def body(p0_ref, p1_ref, w_ref, m_ref, v_ref, g_ref, d_ref, mo_ref, vo_ref):
        layer = pl.program_id(0)

        @pl.when(layer == 0)
        def _():
            g_ref[...] = _sum_partials(p0_ref)

        @pl.when(layer == 1)
        def _():
            g_ref[...] = _sum_partials(p1_ref)
        d_ref[...], mo_ref[...], vo_ref[...] = _adamw_math(g_ref[...], w_ref[...], m_ref[...], v_ref[...])

    part = lambda which: pl.BlockSpec((N_DEV, tr, c), lambda l, i: (0, jnp.where(l == which, i, 0), 0))
    par = lambda: pl.BlockSpec((None, tr, c), lambda l, i: (l, i, 0))
    out = jax.ShapeDtypeStruct(w.shape, F32)
    return _pcall(
        body, name=name, grid=(2, n_r),
        in_specs=[part(0), part(1), par(), par(), par()],
        out_specs=(par(), par(), par(), par()),
        out_shape=(out, out, out, out),
        compiler_params=_params(("arbitrary", "arbitrary")),
    )(parts0, parts1, w, m, v)


def _adamw_small(parts, w, m, v, name):
    def body(p_ref, w_ref, m_ref, v_ref, g_ref, d_ref, mo_ref, vo_ref):
        g = _sum_partials(p_ref)
        g_ref[...] = g
        d_ref[...], mo_ref[...], vo_ref[...] = _adamw_math(g, w_ref[...], m_ref[...], v_ref[...])

    out = jax.ShapeDtypeStruct(w.shape, F32)
    return _pcall(body, name=name, out_shape=(out, out, out, out), compiler_params=_params())(parts, w, m, v)


def _adamw_plain(g, w, m, v, name):
    def body(g_ref, w_ref, m_ref, v_ref, d_ref, mo_ref, vo_ref):
        d_ref[...], mo_ref[...], vo_ref[...] = _adamw_math(g_ref[...], w_ref[...], m_ref[...], v_ref[...])

    out = jax.ShapeDtypeStruct(w.shape, F32)
    return _pcall(body, name=name, out_shape=(out, out, out), compiler_params=_params())(g, w, m, v)


def _rows128(a):
    return a.reshape(-1, LANES)


SMALL_NAMES = ("pre_norm_g", "pool_w", "pool_scale", "conv_w", "conv_b", "post_norm_g")


def kernel(x, pre_norm_g, w_in, pool_w, pool_scale, conv_w, conv_b, w_branch, w_out, post_norm_g, loss_target, m_pre_norm_g, m_w_in, m_pool_w, m_pool_scale, m_conv_w, m_conv_b, m_w_branch, m_w_out, m_post_norm_g, v_pre_norm_g, v_w_in, v_pool_w, v_pool_scale, v_conv_w, v_conv_b, v_w_branch, v_w_out, v_post_norm_g):
    s = x.shape[1]
    me = 4 * lax.axis_index("x") + 2 * lax.axis_index("y") + lax.axis_index("c")
    x0 = x[0]
    target = loss_target[0]
    conv_cols = conv_w.shape[-1]

    conv_w_pad = jnp.pad(conv_w.reshape(2 * 3, conv_cols), ((0, 2), (0, LANES - conv_cols)))
    w_in_all = [None, None]
    w_in_all[0], cw_g = _gather_two_level([w_in[0].astype(BF16), conv_w_pad], "gather_w_in_0")
    conv_w_full = cw_g[:, :6, :conv_cols].reshape(N_DEV, 2, 3, conv_cols).transpose(1, 2, 0, 3).reshape(2, 3, WIDTH)
    later_weights = ([w_in[1].astype(BF16), w_branch.astype(BF16), w_out.astype(BF16)], True)

    saved = []
    xin = x0
    for l in range(2):
        u, h = _in_proj_fwd(xin, pre_norm_g[l:l + 1], w_in_all[l], f"in_proj_fwd_{l}")
        y_pool = _pool_fwd(u, pool_w[l], pool_scale[l:l + 1], f"pool_fwd_{l}")
        y_conv = _conv_fwd(u, conv_w_full[l], conv_b[l:l + 1], f"conv_fwd_{l}")
        if l == 0:
            o_sb, y_sb, w_in_all[1], wb_g, wo_all = _sb_fwd(u, f"sb_fwd_{l}", later_weights)
            wb_all = wb_g.transpose(1, 2, 3, 0, 4).reshape(2, 3, WIDTH, D_MODEL)
        else:
            o_sb, y_sb = _sb_fwd(u, f"sb_fwd_{l}")
        xout, merged, pre = _merge_out_fwd(y_pool, y_conv, y_sb, u, wb_all, wo_all, xin, post_norm_g[l:l + 1], l,
                                           f"merge_out_fwd_{l}")
        saved.append((xin, u, h, y_pool, y_conv, y_sb, o_sb, merged, pre))
        xin = xout

    dy, loss_row = _loss_and_grad(xin, target, "loss")

    small = [None, None]
    recv = [None, None]
    ready = []
    for l in (1, 0):
        xl, u, h, y_pool, y_conv, y_sb, o_sb, merged, pre = saved[l]
        dmerged, dwo, dg_post = _out_proj_bwd(dy, pre, post_norm_g[l:l + 1], merged, wo_all, l, f"out_proj_bwd_{l}")
        du_merge, dyp, dyc, dys, dwb = _merge_bwd(dmerged, y_pool, y_conv, y_sb, u, wb_all, l, f"merge_bwd_{l}")
        dwb = dwb.reshape(N_DEV, 3 * WIDTH, D_MODEL // N_DEV)
        dwo = dwo.reshape(N_DEV, D_MODEL // N_DEV, D_MODEL)
        dxc, dgb, dgc, dcg, dcw, dcb = _conv_bwd(u, conv_w_full[l], conv_b[l:l + 1], dyc, f"conv_bwd_{l}")
        dpv, dpg, dpw, dps = _pool_bwd(u, pool_w[l], pool_scale[l:l + 1], dyp, f"pool_bwd_{l}")
        small[l] = dict(pool_w=dpw, pool_scale=dps, conv_w=dcw, conv_b=dcb, post_norm_g=dg_post)
        if l == 1:
            dq, dk, dv, dsg = _sb_bwd(u, o_sb, dys, f"sb_bwd_{l}")
        else:
            small[l]["pre_norm_g"] = jnp.zeros((1, D_MODEL), F32)
            packed = jnp.concatenate(
                [_rows128(jnp.stack([small[0][n], small[1][n]])) for n in SMALL_NAMES]
                + [jnp.pad(loss_row, ((0, 7), (0, 0)))], axis=0)
            dq, dk, dv, dsg, *got, packed_all = _sb_bwd(
                u, o_sb, dys, f"sb_bwd_{l}", (ready + [dwb, dwo, packed], (False,) * 5 + (True,)))
            recv[1] = got[:3]
        du = jnp.concatenate([dpv, dpg, dxc, dgb, dgc, dcg, dq, dk.astype(BF16), dv.astype(BF16), dsg, du_merge],
                             axis=1)
        if l == 1:
            dwi = _in_proj_bwd_w(h, du, f"in_proj_bwd_w_{l}")
            ready = [dwi, dwb, dwo]
            dx, dg_pre = _in_proj_bwd_x(du, w_in_all[l], xl, pre_norm_g[l:l + 1], dy, f"in_proj_bwd_x_{l}")
            small[l]["pre_norm_g"] = dg_pre
        else:
            dx, dg_pre, got_dwi = _in_proj_bwd_send(h, du, w_in_all[l], xl, pre_norm_g[l:l + 1], dy,
                                                    f"in_proj_bwd_{l}")
            recv[0] = [got_dwi] + got[3:]
        dy = dx
    grad_x = dy[None]

    (g_pre_0_all,) = _exchange([_rows128(dg_pre)], True, "gather_g_pre_0")
    packed_all = lax.dynamic_update_slice(packed_all, g_pre_0_all, (0, 0, 0))
    sizes = dict(pre_norm_g=16, pool_w=1024, pool_scale=8, conv_w=24, conv_b=8, post_norm_g=16)
    n_rows = sum(sizes.values())
    loss = jnp.sum(packed_all[:, n_rows, 0])

    given = dict(pre_norm_g=(pre_norm_g, m_pre_norm_g, v_pre_norm_g), pool_w=(pool_w, m_pool_w, v_pool_w),
                 pool_scale=(pool_scale, m_pool_scale, v_pool_scale), conv_b=(conv_b, m_conv_b, v_conv_b),
                 post_norm_g=(post_norm_g, m_post_norm_g, v_post_norm_g))
    zeros_cw = jnp.zeros((sizes["conv_w"], LANES), F32)
    pack3 = [jnp.concatenate([zeros_cw if n == "conv_w" else _rows128(given[n][k]) for n in SMALL_NAMES], axis=0)
             for k in range(3)]
    sg, sd, sm, sv = _adamw_small(packed_all[:, :n_rows], pack3[0], pack3[1], pack3[2], "adamw_small")

    def unpack(buf, name, shape):
        start = 0
        for n in SMALL_NAMES:
            if n == name:
                return buf[start:start + sizes[n]].reshape(shape)
            start += sizes[n]

    out = {}
    for n in ("pre_norm_g", "pool_w", "pool_scale", "conv_b", "post_norm_g"):
        shape = given[n][0].shape
        out[n] = tuple(unpack(b, n, shape) for b in (sg, sd, sm, sv))
    g_cw = lax.dynamic_slice_in_dim(unpack(sg, "conv_w", (2, 3, WIDTH)), me * conv_cols, conv_cols, axis=2)
    cw2 = lambda a: a.reshape(6, conv_cols)
    d_cw, m_cw, v_cw = _adamw_plain(cw2(g_cw), cw2(conv_w), cw2(m_conv_w), cw2(v_conv_w), "adamw_conv_w")
    out["conv_w"] = (g_cw,) + tuple(a.reshape(2, 3, conv_cols) for a in (d_cw, m_cw, v_cw))

    out["w_in"] = _adamw_layers(recv[0][0], recv[1][0], w_in, m_w_in, v_w_in, "adamw_w_in")
    cols = D_MODEL // N_DEV
    wb3 = lambda a: a.reshape(2, 3 * WIDTH, cols)
    out["w_branch"] = tuple(a.reshape(2, 3, WIDTH, cols) for a in _adamw_layers(
        recv[0][1], recv[1][1], wb3(w_branch), wb3(m_w_branch), wb3(v_w_branch), "adamw_w_branch"))
    out["w_out"] = _adamw_layers(recv[0][2], recv[1][2], w_out, m_w_out, v_w_out, "adamw_w_out")

    order = ("pre_norm_g", "w_in", "pool_w", "pool_scale", "conv_w", "conv_b", "w_branch", "w_out", "post_norm_g")
    return (loss, grad_x) + tuple(out[n][k] for k in range(4) for n in order)
```

```python
import functools

import jax
import jax.numpy as jnp
from jax import lax
from jax.experimental import pallas as pl
from jax.experimental.pallas import tpu as pltpu

F32 = jnp.float32
BF16 = jnp.bfloat16

N_DEV = 8
D_MODEL = 1024
WIDTH = 512
N_IN = 8192
COLS_PER_DEV = N_IN // N_DEV
HEAD_DIM = 64
LANES = 128
SB_SCALE = HEAD_DIM ** -0.5
LOG2E = 1.4426950408889634
RMS_EPS = 1e-6
POOL_HALO = 16
CONV_HALO = 8
ADAM_LR, ADAM_B1, ADAM_B2, ADAM_EPS, ADAM_WD, ADAM_STEP = 0.001, 0.9, 0.999, 1e-08, 0.01, 10
VMEM_LIMIT = 60 * 1024 * 1024

CB_POOL_V, CB_POOL_G = 0, 4
CB_CONV_X, CB_CONV_GB, CB_CONV_GC, CB_CONV_G = 8, 12, 16, 20
CB_SB_Q, CB_SB_K, CB_SB_V, CB_SB_G = 24, 28, 32, 36
MERGE_BLOCK_1024 = 5

DU_PIECES = 16
DU_MERGE = (0, 6)
DU_POOL = (6, 2)
DU_CONV = (8, 4)
DU_SB_QG = (12, 2)
DU_SB_KV = (14, 2)


def _du_pieces_of_block(j):
    first, second = 2 * (j - 5), 2 * (j - 5) + 1
    for block, (a, b) in enumerate(((6, 7), (8, 9), (10, 11), (12, 14), (15, 13))):
        first = jnp.where(j == block, a, first)
        second = jnp.where(j == block, b, second)
    return first, second


def _pcall(body, **kw):
    return pl.pallas_call(body, **kw)


def _params(sem=None):
    if sem is None:
        return pltpu.CompilerParams(vmem_limit_bytes=VMEM_LIMIT)
    return pltpu.CompilerParams(dimension_semantics=sem, vmem_limit_bytes=VMEM_LIMIT)


def _sigmoid(x):
    return 1.0 / (1.0 + jnp.exp(-x))


def _dot(a, b):
    return jnp.dot(a, b, preferred_element_type=F32)


def _dot_nt(a, b):
    return lax.dot_general(a, b, (((1,), (1,)), ((), ())), preferred_element_type=F32)


def _dot_tn(a, b):
    return lax.dot_general(a, b, (((0,), (0,)), ((), ())), preferred_element_type=F32)


def _split_bf16(x):
    hi = x.astype(BF16)
    lo = (x - hi.astype(F32)).astype(BF16)
    return hi, lo


N_PEER = N_DEV - 1
ANY_SPEC = pl.BlockSpec(memory_space=pl.ANY)


def _exchange_copies(ins, outs, send_sems, recv_sems, local_sems, gather, with_recvs=True):
    n = len(ins)
    gathers = _per_array(gather, n)
    x, y, c = lax.axis_index("x"), lax.axis_index("y"), lax.axis_index("c")
    me = 4 * x + 2 * y + c
    flip = lambda v, bit: 1 - v if bit else v
    local, sends, recvs = [], [], []
    for a in range(n):
        src = ins[a] if gathers[a] else ins[a].at[me]
        local.append(pltpu.make_async_copy(src, outs[a].at[me], local_sems.at[a]))
    for k in range(N_PEER):
        px, py, pc = flip(x, ((k + 1) >> 2) & 1), flip(y, ((k + 1) >> 1) & 1), flip(c, (k + 1) & 1)
        peer_id = 4 * px + 2 * py + pc
        for a in range(n):
            src = ins[a] if gathers[a] else ins[a].at[peer_id]
            common = dict(src_ref=src, send_sem=send_sems.at[a * N_PEER + k], recv_sem=recv_sems.at[a * N_PEER + k],
                          device_id=(px, py, pc), device_id_type=pl.DeviceIdType.MESH)
            sends.append(pltpu.make_async_remote_copy(dst_ref=outs[a].at[me], **common))
            if with_recvs:
                recvs.append(pltpu.make_async_remote_copy(dst_ref=outs[a].at[peer_id], **common))
    return local, sends, recvs


def _exchange_start(ins, outs, sems, gather):
    local, sends, _ = _exchange_copies(ins, outs, *sems, gather, with_recvs=False)
    for cp in local + sends:
        cp.start()


def _exchange_wait(ins, outs, sems, gather):
    local, sends, recvs = _exchange_copies(ins, outs, *sems, gather)
    for cp in recvs:
        cp.wait_recv()
    for cp in sends:
        cp.wait_send()
    for cp in local:
        cp.wait()


def _per_array(gather, n):
    return tuple(gather) if isinstance(gather, (tuple, list)) else (gather,) * n


def _exchange_out_shapes(arrs, gather):
    return [jax.ShapeDtypeStruct((N_DEV,) + tuple(a.shape if g else a.shape[1:]), a.dtype)
            for a, g in zip(arrs, _per_array(gather, len(arrs)))]


def _gather_two_level(arrs, name):
    n = len(arrs)

    def body(*refs):
        ins, outs = refs[:n], refs[n:2 * n]
        send_sems, recv_sems, local_sems = refs[2 * n:]
        x, y, c = lax.axis_index("x"), lax.axis_index("y"), lax.axis_index("c")
        me, sibling = (x, y, c), (x, y, 1 - c)
        chips = [(1 - x, y), (x, 1 - y), (1 - x, 1 - y)]
        slot = lambda dev: 4 * dev[0] + 2 * dev[1] + dev[2]

        def copy(a, k, block, to, src=None):
            return pltpu.make_async_remote_copy(
                src_ref=outs[a].at[slot(block)] if src is None else src, dst_ref=outs[a].at[slot(block)],
                send_sem=send_sems.at[a * N_PEER + k], recv_sem=recv_sems.at[a * N_PEER + k],
                device_id=to, device_id_type=pl.DeviceIdType.MESH)

        local = [pltpu.make_async_copy(ins[a], outs[a].at[slot(me)], local_sems.at[a]) for a in range(n)]
        first = []
        for a in range(n):
            first.append(copy(a, 0, me, sibling, src=ins[a]))
            first += [copy(a, 1 + j, me, (*chip, c), src=ins[a]) for j, chip in enumerate(chips)]
        for cp in local + first:
            cp.start()
        passed = []
        for j, chip in enumerate(chips):
            for a in range(n):
                copy(a, 1 + j, (*chip, c), me).wait_recv()
                passed.append(copy(a, 4 + j, (*chip, c), sibling))
                passed[-1].start()
        for a in range(n):
            copy(a, 0, sibling, me).wait_recv()
        for j, chip in enumerate(chips):
            for a in range(n):
                copy(a, 4 + j, (*chip, 1 - c), me).wait_recv()
        for cp in first + passed:
            cp.wait_send()
        for cp in local:
            cp.wait()

    return _pcall(
        body, name=name,
        out_shape=tuple(_exchange_out_shapes(arrs, True)),
        in_specs=[ANY_SPEC] * n, out_specs=tuple([ANY_SPEC] * n),
        scratch_shapes=_exchange_sems(n),
    )(*arrs)


def _exchange_sems(n):
    return [pltpu.SemaphoreType.DMA((n * N_PEER,)), pltpu.SemaphoreType.DMA((n * N_PEER,)),
            pltpu.SemaphoreType.DMA((n,))]


def _exchange(arrs, gather, name):
    n = len(arrs)

    def body(*refs):
        ins, outs, sems = refs[:n], refs[n:2 * n], refs[2 * n:]
        _exchange_start(ins, outs, sems, gather)
        _exchange_wait(ins, outs, sems, gather)

    return _pcall(
        body, name=name,
        out_shape=tuple(_exchange_out_shapes(arrs, gather)),
        in_specs=[ANY_SPEC] * n, out_specs=tuple([ANY_SPEC] * n),
        scratch_shapes=_exchange_sems(n),
    )(*arrs)


def _in_proj_fwd(x, g, w_all, name):
    s = x.shape[0]
    tm = min(1024, s)

    def body(x_ref, g_ref, w_ref, u_ref, h_ref, hs):
        @pl.when(pl.program_id(1) == 0)
        def _():
            xv = x_ref[...]
            r = lax.rsqrt(jnp.mean(xv * xv, axis=-1, keepdims=True) + RMS_EPS)
            hv = (xv * r * g_ref[...]).astype(BF16)
            hs[...] = hv
            h_ref[...] = hv
        u_ref[...] = _dot(hs[...], w_ref[...])

    return _pcall(
        body, name=name, grid=(s // tm, N_DEV),
        in_specs=[pl.BlockSpec((tm, D_MODEL), lambda i, j: (i, 0)),
                  pl.BlockSpec((1, D_MODEL), lambda i, j: (0, 0)),
                  pl.BlockSpec((None, D_MODEL, COLS_PER_DEV), lambda i, j: (j, 0, 0))],
        out_specs=(pl.BlockSpec((tm, COLS_PER_DEV), lambda i, j: (i, j)),
                   pl.BlockSpec((tm, D_MODEL), lambda i, j: (i, 0))),
        out_shape=(jax.ShapeDtypeStruct((s, N_IN), F32), jax.ShapeDtypeStruct((s, D_MODEL), BF16)),
        scratch_shapes=[pltpu.VMEM((tm, D_MODEL), BF16)],
        compiler_params=_params(("parallel", "arbitrary")),
    )(x, g, w_all)


def _pool_window(vs, t0, t, grp):
    ext = vs[pl.ds(t0, t + POOL_HALO), :]
    s2 = ext + pltpu.roll(ext, 1, 0)
    s4 = s2 + pltpu.roll(s2, 2, 0)
    s8 = s4 + pltpu.roll(s4, 4, 0)
    s16 = s8 + pltpu.roll(s8, 8, 0)
    sel = jnp.where(grp == 0, s2, jnp.where(grp == 1, s4, jnp.where(grp == 2, s8, s16)))
    return sel[POOL_HALO:, :], ext[POOL_HALO:, :]


def _pool_count(t0, t, grp):
    pos = t0 + lax.broadcasted_iota(jnp.int32, (t, 1), 0)
    return jnp.minimum(pos + 1, jnp.left_shift(2, grp)).astype(F32)


def _pool_fwd(u, pool_w, pool_scale, name):
    s = u.shape[0]
    t = min(256, s)

    def body(pv_ref, pg_ref, w_ref, sc_ref, y_ref, vs):
        grp = pl.program_id(0)
        vs[0:POOL_HALO, :] = jnp.zeros((POOL_HALO, LANES), F32)
        vs[POOL_HALO:, :] = pv_ref[...]
        wb = w_ref[...].astype(BF16)
        scale = sc_ref[...]

        def tile(i, carry):
            t0 = pl.multiple_of(i * t, t)
            win, v = _pool_window(vs, t0, t, grp)
            pooled = win / _pool_count(t0, t, grp) - v
            mixed = _dot(pooled.astype(BF16), wb)
            gate = pg_ref[pl.ds(t0, t), :]
            y_ref[pl.ds(t0, t), :] = (mixed * scale * (gate * _sigmoid(gate))).astype(BF16)
            return carry

        lax.fori_loop(0, s // t, tile, 0)

    return _pcall(
        body, name=name, grid=(4,),
        in_specs=[pl.BlockSpec((s, LANES), lambda g: (0, CB_POOL_V + g)),
                  pl.BlockSpec((s, LANES), lambda g: (0, CB_POOL_G + g)),
                  pl.BlockSpec((None, LANES, LANES), lambda g: (g, 0, 0)),
                  pl.BlockSpec((1, LANES), lambda g: (0, g))],
        out_specs=pl.BlockSpec((s, LANES), lambda g: (0, g)),
        out_shape=jax.ShapeDtypeStruct((s, WIDTH), BF16),
        scratch_shapes=[pltpu.VMEM((POOL_HALO + s, LANES), F32)],
        compiler_params=_params(("arbitrary",)),
    )(u, u, pool_w, pool_scale)


def _conv_taps(zs, t0, t):
    ext = zs[pl.ds(t0, t + CONV_HALO), :]
    z0 = ext[CONV_HALO:, :]
    z1 = pltpu.roll(ext, 1, 0)[CONV_HALO:, :]
    z2 = pltpu.roll(ext, 2, 0)[CONV_HALO:, :]
    return z0, z1, z2


def _conv_fwd(u, conv_w, conv_b, name):
    s = u.shape[0]
    t = min(256, s)

    def body(xc_ref, gb_ref, gc_ref, cg_ref, w_ref, b_ref, y_ref, zs):
        zs[0:CONV_HALO, :] = jnp.zeros((CONV_HALO, LANES), F32)
        zs[CONV_HALO:, :] = gc_ref[...] * xc_ref[...]
        w0, w1, w2 = w_ref[0:1, :], w_ref[1:2, :], w_ref[2:3, :]
        bias = b_ref[...]

        def tile(i, carry):
            t0 = pl.multiple_of(i * t, t)
            z0, z1, z2 = _conv_taps(zs, t0, t)
            conv = w0 * z2 + w1 * z1 + w2 * z0
            gate = cg_ref[pl.ds(t0, t), :]
            y = gb_ref[pl.ds(t0, t), :] * (conv + bias) * (gate * _sigmoid(gate))
            y_ref[pl.ds(t0, t), :] = y.astype(BF16)
            return carry

        lax.fori_loop(0, s // t, tile, 0)

    col = lambda base: pl.BlockSpec((s, LANES), lambda j: (0, base + j))
    return _pcall(
        body, name=name, grid=(4,),
        in_specs=[col(CB_CONV_X), col(CB_CONV_GB), col(CB_CONV_GC), col(CB_CONV_G),
                  pl.BlockSpec((3, LANES), lambda j: (0, j)),
                  pl.BlockSpec((1, LANES), lambda j: (0, j))],
        out_specs=pl.BlockSpec((s, LANES), lambda j: (0, j)),
        out_shape=jax.ShapeDtypeStruct((s, WIDTH), BF16),
        scratch_shapes=[pltpu.VMEM((CONV_HALO + s, LANES), F32)],
        compiler_params=_params(("arbitrary",)),
    )(u, u, u, u, conv_w, conv_b)


def _first_head_lanes(rows, width=LANES):
    lane = lax.broadcasted_iota(jnp.int32, (rows, width), 1)
    return jnp.bitwise_and(lane, LANES - 1) < HEAD_DIM


def _stack_heads(x, first):
    zero = jnp.zeros_like(x)
    return jnp.concatenate([jnp.where(first, x, zero), jnp.where(first, zero, x)], axis=0).astype(BF16)


def _causal_mask(tq, tk, copies):
    row = lax.broadcasted_iota(jnp.int32, (tq, tk), 0)
    col = lax.broadcasted_iota(jnp.int32, (tq, tk), 1)
    return jnp.concatenate([col < row] * copies, axis=0)


def _suffix_matrix(tk, inclusive, parts):
    r = lax.broadcasted_iota(jnp.int32, (parts * tk, 2 * tk), 0)
    c = lax.broadcasted_iota(jnp.int32, (parts * tk, 2 * tk), 1)
    r = jnp.bitwise_and(r, tk - 1)
    tri = (r >= c) if inclusive else (r > c)
    return jnp.where(c >= tk, 1.0, jnp.where(tri, 1.0, 0.0)).astype(BF16)


def _suffix_sums(x, m):
    hi, lo = _split_bf16(x)
    return _dot(jnp.concatenate([hi, lo], axis=1), m)


def _sb_log_terms(z, mask, m_strict):
    ls = jnp.minimum(z, 0.0) - jnp.log(1.0 + jnp.exp2(jnp.abs(z) * -LOG2E))
    lk = ls - z
    if mask is not None:
        lk = jnp.where(mask, lk, 0.0)
    return ls, _dot(lk.astype(BF16), m_strict)


SB_PAIRS = 4


def _pair_lanes(a):
    return slice(a * LANES, (a + 1) * LANES)


def _sb_fwd(u, name, xchg=None):
    s = u.shape[0]
    tq = tk = min(128, s)
    pairs = SB_PAIRS
    width = pairs * LANES
    rows = 2 * pairs * tq
    x_arrs, x_gather = xchg if xchg else ((), True)
    n_x = len(x_arrs)
    grid = (4 // pairs, s // tq)

    def body(*refs):
        q_ref, k_ref, v_ref, g_ref = refs[:4]
        x_in, refs = refs[4:4 + n_x], refs[4 + n_x:]
        o_ref, y_ref = refs[:2]
        x_out, refs = refs[2:2 + n_x], refs[2 + n_x:]
        kbf, vst, z_s, ell_s, carry_s = refs[:5]
        x_sems = refs[5:]
        i = pl.program_id(1)
        if n_x:
            @pl.when((pl.program_id(0) == 0) & (i == 0))
            def _():
                _exchange_start(x_in, x_out, x_sems, x_gather)

        @pl.when(i == 0)
        def _():
            kbf[...] = k_ref[...].astype(BF16)
            first_s = _first_head_lanes(s, width)
            vf = v_ref[...]
            vst[0] = jnp.where(first_s, vf, 0.0).astype(BF16)
            vst[1] = jnp.where(first_s, 0.0, vf).astype(BF16)

        first = _first_head_lanes(tq)
        mask = _causal_mask(tq, tk, 2 * pairs)
        m_strict = _suffix_matrix(tk, False, 1)
        qcat = jnp.concatenate([_stack_heads(q_ref[:, _pair_lanes(a)] * SB_SCALE, first) for a in range(pairs)],
                               axis=0)

        def scores(b):
            off = pl.multiple_of(jnp.maximum(b, 0) * tk, tk)
            z_s[...] = jnp.concatenate(
                [_dot_nt(qcat[a * 2 * tq:(a + 1) * 2 * tq], kbf[pl.ds(off, tk), _pair_lanes(a)])
                 for a in range(pairs)], axis=0)

        def log_weights(m):
            ls, cs = _sb_log_terms(z_s[...], m, m_strict)
            carry = carry_s[...]
            ell_s[...] = ls + cs[:, :tk] + carry
            carry_s[...] = carry + cs[:, tk:]

        def consume(b, accs, m):
            w = jnp.exp(ell_s[...])
            if m is not None:
                w = jnp.where(m, w, 0.0)
            wb = w.astype(BF16)
            off = pl.multiple_of(b * tk, tk)
            new = []
            for a in range(pairs):
                r0 = a * 2 * tq
                wcat = jnp.concatenate([wb[r0:r0 + tq], wb[r0 + tq:r0 + 2 * tq]], axis=1)
                vcat = jnp.concatenate([vst[0, pl.ds(off, tk), _pair_lanes(a)], vst[1, pl.ds(off, tk), _pair_lanes(a)]],
                                       axis=0)
                new.append(accs[a] + _dot(wcat, vcat))
            return tuple(new)

        carry_s[...] = jnp.zeros((rows, tk), F32)
        scores(i)
        log_weights(mask)
        scores(i - 1)
        accs = consume(i, tuple(jnp.zeros((tq, LANES), F32) for _ in range(pairs)), mask)
        log_weights(None)
        scores(i - 2)

        def step(n, accs):
            accs = consume(i - n, accs, None)
            log_weights(None)
            scores(i - n - 2)
            return accs

        accs = lax.fori_loop(1, i + 1, step, accs)
        o = jnp.concatenate(accs, axis=1)
        o_ref[...] = o
        gate = g_ref[...]
        y_ref[...] = (o * (gate * _sigmoid(gate))).astype(BF16)
        if n_x:
            @pl.when((pl.program_id(0) == grid[0] - 1) & (i == grid[1] - 1))
            def _():
                _exchange_wait(x_in, x_out, x_sems, x_gather)

    base = lambda cb: cb // pairs
    qblk = lambda cb: pl.BlockSpec((tq, width), lambda p, i: (i, base(cb) + p))
    full = lambda cb: pl.BlockSpec((s, width), lambda p, i: (0, base(cb) + p), pipeline_mode=pl.Buffered(1))
    state = pltpu.VMEM((rows, tk), F32)
    return _pcall(
        body, name=name, grid=grid,
        in_specs=[qblk(CB_SB_Q), full(CB_SB_K), full(CB_SB_V), qblk(CB_SB_G)] + [ANY_SPEC] * n_x,
        out_specs=(qblk(0), qblk(0)) + (ANY_SPEC,) * n_x,
        out_shape=(jax.ShapeDtypeStruct((s, WIDTH), F32), jax.ShapeDtypeStruct((s, WIDTH), BF16))
        + tuple(_exchange_out_shapes(x_arrs, x_gather)),
        scratch_shapes=[pltpu.VMEM((s, width), BF16), pltpu.VMEM((2, s, width), BF16), state, state, state]
        + (_exchange_sems(n_x) if n_x else []),
        compiler_params=_params(("arbitrary", "arbitrary")),
    )(u, u, u, u, *x_arrs)


def _merge_out_fwd(y_pool, y_conv, y_sb, u, wb_all, wo_all, x, g_post, layer, name):
    s = x.shape[0]
    tm = min(512, s)

    def body(yp, yc, ys, m0, m1, m2, wb_ref, wo_ref, x_ref, g_ref, out_ref, merged_ref, pre_ref):
        merged = jnp.zeros((tm, D_MODEL), F32)
        for n, (y_ref, m_ref) in enumerate(((yp, m0), (yc, m1), (ys, m2))):
            merged = merged + _sigmoid(m_ref[...]) * _dot(y_ref[...], wb_ref[n])
        mb = merged.astype(BF16)
        merged_ref[...] = mb
        pre = _dot(mb, wo_ref[...].reshape(D_MODEL, D_MODEL))
        pre_ref[...] = pre
        r = lax.rsqrt(jnp.mean(pre * pre, axis=-1, keepdims=True) + RMS_EPS)
        out_ref[...] = x_ref[...] + pre * r * g_ref[...]

    rows = lambda w: pl.BlockSpec((tm, w), lambda i: (i, 0))
    merge = lambda n: pl.BlockSpec((tm, D_MODEL), lambda i: (i, MERGE_BLOCK_1024 + n))
    return _pcall(
        body, name=name, grid=(s // tm,),
        in_specs=[rows(WIDTH), rows(WIDTH), rows(WIDTH), merge(0), merge(1), merge(2),
                  pl.BlockSpec((None, 3, WIDTH, D_MODEL), lambda i: (layer, 0, 0, 0)),
                  pl.BlockSpec((N_DEV, None, D_MODEL // N_DEV, D_MODEL), lambda i: (0, layer, 0, 0)),
                  rows(D_MODEL), pl.BlockSpec((1, D_MODEL), lambda i: (0, 0))],
        out_specs=(rows(D_MODEL), rows(D_MODEL), rows(D_MODEL)),
        out_shape=(jax.ShapeDtypeStruct((s, D_MODEL), F32), jax.ShapeDtypeStruct((s, D_MODEL), BF16),
                   jax.ShapeDtypeStruct((s, D_MODEL), F32)),
        compiler_params=_params(("arbitrary",)),
    )(y_pool, y_conv, y_sb, u, u, u, wb_all, wo_all, x, g_post)


def _loss_and_grad(y, target, name):
    s = y.shape[0]
    tm = min(512, s)

    def body(y_ref, t_ref, dy_ref, loss_ref, acc):
        i = pl.program_id(0)

        @pl.when(i == 0)
        def _():
            acc[...] = jnp.zeros_like(acc)
        err = y_ref[...] - t_ref[...]
        dy_ref[...] = err / D_MODEL
        acc[...] += jnp.sum(err * err, axis=0, keepdims=True)

        @pl.when(i == pl.num_programs(0) - 1)
        def _():
            total = jnp.sum(acc[...], axis=1, keepdims=True) * (0.5 / D_MODEL)
            loss_ref[...] = jnp.broadcast_to(total, (1, LANES))

    return _pcall(
        body, name=name, grid=(s // tm,),
        in_specs=[pl.BlockSpec((tm, D_MODEL), lambda i: (i, 0)), pl.BlockSpec((tm, D_MODEL), lambda i: (i, 0))],
        out_specs=(pl.BlockSpec((tm, D_MODEL), lambda i: (i, 0)), pl.BlockSpec((1, LANES), lambda i: (0, 0))),
        out_shape=(jax.ShapeDtypeStruct((s, D_MODEL), F32), jax.ShapeDtypeStruct((1, LANES), F32)),
        scratch_shapes=[pltpu.VMEM((1, D_MODEL), F32)],
        compiler_params=_params(("arbitrary",)),
    )(y, target)


def _out_proj_bwd(dy, pre, g_post, merged, wo_all, layer, name):
    s = dy.shape[0]
    tm = min(512, s)
    n_tiles = s // tm

    def body(dy_ref, pre_ref, g_ref, mg_ref, wo_ref, dm_ref, dwo_ref, dg_ref, acc):
        i = pl.program_id(0)

        @pl.when(i == 0)
        def _():
            acc[...] = jnp.zeros_like(acc)
            dg_ref[...] = jnp.zeros_like(dg_ref)
        dyv, pre_v = dy_ref[...], pre_ref[...]
        r = lax.rsqrt(jnp.mean(pre_v * pre_v, axis=-1, keepdims=True) + RMS_EPS)
        dg_ref[...] += jnp.sum(dyv * pre_v * r, axis=0, keepdims=True)
        a = dyv * g_ref[...]
        dpre = r * a - pre_v * (r * r * r) * jnp.mean(a * pre_v, axis=-1, keepdims=True)
        db = dpre.astype(BF16)
        acc[...] += _dot_tn(mg_ref[...], db)
        dm_ref[...] = _dot_nt(db, wo_ref[...].reshape(D_MODEL, D_MODEL))

        @pl.when(i == n_tiles - 1)
        def _():
            dwo_ref[...] = acc[...].astype(BF16)

    rows = lambda: pl.BlockSpec((tm, D_MODEL), lambda i: (i, 0))
    return _pcall(
        body, name=name, grid=(n_tiles,),
        in_specs=[rows(), rows(), pl.BlockSpec((1, D_MODEL), lambda i: (0, 0)), rows(),
                  pl.BlockSpec((N_DEV, None, D_MODEL // N_DEV, D_MODEL), lambda i: (0, layer, 0, 0))],
        out_specs=(rows(), pl.BlockSpec((D_MODEL, D_MODEL), lambda i: (0, 0)),
                   pl.BlockSpec((1, D_MODEL), lambda i: (0, 0))),
        out_shape=(jax.ShapeDtypeStruct((s, D_MODEL), F32), jax.ShapeDtypeStruct((D_MODEL, D_MODEL), BF16),
                   jax.ShapeDtypeStruct((1, D_MODEL), F32)),
        scratch_shapes=[pltpu.VMEM((D_MODEL, D_MODEL), F32)],
        compiler_params=_params(("arbitrary",)),
    )(dy, pre, g_post, merged, wo_all)


def _merge_bwd(dmerged, y_pool, y_conv, y_sb, u, wb_all, layer, name):
    s = dmerged.shape[0]
    tm = min(512, s)
    n_tiles = s // tm
    cols = D_MODEL // N_DEV

    def body(dm_ref, yp, yc, ys, m0, m1, m2, wb_ref, du_ref, dyp, dyc, dys, dwb_ref, acc):
        i = pl.program_id(0)

        @pl.when(i == 0)
        def _():
            acc[...] = jnp.zeros_like(acc)
        dm = dm_ref[...]
        for n, (y_ref, m_ref, dy_ref) in enumerate(((yp, m0, dyp), (yc, m1, dyc), (ys, m2, dys))):
            yv = y_ref[...]
            wb = wb_ref[n]
            gate = _sigmoid(m_ref[...])
            proj = _dot(yv, wb)
            dgate = (dm * proj * gate * (1.0 - gate)).astype(BF16)
            du_ref[2 * n] = dgate[:, :WIDTH]
            du_ref[2 * n + 1] = dgate[:, WIDTH:]
            dproj = (dm * gate).astype(BF16)
            acc[n] += _dot_tn(yv, dproj)
            dy_ref[...] = _dot_nt(dproj, wb)

        @pl.when(i == n_tiles - 1)
        def _():
            for j in range(N_DEV):
                for n in range(3):
                    dwb_ref[j, n] = acc[n, :, j * cols:(j + 1) * cols].astype(BF16)

    rows = lambda w: pl.BlockSpec((tm, w), lambda i: (i, 0))
    merge = lambda n: pl.BlockSpec((tm, D_MODEL), lambda i: (i, MERGE_BLOCK_1024 + n))
    return _pcall(
        body, name=name, grid=(n_tiles,),
        in_specs=[rows(D_MODEL), rows(WIDTH), rows(WIDTH), rows(WIDTH), merge(0), merge(1), merge(2),
                  pl.BlockSpec((None, 3, WIDTH, D_MODEL), lambda i: (layer, 0, 0, 0))],
        out_specs=(pl.BlockSpec((DU_MERGE[1], tm, WIDTH), lambda i: (DU_MERGE[0] // DU_MERGE[1], i, 0)),
                   rows(WIDTH), rows(WIDTH), rows(WIDTH),
                   pl.BlockSpec((N_DEV, 3, WIDTH, cols), lambda i: (0, 0, 0, 0))),
        out_shape=(jax.ShapeDtypeStruct((DU_PIECES, s, WIDTH), BF16),
                   jax.ShapeDtypeStruct((s, WIDTH), F32), jax.ShapeDtypeStruct((s, WIDTH), F32),
                   jax.ShapeDtypeStruct((s, WIDTH), F32),
                   jax.ShapeDtypeStruct((N_DEV, 3, WIDTH, cols), BF16)),
        scratch_shapes=[pltpu.VMEM((3, WIDTH, D_MODEL), F32)],
        compiler_params=_params(("arbitrary",)),
    )(dmerged, y_pool, y_conv, y_sb, u, u, u, wb_all)


def _sb_bwd(u, o, dys, du, name, xchg=None):
    s = u.shape[0]
    tq = tk = min(128, s)
    pairs = SB_PAIRS
    width = pairs * LANES
    assert width == WIDTH
    rows = 2 * pairs * tq
    pair_rows = lambda a: slice(a * 2 * tq, (a + 1) * 2 * tq)

    x_arrs, x_gather = xchg if xchg else ((), True)
    n_x = len(x_arrs)
    grid = (4 // pairs, s // tq)

    def body(*refs):
        q_ref, k_ref, v_ref, g_ref, o_ref, dys_ref = refs[:6]
        x_in, refs = refs[7:7 + n_x], refs[7 + n_x:]
        du_ref, dk_ref, dv_ref = refs[:3]
        dq_ref, dg_ref = du_ref.at[0], du_ref.at[1]
        x_out, refs = refs[3:3 + n_x], refs[3 + n_x:]
        kbf, vbf, kst, z_s, ell_s, ls_s, cl_s, wb_s, g_s, bef_s, cg_s, beta_s = refs[:12]
        x_sems = refs[12:]
        i = pl.program_id(1)
        if n_x:
            @pl.when((pl.program_id(0) == 0) & (i == 0))
            def _():
                _exchange_start(x_in, x_out, x_sems, x_gather)

        @pl.when(i == 0)
        def _():
            dk_ref[...] = jnp.zeros_like(dk_ref)
            dv_ref[...] = jnp.zeros_like(dv_ref)
            kf = k_ref[...]
            kbf[...] = kf.astype(BF16)
            vbf[...] = v_ref[...].astype(BF16)
            first_s = _first_head_lanes(s, width)
            kst[0] = jnp.where(first_s, kf, 0.0).astype(BF16)
            kst[1] = jnp.where(first_s, 0.0, kf).astype(BF16)

        first = _first_head_lanes(tq)
        mask = _causal_mask(tq, tk, 2 * pairs)
        m_strict = _suffix_matrix(tk, False, 1)
        m_incl = _suffix_matrix(tk, True, 2)

        gate = g_ref[...]
        sg = _sigmoid(gate)
        dy = dys_ref[...]
        ov = o_ref[...]
        dg_ref[...] = (dy * ov * (sg * (1.0 + gate * (1.0 - sg)))).astype(BF16)
        do = (dy * (gate * sg)).astype(BF16)
        prod = do.astype(F32) * ov
        row_sum = lambda v: jnp.broadcast_to(jnp.sum(v, axis=1, keepdims=True), (tq, tk))
        dsum, docat, qcat = [], [], []
        for a in range(pairs):
            pa = prod[:, _pair_lanes(a)]
            dsum += [row_sum(jnp.where(first, pa, 0.0)), row_sum(jnp.where(first, 0.0, pa))]
            docat.append(_stack_heads(do[:, _pair_lanes(a)], first))
            qcat.append(_stack_heads(q_ref[:, _pair_lanes(a)] * SB_SCALE, first))
        dsum = jnp.concatenate(dsum, axis=0)

        def block_start(b):
            return pl.multiple_of(jnp.maximum(b, 0) * tk, tk)

        def scores(b):
            off = block_start(b)
            z_s[...] = jnp.concatenate([_dot_nt(qcat[a], kbf[pl.ds(off, tk), _pair_lanes(a)]) for a in range(pairs)],
                                       axis=0)

        def log_weights(m):
            ls, cs = _sb_log_terms(z_s[...], m, m_strict)
            cl = cl_s[...]
            ell_s[...] = ls + cs[:, :tk] + cl
            cl_s[...] = cl + cs[:, tk:]
            ls_s[...] = ls

        def weights(b, m):
            off = block_start(b)
            dwt = jnp.concatenate([_dot_nt(docat[a], vbf[pl.ds(off, tk), _pair_lanes(a)]) for a in range(pairs)],
                                  axis=0)
            w = jnp.exp(ell_s[...])
            if m is not None:
                w = jnp.where(m, w, 0.0)
            wb = w.astype(BF16)
            g = dwt * wb.astype(F32)
            gs = _suffix_sums(g, m_incl)
            cg = cg_s[...]
            beta = jnp.exp(ls_s[...])
            wb_s[...] = wb
            beta_s[...] = beta
            g_s[...] = g * (1.0 - beta)
            bef_s[...] = gs[:, :tk] + cg
            cg_s[...] = cg + gs[:, tk:]

        def grads(b, dqs, m):
            dz = g_s[...] - beta_s[...] * (dsum - bef_s[...])
            if m is not None:
                dz = jnp.where(m, dz, 0.0)
            dzb = dz.astype(BF16)
            wb = wb_s[...]
            off = pl.multiple_of(b * tk, tk)
            new = []
            for a in range(pairs):
                r0 = a * 2 * tq
                kcat = jnp.concatenate([kst[0, pl.ds(off, tk), _pair_lanes(a)], kst[1, pl.ds(off, tk), _pair_lanes(a)]],
                                       axis=0)
                new.append(dqs[a] + _dot(jnp.concatenate([dzb[r0:r0 + tq], dzb[r0 + tq:r0 + 2 * tq]], axis=1), kcat))
                dk_ref[pl.ds(off, tk), _pair_lanes(a)] += _dot_tn(dzb[pair_rows(a)], qcat[a])
                dv_ref[pl.ds(off, tk), _pair_lanes(a)] += _dot_tn(wb[pair_rows(a)], docat[a])
            return tuple(new)

        zero = jnp.zeros((rows, tk), F32)
        cl_s[...] = zero
        cg_s[...] = zero
        scores(i)
        log_weights(mask)
        scores(i - 1)
        weights(i, mask)
        log_weights(None)
        scores(i - 2)
        dqs = grads(i, tuple(jnp.zeros((tq, LANES), F32) for _ in range(pairs)), mask)
        weights(i - 1, None)
        log_weights(None)
        scores(i - 3)

        def step(n, dqs):
            dqs = grads(i - n, dqs, None)
            weights(i - n - 1, None)
            log_weights(None)
            scores(i - n - 3)
            return dqs

        dqs = lax.fori_loop(1, i + 1, step, dqs)
        dq_ref[...] = (jnp.concatenate(dqs, axis=1) * SB_SCALE).astype(BF16)
        if n_x:
            @pl.when((pl.program_id(0) == grid[0] - 1) & (i == grid[1] - 1))
            def _():
                _exchange_wait(x_in, x_out, x_sems, x_gather)

    base = lambda cb: cb // pairs
    qblk = lambda cb: pl.BlockSpec((tq, width), lambda p, i: (i, base(cb) + p))
    full = lambda cb: pl.BlockSpec((s, width), lambda p, i: (0, base(cb) + p), pipeline_mode=pl.Buffered(1))
    state = pltpu.VMEM((rows, tk), F32)
    return _pcall(
        body, name=name, grid=grid,
        in_specs=[qblk(CB_SB_Q), full(CB_SB_K), full(CB_SB_V), qblk(CB_SB_G), qblk(0), qblk(0), ANY_SPEC]
        + [ANY_SPEC] * n_x,
        out_specs=(pl.BlockSpec((DU_SB_QG[1], tq, WIDTH), lambda p, i: (DU_SB_QG[0] // DU_SB_QG[1], i, 0)),
                   full(0), full(0)) + (ANY_SPEC,) * n_x,
        out_shape=(jax.ShapeDtypeStruct(du.shape, du.dtype), jax.ShapeDtypeStruct((s, WIDTH), F32),
                   jax.ShapeDtypeStruct((s, WIDTH), F32)) + tuple(_exchange_out_shapes(x_arrs, x_gather)),
        input_output_aliases={6: 0},
        scratch_shapes=[pltpu.VMEM((s, width), BF16), pltpu.VMEM((s, width), BF16), pltpu.VMEM((2, s, width), BF16),
                        state, state, state, state, pltpu.VMEM((rows, tk), BF16),
                        state, state, state, state] + (_exchange_sems(n_x) if n_x else []),
        compiler_params=_params(("arbitrary", "arbitrary")),
    )(u, u, u, u, o, dys, du, *x_arrs)


def _conv_bwd(u, conv_w, conv_b, dyc, du, name):
    s = u.shape[0]
    t = min(256, s)
    n_tiles = s // t

    def body(xc_ref, gb_ref, gc_ref, cg_ref, w_ref, b_ref, dy_ref, du_in, du_ref, dw_ref, db_ref, zs, ds):
        dxc_ref, dgb_ref, dgc_ref, dcg_ref = (du_ref.at[p] for p in range(4))
        zs[0:CONV_HALO, :] = jnp.zeros((CONV_HALO, LANES), F32)
        zs[CONV_HALO:, :] = gc_ref[...] * xc_ref[...]
        ds[s:, :] = jnp.zeros((CONV_HALO, LANES), F32)
        w0, w1, w2 = w_ref[0:1, :], w_ref[1:2, :], w_ref[2:3, :]
        bias = b_ref[...]

        def first(i, sums):
            t0 = pl.multiple_of(i * t, t)
            z0, z1, z2 = _conv_taps(zs, t0, t)
            pre = w0 * z2 + w1 * z1 + w2 * z0 + bias
            gate = cg_ref[pl.ds(t0, t), :]
            sg = _sigmoid(gate)
            gb = gb_ref[pl.ds(t0, t), :]
            dy = dy_ref[pl.ds(t0, t), :]
            dcg_ref[pl.ds(t0, t), :] = (dy * gb * pre * (sg * (1.0 + gate * (1.0 - sg)))).astype(BF16)
            dgb_ref[pl.ds(t0, t), :] = (dy * pre * (gate * sg)).astype(BF16)
            dc = dy * gb * (gate * sg)
            ds[pl.ds(t0, t), :] = dc
            red = lambda v: jnp.sum(v, axis=0, keepdims=True)
            return (sums[0] + red(dc * z2), sums[1] + red(dc * z1), sums[2] + red(dc * z0), sums[3] + red(dc))

        zrow = jnp.zeros((1, LANES), F32)
        sw0, sw1, sw2, sb = lax.fori_loop(0, n_tiles, first, (zrow, zrow, zrow, zrow))
        dw_ref[0:1, :] = sw0
        dw_ref[1:2, :] = sw1
        dw_ref[2:3, :] = sw2
        db_ref[...] = sb

        def second(i, carry):
            t0 = pl.multiple_of(i * t, t)
            ext = ds[pl.ds(t0, t + CONV_HALO), :]
            n = t + CONV_HALO
            d0 = ext[:t, :]
            d1 = pltpu.roll(ext, n - 1, 0)[:t, :]
            d2 = pltpu.roll(ext, n - 2, 0)[:t, :]
            dz = w2 * d0 + w1 * d1 + w0 * d2
            dgc_ref[pl.ds(t0, t), :] = (dz * xc_ref[pl.ds(t0, t), :]).astype(BF16)
            dxc_ref[pl.ds(t0, t), :] = (dz * gc_ref[pl.ds(t0, t), :]).astype(BF16)
            return carry

        lax.fori_loop(0, n_tiles, second, 0)

    col = lambda base: pl.BlockSpec((s, LANES), lambda j: (0, base + j))
    first, count = DU_CONV
    return _pcall(
        body, name=name, grid=(4,),
        in_specs=[col(CB_CONV_X), col(CB_CONV_GB), col(CB_CONV_GC), col(CB_CONV_G),
                  pl.BlockSpec((3, LANES), lambda j: (0, j)), pl.BlockSpec((1, LANES), lambda j: (0, j)), col(0),
                  ANY_SPEC],
        out_specs=(pl.BlockSpec((count, s, LANES), lambda j: (first // count, 0, j)),
                   pl.BlockSpec((3, LANES), lambda j: (0, j)), pl.BlockSpec((1, LANES), lambda j: (0, j))),
        out_shape=(jax.ShapeDtypeStruct(du.shape, du.dtype),
                   jax.ShapeDtypeStruct((3, WIDTH), F32), jax.ShapeDtypeStruct((1, WIDTH), F32)),
        scratch_shapes=[pltpu.VMEM((CONV_HALO + s, LANES), F32), pltpu.VMEM((s + CONV_HALO, LANES), F32)],
        input_output_aliases={7: 0},
        compiler_params=_params(("arbitrary",)),
    )(u, u, u, u, conv_w, conv_b, dyc, du)


def _pool_bwd(u, pool_w, pool_scale, dyp, du, name):
    s = u.shape[0]
    t = min(256, s)
    n_tiles = s // t

    def body(pv_ref, pg_ref, w_ref, sc_ref, dy_ref, du_in, du_ref, dw_ref, dsc_ref, vs, es, dps):
        dpv_ref, dpg_ref = du_ref.at[0], du_ref.at[1]
        grp = pl.program_id(0)
        vs[0:POOL_HALO, :] = jnp.zeros((POOL_HALO, LANES), F32)
        vs[POOL_HALO:, :] = pv_ref[...]
        es[s:, :] = jnp.zeros((POOL_HALO, LANES), F32)
        wb = w_ref[...].astype(BF16)
        scale = sc_ref[...]

        def first(i, sums):
            dw, dsc = sums
            t0 = pl.multiple_of(i * t, t)
            win, v = _pool_window(vs, t0, t, grp)
            cnt = _pool_count(t0, t, grp)
            pb = (win / cnt - v).astype(BF16)
            mixed = _dot(pb, wb)
            gate = pg_ref[pl.ds(t0, t), :]
            sg = _sigmoid(gate)
            dy = dy_ref[pl.ds(t0, t), :]
            dpg_ref[pl.ds(t0, t), :] = (dy * (mixed * scale) * (sg * (1.0 + gate * (1.0 - sg)))).astype(BF16)
            dms = dy * (gate * sg)
            dsc = dsc + jnp.sum(dms * mixed, axis=0, keepdims=True)
            dmb = (dms * scale).astype(BF16)
            dw = dw + _dot_tn(pb, dmb)
            dpooled = _dot_nt(dmb, wb)
            dps[pl.ds(t0, t), :] = dpooled
            es[pl.ds(t0, t), :] = dpooled / cnt
            return dw, dsc

        dw, dsc = lax.fori_loop(0, n_tiles, first, (jnp.zeros((LANES, LANES), F32), jnp.zeros((1, LANES), F32)))
        dw_ref[...] = dw
        dsc_ref[...] = dsc

        def second(i, carry):
            t0 = pl.multiple_of(i * t, t)
            ext = es[pl.ds(t0, t + POOL_HALO), :]
            n = t + POOL_HALO
            f2 = ext + pltpu.roll(ext, n - 1, 0)
            f4 = f2 + pltpu.roll(f2, n - 2, 0)
            f8 = f4 + pltpu.roll(f4, n - 4, 0)
            f16 = f8 + pltpu.roll(f8, n - 8, 0)
            sel = jnp.where(grp == 0, f2, jnp.where(grp == 1, f4, jnp.where(grp == 2, f8, f16)))
            dpv_ref[pl.ds(t0, t), :] = (sel[:t, :] - dps[pl.ds(t0, t), :]).astype(BF16)
            return carry

        lax.fori_loop(0, n_tiles, second, 0)

    col = lambda base: pl.BlockSpec((s, LANES), lambda g: (0, base + g))
    first, count = DU_POOL
    return _pcall(
        body, name=name, grid=(4,),
        in_specs=[col(CB_POOL_V), col(CB_POOL_G), pl.BlockSpec((None, LANES, LANES), lambda g: (g, 0, 0)),
                  pl.BlockSpec((1, LANES), lambda g: (0, g)), col(0), ANY_SPEC],
        out_specs=(pl.BlockSpec((count, s, LANES), lambda g: (first // count, 0, g)),
                   pl.BlockSpec((None, LANES, LANES), lambda g: (g, 0, 0)),
                   pl.BlockSpec((1, LANES), lambda g: (0, g))),
        out_shape=(jax.ShapeDtypeStruct(du.shape, du.dtype),
                   jax.ShapeDtypeStruct((4, LANES, LANES), F32), jax.ShapeDtypeStruct((1, WIDTH), F32)),
        scratch_shapes=[pltpu.VMEM((POOL_HALO + s, LANES), F32), pltpu.VMEM((s + POOL_HALO, LANES), F32),
                        pltpu.VMEM((s, LANES), F32)],
        input_output_aliases={5: 0},
        compiler_params=_params(("arbitrary",)),
    )(u, u, pool_w, pool_scale, dyp, du)


def _in_proj_bwd_x(du, w_all, x, g_pre, dy, name):
    s = x.shape[0]
    tm = min(1024, s)
    grid = (s // tm, N_DEV)

    def body(dua_ref, dub_ref, w_ref, x_ref, g_ref, dy_ref, dx_ref, dg_ref, acc):
        i, k = pl.program_id(0), pl.program_id(1)

        @pl.when(k == 0)
        def _():
            acc[...] = jnp.zeros_like(acc)

        @pl.when((k == 0) & (i == 0))
        def _():
            dg_ref[...] = jnp.zeros_like(dg_ref)
        acc[...] += _dot_nt(jnp.concatenate([dua_ref[...], dub_ref[...]], axis=1), w_ref[...])

        @pl.when(k == N_DEV - 1)
        def _():
            dh, xv = acc[...], x_ref[...]
            r = lax.rsqrt(jnp.mean(xv * xv, axis=-1, keepdims=True) + RMS_EPS)
            dg_ref[...] += jnp.sum(dh * xv * r, axis=0, keepdims=True)
            a = dh * g_ref[...]
            dx_ref[...] = dy_ref[...] + r * a - xv * (r * r * r) * jnp.mean(a * xv, axis=-1, keepdims=True)

    rows = lambda: pl.BlockSpec((tm, D_MODEL), lambda i, k: (i, 0))
    vec = lambda: pl.BlockSpec((1, D_MODEL), lambda i, k: (0, 0))
    piece = lambda half: pl.BlockSpec((None, tm, WIDTH), lambda i, k: (_du_pieces_of_block(k)[half], i, 0))
    return _pcall(
        body, name=name, grid=grid,
        in_specs=[piece(0), piece(1), pl.BlockSpec((None, D_MODEL, COLS_PER_DEV), lambda i, k: (k, 0, 0)),
                  rows(), vec(), rows()],
        out_specs=(rows(), vec()),
        out_shape=(jax.ShapeDtypeStruct((s, D_MODEL), F32), jax.ShapeDtypeStruct((1, D_MODEL), F32)),
        scratch_shapes=[pltpu.VMEM((tm, D_MODEL), F32)],
        compiler_params=_params(("arbitrary", "arbitrary")),
    )(du, du, w_all, x, g_pre, dy)


def _in_proj_bwd_send(h, du, w_all, x, g_pre, dy, name):
    s = x.shape[0]
    tk = s // N_DEV
    tm = min(1024, s)
    n_i = s // tm
    grid = (N_DEV + n_i, N_DEV)
    last = N_DEV - 1
    offset = lambda row: row

    def body(me_ref, h_ref, duwa_ref, duwb_ref, duxa_ref, duxb_ref, w_ref, x_ref, g_ref, dy_ref,
             dx_ref, dg_ref, recv_ref, acc_w, stage, acc_x, send_sems, recv_sems, local_sem):
        r, k = pl.program_id(0), pl.program_id(1)
        x_, y_, c_ = lax.axis_index("x"), lax.axis_index("y"), lax.axis_index("c")
        me = 4 * x_ + 2 * y_ + c_
        flip = lambda v, bit: 1 - v if bit else v
        peer = lambda n: (flip(x_, (n >> 2) & 1), flip(y_, (n >> 1) & 1), flip(c_, n & 1))

        def out_copy(n, landing=None):
            if n == 0:
                return pltpu.make_async_copy(stage.at[0], recv_ref.at[me], local_sem)
            px, py, pc = peer(n)
            dst = recv_ref.at[me] if landing is None else recv_ref.at[4 * px + 2 * py + pc]
            return pltpu.make_async_remote_copy(
                src_ref=stage.at[n % 2], dst_ref=dst, send_sem=send_sems.at[n], recv_sem=recv_sems.at[n],
                device_id=(px, py, pc), device_id_type=pl.DeviceIdType.MESH)

        def wait_sent(n):
            if n == 0:
                out_copy(0).wait()
            else:
                out_copy(n).wait_send()

        @pl.when(r < N_DEV)
        def _():
            @pl.when(k == 0)
            def _():
                acc_w[...] = jnp.zeros_like(acc_w)
            acc_w[...] += _dot_tn(h_ref[...], jnp.concatenate([duwa_ref[...], duwb_ref[...]], axis=1))

            for row in range(N_DEV):
                @pl.when((k == last) & (r == row))
                def _():
                    if row >= 2:
                        wait_sent(offset(row - 2))
                    stage[row % 2] = acc_w[...].astype(BF16)
                    out_copy(offset(row)).start()

        @pl.when(r >= N_DEV)
        def _():
            @pl.when(k == 0)
            def _():
                acc_x[...] = jnp.zeros_like(acc_x)

            @pl.when((k == 0) & (r == N_DEV))
            def _():
                dg_ref[...] = jnp.zeros_like(dg_ref)
            acc_x[...] += _dot_nt(jnp.concatenate([duxa_ref[...], duxb_ref[...]], axis=1), w_ref[...])

            @pl.when(k == last)
            def _():
                dh, xv = acc_x[...], x_ref[...]
                rs = lax.rsqrt(jnp.mean(xv * xv, axis=-1, keepdims=True) + RMS_EPS)
                dg_ref[...] += jnp.sum(dh * xv * rs, axis=0, keepdims=True)
                a = dh * g_ref[...]
                dx_ref[...] = dy_ref[...] + rs * a - xv * (rs * rs * rs) * jnp.mean(a * xv, axis=-1, keepdims=True)

        @pl.when((r == grid[0] - 1) & (k == last))
        def _():
            wait_sent(offset(N_DEV - 2))
            wait_sent(offset(N_DEV - 1))
            for n in range(1, N_DEV):
                out_copy(n, landing=True).wait_recv()

    in_w = lambda r: r < N_DEV
    row_x = lambda r: jnp.maximum(r - N_DEV, 0)
    rows = lambda: pl.BlockSpec((tm, D_MODEL), lambda r, k, me: (row_x(r), 0))
    vec = lambda: pl.BlockSpec((1, D_MODEL), lambda r, k, me: (0, 0))
    block_w = lambda r, me: jnp.bitwise_xor(me[0], offset(jnp.minimum(r, last)))
    block_x = lambda r, k: jnp.where(in_w(r), 0, k)
    piece_w = lambda half: pl.BlockSpec(
        (None, tk, WIDTH), lambda r, k, me: (_du_pieces_of_block(block_w(r, me))[half], jnp.where(in_w(r), k, last), 0))
    piece_x = lambda half: pl.BlockSpec(
        (None, tm, WIDTH), lambda r, k, me: (_du_pieces_of_block(block_x(r, k))[half], row_x(r), 0))
    grid_spec = pltpu.PrefetchScalarGridSpec(
        num_scalar_prefetch=1, grid=grid,
        in_specs=[pl.BlockSpec((tk, D_MODEL), lambda r, k, me: (jnp.where(in_w(r), k, last), 0)),
                  piece_w(0), piece_w(1), piece_x(0), piece_x(1),
                  pl.BlockSpec((None, D_MODEL, COLS_PER_DEV), lambda r, k, me: (block_x(r, k), 0, 0)),
                  rows(), vec(), rows()],
        out_specs=(rows(), vec(), ANY_SPEC),
        scratch_shapes=[pltpu.VMEM((D_MODEL, COLS_PER_DEV), F32), pltpu.VMEM((2, D_MODEL, COLS_PER_DEV), BF16),
                        pltpu.VMEM((tm, D_MODEL), F32), pltpu.SemaphoreType.DMA((N_DEV,)),
                        pltpu.SemaphoreType.DMA((N_DEV,)), pltpu.SemaphoreType.DMA])
    me = 4 * lax.axis_index("x") + 2 * lax.axis_index("y") + lax.axis_index("c")
    return _pcall(
        body, name=name, grid_spec=grid_spec,
        out_shape=(jax.ShapeDtypeStruct((s, D_MODEL), F32), jax.ShapeDtypeStruct((1, D_MODEL), F32),
                   jax.ShapeDtypeStruct((N_DEV, D_MODEL, COLS_PER_DEV), BF16)),
        compiler_params=_params(("arbitrary", "arbitrary")),
    )(jnp.reshape(me, (1,)).astype(jnp.int32), h, du, du, du, du, w_all, x, g_pre, dy)


def _in_proj_bwd_w(h, du, name):
    s = h.shape[0]
    tk = min(512, s)
    n_k = s // tk

    def body(h_ref, dua_ref, dub_ref, out_ref, acc):
        k = pl.program_id(1)

        @pl.when(k == 0)
        def _():
            acc[...] = jnp.zeros_like(acc)
        acc[...] += _dot_tn(h_ref[...], jnp.concatenate([dua_ref[...], dub_ref[...]], axis=1))

        @pl.when(k == n_k - 1)
        def _():
            out_ref[...] = acc[...].astype(BF16)

    piece = lambda half: pl.BlockSpec((None, tk, WIDTH), lambda j, k: (_du_pieces_of_block(j)[half], k, 0))
    return _pcall(
        body, name=name, grid=(N_DEV, n_k),
        in_specs=[pl.BlockSpec((tk, D_MODEL), lambda j, k: (k, 0)), piece(0), piece(1)],
        out_specs=pl.BlockSpec((None, D_MODEL, COLS_PER_DEV), lambda j, k: (j, 0, 0)),
        out_shape=jax.ShapeDtypeStruct((N_DEV, D_MODEL, COLS_PER_DEV), BF16),
        scratch_shapes=[pltpu.VMEM((D_MODEL, COLS_PER_DEV), F32)],
        compiler_params=_params(("parallel", "arbitrary")),
    )(h, du, du)


def _adamw_math(g, w, m, v):
    m_new = ADAM_B1 * m + (1.0 - ADAM_B1) * g
    v_new = ADAM_B2 * v + (1.0 - ADAM_B2) * (g * g)
    m_hat = m_new / (1.0 - ADAM_B1 ** ADAM_STEP)
    v_hat = v_new / (1.0 - ADAM_B2 ** ADAM_STEP)
    delta = -ADAM_LR * (m_hat / (jnp.sqrt(v_hat) + ADAM_EPS) + ADAM_WD * w)
    return delta, m_new, v_new


def _sum_partials(p_ref):
    total = p_ref[0].astype(F32)
    for d in range(1, N_DEV):
        total = total + p_ref[d].astype(F32)
    return total


def _adamw_layers(parts0, parts1, w, m, v, name):
    _, r, c = w.shape
    tr = min(128, r)
    n_r = r // tr

    def body(p0_ref, p1_ref, w_ref, m_ref, v_ref, g_ref, d_ref, mo_ref, vo_ref):
        layer = pl.program_id(0)

        @pl.when(layer == 0)
        def _():
            g_ref[...] = _sum_partials(p0_ref)

        @pl.when(layer == 1)
        def _():
            g_ref[...] = _sum_partials(p1_ref)
        d_ref[...], mo_ref[...], vo_ref[...] = _adamw_math(g_ref[...], w_ref[...], m_ref[...], v_ref[...])

    part = lambda which: pl.BlockSpec((N_DEV, tr, c), lambda l, i: (0, jnp.where(l == which, i, 0), 0))
    par = lambda: pl.BlockSpec((None, tr, c), lambda l, i: (l, i, 0))
    out = jax.ShapeDtypeStruct(w.shape, F32)
    return _pcall(
        body, name=name, grid=(2, n_r),
        in_specs=[part(0), part(1), par(), par(), par()],
        out_specs=(par(), par(), par(), par()),
        out_shape=(out, out, out, out),
        compiler_params=_params(("arbitrary", "arbitrary")),
    )(parts0, parts1, w, m, v)


def _adamw_small(parts, w, m, v, name):
    def body(p_ref, w_ref, m_ref, v_ref, g_ref, d_ref, mo_ref, vo_ref):
        g = _sum_partials(p_ref)
        g_ref[...] = g
        d_ref[...], mo_ref[...], vo_ref[...] = _adamw_math(g, w_ref[...], m_ref[...], v_ref[...])

    out = jax.ShapeDtypeStruct(w.shape, F32)
    return _pcall(body, name=name, out_shape=(out, out, out, out), compiler_params=_params())(parts, w, m, v)


def _adamw_plain(g, w, m, v, name):
    def body(g_ref, w_ref, m_ref, v_ref, d_ref, mo_ref, vo_ref):
        d_ref[...], mo_ref[...], vo_ref[...] = _adamw_math(g_ref[...], w_ref[...], m_ref[...], v_ref[...])

    out = jax.ShapeDtypeStruct(w.shape, F32)
    return _pcall(body, name=name, out_shape=(out, out, out), compiler_params=_params())(g, w, m, v)


def _rows128(a):
    return a.reshape(-1, LANES)


SMALL_NAMES = ("pre_norm_g", "pool_w", "pool_scale", "conv_w", "conv_b", "post_norm_g")


def kernel(x, pre_norm_g, w_in, pool_w, pool_scale, conv_w, conv_b, w_branch, w_out, post_norm_g, loss_target, m_pre_norm_g, m_w_in, m_pool_w, m_pool_scale, m_conv_w, m_conv_b, m_w_branch, m_w_out, m_post_norm_g, v_pre_norm_g, v_w_in, v_pool_w, v_pool_scale, v_conv_w, v_conv_b, v_w_branch, v_w_out, v_post_norm_g):
    s = x.shape[1]
    me = 4 * lax.axis_index("x") + 2 * lax.axis_index("y") + lax.axis_index("c")
    x0 = x[0]
    target = loss_target[0]
    conv_cols = conv_w.shape[-1]

    conv_w_pad = jnp.pad(conv_w.reshape(2 * 3, conv_cols), ((0, 2), (0, LANES - conv_cols)))
    w_in_all = [None, None]
    w_in_all[0], cw_g = _gather_two_level([w_in[0].astype(BF16), conv_w_pad], "gather_w_in_0")
    conv_w_full = cw_g[:, :6, :conv_cols].reshape(N_DEV, 2, 3, conv_cols).transpose(1, 2, 0, 3).reshape(2, 3, WIDTH)
    later_weights = ([w_in[1].astype(BF16), w_branch.astype(BF16), w_out.astype(BF16)], True)

    saved = []
    xin = x0
    for l in range(2):
        u, h = _in_proj_fwd(xin, pre_norm_g[l:l + 1], w_in_all[l], f"in_proj_fwd_{l}")
        y_pool = _pool_fwd(u, pool_w[l], pool_scale[l:l + 1], f"pool_fwd_{l}")
        y_conv = _conv_fwd(u, conv_w_full[l], conv_b[l:l + 1], f"conv_fwd_{l}")
        if l == 0:
            o_sb, y_sb, w_in_all[1], wb_g, wo_all = _sb_fwd(u, f"sb_fwd_{l}", later_weights)
            wb_all = wb_g.transpose(1, 2, 3, 0, 4).reshape(2, 3, WIDTH, D_MODEL)
        else:
            o_sb, y_sb = _sb_fwd(u, f"sb_fwd_{l}")
        xout, merged, pre = _merge_out_fwd(y_pool, y_conv, y_sb, u, wb_all, wo_all, xin, post_norm_g[l:l + 1], l,
                                           f"merge_out_fwd_{l}")
        saved.append((xin, u, h, y_pool, y_conv, y_sb, o_sb, merged, pre))
        xin = xout

    dy, loss_row = _loss_and_grad(xin, target, "loss")

    small = [None, None]
    recv = [None, None]
    ready = []
    for l in (1, 0):
        xl, u, h, y_pool, y_conv, y_sb, o_sb, merged, pre = saved[l]
        dmerged, dwo, dg_post = _out_proj_bwd(dy, pre, post_norm_g[l:l + 1], merged, wo_all, l, f"out_proj_bwd_{l}")
        du, dyp, dyc, dys, dwb = _merge_bwd(dmerged, y_pool, y_conv, y_sb, u, wb_all, l, f"merge_bwd_{l}")
        dwb = dwb.reshape(N_DEV, 3 * WIDTH, D_MODEL // N_DEV)
        dwo = dwo.reshape(N_DEV, D_MODEL // N_DEV, D_MODEL)
        du, dcw, dcb = _conv_bwd(u, conv_w_full[l], conv_b[l:l + 1], dyc, du, f"conv_bwd_{l}")
        du, dpw, dps = _pool_bwd(u, pool_w[l], pool_scale[l:l + 1], dyp, du, f"pool_bwd_{l}")
        small[l] = dict(pool_w=dpw, pool_scale=dps, conv_w=dcw, conv_b=dcb, post_norm_g=dg_post)
        if l == 1:
            du, dk, dv = _sb_bwd(u, o_sb, dys, du, f"sb_bwd_{l}")
        else:
            small[l]["pre_norm_g"] = jnp.zeros((1, D_MODEL), F32)
            packed = jnp.concatenate(
                [_rows128(jnp.stack([small[0][n], small[1][n]])) for n in SMALL_NAMES]
                + [jnp.pad(loss_row, ((0, 7), (0, 0)))], axis=0)
            du, dk, dv, *got, packed_all = _sb_bwd(
                u, o_sb, dys, du, f"sb_bwd_{l}", (ready + [dwb, dwo, packed], (False,) * 5 + (True,)))
            recv[1] = got[:3]
        du = lax.dynamic_update_slice(du, jnp.stack([dk, dv]).astype(BF16), (DU_SB_KV[0], 0, 0))
        if l == 1:
            dwi = _in_proj_bwd_w(h, du, f"in_proj_bwd_w_{l}")
            ready = [dwi, dwb, dwo]
            dx, dg_pre = _in_proj_bwd_x(du, w_in_all[l], xl, pre_norm_g[l:l + 1], dy, f"in_proj_bwd_x_{l}")
            small[l]["pre_norm_g"] = dg_pre
        else:
            dx, dg_pre, got_dwi = _in_proj_bwd_send(h, du, w_in_all[l], xl, pre_norm_g[l:l + 1], dy,
                                                    f"in_proj_bwd_{l}")
            recv[0] = [got_dwi] + got[3:]
        dy = dx
    grad_x = dy[None]

    (g_pre_0_all,) = _exchange([_rows128(dg_pre)], True, "gather_g_pre_0")
    packed_all = lax.dynamic_update_slice(packed_all, g_pre_0_all, (0, 0, 0))
    sizes = dict(pre_norm_g=16, pool_w=1024, pool_scale=8, conv_w=24, conv_b=8, post_norm_g=16)
    n_rows = sum(sizes.values())
    loss = jnp.sum(packed_all[:, n_rows, 0])

    given = dict(pre_norm_g=(pre_norm_g, m_pre_norm_g, v_pre_norm_g), pool_w=(pool_w, m_pool_w, v_pool_w),
                 pool_scale=(pool_scale, m_pool_scale, v_pool_scale), conv_b=(conv_b, m_conv_b, v_conv_b),
                 post_norm_g=(post_norm_g, m_post_norm_g, v_post_norm_g))
    zeros_cw = jnp.zeros((sizes["conv_w"], LANES), F32)
    pack3 = [jnp.concatenate([zeros_cw if n == "conv_w" else _rows128(given[n][k]) for n in SMALL_NAMES], axis=0)
             for k in range(3)]
    sg, sd, sm, sv = _adamw_small(packed_all[:, :n_rows], pack3[0], pack3[1], pack3[2], "adamw_small")

    def unpack(buf, name, shape):
        start = 0
        for n in SMALL_NAMES:
            if n == name:
                return buf[start:start + sizes[n]].reshape(shape)
            start += sizes[n]

    out = {}
    for n in ("pre_norm_g", "pool_w", "pool_scale", "conv_b", "post_norm_g"):
        shape = given[n][0].shape
        out[n] = tuple(unpack(b, n, shape) for b in (sg, sd, sm, sv))
    g_cw = lax.dynamic_slice_in_dim(unpack(sg, "conv_w", (2, 3, WIDTH)), me * conv_cols, conv_cols, axis=2)
    cw2 = lambda a: a.reshape(6, conv_cols)
    d_cw, m_cw, v_cw = _adamw_plain(cw2(g_cw), cw2(conv_w), cw2(m_conv_w), cw2(v_conv_w), "adamw_conv_w")
    out["conv_w"] = (g_cw,) + tuple(a.reshape(2, 3, conv_cols) for a in (d_cw, m_cw, v_cw))

    out["w_in"] = _adamw_layers(recv[0][0], recv[1][0], w_in, m_w_in, v_w_in, "adamw_w_in")
    cols = D_MODEL // N_DEV
    wb3 = lambda a: a.reshape(2, 3 * WIDTH, cols)
    out["w_branch"] = tuple(a.reshape(2, 3, WIDTH, cols) for a in _adamw_layers(
        recv[0][1], recv[1][1], wb3(w_branch), wb3(m_w_branch), wb3(v_w_branch), "adamw_w_branch"))
    out["w_out"] = _adamw_layers(recv[0][2], recv[1][2], w_out, m_w_out, v_w_out, "adamw_w_out")

    order = ("pre_norm_g", "w_in", "pool_w", "pool_scale", "conv_w", "conv_b", "w_branch", "w_out", "post_norm_g")
    return (loss, grad_x) + tuple(out[n][k] for k in range(4) for n in order)
```

```python
import functools

import jax
import jax.numpy as jnp
from jax import lax
from jax.experimental import pallas as pl
from jax.experimental.pallas import tpu as pltpu

F32 = jnp.float32
BF16 = jnp.bfloat16

N_DEV = 8
D_MODEL = 1024
WIDTH = 512
N_IN = 8192
COLS_PER_DEV = N_IN // N_DEV
HEAD_DIM = 64
LANES = 128
SB_SCALE = HEAD_DIM ** -0.5
LOG2E = 1.4426950408889634
RMS_EPS = 1e-6
POOL_HALO = 16
CONV_HALO = 8
ADAM_LR, ADAM_B1, ADAM_B2, ADAM_EPS, ADAM_WD, ADAM_STEP = 0.001, 0.9, 0.999, 1e-08, 0.01, 10
VMEM_LIMIT = 60 * 1024 * 1024

CB_POOL_V, CB_POOL_G = 0, 4
CB_CONV_X, CB_CONV_GB, CB_CONV_GC, CB_CONV_G = 8, 12, 16, 20
CB_SB_Q, CB_SB_K, CB_SB_V, CB_SB_G = 24, 28, 32, 36
MERGE_BLOCK_1024 = 5

DU_PIECES = 16
DU_MERGE = (0, 6)
DU_POOL = (6, 2)
DU_CONV = (8, 4)
DU_SB_QG = (12, 2)
DU_SB_KV = (14, 2)


def _du_pieces_of_block(j):
    first, second = 2 * (j - 5), 2 * (j - 5) + 1
    for block, (a, b) in enumerate(((6, 7), (8, 9), (10, 11), (12, 14), (15, 13))):
        first = jnp.where(j == block, a, first)
        second = jnp.where(j == block, b, second)
    return first, second


def _pcall(body, **kw):
    return pl.pallas_call(body, **kw)


def _params(sem=None):
    if sem is None:
        return pltpu.CompilerParams(vmem_limit_bytes=VMEM_LIMIT)
    return pltpu.CompilerParams(dimension_semantics=sem, vmem_limit_bytes=VMEM_LIMIT)


def _sigmoid(x):
    return 1.0 / (1.0 + jnp.exp(-x))


def _dot(a, b):
    return jnp.dot(a, b, preferred_element_type=F32)


def _dot_nt(a, b):
    return lax.dot_general(a, b, (((1,), (1,)), ((), ())), preferred_element_type=F32)


def _dot_tn(a, b):
    return lax.dot_general(a, b, (((0,), (0,)), ((), ())), preferred_element_type=F32)


def _split_bf16(x):
    hi = x.astype(BF16)
    lo = (x - hi.astype(F32)).astype(BF16)
    return hi, lo


N_PEER = N_DEV - 1
ANY_SPEC = pl.BlockSpec(memory_space=pl.ANY)


def _exchange_copies(ins, outs, send_sems, recv_sems, local_sems, gather, with_recvs=True):
    n = len(ins)
    gathers = _per_array(gather, n)
    x, y, c = lax.axis_index("x"), lax.axis_index("y"), lax.axis_index("c")
    me = 4 * x + 2 * y + c
    flip = lambda v, bit: 1 - v if bit else v
    local, sends, recvs = [], [], []
    for a in range(n):
        src = ins[a] if gathers[a] else ins[a].at[me]
        local.append(pltpu.make_async_copy(src, outs[a].at[me], local_sems.at[a]))
    for k in range(N_PEER):
        px, py, pc = flip(x, ((k + 1) >> 2) & 1), flip(y, ((k + 1) >> 1) & 1), flip(c, (k + 1) & 1)
        peer_id = 4 * px + 2 * py + pc
        for a in range(n):
            src = ins[a] if gathers[a] else ins[a].at[peer_id]
            common = dict(src_ref=src, send_sem=send_sems.at[a * N_PEER + k], recv_sem=recv_sems.at[a * N_PEER + k],
                          device_id=(px, py, pc), device_id_type=pl.DeviceIdType.MESH)
            sends.append(pltpu.make_async_remote_copy(dst_ref=outs[a].at[me], **common))
            if with_recvs:
                recvs.append(pltpu.make_async_remote_copy(dst_ref=outs[a].at[peer_id], **common))
    return local, sends, recvs


def _exchange_start(ins, outs, sems, gather):
    local, sends, _ = _exchange_copies(ins, outs, *sems, gather, with_recvs=False)
    for cp in local + sends:
        cp.start()


def _exchange_wait(ins, outs, sems, gather):
    local, sends, recvs = _exchange_copies(ins, outs, *sems, gather)
    for cp in recvs:
        cp.wait_recv()
    for cp in sends:
        cp.wait_send()
    for cp in local:
        cp.wait()


def _per_array(gather, n):
    return tuple(gather) if isinstance(gather, (tuple, list)) else (gather,) * n


def _exchange_out_shapes(arrs, gather):
    return [jax.ShapeDtypeStruct((N_DEV,) + tuple(a.shape if g else a.shape[1:]), a.dtype)
            for a, g in zip(arrs, _per_array(gather, len(arrs)))]


def _gather_two_level(arrs, name):
    n = len(arrs)

    def body(*refs):
        ins, outs = refs[:n], refs[n:2 * n]
        send_sems, recv_sems, local_sems = refs[2 * n:]
        x, y, c = lax.axis_index("x"), lax.axis_index("y"), lax.axis_index("c")
        me, sibling = (x, y, c), (x, y, 1 - c)
        chips = [(1 - x, y), (x, 1 - y), (1 - x, 1 - y)]
        slot = lambda dev: 4 * dev[0] + 2 * dev[1] + dev[2]

        def copy(a, k, block, to, src=None):
            return pltpu.make_async_remote_copy(
                src_ref=outs[a].at[slot(block)] if src is None else src, dst_ref=outs[a].at[slot(block)],
                send_sem=send_sems.at[a * N_PEER + k], recv_sem=recv_sems.at[a * N_PEER + k],
                device_id=to, device_id_type=pl.DeviceIdType.MESH)

        local = [pltpu.make_async_copy(ins[a], outs[a].at[slot(me)], local_sems.at[a]) for a in range(n)]
        first = []
        for a in range(n):
            first.append(copy(a, 0, me, sibling, src=ins[a]))
            first += [copy(a, 1 + j, me, (*chip, c), src=ins[a]) for j, chip in enumerate(chips)]
        for cp in local + first:
            cp.start()
        passed = []
        for j, chip in enumerate(chips):
            for a in range(n):
                copy(a, 1 + j, (*chip, c), me).wait_recv()
                passed.append(copy(a, 4 + j, (*chip, c), sibling))
                passed[-1].start()
        for a in range(n):
            copy(a, 0, sibling, me).wait_recv()
        for j, chip in enumerate(chips):
            for a in range(n):
                copy(a, 4 + j, (*chip, 1 - c), me).wait_recv()
        for cp in first + passed:
            cp.wait_send()
        for cp in local:
            cp.wait()

    return _pcall(
        body, name=name,
        out_shape=tuple(_exchange_out_shapes(arrs, True)),
        in_specs=[ANY_SPEC] * n, out_specs=tuple([ANY_SPEC] * n),
        scratch_shapes=_exchange_sems(n),
    )(*arrs)


def _exchange_sems(n):
    return [pltpu.SemaphoreType.DMA((n * N_PEER,)), pltpu.SemaphoreType.DMA((n * N_PEER,)),
            pltpu.SemaphoreType.DMA((n,))]


def _exchange(arrs, gather, name):
    n = len(arrs)

    def body(*refs):
        ins, outs, sems = refs[:n], refs[n:2 * n], refs[2 * n:]
        _exchange_start(ins, outs, sems, gather)
        _exchange_wait(ins, outs, sems, gather)

    return _pcall(
        body, name=name,
        out_shape=tuple(_exchange_out_shapes(arrs, gather)),
        in_specs=[ANY_SPEC] * n, out_specs=tuple([ANY_SPEC] * n),
        scratch_shapes=_exchange_sems(n),
    )(*arrs)


def _in_proj_fwd(x, g, w_all, name):
    s = x.shape[0]
    tm = min(1024, s)

    def body(x_ref, g_ref, w_ref, u_ref, h_ref, hs):
        @pl.when(pl.program_id(1) == 0)
        def _():
            xv = x_ref[...]
            r = lax.rsqrt(jnp.mean(xv * xv, axis=-1, keepdims=True) + RMS_EPS)
            hv = (xv * r * g_ref[...]).astype(BF16)
            hs[...] = hv
            h_ref[...] = hv
        u_ref[...] = _dot(hs[...], w_ref[...])

    return _pcall(
        body, name=name, grid=(s // tm, N_DEV),
        in_specs=[pl.BlockSpec((tm, D_MODEL), lambda i, j: (i, 0)),
                  pl.BlockSpec((1, D_MODEL), lambda i, j: (0, 0)),
                  pl.BlockSpec((None, D_MODEL, COLS_PER_DEV), lambda i, j: (j, 0, 0))],
        out_specs=(pl.BlockSpec((tm, COLS_PER_DEV), lambda i, j: (i, j)),
                   pl.BlockSpec((tm, D_MODEL), lambda i, j: (i, 0))),
        out_shape=(jax.ShapeDtypeStruct((s, N_IN), F32), jax.ShapeDtypeStruct((s, D_MODEL), BF16)),
        scratch_shapes=[pltpu.VMEM((tm, D_MODEL), BF16)],
        compiler_params=_params(("parallel", "arbitrary")),
    )(x, g, w_all)


def _pool_window(vs, t0, t, grp):
    ext = vs[pl.ds(t0, t + POOL_HALO), :]
    s2 = ext + pltpu.roll(ext, 1, 0)
    s4 = s2 + pltpu.roll(s2, 2, 0)
    s8 = s4 + pltpu.roll(s4, 4, 0)
    s16 = s8 + pltpu.roll(s8, 8, 0)
    sel = jnp.where(grp == 0, s2, jnp.where(grp == 1, s4, jnp.where(grp == 2, s8, s16)))
    return sel[POOL_HALO:, :], ext[POOL_HALO:, :]


def _pool_count(t0, t, grp):
    pos = t0 + lax.broadcasted_iota(jnp.int32, (t, 1), 0)
    return jnp.minimum(pos + 1, jnp.left_shift(2, grp)).astype(F32)


def _pool_fwd(u, pool_w, pool_scale, name):
    s = u.shape[0]
    t = min(256, s)

    def body(pv_ref, pg_ref, w_ref, sc_ref, y_ref, vs):
        grp = pl.program_id(0)
        vs[0:POOL_HALO, :] = jnp.zeros((POOL_HALO, LANES), F32)
        vs[POOL_HALO:, :] = pv_ref[...]
        wb = w_ref[...].astype(BF16)
        scale = sc_ref[...]

        def tile(i, carry):
            t0 = pl.multiple_of(i * t, t)
            win, v = _pool_window(vs, t0, t, grp)
            pooled = win / _pool_count(t0, t, grp) - v
            mixed = _dot(pooled.astype(BF16), wb)
            gate = pg_ref[pl.ds(t0, t), :]
            y_ref[pl.ds(t0, t), :] = (mixed * scale * (gate * _sigmoid(gate))).astype(BF16)
            return carry

        lax.fori_loop(0, s // t, tile, 0)

    return _pcall(
        body, name=name, grid=(4,),
        in_specs=[pl.BlockSpec((s, LANES), lambda g: (0, CB_POOL_V + g)),
                  pl.BlockSpec((s, LANES), lambda g: (0, CB_POOL_G + g)),
                  pl.BlockSpec((None, LANES, LANES), lambda g: (g, 0, 0)),
                  pl.BlockSpec((1, LANES), lambda g: (0, g))],
        out_specs=pl.BlockSpec((s, LANES), lambda g: (0, g)),
        out_shape=jax.ShapeDtypeStruct((s, WIDTH), BF16),
        scratch_shapes=[pltpu.VMEM((POOL_HALO + s, LANES), F32)],
        compiler_params=_params(("arbitrary",)),
    )(u, u, pool_w, pool_scale)


def _conv_taps(zs, t0, t):
    ext = zs[pl.ds(t0, t + CONV_HALO), :]
    z0 = ext[CONV_HALO:, :]
    z1 = pltpu.roll(ext, 1, 0)[CONV_HALO:, :]
    z2 = pltpu.roll(ext, 2, 0)[CONV_HALO:, :]
    return z0, z1, z2


def _conv_fwd(u, conv_w, conv_b, name):
    s = u.shape[0]
    t = min(256, s)

    def body(xc_ref, gb_ref, gc_ref, cg_ref, w_ref, b_ref, y_ref, zs):
        zs[0:CONV_HALO, :] = jnp.zeros((CONV_HALO, LANES), F32)
        zs[CONV_HALO:, :] = gc_ref[...] * xc_ref[...]
        w0, w1, w2 = w_ref[0:1, :], w_ref[1:2, :], w_ref[2:3, :]
        bias = b_ref[...]

        def tile(i, carry):
            t0 = pl.multiple_of(i * t, t)
            z0, z1, z2 = _conv_taps(zs, t0, t)
            conv = w0 * z2 + w1 * z1 + w2 * z0
            gate = cg_ref[pl.ds(t0, t), :]
            y = gb_ref[pl.ds(t0, t), :] * (conv + bias) * (gate * _sigmoid(gate))
            y_ref[pl.ds(t0, t), :] = y.astype(BF16)
            return carry

        lax.fori_loop(0, s // t, tile, 0)

    col = lambda base: pl.BlockSpec((s, LANES), lambda j: (0, base + j))
    return _pcall(
        body, name=name, grid=(4,),
        in_specs=[col(CB_CONV_X), col(CB_CONV_GB), col(CB_CONV_GC), col(CB_CONV_G),
                  pl.BlockSpec((3, LANES), lambda j: (0, j)),
                  pl.BlockSpec((1, LANES), lambda j: (0, j))],
        out_specs=pl.BlockSpec((s, LANES), lambda j: (0, j)),
        out_shape=jax.ShapeDtypeStruct((s, WIDTH), BF16),
        scratch_shapes=[pltpu.VMEM((CONV_HALO + s, LANES), F32)],
        compiler_params=_params(("arbitrary",)),
    )(u, u, u, u, conv_w, conv_b)


def _first_head_lanes(rows, width=LANES):
    lane = lax.broadcasted_iota(jnp.int32, (rows, width), 1)
    return jnp.bitwise_and(lane, LANES - 1) < HEAD_DIM


def _stack_heads(x, first):
    zero = jnp.zeros_like(x)
    return jnp.concatenate([jnp.where(first, x, zero), jnp.where(first, zero, x)], axis=0).astype(BF16)


def _causal_mask(tq, tk, copies):
    row = lax.broadcasted_iota(jnp.int32, (tq, tk), 0)
    col = lax.broadcasted_iota(jnp.int32, (tq, tk), 1)
    return jnp.concatenate([col < row] * copies, axis=0)


def _suffix_matrix(tk, inclusive, parts):
    r = lax.broadcasted_iota(jnp.int32, (parts * tk, 2 * tk), 0)
    c = lax.broadcasted_iota(jnp.int32, (parts * tk, 2 * tk), 1)
    r = jnp.bitwise_and(r, tk - 1)
    tri = (r >= c) if inclusive else (r > c)
    return jnp.where(c >= tk, 1.0, jnp.where(tri, 1.0, 0.0)).astype(BF16)


def _suffix_sums(x, m):
    hi, lo = _split_bf16(x)
    return _dot(jnp.concatenate([hi, lo], axis=1), m)


def _sb_log_terms(z, mask, m_strict):
    ls = jnp.minimum(z, 0.0) - jnp.log(1.0 + jnp.exp2(jnp.abs(z) * -LOG2E))
    lk = ls - z
    if mask is not None:
        lk = jnp.where(mask, lk, 0.0)
    return ls, _dot(lk.astype(BF16), m_strict)


SB_PAIRS = 4


def _pair_lanes(a):
    return slice(a * LANES, (a + 1) * LANES)


def _sb_fwd(u, name, xchg=None):
    s = u.shape[0]
    tq = tk = min(128, s)
    pairs = SB_PAIRS
    width = pairs * LANES
    rows = 2 * pairs * tq
    x_arrs, x_gather = xchg if xchg else ((), True)
    n_x = len(x_arrs)
    grid = (4 // pairs, s // tq)

    def body(*refs):
        q_ref, k_ref, v_ref, g_ref = refs[:4]
        x_in, refs = refs[4:4 + n_x], refs[4 + n_x:]
        o_ref, y_ref = refs[:2]
        x_out, refs = refs[2:2 + n_x], refs[2 + n_x:]
        kbf, vst, z_s, ell_s, carry_s = refs[:5]
        x_sems = refs[5:]
        i = pl.program_id(1)
        if n_x:
            @pl.when((pl.program_id(0) == 0) & (i == 0))
            def _():
                _exchange_start(x_in, x_out, x_sems, x_gather)

        @pl.when(i == 0)
        def _():
            kbf[...] = k_ref[...].astype(BF16)
            first_s = _first_head_lanes(s, width)
            vf = v_ref[...]
            vst[0] = jnp.where(first_s, vf, 0.0).astype(BF16)
            vst[1] = jnp.where(first_s, 0.0, vf).astype(BF16)

        first = _first_head_lanes(tq)
        mask = _causal_mask(tq, tk, 2 * pairs)
        m_strict = _suffix_matrix(tk, False, 1)
        qcat = jnp.concatenate([_stack_heads(q_ref[:, _pair_lanes(a)] * SB_SCALE, first) for a in range(pairs)],
                               axis=0)

        def scores(b):
            off = pl.multiple_of(jnp.maximum(b, 0) * tk, tk)
            z_s[...] = jnp.concatenate(
                [_dot_nt(qcat[a * 2 * tq:(a + 1) * 2 * tq], kbf[pl.ds(off, tk), _pair_lanes(a)])
                 for a in range(pairs)], axis=0)

        def log_weights(m):
            ls, cs = _sb_log_terms(z_s[...], m, m_strict)
            carry = carry_s[...]
            ell_s[...] = ls + cs[:, :tk] + carry
            carry_s[...] = carry + cs[:, tk:]

        def consume(b, accs, m):
            w = jnp.exp(ell_s[...])
            if m is not None:
                w = jnp.where(m, w, 0.0)
            wb = w.astype(BF16)
            off = pl.multiple_of(b * tk, tk)
            new = []
            for a in range(pairs):
                r0 = a * 2 * tq
                wcat = jnp.concatenate([wb[r0:r0 + tq], wb[r0 + tq:r0 + 2 * tq]], axis=1)
                vcat = jnp.concatenate([vst[0, pl.ds(off, tk), _pair_lanes(a)], vst[1, pl.ds(off, tk), _pair_lanes(a)]],
                                       axis=0)
                new.append(accs[a] + _dot(wcat, vcat))
            return tuple(new)

        carry_s[...] = jnp.zeros((rows, tk), F32)
        scores(i)
        log_weights(mask)
        scores(i - 1)
        accs = consume(i, tuple(jnp.zeros((tq, LANES), F32) for _ in range(pairs)), mask)
        log_weights(None)
        scores(i - 2)

        def step(n, accs):
            accs = consume(i - n, accs, None)
            log_weights(None)
            scores(i - n - 2)
            return accs

        accs = lax.fori_loop(1, i + 1, step, accs)
        o = jnp.concatenate(accs, axis=1)
        o_ref[...] = o
        gate = g_ref[...]
        y_ref[...] = (o * (gate * _sigmoid(gate))).astype(BF16)
        if n_x:
            @pl.when((pl.program_id(0) == grid[0] - 1) & (i == grid[1] - 1))
            def _():
                _exchange_wait(x_in, x_out, x_sems, x_gather)

    base = lambda cb: cb // pairs
    qblk = lambda cb: pl.BlockSpec((tq, width), lambda p, i: (i, base(cb) + p))
    full = lambda cb: pl.BlockSpec((s, width), lambda p, i: (0, base(cb) + p), pipeline_mode=pl.Buffered(1))
    state = pltpu.VMEM((rows, tk), F32)
    return _pcall(
        body, name=name, grid=grid,
        in_specs=[qblk(CB_SB_Q), full(CB_SB_K), full(CB_SB_V), qblk(CB_SB_G)] + [ANY_SPEC] * n_x,
        out_specs=(qblk(0), qblk(0)) + (ANY_SPEC,) * n_x,
        out_shape=(jax.ShapeDtypeStruct((s, WIDTH), F32), jax.ShapeDtypeStruct((s, WIDTH), BF16))
        + tuple(_exchange_out_shapes(x_arrs, x_gather)),
        scratch_shapes=[pltpu.VMEM((s, width), BF16), pltpu.VMEM((2, s, width), BF16), state, state, state]
        + (_exchange_sems(n_x) if n_x else []),
        compiler_params=_params(("arbitrary", "arbitrary")),
    )(u, u, u, u, *x_arrs)


def _merge_out_fwd(y_pool, y_conv, y_sb, u, wb_all, wo_all, x, g_post, layer, name):
    s = x.shape[0]
    tm = min(512, s)

    def body(yp, yc, ys, m0, m1, m2, wb_ref, wo_ref, x_ref, g_ref, out_ref, merged_ref, pre_ref):
        merged = jnp.zeros((tm, D_MODEL), F32)
        for n, (y_ref, m_ref) in enumerate(((yp, m0), (yc, m1), (ys, m2))):
            merged = merged + _sigmoid(m_ref[...]) * _dot(y_ref[...], wb_ref[n])
        mb = merged.astype(BF16)
        merged_ref[...] = mb
        pre = _dot(mb, wo_ref[...].reshape(D_MODEL, D_MODEL))
        pre_ref[...] = pre
        r = lax.rsqrt(jnp.mean(pre * pre, axis=-1, keepdims=True) + RMS_EPS)
        out_ref[...] = x_ref[...] + pre * r * g_ref[...]

    rows = lambda w: pl.BlockSpec((tm, w), lambda i: (i, 0))
    merge = lambda n: pl.BlockSpec((tm, D_MODEL), lambda i: (i, MERGE_BLOCK_1024 + n))
    return _pcall(
        body, name=name, grid=(s // tm,),
        in_specs=[rows(WIDTH), rows(WIDTH), rows(WIDTH), merge(0), merge(1), merge(2),
                  pl.BlockSpec((None, 3, WIDTH, D_MODEL), lambda i: (layer, 0, 0, 0)),
                  pl.BlockSpec((N_DEV, None, D_MODEL // N_DEV, D_MODEL), lambda i: (0, layer, 0, 0)),
                  rows(D_MODEL), pl.BlockSpec((1, D_MODEL), lambda i: (0, 0))],
        out_specs=(rows(D_MODEL), rows(D_MODEL), rows(D_MODEL)),
        out_shape=(jax.ShapeDtypeStruct((s, D_MODEL), F32), jax.ShapeDtypeStruct((s, D_MODEL), BF16),
                   jax.ShapeDtypeStruct((s, D_MODEL), F32)),
        compiler_params=_params(("arbitrary",)),
    )(y_pool, y_conv, y_sb, u, u, u, wb_all, wo_all, x, g_post)


def _loss_and_grad(y, target, name):
    s = y.shape[0]
    tm = min(512, s)

    def body(y_ref, t_ref, dy_ref, loss_ref, acc):
        i = pl.program_id(0)

        @pl.when(i == 0)
        def _():
            acc[...] = jnp.zeros_like(acc)
        err = y_ref[...] - t_ref[...]
        dy_ref[...] = err / D_MODEL
        acc[...] += jnp.sum(err * err, axis=0, keepdims=True)

        @pl.when(i == pl.num_programs(0) - 1)
        def _():
            total = jnp.sum(acc[...], axis=1, keepdims=True) * (0.5 / D_MODEL)
            loss_ref[...] = jnp.broadcast_to(total, (1, LANES))

    return _pcall(
        body, name=name, grid=(s // tm,),
        in_specs=[pl.BlockSpec((tm, D_MODEL), lambda i: (i, 0)), pl.BlockSpec((tm, D_MODEL), lambda i: (i, 0))],
        out_specs=(pl.BlockSpec((tm, D_MODEL), lambda i: (i, 0)), pl.BlockSpec((1, LANES), lambda i: (0, 0))),
        out_shape=(jax.ShapeDtypeStruct((s, D_MODEL), F32), jax.ShapeDtypeStruct((1, LANES), F32)),
        scratch_shapes=[pltpu.VMEM((1, D_MODEL), F32)],
        compiler_params=_params(("arbitrary",)),
    )(y, target)


def _out_proj_bwd(dy, pre, g_post, merged, wo_all, layer, name):
    s = dy.shape[0]
    tm = min(512, s)
    n_tiles = s // tm

    def body(dy_ref, pre_ref, g_ref, mg_ref, wo_ref, dm_ref, dwo_ref, dg_ref, acc):
        i = pl.program_id(0)

        @pl.when(i == 0)
        def _():
            acc[...] = jnp.zeros_like(acc)
            dg_ref[...] = jnp.zeros_like(dg_ref)
        dyv, pre_v = dy_ref[...], pre_ref[...]
        r = lax.rsqrt(jnp.mean(pre_v * pre_v, axis=-1, keepdims=True) + RMS_EPS)
        dg_ref[...] += jnp.sum(dyv * pre_v * r, axis=0, keepdims=True)
        a = dyv * g_ref[...]
        dpre = r * a - pre_v * (r * r * r) * jnp.mean(a * pre_v, axis=-1, keepdims=True)
        db = dpre.astype(BF16)
        acc[...] += _dot_tn(mg_ref[...], db)
        dm_ref[...] = _dot_nt(db, wo_ref[...].reshape(D_MODEL, D_MODEL))

        @pl.when(i == n_tiles - 1)
        def _():
            dwo_ref[...] = acc[...].astype(BF16)

    rows = lambda: pl.BlockSpec((tm, D_MODEL), lambda i: (i, 0))
    return _pcall(
        body, name=name, grid=(n_tiles,),
        in_specs=[rows(), rows(), pl.BlockSpec((1, D_MODEL), lambda i: (0, 0)), rows(),
                  pl.BlockSpec((N_DEV, None, D_MODEL // N_DEV, D_MODEL), lambda i: (0, layer, 0, 0))],
        out_specs=(rows(), pl.BlockSpec((D_MODEL, D_MODEL), lambda i: (0, 0)),
                   pl.BlockSpec((1, D_MODEL), lambda i: (0, 0))),
        out_shape=(jax.ShapeDtypeStruct((s, D_MODEL), F32), jax.ShapeDtypeStruct((D_MODEL, D_MODEL), BF16),
                   jax.ShapeDtypeStruct((1, D_MODEL), F32)),
        scratch_shapes=[pltpu.VMEM((D_MODEL, D_MODEL), F32)],
        compiler_params=_params(("arbitrary",)),
    )(dy, pre, g_post, merged, wo_all)


def _merge_bwd(dmerged, y_pool, y_conv, y_sb, u, wb_all, layer, name):
    s = dmerged.shape[0]
    tm = min(512, s)
    n_tiles = s // tm
    cols = D_MODEL // N_DEV

    def body(dm_ref, yp, yc, ys, m0, m1, m2, wb_ref, du_ref, dyp, dyc, dys, dwb_ref, acc):
        i = pl.program_id(0)

        @pl.when(i == 0)
        def _():
            acc[...] = jnp.zeros_like(acc)
        dm = dm_ref[...]
        for n, (y_ref, m_ref, dy_ref) in enumerate(((yp, m0, dyp), (yc, m1, dyc), (ys, m2, dys))):
            yv = y_ref[...]
            wb = wb_ref[n]
            gate = _sigmoid(m_ref[...])
            proj = _dot(yv, wb)
            dgate = (dm * proj * gate * (1.0 - gate)).astype(BF16)
            du_ref[2 * n] = dgate[:, :WIDTH]
            du_ref[2 * n + 1] = dgate[:, WIDTH:]
            dproj = (dm * gate).astype(BF16)
            acc[n] += _dot_tn(yv, dproj)
            dy_ref[...] = _dot_nt(dproj, wb)

        @pl.when(i == n_tiles - 1)
        def _():
            for j in range(N_DEV):
                for n in range(3):
                    dwb_ref[j, n] = acc[n, :, j * cols:(j + 1) * cols].astype(BF16)

    rows = lambda w: pl.BlockSpec((tm, w), lambda i: (i, 0))
    merge = lambda n: pl.BlockSpec((tm, D_MODEL), lambda i: (i, MERGE_BLOCK_1024 + n))
    return _pcall(
        body, name=name, grid=(n_tiles,),
        in_specs=[rows(D_MODEL), rows(WIDTH), rows(WIDTH), rows(WIDTH), merge(0), merge(1), merge(2),
                  pl.BlockSpec((None, 3, WIDTH, D_MODEL), lambda i: (layer, 0, 0, 0))],
        out_specs=(pl.BlockSpec((DU_MERGE[1], tm, WIDTH), lambda i: (DU_MERGE[0] // DU_MERGE[1], i, 0)),
                   rows(WIDTH), rows(WIDTH), rows(WIDTH),
                   pl.BlockSpec((N_DEV, 3, WIDTH, cols), lambda i: (0, 0, 0, 0))),
        out_shape=(jax.ShapeDtypeStruct((DU_PIECES, s, WIDTH), BF16),
                   jax.ShapeDtypeStruct((s, WIDTH), F32), jax.ShapeDtypeStruct((s, WIDTH), F32),
                   jax.ShapeDtypeStruct((s, WIDTH), F32),
                   jax.ShapeDtypeStruct((N_DEV, 3, WIDTH, cols), BF16)),
        scratch_shapes=[pltpu.VMEM((3, WIDTH, D_MODEL), F32)],
        compiler_params=_params(("arbitrary",)),
    )(dmerged, y_pool, y_conv, y_sb, u, u, u, wb_all)


def _sb_bwd(u, o, dys, du, name, xchg=None):
    s = u.shape[0]
    tq = tk = min(128, s)
    pairs = SB_PAIRS
    width = pairs * LANES
    assert width == WIDTH
    rows = 2 * pairs * tq
    pair_rows = lambda a: slice(a * 2 * tq, (a + 1) * 2 * tq)

    x_arrs, x_gather = xchg if xchg else ((), True)
    n_x = len(x_arrs)
    grid = (4 // pairs, s // tq)

    def body(*refs):
        q_ref, k_ref, v_ref, g_ref, o_ref, dys_ref = refs[:6]
        x_in, refs = refs[7:7 + n_x], refs[7 + n_x:]
        du_ref, dk_ref, dv_ref = refs[:3]
        dq_ref, dg_ref = du_ref.at[0], du_ref.at[1]
        x_out, refs = refs[3:3 + n_x], refs[3 + n_x:]
        kbf, vbf, kst, z_s, ell_s, ls_s, cl_s, wb_s, g_s, bef_s, cg_s, beta_s = refs[:12]
        x_sems = refs[12:]
        i = pl.program_id(1)
        if n_x:
            @pl.when((pl.program_id(0) == 0) & (i == 0))
            def _():
                _exchange_start(x_in, x_out, x_sems, x_gather)

        @pl.when(i == 0)
        def _():
            dk_ref[...] = jnp.zeros_like(dk_ref)
            dv_ref[...] = jnp.zeros_like(dv_ref)
            kf = k_ref[...]
            kbf[...] = kf.astype(BF16)
            vbf[...] = v_ref[...].astype(BF16)
            first_s = _first_head_lanes(s, width)
            kst[0] = jnp.where(first_s, kf, 0.0).astype(BF16)
            kst[1] = jnp.where(first_s, 0.0, kf).astype(BF16)

        first = _first_head_lanes(tq)
        mask = _causal_mask(tq, tk, 2 * pairs)
        m_strict = _suffix_matrix(tk, False, 1)
        m_incl = _suffix_matrix(tk, True, 2)

        gate = g_ref[...]
        sg = _sigmoid(gate)
        dy = dys_ref[...]
        ov = o_ref[...]
        dg_ref[...] = (dy * ov * (sg * (1.0 + gate * (1.0 - sg)))).astype(BF16)
        do = (dy * (gate * sg)).astype(BF16)
        prod = do.astype(F32) * ov
        row_sum = lambda v: jnp.broadcast_to(jnp.sum(v, axis=1, keepdims=True), (tq, tk))
        dsum, docat, qcat = [], [], []
        for a in range(pairs):
            pa = prod[:, _pair_lanes(a)]
            dsum += [row_sum(jnp.where(first, pa, 0.0)), row_sum(jnp.where(first, 0.0, pa))]
            docat.append(_stack_heads(do[:, _pair_lanes(a)], first))
            qcat.append(_stack_heads(q_ref[:, _pair_lanes(a)] * SB_SCALE, first))
        dsum = jnp.concatenate(dsum, axis=0)

        def block_start(b):
            return pl.multiple_of(jnp.maximum(b, 0) * tk, tk)

        def scores(b):
            off = block_start(b)
            z_s[...] = jnp.concatenate([_dot_nt(qcat[a], kbf[pl.ds(off, tk), _pair_lanes(a)]) for a in range(pairs)],
                                       axis=0)

        def log_weights(m):
            ls, cs = _sb_log_terms(z_s[...], m, m_strict)
            cl = cl_s[...]
            ell_s[...] = ls + cs[:, :tk] + cl
            cl_s[...] = cl + cs[:, tk:]
            ls_s[...] = ls

        def weights(b, m):
            off = block_start(b)
            dwt = jnp.concatenate([_dot_nt(docat[a], vbf[pl.ds(off, tk), _pair_lanes(a)]) for a in range(pairs)],
                                  axis=0)
            w = jnp.exp(ell_s[...])
            if m is not None:
                w = jnp.where(m, w, 0.0)
            wb = w.astype(BF16)
            g = dwt * wb.astype(F32)
            gs = _suffix_sums(g, m_incl)
            cg = cg_s[...]
            beta = jnp.exp(ls_s[...])
            wb_s[...] = wb
            beta_s[...] = beta
            g_s[...] = g * (1.0 - beta)
            bef_s[...] = gs[:, :tk] + cg
            cg_s[...] = cg + gs[:, tk:]

        def grads(b, dqs, m):
            dz = g_s[...] - beta_s[...] * (dsum - bef_s[...])
            if m is not None:
                dz = jnp.where(m, dz, 0.0)
            dzb = dz.astype(BF16)
            wb = wb_s[...]
            off = pl.multiple_of(b * tk, tk)
            new = []
            for a in range(pairs):
                r0 = a * 2 * tq
                kcat = jnp.concatenate([kst[0, pl.ds(off, tk), _pair_lanes(a)], kst[1, pl.ds(off, tk), _pair_lanes(a)]],
                                       axis=0)
                new.append(dqs[a] + _dot(jnp.concatenate([dzb[r0:r0 + tq], dzb[r0 + tq:r0 + 2 * tq]], axis=1), kcat))
                dk_ref[pl.ds(off, tk), _pair_lanes(a)] += _dot_tn(dzb[pair_rows(a)], qcat[a])
                dv_ref[pl.ds(off, tk), _pair_lanes(a)] += _dot_tn(wb[pair_rows(a)], docat[a])
            return tuple(new)

        zero = jnp.zeros((rows, tk), F32)
        cl_s[...] = zero
        cg_s[...] = zero
        scores(i)
        log_weights(mask)
        scores(i - 1)
        weights(i, mask)
        log_weights(None)
        scores(i - 2)
        dqs = grads(i, tuple(jnp.zeros((tq, LANES), F32) for _ in range(pairs)), mask)
        weights(i - 1, None)
        log_weights(None)
        scores(i - 3)

        def step(n, dqs):
            dqs = grads(i - n, dqs, None)
            weights(i - n - 1, None)
            log_weights(None)
            scores(i - n - 3)
            return dqs

        dqs = lax.fori_loop(1, i + 1, step, dqs)
        dq_ref[...] = (jnp.concatenate(dqs, axis=1) * SB_SCALE).astype(BF16)
        if n_x:
            @pl.when((pl.program_id(0) == grid[0] - 1) & (i == grid[1] - 1))
            def _():
                _exchange_wait(x_in, x_out, x_sems, x_gather)

    base = lambda cb: cb // pairs
    qblk = lambda cb: pl.BlockSpec((tq, width), lambda p, i: (i, base(cb) + p))
    full = lambda cb: pl.BlockSpec((s, width), lambda p, i: (0, base(cb) + p), pipeline_mode=pl.Buffered(1))
    state = pltpu.VMEM((rows, tk), F32)
    return _pcall(
        body, name=name, grid=grid,
        in_specs=[qblk(CB_SB_Q), full(CB_SB_K), full(CB_SB_V), qblk(CB_SB_G), qblk(0), qblk(0), ANY_SPEC]
        + [ANY_SPEC] * n_x,
        out_specs=(pl.BlockSpec((DU_SB_QG[1], tq, WIDTH), lambda p, i: (DU_SB_QG[0] // DU_SB_QG[1], i, 0)),
                   full(0), full(0)) + (ANY_SPEC,) * n_x,
        out_shape=(jax.ShapeDtypeStruct(du.shape, du.dtype), jax.ShapeDtypeStruct((s, WIDTH), F32),
                   jax.ShapeDtypeStruct((s, WIDTH), F32)) + tuple(_exchange_out_shapes(x_arrs, x_gather)),
        input_output_aliases={6: 0},
        scratch_shapes=[pltpu.VMEM((s, width), BF16), pltpu.VMEM((s, width), BF16), pltpu.VMEM((2, s, width), BF16),
                        state, state, state, state, pltpu.VMEM((rows, tk), BF16),
                        state, state, state, state] + (_exchange_sems(n_x) if n_x else []),
        compiler_params=_params(("arbitrary", "arbitrary")),
    )(u, u, u, u, o, dys, du, *x_arrs)


def _conv_bwd(u, conv_w, conv_b, dyc, du, name):
    s = u.shape[0]
    t = min(256, s)
    n_tiles = s // t

    def body(xc_ref, gb_ref, gc_ref, cg_ref, w_ref, b_ref, dy_ref, du_in, du_ref, dw_ref, db_ref, zs, ds):
        dxc_ref, dgb_ref, dgc_ref, dcg_ref = (du_ref.at[p] for p in range(4))
        zs[0:CONV_HALO, :] = jnp.zeros((CONV_HALO, LANES), F32)
        zs[CONV_HALO:, :] = gc_ref[...] * xc_ref[...]
        ds[s:, :] = jnp.zeros((CONV_HALO, LANES), F32)
        w0, w1, w2 = w_ref[0:1, :], w_ref[1:2, :], w_ref[2:3, :]
        bias = b_ref[...]

        def first(i, sums):
            t0 = pl.multiple_of(i * t, t)
            z0, z1, z2 = _conv_taps(zs, t0, t)
            pre = w0 * z2 + w1 * z1 + w2 * z0 + bias
            gate = cg_ref[pl.ds(t0, t), :]
            sg = _sigmoid(gate)
            gb = gb_ref[pl.ds(t0, t), :]
            dy = dy_ref[pl.ds(t0, t), :]
            dcg_ref[pl.ds(t0, t), :] = (dy * gb * pre * (sg * (1.0 + gate * (1.0 - sg)))).astype(BF16)
            dgb_ref[pl.ds(t0, t), :] = (dy * pre * (gate * sg)).astype(BF16)
            dc = dy * gb * (gate * sg)
            ds[pl.ds(t0, t), :] = dc
            red = lambda v: jnp.sum(v, axis=0, keepdims=True)
            return (sums[0] + red(dc * z2), sums[1] + red(dc * z1), sums[2] + red(dc * z0), sums[3] + red(dc))

        zrow = jnp.zeros((1, LANES), F32)
        sw0, sw1, sw2, sb = lax.fori_loop(0, n_tiles, first, (zrow, zrow, zrow, zrow))
        dw_ref[0:1, :] = sw0
        dw_ref[1:2, :] = sw1
        dw_ref[2:3, :] = sw2
        db_ref[...] = sb

        def second(i, carry):
            t0 = pl.multiple_of(i * t, t)
            ext = ds[pl.ds(t0, t + CONV_HALO), :]
            n = t + CONV_HALO
            d0 = ext[:t, :]
            d1 = pltpu.roll(ext, n - 1, 0)[:t, :]
            d2 = pltpu.roll(ext, n - 2, 0)[:t, :]
            dz = w2 * d0 + w1 * d1 + w0 * d2
            dgc_ref[pl.ds(t0, t), :] = (dz * xc_ref[pl.ds(t0, t), :]).astype(BF16)
            dxc_ref[pl.ds(t0, t), :] = (dz * gc_ref[pl.ds(t0, t), :]).astype(BF16)
            return carry

        lax.fori_loop(0, n_tiles, second, 0)

    col = lambda base: pl.BlockSpec((s, LANES), lambda j: (0, base + j))
    first, count = DU_CONV
    return _pcall(
        body, name=name, grid=(4,),
        in_specs=[col(CB_CONV_X), col(CB_CONV_GB), col(CB_CONV_GC), col(CB_CONV_G),
                  pl.BlockSpec((3, LANES), lambda j: (0, j)), pl.BlockSpec((1, LANES), lambda j: (0, j)), col(0),
                  ANY_SPEC],
        out_specs=(pl.BlockSpec((count, s, LANES), lambda j: (first // count, 0, j)),
                   pl.BlockSpec((3, LANES), lambda j: (0, j)), pl.BlockSpec((1, LANES), lambda j: (0, j))),
        out_shape=(jax.ShapeDtypeStruct(du.shape, du.dtype),
                   jax.ShapeDtypeStruct((3, WIDTH), F32), jax.ShapeDtypeStruct((1, WIDTH), F32)),
        scratch_shapes=[pltpu.VMEM((CONV_HALO + s, LANES), F32), pltpu.VMEM((s + CONV_HALO, LANES), F32)],
        input_output_aliases={7: 0},
        compiler_params=_params(("arbitrary",)),
    )(u, u, u, u, conv_w, conv_b, dyc, du)


def _pool_bwd(u, pool_w, pool_scale, dyp, du, name):
    s = u.shape[0]
    t = min(256, s)
    n_tiles = s // t

    def body(pv_ref, pg_ref, w_ref, sc_ref, dy_ref, du_in, du_ref, dw_ref, dsc_ref, vs, es, dps):
        dpv_ref, dpg_ref = du_ref.at[0], du_ref.at[1]
        grp = pl.program_id(0)
        vs[0:POOL_HALO, :] = jnp.zeros((POOL_HALO, LANES), F32)
        vs[POOL_HALO:, :] = pv_ref[...]
        es[s:, :] = jnp.zeros((POOL_HALO, LANES), F32)
        wb = w_ref[...].astype(BF16)
        scale = sc_ref[...]

        def first(i, sums):
            dw, dsc = sums
            t0 = pl.multiple_of(i * t, t)
            win, v = _pool_window(vs, t0, t, grp)
            cnt = _pool_count(t0, t, grp)
            pb = (win / cnt - v).astype(BF16)
            mixed = _dot(pb, wb)
            gate = pg_ref[pl.ds(t0, t), :]
            sg = _sigmoid(gate)
            dy = dy_ref[pl.ds(t0, t), :]
            dpg_ref[pl.ds(t0, t), :] = (dy * (mixed * scale) * (sg * (1.0 + gate * (1.0 - sg)))).astype(BF16)
            dms = dy * (gate * sg)
            dsc = dsc + jnp.sum(dms * mixed, axis=0, keepdims=True)
            dmb = (dms * scale).astype(BF16)
            dw = dw + _dot_tn(pb, dmb)
            dpooled = _dot_nt(dmb, wb)
            dps[pl.ds(t0, t), :] = dpooled
            es[pl.ds(t0, t), :] = dpooled / cnt
            return dw, dsc

        dw, dsc = lax.fori_loop(0, n_tiles, first, (jnp.zeros((LANES, LANES), F32), jnp.zeros((1, LANES), F32)))
        dw_ref[...] = dw
        dsc_ref[...] = dsc

        def second(i, carry):
            t0 = pl.multiple_of(i * t, t)
            ext = es[pl.ds(t0, t + POOL_HALO), :]
            n = t + POOL_HALO
            f2 = ext + pltpu.roll(ext, n - 1, 0)
            f4 = f2 + pltpu.roll(f2, n - 2, 0)
            f8 = f4 + pltpu.roll(f4, n - 4, 0)
            f16 = f8 + pltpu.roll(f8, n - 8, 0)
            sel = jnp.where(grp == 0, f2, jnp.where(grp == 1, f4, jnp.where(grp == 2, f8, f16)))
            dpv_ref[pl.ds(t0, t), :] = (sel[:t, :] - dps[pl.ds(t0, t), :]).astype(BF16)
            return carry

        lax.fori_loop(0, n_tiles, second, 0)

    col = lambda base: pl.BlockSpec((s, LANES), lambda g: (0, base + g))
    first, count = DU_POOL
    return _pcall(
        body, name=name, grid=(4,),
        in_specs=[col(CB_POOL_V), col(CB_POOL_G), pl.BlockSpec((None, LANES, LANES), lambda g: (g, 0, 0)),
                  pl.BlockSpec((1, LANES), lambda g: (0, g)), col(0), ANY_SPEC],
        out_specs=(pl.BlockSpec((count, s, LANES), lambda g: (first // count, 0, g)),
                   pl.BlockSpec((None, LANES, LANES), lambda g: (g, 0, 0)),
                   pl.BlockSpec((1, LANES), lambda g: (0, g))),
        out_shape=(jax.ShapeDtypeStruct(du.shape, du.dtype),
                   jax.ShapeDtypeStruct((4, LANES, LANES), F32), jax.ShapeDtypeStruct((1, WIDTH), F32)),
        scratch_shapes=[pltpu.VMEM((POOL_HALO + s, LANES), F32), pltpu.VMEM((s + POOL_HALO, LANES), F32),
                        pltpu.VMEM((s, LANES), F32)],
        input_output_aliases={5: 0},
        compiler_params=_params(("arbitrary",)),
    )(u, u, pool_w, pool_scale, dyp, du)


def _in_proj_bwd_x(du, w_all, x, g_pre, dy, name):
    s = x.shape[0]
    tm = min(1024, s)
    grid = (s // tm, N_DEV)

    def body(dua_ref, dub_ref, w_ref, x_ref, g_ref, dy_ref, dx_ref, dg_ref, acc):
        i, k = pl.program_id(0), pl.program_id(1)

        @pl.when(k == 0)
        def _():
            acc[...] = jnp.zeros_like(acc)

        @pl.when((k == 0) & (i == 0))
        def _():
            dg_ref[...] = jnp.zeros_like(dg_ref)
        acc[...] += _dot_nt(jnp.concatenate([dua_ref[...], dub_ref[...]], axis=1), w_ref[...])

        @pl.when(k == N_DEV - 1)
        def _():
            dh, xv = acc[...], x_ref[...]
            r = lax.rsqrt(jnp.mean(xv * xv, axis=-1, keepdims=True) + RMS_EPS)
            dg_ref[...] += jnp.sum(dh * xv * r, axis=0, keepdims=True)
            a = dh * g_ref[...]
            dx_ref[...] = dy_ref[...] + r * a - xv * (r * r * r) * jnp.mean(a * xv, axis=-1, keepdims=True)

    rows = lambda: pl.BlockSpec((tm, D_MODEL), lambda i, k: (i, 0))
    vec = lambda: pl.BlockSpec((1, D_MODEL), lambda i, k: (0, 0))
    piece = lambda half: pl.BlockSpec((None, tm, WIDTH), lambda i, k: (_du_pieces_of_block(k)[half], i, 0))
    return _pcall(
        body, name=name, grid=grid,
        in_specs=[piece(0), piece(1), pl.BlockSpec((None, D_MODEL, COLS_PER_DEV), lambda i, k: (k, 0, 0)),
                  rows(), vec(), rows()],
        out_specs=(rows(), vec()),
        out_shape=(jax.ShapeDtypeStruct((s, D_MODEL), F32), jax.ShapeDtypeStruct((1, D_MODEL), F32)),
        scratch_shapes=[pltpu.VMEM((tm, D_MODEL), F32)],
        compiler_params=_params(("arbitrary", "arbitrary")),
    )(du, du, w_all, x, g_pre, dy)


def _in_proj_bwd_send(h, du, w_all, x, g_pre, dy, name):
    s = x.shape[0]
    tk = s // N_DEV
    tm = min(1024, s)
    n_i = s // tm
    grid = (N_DEV + n_i, N_DEV)
    last = N_DEV - 1
    offset = lambda row: row

    def body(me_ref, h_ref, duwa_ref, duwb_ref, duxa_ref, duxb_ref, w_ref, x_ref, g_ref, dy_ref,
             dx_ref, dg_ref, recv_ref, part_ref, acc_w, stage, acc_x, send_sems, recv_sems, park_sems):
        r, k = pl.program_id(0), pl.program_id(1)
        x_, y_, c_ = lax.axis_index("x"), lax.axis_index("y"), lax.axis_index("c")
        me = 4 * x_ + 2 * y_ + c_
        flip = lambda v, bit: 1 - v if bit else v
        peer = lambda n: (flip(x_, (n >> 2) & 1), flip(y_, (n >> 1) & 1), flip(c_, n & 1))

        def park(n):
            dst = recv_ref.at[me] if n == 0 else part_ref.at[n]
            return pltpu.make_async_copy(stage.at[n % 2], dst, park_sems.at[n % 2])

        def send(n, landing=False):
            px, py, pc = peer(n)
            dst = recv_ref.at[4 * px + 2 * py + pc] if landing else recv_ref.at[me]
            return pltpu.make_async_remote_copy(
                src_ref=part_ref.at[n], dst_ref=dst, send_sem=send_sems.at[n], recv_sem=recv_sems.at[n],
                device_id=(px, py, pc), device_id_type=pl.DeviceIdType.MESH)

        def parked(n):
            park(n).wait()
            if n >= 1:
                send(n).start()

        @pl.when(r < N_DEV)
        def _():
            @pl.when(k == 0)
            def _():
                acc_w[...] = jnp.zeros_like(acc_w)
            acc_w[...] += _dot_tn(h_ref[...], jnp.concatenate([duwa_ref[...], duwb_ref[...]], axis=1))

            for row in range(N_DEV):
                @pl.when((k == last) & (r == row))
                def _():
                    if row >= 1:
                        parked(row - 1)
                    stage[row % 2] = acc_w[...].astype(BF16)
                    park(row).start()

        @pl.when(r >= N_DEV)
        def _():
            @pl.when(k == 0)
            def _():
                acc_x[...] = jnp.zeros_like(acc_x)

            @pl.when((k == 0) & (r == N_DEV))
            def _():
                dg_ref[...] = jnp.zeros_like(dg_ref)
                parked(last)
            acc_x[...] += _dot_nt(jnp.concatenate([duxa_ref[...], duxb_ref[...]], axis=1), w_ref[...])

            @pl.when(k == last)
            def _():
                dh, xv = acc_x[...], x_ref[...]
                rs = lax.rsqrt(jnp.mean(xv * xv, axis=-1, keepdims=True) + RMS_EPS)
                dg_ref[...] += jnp.sum(dh * xv * rs, axis=0, keepdims=True)
                a = dh * g_ref[...]
                dx_ref[...] = dy_ref[...] + rs * a - xv * (rs * rs * rs) * jnp.mean(a * xv, axis=-1, keepdims=True)

        @pl.when((r == grid[0] - 1) & (k == last))
        def _():
            for n in range(1, N_DEV):
                send(n).wait_send()
            for n in range(1, N_DEV):
                send(n, landing=True).wait_recv()

    in_w = lambda r: r < N_DEV
    row_x = lambda r: jnp.maximum(r - N_DEV, 0)
    rows = lambda: pl.BlockSpec((tm, D_MODEL), lambda r, k, me: (row_x(r), 0))
    vec = lambda: pl.BlockSpec((1, D_MODEL), lambda r, k, me: (0, 0))
    block_w = lambda r, me: jnp.bitwise_xor(me[0], offset(jnp.minimum(r, last)))
    block_x = lambda r, k: jnp.where(in_w(r), 0, k)
    piece_w = lambda half: pl.BlockSpec(
        (None, tk, WIDTH), lambda r, k, me: (_du_pieces_of_block(block_w(r, me))[half], jnp.where(in_w(r), k, last), 0))
    piece_x = lambda half: pl.BlockSpec(
        (None, tm, WIDTH), lambda r, k, me: (_du_pieces_of_block(block_x(r, k))[half], row_x(r), 0))
    grid_spec = pltpu.PrefetchScalarGridSpec(
        num_scalar_prefetch=1, grid=grid,
        in_specs=[pl.BlockSpec((tk, D_MODEL), lambda r, k, me: (jnp.where(in_w(r), k, last), 0)),
                  piece_w(0), piece_w(1), piece_x(0), piece_x(1),
                  pl.BlockSpec((None, D_MODEL, COLS_PER_DEV), lambda r, k, me: (block_x(r, k), 0, 0)),
                  rows(), vec(), rows()],
        out_specs=(rows(), vec(), ANY_SPEC, ANY_SPEC),
        scratch_shapes=[pltpu.VMEM((D_MODEL, COLS_PER_DEV), F32), pltpu.VMEM((2, D_MODEL, COLS_PER_DEV), BF16),
                        pltpu.VMEM((tm, D_MODEL), F32), pltpu.SemaphoreType.DMA((N_DEV,)),
                        pltpu.SemaphoreType.DMA((N_DEV,)), pltpu.SemaphoreType.DMA((2,))])
    me = 4 * lax.axis_index("x") + 2 * lax.axis_index("y") + lax.axis_index("c")
    blocks = jax.ShapeDtypeStruct((N_DEV, D_MODEL, COLS_PER_DEV), BF16)
    dx, dg, received, _ = _pcall(
        body, name=name, grid_spec=grid_spec,
        out_shape=(jax.ShapeDtypeStruct((s, D_MODEL), F32), jax.ShapeDtypeStruct((1, D_MODEL), F32), blocks, blocks),
        compiler_params=_params(("arbitrary", "arbitrary")),
    )(jnp.reshape(me, (1,)).astype(jnp.int32), h, du, du, du, du, w_all, x, g_pre, dy)
    return dx, dg, received


def _in_proj_bwd_w(h, du, name):
    s = h.shape[0]
    tk = min(512, s)
    n_k = s // tk

    def body(h_ref, dua_ref, dub_ref, out_ref, acc):
        k = pl.program_id(1)

        @pl.when(k == 0)
        def _():
            acc[...] = jnp.zeros_like(acc)
        acc[...] += _dot_tn(h_ref[...], jnp.concatenate([dua_ref[...], dub_ref[...]], axis=1))

        @pl.when(k == n_k - 1)
        def _():
            out_ref[...] = acc[...].astype(BF16)

    piece = lambda half: pl.BlockSpec((None, tk, WIDTH), lambda j, k: (_du_pieces_of_block(j)[half], k, 0))
    return _pcall(
        body, name=name, grid=(N_DEV, n_k),
        in_specs=[pl.BlockSpec((tk, D_MODEL), lambda j, k: (k, 0)), piece(0), piece(1)],
        out_specs=pl.BlockSpec((None, D_MODEL, COLS_PER_DEV), lambda j, k: (j, 0, 0)),
        out_shape=jax.ShapeDtypeStruct((N_DEV, D_MODEL, COLS_PER_DEV), BF16),
        scratch_shapes=[pltpu.VMEM((D_MODEL, COLS_PER_DEV), F32)],
        compiler_params=_params(("parallel", "arbitrary")),
    )(h, du, du)


def _adamw_math(g, w, m, v):
    m_new = ADAM_B1 * m + (1.0 - ADAM_B1) * g
    v_new = ADAM_B2 * v + (1.0 - ADAM_B2) * (g * g)
    m_hat = m_new / (1.0 - ADAM_B1 ** ADAM_STEP)
    v_hat = v_new / (1.0 - ADAM_B2 ** ADAM_STEP)
    delta = -ADAM_LR * (m_hat / (jnp.sqrt(v_hat) + ADAM_EPS) + ADAM_WD * w)
    return delta, m_new, v_new


def _sum_partials(p_ref):
    total = p_ref[0].astype(F32)
    for d in range(1, N_DEV):
        total = total + p_ref[d].astype(F32)
    return total


def _adamw_layers(parts0, parts1, w, m, v, name):
    _, r, c = w.shape
    tr = min(128, r)
    n_r = r // tr

    def body(p0_ref, p1_ref, w_ref, m_ref, v_ref, g_ref, d_ref, mo_ref, vo_ref):
        layer = pl.program_id(0)

        @pl.when(layer == 0)
        def _():
            g_ref[...] = _sum_partials(p0_ref)

        @pl.when(layer == 1)
        def _():
            g_ref[...] = _sum_partials(p1_ref)
        d_ref[...], mo_ref[...], vo_ref[...] = _adamw_math(g_ref[...], w_ref[...], m_ref[...], v_ref[...])

    part = lambda which: pl.BlockSpec((N_DEV, tr, c), lambda l, i: (0, jnp.where(l == which, i, 0), 0))
    par = lambda: pl.BlockSpec((None, tr, c), lambda l, i: (l, i, 0))
    out = jax.ShapeDtypeStruct(w.shape, F32)
    return _pcall(
        body, name=name, grid=(2, n_r),
        in_specs=[part(0), part(1), par(), par(), par()],
        out_specs=(par(), par(), par(), par()),
        out_shape=(out, out, out, out),
        compiler_params=_params(("arbitrary", "arbitrary")),
    )(parts0, parts1, w, m, v)


def _adamw_small(parts, w, m, v, name):
    def body(p_ref, w_ref, m_ref, v_ref, g_ref, d_ref, mo_ref, vo_ref):
        g = _sum_partials(p_ref)
        g_ref[...] = g
        d_ref[...], mo_ref[...], vo_ref[...] = _adamw_math(g, w_ref[...], m_ref[...], v_ref[...])

    out = jax.ShapeDtypeStruct(w.shape, F32)
    return _pcall(body, name=name, out_shape=(out, out, out, out), compiler_params=_params())(parts, w, m, v)


def _adamw_plain(g, w, m, v, name):
    def body(g_ref, w_ref, m_ref, v_ref, d_ref, mo_ref, vo_ref):
        d_ref[...], mo_ref[...], vo_ref[...] = _adamw_math(g_ref[...], w_ref[...], m_ref[...], v_ref[...])

    out = jax.ShapeDtypeStruct(w.shape, F32)
    return _pcall(body, name=name, out_shape=(out, out, out), compiler_params=_params())(g, w, m, v)


def _rows128(a):
    return a.reshape(-1, LANES)


SMALL_NAMES = ("pre_norm_g", "pool_w", "pool_scale", "conv_w", "conv_b", "post_norm_g")


def kernel(x, pre_norm_g, w_in, pool_w, pool_scale, conv_w, conv_b, w_branch, w_out, post_norm_g, loss_target, m_pre_norm_g, m_w_in, m_pool_w, m_pool_scale, m_conv_w, m_conv_b, m_w_branch, m_w_out, m_post_norm_g, v_pre_norm_g, v_w_in, v_pool_w, v_pool_scale, v_conv_w, v_conv_b, v_w_branch, v_w_out, v_post_norm_g):
    s = x.shape[1]
    me = 4 * lax.axis_index("x") + 2 * lax.axis_index("y") + lax.axis_index("c")
    x0 = x[0]
    target = loss_target[0]
    conv_cols = conv_w.shape[-1]

    conv_w_pad = jnp.pad(conv_w.reshape(2 * 3, conv_cols), ((0, 2), (0, LANES - conv_cols)))
    w_in_all = [None, None]
    w_in_all[0], cw_g = _gather_two_level([w_in[0].astype(BF16), conv_w_pad], "gather_w_in_0")
    conv_w_full = cw_g[:, :6, :conv_cols].reshape(N_DEV, 2, 3, conv_cols).transpose(1, 2, 0, 3).reshape(2, 3, WIDTH)
    later_weights = ([w_in[1].astype(BF16), w_branch.astype(BF16), w_out.astype(BF16)], True)

    saved = []
    xin = x0
    for l in range(2):
        u, h = _in_proj_fwd(xin, pre_norm_g[l:l + 1], w_in_all[l], f"in_proj_fwd_{l}")
        y_pool = _pool_fwd(u, pool_w[l], pool_scale[l:l + 1], f"pool_fwd_{l}")
        y_conv = _conv_fwd(u, conv_w_full[l], conv_b[l:l + 1], f"conv_fwd_{l}")
        if l == 0:
            o_sb, y_sb, w_in_all[1], wb_g, wo_all = _sb_fwd(u, f"sb_fwd_{l}", later_weights)
            wb_all = wb_g.transpose(1, 2, 3, 0, 4).reshape(2, 3, WIDTH, D_MODEL)
        else:
            o_sb, y_sb = _sb_fwd(u, f"sb_fwd_{l}")
        xout, merged, pre = _merge_out_fwd(y_pool, y_conv, y_sb, u, wb_all, wo_all, xin, post_norm_g[l:l + 1], l,
                                           f"merge_out_fwd_{l}")
        saved.append((xin, u, h, y_pool, y_conv, y_sb, o_sb, merged, pre))
        xin = xout

    dy, loss_row = _loss_and_grad(xin, target, "loss")

    small = [None, None]
    recv = [None, None]
    ready = []
    for l in (1, 0):
        xl, u, h, y_pool, y_conv, y_sb, o_sb, merged, pre = saved[l]
        dmerged, dwo, dg_post = _out_proj_bwd(dy, pre, post_norm_g[l:l + 1], merged, wo_all, l, f"out_proj_bwd_{l}")
        du, dyp, dyc, dys, dwb = _merge_bwd(dmerged, y_pool, y_conv, y_sb, u, wb_all, l, f"merge_bwd_{l}")
        dwb = dwb.reshape(N_DEV, 3 * WIDTH, D_MODEL // N_DEV)
        dwo = dwo.reshape(N_DEV, D_MODEL // N_DEV, D_MODEL)
        du, dcw, dcb = _conv_bwd(u, conv_w_full[l], conv_b[l:l + 1], dyc, du, f"conv_bwd_{l}")
        du, dpw, dps = _pool_bwd(u, pool_w[l], pool_scale[l:l + 1], dyp, du, f"pool_bwd_{l}")
        small[l] = dict(pool_w=dpw, pool_scale=dps, conv_w=dcw, conv_b=dcb, post_norm_g=dg_post)
        if l == 1:
            du, dk, dv = _sb_bwd(u, o_sb, dys, du, f"sb_bwd_{l}")
        else:
            small[l]["pre_norm_g"] = jnp.zeros((1, D_MODEL), F32)
            packed = jnp.concatenate(
                [_rows128(jnp.stack([small[0][n], small[1][n]])) for n in SMALL_NAMES]
                + [jnp.pad(loss_row, ((0, 7), (0, 0)))], axis=0)
            du, dk, dv, *got, packed_all = _sb_bwd(
                u, o_sb, dys, du, f"sb_bwd_{l}", (ready + [dwb, dwo, packed], (False,) * 5 + (True,)))
            recv[1] = got[:3]
        du = lax.dynamic_update_slice(du, jnp.stack([dk, dv]).astype(BF16), (DU_SB_KV[0], 0, 0))
        if l == 1:
            dwi = _in_proj_bwd_w(h, du, f"in_proj_bwd_w_{l}")
            ready = [dwi, dwb, dwo]
            dx, dg_pre = _in_proj_bwd_x(du, w_in_all[l], xl, pre_norm_g[l:l + 1], dy, f"in_proj_bwd_x_{l}")
            small[l]["pre_norm_g"] = dg_pre
        else:
            dx, dg_pre, got_dwi = _in_proj_bwd_send(h, du, w_in_all[l], xl, pre_norm_g[l:l + 1], dy,
                                                    f"in_proj_bwd_{l}")
            recv[0] = [got_dwi] + got[3:]
        dy = dx
    grad_x = dy[None]

    (g_pre_0_all,) = _exchange([_rows128(dg_pre)], True, "gather_g_pre_0")
    packed_all = lax.dynamic_update_slice(packed_all, g_pre_0_all, (0, 0, 0))
    sizes = dict(pre_norm_g=16, pool_w=1024, pool_scale=8, conv_w=24, conv_b=8, post_norm_g=16)
    n_rows = sum(sizes.values())
    loss = jnp.sum(packed_all[:, n_rows, 0])

    given = dict(pre_norm_g=(pre_norm_g, m_pre_norm_g, v_pre_norm_g), pool_w=(pool_w, m_pool_w, v_pool_w),
                 pool_scale=(pool_scale, m_pool_scale, v_pool_scale), conv_b=(conv_b, m_conv_b, v_conv_b),
                 post_norm_g=(post_norm_g, m_post_norm_g, v_post_norm_g))
    zeros_cw = jnp.zeros((sizes["conv_w"], LANES), F32)
    pack3 = [jnp.concatenate([zeros_cw if n == "conv_w" else _rows128(given[n][k]) for n in SMALL_NAMES], axis=0)
             for k in range(3)]
    sg, sd, sm, sv = _adamw_small(packed_all[:, :n_rows], pack3[0], pack3[1], pack3[2], "adamw_small")

    def unpack(buf, name, shape):
        start = 0
        for n in SMALL_NAMES:
            if n == name:
                return buf[start:start + sizes[n]].reshape(shape)
            start += sizes[n]

    out = {}
    for n in ("pre_norm_g", "pool_w", "pool_scale", "conv_b", "post_norm_g"):
        shape = given[n][0].shape
        out[n] = tuple(unpack(b, n, shape) for b in (sg, sd, sm, sv))
    g_cw = lax.dynamic_slice_in_dim(unpack(sg, "conv_w", (2, 3, WIDTH)), me * conv_cols, conv_cols, axis=2)
    cw2 = lambda a: a.reshape(6, conv_cols)
    d_cw, m_cw, v_cw = _adamw_plain(cw2(g_cw), cw2(conv_w), cw2(m_conv_w), cw2(v_conv_w), "adamw_conv_w")
    out["conv_w"] = (g_cw,) + tuple(a.reshape(2, 3, conv_cols) for a in (d_cw, m_cw, v_cw))

    out["w_in"] = _adamw_layers(recv[0][0], recv[1][0], w_in, m_w_in, v_w_in, "adamw_w_in")
    cols = D_MODEL // N_DEV
    wb3 = lambda a: a.reshape(2, 3 * WIDTH, cols)
    out["w_branch"] = tuple(a.reshape(2, 3, WIDTH, cols) for a in _adamw_layers(
        recv[0][1], recv[1][1], wb3(w_branch), wb3(m_w_branch), wb3(v_w_branch), "adamw_w_branch"))
    out["w_out"] = _adamw_layers(recv[0][2], recv[1][2], w_out, m_w_out, v_w_out, "adamw_w_out")

    order = ("pre_norm_g", "w_in", "pool_w", "pool_scale", "conv_w", "conv_b", "w_branch", "w_out", "post_norm_g")
    return (loss, grad_x) + tuple(out[n][k] for k in range(4) for n in order)
```

```python
import functools

import jax
import jax.numpy as jnp
from jax import lax
from jax.experimental import pallas as pl
from jax.experimental.pallas import tpu as pltpu

F32 = jnp.float32
BF16 = jnp.bfloat16

N_DEV = 8
D_MODEL = 1024
WIDTH = 512
N_IN = 8192
COLS_PER_DEV = N_IN // N_DEV
HEAD_DIM = 64
LANES = 128
SB_SCALE = HEAD_DIM ** -0.5
LOG2E = 1.4426950408889634
RMS_EPS = 1e-6
POOL_HALO = 16
CONV_HALO = 8
ADAM_LR, ADAM_B1, ADAM_B2, ADAM_EPS, ADAM_WD, ADAM_STEP = 0.001, 0.9, 0.999, 1e-08, 0.01, 10
VMEM_LIMIT = 60 * 1024 * 1024

CB_POOL_V, CB_POOL_G = 0, 4
CB_CONV_X, CB_CONV_GB, CB_CONV_GC, CB_CONV_G = 8, 12, 16, 20
CB_SB_Q, CB_SB_K, CB_SB_V, CB_SB_G = 24, 28, 32, 36
MERGE_BLOCK_1024 = 5

DU_PIECES = 16
DU_MERGE = (0, 6)
DU_POOL = (6, 2)
DU_CONV = (8, 4)
DU_SB_QG = (12, 2)
DU_SB_KV = (14, 2)


def _du_pieces_of_block(j):
    first, second = 2 * (j - 5), 2 * (j - 5) + 1
    for block, (a, b) in enumerate(((6, 7), (8, 9), (10, 11), (12, 14), (15, 13))):
        first = jnp.where(j == block, a, first)
        second = jnp.where(j == block, b, second)
    return first, second


def _pcall(body, **kw):
    return pl.pallas_call(body, **kw)


def _params(sem=None):
    if sem is None:
        return pltpu.CompilerParams(vmem_limit_bytes=VMEM_LIMIT)
    return pltpu.CompilerParams(dimension_semantics=sem, vmem_limit_bytes=VMEM_LIMIT)


def _sigmoid(x):
    return 1.0 / (1.0 + jnp.exp(-x))


def _dot(a, b):
    return jnp.dot(a, b, preferred_element_type=F32)


def _dot_nt(a, b):
    return lax.dot_general(a, b, (((1,), (1,)), ((), ())), preferred_element_type=F32)


def _dot_tn(a, b):
    return lax.dot_general(a, b, (((0,), (0,)), ((), ())), preferred_element_type=F32)


def _split_bf16(x):
    hi = x.astype(BF16)
    lo = (x - hi.astype(F32)).astype(BF16)
    return hi, lo


N_PEER = N_DEV - 1
ANY_SPEC = pl.BlockSpec(memory_space=pl.ANY)


def _exchange_copies(ins, outs, send_sems, recv_sems, local_sems, gather, with_recvs=True):
    n = len(ins)
    gathers = _per_array(gather, n)
    x, y, c = lax.axis_index("x"), lax.axis_index("y"), lax.axis_index("c")
    me = 4 * x + 2 * y + c
    flip = lambda v, bit: 1 - v if bit else v
    local, sends, recvs = [], [], []
    for a in range(n):
        src = ins[a] if gathers[a] else ins[a].at[me]
        local.append(pltpu.make_async_copy(src, outs[a].at[me], local_sems.at[a]))
    for k in range(N_PEER):
        px, py, pc = flip(x, ((k + 1) >> 2) & 1), flip(y, ((k + 1) >> 1) & 1), flip(c, (k + 1) & 1)
        peer_id = 4 * px + 2 * py + pc
        for a in range(n):
            src = ins[a] if gathers[a] else ins[a].at[peer_id]
            common = dict(src_ref=src, send_sem=send_sems.at[a * N_PEER + k], recv_sem=recv_sems.at[a * N_PEER + k],
                          device_id=(px, py, pc), device_id_type=pl.DeviceIdType.MESH)
            sends.append(pltpu.make_async_remote_copy(dst_ref=outs[a].at[me], **common))
            if with_recvs:
                recvs.append(pltpu.make_async_remote_copy(dst_ref=outs[a].at[peer_id], **common))
    return local, sends, recvs


def _exchange_start(ins, outs, sems, gather):
    local, sends, _ = _exchange_copies(ins, outs, *sems, gather, with_recvs=False)
    for cp in local + sends:
        cp.start()


def _exchange_wait(ins, outs, sems, gather):
    local, sends, recvs = _exchange_copies(ins, outs, *sems, gather)
    for cp in recvs:
        cp.wait_recv()
    for cp in sends:
        cp.wait_send()
    for cp in local:
        cp.wait()


def _per_array(gather, n):
    return tuple(gather) if isinstance(gather, (tuple, list)) else (gather,) * n


def _exchange_out_shapes(arrs, gather):
    return [jax.ShapeDtypeStruct((N_DEV,) + tuple(a.shape if g else a.shape[1:]), a.dtype)
            for a, g in zip(arrs, _per_array(gather, len(arrs)))]


def _gather_two_level(arrs, name):
    n = len(arrs)

    def body(*refs):
        ins, outs = refs[:n], refs[n:2 * n]
        send_sems, recv_sems, local_sems = refs[2 * n:]
        x, y, c = lax.axis_index("x"), lax.axis_index("y"), lax.axis_index("c")
        me, sibling = (x, y, c), (x, y, 1 - c)
        chips = [(1 - x, y), (x, 1 - y), (1 - x, 1 - y)]
        slot = lambda dev: 4 * dev[0] + 2 * dev[1] + dev[2]

        def copy(a, k, block, to, src=None):
            return pltpu.make_async_remote_copy(
                src_ref=outs[a].at[slot(block)] if src is None else src, dst_ref=outs[a].at[slot(block)],
                send_sem=send_sems.at[a * N_PEER + k], recv_sem=recv_sems.at[a * N_PEER + k],
                device_id=to, device_id_type=pl.DeviceIdType.MESH)

        local = [pltpu.make_async_copy(ins[a], outs[a].at[slot(me)], local_sems.at[a]) for a in range(n)]
        first = []
        for a in range(n):
            first.append(copy(a, 0, me, sibling, src=ins[a]))
            first += [copy(a, 1 + j, me, (*chip, c), src=ins[a]) for j, chip in enumerate(chips)]
        for cp in local + first:
            cp.start()
        passed = []
        for j, chip in enumerate(chips):
            for a in range(n):
                copy(a, 1 + j, (*chip, c), me).wait_recv()
                passed.append(copy(a, 4 + j, (*chip, c), sibling))
                passed[-1].start()
        for a in range(n):
            copy(a, 0, sibling, me).wait_recv()
        for j, chip in enumerate(chips):
            for a in range(n):
                copy(a, 4 + j, (*chip, 1 - c), me).wait_recv()
        for cp in first + passed:
            cp.wait_send()
        for cp in local:
            cp.wait()

    return _pcall(
        body, name=name,
        out_shape=tuple(_exchange_out_shapes(arrs, True)),
        in_specs=[ANY_SPEC] * n, out_specs=tuple([ANY_SPEC] * n),
        scratch_shapes=_exchange_sems(n),
    )(*arrs)


def _exchange_sems(n):
    return [pltpu.SemaphoreType.DMA((n * N_PEER,)), pltpu.SemaphoreType.DMA((n * N_PEER,)),
            pltpu.SemaphoreType.DMA((n,))]


def _exchange(arrs, gather, name):
    n = len(arrs)

    def body(*refs):
        ins, outs, sems = refs[:n], refs[n:2 * n], refs[2 * n:]
        _exchange_start(ins, outs, sems, gather)
        _exchange_wait(ins, outs, sems, gather)

    return _pcall(
        body, name=name,
        out_shape=tuple(_exchange_out_shapes(arrs, gather)),
        in_specs=[ANY_SPEC] * n, out_specs=tuple([ANY_SPEC] * n),
        scratch_shapes=_exchange_sems(n),
    )(*arrs)


def _in_proj_fwd(x, g, w_all, name):
    s = x.shape[0]
    tm = min(1024, s)

    def body(x_ref, g_ref, w_ref, u_ref, h_ref, hs):
        @pl.when(pl.program_id(1) == 0)
        def _():
            xv = x_ref[...]
            r = lax.rsqrt(jnp.mean(xv * xv, axis=-1, keepdims=True) + RMS_EPS)
            hv = (xv * r * g_ref[...]).astype(BF16)
            hs[...] = hv
            h_ref[...] = hv
        u_ref[...] = _dot(hs[...], w_ref[...])

    return _pcall(
        body, name=name, grid=(s // tm, N_DEV),
        in_specs=[pl.BlockSpec((tm, D_MODEL), lambda i, j: (i, 0)),
                  pl.BlockSpec((1, D_MODEL), lambda i, j: (0, 0)),
                  pl.BlockSpec((None, D_MODEL, COLS_PER_DEV), lambda i, j: (j, 0, 0))],
        out_specs=(pl.BlockSpec((tm, COLS_PER_DEV), lambda i, j: (i, j)),
                   pl.BlockSpec((tm, D_MODEL), lambda i, j: (i, 0))),
        out_shape=(jax.ShapeDtypeStruct((s, N_IN), F32), jax.ShapeDtypeStruct((s, D_MODEL), BF16)),
        scratch_shapes=[pltpu.VMEM((tm, D_MODEL), BF16)],
        compiler_params=_params(("parallel", "arbitrary")),
    )(x, g, w_all)


def _pool_window(vs, t0, t, grp):
    ext = vs[pl.ds(t0, t + POOL_HALO), :]
    s2 = ext + pltpu.roll(ext, 1, 0)
    s4 = s2 + pltpu.roll(s2, 2, 0)
    s8 = s4 + pltpu.roll(s4, 4, 0)
    s16 = s8 + pltpu.roll(s8, 8, 0)
    sel = jnp.where(grp == 0, s2, jnp.where(grp == 1, s4, jnp.where(grp == 2, s8, s16)))
    return sel[POOL_HALO:, :], ext[POOL_HALO:, :]


def _pool_count(t0, t, grp):
    pos = t0 + lax.broadcasted_iota(jnp.int32, (t, 1), 0)
    return jnp.minimum(pos + 1, jnp.left_shift(2, grp)).astype(F32)


def _pool_fwd(u, pool_w, pool_scale, name):
    s = u.shape[0]
    t = min(256, s)

    def body(pv_ref, pg_ref, w_ref, sc_ref, y_ref, vs):
        grp = pl.program_id(0)
        vs[0:POOL_HALO, :] = jnp.zeros((POOL_HALO, LANES), F32)
        vs[POOL_HALO:, :] = pv_ref[...]
        wb = w_ref[...].astype(BF16)
        scale = sc_ref[...]

        def tile(i, carry):
            t0 = pl.multiple_of(i * t, t)
            win, v = _pool_window(vs, t0, t, grp)
            pooled = win / _pool_count(t0, t, grp) - v
            mixed = _dot(pooled.astype(BF16), wb)
            gate = pg_ref[pl.ds(t0, t), :]
            y_ref[pl.ds(t0, t), :] = (mixed * scale * (gate * _sigmoid(gate))).astype(BF16)
            return carry

        lax.fori_loop(0, s // t, tile, 0)

    return _pcall(
        body, name=name, grid=(4,),
        in_specs=[pl.BlockSpec((s, LANES), lambda g: (0, CB_POOL_V + g)),
                  pl.BlockSpec((s, LANES), lambda g: (0, CB_POOL_G + g)),
                  pl.BlockSpec((None, LANES, LANES), lambda g: (g, 0, 0)),
                  pl.BlockSpec((1, LANES), lambda g: (0, g))],
        out_specs=pl.BlockSpec((s, LANES), lambda g: (0, g)),
        out_shape=jax.ShapeDtypeStruct((s, WIDTH), BF16),
        scratch_shapes=[pltpu.VMEM((POOL_HALO + s, LANES), F32)],
        compiler_params=_params(("arbitrary",)),
    )(u, u, pool_w, pool_scale)


def _conv_taps(zs, t0, t):
    ext = zs[pl.ds(t0, t + CONV_HALO), :]
    z0 = ext[CONV_HALO:, :]
    z1 = pltpu.roll(ext, 1, 0)[CONV_HALO:, :]
    z2 = pltpu.roll(ext, 2, 0)[CONV_HALO:, :]
    return z0, z1, z2


def _conv_fwd(u, conv_w, conv_b, name):
    s = u.shape[0]
    t = min(256, s)

    def body(xc_ref, gb_ref, gc_ref, cg_ref, w_ref, b_ref, y_ref, zs):
        zs[0:CONV_HALO, :] = jnp.zeros((CONV_HALO, LANES), F32)
        zs[CONV_HALO:, :] = gc_ref[...] * xc_ref[...]
        w0, w1, w2 = w_ref[0:1, :], w_ref[1:2, :], w_ref[2:3, :]
        bias = b_ref[...]

        def tile(i, carry):
            t0 = pl.multiple_of(i * t, t)
            z0, z1, z2 = _conv_taps(zs, t0, t)
            conv = w0 * z2 + w1 * z1 + w2 * z0
            gate = cg_ref[pl.ds(t0, t), :]
            y = gb_ref[pl.ds(t0, t), :] * (conv + bias) * (gate * _sigmoid(gate))
            y_ref[pl.ds(t0, t), :] = y.astype(BF16)
            return carry

        lax.fori_loop(0, s // t, tile, 0)

    col = lambda base: pl.BlockSpec((s, LANES), lambda j: (0, base + j))
    return _pcall(
        body, name=name, grid=(4,),
        in_specs=[col(CB_CONV_X), col(CB_CONV_GB), col(CB_CONV_GC), col(CB_CONV_G),
                  pl.BlockSpec((3, LANES), lambda j: (0, j)),
                  pl.BlockSpec((1, LANES), lambda j: (0, j))],
        out_specs=pl.BlockSpec((s, LANES), lambda j: (0, j)),
        out_shape=jax.ShapeDtypeStruct((s, WIDTH), BF16),
        scratch_shapes=[pltpu.VMEM((CONV_HALO + s, LANES), F32)],
        compiler_params=_params(("arbitrary",)),
    )(u, u, u, u, conv_w, conv_b)


def _first_head_lanes(rows, width=LANES):
    lane = lax.broadcasted_iota(jnp.int32, (rows, width), 1)
    return jnp.bitwise_and(lane, LANES - 1) < HEAD_DIM


def _stack_heads(x, first):
    zero = jnp.zeros_like(x)
    return jnp.concatenate([jnp.where(first, x, zero), jnp.where(first, zero, x)], axis=0).astype(BF16)


def _causal_mask(tq, tk, copies):
    row = lax.broadcasted_iota(jnp.int32, (tq, tk), 0)
    col = lax.broadcasted_iota(jnp.int32, (tq, tk), 1)
    return jnp.concatenate([col < row] * copies, axis=0)


def _suffix_matrix(tk, inclusive, parts):
    r = lax.broadcasted_iota(jnp.int32, (parts * tk, 2 * tk), 0)
    c = lax.broadcasted_iota(jnp.int32, (parts * tk, 2 * tk), 1)
    r = jnp.bitwise_and(r, tk - 1)
    tri = (r >= c) if inclusive else (r > c)
    return jnp.where(c >= tk, 1.0, jnp.where(tri, 1.0, 0.0)).astype(BF16)


def _suffix_sums(x, m):
    hi, lo = _split_bf16(x)
    return _dot(jnp.concatenate([hi, lo], axis=1), m)


def _sb_log_terms(z, mask, m_strict):
    ls = jnp.minimum(z, 0.0) - jnp.log(1.0 + jnp.exp2(jnp.abs(z) * -LOG2E))
    lk = ls - z
    if mask is not None:
        lk = jnp.where(mask, lk, 0.0)
    return ls, _dot(lk.astype(BF16), m_strict)


SB_PAIRS = 4


def _pair_lanes(a):
    return slice(a * LANES, (a + 1) * LANES)


def _sb_fwd(u, name, xchg=None):
    s = u.shape[0]
    tq = tk = min(128, s)
    pairs = SB_PAIRS
    width = pairs * LANES
    rows = 2 * pairs * tq
    x_arrs, x_gather = xchg if xchg else ((), True)
    n_x = len(x_arrs)
    grid = (4 // pairs, s // tq)

    def body(*refs):
        q_ref, k_ref, v_ref, g_ref = refs[:4]
        x_in, refs = refs[4:4 + n_x], refs[4 + n_x:]
        o_ref, y_ref = refs[:2]
        x_out, refs = refs[2:2 + n_x], refs[2 + n_x:]
        kbf, vst, z_s, ell_s, carry_s = refs[:5]
        x_sems = refs[5:]
        i = pl.program_id(1)
        if n_x:
            @pl.when((pl.program_id(0) == 0) & (i == 0))
            def _():
                _exchange_start(x_in, x_out, x_sems, x_gather)

        @pl.when(i == 0)
        def _():
            kbf[...] = k_ref[...].astype(BF16)
            first_s = _first_head_lanes(s, width)
            vf = v_ref[...]
            vst[0] = jnp.where(first_s, vf, 0.0).astype(BF16)
            vst[1] = jnp.where(first_s, 0.0, vf).astype(BF16)

        first = _first_head_lanes(tq)
        mask = _causal_mask(tq, tk, 2 * pairs)
        m_strict = _suffix_matrix(tk, False, 1)
        qcat = jnp.concatenate([_stack_heads(q_ref[:, _pair_lanes(a)] * SB_SCALE, first) for a in range(pairs)],
                               axis=0)

        def scores(b):
            off = pl.multiple_of(jnp.maximum(b, 0) * tk, tk)
            z_s[...] = jnp.concatenate(
                [_dot_nt(qcat[a * 2 * tq:(a + 1) * 2 * tq], kbf[pl.ds(off, tk), _pair_lanes(a)])
                 for a in range(pairs)], axis=0)

        def log_weights(m):
            ls, cs = _sb_log_terms(z_s[...], m, m_strict)
            carry = carry_s[...]
            ell_s[...] = ls + cs[:, :tk] + carry
            carry_s[...] = carry + cs[:, tk:]

        def consume(b, accs, m):
            w = jnp.exp(ell_s[...])
            if m is not None:
                w = jnp.where(m, w, 0.0)
            wb = w.astype(BF16)
            off = pl.multiple_of(b * tk, tk)
            new = []
            for a in range(pairs):
                r0 = a * 2 * tq
                wcat = jnp.concatenate([wb[r0:r0 + tq], wb[r0 + tq:r0 + 2 * tq]], axis=1)
                vcat = jnp.concatenate([vst[0, pl.ds(off, tk), _pair_lanes(a)], vst[1, pl.ds(off, tk), _pair_lanes(a)]],
                                       axis=0)
                new.append(accs[a] + _dot(wcat, vcat))
            return tuple(new)

        carry_s[...] = jnp.zeros((rows, tk), F32)
        scores(i)
        log_weights(mask)
        scores(i - 1)
        accs = consume(i, tuple(jnp.zeros((tq, LANES), F32) for _ in range(pairs)), mask)
        log_weights(None)
        scores(i - 2)

        def step(n, accs):
            accs = consume(i - n, accs, None)
            log_weights(None)
            scores(i - n - 2)
            return accs

        accs = lax.fori_loop(1, i + 1, step, accs)
        o = jnp.concatenate(accs, axis=1)
        o_ref[...] = o
        gate = g_ref[...]
        y_ref[...] = (o * (gate * _sigmoid(gate))).astype(BF16)
        if n_x:
            @pl.when((pl.program_id(0) == grid[0] - 1) & (i == grid[1] - 1))
            def _():
                _exchange_wait(x_in, x_out, x_sems, x_gather)

    base = lambda cb: cb // pairs
    qblk = lambda cb: pl.BlockSpec((tq, width), lambda p, i: (i, base(cb) + p))
    full = lambda cb: pl.BlockSpec((s, width), lambda p, i: (0, base(cb) + p), pipeline_mode=pl.Buffered(1))
    state = pltpu.VMEM((rows, tk), F32)
    return _pcall(
        body, name=name, grid=grid,
        in_specs=[qblk(CB_SB_Q), full(CB_SB_K), full(CB_SB_V), qblk(CB_SB_G)] + [ANY_SPEC] * n_x,
        out_specs=(qblk(0), qblk(0)) + (ANY_SPEC,) * n_x,
        out_shape=(jax.ShapeDtypeStruct((s, WIDTH), F32), jax.ShapeDtypeStruct((s, WIDTH), BF16))
        + tuple(_exchange_out_shapes(x_arrs, x_gather)),
        scratch_shapes=[pltpu.VMEM((s, width), BF16), pltpu.VMEM((2, s, width), BF16), state, state, state]
        + (_exchange_sems(n_x) if n_x else []),
        compiler_params=_params(("arbitrary", "arbitrary")),
    )(u, u, u, u, *x_arrs)


def _merge_out_fwd(y_pool, y_conv, y_sb, u, wb_all, wo_all, x, g_post, layer, name):
    s = x.shape[0]
    tm = min(512, s)

    def body(yp, yc, ys, m0, m1, m2, wb_ref, wo_ref, x_ref, g_ref, out_ref, merged_ref, pre_ref):
        merged = jnp.zeros((tm, D_MODEL), F32)
        for n, (y_ref, m_ref) in enumerate(((yp, m0), (yc, m1), (ys, m2))):
            merged = merged + _sigmoid(m_ref[...]) * _dot(y_ref[...], wb_ref[n])
        mb = merged.astype(BF16)
        merged_ref[...] = mb
        pre = _dot(mb, wo_ref[...].reshape(D_MODEL, D_MODEL))
        pre_ref[...] = pre
        r = lax.rsqrt(jnp.mean(pre * pre, axis=-1, keepdims=True) + RMS_EPS)
        out_ref[...] = x_ref[...] + pre * r * g_ref[...]

    rows = lambda w: pl.BlockSpec((tm, w), lambda i: (i, 0))
    merge = lambda n: pl.BlockSpec((tm, D_MODEL), lambda i: (i, MERGE_BLOCK_1024 + n))
    return _pcall(
        body, name=name, grid=(s // tm,),
        in_specs=[rows(WIDTH), rows(WIDTH), rows(WIDTH), merge(0), merge(1), merge(2),
                  pl.BlockSpec((None, 3, WIDTH, D_MODEL), lambda i: (layer, 0, 0, 0)),
                  pl.BlockSpec((N_DEV, None, D_MODEL // N_DEV, D_MODEL), lambda i: (0, layer, 0, 0)),
                  rows(D_MODEL), pl.BlockSpec((1, D_MODEL), lambda i: (0, 0))],
        out_specs=(rows(D_MODEL), rows(D_MODEL), rows(D_MODEL)),
        out_shape=(jax.ShapeDtypeStruct((s, D_MODEL), F32), jax.ShapeDtypeStruct((s, D_MODEL), BF16),
                   jax.ShapeDtypeStruct((s, D_MODEL), F32)),
        compiler_params=_params(("arbitrary",)),
    )(y_pool, y_conv, y_sb, u, u, u, wb_all, wo_all, x, g_post)


def _loss_and_grad(y, target, name):
    s = y.shape[0]
    tm = min(512, s)

    def body(y_ref, t_ref, dy_ref, loss_ref, acc):
        i = pl.program_id(0)

        @pl.when(i == 0)
        def _():
            acc[...] = jnp.zeros_like(acc)
        err = y_ref[...] - t_ref[...]
        dy_ref[...] = err / D_MODEL
        acc[...] += jnp.sum(err * err, axis=0, keepdims=True)

        @pl.when(i == pl.num_programs(0) - 1)
        def _():
            total = jnp.sum(acc[...], axis=1, keepdims=True) * (0.5 / D_MODEL)
            loss_ref[...] = jnp.broadcast_to(total, (1, LANES))

    return _pcall(
        body, name=name, grid=(s // tm,),
        in_specs=[pl.BlockSpec((tm, D_MODEL), lambda i: (i, 0)), pl.BlockSpec((tm, D_MODEL), lambda i: (i, 0))],
        out_specs=(pl.BlockSpec((tm, D_MODEL), lambda i: (i, 0)), pl.BlockSpec((1, LANES), lambda i: (0, 0))),
        out_shape=(jax.ShapeDtypeStruct((s, D_MODEL), F32), jax.ShapeDtypeStruct((1, LANES), F32)),
        scratch_shapes=[pltpu.VMEM((1, D_MODEL), F32)],
        compiler_params=_params(("arbitrary",)),
    )(y, target)


def _out_proj_bwd(dy, pre, g_post, merged, wo_all, layer, name):
    s = dy.shape[0]
    tm = min(512, s)
    n_tiles = s // tm

    def body(dy_ref, pre_ref, g_ref, mg_ref, wo_ref, dm_ref, dwo_ref, dg_ref, acc):
        i = pl.program_id(0)

        @pl.when(i == 0)
        def _():
            acc[...] = jnp.zeros_like(acc)
            dg_ref[...] = jnp.zeros_like(dg_ref)
        dyv, pre_v = dy_ref[...], pre_ref[...]
        r = lax.rsqrt(jnp.mean(pre_v * pre_v, axis=-1, keepdims=True) + RMS_EPS)
        dg_ref[...] += jnp.sum(dyv * pre_v * r, axis=0, keepdims=True)
        a = dyv * g_ref[...]
        dpre = r * a - pre_v * (r * r * r) * jnp.mean(a * pre_v, axis=-1, keepdims=True)
        db = dpre.astype(BF16)
        acc[...] += _dot_tn(mg_ref[...], db)
        dm_ref[...] = _dot_nt(db, wo_ref[...].reshape(D_MODEL, D_MODEL))

        @pl.when(i == n_tiles - 1)
        def _():
            dwo_ref[...] = acc[...].astype(BF16)

    rows = lambda: pl.BlockSpec((tm, D_MODEL), lambda i: (i, 0))
    return _pcall(
        body, name=name, grid=(n_tiles,),
        in_specs=[rows(), rows(), pl.BlockSpec((1, D_MODEL), lambda i: (0, 0)), rows(),
                  pl.BlockSpec((N_DEV, None, D_MODEL // N_DEV, D_MODEL), lambda i: (0, layer, 0, 0))],
        out_specs=(rows(), pl.BlockSpec((D_MODEL, D_MODEL), lambda i: (0, 0)),
                   pl.BlockSpec((1, D_MODEL), lambda i: (0, 0))),
        out_shape=(jax.ShapeDtypeStruct((s, D_MODEL), F32), jax.ShapeDtypeStruct((D_MODEL, D_MODEL), BF16),
                   jax.ShapeDtypeStruct((1, D_MODEL), F32)),
        scratch_shapes=[pltpu.VMEM((D_MODEL, D_MODEL), F32)],
        compiler_params=_params(("arbitrary",)),
    )(dy, pre, g_post, merged, wo_all)


def _merge_bwd(dmerged, y_pool, y_conv, y_sb, u, wb_all, layer, name):
    s = dmerged.shape[0]
    tm = min(512, s)
    n_tiles = s // tm
    cols = D_MODEL // N_DEV

    def body(dm_ref, yp, yc, ys, m0, m1, m2, wb_ref, du_ref, dyp, dyc, dys, dwb_ref, acc):
        i = pl.program_id(0)

        @pl.when(i == 0)
        def _():
            acc[...] = jnp.zeros_like(acc)
        dm = dm_ref[...]
        for n, (y_ref, m_ref, dy_ref) in enumerate(((yp, m0, dyp), (yc, m1, dyc), (ys, m2, dys))):
            yv = y_ref[...]
            wb = wb_ref[n]
            gate = _sigmoid(m_ref[...])
            proj = _dot(yv, wb)
            dgate = (dm * proj * gate * (1.0 - gate)).astype(BF16)
            du_ref[2 * n] = dgate[:, :WIDTH]
            du_ref[2 * n + 1] = dgate[:, WIDTH:]
            dproj = (dm * gate).astype(BF16)
            acc[n] += _dot_tn(yv, dproj)
            dy_ref[...] = _dot_nt(dproj, wb)

        @pl.when(i == n_tiles - 1)
        def _():
            for j in range(N_DEV):
                for n in range(3):
                    dwb_ref[j, n] = acc[n, :, j * cols:(j + 1) * cols].astype(BF16)

    rows = lambda w: pl.BlockSpec((tm, w), lambda i: (i, 0))
    merge = lambda n: pl.BlockSpec((tm, D_MODEL), lambda i: (i, MERGE_BLOCK_1024 + n))
    return _pcall(
        body, name=name, grid=(n_tiles,),
        in_specs=[rows(D_MODEL), rows(WIDTH), rows(WIDTH), rows(WIDTH), merge(0), merge(1), merge(2),
                  pl.BlockSpec((None, 3, WIDTH, D_MODEL), lambda i: (layer, 0, 0, 0))],
        out_specs=(pl.BlockSpec((DU_MERGE[1], tm, WIDTH), lambda i: (DU_MERGE[0] // DU_MERGE[1], i, 0)),
                   rows(WIDTH), rows(WIDTH), rows(WIDTH),
                   pl.BlockSpec((N_DEV, 3, WIDTH, cols), lambda i: (0, 0, 0, 0))),
        out_shape=(jax.ShapeDtypeStruct((DU_PIECES, s, WIDTH), BF16),
                   jax.ShapeDtypeStruct((s, WIDTH), F32), jax.ShapeDtypeStruct((s, WIDTH), F32),
                   jax.ShapeDtypeStruct((s, WIDTH), F32),
                   jax.ShapeDtypeStruct((N_DEV, 3, WIDTH, cols), BF16)),
        scratch_shapes=[pltpu.VMEM((3, WIDTH, D_MODEL), F32)],
        compiler_params=_params(("arbitrary",)),
    )(dmerged, y_pool, y_conv, y_sb, u, u, u, wb_all)


def _sb_bwd(u, o, dys, du, name, xchg=None):
    s = u.shape[0]
    tq = tk = min(128, s)
    pairs = SB_PAIRS
    width = pairs * LANES
    assert width == WIDTH
    rows = 2 * pairs * tq
    pair_rows = lambda a: slice(a * 2 * tq, (a + 1) * 2 * tq)

    x_arrs, x_gather = xchg if xchg else ((), True)
    n_x = len(x_arrs)
    grid = (4 // pairs, s // tq)

    def body(*refs):
        q_ref, k_ref, v_ref, g_ref, o_ref, dys_ref = refs[:6]
        x_in, refs = refs[7:7 + n_x], refs[7 + n_x:]
        du_ref, dk_ref, dv_ref = refs[:3]
        dq_ref, dg_ref = du_ref.at[0], du_ref.at[1]
        x_out, refs = refs[3:3 + n_x], refs[3 + n_x:]
        kbf, vbf, kst, z_s, ell_s, ls_s, cl_s, wb_s, g_s, bef_s, cg_s, beta_s = refs[:12]
        x_sems = refs[12:]
        i = pl.program_id(1)
        if n_x:
            @pl.when((pl.program_id(0) == 0) & (i == 0))
            def _():
                _exchange_start(x_in, x_out, x_sems, x_gather)

        @pl.when(i == 0)
        def _():
            dk_ref[...] = jnp.zeros_like(dk_ref)
            dv_ref[...] = jnp.zeros_like(dv_ref)
            kf = k_ref[...]
            kbf[...] = kf.astype(BF16)
            vbf[...] = v_ref[...].astype(BF16)
            first_s = _first_head_lanes(s, width)
            kst[0] = jnp.where(first_s, kf, 0.0).astype(BF16)
            kst[1] = jnp.where(first_s, 0.0, kf).astype(BF16)

        first = _first_head_lanes(tq)
        mask = _causal_mask(tq, tk, 2 * pairs)
        m_strict = _suffix_matrix(tk, False, 1)
        m_incl = _suffix_matrix(tk, True, 2)

        gate = g_ref[...]
        sg = _sigmoid(gate)
        dy = dys_ref[...]
        ov = o_ref[...]
        dg_ref[...] = (dy * ov * (sg * (1.0 + gate * (1.0 - sg)))).astype(BF16)
        do = (dy * (gate * sg)).astype(BF16)
        prod = do.astype(F32) * ov
        row_sum = lambda v: jnp.broadcast_to(jnp.sum(v, axis=1, keepdims=True), (tq, tk))
        dsum, docat, qcat = [], [], []
        for a in range(pairs):
            pa = prod[:, _pair_lanes(a)]
            dsum += [row_sum(jnp.where(first, pa, 0.0)), row_sum(jnp.where(first, 0.0, pa))]
            docat.append(_stack_heads(do[:, _pair_lanes(a)], first))
            qcat.append(_stack_heads(q_ref[:, _pair_lanes(a)] * SB_SCALE, first))
        dsum = jnp.concatenate(dsum, axis=0)

        def block_start(b):
            return pl.multiple_of(jnp.maximum(b, 0) * tk, tk)

        def scores(b):
            off = block_start(b)
            z_s[...] = jnp.concatenate([_dot_nt(qcat[a], kbf[pl.ds(off, tk), _pair_lanes(a)]) for a in range(pairs)],
                                       axis=0)

        def log_weights(m):
            ls, cs = _sb_log_terms(z_s[...], m, m_strict)
            cl = cl_s[...]
            ell_s[...] = ls + cs[:, :tk] + cl
            cl_s[...] = cl + cs[:, tk:]
            ls_s[...] = ls

        def weights(b, m):
            off = block_start(b)
            dwt = jnp.concatenate([_dot_nt(docat[a], vbf[pl.ds(off, tk), _pair_lanes(a)]) for a in range(pairs)],
                                  axis=0)
            w = jnp.exp(ell_s[...])
            if m is not None:
                w = jnp.where(m, w, 0.0)
            wb = w.astype(BF16)
            g = dwt * wb.astype(F32)
            gs = _suffix_sums(g, m_incl)
            cg = cg_s[...]
            beta = jnp.exp(ls_s[...])
            wb_s[...] = wb
            beta_s[...] = beta
            g_s[...] = g * (1.0 - beta)
            bef_s[...] = gs[:, :tk] + cg
            cg_s[...] = cg + gs[:, tk:]

        def grads(b, dqs, m):
            dz = g_s[...] - beta_s[...] * (dsum - bef_s[...])
            if m is not None:
                dz = jnp.where(m, dz, 0.0)
            dzb = dz.astype(BF16)
            wb = wb_s[...]
            off = pl.multiple_of(b * tk, tk)
            new = []
            for a in range(pairs):
                r0 = a * 2 * tq
                kcat = jnp.concatenate([kst[0, pl.ds(off, tk), _pair_lanes(a)], kst[1, pl.ds(off, tk), _pair_lanes(a)]],
                                       axis=0)
                new.append(dqs[a] + _dot(jnp.concatenate([dzb[r0:r0 + tq], dzb[r0 + tq:r0 + 2 * tq]], axis=1), kcat))
                dk_ref[pl.ds(off, tk), _pair_lanes(a)] += _dot_tn(dzb[pair_rows(a)], qcat[a])
                dv_ref[pl.ds(off, tk), _pair_lanes(a)] += _dot_tn(wb[pair_rows(a)], docat[a])
            return tuple(new)

        zero = jnp.zeros((rows, tk), F32)
        cl_s[...] = zero
        cg_s[...] = zero
        scores(i)
        log_weights(mask)
        scores(i - 1)
        weights(i, mask)
        log_weights(None)
        scores(i - 2)
        dqs = grads(i, tuple(jnp.zeros((tq, LANES), F32) for _ in range(pairs)), mask)
        weights(i - 1, None)
        log_weights(None)
        scores(i - 3)

        def step(n, dqs):
            dqs = grads(i - n, dqs, None)
            weights(i - n - 1, None)
            log_weights(None)
            scores(i - n - 3)
            return dqs

        dqs = lax.fori_loop(1, i + 1, step, dqs)
        dq_ref[...] = (jnp.concatenate(dqs, axis=1) * SB_SCALE).astype(BF16)
        if n_x:
            @pl.when((pl.program_id(0) == grid[0] - 1) & (i == grid[1] - 1))
            def _():
                _exchange_wait(x_in, x_out, x_sems, x_gather)

    base = lambda cb: cb // pairs
    qblk = lambda cb: pl.BlockSpec((tq, width), lambda p, i: (i, base(cb) + p))
    full = lambda cb: pl.BlockSpec((s, width), lambda p, i: (0, base(cb) + p), pipeline_mode=pl.Buffered(1))
    state = pltpu.VMEM((rows, tk), F32)
    return _pcall(
        body, name=name, grid=grid,
        in_specs=[qblk(CB_SB_Q), full(CB_SB_K), full(CB_SB_V), qblk(CB_SB_G), qblk(0), qblk(0), ANY_SPEC]
        + [ANY_SPEC] * n_x,
        out_specs=(pl.BlockSpec((DU_SB_QG[1], tq, WIDTH), lambda p, i: (DU_SB_QG[0] // DU_SB_QG[1], i, 0)),
                   full(0), full(0)) + (ANY_SPEC,) * n_x,
        out_shape=(jax.ShapeDtypeStruct(du.shape, du.dtype), jax.ShapeDtypeStruct((s, WIDTH), F32),
                   jax.ShapeDtypeStruct((s, WIDTH), F32)) + tuple(_exchange_out_shapes(x_arrs, x_gather)),
        input_output_aliases={6: 0},
        scratch_shapes=[pltpu.VMEM((s, width), BF16), pltpu.VMEM((s, width), BF16), pltpu.VMEM((2, s, width), BF16),
                        state, state, state, state, pltpu.VMEM((rows, tk), BF16),
                        state, state, state, state] + (_exchange_sems(n_x) if n_x else []),
        compiler_params=_params(("arbitrary", "arbitrary")),
    )(u, u, u, u, o, dys, du, *x_arrs)


def _conv_bwd(u, conv_w, conv_b, dyc, du, name):
    s = u.shape[0]
    t = min(256, s)
    n_tiles = s // t

    def body(xc_ref, gb_ref, gc_ref, cg_ref, w_ref, b_ref, dy_ref, du_in, du_ref, dw_ref, db_ref, zs, ds):
        dxc_ref, dgb_ref, dgc_ref, dcg_ref = (du_ref.at[p] for p in range(4))
        zs[0:CONV_HALO, :] = jnp.zeros((CONV_HALO, LANES), F32)
        zs[CONV_HALO:, :] = gc_ref[...] * xc_ref[...]
        ds[s:, :] = jnp.zeros((CONV_HALO, LANES), F32)
        w0, w1, w2 = w_ref[0:1, :], w_ref[1:2, :], w_ref[2:3, :]
        bias = b_ref[...]

        def first(i, sums):
            t0 = pl.multiple_of(i * t, t)
            z0, z1, z2 = _conv_taps(zs, t0, t)
            pre = w0 * z2 + w1 * z1 + w2 * z0 + bias
            gate = cg_ref[pl.ds(t0, t), :]
            sg = _sigmoid(gate)
            gb = gb_ref[pl.ds(t0, t), :]
            dy = dy_ref[pl.ds(t0, t), :]
            dcg_ref[pl.ds(t0, t), :] = (dy * gb * pre * (sg * (1.0 + gate * (1.0 - sg)))).astype(BF16)
            dgb_ref[pl.ds(t0, t), :] = (dy * pre * (gate * sg)).astype(BF16)
            dc = dy * gb * (gate * sg)
            ds[pl.ds(t0, t), :] = dc
            red = lambda v: jnp.sum(v, axis=0, keepdims=True)
            return (sums[0] + red(dc * z2), sums[1] + red(dc * z1), sums[2] + red(dc * z0), sums[3] + red(dc))

        zrow = jnp.zeros((1, LANES), F32)
        sw0, sw1, sw2, sb = lax.fori_loop(0, n_tiles, first, (zrow, zrow, zrow, zrow))
        dw_ref[0:1, :] = sw0
        dw_ref[1:2, :] = sw1
        dw_ref[2:3, :] = sw2
        db_ref[...] = sb

        def second(i, carry):
            t0 = pl.multiple_of(i * t, t)
            ext = ds[pl.ds(t0, t + CONV_HALO), :]
            n = t + CONV_HALO
            d0 = ext[:t, :]
            d1 = pltpu.roll(ext, n - 1, 0)[:t, :]
            d2 = pltpu.roll(ext, n - 2, 0)[:t, :]
            dz = w2 * d0 + w1 * d1 + w0 * d2
            dgc_ref[pl.ds(t0, t), :] = (dz * xc_ref[pl.ds(t0, t), :]).astype(BF16)
            dxc_ref[pl.ds(t0, t), :] = (dz * gc_ref[pl.ds(t0, t), :]).astype(BF16)
            return carry

        lax.fori_loop(0, n_tiles, second, 0)

    col = lambda base: pl.BlockSpec((s, LANES), lambda j: (0, base + j))
    first, count = DU_CONV
    return _pcall(
        body, name=name, grid=(4,),
        in_specs=[col(CB_CONV_X), col(CB_CONV_GB), col(CB_CONV_GC), col(CB_CONV_G),
                  pl.BlockSpec((3, LANES), lambda j: (0, j)), pl.BlockSpec((1, LANES), lambda j: (0, j)), col(0),
                  ANY_SPEC],
        out_specs=(pl.BlockSpec((count, s, LANES), lambda j: (first // count, 0, j)),
                   pl.BlockSpec((3, LANES), lambda j: (0, j)), pl.BlockSpec((1, LANES), lambda j: (0, j))),
        out_shape=(jax.ShapeDtypeStruct(du.shape, du.dtype),
                   jax.ShapeDtypeStruct((3, WIDTH), F32), jax.ShapeDtypeStruct((1, WIDTH), F32)),
        scratch_shapes=[pltpu.VMEM((CONV_HALO + s, LANES), F32), pltpu.VMEM((s + CONV_HALO, LANES), F32)],
        input_output_aliases={7: 0},
        compiler_params=_params(("arbitrary",)),
    )(u, u, u, u, conv_w, conv_b, dyc, du)


def _pool_bwd(u, pool_w, pool_scale, dyp, du, name):
    s = u.shape[0]
    t = min(256, s)
    n_tiles = s // t

    def body(pv_ref, pg_ref, w_ref, sc_ref, dy_ref, du_in, du_ref, dw_ref, dsc_ref, vs, es, dps):
        dpv_ref, dpg_ref = du_ref.at[0], du_ref.at[1]
        grp = pl.program_id(0)
        vs[0:POOL_HALO, :] = jnp.zeros((POOL_HALO, LANES), F32)
        vs[POOL_HALO:, :] = pv_ref[...]
        es[s:, :] = jnp.zeros((POOL_HALO, LANES), F32)
        wb = w_ref[...].astype(BF16)
        scale = sc_ref[...]

        def first(i, sums):
            dw, dsc = sums
            t0 = pl.multiple_of(i * t, t)
            win, v = _pool_window(vs, t0, t, grp)
            cnt = _pool_count(t0, t, grp)
            pb = (win / cnt - v).astype(BF16)
            mixed = _dot(pb, wb)
            gate = pg_ref[pl.ds(t0, t), :]
            sg = _sigmoid(gate)
            dy = dy_ref[pl.ds(t0, t), :]
            dpg_ref[pl.ds(t0, t), :] = (dy * (mixed * scale) * (sg * (1.0 + gate * (1.0 - sg)))).astype(BF16)
            dms = dy * (gate * sg)
            dsc = dsc + jnp.sum(dms * mixed, axis=0, keepdims=True)
            dmb = (dms * scale).astype(BF16)
            dw = dw + _dot_tn(pb, dmb)
            dpooled = _dot_nt(dmb, wb)
            dps[pl.ds(t0, t), :] = dpooled
            es[pl.ds(t0, t), :] = dpooled / cnt
            return dw, dsc

        dw, dsc = lax.fori_loop(0, n_tiles, first, (jnp.zeros((LANES, LANES), F32), jnp.zeros((1, LANES), F32)))
        dw_ref[...] = dw
        dsc_ref[...] = dsc

        def second(i, carry):
            t0 = pl.multiple_of(i * t, t)
            ext = es[pl.ds(t0, t + POOL_HALO), :]
            n = t + POOL_HALO
            f2 = ext + pltpu.roll(ext, n - 1, 0)
            f4 = f2 + pltpu.roll(f2, n - 2, 0)
            f8 = f4 + pltpu.roll(f4, n - 4, 0)
            f16 = f8 + pltpu.roll(f8, n - 8, 0)
            sel = jnp.where(grp == 0, f2, jnp.where(grp == 1, f4, jnp.where(grp == 2, f8, f16)))
            dpv_ref[pl.ds(t0, t), :] = (sel[:t, :] - dps[pl.ds(t0, t), :]).astype(BF16)
            return carry

        lax.fori_loop(0, n_tiles, second, 0)

    col = lambda base: pl.BlockSpec((s, LANES), lambda g: (0, base + g))
    first, count = DU_POOL
    return _pcall(
        body, name=name, grid=(4,),
        in_specs=[col(CB_POOL_V), col(CB_POOL_G), pl.BlockSpec((None, LANES, LANES), lambda g: (g, 0, 0)),
                  pl.BlockSpec((1, LANES), lambda g: (0, g)), col(0), ANY_SPEC],
        out_specs=(pl.BlockSpec((count, s, LANES), lambda g: (first // count, 0, g)),
                   pl.BlockSpec((None, LANES, LANES), lambda g: (g, 0, 0)),
                   pl.BlockSpec((1, LANES), lambda g: (0, g))),
        out_shape=(jax.ShapeDtypeStruct(du.shape, du.dtype),
                   jax.ShapeDtypeStruct((4, LANES, LANES), F32), jax.ShapeDtypeStruct((1, WIDTH), F32)),
        scratch_shapes=[pltpu.VMEM((POOL_HALO + s, LANES), F32), pltpu.VMEM((s + POOL_HALO, LANES), F32),
                        pltpu.VMEM((s, LANES), F32)],
        input_output_aliases={5: 0},
        compiler_params=_params(("arbitrary",)),
    )(u, u, pool_w, pool_scale, dyp, du)


def _in_proj_bwd_x(du, w_all, x, g_pre, dy, name):
    s = x.shape[0]
    tm = min(1024, s)
    grid = (s // tm, N_DEV)

    def body(dua_ref, dub_ref, w_ref, x_ref, g_ref, dy_ref, dx_ref, dg_ref, acc):
        i, k = pl.program_id(0), pl.program_id(1)

        @pl.when(k == 0)
        def _():
            acc[...] = jnp.zeros_like(acc)

        @pl.when((k == 0) & (i == 0))
        def _():
            dg_ref[...] = jnp.zeros_like(dg_ref)
        acc[...] += _dot_nt(jnp.concatenate([dua_ref[...], dub_ref[...]], axis=1), w_ref[...])

        @pl.when(k == N_DEV - 1)
        def _():
            dh, xv = acc[...], x_ref[...]
            r = lax.rsqrt(jnp.mean(xv * xv, axis=-1, keepdims=True) + RMS_EPS)
            dg_ref[...] += jnp.sum(dh * xv * r, axis=0, keepdims=True)
            a = dh * g_ref[...]
            dx_ref[...] = dy_ref[...] + r * a - xv * (r * r * r) * jnp.mean(a * xv, axis=-1, keepdims=True)

    rows = lambda: pl.BlockSpec((tm, D_MODEL), lambda i, k: (i, 0))
    vec = lambda: pl.BlockSpec((1, D_MODEL), lambda i, k: (0, 0))
    piece = lambda half: pl.BlockSpec((None, tm, WIDTH), lambda i, k: (_du_pieces_of_block(k)[half], i, 0))
    return _pcall(
        body, name=name, grid=grid,
        in_specs=[piece(0), piece(1), pl.BlockSpec((None, D_MODEL, COLS_PER_DEV), lambda i, k: (k, 0, 0)),
                  rows(), vec(), rows()],
        out_specs=(rows(), vec()),
        out_shape=(jax.ShapeDtypeStruct((s, D_MODEL), F32), jax.ShapeDtypeStruct((1, D_MODEL), F32)),
        scratch_shapes=[pltpu.VMEM((tm, D_MODEL), F32)],
        compiler_params=_params(("arbitrary", "arbitrary")),
    )(du, du, w_all, x, g_pre, dy)


ROW_OFFSETS = (2, 4, 3, 5, 6, 7, 0, 1)


def _in_proj_bwd_send(h, du, w_all, x, g_pre, dy, name):
    s = x.shape[0]
    tk = s // N_DEV
    tm = min(1024, s)
    n_i = s // tm
    grid = (N_DEV + n_i, N_DEV)
    last = N_DEV - 1
    def offset(row):
        return functools.reduce(lambda acc, rn: jnp.where(row == rn[0], rn[1], acc), enumerate(ROW_OFFSETS), 0)

    def body(me_ref, h_ref, duwa_ref, duwb_ref, duxa_ref, duxb_ref, w_ref, x_ref, g_ref, dy_ref,
             dx_ref, dg_ref, recv_ref, part_ref, acc_w, stage, acc_x, send_sems, recv_sems, park_sems):
        r, k = pl.program_id(0), pl.program_id(1)
        x_, y_, c_ = lax.axis_index("x"), lax.axis_index("y"), lax.axis_index("c")
        me = 4 * x_ + 2 * y_ + c_
        flip = lambda v, bit: 1 - v if bit else v
        peer = lambda n: (flip(x_, (n >> 2) & 1), flip(y_, (n >> 1) & 1), flip(c_, n & 1))

        def park(row):
            n = ROW_OFFSETS[row]
            dst = recv_ref.at[me] if n == 0 else part_ref.at[n]
            return pltpu.make_async_copy(stage.at[row % 2], dst, park_sems.at[row % 2])

        def send(n, landing=False):
            px, py, pc = peer(n)
            dst = recv_ref.at[4 * px + 2 * py + pc] if landing else recv_ref.at[me]
            return pltpu.make_async_remote_copy(
                src_ref=part_ref.at[n], dst_ref=dst, send_sem=send_sems.at[n], recv_sem=recv_sems.at[n],
                device_id=(px, py, pc), device_id_type=pl.DeviceIdType.MESH)

        def parked(row):
            park(row).wait()
            if ROW_OFFSETS[row] >= 1:
                send(ROW_OFFSETS[row]).start()

        @pl.when(r < N_DEV)
        def _():
            @pl.when(k == 0)
            def _():
                acc_w[...] = jnp.zeros_like(acc_w)
            acc_w[...] += _dot_tn(h_ref[...], jnp.concatenate([duwa_ref[...], duwb_ref[...]], axis=1))

            for row in range(N_DEV):
                @pl.when((k == last) & (r == row))
                def _():
                    if row >= 1:
                        parked(row - 1)
                    stage[row % 2] = acc_w[...].astype(BF16)
                    park(row).start()

        @pl.when(r >= N_DEV)
        def _():
            @pl.when(k == 0)
            def _():
                acc_x[...] = jnp.zeros_like(acc_x)

            @pl.when((k == 0) & (r == N_DEV))
            def _():
                dg_ref[...] = jnp.zeros_like(dg_ref)
                parked(last)
            acc_x[...] += _dot_nt(jnp.concatenate([duxa_ref[...], duxb_ref[...]], axis=1), w_ref[...])

            @pl.when(k == last)
            def _():
                dh, xv = acc_x[...], x_ref[...]
                rs = lax.rsqrt(jnp.mean(xv * xv, axis=-1, keepdims=True) + RMS_EPS)
                dg_ref[...] += jnp.sum(dh * xv * rs, axis=0, keepdims=True)
                a = dh * g_ref[...]
                dx_ref[...] = dy_ref[...] + rs * a - xv * (rs * rs * rs) * jnp.mean(a * xv, axis=-1, keepdims=True)

        @pl.when((r == grid[0] - 1) & (k == last))
        def _():
            for n in range(1, N_DEV):
                send(n).wait_send()
            for n in range(1, N_DEV):
                send(n, landing=True).wait_recv()

    in_w = lambda r: r < N_DEV
    row_x = lambda r: jnp.maximum(r - N_DEV, 0)
    rows = lambda: pl.BlockSpec((tm, D_MODEL), lambda r, k, me: (row_x(r), 0))
    vec = lambda: pl.BlockSpec((1, D_MODEL), lambda r, k, me: (0, 0))
    block_w = lambda r, me: jnp.bitwise_xor(me[0], offset(jnp.minimum(r, last)))
    block_x = lambda r, k: jnp.where(in_w(r), 0, k)
    piece_w = lambda half: pl.BlockSpec(
        (None, tk, WIDTH), lambda r, k, me: (_du_pieces_of_block(block_w(r, me))[half], jnp.where(in_w(r), k, last), 0))
    piece_x = lambda half: pl.BlockSpec(
        (None, tm, WIDTH), lambda r, k, me: (_du_pieces_of_block(block_x(r, k))[half], row_x(r), 0))
    grid_spec = pltpu.PrefetchScalarGridSpec(
        num_scalar_prefetch=1, grid=grid,
        in_specs=[pl.BlockSpec((tk, D_MODEL), lambda r, k, me: (jnp.where(in_w(r), k, last), 0)),
                  piece_w(0), piece_w(1), piece_x(0), piece_x(1),
                  pl.BlockSpec((None, D_MODEL, COLS_PER_DEV), lambda r, k, me: (block_x(r, k), 0, 0)),
                  rows(), vec(), rows()],
        out_specs=(rows(), vec(), ANY_SPEC, ANY_SPEC),
        scratch_shapes=[pltpu.VMEM((D_MODEL, COLS_PER_DEV), F32), pltpu.VMEM((2, D_MODEL, COLS_PER_DEV), BF16),
                        pltpu.VMEM((tm, D_MODEL), F32), pltpu.SemaphoreType.DMA((N_DEV,)),
                        pltpu.SemaphoreType.DMA((N_DEV,)), pltpu.SemaphoreType.DMA((2,))])
    me = 4 * lax.axis_index("x") + 2 * lax.axis_index("y") + lax.axis_index("c")
    blocks = jax.ShapeDtypeStruct((N_DEV, D_MODEL, COLS_PER_DEV), BF16)
    dx, dg, received, _ = _pcall(
        body, name=name, grid_spec=grid_spec,
        out_shape=(jax.ShapeDtypeStruct((s, D_MODEL), F32), jax.ShapeDtypeStruct((1, D_MODEL), F32), blocks, blocks),
        compiler_params=_params(("arbitrary", "arbitrary")),
    )(jnp.reshape(me, (1,)).astype(jnp.int32), h, du, du, du, du, w_all, x, g_pre, dy)
    return dx, dg, received


def _in_proj_bwd_w(h, du, name):
    s = h.shape[0]
    tk = min(512, s)
    n_k = s // tk

    def body(h_ref, dua_ref, dub_ref, out_ref, acc):
        k = pl.program_id(1)

        @pl.when(k == 0)
        def _():
            acc[...] = jnp.zeros_like(acc)
        acc[...] += _dot_tn(h_ref[...], jnp.concatenate([dua_ref[...], dub_ref[...]], axis=1))

        @pl.when(k == n_k - 1)
        def _():
            out_ref[...] = acc[...].astype(BF16)

    piece = lambda half: pl.BlockSpec((None, tk, WIDTH), lambda j, k: (_du_pieces_of_block(j)[half], k, 0))
    return _pcall(
        body, name=name, grid=(N_DEV, n_k),
        in_specs=[pl.BlockSpec((tk, D_MODEL), lambda j, k: (k, 0)), piece(0), piece(1)],
        out_specs=pl.BlockSpec((None, D_MODEL, COLS_PER_DEV), lambda j, k: (j, 0, 0)),
        out_shape=jax.ShapeDtypeStruct((N_DEV, D_MODEL, COLS_PER_DEV), BF16),
        scratch_shapes=[pltpu.VMEM((D_MODEL, COLS_PER_DEV), F32)],
        compiler_params=_params(("parallel", "arbitrary")),
    )(h, du, du)


def _adamw_math(g, w, m, v):
    m_new = ADAM_B1 * m + (1.0 - ADAM_B1) * g
    v_new = ADAM_B2 * v + (1.0 - ADAM_B2) * (g * g)
    m_hat = m_new / (1.0 - ADAM_B1 ** ADAM_STEP)
    v_hat = v_new / (1.0 - ADAM_B2 ** ADAM_STEP)
    delta = -ADAM_LR * (m_hat / (jnp.sqrt(v_hat) + ADAM_EPS) + ADAM_WD * w)
    return delta, m_new, v_new


def _sum_partials(p_ref):
    total = p_ref[0].astype(F32)
    for d in range(1, N_DEV):
        total = total + p_ref[d].astype(F32)
    return total


def _adamw_layers(parts0, parts1, w, m, v, name):
    _, r, c = w.shape
    tr = min(128, r)
    n_r = r // tr

    def body(p0_ref, p1_ref, w_ref, m_ref, v_ref, g_ref, d_ref, mo_ref, vo_ref):
        layer = pl.program_id(0)

        @pl.when(layer == 0)
        def _():
            g_ref[...] = _sum_partials(p0_ref)

        @pl.when(layer == 1)
        def _():
            g_ref[...] = _sum_partials(p1_ref)
        d_ref[...], mo_ref[...], vo_ref[...] = _adamw_math(g_ref[...], w_ref[...], m_ref[...], v_ref[...])

    part = lambda which: pl.BlockSpec((N_DEV, tr, c), lambda l, i: (0, jnp.where(l == which, i, 0), 0))
    par = lambda: pl.BlockSpec((None, tr, c), lambda l, i: (l, i, 0))
    out = jax.ShapeDtypeStruct(w.shape, F32)
    return _pcall(
        body, name=name, grid=(2, n_r),
        in_specs=[part(0), part(1), par(), par(), par()],
        out_specs=(par(), par(), par(), par()),
        out_shape=(out, out, out, out),
        compiler_params=_params(("arbitrary", "arbitrary")),
    )(parts0, parts1, w, m, v)


def _adamw_small(parts, w, m, v, name):
    def body(p_ref, w_ref, m_ref, v_ref, g_ref, d_ref, mo_ref, vo_ref):
        g = _sum_partials(p_ref)
        g_ref[...] = g
        d_ref[...], mo_ref[...], vo_ref[...] = _adamw_math(g, w_ref[...], m_ref[...], v_ref[...])

    out = jax.ShapeDtypeStruct(w.shape, F32)
    return _pcall(body, name=name, out_shape=(out, out, out, out), compiler_params=_params())(parts, w, m, v)


def _adamw_plain(g, w, m, v, name):
    def body(g_ref, w_ref, m_ref, v_ref, d_ref, mo_ref, vo_ref):
        d_ref[...], mo_ref[...], vo_ref[...] = _adamw_math(g_ref[...], w_ref[...], m_ref[...], v_ref[...])

    out = jax.ShapeDtypeStruct(w.shape, F32)
    return _pcall(body, name=name, out_shape=(out, out, out), compiler_params=_params())(g, w, m, v)


def _rows128(a):
    return a.reshape(-1, LANES)


SMALL_NAMES = ("pre_norm_g", "pool_w", "pool_scale", "conv_w", "conv_b", "post_norm_g")


def kernel(x, pre_norm_g, w_in, pool_w, pool_scale, conv_w, conv_b, w_branch, w_out, post_norm_g, loss_target, m_pre_norm_g, m_w_in, m_pool_w, m_pool_scale, m_conv_w, m_conv_b, m_w_branch, m_w_out, m_post_norm_g, v_pre_norm_g, v_w_in, v_pool_w, v_pool_scale, v_conv_w, v_conv_b, v_w_branch, v_w_out, v_post_norm_g):
    s = x.shape[1]
    me = 4 * lax.axis_index("x") + 2 * lax.axis_index("y") + lax.axis_index("c")
    x0 = x[0]
    target = loss_target[0]
    conv_cols = conv_w.shape[-1]

    conv_w_pad = jnp.pad(conv_w.reshape(2 * 3, conv_cols), ((0, 2), (0, LANES - conv_cols)))
    w_in_all = [None, None]
    w_in_all[0], cw_g = _gather_two_level([w_in[0].astype(BF16), conv_w_pad], "gather_w_in_0")
    conv_w_full = cw_g[:, :6, :conv_cols].reshape(N_DEV, 2, 3, conv_cols).transpose(1, 2, 0, 3).reshape(2, 3, WIDTH)
    later_weights = ([w_in[1].astype(BF16), w_branch.astype(BF16), w_out.astype(BF16)], True)

    saved = []
    xin = x0
    for l in range(2):
        u, h = _in_proj_fwd(xin, pre_norm_g[l:l + 1], w_in_all[l], f"in_proj_fwd_{l}")
        y_pool = _pool_fwd(u, pool_w[l], pool_scale[l:l + 1], f"pool_fwd_{l}")
        y_conv = _conv_fwd(u, conv_w_full[l], conv_b[l:l + 1], f"conv_fwd_{l}")
        if l == 0:
            o_sb, y_sb, w_in_all[1], wb_g, wo_all = _sb_fwd(u, f"sb_fwd_{l}", later_weights)
            wb_all = wb_g.transpose(1, 2, 3, 0, 4).reshape(2, 3, WIDTH, D_MODEL)
        else:
            o_sb, y_sb = _sb_fwd(u, f"sb_fwd_{l}")
        xout, merged, pre = _merge_out_fwd(y_pool, y_conv, y_sb, u, wb_all, wo_all, xin, post_norm_g[l:l + 1], l,
                                           f"merge_out_fwd_{l}")
        saved.append((xin, u, h, y_pool, y_conv, y_sb, o_sb, merged, pre))
        xin = xout

    dy, loss_row = _loss_and_grad(xin, target, "loss")

    small = [None, None]
    recv = [None, None]
    ready = []
    for l in (1, 0):
        xl, u, h, y_pool, y_conv, y_sb, o_sb, merged, pre = saved[l]
        dmerged, dwo, dg_post = _out_proj_bwd(dy, pre, post_norm_g[l:l + 1], merged, wo_all, l, f"out_proj_bwd_{l}")
        du, dyp, dyc, dys, dwb = _merge_bwd(dmerged, y_pool, y_conv, y_sb, u, wb_all, l, f"merge_bwd_{l}")
        dwb = dwb.reshape(N_DEV, 3 * WIDTH, D_MODEL // N_DEV)
        dwo = dwo.reshape(N_DEV, D_MODEL // N_DEV, D_MODEL)
        du, dcw, dcb = _conv_bwd(u, conv_w_full[l], conv_b[l:l + 1], dyc, du, f"conv_bwd_{l}")
        du, dpw, dps = _pool_bwd(u, pool_w[l], pool_scale[l:l + 1], dyp, du, f"pool_bwd_{l}")
        small[l] = dict(pool_w=dpw, pool_scale=dps, conv_w=dcw, conv_b=dcb, post_norm_g=dg_post)
        if l == 1:
            du, dk, dv = _sb_bwd(u, o_sb, dys, du, f"sb_bwd_{l}")
        else:
            small[l]["pre_norm_g"] = jnp.zeros((1, D_MODEL), F32)
            packed = jnp.concatenate(
                [_rows128(jnp.stack([small[0][n], small[1][n]])) for n in SMALL_NAMES]
                + [jnp.pad(loss_row, ((0, 7), (0, 0)))], axis=0)
            du, dk, dv, *got, packed_all = _sb_bwd(
                u, o_sb, dys, du, f"sb_bwd_{l}", (ready + [dwb, dwo, packed], (False,) * 5 + (True,)))
            recv[1] = got[:3]
        du = lax.dynamic_update_slice(du, jnp.stack([dk, dv]).astype(BF16), (DU_SB_KV[0], 0, 0))
        if l == 1:
            dwi = _in_proj_bwd_w(h, du, f"in_proj_bwd_w_{l}")
            ready = [dwi, dwb, dwo]
            dx, dg_pre = _in_proj_bwd_x(du, w_in_all[l], xl, pre_norm_g[l:l + 1], dy, f"in_proj_bwd_x_{l}")
            small[l]["pre_norm_g"] = dg_pre
        else:
            dx, dg_pre, got_dwi = _in_proj_bwd_send(h, du, w_in_all[l], xl, pre_norm_g[l:l + 1], dy,
                                                    f"in_proj_bwd_{l}")
            recv[0] = [got_dwi] + got[3:]
        dy = dx
    grad_x = dy[None]

    (g_pre_0_all,) = _exchange([_rows128(dg_pre)], True, "gather_g_pre_0")
    packed_all = lax.dynamic_update_slice(packed_all, g_pre_0_all, (0, 0, 0))
    sizes = dict(pre_norm_g=16, pool_w=1024, pool_scale=8, conv_w=24, conv_b=8, post_norm_g=16)
    n_rows = sum(sizes.values())
    loss = jnp.sum(packed_all[:, n_rows, 0])

    given = dict(pre_norm_g=(pre_norm_g, m_pre_norm_g, v_pre_norm_g), pool_w=(pool_w, m_pool_w, v_pool_w),
                 pool_scale=(pool_scale, m_pool_scale, v_pool_scale), conv_b=(conv_b, m_conv_b, v_conv_b),
                 post_norm_g=(post_norm_g, m_post_norm_g, v_post_norm_g))
    zeros_cw = jnp.zeros((sizes["conv_w"], LANES), F32)
    pack3 = [jnp.concatenate([zeros_cw if n == "conv_w" else _rows128(given[n][k]) for n in SMALL_NAMES], axis=0)
             for k in range(3)]
    sg, sd, sm, sv = _adamw_small(packed_all[:, :n_rows], pack3[0], pack3[1], pack3[2], "adamw_small")

    def unpack(buf, name, shape):
        start = 0
        for n in SMALL_NAMES:
            if n == name:
                return buf[start:start + sizes[n]].reshape(shape)
            start += sizes[n]

    out = {}
    for n in ("pre_norm_g", "pool_w", "pool_scale", "conv_b", "post_norm_g"):
        shape = given[n][0].shape
        out[n] = tuple(unpack(b, n, shape) for b in (sg, sd, sm, sv))
    g_cw = lax.dynamic_slice_in_dim(unpack(sg, "conv_w", (2, 3, WIDTH)), me * conv_cols, conv_cols, axis=2)
    cw2 = lambda a: a.reshape(6, conv_cols)
    d_cw, m_cw, v_cw = _adamw_plain(cw2(g_cw), cw2(conv_w), cw2(m_conv_w), cw2(v_conv_w), "adamw_conv_w")
    out["conv_w"] = (g_cw,) + tuple(a.reshape(2, 3, conv_cols) for a in (d_cw, m_cw, v_cw))

    out["w_in"] = _adamw_layers(recv[0][0], recv[1][0], w_in, m_w_in, v_w_in, "adamw_w_in")
    cols = D_MODEL // N_DEV
    wb3 = lambda a: a.reshape(2, 3 * WIDTH, cols)
    out["w_branch"] = tuple(a.reshape(2, 3, WIDTH, cols) for a in _adamw_layers(
        recv[0][1], recv[1][1], wb3(w_branch), wb3(m_w_branch), wb3(v_w_branch), "adamw_w_branch"))
    out["w_out"] = _adamw_layers(recv[0][2], recv[1][2], w_out, m_w_out, v_w_out, "adamw_w_out")

    order = ("pre_norm_g", "w_in", "pool_w", "pool_scale", "conv_w", "conv_b", "w_branch", "w_out", "post_norm_g")
    return (loss, grad_x) + tuple(out[n][k] for k in range(4) for n in order)
```

```python
import functools

import jax
import jax.numpy as jnp
from jax import lax
from jax.experimental import pallas as pl
from jax.experimental.pallas import tpu as pltpu

F32 = jnp.float32
BF16 = jnp.bfloat16

N_DEV = 8
D_MODEL = 1024
WIDTH = 512
N_IN = 8192
COLS_PER_DEV = N_IN // N_DEV
HEAD_DIM = 64
LANES = 128
SB_SCALE = HEAD_DIM ** -0.5
LOG2E = 1.4426950408889634
RMS_EPS = 1e-6
POOL_HALO = 16
CONV_HALO = 8
ADAM_LR, ADAM_B1, ADAM_B2, ADAM_EPS, ADAM_WD, ADAM_STEP = 0.001, 0.9, 0.999, 1e-08, 0.01, 10
VMEM_LIMIT = 60 * 1024 * 1024

CB_POOL_V, CB_POOL_G = 0, 4
CB_CONV_X, CB_CONV_GB, CB_CONV_GC, CB_CONV_G = 8, 12, 16, 20
CB_SB_Q, CB_SB_K, CB_SB_V, CB_SB_G = 24, 28, 32, 36
MERGE_BLOCK_1024 = 5

DU_PIECES = 16
DU_MERGE = (0, 6)
DU_POOL = (6, 2)
DU_CONV = (8, 4)
DU_SB_QG = (12, 2)
DU_SB_KV = (14, 2)


def _du_pieces_of_block(j):
    first, second = 2 * (j - 5), 2 * (j - 5) + 1
    for block, (a, b) in enumerate(((6, 7), (8, 9), (10, 11), (12, 14), (15, 13))):
        first = jnp.where(j == block, a, first)
        second = jnp.where(j == block, b, second)
    return first, second


def _pcall(body, **kw):
    return pl.pallas_call(body, **kw)


def _params(sem=None):
    if sem is None:
        return pltpu.CompilerParams(vmem_limit_bytes=VMEM_LIMIT)
    return pltpu.CompilerParams(dimension_semantics=sem, vmem_limit_bytes=VMEM_LIMIT)


def _sigmoid(x):
    return 1.0 / (1.0 + jnp.exp(-x))


def _dot(a, b):
    return jnp.dot(a, b, preferred_element_type=F32)


def _dot_nt(a, b):
    return lax.dot_general(a, b, (((1,), (1,)), ((), ())), preferred_element_type=F32)


def _dot_tn(a, b):
    return lax.dot_general(a, b, (((0,), (0,)), ((), ())), preferred_element_type=F32)


def _split_bf16(x):
    hi = x.astype(BF16)
    lo = (x - hi.astype(F32)).astype(BF16)
    return hi, lo


N_PEER = N_DEV - 1
ANY_SPEC = pl.BlockSpec(memory_space=pl.ANY)


def _exchange_copies(ins, outs, send_sems, recv_sems, local_sems, gather, with_recvs=True):
    n = len(ins)
    gathers = _per_array(gather, n)
    x, y, c = lax.axis_index("x"), lax.axis_index("y"), lax.axis_index("c")
    me = 4 * x + 2 * y + c
    flip = lambda v, bit: 1 - v if bit else v
    local, sends, recvs = [], [], []
    for a in range(n):
        src = ins[a] if gathers[a] else ins[a].at[me]
        local.append(pltpu.make_async_copy(src, outs[a].at[me], local_sems.at[a]))
    for k in range(N_PEER):
        px, py, pc = flip(x, ((k + 1) >> 2) & 1), flip(y, ((k + 1) >> 1) & 1), flip(c, (k + 1) & 1)
        peer_id = 4 * px + 2 * py + pc
        for a in range(n):
            src = ins[a] if gathers[a] else ins[a].at[peer_id]
            common = dict(src_ref=src, send_sem=send_sems.at[a * N_PEER + k], recv_sem=recv_sems.at[a * N_PEER + k],
                          device_id=(px, py, pc), device_id_type=pl.DeviceIdType.MESH)
            sends.append(pltpu.make_async_remote_copy(dst_ref=outs[a].at[me], **common))
            if with_recvs:
                recvs.append(pltpu.make_async_remote_copy(dst_ref=outs[a].at[peer_id], **common))
    return local, sends, recvs


def _exchange_start(ins, outs, sems, gather):
    local, sends, _ = _exchange_copies(ins, outs, *sems, gather, with_recvs=False)
    for cp in local + sends:
        cp.start()


def _exchange_wait(ins, outs, sems, gather):
    local, sends, recvs = _exchange_copies(ins, outs, *sems, gather)
    for cp in recvs:
        cp.wait_recv()
    for cp in sends:
        cp.wait_send()
    for cp in local:
        cp.wait()


def _per_array(gather, n):
    return tuple(gather) if isinstance(gather, (tuple, list)) else (gather,) * n


def _exchange_out_shapes(arrs, gather):
    return [jax.ShapeDtypeStruct((N_DEV,) + tuple(a.shape if g else a.shape[1:]), a.dtype)
            for a, g in zip(arrs, _per_array(gather, len(arrs)))]


def _gather_two_level(arrs, name):
    n = len(arrs)

    def body(*refs):
        ins, outs = refs[:n], refs[n:2 * n]
        send_sems, recv_sems, local_sems = refs[2 * n:]
        x, y, c = lax.axis_index("x"), lax.axis_index("y"), lax.axis_index("c")
        me, sibling = (x, y, c), (x, y, 1 - c)
        chips = [(1 - x, y), (x, 1 - y), (1 - x, 1 - y)]
        slot = lambda dev: 4 * dev[0] + 2 * dev[1] + dev[2]

        def copy(a, k, block, to, src=None):
            return pltpu.make_async_remote_copy(
                src_ref=outs[a].at[slot(block)] if src is None else src, dst_ref=outs[a].at[slot(block)],
                send_sem=send_sems.at[a * N_PEER + k], recv_sem=recv_sems.at[a * N_PEER + k],
                device_id=to, device_id_type=pl.DeviceIdType.MESH)

        local = [pltpu.make_async_copy(ins[a], outs[a].at[slot(me)], local_sems.at[a]) for a in range(n)]
        first = []
        for a in range(n):
            first.append(copy(a, 0, me, sibling, src=ins[a]))
            first += [copy(a, 1 + j, me, (*chip, c), src=ins[a]) for j, chip in enumerate(chips)]
        for cp in local + first:
            cp.start()
        passed = []
        for j, chip in enumerate(chips):
            for a in range(n):
                copy(a, 1 + j, (*chip, c), me).wait_recv()
                passed.append(copy(a, 4 + j, (*chip, c), sibling))
                passed[-1].start()
        for a in range(n):
            copy(a, 0, sibling, me).wait_recv()
        for j, chip in enumerate(chips):
            for a in range(n):
                copy(a, 4 + j, (*chip, 1 - c), me).wait_recv()
        for cp in first + passed:
            cp.wait_send()
        for cp in local:
            cp.wait()

    return _pcall(
        body, name=name,
        out_shape=tuple(_exchange_out_shapes(arrs, True)),
        in_specs=[ANY_SPEC] * n, out_specs=tuple([ANY_SPEC] * n),
        scratch_shapes=_exchange_sems(n),
    )(*arrs)


def _exchange_sems(n):
    return [pltpu.SemaphoreType.DMA((n * N_PEER,)), pltpu.SemaphoreType.DMA((n * N_PEER,)),
            pltpu.SemaphoreType.DMA((n,))]


def _exchange(arrs, gather, name):
    n = len(arrs)

    def body(*refs):
        ins, outs, sems = refs[:n], refs[n:2 * n], refs[2 * n:]
        _exchange_start(ins, outs, sems, gather)
        _exchange_wait(ins, outs, sems, gather)

    return _pcall(
        body, name=name,
        out_shape=tuple(_exchange_out_shapes(arrs, gather)),
        in_specs=[ANY_SPEC] * n, out_specs=tuple([ANY_SPEC] * n),
        scratch_shapes=_exchange_sems(n),
    )(*arrs)


def _in_proj_fwd(x, g, w_all, name):
    s = x.shape[0]
    tm = min(1024, s)

    def body(x_ref, g_ref, w_ref, u_ref, h_ref, hs):
        @pl.when(pl.program_id(1) == 0)
        def _():
            xv = x_ref[...]
            r = lax.rsqrt(jnp.mean(xv * xv, axis=-1, keepdims=True) + RMS_EPS)
            hv = (xv * r * g_ref[...]).astype(BF16)
            hs[...] = hv
            h_ref[...] = hv
        u_ref[...] = _dot(hs[...], w_ref[...])

    return _pcall(
        body, name=name, grid=(s // tm, N_DEV),
        in_specs=[pl.BlockSpec((tm, D_MODEL), lambda i, j: (i, 0)),
                  pl.BlockSpec((1, D_MODEL), lambda i, j: (0, 0)),
                  pl.BlockSpec((None, D_MODEL, COLS_PER_DEV), lambda i, j: (j, 0, 0))],
        out_specs=(pl.BlockSpec((tm, COLS_PER_DEV), lambda i, j: (i, j)),
                   pl.BlockSpec((tm, D_MODEL), lambda i, j: (i, 0))),
        out_shape=(jax.ShapeDtypeStruct((s, N_IN), F32), jax.ShapeDtypeStruct((s, D_MODEL), BF16)),
        scratch_shapes=[pltpu.VMEM((tm, D_MODEL), BF16)],
        compiler_params=_params(("parallel", "arbitrary")),
    )(x, g, w_all)


def _pool_window(vs, t0, t, grp):
    ext = vs[pl.ds(t0, t + POOL_HALO), :]
    s2 = ext + pltpu.roll(ext, 1, 0)
    s4 = s2 + pltpu.roll(s2, 2, 0)
    s8 = s4 + pltpu.roll(s4, 4, 0)
    s16 = s8 + pltpu.roll(s8, 8, 0)
    sel = jnp.where(grp == 0, s2, jnp.where(grp == 1, s4, jnp.where(grp == 2, s8, s16)))
    return sel[POOL_HALO:, :], ext[POOL_HALO:, :]


def _pool_count(t0, t, grp):
    pos = t0 + lax.broadcasted_iota(jnp.int32, (t, 1), 0)
    return jnp.minimum(pos + 1, jnp.left_shift(2, grp)).astype(F32)


def _pool_fwd(u, pool_w, pool_scale, name):
    s = u.shape[0]
    t = min(256, s)

    def body(pv_ref, pg_ref, w_ref, sc_ref, y_ref, vs):
        grp = pl.program_id(0)
        vs[0:POOL_HALO, :] = jnp.zeros((POOL_HALO, LANES), F32)
        vs[POOL_HALO:, :] = pv_ref[...]
        wb = w_ref[...].astype(BF16)
        scale = sc_ref[...]

        def tile(i, carry):
            t0 = pl.multiple_of(i * t, t)
            win, v = _pool_window(vs, t0, t, grp)
            pooled = win / _pool_count(t0, t, grp) - v
            mixed = _dot(pooled.astype(BF16), wb)
            gate = pg_ref[pl.ds(t0, t), :]
            y_ref[pl.ds(t0, t), :] = (mixed * scale * (gate * _sigmoid(gate))).astype(BF16)
            return carry

        lax.fori_loop(0, s // t, tile, 0)

    return _pcall(
        body, name=name, grid=(4,),
        in_specs=[pl.BlockSpec((s, LANES), lambda g: (0, CB_POOL_V + g)),
                  pl.BlockSpec((s, LANES), lambda g: (0, CB_POOL_G + g)),
                  pl.BlockSpec((None, LANES, LANES), lambda g: (g, 0, 0)),
                  pl.BlockSpec((1, LANES), lambda g: (0, g))],
        out_specs=pl.BlockSpec((s, LANES), lambda g: (0, g)),
        out_shape=jax.ShapeDtypeStruct((s, WIDTH), BF16),
        scratch_shapes=[pltpu.VMEM((POOL_HALO + s, LANES), F32)],
        compiler_params=_params(("arbitrary",)),
    )(u, u, pool_w, pool_scale)


def _conv_taps(zs, t0, t):
    ext = zs[pl.ds(t0, t + CONV_HALO), :]
    z0 = ext[CONV_HALO:, :]
    z1 = pltpu.roll(ext, 1, 0)[CONV_HALO:, :]
    z2 = pltpu.roll(ext, 2, 0)[CONV_HALO:, :]
    return z0, z1, z2


def _conv_fwd(u, conv_w, conv_b, name):
    s = u.shape[0]
    t = min(256, s)

    def body(xc_ref, gb_ref, gc_ref, cg_ref, w_ref, b_ref, y_ref, zs):
        zs[0:CONV_HALO, :] = jnp.zeros((CONV_HALO, LANES), F32)
        zs[CONV_HALO:, :] = gc_ref[...] * xc_ref[...]
        w0, w1, w2 = w_ref[0:1, :], w_ref[1:2, :], w_ref[2:3, :]
        bias = b_ref[...]

        def tile(i, carry):
            t0 = pl.multiple_of(i * t, t)
            z0, z1, z2 = _conv_taps(zs, t0, t)
            conv = w0 * z2 + w1 * z1 + w2 * z0
            gate = cg_ref[pl.ds(t0, t), :]
            y = gb_ref[pl.ds(t0, t), :] * (conv + bias) * (gate * _sigmoid(gate))
            y_ref[pl.ds(t0, t), :] = y.astype(BF16)
            return carry

        lax.fori_loop(0, s // t, tile, 0)

    col = lambda base: pl.BlockSpec((s, LANES), lambda j: (0, base + j))
    return _pcall(
        body, name=name, grid=(4,),
        in_specs=[col(CB_CONV_X), col(CB_CONV_GB), col(CB_CONV_GC), col(CB_CONV_G),
                  pl.BlockSpec((3, LANES), lambda j: (0, j)),
                  pl.BlockSpec((1, LANES), lambda j: (0, j))],
        out_specs=pl.BlockSpec((s, LANES), lambda j: (0, j)),
        out_shape=jax.ShapeDtypeStruct((s, WIDTH), BF16),
        scratch_shapes=[pltpu.VMEM((CONV_HALO + s, LANES), F32)],
        compiler_params=_params(("arbitrary",)),
    )(u, u, u, u, conv_w, conv_b)


def _first_head_lanes(rows, width=LANES):
    lane = lax.broadcasted_iota(jnp.int32, (rows, width), 1)
    return jnp.bitwise_and(lane, LANES - 1) < HEAD_DIM


def _stack_heads(x, first):
    zero = jnp.zeros_like(x)
    return jnp.concatenate([jnp.where(first, x, zero), jnp.where(first, zero, x)], axis=0).astype(BF16)


def _causal_mask(tq, tk, copies):
    row = lax.broadcasted_iota(jnp.int32, (tq, tk), 0)
    col = lax.broadcasted_iota(jnp.int32, (tq, tk), 1)
    return jnp.concatenate([col < row] * copies, axis=0)


def _suffix_matrix(tk, inclusive, parts):
    r = lax.broadcasted_iota(jnp.int32, (parts * tk, 2 * tk), 0)
    c = lax.broadcasted_iota(jnp.int32, (parts * tk, 2 * tk), 1)
    r = jnp.bitwise_and(r, tk - 1)
    tri = (r >= c) if inclusive else (r > c)
    return jnp.where(c >= tk, 1.0, jnp.where(tri, 1.0, 0.0)).astype(BF16)


def _suffix_sums(x, m):
    hi, lo = _split_bf16(x)
    return _dot(jnp.concatenate([hi, lo], axis=1), m)


def _sb_log_terms(z, mask, m_strict):
    ls = jnp.minimum(z, 0.0) - jnp.log(1.0 + jnp.exp2(jnp.abs(z) * -LOG2E))
    lk = ls - z
    if mask is not None:
        lk = jnp.where(mask, lk, 0.0)
    return ls, _dot(lk.astype(BF16), m_strict)


SB_PAIRS = 4


def _pair_lanes(a):
    return slice(a * LANES, (a + 1) * LANES)


def _sb_fwd(u, name, xchg=None):
    s = u.shape[0]
    tq = tk = min(128, s)
    pairs = SB_PAIRS
    width = pairs * LANES
    rows = 2 * pairs * tq
    x_arrs, x_gather = xchg if xchg else ((), True)
    n_x = len(x_arrs)
    grid = (4 // pairs, s // tq)

    def body(*refs):
        q_ref, k_ref, v_ref, g_ref = refs[:4]
        x_in, refs = refs[4:4 + n_x], refs[4 + n_x:]
        o_ref, y_ref = refs[:2]
        x_out, refs = refs[2:2 + n_x], refs[2 + n_x:]
        kbf, vst, z_s, ell_s, carry_s = refs[:5]
        x_sems = refs[5:]
        i = pl.program_id(1)
        if n_x:
            @pl.when((pl.program_id(0) == 0) & (i == 0))
            def _():
                _exchange_start(x_in, x_out, x_sems, x_gather)

        @pl.when(i == 0)
        def _():
            kbf[...] = k_ref[...].astype(BF16)
            first_s = _first_head_lanes(s, width)
            vf = v_ref[...]
            vst[0] = jnp.where(first_s, vf, 0.0).astype(BF16)
            vst[1] = jnp.where(first_s, 0.0, vf).astype(BF16)

        first = _first_head_lanes(tq)
        mask = _causal_mask(tq, tk, 2 * pairs)
        m_strict = _suffix_matrix(tk, False, 1)
        qcat = jnp.concatenate([_stack_heads(q_ref[:, _pair_lanes(a)] * SB_SCALE, first) for a in range(pairs)],
                               axis=0)

        def scores(b):
            off = pl.multiple_of(jnp.maximum(b, 0) * tk, tk)
            z_s[...] = jnp.concatenate(
                [_dot_nt(qcat[a * 2 * tq:(a + 1) * 2 * tq], kbf[pl.ds(off, tk), _pair_lanes(a)])
                 for a in range(pairs)], axis=0)

        def log_weights(m):
            ls, cs = _sb_log_terms(z_s[...], m, m_strict)
            carry = carry_s[...]
            ell_s[...] = ls + cs[:, :tk] + carry
            carry_s[...] = carry + cs[:, tk:]

        def consume(b, accs, m):
            w = jnp.exp(ell_s[...])
            if m is not None:
                w = jnp.where(m, w, 0.0)
            wb = w.astype(BF16)
            off = pl.multiple_of(b * tk, tk)
            new = []
            for a in range(pairs):
                r0 = a * 2 * tq
                wcat = jnp.concatenate([wb[r0:r0 + tq], wb[r0 + tq:r0 + 2 * tq]], axis=1)
                vcat = jnp.concatenate([vst[0, pl.ds(off, tk), _pair_lanes(a)], vst[1, pl.ds(off, tk), _pair_lanes(a)]],
                                       axis=0)
                new.append(accs[a] + _dot(wcat, vcat))
            return tuple(new)

        carry_s[...] = jnp.zeros((rows, tk), F32)
        scores(i)
        log_weights(mask)
        scores(i - 1)
        accs = consume(i, tuple(jnp.zeros((tq, LANES), F32) for _ in range(pairs)), mask)
        log_weights(None)
        scores(i - 2)

        def step(n, accs):
            accs = consume(i - n, accs, None)
            log_weights(None)
            scores(i - n - 2)
            return accs

        accs = lax.fori_loop(1, i + 1, step, accs)
        o = jnp.concatenate(accs, axis=1)
        o_ref[...] = o
        gate = g_ref[...]
        y_ref[...] = (o * (gate * _sigmoid(gate))).astype(BF16)
        if n_x:
            @pl.when((pl.program_id(0) == grid[0] - 1) & (i == grid[1] - 1))
            def _():
                _exchange_wait(x_in, x_out, x_sems, x_gather)

    base = lambda cb: cb // pairs
    qblk = lambda cb: pl.BlockSpec((tq, width), lambda p, i: (i, base(cb) + p))
    full = lambda cb: pl.BlockSpec((s, width), lambda p, i: (0, base(cb) + p), pipeline_mode=pl.Buffered(1))
    state = pltpu.VMEM((rows, tk), F32)
    return _pcall(
        body, name=name, grid=grid,
        in_specs=[qblk(CB_SB_Q), full(CB_SB_K), full(CB_SB_V), qblk(CB_SB_G)] + [ANY_SPEC] * n_x,
        out_specs=(qblk(0), qblk(0)) + (ANY_SPEC,) * n_x,
        out_shape=(jax.ShapeDtypeStruct((s, WIDTH), F32), jax.ShapeDtypeStruct((s, WIDTH), BF16))
        + tuple(_exchange_out_shapes(x_arrs, x_gather)),
        scratch_shapes=[pltpu.VMEM((s, width), BF16), pltpu.VMEM((2, s, width), BF16), state, state, state]
        + (_exchange_sems(n_x) if n_x else []),
        compiler_params=_params(("arbitrary", "arbitrary")),
    )(u, u, u, u, *x_arrs)


def _merge_out_fwd(y_pool, y_conv, y_sb, u, wb_all, wo_all, x, g_post, layer, name, target=None):
    s = x.shape[0]
    tm = min(512, s)
    n_tiles = s // tm
    with_loss = target is not None

    def body(yp, yc, ys, m0, m1, m2, wb_ref, wo_ref, x_ref, g_ref, *rest):
        merged = jnp.zeros((tm, D_MODEL), F32)
        for n, (y_ref, m_ref) in enumerate(((yp, m0), (yc, m1), (ys, m2))):
            merged = merged + _sigmoid(m_ref[...]) * _dot(y_ref[...], wb_ref[n])
        mb = merged.astype(BF16)
        pre = _dot(mb, wo_ref[...].reshape(D_MODEL, D_MODEL))
        r = lax.rsqrt(jnp.mean(pre * pre, axis=-1, keepdims=True) + RMS_EPS)
        y = x_ref[...] + pre * r * g_ref[...]
        if not with_loss:
            out_ref, merged_ref, pre_ref = rest
            out_ref[...] = y
        else:
            t_ref, out_ref, merged_ref, pre_ref, loss_ref, acc = rest
            i = pl.program_id(0)

            @pl.when(i == 0)
            def _():
                acc[...] = jnp.zeros_like(acc)
            err = y - t_ref[...]
            out_ref[...] = err / D_MODEL
            acc[...] += jnp.sum(err * err, axis=0, keepdims=True)

            @pl.when(i == n_tiles - 1)
            def _():
                total = jnp.sum(acc[...], axis=1, keepdims=True) * (0.5 / D_MODEL)
                loss_ref[...] = jnp.broadcast_to(total, (1, LANES))
        merged_ref[...] = mb
        pre_ref[...] = pre

    rows = lambda w: pl.BlockSpec((tm, w), lambda i: (i, 0))
    merge = lambda n: pl.BlockSpec((tm, D_MODEL), lambda i: (i, MERGE_BLOCK_1024 + n))
    out_specs = (rows(D_MODEL), rows(D_MODEL), rows(D_MODEL))
    out_shape = (jax.ShapeDtypeStruct((s, D_MODEL), F32), jax.ShapeDtypeStruct((s, D_MODEL), BF16),
                 jax.ShapeDtypeStruct((s, D_MODEL), F32))
    if with_loss:
        out_specs += (pl.BlockSpec((1, LANES), lambda i: (0, 0)),)
        out_shape += (jax.ShapeDtypeStruct((1, LANES), F32),)
    return _pcall(
        body, name=name, grid=(n_tiles,),
        in_specs=[rows(WIDTH), rows(WIDTH), rows(WIDTH), merge(0), merge(1), merge(2),
                  pl.BlockSpec((None, 3, WIDTH, D_MODEL), lambda i: (layer, 0, 0, 0)),
                  pl.BlockSpec((N_DEV, None, D_MODEL // N_DEV, D_MODEL), lambda i: (0, layer, 0, 0)),
                  rows(D_MODEL), pl.BlockSpec((1, D_MODEL), lambda i: (0, 0))] + ([rows(D_MODEL)] if with_loss else []),
        out_specs=out_specs, out_shape=out_shape,
        scratch_shapes=[pltpu.VMEM((1, D_MODEL), F32)] if with_loss else [],
        compiler_params=_params(("arbitrary",)),
    )(y_pool, y_conv, y_sb, u, u, u, wb_all, wo_all, x, g_post, *([target] if with_loss else []))


def _out_proj_bwd(dy, pre, g_post, merged, wo_all, layer, name):
    s = dy.shape[0]
    tm = min(512, s)
    n_tiles = s // tm

    def body(dy_ref, pre_ref, g_ref, mg_ref, wo_ref, dm_ref, dwo_ref, dg_ref, acc):
        i = pl.program_id(0)

        @pl.when(i == 0)
        def _():
            acc[...] = jnp.zeros_like(acc)
            dg_ref[...] = jnp.zeros_like(dg_ref)
        dyv, pre_v = dy_ref[...], pre_ref[...]
        r = lax.rsqrt(jnp.mean(pre_v * pre_v, axis=-1, keepdims=True) + RMS_EPS)
        dg_ref[...] += jnp.sum(dyv * pre_v * r, axis=0, keepdims=True)
        a = dyv * g_ref[...]
        dpre = r * a - pre_v * (r * r * r) * jnp.mean(a * pre_v, axis=-1, keepdims=True)
        db = dpre.astype(BF16)
        acc[...] += _dot_tn(mg_ref[...], db)
        dm_ref[...] = _dot_nt(db, wo_ref[...].reshape(D_MODEL, D_MODEL))

        @pl.when(i == n_tiles - 1)
        def _():
            dwo_ref[...] = acc[...].astype(BF16)

    rows = lambda: pl.BlockSpec((tm, D_MODEL), lambda i: (i, 0))
    return _pcall(
        body, name=name, grid=(n_tiles,),
        in_specs=[rows(), rows(), pl.BlockSpec((1, D_MODEL), lambda i: (0, 0)), rows(),
                  pl.BlockSpec((N_DEV, None, D_MODEL // N_DEV, D_MODEL), lambda i: (0, layer, 0, 0))],
        out_specs=(rows(), pl.BlockSpec((D_MODEL, D_MODEL), lambda i: (0, 0)),
                   pl.BlockSpec((1, D_MODEL), lambda i: (0, 0))),
        out_shape=(jax.ShapeDtypeStruct((s, D_MODEL), F32), jax.ShapeDtypeStruct((D_MODEL, D_MODEL), BF16),
                   jax.ShapeDtypeStruct((1, D_MODEL), F32)),
        scratch_shapes=[pltpu.VMEM((D_MODEL, D_MODEL), F32)],
        compiler_params=_params(("arbitrary",)),
    )(dy, pre, g_post, merged, wo_all)


def _merge_bwd(dmerged, y_pool, y_conv, y_sb, u, wb_all, layer, name):
    s = dmerged.shape[0]
    tm = min(512, s)
    n_tiles = s // tm
    cols = D_MODEL // N_DEV

    def body(dm_ref, yp, yc, ys, m0, m1, m2, wb_ref, du_ref, dyp, dyc, dys, dwb_ref, acc):
        i = pl.program_id(0)

        @pl.when(i == 0)
        def _():
            acc[...] = jnp.zeros_like(acc)
        dm = dm_ref[...]
        for n, (y_ref, m_ref, dy_ref) in enumerate(((yp, m0, dyp), (yc, m1, dyc), (ys, m2, dys))):
            yv = y_ref[...]
            wb = wb_ref[n]
            gate = _sigmoid(m_ref[...])
            proj = _dot(yv, wb)
            dgate = (dm * proj * gate * (1.0 - gate)).astype(BF16)
            du_ref[2 * n] = dgate[:, :WIDTH]
            du_ref[2 * n + 1] = dgate[:, WIDTH:]
            dproj = (dm * gate).astype(BF16)
            acc[n] += _dot_tn(yv, dproj)
            dy_ref[...] = _dot_nt(dproj, wb)

        @pl.when(i == n_tiles - 1)
        def _():
            for j in range(N_DEV):
                for n in range(3):
                    dwb_ref[j, n] = acc[n, :, j * cols:(j + 1) * cols].astype(BF16)

    rows = lambda w: pl.BlockSpec((tm, w), lambda i: (i, 0))
    merge = lambda n: pl.BlockSpec((tm, D_MODEL), lambda i: (i, MERGE_BLOCK_1024 + n))
    return _pcall(
        body, name=name, grid=(n_tiles,),
        in_specs=[rows(D_MODEL), rows(WIDTH), rows(WIDTH), rows(WIDTH), merge(0), merge(1), merge(2),
                  pl.BlockSpec((None, 3, WIDTH, D_MODEL), lambda i: (layer, 0, 0, 0))],
        out_specs=(pl.BlockSpec((DU_MERGE[1], tm, WIDTH), lambda i: (DU_MERGE[0] // DU_MERGE[1], i, 0)),
                   rows(WIDTH), rows(WIDTH), rows(WIDTH),
                   pl.BlockSpec((N_DEV, 3, WIDTH, cols), lambda i: (0, 0, 0, 0))),
        out_shape=(jax.ShapeDtypeStruct((DU_PIECES, s, WIDTH), BF16),
                   jax.ShapeDtypeStruct((s, WIDTH), F32), jax.ShapeDtypeStruct((s, WIDTH), F32),
                   jax.ShapeDtypeStruct((s, WIDTH), F32),
                   jax.ShapeDtypeStruct((N_DEV, 3, WIDTH, cols), BF16)),
        scratch_shapes=[pltpu.VMEM((3, WIDTH, D_MODEL), F32)],
        compiler_params=_params(("arbitrary",)),
    )(dmerged, y_pool, y_conv, y_sb, u, u, u, wb_all)


def _sb_bwd(u, o, dys, du, name, xchg=None):
    s = u.shape[0]
    tq = tk = min(128, s)
    pairs = SB_PAIRS
    width = pairs * LANES
    assert width == WIDTH
    rows = 2 * pairs * tq
    pair_rows = lambda a: slice(a * 2 * tq, (a + 1) * 2 * tq)

    x_arrs, x_gather = xchg if xchg else ((), True)
    n_x = len(x_arrs)
    grid = (4 // pairs, s // tq)

    def body(*refs):
        q_ref, k_ref, v_ref, g_ref, o_ref, dys_ref = refs[:6]
        x_in, refs = refs[7:7 + n_x], refs[7 + n_x:]
        du_ref, dk_ref, dv_ref = refs[:3]
        dq_ref, dg_ref = du_ref.at[0], du_ref.at[1]
        x_out, refs = refs[3:3 + n_x], refs[3 + n_x:]
        kbf, vbf, kst, z_s, ell_s, ls_s, cl_s, wb_s, g_s, bef_s, cg_s, beta_s = refs[:12]
        x_sems = refs[12:]
        i = pl.program_id(1)
        if n_x:
            @pl.when((pl.program_id(0) == 0) & (i == 0))
            def _():
                _exchange_start(x_in, x_out, x_sems, x_gather)

        @pl.when(i == 0)
        def _():
            dk_ref[...] = jnp.zeros_like(dk_ref)
            dv_ref[...] = jnp.zeros_like(dv_ref)
            kf = k_ref[...]
            kbf[...] = kf.astype(BF16)
            vbf[...] = v_ref[...].astype(BF16)
            first_s = _first_head_lanes(s, width)
            kst[0] = jnp.where(first_s, kf, 0.0).astype(BF16)
            kst[1] = jnp.where(first_s, 0.0, kf).astype(BF16)

        first = _first_head_lanes(tq)
        mask = _causal_mask(tq, tk, 2 * pairs)
        m_strict = _suffix_matrix(tk, False, 1)
        m_incl = _suffix_matrix(tk, True, 2)

        gate = g_ref[...]
        sg = _sigmoid(gate)
        dy = dys_ref[...]
        ov = o_ref[...]
        dg_ref[...] = (dy * ov * (sg * (1.0 + gate * (1.0 - sg)))).astype(BF16)
        do = (dy * (gate * sg)).astype(BF16)
        prod = do.astype(F32) * ov
        row_sum = lambda v: jnp.broadcast_to(jnp.sum(v, axis=1, keepdims=True), (tq, tk))
        dsum, docat, qcat = [], [], []
        for a in range(pairs):
            pa = prod[:, _pair_lanes(a)]
            dsum += [row_sum(jnp.where(first, pa, 0.0)), row_sum(jnp.where(first, 0.0, pa))]
            docat.append(_stack_heads(do[:, _pair_lanes(a)], first))
            qcat.append(_stack_heads(q_ref[:, _pair_lanes(a)] * SB_SCALE, first))
        dsum = jnp.concatenate(dsum, axis=0)

        def block_start(b):
            return pl.multiple_of(jnp.maximum(b, 0) * tk, tk)

        def scores(b):
            off = block_start(b)
            z_s[...] = jnp.concatenate([_dot_nt(qcat[a], kbf[pl.ds(off, tk), _pair_lanes(a)]) for a in range(pairs)],
                                       axis=0)

        def log_weights(m):
            ls, cs = _sb_log_terms(z_s[...], m, m_strict)
            cl = cl_s[...]
            ell_s[...] = ls + cs[:, :tk] + cl
            cl_s[...] = cl + cs[:, tk:]
            ls_s[...] = ls

        def weights(b, m):
            off = block_start(b)
            dwt = jnp.concatenate([_dot_nt(docat[a], vbf[pl.ds(off, tk), _pair_lanes(a)]) for a in range(pairs)],
                                  axis=0)
            w = jnp.exp(ell_s[...])
            if m is not None:
                w = jnp.where(m, w, 0.0)
            wb = w.astype(BF16)
            g = dwt * wb.astype(F32)
            gs = _suffix_sums(g, m_incl)
            cg = cg_s[...]
            beta = jnp.exp(ls_s[...])
            wb_s[...] = wb
            beta_s[...] = beta
            g_s[...] = g * (1.0 - beta)
            bef_s[...] = gs[:, :tk] + cg
            cg_s[...] = cg + gs[:, tk:]

        def grads(b, dqs, m):
            dz = g_s[...] - beta_s[...] * (dsum - bef_s[...])
            if m is not None:
                dz = jnp.where(m, dz, 0.0)
            dzb = dz.astype(BF16)
            wb = wb_s[...]
            off = pl.multiple_of(b * tk, tk)
            new = []
            for a in range(pairs):
                r0 = a * 2 * tq
                kcat = jnp.concatenate([kst[0, pl.ds(off, tk), _pair_lanes(a)], kst[1, pl.ds(off, tk), _pair_lanes(a)]],
                                       axis=0)
                new.append(dqs[a] + _dot(jnp.concatenate([dzb[r0:r0 + tq], dzb[r0 + tq:r0 + 2 * tq]], axis=1), kcat))
                dk_ref[pl.ds(off, tk), _pair_lanes(a)] += _dot_tn(dzb[pair_rows(a)], qcat[a])
                dv_ref[pl.ds(off, tk), _pair_lanes(a)] += _dot_tn(wb[pair_rows(a)], docat[a])
            return tuple(new)

        zero = jnp.zeros((rows, tk), F32)
        cl_s[...] = zero
        cg_s[...] = zero
        scores(i)
        log_weights(mask)
        scores(i - 1)
        weights(i, mask)
        log_weights(None)
        scores(i - 2)
        dqs = grads(i, tuple(jnp.zeros((tq, LANES), F32) for _ in range(pairs)), mask)
        weights(i - 1, None)
        log_weights(None)
        scores(i - 3)

        def step(n, dqs):
            dqs = grads(i - n, dqs, None)
            weights(i - n - 1, None)
            log_weights(None)
            scores(i - n - 3)
            return dqs

        dqs = lax.fori_loop(1, i + 1, step, dqs)
        dq_ref[...] = (jnp.concatenate(dqs, axis=1) * SB_SCALE).astype(BF16)
        if n_x:
            @pl.when((pl.program_id(0) == grid[0] - 1) & (i == grid[1] - 1))
            def _():
                _exchange_wait(x_in, x_out, x_sems, x_gather)

    base = lambda cb: cb // pairs
    qblk = lambda cb: pl.BlockSpec((tq, width), lambda p, i: (i, base(cb) + p))
    full = lambda cb: pl.BlockSpec((s, width), lambda p, i: (0, base(cb) + p), pipeline_mode=pl.Buffered(1))
    state = pltpu.VMEM((rows, tk), F32)
    return _pcall(
        body, name=name, grid=grid,
        in_specs=[qblk(CB_SB_Q), full(CB_SB_K), full(CB_SB_V), qblk(CB_SB_G), qblk(0), qblk(0), ANY_SPEC]
        + [ANY_SPEC] * n_x,
        out_specs=(pl.BlockSpec((DU_SB_QG[1], tq, WIDTH), lambda p, i: (DU_SB_QG[0] // DU_SB_QG[1], i, 0)),
                   full(0), full(0)) + (ANY_SPEC,) * n_x,
        out_shape=(jax.ShapeDtypeStruct(du.shape, du.dtype), jax.ShapeDtypeStruct((s, WIDTH), F32),
                   jax.ShapeDtypeStruct((s, WIDTH), F32)) + tuple(_exchange_out_shapes(x_arrs, x_gather)),
        input_output_aliases={6: 0},
        scratch_shapes=[pltpu.VMEM((s, width), BF16), pltpu.VMEM((s, width), BF16), pltpu.VMEM((2, s, width), BF16),
                        state, state, state, state, pltpu.VMEM((rows, tk), BF16),
                        state, state, state, state] + (_exchange_sems(n_x) if n_x else []),
        compiler_params=_params(("arbitrary", "arbitrary")),
    )(u, u, u, u, o, dys, du, *x_arrs)


def _conv_bwd(u, conv_w, conv_b, dyc, du, name):
    s = u.shape[0]
    t = min(256, s)
    n_tiles = s // t

    def body(xc_ref, gb_ref, gc_ref, cg_ref, w_ref, b_ref, dy_ref, du_in, du_ref, dw_ref, db_ref, zs, ds):
        dxc_ref, dgb_ref, dgc_ref, dcg_ref = (du_ref.at[p] for p in range(4))
        zs[0:CONV_HALO, :] = jnp.zeros((CONV_HALO, LANES), F32)
        zs[CONV_HALO:, :] = gc_ref[...] * xc_ref[...]
        ds[s:, :] = jnp.zeros((CONV_HALO, LANES), F32)
        w0, w1, w2 = w_ref[0:1, :], w_ref[1:2, :], w_ref[2:3, :]
        bias = b_ref[...]

        def first(i, sums):
            t0 = pl.multiple_of(i * t, t)
            z0, z1, z2 = _conv_taps(zs, t0, t)
            pre = w0 * z2 + w1 * z1 + w2 * z0 + bias
            gate = cg_ref[pl.ds(t0, t), :]
            sg = _sigmoid(gate)
            gb = gb_ref[pl.ds(t0, t), :]
            dy = dy_ref[pl.ds(t0, t), :]
            dcg_ref[pl.ds(t0, t), :] = (dy * gb * pre * (sg * (1.0 + gate * (1.0 - sg)))).astype(BF16)
            dgb_ref[pl.ds(t0, t), :] = (dy * pre * (gate * sg)).astype(BF16)
            dc = dy * gb * (gate * sg)
            ds[pl.ds(t0, t), :] = dc
            red = lambda v: jnp.sum(v, axis=0, keepdims=True)
            return (sums[0] + red(dc * z2), sums[1] + red(dc * z1), sums[2] + red(dc * z0), sums[3] + red(dc))

        zrow = jnp.zeros((1, LANES), F32)
        sw0, sw1, sw2, sb = lax.fori_loop(0, n_tiles, first, (zrow, zrow, zrow, zrow))
        dw_ref[0:1, :] = sw0
        dw_ref[1:2, :] = sw1
        dw_ref[2:3, :] = sw2
        db_ref[...] = sb

        def second(i, carry):
            t0 = pl.multiple_of(i * t, t)
            ext = ds[pl.ds(t0, t + CONV_HALO), :]
            n = t + CONV_HALO
            d0 = ext[:t, :]
            d1 = pltpu.roll(ext, n - 1, 0)[:t, :]
            d2 = pltpu.roll(ext, n - 2, 0)[:t, :]
            dz = w2 * d0 + w1 * d1 + w0 * d2
            dgc_ref[pl.ds(t0, t), :] = (dz * xc_ref[pl.ds(t0, t), :]).astype(BF16)
            dxc_ref[pl.ds(t0, t), :] = (dz * gc_ref[pl.ds(t0, t), :]).astype(BF16)
            return carry

        lax.fori_loop(0, n_tiles, second, 0)

    col = lambda base: pl.BlockSpec((s, LANES), lambda j: (0, base + j))
    first, count = DU_CONV
    return _pcall(
        body, name=name, grid=(4,),
        in_specs=[col(CB_CONV_X), col(CB_CONV_GB), col(CB_CONV_GC), col(CB_CONV_G),
                  pl.BlockSpec((3, LANES), lambda j: (0, j)), pl.BlockSpec((1, LANES), lambda j: (0, j)), col(0),
                  ANY_SPEC],
        out_specs=(pl.BlockSpec((count, s, LANES), lambda j: (first // count, 0, j)),
                   pl.BlockSpec((3, LANES), lambda j: (0, j)), pl.BlockSpec((1, LANES), lambda j: (0, j))),
        out_shape=(jax.ShapeDtypeStruct(du.shape, du.dtype),
                   jax.ShapeDtypeStruct((3, WIDTH), F32), jax.ShapeDtypeStruct((1, WIDTH), F32)),
        scratch_shapes=[pltpu.VMEM((CONV_HALO + s, LANES), F32), pltpu.VMEM((s + CONV_HALO, LANES), F32)],
        input_output_aliases={7: 0},
        compiler_params=_params(("arbitrary",)),
    )(u, u, u, u, conv_w, conv_b, dyc, du)


def _pool_bwd(u, pool_w, pool_scale, dyp, du, name):
    s = u.shape[0]
    t = min(256, s)
    n_tiles = s // t

    def body(pv_ref, pg_ref, w_ref, sc_ref, dy_ref, du_in, du_ref, dw_ref, dsc_ref, vs, es, dps):
        dpv_ref, dpg_ref = du_ref.at[0], du_ref.at[1]
        grp = pl.program_id(0)
        vs[0:POOL_HALO, :] = jnp.zeros((POOL_HALO, LANES), F32)
        vs[POOL_HALO:, :] = pv_ref[...]
        es[s:, :] = jnp.zeros((POOL_HALO, LANES), F32)
        wb = w_ref[...].astype(BF16)
        scale = sc_ref[...]

        def first(i, sums):
            dw, dsc = sums
            t0 = pl.multiple_of(i * t, t)
            win, v = _pool_window(vs, t0, t, grp)
            cnt = _pool_count(t0, t, grp)
            pb = (win / cnt - v).astype(BF16)
            mixed = _dot(pb, wb)
            gate = pg_ref[pl.ds(t0, t), :]
            sg = _sigmoid(gate)
            dy = dy_ref[pl.ds(t0, t), :]
            dpg_ref[pl.ds(t0, t), :] = (dy * (mixed * scale) * (sg * (1.0 + gate * (1.0 - sg)))).astype(BF16)
            dms = dy * (gate * sg)
            dsc = dsc + jnp.sum(dms * mixed, axis=0, keepdims=True)
            dmb = (dms * scale).astype(BF16)
            dw = dw + _dot_tn(pb, dmb)
            dpooled = _dot_nt(dmb, wb)
            dps[pl.ds(t0, t), :] = dpooled
            es[pl.ds(t0, t), :] = dpooled / cnt
            return dw, dsc

        dw, dsc = lax.fori_loop(0, n_tiles, first, (jnp.zeros((LANES, LANES), F32), jnp.zeros((1, LANES), F32)))
        dw_ref[...] = dw
        dsc_ref[...] = dsc

        def second(i, carry):
            t0 = pl.multiple_of(i * t, t)
            ext = es[pl.ds(t0, t + POOL_HALO), :]
            n = t + POOL_HALO
            f2 = ext + pltpu.roll(ext, n - 1, 0)
            f4 = f2 + pltpu.roll(f2, n - 2, 0)
            f8 = f4 + pltpu.roll(f4, n - 4, 0)
            f16 = f8 + pltpu.roll(f8, n - 8, 0)
            sel = jnp.where(grp == 0, f2, jnp.where(grp == 1, f4, jnp.where(grp == 2, f8, f16)))
            dpv_ref[pl.ds(t0, t), :] = (sel[:t, :] - dps[pl.ds(t0, t), :]).astype(BF16)
            return carry

        lax.fori_loop(0, n_tiles, second, 0)

    col = lambda base: pl.BlockSpec((s, LANES), lambda g: (0, base + g))
    first, count = DU_POOL
    return _pcall(
        body, name=name, grid=(4,),
        in_specs=[col(CB_POOL_V), col(CB_POOL_G), pl.BlockSpec((None, LANES, LANES), lambda g: (g, 0, 0)),
                  pl.BlockSpec((1, LANES), lambda g: (0, g)), col(0), ANY_SPEC],
        out_specs=(pl.BlockSpec((count, s, LANES), lambda g: (first // count, 0, g)),
                   pl.BlockSpec((None, LANES, LANES), lambda g: (g, 0, 0)),
                   pl.BlockSpec((1, LANES), lambda g: (0, g))),
        out_shape=(jax.ShapeDtypeStruct(du.shape, du.dtype),
                   jax.ShapeDtypeStruct((4, LANES, LANES), F32), jax.ShapeDtypeStruct((1, WIDTH), F32)),
        scratch_shapes=[pltpu.VMEM((POOL_HALO + s, LANES), F32), pltpu.VMEM((s + POOL_HALO, LANES), F32),
                        pltpu.VMEM((s, LANES), F32)],
        input_output_aliases={5: 0},
        compiler_params=_params(("arbitrary",)),
    )(u, u, pool_w, pool_scale, dyp, du)


def _in_proj_bwd_x(du, w_all, x, g_pre, dy, name):
    s = x.shape[0]
    tm = min(1024, s)
    grid = (s // tm, N_DEV)

    def body(dua_ref, dub_ref, w_ref, x_ref, g_ref, dy_ref, dx_ref, dg_ref, acc):
        i, k = pl.program_id(0), pl.program_id(1)

        @pl.when(k == 0)
        def _():
            acc[...] = jnp.zeros_like(acc)

        @pl.when((k == 0) & (i == 0))
        def _():
            dg_ref[...] = jnp.zeros_like(dg_ref)
        acc[...] += _dot_nt(jnp.concatenate([dua_ref[...], dub_ref[...]], axis=1), w_ref[...])

        @pl.when(k == N_DEV - 1)
        def _():
            dh, xv = acc[...], x_ref[...]
            r = lax.rsqrt(jnp.mean(xv * xv, axis=-1, keepdims=True) + RMS_EPS)
            dg_ref[...] += jnp.sum(dh * xv * r, axis=0, keepdims=True)
            a = dh * g_ref[...]
            dx_ref[...] = dy_ref[...] + r * a - xv * (r * r * r) * jnp.mean(a * xv, axis=-1, keepdims=True)

    rows = lambda: pl.BlockSpec((tm, D_MODEL), lambda i, k: (i, 0))
    vec = lambda: pl.BlockSpec((1, D_MODEL), lambda i, k: (0, 0))
    piece = lambda half: pl.BlockSpec((None, tm, WIDTH), lambda i, k: (_du_pieces_of_block(k)[half], i, 0))
    return _pcall(
        body, name=name, grid=grid,
        in_specs=[piece(0), piece(1), pl.BlockSpec((None, D_MODEL, COLS_PER_DEV), lambda i, k: (k, 0, 0)),
                  rows(), vec(), rows()],
        out_specs=(rows(), vec()),
        out_shape=(jax.ShapeDtypeStruct((s, D_MODEL), F32), jax.ShapeDtypeStruct((1, D_MODEL), F32)),
        scratch_shapes=[pltpu.VMEM((tm, D_MODEL), F32)],
        compiler_params=_params(("arbitrary", "arbitrary")),
    )(du, du, w_all, x, g_pre, dy)


ROW_OFFSETS = (2, 4, 3, 5, 6, 7, 0, 1)


def _in_proj_bwd_send(h, du, w_all, x, g_pre, dy, name):
    s = x.shape[0]
    tk = s // N_DEV
    tm = min(1024, s)
    n_i = s // tm
    grid = (N_DEV + n_i, N_DEV)
    last = N_DEV - 1
    def offset(row):
        return functools.reduce(lambda acc, rn: jnp.where(row == rn[0], rn[1], acc), enumerate(ROW_OFFSETS), 0)

    def body(me_ref, h_ref, duwa_ref, duwb_ref, duxa_ref, duxb_ref, w_ref, x_ref, g_ref, dy_ref,
             dx_ref, dg_ref, recv_ref, part_ref, acc_w, stage, acc_x, send_sems, recv_sems, park_sems):
        r, k = pl.program_id(0), pl.program_id(1)
        x_, y_, c_ = lax.axis_index("x"), lax.axis_index("y"), lax.axis_index("c")
        me = 4 * x_ + 2 * y_ + c_
        flip = lambda v, bit: 1 - v if bit else v
        peer = lambda n: (flip(x_, (n >> 2) & 1), flip(y_, (n >> 1) & 1), flip(c_, n & 1))

        def park(row):
            n = ROW_OFFSETS[row]
            dst = recv_ref.at[me] if n == 0 else part_ref.at[n]
            return pltpu.make_async_copy(stage.at[row % 2], dst, park_sems.at[row % 2])

        def send(n, landing=False):
            px, py, pc = peer(n)
            dst = recv_ref.at[4 * px + 2 * py + pc] if landing else recv_ref.at[me]
            return pltpu.make_async_remote_copy(
                src_ref=part_ref.at[n], dst_ref=dst, send_sem=send_sems.at[n], recv_sem=recv_sems.at[n],
                device_id=(px, py, pc), device_id_type=pl.DeviceIdType.MESH)

        def parked(row):
            park(row).wait()
            if ROW_OFFSETS[row] >= 1:
                send(ROW_OFFSETS[row]).start()

        @pl.when(r < N_DEV)
        def _():
            @pl.when(k == 0)
            def _():
                acc_w[...] = jnp.zeros_like(acc_w)
            acc_w[...] += _dot_tn(h_ref[...], jnp.concatenate([duwa_ref[...], duwb_ref[...]], axis=1))

            for row in range(N_DEV):
                @pl.when((k == last) & (r == row))
                def _():
                    if row >= 1:
                        parked(row - 1)
                    stage[row % 2] = acc_w[...].astype(BF16)
                    park(row).start()

        @pl.when(r >= N_DEV)
        def _():
            @pl.when(k == 0)
            def _():
                acc_x[...] = jnp.zeros_like(acc_x)

            @pl.when((k == 0) & (r == N_DEV))
            def _():
                dg_ref[...] = jnp.zeros_like(dg_ref)
                parked(last)
            acc_x[...] += _dot_nt(jnp.concatenate([duxa_ref[...], duxb_ref[...]], axis=1), w_ref[...])

            @pl.when(k == last)
            def _():
                dh, xv = acc_x[...], x_ref[...]
                rs = lax.rsqrt(jnp.mean(xv * xv, axis=-1, keepdims=True) + RMS_EPS)
                dg_ref[...] += jnp.sum(dh * xv * rs, axis=0, keepdims=True)
                a = dh * g_ref[...]
                dx_ref[...] = dy_ref[...] + rs * a - xv * (rs * rs * rs) * jnp.mean(a * xv, axis=-1, keepdims=True)

        @pl.when((r == grid[0] - 1) & (k == last))
        def _():
            for n in range(1, N_DEV):
                send(n).wait_send()
            for n in range(1, N_DEV):
                send(n, landing=True).wait_recv()

    in_w = lambda r: r < N_DEV
    row_x = lambda r: jnp.maximum(r - N_DEV, 0)
    rows = lambda: pl.BlockSpec((tm, D_MODEL), lambda r, k, me: (row_x(r), 0))
    vec = lambda: pl.BlockSpec((1, D_MODEL), lambda r, k, me: (0, 0))
    block_w = lambda r, me: jnp.bitwise_xor(me[0], offset(jnp.minimum(r, last)))
    block_x = lambda r, k: jnp.where(in_w(r), 0, k)
    piece_w = lambda half: pl.BlockSpec(
        (None, tk, WIDTH), lambda r, k, me: (_du_pieces_of_block(block_w(r, me))[half], jnp.where(in_w(r), k, last), 0))
    piece_x = lambda half: pl.BlockSpec(
        (None, tm, WIDTH), lambda r, k, me: (_du_pieces_of_block(block_x(r, k))[half], row_x(r), 0))
    grid_spec = pltpu.PrefetchScalarGridSpec(
        num_scalar_prefetch=1, grid=grid,
        in_specs=[pl.BlockSpec((tk, D_MODEL), lambda r, k, me: (jnp.where(in_w(r), k, last), 0)),
                  piece_w(0), piece_w(1), piece_x(0), piece_x(1),
                  pl.BlockSpec((None, D_MODEL, COLS_PER_DEV), lambda r, k, me: (block_x(r, k), 0, 0)),
                  rows(), vec(), rows()],
        out_specs=(rows(), vec(), ANY_SPEC, ANY_SPEC),
        scratch_shapes=[pltpu.VMEM((D_MODEL, COLS_PER_DEV), F32), pltpu.VMEM((2, D_MODEL, COLS_PER_DEV), BF16),
                        pltpu.VMEM((tm, D_MODEL), F32), pltpu.SemaphoreType.DMA((N_DEV,)),
                        pltpu.SemaphoreType.DMA((N_DEV,)), pltpu.SemaphoreType.DMA((2,))])
    me = 4 * lax.axis_index("x") + 2 * lax.axis_index("y") + lax.axis_index("c")
    blocks = jax.ShapeDtypeStruct((N_DEV, D_MODEL, COLS_PER_DEV), BF16)
    dx, dg, received, _ = _pcall(
        body, name=name, grid_spec=grid_spec,
        out_shape=(jax.ShapeDtypeStruct((s, D_MODEL), F32), jax.ShapeDtypeStruct((1, D_MODEL), F32), blocks, blocks),
        compiler_params=_params(("arbitrary", "arbitrary")),
    )(jnp.reshape(me, (1,)).astype(jnp.int32), h, du, du, du, du, w_all, x, g_pre, dy)
    return dx, dg, received


def _in_proj_bwd_w(h, du, name):
    s = h.shape[0]
    tk = min(512, s)
    n_k = s // tk

    def body(h_ref, dua_ref, dub_ref, out_ref, acc):
        k = pl.program_id(1)

        @pl.when(k == 0)
        def _():
            acc[...] = jnp.zeros_like(acc)
        acc[...] += _dot_tn(h_ref[...], jnp.concatenate([dua_ref[...], dub_ref[...]], axis=1))

        @pl.when(k == n_k - 1)
        def _():
            out_ref[...] = acc[...].astype(BF16)

    piece = lambda half: pl.BlockSpec((None, tk, WIDTH), lambda j, k: (_du_pieces_of_block(j)[half], k, 0))
    return _pcall(
        body, name=name, grid=(N_DEV, n_k),
        in_specs=[pl.BlockSpec((tk, D_MODEL), lambda j, k: (k, 0)), piece(0), piece(1)],
        out_specs=pl.BlockSpec((None, D_MODEL, COLS_PER_DEV), lambda j, k: (j, 0, 0)),
        out_shape=jax.ShapeDtypeStruct((N_DEV, D_MODEL, COLS_PER_DEV), BF16),
        scratch_shapes=[pltpu.VMEM((D_MODEL, COLS_PER_DEV), F32)],
        compiler_params=_params(("parallel", "arbitrary")),
    )(h, du, du)


def _adamw_math(g, w, m, v):
    m_new = ADAM_B1 * m + (1.0 - ADAM_B1) * g
    v_new = ADAM_B2 * v + (1.0 - ADAM_B2) * (g * g)
    m_hat = m_new / (1.0 - ADAM_B1 ** ADAM_STEP)
    v_hat = v_new / (1.0 - ADAM_B2 ** ADAM_STEP)
    delta = -ADAM_LR * (m_hat / (jnp.sqrt(v_hat) + ADAM_EPS) + ADAM_WD * w)
    return delta, m_new, v_new


def _sum_partials(p_ref):
    total = p_ref[0].astype(F32)
    for d in range(1, N_DEV):
        total = total + p_ref[d].astype(F32)
    return total


def _adamw_layers(parts0, parts1, w, m, v, name):
    _, r, c = w.shape
    tr = min(128, r)
    n_r = r // tr

    def body(p0_ref, p1_ref, w_ref, m_ref, v_ref, g_ref, d_ref, mo_ref, vo_ref):
        layer = pl.program_id(0)

        @pl.when(layer == 0)
        def _():
            g_ref[...] = _sum_partials(p0_ref)

        @pl.when(layer == 1)
        def _():
            g_ref[...] = _sum_partials(p1_ref)
        d_ref[...], mo_ref[...], vo_ref[...] = _adamw_math(g_ref[...], w_ref[...], m_ref[...], v_ref[...])

    part = lambda which: pl.BlockSpec((N_DEV, tr, c), lambda l, i: (0, jnp.where(l == which, i, 0), 0))
    par = lambda: pl.BlockSpec((None, tr, c), lambda l, i: (l, i, 0))
    out = jax.ShapeDtypeStruct(w.shape, F32)
    return _pcall(
        body, name=name, grid=(2, n_r),
        in_specs=[part(0), part(1), par(), par(), par()],
        out_specs=(par(), par(), par(), par()),
        out_shape=(out, out, out, out),
        compiler_params=_params(("arbitrary", "arbitrary")),
    )(parts0, parts1, w, m, v)


def _adamw_small(parts, w, m, v, name):
    def body(p_ref, w_ref, m_ref, v_ref, g_ref, d_ref, mo_ref, vo_ref):
        g = _sum_partials(p_ref)
        g_ref[...] = g
        d_ref[...], mo_ref[...], vo_ref[...] = _adamw_math(g, w_ref[...], m_ref[...], v_ref[...])

    out = jax.ShapeDtypeStruct(w.shape, F32)
    return _pcall(body, name=name, out_shape=(out, out, out, out), compiler_params=_params())(parts, w, m, v)


def _adamw_plain(g, w, m, v, name):
    def body(g_ref, w_ref, m_ref, v_ref, d_ref, mo_ref, vo_ref):
        d_ref[...], mo_ref[...], vo_ref[...] = _adamw_math(g_ref[...], w_ref[...], m_ref[...], v_ref[...])

    out = jax.ShapeDtypeStruct(w.shape, F32)
    return _pcall(body, name=name, out_shape=(out, out, out), compiler_params=_params())(g, w, m, v)


def _rows128(a):
    return a.reshape(-1, LANES)


SMALL_NAMES = ("pre_norm_g", "pool_w", "pool_scale", "conv_w", "conv_b", "post_norm_g")


def kernel(x, pre_norm_g, w_in, pool_w, pool_scale, conv_w, conv_b, w_branch, w_out, post_norm_g, loss_target, m_pre_norm_g, m_w_in, m_pool_w, m_pool_scale, m_conv_w, m_conv_b, m_w_branch, m_w_out, m_post_norm_g, v_pre_norm_g, v_w_in, v_pool_w, v_pool_scale, v_conv_w, v_conv_b, v_w_branch, v_w_out, v_post_norm_g):
    me = 4 * lax.axis_index("x") + 2 * lax.axis_index("y") + lax.axis_index("c")
    x0 = x[0]
    target = loss_target[0]
    conv_cols = conv_w.shape[-1]

    conv_w_pad = jnp.pad(conv_w.reshape(2 * 3, conv_cols), ((0, 2), (0, LANES - conv_cols)))
    w_in_all = [None, None]
    w_in_all[0], cw_g = _gather_two_level([w_in[0].astype(BF16), conv_w_pad], "gather_w_in_0")
    conv_w_full = cw_g[:, :6, :conv_cols].reshape(N_DEV, 2, 3, conv_cols).transpose(1, 2, 0, 3).reshape(2, 3, WIDTH)
    later_weights = ([w_in[1].astype(BF16), w_branch.astype(BF16), w_out.astype(BF16)], True)

    saved = []
    xin = x0
    for l in range(2):
        u, h = _in_proj_fwd(xin, pre_norm_g[l:l + 1], w_in_all[l], f"in_proj_fwd_{l}")
        y_pool = _pool_fwd(u, pool_w[l], pool_scale[l:l + 1], f"pool_fwd_{l}")
        y_conv = _conv_fwd(u, conv_w_full[l], conv_b[l:l + 1], f"conv_fwd_{l}")
        if l == 0:
            o_sb, y_sb, w_in_all[1], wb_g, wo_all = _sb_fwd(u, f"sb_fwd_{l}", later_weights)
            wb_all = wb_g.transpose(1, 2, 3, 0, 4).reshape(2, 3, WIDTH, D_MODEL)
        else:
            o_sb, y_sb = _sb_fwd(u, f"sb_fwd_{l}")
        if l == 0:
            xout, merged, pre = _merge_out_fwd(y_pool, y_conv, y_sb, u, wb_all, wo_all, xin, post_norm_g[l:l + 1], l,
                                               f"merge_out_fwd_{l}")
        else:
            dy, merged, pre, loss_row = _merge_out_fwd(y_pool, y_conv, y_sb, u, wb_all, wo_all, xin,
                                                       post_norm_g[l:l + 1], l, f"merge_out_fwd_{l}", target)
        saved.append((xin, u, h, y_pool, y_conv, y_sb, o_sb, merged, pre))
        xin = xout

    small = [None, None]
    recv = [None, None]
    ready = []
    for l in (1, 0):
        xl, u, h, y_pool, y_conv, y_sb, o_sb, merged, pre = saved[l]
        dmerged, dwo, dg_post = _out_proj_bwd(dy, pre, post_norm_g[l:l + 1], merged, wo_all, l, f"out_proj_bwd_{l}")
        du, dyp, dyc, dys, dwb = _merge_bwd(dmerged, y_pool, y_conv, y_sb, u, wb_all, l, f"merge_bwd_{l}")
        dwb = dwb.reshape(N_DEV, 3 * WIDTH, D_MODEL // N_DEV)
        dwo = dwo.reshape(N_DEV, D_MODEL // N_DEV, D_MODEL)
        du, dcw, dcb = _conv_bwd(u, conv_w_full[l], conv_b[l:l + 1], dyc, du, f"conv_bwd_{l}")
        du, dpw, dps = _pool_bwd(u, pool_w[l], pool_scale[l:l + 1], dyp, du, f"pool_bwd_{l}")
        small[l] = dict(pool_w=dpw, pool_scale=dps, conv_w=dcw, conv_b=dcb, post_norm_g=dg_post)
        if l == 1:
            du, dk, dv = _sb_bwd(u, o_sb, dys, du, f"sb_bwd_{l}")
        else:
            small[l]["pre_norm_g"] = jnp.zeros((1, D_MODEL), F32)
            packed = jnp.concatenate(
                [_rows128(jnp.stack([small[0][n], small[1][n]])) for n in SMALL_NAMES]
                + [jnp.pad(loss_row, ((0, 7), (0, 0)))], axis=0)
            du, dk, dv, *got, packed_all = _sb_bwd(
                u, o_sb, dys, du, f"sb_bwd_{l}", (ready + [dwb, dwo, packed], (False,) * 5 + (True,)))
            recv[1] = got[:3]
        du = lax.dynamic_update_slice(du, jnp.stack([dk, dv]).astype(BF16), (DU_SB_KV[0], 0, 0))
        if l == 1:
            dwi = _in_proj_bwd_w(h, du, f"in_proj_bwd_w_{l}")
            ready = [dwi, dwb, dwo]
            dx, dg_pre = _in_proj_bwd_x(du, w_in_all[l], xl, pre_norm_g[l:l + 1], dy, f"in_proj_bwd_x_{l}")
            small[l]["pre_norm_g"] = dg_pre
        else:
            dx, dg_pre, got_dwi = _in_proj_bwd_send(h, du, w_in_all[l], xl, pre_norm_g[l:l + 1], dy,
                                                    f"in_proj_bwd_{l}")
            recv[0] = [got_dwi] + got[3:]
        dy = dx
    grad_x = dy[None]

    (g_pre_0_all,) = _exchange([_rows128(dg_pre)], True, "gather_g_pre_0")
    packed_all = lax.dynamic_update_slice(packed_all, g_pre_0_all, (0, 0, 0))
    sizes = dict(pre_norm_g=16, pool_w=1024, pool_scale=8, conv_w=24, conv_b=8, post_norm_g=16)
    n_rows = sum(sizes.values())
    loss = jnp.sum(packed_all[:, n_rows, 0])

    given = dict(pre_norm_g=(pre_norm_g, m_pre_norm_g, v_pre_norm_g), pool_w=(pool_w, m_pool_w, v_pool_w),
                 pool_scale=(pool_scale, m_pool_scale, v_pool_scale), conv_b=(conv_b, m_conv_b, v_conv_b),
                 post_norm_g=(post_norm_g, m_post_norm_g, v_post_norm_g))
    zeros_cw = jnp.zeros((sizes["conv_w"], LANES), F32)
    pack3 = [jnp.concatenate([zeros_cw if n == "conv_w" else _rows128(given[n][k]) for n in SMALL_NAMES], axis=0)
             for k in range(3)]
    sg, sd, sm, sv = _adamw_small(packed_all[:, :n_rows], pack3[0], pack3[1], pack3[2], "adamw_small")

    def unpack(buf, name, shape):
        start = 0
        for n in SMALL_NAMES:
            if n == name:
                return buf[start:start + sizes[n]].reshape(shape)
            start += sizes[n]

    out = {}
    for n in ("pre_norm_g", "pool_w", "pool_scale", "conv_b", "post_norm_g"):
        shape = given[n][0].shape
        out[n] = tuple(unpack(b, n, shape) for b in (sg, sd, sm, sv))
    g_cw = lax.dynamic_slice_in_dim(unpack(sg, "conv_w", (2, 3, WIDTH)), me * conv_cols, conv_cols, axis=2)
    cw2 = lambda a: a.reshape(6, conv_cols)
    d_cw, m_cw, v_cw = _adamw_plain(cw2(g_cw), cw2(conv_w), cw2(m_conv_w), cw2(v_conv_w), "adamw_conv_w")
    out["conv_w"] = (g_cw,) + tuple(a.reshape(2, 3, conv_cols) for a in (d_cw, m_cw, v_cw))

    out["w_in"] = _adamw_layers(recv[0][0], recv[1][0], w_in, m_w_in, v_w_in, "adamw_w_in")
    cols = D_MODEL // N_DEV
    wb3 = lambda a: a.reshape(2, 3 * WIDTH, cols)
    out["w_branch"] = tuple(a.reshape(2, 3, WIDTH, cols) for a in _adamw_layers(
        recv[0][1], recv[1][1], wb3(w_branch), wb3(m_w_branch), wb3(v_w_branch), "adamw_w_branch"))
    out["w_out"] = _adamw_layers(recv[0][2], recv[1][2], w_out, m_w_out, v_w_out, "adamw_w_out")

    order = ("pre_norm_g", "w_in", "pool_w", "pool_scale", "conv_w", "conv_b", "w_branch", "w_out", "post_norm_g")
    return (loss, grad_x) + tuple(out[n][k] for k in range(4) for n in order)
```

```python
import functools

import jax
import jax.numpy as jnp
from jax import lax
from jax.experimental import pallas as pl
from jax.experimental.pallas import tpu as pltpu

F32 = jnp.float32
BF16 = jnp.bfloat16

N_DEV = 8
D_MODEL = 1024
WIDTH = 512
N_IN = 8192
COLS_PER_DEV = N_IN // N_DEV
HEAD_DIM = 64
LANES = 128
SB_SCALE = HEAD_DIM ** -0.5
LOG2E = 1.4426950408889634
RMS_EPS = 1e-6
POOL_HALO = 16
CONV_HALO = 8
ADAM_LR, ADAM_B1, ADAM_B2, ADAM_EPS, ADAM_WD, ADAM_STEP = 0.001, 0.9, 0.999, 1e-08, 0.01, 10
VMEM_LIMIT = 60 * 1024 * 1024

CB_POOL_V, CB_POOL_G = 0, 4
CB_CONV_X, CB_CONV_GB, CB_CONV_GC, CB_CONV_G = 8, 12, 16, 20
CB_SB_Q, CB_SB_K, CB_SB_V, CB_SB_G = 24, 28, 32, 36
MERGE_BLOCK_1024 = 5

DU_PIECES = 16
DU_MERGE = (0, 6)
DU_POOL = (6, 2)
DU_CONV = (8, 4)
DU_SB_QG = (12, 2)
DU_SB_KV = (14, 2)


def _du_pieces_of_block(j):
    first, second = 2 * (j - 5), 2 * (j - 5) + 1
    for block, (a, b) in enumerate(((6, 7), (8, 9), (10, 11), (12, 14), (15, 13))):
        first = jnp.where(j == block, a, first)
        second = jnp.where(j == block, b, second)
    return first, second


def _pcall(body, **kw):
    return pl.pallas_call(body, **kw)


def _params(sem=None):
    if sem is None:
        return pltpu.CompilerParams(vmem_limit_bytes=VMEM_LIMIT)
    return pltpu.CompilerParams(dimension_semantics=sem, vmem_limit_bytes=VMEM_LIMIT)


def _sigmoid(x):
    return 1.0 / (1.0 + jnp.exp(-x))


def _dot(a, b):
    return jnp.dot(a, b, preferred_element_type=F32)


def _dot_nt(a, b):
    return lax.dot_general(a, b, (((1,), (1,)), ((), ())), preferred_element_type=F32)


def _dot_tn(a, b):
    return lax.dot_general(a, b, (((0,), (0,)), ((), ())), preferred_element_type=F32)


def _split_bf16(x):
    hi = x.astype(BF16)
    lo = (x - hi.astype(F32)).astype(BF16)
    return hi, lo


N_PEER = N_DEV - 1
ANY_SPEC = pl.BlockSpec(memory_space=pl.ANY)


def _exchange_copies(ins, outs, send_sems, recv_sems, local_sems, gather, with_recvs=True):
    n = len(ins)
    gathers = _per_array(gather, n)
    x, y, c = lax.axis_index("x"), lax.axis_index("y"), lax.axis_index("c")
    me = 4 * x + 2 * y + c
    flip = lambda v, bit: 1 - v if bit else v
    local, sends, recvs = [], [], []
    for a in range(n):
        src = ins[a] if gathers[a] else ins[a].at[me]
        local.append(pltpu.make_async_copy(src, outs[a].at[me], local_sems.at[a]))
    for k in range(N_PEER):
        px, py, pc = flip(x, ((k + 1) >> 2) & 1), flip(y, ((k + 1) >> 1) & 1), flip(c, (k + 1) & 1)
        peer_id = 4 * px + 2 * py + pc
        for a in range(n):
            src = ins[a] if gathers[a] else ins[a].at[peer_id]
            common = dict(src_ref=src, send_sem=send_sems.at[a * N_PEER + k], recv_sem=recv_sems.at[a * N_PEER + k],
                          device_id=(px, py, pc), device_id_type=pl.DeviceIdType.MESH)
            sends.append(pltpu.make_async_remote_copy(dst_ref=outs[a].at[me], **common))
            if with_recvs:
                recvs.append(pltpu.make_async_remote_copy(dst_ref=outs[a].at[peer_id], **common))
    return local, sends, recvs


def _exchange_start(ins, outs, sems, gather):
    local, sends, _ = _exchange_copies(ins, outs, *sems, gather, with_recvs=False)
    for cp in local + sends:
        cp.start()


def _exchange_wait(ins, outs, sems, gather):
    local, sends, recvs = _exchange_copies(ins, outs, *sems, gather)
    for cp in recvs:
        cp.wait_recv()
    for cp in sends:
        cp.wait_send()
    for cp in local:
        cp.wait()


def _per_array(gather, n):
    return tuple(gather) if isinstance(gather, (tuple, list)) else (gather,) * n


def _exchange_out_shapes(arrs, gather):
    return [jax.ShapeDtypeStruct((N_DEV,) + tuple(a.shape if g else a.shape[1:]), a.dtype)
            for a, g in zip(arrs, _per_array(gather, len(arrs)))]


def _gather_two_level(arrs, name):
    n = len(arrs)

    def body(*refs):
        ins, outs = refs[:n], refs[n:2 * n]
        send_sems, recv_sems, local_sems = refs[2 * n:]
        x, y, c = lax.axis_index("x"), lax.axis_index("y"), lax.axis_index("c")
        me, sibling = (x, y, c), (x, y, 1 - c)
        chips = [(1 - x, y), (x, 1 - y), (1 - x, 1 - y)]
        slot = lambda dev: 4 * dev[0] + 2 * dev[1] + dev[2]

        def copy(a, k, block, to, src=None):
            return pltpu.make_async_remote_copy(
                src_ref=outs[a].at[slot(block)] if src is None else src, dst_ref=outs[a].at[slot(block)],
                send_sem=send_sems.at[a * N_PEER + k], recv_sem=recv_sems.at[a * N_PEER + k],
                device_id=to, device_id_type=pl.DeviceIdType.MESH)

        local = [pltpu.make_async_copy(ins[a], outs[a].at[slot(me)], local_sems.at[a]) for a in range(n)]
        first = []
        for a in range(n):
            first.append(copy(a, 0, me, sibling, src=ins[a]))
            first += [copy(a, 1 + j, me, (*chip, c), src=ins[a]) for j, chip in enumerate(chips)]
        for cp in local + first:
            cp.start()
        passed = []
        for j, chip in enumerate(chips):
            for a in range(n):
                copy(a, 1 + j, (*chip, c), me).wait_recv()
                passed.append(copy(a, 4 + j, (*chip, c), sibling))
                passed[-1].start()
        for a in range(n):
            copy(a, 0, sibling, me).wait_recv()
        for j, chip in enumerate(chips):
            for a in range(n):
                copy(a, 4 + j, (*chip, 1 - c), me).wait_recv()
        for cp in first + passed:
            cp.wait_send()
        for cp in local:
            cp.wait()

    return _pcall(
        body, name=name,
        out_shape=tuple(_exchange_out_shapes(arrs, True)),
        in_specs=[ANY_SPEC] * n, out_specs=tuple([ANY_SPEC] * n),
        scratch_shapes=_exchange_sems(n),
    )(*arrs)


def _exchange_sems(n):
    return [pltpu.SemaphoreType.DMA((n * N_PEER,)), pltpu.SemaphoreType.DMA((n * N_PEER,)),
            pltpu.SemaphoreType.DMA((n,))]


def _exchange(arrs, gather, name):
    n = len(arrs)

    def body(*refs):
        ins, outs, sems = refs[:n], refs[n:2 * n], refs[2 * n:]
        _exchange_start(ins, outs, sems, gather)
        _exchange_wait(ins, outs, sems, gather)

    return _pcall(
        body, name=name,
        out_shape=tuple(_exchange_out_shapes(arrs, gather)),
        in_specs=[ANY_SPEC] * n, out_specs=tuple([ANY_SPEC] * n),
        scratch_shapes=_exchange_sems(n),
    )(*arrs)


def _in_proj_fwd(x, g, w_all, name):
    s = x.shape[0]
    tm = min(1024, s)

    def body(x_ref, g_ref, w_ref, u_ref, h_ref, hs):
        @pl.when(pl.program_id(1) == 0)
        def _():
            xv = x_ref[...]
            r = lax.rsqrt(jnp.mean(xv * xv, axis=-1, keepdims=True) + RMS_EPS)
            hv = (xv * r * g_ref[...]).astype(BF16)
            hs[...] = hv
            h_ref[...] = hv
        u_ref[...] = _dot(hs[...], w_ref[...])

    return _pcall(
        body, name=name, grid=(s // tm, N_DEV),
        in_specs=[pl.BlockSpec((tm, D_MODEL), lambda i, j: (i, 0)),
                  pl.BlockSpec((1, D_MODEL), lambda i, j: (0, 0)),
                  pl.BlockSpec((None, D_MODEL, COLS_PER_DEV), lambda i, j: (j, 0, 0))],
        out_specs=(pl.BlockSpec((tm, COLS_PER_DEV), lambda i, j: (i, j)),
                   pl.BlockSpec((tm, D_MODEL), lambda i, j: (i, 0))),
        out_shape=(jax.ShapeDtypeStruct((s, N_IN), F32), jax.ShapeDtypeStruct((s, D_MODEL), BF16)),
        scratch_shapes=[pltpu.VMEM((tm, D_MODEL), BF16)],
        compiler_params=_params(("parallel", "arbitrary")),
    )(x, g, w_all)


def _pool_window(vs, t0, t, grp):
    ext = vs[pl.ds(t0, t + POOL_HALO), :]
    s2 = ext + pltpu.roll(ext, 1, 0)
    s4 = s2 + pltpu.roll(s2, 2, 0)
    s8 = s4 + pltpu.roll(s4, 4, 0)
    s16 = s8 + pltpu.roll(s8, 8, 0)
    sel = jnp.where(grp == 0, s2, jnp.where(grp == 1, s4, jnp.where(grp == 2, s8, s16)))
    return sel[POOL_HALO:, :], ext[POOL_HALO:, :]


def _pool_count(t0, t, grp):
    pos = t0 + lax.broadcasted_iota(jnp.int32, (t, 1), 0)
    return jnp.minimum(pos + 1, jnp.left_shift(2, grp)).astype(F32)


def _pool_fwd(u, pool_w, pool_scale, name):
    s = u.shape[0]
    t = min(256, s)

    def body(pv_ref, pg_ref, w_ref, sc_ref, y_ref, vs):
        grp = pl.program_id(0)
        vs[0:POOL_HALO, :] = jnp.zeros((POOL_HALO, LANES), F32)
        vs[POOL_HALO:, :] = pv_ref[...]
        wb = w_ref[...].astype(BF16)
        scale = sc_ref[...]

        def tile(i, carry):
            t0 = pl.multiple_of(i * t, t)
            win, v = _pool_window(vs, t0, t, grp)
            pooled = win / _pool_count(t0, t, grp) - v
            mixed = _dot(pooled.astype(BF16), wb)
            gate = pg_ref[pl.ds(t0, t), :]
            y_ref[pl.ds(t0, t), :] = (mixed * scale * (gate * _sigmoid(gate))).astype(BF16)
            return carry

        lax.fori_loop(0, s // t, tile, 0)

    return _pcall(
        body, name=name, grid=(4,),
        in_specs=[pl.BlockSpec((s, LANES), lambda g: (0, CB_POOL_V + g)),
                  pl.BlockSpec((s, LANES), lambda g: (0, CB_POOL_G + g)),
                  pl.BlockSpec((None, LANES, LANES), lambda g: (g, 0, 0)),
                  pl.BlockSpec((1, LANES), lambda g: (0, g))],
        out_specs=pl.BlockSpec((s, LANES), lambda g: (0, g)),
        out_shape=jax.ShapeDtypeStruct((s, WIDTH), BF16),
        scratch_shapes=[pltpu.VMEM((POOL_HALO + s, LANES), F32)],
        compiler_params=_params(("arbitrary",)),
    )(u, u, pool_w, pool_scale)


def _conv_taps(zs, t0, t):
    ext = zs[pl.ds(t0, t + CONV_HALO), :]
    z0 = ext[CONV_HALO:, :]
    z1 = pltpu.roll(ext, 1, 0)[CONV_HALO:, :]
    z2 = pltpu.roll(ext, 2, 0)[CONV_HALO:, :]
    return z0, z1, z2


def _conv_fwd(u, conv_w, conv_b, name):
    s = u.shape[0]
    t = min(256, s)

    def body(xc_ref, gb_ref, gc_ref, cg_ref, w_ref, b_ref, y_ref, zs):
        zs[0:CONV_HALO, :] = jnp.zeros((CONV_HALO, LANES), F32)
        zs[CONV_HALO:, :] = gc_ref[...] * xc_ref[...]
        w0, w1, w2 = w_ref[0:1, :], w_ref[1:2, :], w_ref[2:3, :]
        bias = b_ref[...]

        def tile(i, carry):
            t0 = pl.multiple_of(i * t, t)
            z0, z1, z2 = _conv_taps(zs, t0, t)
            conv = w0 * z2 + w1 * z1 + w2 * z0
            gate = cg_ref[pl.ds(t0, t), :]
            y = gb_ref[pl.ds(t0, t), :] * (conv + bias) * (gate * _sigmoid(gate))
            y_ref[pl.ds(t0, t), :] = y.astype(BF16)
            return carry

        lax.fori_loop(0, s // t, tile, 0)

    col = lambda base: pl.BlockSpec((s, LANES), lambda j: (0, base + j))
    return _pcall(
        body, name=name, grid=(4,),
        in_specs=[col(CB_CONV_X), col(CB_CONV_GB), col(CB_CONV_GC), col(CB_CONV_G),
                  pl.BlockSpec((3, LANES), lambda j: (0, j)),
                  pl.BlockSpec((1, LANES), lambda j: (0, j))],
        out_specs=pl.BlockSpec((s, LANES), lambda j: (0, j)),
        out_shape=jax.ShapeDtypeStruct((s, WIDTH), BF16),
        scratch_shapes=[pltpu.VMEM((CONV_HALO + s, LANES), F32)],
        compiler_params=_params(("arbitrary",)),
    )(u, u, u, u, conv_w, conv_b)


def _first_head_lanes(rows, width=LANES):
    lane = lax.broadcasted_iota(jnp.int32, (rows, width), 1)
    return jnp.bitwise_and(lane, LANES - 1) < HEAD_DIM


def _stack_heads(x, first):
    zero = jnp.zeros_like(x)
    return jnp.concatenate([jnp.where(first, x, zero), jnp.where(first, zero, x)], axis=0).astype(BF16)


def _causal_mask(tq, tk, copies):
    row = lax.broadcasted_iota(jnp.int32, (tq, tk), 0)
    col = lax.broadcasted_iota(jnp.int32, (tq, tk), 1)
    return jnp.concatenate([col < row] * copies, axis=0)


def _suffix_matrix(tk, inclusive, parts):
    r = lax.broadcasted_iota(jnp.int32, (parts * tk, 2 * tk), 0)
    c = lax.broadcasted_iota(jnp.int32, (parts * tk, 2 * tk), 1)
    r = jnp.bitwise_and(r, tk - 1)
    tri = (r >= c) if inclusive else (r > c)
    return jnp.where(c >= tk, 1.0, jnp.where(tri, 1.0, 0.0)).astype(BF16)


def _suffix_sums(x, m):
    hi, lo = _split_bf16(x)
    return _dot(jnp.concatenate([hi, lo], axis=1), m)


def _sb_log_terms(z, mask, m_strict):
    ls = jnp.minimum(z, 0.0) - jnp.log(1.0 + jnp.exp2(jnp.abs(z) * -LOG2E))
    lk = ls - z
    if mask is not None:
        lk = jnp.where(mask, lk, 0.0)
    return ls, _dot(lk.astype(BF16), m_strict)


SB_PAIRS = 4


def _pair_lanes(a):
    return slice(a * LANES, (a + 1) * LANES)


def _sb_fwd(u, name, xchg=None):
    s = u.shape[0]
    tq = tk = min(128, s)
    pairs = SB_PAIRS
    width = pairs * LANES
    rows = 2 * pairs * tq
    x_arrs, x_gather = xchg if xchg else ((), True)
    n_x = len(x_arrs)
    grid = (4 // pairs, s // tq)

    def body(*refs):
        q_ref, k_ref, v_ref, g_ref = refs[:4]
        x_in, refs = refs[4:4 + n_x], refs[4 + n_x:]
        o_ref, y_ref = refs[:2]
        x_out, refs = refs[2:2 + n_x], refs[2 + n_x:]
        kbf, vst, z_s, ell_s, carry_s = refs[:5]
        x_sems = refs[5:]
        i = pl.program_id(1)
        if n_x:
            @pl.when((pl.program_id(0) == 0) & (i == 0))
            def _():
                _exchange_start(x_in, x_out, x_sems, x_gather)

        @pl.when(i == 0)
        def _():
            kbf[...] = k_ref[...].astype(BF16)
            first_s = _first_head_lanes(s, width)
            vf = v_ref[...]
            vst[0] = jnp.where(first_s, vf, 0.0).astype(BF16)
            vst[1] = jnp.where(first_s, 0.0, vf).astype(BF16)

        first = _first_head_lanes(tq)
        mask = _causal_mask(tq, tk, 2 * pairs)
        m_strict = _suffix_matrix(tk, False, 1)
        qcat = jnp.concatenate([_stack_heads(q_ref[:, _pair_lanes(a)] * SB_SCALE, first) for a in range(pairs)],
                               axis=0)

        def scores(b):
            off = pl.multiple_of(jnp.maximum(b, 0) * tk, tk)
            z_s[...] = jnp.concatenate(
                [_dot_nt(qcat[a * 2 * tq:(a + 1) * 2 * tq], kbf[pl.ds(off, tk), _pair_lanes(a)])
                 for a in range(pairs)], axis=0)

        def log_weights(m):
            ls, cs = _sb_log_terms(z_s[...], m, m_strict)
            carry = carry_s[...]
            ell_s[...] = ls + cs[:, :tk] + carry
            carry_s[...] = carry + cs[:, tk:]

        def consume(b, accs, m):
            w = jnp.exp(ell_s[...])
            if m is not None:
                w = jnp.where(m, w, 0.0)
            wb = w.astype(BF16)
            off = pl.multiple_of(b * tk, tk)
            new = []
            for a in range(pairs):
                r0 = a * 2 * tq
                wcat = jnp.concatenate([wb[r0:r0 + tq], wb[r0 + tq:r0 + 2 * tq]], axis=1)
                vcat = jnp.concatenate([vst[0, pl.ds(off, tk), _pair_lanes(a)], vst[1, pl.ds(off, tk), _pair_lanes(a)]],
                                       axis=0)
                new.append(accs[a] + _dot(wcat, vcat))
            return tuple(new)

        carry_s[...] = jnp.zeros((rows, tk), F32)
        scores(i)
        log_weights(mask)
        scores(i - 1)
        accs = consume(i, tuple(jnp.zeros((tq, LANES), F32) for _ in range(pairs)), mask)
        log_weights(None)
        scores(i - 2)

        def step(n, accs):
            accs = consume(i - n, accs, None)
            log_weights(None)
            scores(i - n - 2)
            return accs

        accs = lax.fori_loop(1, i + 1, step, accs)
        o = jnp.concatenate(accs, axis=1)
        o_ref[...] = o
        gate = g_ref[...]
        y_ref[...] = (o * (gate * _sigmoid(gate))).astype(BF16)
        if n_x:
            @pl.when((pl.program_id(0) == grid[0] - 1) & (i == grid[1] - 1))
            def _():
                _exchange_wait(x_in, x_out, x_sems, x_gather)

    base = lambda cb: cb // pairs
    qblk = lambda cb: pl.BlockSpec((tq, width), lambda p, i: (i, base(cb) + p))
    full = lambda cb: pl.BlockSpec((s, width), lambda p, i: (0, base(cb) + p), pipeline_mode=pl.Buffered(1))
    state = pltpu.VMEM((rows, tk), F32)
    return _pcall(
        body, name=name, grid=grid,
        in_specs=[qblk(CB_SB_Q), full(CB_SB_K), full(CB_SB_V), qblk(CB_SB_G)] + [ANY_SPEC] * n_x,
        out_specs=(qblk(0), qblk(0)) + (ANY_SPEC,) * n_x,
        out_shape=(jax.ShapeDtypeStruct((s, WIDTH), F32), jax.ShapeDtypeStruct((s, WIDTH), BF16))
        + tuple(_exchange_out_shapes(x_arrs, x_gather)),
        scratch_shapes=[pltpu.VMEM((s, width), BF16), pltpu.VMEM((2, s, width), BF16), state, state, state]
        + (_exchange_sems(n_x) if n_x else []),
        compiler_params=_params(("arbitrary", "arbitrary")),
    )(u, u, u, u, *x_arrs)


def _merge_out_fwd(y_pool, y_conv, y_sb, u, wb_all, wo_all, x, g_post, layer, name, target=None):
    s = x.shape[0]
    tm = min(512, s)
    n_tiles = s // tm
    with_loss = target is not None

    def body(yp, yc, ys, m0, m1, m2, wb_ref, wo_ref, x_ref, g_ref, *rest):
        merged = jnp.zeros((tm, D_MODEL), F32)
        for n, (y_ref, m_ref) in enumerate(((yp, m0), (yc, m1), (ys, m2))):
            merged = merged + _sigmoid(m_ref[...]) * _dot(y_ref[...], wb_ref[n])
        mb = merged.astype(BF16)
        pre = _dot(mb, wo_ref[...].reshape(D_MODEL, D_MODEL))
        r = lax.rsqrt(jnp.mean(pre * pre, axis=-1, keepdims=True) + RMS_EPS)
        y = x_ref[...] + pre * r * g_ref[...]
        if not with_loss:
            out_ref, merged_ref, pre_ref = rest
            out_ref[...] = y
        else:
            t_ref, out_ref, merged_ref, pre_ref, loss_ref, acc = rest
            i = pl.program_id(0)

            @pl.when(i == 0)
            def _():
                acc[...] = jnp.zeros_like(acc)
            err = y - t_ref[...]
            out_ref[...] = err / D_MODEL
            acc[...] += jnp.sum(err * err, axis=0, keepdims=True)

            @pl.when(i == n_tiles - 1)
            def _():
                total = jnp.sum(acc[...], axis=1, keepdims=True) * (0.5 / D_MODEL)
                loss_ref[...] = jnp.broadcast_to(total, (1, LANES))
        merged_ref[...] = mb
        pre_ref[...] = pre

    rows = lambda w: pl.BlockSpec((tm, w), lambda i: (i, 0))
    merge = lambda n: pl.BlockSpec((tm, D_MODEL), lambda i: (i, MERGE_BLOCK_1024 + n))
    out_specs = (rows(D_MODEL), rows(D_MODEL), rows(D_MODEL))
    out_shape = (jax.ShapeDtypeStruct((s, D_MODEL), F32), jax.ShapeDtypeStruct((s, D_MODEL), BF16),
                 jax.ShapeDtypeStruct((s, D_MODEL), F32))
    if with_loss:
        out_specs += (pl.BlockSpec((1, LANES), lambda i: (0, 0)),)
        out_shape += (jax.ShapeDtypeStruct((1, LANES), F32),)
    return _pcall(
        body, name=name, grid=(n_tiles,),
        in_specs=[rows(WIDTH), rows(WIDTH), rows(WIDTH), merge(0), merge(1), merge(2),
                  pl.BlockSpec((None, 3, WIDTH, D_MODEL), lambda i: (layer, 0, 0, 0)),
                  pl.BlockSpec((N_DEV, None, D_MODEL // N_DEV, D_MODEL), lambda i: (0, layer, 0, 0)),
                  rows(D_MODEL), pl.BlockSpec((1, D_MODEL), lambda i: (0, 0))] + ([rows(D_MODEL)] if with_loss else []),
        out_specs=out_specs, out_shape=out_shape,
        scratch_shapes=[pltpu.VMEM((1, D_MODEL), F32)] if with_loss else [],
        compiler_params=_params(("arbitrary",)),
    )(y_pool, y_conv, y_sb, u, u, u, wb_all, wo_all, x, g_post, *([target] if with_loss else []))


def _out_proj_bwd(dy, pre, g_post, merged, wo_all, layer, name):
    s = dy.shape[0]
    tm = min(512, s)
    n_tiles = s // tm

    def body(dy_ref, pre_ref, g_ref, mg_ref, wo_ref, dm_ref, dwo_ref, dg_ref, acc):
        i = pl.program_id(0)

        @pl.when(i == 0)
        def _():
            acc[...] = jnp.zeros_like(acc)
            dg_ref[...] = jnp.zeros_like(dg_ref)
        dyv, pre_v = dy_ref[...], pre_ref[...]
        r = lax.rsqrt(jnp.mean(pre_v * pre_v, axis=-1, keepdims=True) + RMS_EPS)
        dg_ref[...] += jnp.sum(dyv * pre_v * r, axis=0, keepdims=True)
        a = dyv * g_ref[...]
        dpre = r * a - pre_v * (r * r * r) * jnp.mean(a * pre_v, axis=-1, keepdims=True)
        db = dpre.astype(BF16)
        acc[...] += _dot_tn(mg_ref[...], db)
        dm_ref[...] = _dot_nt(db, wo_ref[...].reshape(D_MODEL, D_MODEL))

        @pl.when(i == n_tiles - 1)
        def _():
            dwo_ref[...] = acc[...].astype(BF16)

    rows = lambda: pl.BlockSpec((tm, D_MODEL), lambda i: (i, 0))
    return _pcall(
        body, name=name, grid=(n_tiles,),
        in_specs=[rows(), rows(), pl.BlockSpec((1, D_MODEL), lambda i: (0, 0)), rows(),
                  pl.BlockSpec((N_DEV, None, D_MODEL // N_DEV, D_MODEL), lambda i: (0, layer, 0, 0))],
        out_specs=(rows(), pl.BlockSpec((D_MODEL, D_MODEL), lambda i: (0, 0)),
                   pl.BlockSpec((1, D_MODEL), lambda i: (0, 0))),
        out_shape=(jax.ShapeDtypeStruct((s, D_MODEL), F32), jax.ShapeDtypeStruct((D_MODEL, D_MODEL), BF16),
                   jax.ShapeDtypeStruct((1, D_MODEL), F32)),
        scratch_shapes=[pltpu.VMEM((D_MODEL, D_MODEL), F32)],
        compiler_params=_params(("arbitrary",)),
    )(dy, pre, g_post, merged, wo_all)


def _merge_bwd(dmerged, y_pool, y_conv, y_sb, u, wb_all, layer, name):
    s = dmerged.shape[0]
    tm = min(512, s)
    n_tiles = s // tm
    cols = D_MODEL // N_DEV

    def body(dm_ref, yp, yc, ys, m0, m1, m2, wb_ref, du_ref, dyp, dyc, dys, dwb_ref, acc):
        i = pl.program_id(0)

        @pl.when(i == 0)
        def _():
            acc[...] = jnp.zeros_like(acc)
        dm = dm_ref[...]
        for n, (y_ref, m_ref, dy_ref) in enumerate(((yp, m0, dyp), (yc, m1, dyc), (ys, m2, dys))):
            yv = y_ref[...]
            wb = wb_ref[n]
            gate = _sigmoid(m_ref[...])
            proj = _dot(yv, wb)
            dgate = (dm * proj * gate * (1.0 - gate)).astype(BF16)
            du_ref[2 * n] = dgate[:, :WIDTH]
            du_ref[2 * n + 1] = dgate[:, WIDTH:]
            dproj = (dm * gate).astype(BF16)
            acc[n] += _dot_tn(yv, dproj)
            dy_ref[...] = _dot_nt(dproj, wb)

        @pl.when(i == n_tiles - 1)
        def _():
            for j in range(N_DEV):
                for n in range(3):
                    dwb_ref[j, n] = acc[n, :, j * cols:(j + 1) * cols].astype(BF16)

    rows = lambda w: pl.BlockSpec((tm, w), lambda i: (i, 0))
    merge = lambda n: pl.BlockSpec((tm, D_MODEL), lambda i: (i, MERGE_BLOCK_1024 + n))
    return _pcall(
        body, name=name, grid=(n_tiles,),
        in_specs=[rows(D_MODEL), rows(WIDTH), rows(WIDTH), rows(WIDTH), merge(0), merge(1), merge(2),
                  pl.BlockSpec((None, 3, WIDTH, D_MODEL), lambda i: (layer, 0, 0, 0))],
        out_specs=(pl.BlockSpec((DU_MERGE[1], tm, WIDTH), lambda i: (DU_MERGE[0] // DU_MERGE[1], i, 0)),
                   rows(WIDTH), rows(WIDTH), rows(WIDTH),
                   pl.BlockSpec((N_DEV, 3, WIDTH, cols), lambda i: (0, 0, 0, 0))),
        out_shape=(jax.ShapeDtypeStruct((DU_PIECES, s, WIDTH), BF16),
                   jax.ShapeDtypeStruct((s, WIDTH), F32), jax.ShapeDtypeStruct((s, WIDTH), F32),
                   jax.ShapeDtypeStruct((s, WIDTH), F32),
                   jax.ShapeDtypeStruct((N_DEV, 3, WIDTH, cols), BF16)),
        scratch_shapes=[pltpu.VMEM((3, WIDTH, D_MODEL), F32)],
        compiler_params=_params(("arbitrary",)),
    )(dmerged, y_pool, y_conv, y_sb, u, u, u, wb_all)


def _sb_bwd(u, o, dys, du, name, xchg=None):
    s = u.shape[0]
    tq = tk = min(128, s)
    pairs = SB_PAIRS
    width = pairs * LANES
    assert width == WIDTH
    rows = 2 * pairs * tq
    pair_rows = lambda a: slice(a * 2 * tq, (a + 1) * 2 * tq)

    x_arrs, x_gather = xchg if xchg else ((), True)
    n_x = len(x_arrs)
    grid = (4 // pairs, s // tq)

    def body(*refs):
        q_ref, k_ref, v_ref, g_ref, o_ref, dys_ref = refs[:6]
        x_in, refs = refs[7:7 + n_x], refs[7 + n_x:]
        du_ref, dk_ref, dv_ref = refs[:3]
        dq_ref, dg_ref = du_ref.at[0], du_ref.at[1]
        x_out, refs = refs[3:3 + n_x], refs[3 + n_x:]
        kbf, vbf, kst, z_s, ell_s, ls_s, cl_s, wb_s, g_s, bef_s, cg_s, beta_s = refs[:12]
        x_sems = refs[12:]
        i = pl.program_id(1)
        if n_x:
            @pl.when((pl.program_id(0) == 0) & (i == 0))
            def _():
                _exchange_start(x_in, x_out, x_sems, x_gather)

        @pl.when(i == 0)
        def _():
            dk_ref[...] = jnp.zeros_like(dk_ref)
            dv_ref[...] = jnp.zeros_like(dv_ref)
            kf = k_ref[...]
            kbf[...] = kf.astype(BF16)
            vbf[...] = v_ref[...].astype(BF16)
            first_s = _first_head_lanes(s, width)
            kst[0] = jnp.where(first_s, kf, 0.0).astype(BF16)
            kst[1] = jnp.where(first_s, 0.0, kf).astype(BF16)

        first = _first_head_lanes(tq)
        mask = _causal_mask(tq, tk, 2 * pairs)
        m_strict = _suffix_matrix(tk, False, 1)
        m_incl = _suffix_matrix(tk, True, 2)

        gate = g_ref[...]
        sg = _sigmoid(gate)
        dy = dys_ref[...]
        ov = o_ref[...]
        dg_ref[...] = (dy * ov * (sg * (1.0 + gate * (1.0 - sg)))).astype(BF16)
        do = (dy * (gate * sg)).astype(BF16)
        prod = do.astype(F32) * ov
        row_sum = lambda v: jnp.broadcast_to(jnp.sum(v, axis=1, keepdims=True), (tq, tk))
        dsum, docat, qcat = [], [], []
        for a in range(pairs):
            pa = prod[:, _pair_lanes(a)]
            dsum += [row_sum(jnp.where(first, pa, 0.0)), row_sum(jnp.where(first, 0.0, pa))]
            docat.append(_stack_heads(do[:, _pair_lanes(a)], first))
            qcat.append(_stack_heads(q_ref[:, _pair_lanes(a)] * SB_SCALE, first))
        dsum = jnp.concatenate(dsum, axis=0)

        def block_start(b):
            return pl.multiple_of(jnp.maximum(b, 0) * tk, tk)

        def scores(b):
            off = block_start(b)
            z_s[...] = jnp.concatenate([_dot_nt(qcat[a], kbf[pl.ds(off, tk), _pair_lanes(a)]) for a in range(pairs)],
                                       axis=0)

        def log_weights(m):
            ls, cs = _sb_log_terms(z_s[...], m, m_strict)
            cl = cl_s[...]
            ell_s[...] = ls + cs[:, :tk] + cl
            cl_s[...] = cl + cs[:, tk:]
            ls_s[...] = ls

        def weights(b, m):
            off = block_start(b)
            dwt = jnp.concatenate([_dot_nt(docat[a], vbf[pl.ds(off, tk), _pair_lanes(a)]) for a in range(pairs)],
                                  axis=0)
            w = jnp.exp(ell_s[...])
            if m is not None:
                w = jnp.where(m, w, 0.0)
            wb = w.astype(BF16)
            g = dwt * wb.astype(F32)
            gs = _suffix_sums(g, m_incl)
            cg = cg_s[...]
            beta = jnp.exp(ls_s[...])
            wb_s[...] = wb
            beta_s[...] = beta
            g_s[...] = g * (1.0 - beta)
            bef_s[...] = gs[:, :tk] + cg
            cg_s[...] = cg + gs[:, tk:]

        def grads(b, dqs, m):
            dz = g_s[...] - beta_s[...] * (dsum - bef_s[...])
            if m is not None:
                dz = jnp.where(m, dz, 0.0)
            dzb = dz.astype(BF16)
            wb = wb_s[...]
            off = pl.multiple_of(b * tk, tk)
            new = []
            for a in range(pairs):
                r0 = a * 2 * tq
                kcat = jnp.concatenate([kst[0, pl.ds(off, tk), _pair_lanes(a)], kst[1, pl.ds(off, tk), _pair_lanes(a)]],
                                       axis=0)
                new.append(dqs[a] + _dot(jnp.concatenate([dzb[r0:r0 + tq], dzb[r0 + tq:r0 + 2 * tq]], axis=1), kcat))
                dk_ref[pl.ds(off, tk), _pair_lanes(a)] += _dot_tn(dzb[pair_rows(a)], qcat[a])
                dv_ref[pl.ds(off, tk), _pair_lanes(a)] += _dot_tn(wb[pair_rows(a)], docat[a])
            return tuple(new)

        zero = jnp.zeros((rows, tk), F32)
        cl_s[...] = zero
        cg_s[...] = zero
        scores(i)
        log_weights(mask)
        scores(i - 1)
        weights(i, mask)
        log_weights(None)
        scores(i - 2)
        dqs = grads(i, tuple(jnp.zeros((tq, LANES), F32) for _ in range(pairs)), mask)
        weights(i - 1, None)
        log_weights(None)
        scores(i - 3)

        def step(n, dqs):
            dqs = grads(i - n, dqs, None)
            weights(i - n - 1, None)
            log_weights(None)
            scores(i - n - 3)
            return dqs

        dqs = lax.fori_loop(1, i + 1, step, dqs)
        dq_ref[...] = (jnp.concatenate(dqs, axis=1) * SB_SCALE).astype(BF16)
        if n_x:
            @pl.when((pl.program_id(0) == grid[0] - 1) & (i == grid[1] - 1))
            def _():
                _exchange_wait(x_in, x_out, x_sems, x_gather)

    base = lambda cb: cb // pairs
    qblk = lambda cb: pl.BlockSpec((tq, width), lambda p, i: (i, base(cb) + p))
    full = lambda cb: pl.BlockSpec((s, width), lambda p, i: (0, base(cb) + p), pipeline_mode=pl.Buffered(1))
    state = pltpu.VMEM((rows, tk), F32)
    return _pcall(
        body, name=name, grid=grid,
        in_specs=[qblk(CB_SB_Q), full(CB_SB_K), full(CB_SB_V), qblk(CB_SB_G), qblk(0), qblk(0), ANY_SPEC]
        + [ANY_SPEC] * n_x,
        out_specs=(pl.BlockSpec((DU_SB_QG[1], tq, WIDTH), lambda p, i: (DU_SB_QG[0] // DU_SB_QG[1], i, 0)),
                   full(0), full(0)) + (ANY_SPEC,) * n_x,
        out_shape=(jax.ShapeDtypeStruct(du.shape, du.dtype), jax.ShapeDtypeStruct((s, WIDTH), F32),
                   jax.ShapeDtypeStruct((s, WIDTH), F32)) + tuple(_exchange_out_shapes(x_arrs, x_gather)),
        input_output_aliases={6: 0},
        scratch_shapes=[pltpu.VMEM((s, width), BF16), pltpu.VMEM((s, width), BF16), pltpu.VMEM((2, s, width), BF16),
                        state, state, state, state, pltpu.VMEM((rows, tk), BF16),
                        state, state, state, state] + (_exchange_sems(n_x) if n_x else []),
        compiler_params=_params(("arbitrary", "arbitrary")),
    )(u, u, u, u, o, dys, du, *x_arrs)


def _conv_bwd(u, conv_w, conv_b, dyc, du, name):
    s = u.shape[0]
    t = min(256, s)
    n_tiles = s // t

    def body(xc_ref, gb_ref, gc_ref, cg_ref, w_ref, b_ref, dy_ref, du_in, du_ref, dw_ref, db_ref, zs, ds):
        dxc_ref, dgb_ref, dgc_ref, dcg_ref = (du_ref.at[p] for p in range(4))
        zs[0:CONV_HALO, :] = jnp.zeros((CONV_HALO, LANES), F32)
        zs[CONV_HALO:, :] = gc_ref[...] * xc_ref[...]
        ds[s:, :] = jnp.zeros((CONV_HALO, LANES), F32)
        w0, w1, w2 = w_ref[0:1, :], w_ref[1:2, :], w_ref[2:3, :]
        bias = b_ref[...]

        def first(i, sums):
            t0 = pl.multiple_of(i * t, t)
            z0, z1, z2 = _conv_taps(zs, t0, t)
            pre = w0 * z2 + w1 * z1 + w2 * z0 + bias
            gate = cg_ref[pl.ds(t0, t), :]
            sg = _sigmoid(gate)
            gb = gb_ref[pl.ds(t0, t), :]
            dy = dy_ref[pl.ds(t0, t), :]
            dcg_ref[pl.ds(t0, t), :] = (dy * gb * pre * (sg * (1.0 + gate * (1.0 - sg)))).astype(BF16)
            dgb_ref[pl.ds(t0, t), :] = (dy * pre * (gate * sg)).astype(BF16)
            dc = dy * gb * (gate * sg)
            ds[pl.ds(t0, t), :] = dc
            red = lambda v: jnp.sum(v, axis=0, keepdims=True)
            return (sums[0] + red(dc * z2), sums[1] + red(dc * z1), sums[2] + red(dc * z0), sums[3] + red(dc))

        zrow = jnp.zeros((1, LANES), F32)
        sw0, sw1, sw2, sb = lax.fori_loop(0, n_tiles, first, (zrow, zrow, zrow, zrow))
        dw_ref[0:1, :] = sw0
        dw_ref[1:2, :] = sw1
        dw_ref[2:3, :] = sw2
        db_ref[...] = sb

        def second(i, carry):
            t0 = pl.multiple_of(i * t, t)
            ext = ds[pl.ds(t0, t + CONV_HALO), :]
            n = t + CONV_HALO
            d0 = ext[:t, :]
            d1 = pltpu.roll(ext, n - 1, 0)[:t, :]
            d2 = pltpu.roll(ext, n - 2, 0)[:t, :]
            dz = w2 * d0 + w1 * d1 + w0 * d2
            dgc_ref[pl.ds(t0, t), :] = (dz * xc_ref[pl.ds(t0, t), :]).astype(BF16)
            dxc_ref[pl.ds(t0, t), :] = (dz * gc_ref[pl.ds(t0, t), :]).astype(BF16)
            return carry

        lax.fori_loop(0, n_tiles, second, 0)

    col = lambda base: pl.BlockSpec((s, LANES), lambda j: (0, base + j))
    first, count = DU_CONV
    return _pcall(
        body, name=name, grid=(4,),
        in_specs=[col(CB_CONV_X), col(CB_CONV_GB), col(CB_CONV_GC), col(CB_CONV_G),
                  pl.BlockSpec((3, LANES), lambda j: (0, j)), pl.BlockSpec((1, LANES), lambda j: (0, j)), col(0),
                  ANY_SPEC],
        out_specs=(pl.BlockSpec((count, s, LANES), lambda j: (first // count, 0, j)),
                   pl.BlockSpec((3, LANES), lambda j: (0, j)), pl.BlockSpec((1, LANES), lambda j: (0, j))),
        out_shape=(jax.ShapeDtypeStruct(du.shape, du.dtype),
                   jax.ShapeDtypeStruct((3, WIDTH), F32), jax.ShapeDtypeStruct((1, WIDTH), F32)),
        scratch_shapes=[pltpu.VMEM((CONV_HALO + s, LANES), F32), pltpu.VMEM((s + CONV_HALO, LANES), F32)],
        input_output_aliases={7: 0},
        compiler_params=_params(("arbitrary",)),
    )(u, u, u, u, conv_w, conv_b, dyc, du)


def _pool_bwd(u, pool_w, pool_scale, dyp, du, name):
    s = u.shape[0]
    t = min(256, s)
    n_tiles = s // t

    def body(pv_ref, pg_ref, w_ref, sc_ref, dy_ref, du_in, du_ref, dw_ref, dsc_ref, vs, es, dps):
        dpv_ref, dpg_ref = du_ref.at[0], du_ref.at[1]
        grp = pl.program_id(0)
        vs[0:POOL_HALO, :] = jnp.zeros((POOL_HALO, LANES), F32)
        vs[POOL_HALO:, :] = pv_ref[...]
        es[s:, :] = jnp.zeros((POOL_HALO, LANES), F32)
        wb = w_ref[...].astype(BF16)
        scale = sc_ref[...]

        def first(i, sums):
            dw, dsc = sums
            t0 = pl.multiple_of(i * t, t)
            win, v = _pool_window(vs, t0, t, grp)
            cnt = _pool_count(t0, t, grp)
            pb = (win / cnt - v).astype(BF16)
            mixed = _dot(pb, wb)
            gate = pg_ref[pl.ds(t0, t), :]
            sg = _sigmoid(gate)
            dy = dy_ref[pl.ds(t0, t), :]
            dpg_ref[pl.ds(t0, t), :] = (dy * (mixed * scale) * (sg * (1.0 + gate * (1.0 - sg)))).astype(BF16)
            dms = dy * (gate * sg)
            dsc = dsc + jnp.sum(dms * mixed, axis=0, keepdims=True)
            dmb = (dms * scale).astype(BF16)
            dw = dw + _dot_tn(pb, dmb)
            dpooled = _dot_nt(dmb, wb)
            dps[pl.ds(t0, t), :] = dpooled
            es[pl.ds(t0, t), :] = dpooled / cnt
            return dw, dsc

        dw, dsc = lax.fori_loop(0, n_tiles, first, (jnp.zeros((LANES, LANES), F32), jnp.zeros((1, LANES), F32)))
        dw_ref[...] = dw
        dsc_ref[...] = dsc

        def second(i, carry):
            t0 = pl.multiple_of(i * t, t)
            ext = es[pl.ds(t0, t + POOL_HALO), :]
            n = t + POOL_HALO
            f2 = ext + pltpu.roll(ext, n - 1, 0)
            f4 = f2 + pltpu.roll(f2, n - 2, 0)
            f8 = f4 + pltpu.roll(f4, n - 4, 0)
            f16 = f8 + pltpu.roll(f8, n - 8, 0)
            sel = jnp.where(grp == 0, f2, jnp.where(grp == 1, f4, jnp.where(grp == 2, f8, f16)))
            dpv_ref[pl.ds(t0, t), :] = (sel[:t, :] - dps[pl.ds(t0, t), :]).astype(BF16)
            return carry

        lax.fori_loop(0, n_tiles, second, 0)

    col = lambda base: pl.BlockSpec((s, LANES), lambda g: (0, base + g))
    first, count = DU_POOL
    return _pcall(
        body, name=name, grid=(4,),
        in_specs=[col(CB_POOL_V), col(CB_POOL_G), pl.BlockSpec((None, LANES, LANES), lambda g: (g, 0, 0)),
                  pl.BlockSpec((1, LANES), lambda g: (0, g)), col(0), ANY_SPEC],
        out_specs=(pl.BlockSpec((count, s, LANES), lambda g: (first // count, 0, g)),
                   pl.BlockSpec((None, LANES, LANES), lambda g: (g, 0, 0)),
                   pl.BlockSpec((1, LANES), lambda g: (0, g))),
        out_shape=(jax.ShapeDtypeStruct(du.shape, du.dtype),
                   jax.ShapeDtypeStruct((4, LANES, LANES), F32), jax.ShapeDtypeStruct((1, WIDTH), F32)),
        scratch_shapes=[pltpu.VMEM((POOL_HALO + s, LANES), F32), pltpu.VMEM((s + POOL_HALO, LANES), F32),
                        pltpu.VMEM((s, LANES), F32)],
        input_output_aliases={5: 0},
        compiler_params=_params(("arbitrary",)),
    )(u, u, pool_w, pool_scale, dyp, du)


def _in_proj_bwd_x(du, w_all, x, g_pre, dy, name):
    s = x.shape[0]
    tm = min(1024, s)
    grid = (s // tm, N_DEV)

    def body(dua_ref, dub_ref, w_ref, x_ref, g_ref, dy_ref, dx_ref, dg_ref, acc):
        i, k = pl.program_id(0), pl.program_id(1)

        @pl.when(k == 0)
        def _():
            acc[...] = jnp.zeros_like(acc)

        @pl.when((k == 0) & (i == 0))
        def _():
            dg_ref[...] = jnp.zeros_like(dg_ref)
        acc[...] += _dot_nt(jnp.concatenate([dua_ref[...], dub_ref[...]], axis=1), w_ref[...])

        @pl.when(k == N_DEV - 1)
        def _():
            dh, xv = acc[...], x_ref[...]
            r = lax.rsqrt(jnp.mean(xv * xv, axis=-1, keepdims=True) + RMS_EPS)
            dg_ref[...] += jnp.sum(dh * xv * r, axis=0, keepdims=True)
            a = dh * g_ref[...]
            dx_ref[...] = dy_ref[...] + r * a - xv * (r * r * r) * jnp.mean(a * xv, axis=-1, keepdims=True)

    rows = lambda: pl.BlockSpec((tm, D_MODEL), lambda i, k: (i, 0))
    vec = lambda: pl.BlockSpec((1, D_MODEL), lambda i, k: (0, 0))
    piece = lambda half: pl.BlockSpec((None, tm, WIDTH), lambda i, k: (_du_pieces_of_block(k)[half], i, 0))
    return _pcall(
        body, name=name, grid=grid,
        in_specs=[piece(0), piece(1), pl.BlockSpec((None, D_MODEL, COLS_PER_DEV), lambda i, k: (k, 0, 0)),
                  rows(), vec(), rows()],
        out_specs=(rows(), vec()),
        out_shape=(jax.ShapeDtypeStruct((s, D_MODEL), F32), jax.ShapeDtypeStruct((1, D_MODEL), F32)),
        scratch_shapes=[pltpu.VMEM((tm, D_MODEL), F32)],
        compiler_params=_params(("arbitrary", "arbitrary")),
    )(du, du, w_all, x, g_pre, dy)


ROW_OFFSETS = (6, 7, 2, 4, 3, 5, 0, 1)


def _in_proj_bwd_send(h, du, w_all, x, g_pre, dy, name):
    s = x.shape[0]
    tk = s // N_DEV
    tm = min(1024, s)
    n_i = s // tm
    grid = (N_DEV + n_i, N_DEV)
    last = N_DEV - 1
    def offset(row):
        return functools.reduce(lambda acc, rn: jnp.where(row == rn[0], rn[1], acc), enumerate(ROW_OFFSETS), 0)

    def body(me_ref, h_ref, duwa_ref, duwb_ref, duxa_ref, duxb_ref, w_ref, x_ref, g_ref, dy_ref,
             dx_ref, dg_ref, recv_ref, part_ref, acc_w, stage, acc_x, send_sems, recv_sems, park_sems):
        r, k = pl.program_id(0), pl.program_id(1)
        x_, y_, c_ = lax.axis_index("x"), lax.axis_index("y"), lax.axis_index("c")
        me = 4 * x_ + 2 * y_ + c_
        flip = lambda v, bit: 1 - v if bit else v
        peer = lambda n: (flip(x_, (n >> 2) & 1), flip(y_, (n >> 1) & 1), flip(c_, n & 1))

        def park(row):
            n = ROW_OFFSETS[row]
            dst = recv_ref.at[me] if n == 0 else part_ref.at[n]
            return pltpu.make_async_copy(stage.at[row % 2], dst, park_sems.at[row % 2])

        def send(n, landing=False):
            px, py, pc = peer(n)
            dst = recv_ref.at[4 * px + 2 * py + pc] if landing else recv_ref.at[me]
            return pltpu.make_async_remote_copy(
                src_ref=part_ref.at[n], dst_ref=dst, send_sem=send_sems.at[n], recv_sem=recv_sems.at[n],
                device_id=(px, py, pc), device_id_type=pl.DeviceIdType.MESH)

        def parked(row):
            park(row).wait()
            if ROW_OFFSETS[row] >= 1:
                send(ROW_OFFSETS[row]).start()

        @pl.when(r < N_DEV)
        def _():
            @pl.when(k == 0)
            def _():
                acc_w[...] = jnp.zeros_like(acc_w)
            acc_w[...] += _dot_tn(h_ref[...], jnp.concatenate([duwa_ref[...], duwb_ref[...]], axis=1))

            for row in range(N_DEV):
                @pl.when((k == last) & (r == row))
                def _():
                    if row >= 1:
                        parked(row - 1)
                    stage[row % 2] = acc_w[...].astype(BF16)
                    park(row).start()

        @pl.when(r >= N_DEV)
        def _():
            @pl.when(k == 0)
            def _():
                acc_x[...] = jnp.zeros_like(acc_x)

            @pl.when((k == 0) & (r == N_DEV))
            def _():
                dg_ref[...] = jnp.zeros_like(dg_ref)
                parked(last)
            acc_x[...] += _dot_nt(jnp.concatenate([duxa_ref[...], duxb_ref[...]], axis=1), w_ref[...])

            @pl.when(k == last)
            def _():
                dh, xv = acc_x[...], x_ref[...]
                rs = lax.rsqrt(jnp.mean(xv * xv, axis=-1, keepdims=True) + RMS_EPS)
                dg_ref[...] += jnp.sum(dh * xv * rs, axis=0, keepdims=True)
                a = dh * g_ref[...]
                dx_ref[...] = dy_ref[...] + rs * a - xv * (rs * rs * rs) * jnp.mean(a * xv, axis=-1, keepdims=True)

        @pl.when((r == grid[0] - 1) & (k == last))
        def _():
            for n in range(1, N_DEV):
                send(n).wait_send()
            for n in range(1, N_DEV):
                send(n, landing=True).wait_recv()

    in_w = lambda r: r < N_DEV
    row_x = lambda r: jnp.maximum(r - N_DEV, 0)
    rows = lambda: pl.BlockSpec((tm, D_MODEL), lambda r, k, me: (row_x(r), 0))
    vec = lambda: pl.BlockSpec((1, D_MODEL), lambda r, k, me: (0, 0))
    block_w = lambda r, me: jnp.bitwise_xor(me[0], offset(jnp.minimum(r, last)))
    block_x = lambda r, k: jnp.where(in_w(r), 0, k)
    piece_w = lambda half: pl.BlockSpec(
        (None, tk, WIDTH), lambda r, k, me: (_du_pieces_of_block(block_w(r, me))[half], jnp.where(in_w(r), k, last), 0))
    piece_x = lambda half: pl.BlockSpec(
        (None, tm, WIDTH), lambda r, k, me: (_du_pieces_of_block(block_x(r, k))[half], row_x(r), 0))
    grid_spec = pltpu.PrefetchScalarGridSpec(
        num_scalar_prefetch=1, grid=grid,
        in_specs=[pl.BlockSpec((tk, D_MODEL), lambda r, k, me: (jnp.where(in_w(r), k, last), 0)),
                  piece_w(0), piece_w(1), piece_x(0), piece_x(1),
                  pl.BlockSpec((None, D_MODEL, COLS_PER_DEV), lambda r, k, me: (block_x(r, k), 0, 0)),
                  rows(), vec(), rows()],
        out_specs=(rows(), vec(), ANY_SPEC, ANY_SPEC),
        scratch_shapes=[pltpu.VMEM((D_MODEL, COLS_PER_DEV), F32), pltpu.VMEM((2, D_MODEL, COLS_PER_DEV), BF16),
                        pltpu.VMEM((tm, D_MODEL), F32), pltpu.SemaphoreType.DMA((N_DEV,)),
                        pltpu.SemaphoreType.DMA((N_DEV,)), pltpu.SemaphoreType.DMA((2,))])
    me = 4 * lax.axis_index("x") + 2 * lax.axis_index("y") + lax.axis_index("c")
    blocks = jax.ShapeDtypeStruct((N_DEV, D_MODEL, COLS_PER_DEV), BF16)
    dx, dg, received, _ = _pcall(
        body, name=name, grid_spec=grid_spec,
        out_shape=(jax.ShapeDtypeStruct((s, D_MODEL), F32), jax.ShapeDtypeStruct((1, D_MODEL), F32), blocks, blocks),
        compiler_params=_params(("arbitrary", "arbitrary")),
    )(jnp.reshape(me, (1,)).astype(jnp.int32), h, du, du, du, du, w_all, x, g_pre, dy)
    return dx, dg, received


def _in_proj_bwd_w(h, du, name):
    s = h.shape[0]
    tk = min(512, s)
    n_k = s // tk

    def body(h_ref, dua_ref, dub_ref, out_ref, acc):
        k = pl.program_id(1)

        @pl.when(k == 0)
        def _():
            acc[...] = jnp.zeros_like(acc)
        acc[...] += _dot_tn(h_ref[...], jnp.concatenate([dua_ref[...], dub_ref[...]], axis=1))

        @pl.when(k == n_k - 1)
        def _():
            out_ref[...] = acc[...].astype(BF16)

    piece = lambda half: pl.BlockSpec((None, tk, WIDTH), lambda j, k: (_du_pieces_of_block(j)[half], k, 0))
    return _pcall(
        body, name=name, grid=(N_DEV, n_k),
        in_specs=[pl.BlockSpec((tk, D_MODEL), lambda j, k: (k, 0)), piece(0), piece(1)],
        out_specs=pl.BlockSpec((None, D_MODEL, COLS_PER_DEV), lambda j, k: (j, 0, 0)),
        out_shape=jax.ShapeDtypeStruct((N_DEV, D_MODEL, COLS_PER_DEV), BF16),
        scratch_shapes=[pltpu.VMEM((D_MODEL, COLS_PER_DEV), F32)],
        compiler_params=_params(("parallel", "arbitrary")),
    )(h, du, du)


def _adamw_math(g, w, m, v):
    m_new = ADAM_B1 * m + (1.0 - ADAM_B1) * g
    v_new = ADAM_B2 * v + (1.0 - ADAM_B2) * (g * g)
    m_hat = m_new / (1.0 - ADAM_B1 ** ADAM_STEP)
    v_hat = v_new / (1.0 - ADAM_B2 ** ADAM_STEP)
    delta = -ADAM_LR * (m_hat / (jnp.sqrt(v_hat) + ADAM_EPS) + ADAM_WD * w)
    return delta, m_new, v_new


def _sum_partials(p_ref):
    total = p_ref[0].astype(F32)
    for d in range(1, N_DEV):
        total = total + p_ref[d].astype(F32)
    return total


def _adamw_layers(parts0, parts1, w, m, v, name):
    _, r, c = w.shape
    tr = min(128, r)
    n_r = r // tr

    def body(p0_ref, p1_ref, w_ref, m_ref, v_ref, g_ref, d_ref, mo_ref, vo_ref):
        layer = pl.program_id(0)

        @pl.when(layer == 0)
        def _():
            g_ref[...] = _sum_partials(p0_ref)

        @pl.when(layer == 1)
        def _():
            g_ref[...] = _sum_partials(p1_ref)
        d_ref[...], mo_ref[...], vo_ref[...] = _adamw_math(g_ref[...], w_ref[...], m_ref[...], v_ref[...])

    part = lambda which: pl.BlockSpec((N_DEV, tr, c), lambda l, i: (0, jnp.where(l == which, i, 0), 0))
    par = lambda: pl.BlockSpec((None, tr, c), lambda l, i: (l, i, 0))
    out = jax.ShapeDtypeStruct(w.shape, F32)
    return _pcall(
        body, name=name, grid=(2, n_r),
        in_specs=[part(0), part(1), par(), par(), par()],
        out_specs=(par(), par(), par(), par()),
        out_shape=(out, out, out, out),
        compiler_params=_params(("arbitrary", "arbitrary")),
    )(parts0, parts1, w, m, v)


def _adamw_small(parts, w, m, v, name):
    def body(p_ref, w_ref, m_ref, v_ref, g_ref, d_ref, mo_ref, vo_ref):
        g = _sum_partials(p_ref)
        g_ref[...] = g
        d_ref[...], mo_ref[...], vo_ref[...] = _adamw_math(g, w_ref[...], m_ref[...], v_ref[...])

    out = jax.ShapeDtypeStruct(w.shape, F32)
    return _pcall(body, name=name, out_shape=(out, out, out, out), compiler_params=_params())(parts, w, m, v)


def _adamw_plain(g, w, m, v, name):
    def body(g_ref, w_ref, m_ref, v_ref, d_ref, mo_ref, vo_ref):
        d_ref[...], mo_ref[...], vo_ref[...] = _adamw_math(g_ref[...], w_ref[...], m_ref[...], v_ref[...])

    out = jax.ShapeDtypeStruct(w.shape, F32)
    return _pcall(body, name=name, out_shape=(out, out, out), compiler_params=_params())(g, w, m, v)


def _rows128(a):
    return a.reshape(-1, LANES)


SMALL_NAMES = ("pre_norm_g", "pool_w", "pool_scale", "conv_w", "conv_b", "post_norm_g")


def kernel(x, pre_norm_g, w_in, pool_w, pool_scale, conv_w, conv_b, w_branch, w_out, post_norm_g, loss_target, m_pre_norm_g, m_w_in, m_pool_w, m_pool_scale, m_conv_w, m_conv_b, m_w_branch, m_w_out, m_post_norm_g, v_pre_norm_g, v_w_in, v_pool_w, v_pool_scale, v_conv_w, v_conv_b, v_w_branch, v_w_out, v_post_norm_g):
    me = 4 * lax.axis_index("x") + 2 * lax.axis_index("y") + lax.axis_index("c")
    x0 = x[0]
    target = loss_target[0]
    conv_cols = conv_w.shape[-1]

    conv_w_pad = jnp.pad(conv_w.reshape(2 * 3, conv_cols), ((0, 2), (0, LANES - conv_cols)))
    w_in_all = [None, None]
    w_in_all[0], cw_g = _gather_two_level([w_in[0].astype(BF16), conv_w_pad], "gather_w_in_0")
    conv_w_full = cw_g[:, :6, :conv_cols].reshape(N_DEV, 2, 3, conv_cols).transpose(1, 2, 0, 3).reshape(2, 3, WIDTH)
    later_weights = ([w_in[1].astype(BF16), w_branch.astype(BF16), w_out.astype(BF16)], True)

    saved = []
    xin = x0
    for l in range(2):
        u, h = _in_proj_fwd(xin, pre_norm_g[l:l + 1], w_in_all[l], f"in_proj_fwd_{l}")
        y_pool = _pool_fwd(u, pool_w[l], pool_scale[l:l + 1], f"pool_fwd_{l}")
        y_conv = _conv_fwd(u, conv_w_full[l], conv_b[l:l + 1], f"conv_fwd_{l}")
        if l == 0:
            o_sb, y_sb, w_in_all[1], wb_g, wo_all = _sb_fwd(u, f"sb_fwd_{l}", later_weights)
            wb_all = wb_g.transpose(1, 2, 3, 0, 4).reshape(2, 3, WIDTH, D_MODEL)
        else:
            o_sb, y_sb = _sb_fwd(u, f"sb_fwd_{l}")
        if l == 0:
            xout, merged, pre = _merge_out_fwd(y_pool, y_conv, y_sb, u, wb_all, wo_all, xin, post_norm_g[l:l + 1], l,
                                               f"merge_out_fwd_{l}")
        else:
            dy, merged, pre, loss_row = _merge_out_fwd(y_pool, y_conv, y_sb, u, wb_all, wo_all, xin,
                                                       post_norm_g[l:l + 1], l, f"merge_out_fwd_{l}", target)
        saved.append((xin, u, h, y_pool, y_conv, y_sb, o_sb, merged, pre))
        xin = xout

    small = [None, None]
    recv = [None, None]
    ready = []
    for l in (1, 0):
        xl, u, h, y_pool, y_conv, y_sb, o_sb, merged, pre = saved[l]
        dmerged, dwo, dg_post = _out_proj_bwd(dy, pre, post_norm_g[l:l + 1], merged, wo_all, l, f"out_proj_bwd_{l}")
        du, dyp, dyc, dys, dwb = _merge_bwd(dmerged, y_pool, y_conv, y_sb, u, wb_all, l, f"merge_bwd_{l}")
        dwb = dwb.reshape(N_DEV, 3 * WIDTH, D_MODEL // N_DEV)
        dwo = dwo.reshape(N_DEV, D_MODEL // N_DEV, D_MODEL)
        du, dcw, dcb = _conv_bwd(u, conv_w_full[l], conv_b[l:l + 1], dyc, du, f"conv_bwd_{l}")
        du, dpw, dps = _pool_bwd(u, pool_w[l], pool_scale[l:l + 1], dyp, du, f"pool_bwd_{l}")
        small[l] = dict(pool_w=dpw, pool_scale=dps, conv_w=dcw, conv_b=dcb, post_norm_g=dg_post)
        if l == 1:
            du, dk, dv = _sb_bwd(u, o_sb, dys, du, f"sb_bwd_{l}")
        else:
            small[l]["pre_norm_g"] = jnp.zeros((1, D_MODEL), F32)
            packed = jnp.concatenate(
                [_rows128(jnp.stack([small[0][n], small[1][n]])) for n in SMALL_NAMES]
                + [jnp.pad(loss_row, ((0, 7), (0, 0)))], axis=0)
            du, dk, dv, *got, packed_all = _sb_bwd(
                u, o_sb, dys, du, f"sb_bwd_{l}", (ready + [dwb, dwo, packed], (False,) * 5 + (True,)))
            recv[1] = got[:3]
        du = lax.dynamic_update_slice(du, jnp.stack([dk, dv]).astype(BF16), (DU_SB_KV[0], 0, 0))
        if l == 1:
            dwi = _in_proj_bwd_w(h, du, f"in_proj_bwd_w_{l}")
            ready = [dwi, dwb, dwo]
            dx, dg_pre = _in_proj_bwd_x(du, w_in_all[l], xl, pre_norm_g[l:l + 1], dy, f"in_proj_bwd_x_{l}")
            small[l]["pre_norm_g"] = dg_pre
        else:
            dx, dg_pre, got_dwi = _in_proj_bwd_send(h, du, w_in_all[l], xl, pre_norm_g[l:l + 1], dy,
                                                    f"in_proj_bwd_{l}")
            recv[0] = [got_dwi] + got[3:]
        dy = dx
    grad_x = dy[None]

    (g_pre_0_all,) = _exchange([_rows128(dg_pre)], True, "gather_g_pre_0")
    packed_all = lax.dynamic_update_slice(packed_all, g_pre_0_all, (0, 0, 0))
    sizes = dict(pre_norm_g=16, pool_w=1024, pool_scale=8, conv_w=24, conv_b=8, post_norm_g=16)
    n_rows = sum(sizes.values())
    loss = jnp.sum(packed_all[:, n_rows, 0])

    given = dict(pre_norm_g=(pre_norm_g, m_pre_norm_g, v_pre_norm_g), pool_w=(pool_w, m_pool_w, v_pool_w),
                 pool_scale=(pool_scale, m_pool_scale, v_pool_scale), conv_b=(conv_b, m_conv_b, v_conv_b),
                 post_norm_g=(post_norm_g, m_post_norm_g, v_post_norm_g))
    zeros_cw = jnp.zeros((sizes["conv_w"], LANES), F32)
    pack3 = [jnp.concatenate([zeros_cw if n == "conv_w" else _rows128(given[n][k]) for n in SMALL_NAMES], axis=0)
             for k in range(3)]
    sg, sd, sm, sv = _adamw_small(packed_all[:, :n_rows], pack3[0], pack3[1], pack3[2], "adamw_small")

    def unpack(buf, name, shape):
        start = 0
        for n in SMALL_NAMES:
            if n == name:
                return buf[start:start + sizes[n]].reshape(shape)
            start += sizes[n]

    out = {}
    for n in ("pre_norm_g", "pool_w", "pool_scale", "conv_b", "post_norm_g"):
        shape = given[n][0].shape
        out[n] = tuple(unpack(b, n, shape) for b in (sg, sd, sm, sv))
    g_cw = lax.dynamic_slice_in_dim(unpack(sg, "conv_w", (2, 3, WIDTH)), me * conv_cols, conv_cols, axis=2)
    cw2 = lambda a: a.reshape(6, conv_cols)
    d_cw, m_cw, v_cw = _adamw_plain(cw2(g_cw), cw2(conv_w), cw2(m_conv_w), cw2(v_conv_w), "adamw_conv_w")
    out["conv_w"] = (g_cw,) + tuple(a.reshape(2, 3, conv_cols) for a in (d_cw, m_cw, v_cw))

    out["w_in"] = _adamw_layers(recv[0][0], recv[1][0], w_in, m_w_in, v_w_in, "adamw_w_in")
    cols = D_MODEL // N_DEV
    wb3 = lambda a: a.reshape(2, 3 * WIDTH, cols)
    out["w_branch"] = tuple(a.reshape(2, 3, WIDTH, cols) for a in _adamw_layers(
        recv[0][1], recv[1][1], wb3(w_branch), wb3(m_w_branch), wb3(v_w_branch), "adamw_w_branch"))
    out["w_out"] = _adamw_layers(recv[0][2], recv[1][2], w_out, m_w_out, v_w_out, "adamw_w_out")

    order = ("pre_norm_g", "w_in", "pool_w", "pool_scale", "conv_w", "conv_b", "w_branch", "w_out", "post_norm_g")
    return (loss, grad_x) + tuple(out[n][k] for k in range(4) for n in order)
```

```python
import functools

import jax
import jax.numpy as jnp
from jax import lax
from jax.experimental import pallas as pl
from jax.experimental.pallas import tpu as pltpu

F32 = jnp.float32
BF16 = jnp.bfloat16

N_DEV = 8
D_MODEL = 1024
WIDTH = 512
N_IN = 8192
COLS_PER_DEV = N_IN // N_DEV
HEAD_DIM = 64
LANES = 128
SB_SCALE = HEAD_DIM ** -0.5
LOG2E = 1.4426950408889634
RMS_EPS = 1e-6
POOL_HALO = 16
CONV_HALO = 8
ADAM_LR, ADAM_B1, ADAM_B2, ADAM_EPS, ADAM_WD, ADAM_STEP = 0.001, 0.9, 0.999, 1e-08, 0.01, 10
VMEM_LIMIT = 60 * 1024 * 1024

CB_POOL_V, CB_POOL_G = 0, 4
CB_CONV_X, CB_CONV_GB, CB_CONV_GC, CB_CONV_G = 8, 12, 16, 20
CB_SB_Q, CB_SB_K, CB_SB_V, CB_SB_G = 24, 28, 32, 36
MERGE_BLOCK_1024 = 5

DU_PIECES = 16
DU_MERGE = (0, 6)
DU_POOL = (6, 2)
DU_CONV = (8, 4)
DU_SB_QG = (12, 2)
DU_SB_KV = (14, 2)


def _du_pieces_of_block(j):
    first, second = 2 * (j - 5), 2 * (j - 5) + 1
    for block, (a, b) in enumerate(((6, 7), (8, 9), (10, 11), (12, 14), (15, 13))):
        first = jnp.where(j == block, a, first)
        second = jnp.where(j == block, b, second)
    return first, second


def _pcall(body, **kw):
    return pl.pallas_call(body, **kw)


def _params(sem=None):
    if sem is None:
        return pltpu.CompilerParams(vmem_limit_bytes=VMEM_LIMIT)
    return pltpu.CompilerParams(dimension_semantics=sem, vmem_limit_bytes=VMEM_LIMIT)


def _sigmoid(x):
    return 1.0 / (1.0 + jnp.exp(-x))


def _dot(a, b):
    return jnp.dot(a, b, preferred_element_type=F32)


def _dot_nt(a, b):
    return lax.dot_general(a, b, (((1,), (1,)), ((), ())), preferred_element_type=F32)


def _dot_tn(a, b):
    return lax.dot_general(a, b, (((0,), (0,)), ((), ())), preferred_element_type=F32)


def _split_bf16(x):
    hi = x.astype(BF16)
    lo = (x - hi.astype(F32)).astype(BF16)
    return hi, lo


N_PEER = N_DEV - 1
ANY_SPEC = pl.BlockSpec(memory_space=pl.ANY)


def _exchange_copies(ins, outs, send_sems, recv_sems, local_sems, gather, with_recvs=True):
    n = len(ins)
    gathers = _per_array(gather, n)
    x, y, c = lax.axis_index("x"), lax.axis_index("y"), lax.axis_index("c")
    me = 4 * x + 2 * y + c
    flip = lambda v, bit: 1 - v if bit else v
    local, sends, recvs = [], [], []
    for a in range(n):
        src = ins[a] if gathers[a] else ins[a].at[me]
        local.append(pltpu.make_async_copy(src, outs[a].at[me], local_sems.at[a]))
    for k in range(N_PEER):
        px, py, pc = flip(x, ((k + 1) >> 2) & 1), flip(y, ((k + 1) >> 1) & 1), flip(c, (k + 1) & 1)
        peer_id = 4 * px + 2 * py + pc
        for a in range(n):
            src = ins[a] if gathers[a] else ins[a].at[peer_id]
            common = dict(src_ref=src, send_sem=send_sems.at[a * N_PEER + k], recv_sem=recv_sems.at[a * N_PEER + k],
                          device_id=(px, py, pc), device_id_type=pl.DeviceIdType.MESH)
            sends.append(pltpu.make_async_remote_copy(dst_ref=outs[a].at[me], **common))
            if with_recvs:
                recvs.append(pltpu.make_async_remote_copy(dst_ref=outs[a].at[peer_id], **common))
    return local, sends, recvs


def _exchange_start(ins, outs, sems, gather):
    local, sends, _ = _exchange_copies(ins, outs, *sems, gather, with_recvs=False)
    for cp in local + sends:
        cp.start()


def _exchange_wait(ins, outs, sems, gather):
    local, sends, recvs = _exchange_copies(ins, outs, *sems, gather)
    for cp in recvs:
        cp.wait_recv()
    for cp in sends:
        cp.wait_send()
    for cp in local:
        cp.wait()


def _per_array(gather, n):
    return tuple(gather) if isinstance(gather, (tuple, list)) else (gather,) * n


def _exchange_out_shapes(arrs, gather):
    return [jax.ShapeDtypeStruct((N_DEV,) + tuple(a.shape if g else a.shape[1:]), a.dtype)
            for a, g in zip(arrs, _per_array(gather, len(arrs)))]


def _gather_two_level(arrs, name):
    n = len(arrs)

    def body(*refs):
        ins, outs = refs[:n], refs[n:2 * n]
        send_sems, recv_sems, local_sems = refs[2 * n:]
        x, y, c = lax.axis_index("x"), lax.axis_index("y"), lax.axis_index("c")
        me, sibling = (x, y, c), (x, y, 1 - c)
        chips = [(1 - x, y), (x, 1 - y), (1 - x, 1 - y)]
        slot = lambda dev: 4 * dev[0] + 2 * dev[1] + dev[2]

        def copy(a, k, block, to, src=None):
            return pltpu.make_async_remote_copy(
                src_ref=outs[a].at[slot(block)] if src is None else src, dst_ref=outs[a].at[slot(block)],
                send_sem=send_sems.at[a * N_PEER + k], recv_sem=recv_sems.at[a * N_PEER + k],
                device_id=to, device_id_type=pl.DeviceIdType.MESH)

        local = [pltpu.make_async_copy(ins[a], outs[a].at[slot(me)], local_sems.at[a]) for a in range(n)]
        first = []
        for a in range(n):
            first.append(copy(a, 0, me, sibling, src=ins[a]))
            first += [copy(a, 1 + j, me, (*chip, c), src=ins[a]) for j, chip in enumerate(chips)]
        for cp in local + first:
            cp.start()
        passed = []
        for j, chip in enumerate(chips):
            for a in range(n):
                copy(a, 1 + j, (*chip, c), me).wait_recv()
                passed.append(copy(a, 4 + j, (*chip, c), sibling))
                passed[-1].start()
        for a in range(n):
            copy(a, 0, sibling, me).wait_recv()
        for j, chip in enumerate(chips):
            for a in range(n):
                copy(a, 4 + j, (*chip, 1 - c), me).wait_recv()
        for cp in first + passed:
            cp.wait_send()
        for cp in local:
            cp.wait()

    return _pcall(
        body, name=name,
        out_shape=tuple(_exchange_out_shapes(arrs, True)),
        in_specs=[ANY_SPEC] * n, out_specs=tuple([ANY_SPEC] * n),
        scratch_shapes=_exchange_sems(n),
    )(*arrs)


def _exchange_sems(n):
    return [pltpu.SemaphoreType.DMA((n * N_PEER,)), pltpu.SemaphoreType.DMA((n * N_PEER,)),
            pltpu.SemaphoreType.DMA((n,))]


def _exchange(arrs, gather, name):
    n = len(arrs)

    def body(*refs):
        ins, outs, sems = refs[:n], refs[n:2 * n], refs[2 * n:]
        _exchange_start(ins, outs, sems, gather)
        _exchange_wait(ins, outs, sems, gather)

    return _pcall(
        body, name=name,
        out_shape=tuple(_exchange_out_shapes(arrs, gather)),
        in_specs=[ANY_SPEC] * n, out_specs=tuple([ANY_SPEC] * n),
        scratch_shapes=_exchange_sems(n),
    )(*arrs)


def _in_proj_fwd(x, g, w_all, name):
    s = x.shape[0]
    tm = min(1024, s)

    def body(x_ref, g_ref, w_ref, u_ref, h_ref, hs):
        @pl.when(pl.program_id(1) == 0)
        def _():
            xv = x_ref[...]
            r = lax.rsqrt(jnp.mean(xv * xv, axis=-1, keepdims=True) + RMS_EPS)
            hv = (xv * r * g_ref[...]).astype(BF16)
            hs[...] = hv
            h_ref[...] = hv
        u_ref[...] = _dot(hs[...], w_ref[...])

    return _pcall(
        body, name=name, grid=(s // tm, N_DEV),
        in_specs=[pl.BlockSpec((tm, D_MODEL), lambda i, j: (i, 0)),
                  pl.BlockSpec((1, D_MODEL), lambda i, j: (0, 0)),
                  pl.BlockSpec((None, D_MODEL, COLS_PER_DEV), lambda i, j: (j, 0, 0))],
        out_specs=(pl.BlockSpec((tm, COLS_PER_DEV), lambda i, j: (i, j)),
                   pl.BlockSpec((tm, D_MODEL), lambda i, j: (i, 0))),
        out_shape=(jax.ShapeDtypeStruct((s, N_IN), F32), jax.ShapeDtypeStruct((s, D_MODEL), BF16)),
        scratch_shapes=[pltpu.VMEM((tm, D_MODEL), BF16)],
        compiler_params=_params(("parallel", "arbitrary")),
    )(x, g, w_all)


def _pool_window(vs, t0, t, grp):
    ext = vs[pl.ds(t0, t + POOL_HALO), :]
    s2 = ext + pltpu.roll(ext, 1, 0)
    s4 = s2 + pltpu.roll(s2, 2, 0)
    s8 = s4 + pltpu.roll(s4, 4, 0)
    s16 = s8 + pltpu.roll(s8, 8, 0)
    sel = jnp.where(grp == 0, s2, jnp.where(grp == 1, s4, jnp.where(grp == 2, s8, s16)))
    return sel[POOL_HALO:, :], ext[POOL_HALO:, :]


def _pool_count(t0, t, grp):
    pos = t0 + lax.broadcasted_iota(jnp.int32, (t, 1), 0)
    return jnp.minimum(pos + 1, jnp.left_shift(2, grp)).astype(F32)


def _pool_fwd(u, pool_w, pool_scale, name):
    s = u.shape[0]
    t = min(256, s)

    def body(pv_ref, pg_ref, w_ref, sc_ref, y_ref, vs):
        grp = pl.program_id(0)
        vs[0:POOL_HALO, :] = jnp.zeros((POOL_HALO, LANES), F32)
        vs[POOL_HALO:, :] = pv_ref[...]
        wb = w_ref[...].astype(BF16)
        scale = sc_ref[...]

        def tile(i, carry):
            t0 = pl.multiple_of(i * t, t)
            win, v = _pool_window(vs, t0, t, grp)
            pooled = win / _pool_count(t0, t, grp) - v
            mixed = _dot(pooled.astype(BF16), wb)
            gate = pg_ref[pl.ds(t0, t), :]
            y_ref[pl.ds(t0, t), :] = (mixed * scale * (gate * _sigmoid(gate))).astype(BF16)
            return carry

        lax.fori_loop(0, s // t, tile, 0)

    return _pcall(
        body, name=name, grid=(4,),
        in_specs=[pl.BlockSpec((s, LANES), lambda g: (0, CB_POOL_V + g)),
                  pl.BlockSpec((s, LANES), lambda g: (0, CB_POOL_G + g)),
                  pl.BlockSpec((None, LANES, LANES), lambda g: (g, 0, 0)),
                  pl.BlockSpec((1, LANES), lambda g: (0, g))],
        out_specs=pl.BlockSpec((s, LANES), lambda g: (0, g)),
        out_shape=jax.ShapeDtypeStruct((s, WIDTH), BF16),
        scratch_shapes=[pltpu.VMEM((POOL_HALO + s, LANES), F32)],
        compiler_params=_params(("arbitrary",)),
    )(u, u, pool_w, pool_scale)


def _conv_taps(zs, t0, t):
    ext = zs[pl.ds(t0, t + CONV_HALO), :]
    z0 = ext[CONV_HALO:, :]
    z1 = pltpu.roll(ext, 1, 0)[CONV_HALO:, :]
    z2 = pltpu.roll(ext, 2, 0)[CONV_HALO:, :]
    return z0, z1, z2


def _conv_fwd(u, conv_w, conv_b, name):
    s = u.shape[0]
    t = min(256, s)

    def body(xc_ref, gb_ref, gc_ref, cg_ref, w_ref, b_ref, y_ref, zs):
        zs[0:CONV_HALO, :] = jnp.zeros((CONV_HALO, LANES), F32)
        zs[CONV_HALO:, :] = gc_ref[...] * xc_ref[...]
        w0, w1, w2 = w_ref[0:1, :], w_ref[1:2, :], w_ref[2:3, :]
        bias = b_ref[...]

        def tile(i, carry):
            t0 = pl.multiple_of(i * t, t)
            z0, z1, z2 = _conv_taps(zs, t0, t)
            conv = w0 * z2 + w1 * z1 + w2 * z0
            gate = cg_ref[pl.ds(t0, t), :]
            y = gb_ref[pl.ds(t0, t), :] * (conv + bias) * (gate * _sigmoid(gate))
            y_ref[pl.ds(t0, t), :] = y.astype(BF16)
            return carry

        lax.fori_loop(0, s // t, tile, 0)

    col = lambda base: pl.BlockSpec((s, LANES), lambda j: (0, base + j))
    return _pcall(
        body, name=name, grid=(4,),
        in_specs=[col(CB_CONV_X), col(CB_CONV_GB), col(CB_CONV_GC), col(CB_CONV_G),
                  pl.BlockSpec((3, LANES), lambda j: (0, j)),
                  pl.BlockSpec((1, LANES), lambda j: (0, j))],
        out_specs=pl.BlockSpec((s, LANES), lambda j: (0, j)),
        out_shape=jax.ShapeDtypeStruct((s, WIDTH), BF16),
        scratch_shapes=[pltpu.VMEM((CONV_HALO + s, LANES), F32)],
        compiler_params=_params(("arbitrary",)),
    )(u, u, u, u, conv_w, conv_b)


def _first_head_lanes(rows, width=LANES):
    lane = lax.broadcasted_iota(jnp.int32, (rows, width), 1)
    return jnp.bitwise_and(lane, LANES - 1) < HEAD_DIM


def _stack_heads(x, first):
    zero = jnp.zeros_like(x)
    return jnp.concatenate([jnp.where(first, x, zero), jnp.where(first, zero, x)], axis=0).astype(BF16)


def _causal_mask(tq, tk, copies):
    row = lax.broadcasted_iota(jnp.int32, (tq, tk), 0)
    col = lax.broadcasted_iota(jnp.int32, (tq, tk), 1)
    return jnp.concatenate([col < row] * copies, axis=0)


def _suffix_matrix(tk, inclusive, parts):
    r = lax.broadcasted_iota(jnp.int32, (parts * tk, 2 * tk), 0)
    c = lax.broadcasted_iota(jnp.int32, (parts * tk, 2 * tk), 1)
    r = jnp.bitwise_and(r, tk - 1)
    tri = (r >= c) if inclusive else (r > c)
    return jnp.where(c >= tk, 1.0, jnp.where(tri, 1.0, 0.0)).astype(BF16)


def _suffix_sums(x, m):
    hi, lo = _split_bf16(x)
    return _dot(jnp.concatenate([hi, lo], axis=1), m)


def _sb_log_terms(z, mask, m_strict):
    ls = jnp.minimum(z, 0.0) - jnp.log(1.0 + jnp.exp2(jnp.abs(z) * -LOG2E))
    lk = ls - z
    if mask is not None:
        lk = jnp.where(mask, lk, 0.0)
    return ls, _dot(lk.astype(BF16), m_strict)


SB_PAIRS = 4


def _pair_lanes(a):
    return slice(a * LANES, (a + 1) * LANES)


def _sb_fwd(u, name, xchg=None):
    s = u.shape[0]
    tq = tk = min(128, s)
    pairs = SB_PAIRS
    width = pairs * LANES
    rows = 2 * pairs * tq
    x_arrs, x_gather = xchg if xchg else ((), True)
    n_x = len(x_arrs)
    grid = (4 // pairs, s // tq)

    def body(*refs):
        q_ref, k_ref, v_ref, g_ref = refs[:4]
        x_in, refs = refs[4:4 + n_x], refs[4 + n_x:]
        o_ref, y_ref = refs[:2]
        x_out, refs = refs[2:2 + n_x], refs[2 + n_x:]
        kbf, vst, z_s, ell_s, carry_s = refs[:5]
        x_sems = refs[5:]
        i = pl.program_id(1)
        if n_x:
            @pl.when((pl.program_id(0) == 0) & (i == 0))
            def _():
                _exchange_start(x_in, x_out, x_sems, x_gather)

        @pl.when(i == 0)
        def _():
            kbf[...] = k_ref[...].astype(BF16)
            first_s = _first_head_lanes(s, width)
            vf = v_ref[...]
            vst[0] = jnp.where(first_s, vf, 0.0).astype(BF16)
            vst[1] = jnp.where(first_s, 0.0, vf).astype(BF16)

        first = _first_head_lanes(tq)
        mask = _causal_mask(tq, tk, 2 * pairs)
        m_strict = _suffix_matrix(tk, False, 1)
        qcat = jnp.concatenate([_stack_heads(q_ref[:, _pair_lanes(a)] * SB_SCALE, first) for a in range(pairs)],
                               axis=0)

        def scores(b):
            off = pl.multiple_of(jnp.maximum(b, 0) * tk, tk)
            z_s[...] = jnp.concatenate(
                [_dot_nt(qcat[a * 2 * tq:(a + 1) * 2 * tq], kbf[pl.ds(off, tk), _pair_lanes(a)])
                 for a in range(pairs)], axis=0)

        def log_weights(m):
            ls, cs = _sb_log_terms(z_s[...], m, m_strict)
            carry = carry_s[...]
            ell_s[...] = ls + cs[:, :tk] + carry
            carry_s[...] = carry + cs[:, tk:]

        def consume(b, accs, m):
            w = jnp.exp(ell_s[...])
            if m is not None:
                w = jnp.where(m, w, 0.0)
            wb = w.astype(BF16)
            off = pl.multiple_of(b * tk, tk)
            new = []
            for a in range(pairs):
                r0 = a * 2 * tq
                wcat = jnp.concatenate([wb[r0:r0 + tq], wb[r0 + tq:r0 + 2 * tq]], axis=1)
                vcat = jnp.concatenate([vst[0, pl.ds(off, tk), _pair_lanes(a)], vst[1, pl.ds(off, tk), _pair_lanes(a)]],
                                       axis=0)
                new.append(accs[a] + _dot(wcat, vcat))
            return tuple(new)

        carry_s[...] = jnp.zeros((rows, tk), F32)
        scores(i)
        log_weights(mask)
        scores(i - 1)
        accs = consume(i, tuple(jnp.zeros((tq, LANES), F32) for _ in range(pairs)), mask)
        log_weights(None)
        scores(i - 2)

        def step(n, accs):
            accs = consume(i - n, accs, None)
            log_weights(None)
            scores(i - n - 2)
            return accs

        accs = lax.fori_loop(1, i + 1, step, accs)
        o = jnp.concatenate(accs, axis=1)
        o_ref[...] = o
        gate = g_ref[...]
        y_ref[...] = (o * (gate * _sigmoid(gate))).astype(BF16)
        if n_x:
            @pl.when((pl.program_id(0) == grid[0] - 1) & (i == grid[1] - 1))
            def _():
                _exchange_wait(x_in, x_out, x_sems, x_gather)

    base = lambda cb: cb // pairs
    qblk = lambda cb: pl.BlockSpec((tq, width), lambda p, i: (i, base(cb) + p))
    full = lambda cb: pl.BlockSpec((s, width), lambda p, i: (0, base(cb) + p), pipeline_mode=pl.Buffered(1))
    state = pltpu.VMEM((rows, tk), F32)
    return _pcall(
        body, name=name, grid=grid,
        in_specs=[qblk(CB_SB_Q), full(CB_SB_K), full(CB_SB_V), qblk(CB_SB_G)] + [ANY_SPEC] * n_x,
        out_specs=(qblk(0), qblk(0)) + (ANY_SPEC,) * n_x,
        out_shape=(jax.ShapeDtypeStruct((s, WIDTH), F32), jax.ShapeDtypeStruct((s, WIDTH), BF16))
        + tuple(_exchange_out_shapes(x_arrs, x_gather)),
        scratch_shapes=[pltpu.VMEM((s, width), BF16), pltpu.VMEM((2, s, width), BF16), state, state, state]
        + (_exchange_sems(n_x) if n_x else []),
        compiler_params=_params(("arbitrary", "arbitrary")),
    )(u, u, u, u, *x_arrs)


def _merge_out_fwd(y_pool, y_conv, y_sb, u, wb_all, wo_all, x, g_post, layer, name, target=None):
    s = x.shape[0]
    tm = min(512, s)
    n_tiles = s // tm
    with_loss = target is not None

    def body(yp, yc, ys, m0, m1, m2, wb_ref, wo_ref, x_ref, g_ref, *rest):
        merged = jnp.zeros((tm, D_MODEL), F32)
        for n, (y_ref, m_ref) in enumerate(((yp, m0), (yc, m1), (ys, m2))):
            merged = merged + _sigmoid(m_ref[...]) * _dot(y_ref[...], wb_ref[n])
        mb = merged.astype(BF16)
        pre = _dot(mb, wo_ref[...].reshape(D_MODEL, D_MODEL))
        r = lax.rsqrt(jnp.mean(pre * pre, axis=-1, keepdims=True) + RMS_EPS)
        y = x_ref[...] + pre * r * g_ref[...]
        if not with_loss:
            out_ref, merged_ref, pre_ref = rest
            out_ref[...] = y
        else:
            t_ref, out_ref, merged_ref, pre_ref, loss_ref, acc = rest
            i = pl.program_id(0)

            @pl.when(i == 0)
            def _():
                acc[...] = jnp.zeros_like(acc)
            err = y - t_ref[...]
            out_ref[...] = err / D_MODEL
            acc[...] += jnp.sum(err * err, axis=0, keepdims=True)

            @pl.when(i == n_tiles - 1)
            def _():
                total = jnp.sum(acc[...], axis=1, keepdims=True) * (0.5 / D_MODEL)
                loss_ref[...] = jnp.broadcast_to(total, (1, LANES))
        merged_ref[...] = mb
        pre_ref[...] = pre

    rows = lambda w: pl.BlockSpec((tm, w), lambda i: (i, 0))
    merge = lambda n: pl.BlockSpec((tm, D_MODEL), lambda i: (i, MERGE_BLOCK_1024 + n))
    out_specs = (rows(D_MODEL), rows(D_MODEL), rows(D_MODEL))
    out_shape = (jax.ShapeDtypeStruct((s, D_MODEL), F32), jax.ShapeDtypeStruct((s, D_MODEL), BF16),
                 jax.ShapeDtypeStruct((s, D_MODEL), F32))
    if with_loss:
        out_specs += (pl.BlockSpec((1, LANES), lambda i: (0, 0)),)
        out_shape += (jax.ShapeDtypeStruct((1, LANES), F32),)
    return _pcall(
        body, name=name, grid=(n_tiles,),
        in_specs=[rows(WIDTH), rows(WIDTH), rows(WIDTH), merge(0), merge(1), merge(2),
                  pl.BlockSpec((None, 3, WIDTH, D_MODEL), lambda i: (layer, 0, 0, 0)),
                  pl.BlockSpec((N_DEV, None, D_MODEL // N_DEV, D_MODEL), lambda i: (0, layer, 0, 0)),
                  rows(D_MODEL), pl.BlockSpec((1, D_MODEL), lambda i: (0, 0))] + ([rows(D_MODEL)] if with_loss else []),
        out_specs=out_specs, out_shape=out_shape,
        scratch_shapes=[pltpu.VMEM((1, D_MODEL), F32)] if with_loss else [],
        compiler_params=_params(("arbitrary",)),
    )(y_pool, y_conv, y_sb, u, u, u, wb_all, wo_all, x, g_post, *([target] if with_loss else []))


def _out_proj_bwd(dy, pre, g_post, merged, wo_all, layer, name):
    s = dy.shape[0]
    tm = min(512, s)
    n_tiles = s // tm

    def body(dy_ref, pre_ref, g_ref, mg_ref, wo_ref, dm_ref, dwo_ref, dg_ref, acc):
        i = pl.program_id(0)

        @pl.when(i == 0)
        def _():
            acc[...] = jnp.zeros_like(acc)
            dg_ref[...] = jnp.zeros_like(dg_ref)
        dyv, pre_v = dy_ref[...], pre_ref[...]
        r = lax.rsqrt(jnp.mean(pre_v * pre_v, axis=-1, keepdims=True) + RMS_EPS)
        dg_ref[...] += jnp.sum(dyv * pre_v * r, axis=0, keepdims=True)
        a = dyv * g_ref[...]
        dpre = r * a - pre_v * (r * r * r) * jnp.mean(a * pre_v, axis=-1, keepdims=True)
        db = dpre.astype(BF16)
        acc[...] += _dot_tn(mg_ref[...], db)
        dm_ref[...] = _dot_nt(db, wo_ref[...].reshape(D_MODEL, D_MODEL))

        @pl.when(i == n_tiles - 1)
        def _():
            dwo_ref[...] = acc[...].astype(BF16)

    rows = lambda: pl.BlockSpec((tm, D_MODEL), lambda i: (i, 0))
    return _pcall(
        body, name=name, grid=(n_tiles,),
        in_specs=[rows(), rows(), pl.BlockSpec((1, D_MODEL), lambda i: (0, 0)), rows(),
                  pl.BlockSpec((N_DEV, None, D_MODEL // N_DEV, D_MODEL), lambda i: (0, layer, 0, 0))],
        out_specs=(rows(), pl.BlockSpec((D_MODEL, D_MODEL), lambda i: (0, 0)),
                   pl.BlockSpec((1, D_MODEL), lambda i: (0, 0))),
        out_shape=(jax.ShapeDtypeStruct((s, D_MODEL), F32), jax.ShapeDtypeStruct((D_MODEL, D_MODEL), BF16),
                   jax.ShapeDtypeStruct((1, D_MODEL), F32)),
        scratch_shapes=[pltpu.VMEM((D_MODEL, D_MODEL), F32)],
        compiler_params=_params(("arbitrary",)),
    )(dy, pre, g_post, merged, wo_all)


def _merge_bwd(dmerged, y_pool, y_conv, y_sb, u, wb_all, layer, name):
    s = dmerged.shape[0]
    tm = min(512, s)
    n_tiles = s // tm
    cols = D_MODEL // N_DEV

    def body(dm_ref, yp, yc, ys, m0, m1, m2, wb_ref, du_ref, dyp, dyc, dys, dwb_ref, acc):
        i = pl.program_id(0)

        @pl.when(i == 0)
        def _():
            acc[...] = jnp.zeros_like(acc)
        dm = dm_ref[...]
        for n, (y_ref, m_ref, dy_ref) in enumerate(((yp, m0, dyp), (yc, m1, dyc), (ys, m2, dys))):
            yv = y_ref[...]
            wb = wb_ref[n]
            gate = _sigmoid(m_ref[...])
            proj = _dot(yv, wb)
            dgate = (dm * proj * gate * (1.0 - gate)).astype(BF16)
            du_ref[2 * n] = dgate[:, :WIDTH]
            du_ref[2 * n + 1] = dgate[:, WIDTH:]
            dproj = (dm * gate).astype(BF16)
            acc[n] += _dot_tn(yv, dproj)
            dy_ref[...] = _dot_nt(dproj, wb)

        @pl.when(i == n_tiles - 1)
        def _():
            for j in range(N_DEV):
                for n in range(3):
                    dwb_ref[j, n] = acc[n, :, j * cols:(j + 1) * cols].astype(BF16)

    rows = lambda w: pl.BlockSpec((tm, w), lambda i: (i, 0))
    merge = lambda n: pl.BlockSpec((tm, D_MODEL), lambda i: (i, MERGE_BLOCK_1024 + n))
    return _pcall(
        body, name=name, grid=(n_tiles,),
        in_specs=[rows(D_MODEL), rows(WIDTH), rows(WIDTH), rows(WIDTH), merge(0), merge(1), merge(2),
                  pl.BlockSpec((None, 3, WIDTH, D_MODEL), lambda i: (layer, 0, 0, 0))],
        out_specs=(pl.BlockSpec((DU_MERGE[1], tm, WIDTH), lambda i: (DU_MERGE[0] // DU_MERGE[1], i, 0)),
                   rows(WIDTH), rows(WIDTH), rows(WIDTH),
                   pl.BlockSpec((N_DEV, 3, WIDTH, cols), lambda i: (0, 0, 0, 0))),
        out_shape=(jax.ShapeDtypeStruct((DU_PIECES, s, WIDTH), BF16),
                   jax.ShapeDtypeStruct((s, WIDTH), F32), jax.ShapeDtypeStruct((s, WIDTH), F32),
                   jax.ShapeDtypeStruct((s, WIDTH), F32),
                   jax.ShapeDtypeStruct((N_DEV, 3, WIDTH, cols), BF16)),
        scratch_shapes=[pltpu.VMEM((3, WIDTH, D_MODEL), F32)],
        compiler_params=_params(("arbitrary",)),
    )(dmerged, y_pool, y_conv, y_sb, u, u, u, wb_all)


def _sb_bwd(u, o, dys, du, name, xchg=None):
    s = u.shape[0]
    tq = tk = min(128, s)
    pairs = SB_PAIRS
    width = pairs * LANES
    assert width == WIDTH
    rows = 2 * pairs * tq
    pair_rows = lambda a: slice(a * 2 * tq, (a + 1) * 2 * tq)

    x_arrs, x_gather = xchg if xchg else ((), True)
    n_x = len(x_arrs)
    grid = (4 // pairs, s // tq)

    def body(*refs):
        q_ref, k_ref, v_ref, g_ref, o_ref, dys_ref = refs[:6]
        x_in, refs = refs[7:7 + n_x], refs[7 + n_x:]
        du_ref, dk_ref, dv_ref = refs[:3]
        dq_ref, dg_ref = du_ref.at[0], du_ref.at[1]
        x_out, refs = refs[3:3 + n_x], refs[3 + n_x:]
        kbf, vbf, kst, z_s, ell_s, ls_s, cl_s, wb_s, g_s, bef_s, cg_s, beta_s = refs[:12]
        x_sems = refs[12:]
        i = pl.program_id(1)
        if n_x:
            @pl.when((pl.program_id(0) == 0) & (i == 0))
            def _():
                _exchange_start(x_in, x_out, x_sems, x_gather)

        @pl.when(i == 0)
        def _():
            dk_ref[...] = jnp.zeros_like(dk_ref)
            dv_ref[...] = jnp.zeros_like(dv_ref)
            kf = k_ref[...]
            kbf[...] = kf.astype(BF16)
            vbf[...] = v_ref[...].astype(BF16)
            first_s = _first_head_lanes(s, width)
            kst[0] = jnp.where(first_s, kf, 0.0).astype(BF16)
            kst[1] = jnp.where(first_s, 0.0, kf).astype(BF16)

        first = _first_head_lanes(tq)
        mask = _causal_mask(tq, tk, 2 * pairs)
        m_strict = _suffix_matrix(tk, False, 1)
        m_incl = _suffix_matrix(tk, True, 2)

        gate = g_ref[...]
        sg = _sigmoid(gate)
        dy = dys_ref[...]
        ov = o_ref[...]
        dg_ref[...] = (dy * ov * (sg * (1.0 + gate * (1.0 - sg)))).astype(BF16)
        do = (dy * (gate * sg)).astype(BF16)
        prod = do.astype(F32) * ov
        row_sum = lambda v: jnp.broadcast_to(jnp.sum(v, axis=1, keepdims=True), (tq, tk))
        dsum, docat, qcat = [], [], []
        for a in range(pairs):
            pa = prod[:, _pair_lanes(a)]
            dsum += [row_sum(jnp.where(first, pa, 0.0)), row_sum(jnp.where(first, 0.0, pa))]
            docat.append(_stack_heads(do[:, _pair_lanes(a)], first))
            qcat.append(_stack_heads(q_ref[:, _pair_lanes(a)] * SB_SCALE, first))
        dsum = jnp.concatenate(dsum, axis=0)

        def block_start(b):
            return pl.multiple_of(jnp.maximum(b, 0) * tk, tk)

        def scores(b):
            off = block_start(b)
            z_s[...] = jnp.concatenate([_dot_nt(qcat[a], kbf[pl.ds(off, tk), _pair_lanes(a)]) for a in range(pairs)],
                                       axis=0)

        def log_weights(m):
            ls, cs = _sb_log_terms(z_s[...], m, m_strict)
            cl = cl_s[...]
            ell_s[...] = ls + cs[:, :tk] + cl
            cl_s[...] = cl + cs[:, tk:]
            ls_s[...] = ls

        def weights(b, m):
            off = block_start(b)
            dwt = jnp.concatenate([_dot_nt(docat[a], vbf[pl.ds(off, tk), _pair_lanes(a)]) for a in range(pairs)],
                                  axis=0)
            w = jnp.exp(ell_s[...])
            if m is not None:
                w = jnp.where(m, w, 0.0)
            wb = w.astype(BF16)
            g = dwt * wb.astype(F32)
            gs = _suffix_sums(g, m_incl)
            cg = cg_s[...]
            beta = jnp.exp(ls_s[...])
            wb_s[...] = wb
            beta_s[...] = beta
            g_s[...] = g * (1.0 - beta)
            bef_s[...] = gs[:, :tk] + cg
            cg_s[...] = cg + gs[:, tk:]

        def grads(b, dqs, m):
            dz = g_s[...] - beta_s[...] * (dsum - bef_s[...])
            if m is not None:
                dz = jnp.where(m, dz, 0.0)
            dzb = dz.astype(BF16)
            wb = wb_s[...]
            off = pl.multiple_of(b * tk, tk)
            new = []
            for a in range(pairs):
                r0 = a * 2 * tq
                kcat = jnp.concatenate([kst[0, pl.ds(off, tk), _pair_lanes(a)], kst[1, pl.ds(off, tk), _pair_lanes(a)]],
                                       axis=0)
                new.append(dqs[a] + _dot(jnp.concatenate([dzb[r0:r0 + tq], dzb[r0 + tq:r0 + 2 * tq]], axis=1), kcat))
                dk_ref[pl.ds(off, tk), _pair_lanes(a)] += _dot_tn(dzb[pair_rows(a)], qcat[a])
                dv_ref[pl.ds(off, tk), _pair_lanes(a)] += _dot_tn(wb[pair_rows(a)], docat[a])
            return tuple(new)

        zero = jnp.zeros((rows, tk), F32)
        cl_s[...] = zero
        cg_s[...] = zero
        scores(i)
        log_weights(mask)
        scores(i - 1)
        weights(i, mask)
        log_weights(None)
        scores(i - 2)
        dqs = grads(i, tuple(jnp.zeros((tq, LANES), F32) for _ in range(pairs)), mask)
        weights(i - 1, None)
        log_weights(None)
        scores(i - 3)

        def step(n, dqs):
            dqs = grads(i - n, dqs, None)
            weights(i - n - 1, None)
            log_weights(None)
            scores(i - n - 3)
            return dqs

        dqs = lax.fori_loop(1, i + 1, step, dqs)
        dq_ref[...] = (jnp.concatenate(dqs, axis=1) * SB_SCALE).astype(BF16)
        if n_x:
            @pl.when((pl.program_id(0) == grid[0] - 1) & (i == grid[1] - 1))
            def _():
                _exchange_wait(x_in, x_out, x_sems, x_gather)

    base = lambda cb: cb // pairs
    qblk = lambda cb: pl.BlockSpec((tq, width), lambda p, i: (i, base(cb) + p))
    full = lambda cb: pl.BlockSpec((s, width), lambda p, i: (0, base(cb) + p), pipeline_mode=pl.Buffered(1))
    state = pltpu.VMEM((rows, tk), F32)
    return _pcall(
        body, name=name, grid=grid,
        in_specs=[qblk(CB_SB_Q), full(CB_SB_K), full(CB_SB_V), qblk(CB_SB_G), qblk(0), qblk(0), ANY_SPEC]
        + [ANY_SPEC] * n_x,
        out_specs=(pl.BlockSpec((DU_SB_QG[1], tq, WIDTH), lambda p, i: (DU_SB_QG[0] // DU_SB_QG[1], i, 0)),
                   full(0), full(0)) + (ANY_SPEC,) * n_x,
        out_shape=(jax.ShapeDtypeStruct(du.shape, du.dtype), jax.ShapeDtypeStruct((s, WIDTH), F32),
                   jax.ShapeDtypeStruct((s, WIDTH), F32)) + tuple(_exchange_out_shapes(x_arrs, x_gather)),
        input_output_aliases={6: 0},
        scratch_shapes=[pltpu.VMEM((s, width), BF16), pltpu.VMEM((s, width), BF16), pltpu.VMEM((2, s, width), BF16),
                        state, state, state, state, pltpu.VMEM((rows, tk), BF16),
                        state, state, state, state] + (_exchange_sems(n_x) if n_x else []),
        compiler_params=_params(("arbitrary", "arbitrary")),
    )(u, u, u, u, o, dys, du, *x_arrs)


def _conv_bwd(u, conv_w, conv_b, dyc, du, name):
    s = u.shape[0]
    t = min(256, s)
    n_tiles = s // t

    def body(xc_ref, gb_ref, gc_ref, cg_ref, w_ref, b_ref, dy_ref, du_in, du_ref, dw_ref, db_ref, zs, ds):
        dxc_ref, dgb_ref, dgc_ref, dcg_ref = (du_ref.at[p] for p in range(4))
        zs[0:CONV_HALO, :] = jnp.zeros((CONV_HALO, LANES), F32)
        zs[CONV_HALO:, :] = gc_ref[...] * xc_ref[...]
        ds[s:, :] = jnp.zeros((CONV_HALO, LANES), F32)
        w0, w1, w2 = w_ref[0:1, :], w_ref[1:2, :], w_ref[2:3, :]
        bias = b_ref[...]

        def first(i, sums):
            t0 = pl.multiple_of(i * t, t)
            z0, z1, z2 = _conv_taps(zs, t0, t)
            pre = w0 * z2 + w1 * z1 + w2 * z0 + bias
            gate = cg_ref[pl.ds(t0, t), :]
            sg = _sigmoid(gate)
            gb = gb_ref[pl.ds(t0, t), :]
            dy = dy_ref[pl.ds(t0, t), :]
            dcg_ref[pl.ds(t0, t), :] = (dy * gb * pre * (sg * (1.0 + gate * (1.0 - sg)))).astype(BF16)
            dgb_ref[pl.ds(t0, t), :] = (dy * pre * (gate * sg)).astype(BF16)
            dc = dy * gb * (gate * sg)
            ds[pl.ds(t0, t), :] = dc
            red = lambda v: jnp.sum(v, axis=0, keepdims=True)
            return (sums[0] + red(dc * z2), sums[1] + red(dc * z1), sums[2] + red(dc * z0), sums[3] + red(dc))

        zrow = jnp.zeros((1, LANES), F32)
        sw0, sw1, sw2, sb = lax.fori_loop(0, n_tiles, first, (zrow, zrow, zrow, zrow))
        dw_ref[0:1, :] = sw0
        dw_ref[1:2, :] = sw1
        dw_ref[2:3, :] = sw2
        db_ref[...] = sb

        def second(i, carry):
            t0 = pl.multiple_of(i * t, t)
            ext = ds[pl.ds(t0, t + CONV_HALO), :]
            n = t + CONV_HALO
            d0 = ext[:t, :]
            d1 = pltpu.roll(ext, n - 1, 0)[:t, :]
            d2 = pltpu.roll(ext, n - 2, 0)[:t, :]
            dz = w2 * d0 + w1 * d1 + w0 * d2
            dgc_ref[pl.ds(t0, t), :] = (dz * xc_ref[pl.ds(t0, t), :]).astype(BF16)
            dxc_ref[pl.ds(t0, t), :] = (dz * gc_ref[pl.ds(t0, t), :]).astype(BF16)
            return carry

        lax.fori_loop(0, n_tiles, second, 0)

    col = lambda base: pl.BlockSpec((s, LANES), lambda j: (0, base + j))
    first, count = DU_CONV
    return _pcall(
        body, name=name, grid=(4,),
        in_specs=[col(CB_CONV_X), col(CB_CONV_GB), col(CB_CONV_GC), col(CB_CONV_G),
                  pl.BlockSpec((3, LANES), lambda j: (0, j)), pl.BlockSpec((1, LANES), lambda j: (0, j)), col(0),
                  ANY_SPEC],
        out_specs=(pl.BlockSpec((count, s, LANES), lambda j: (first // count, 0, j)),
                   pl.BlockSpec((3, LANES), lambda j: (0, j)), pl.BlockSpec((1, LANES), lambda j: (0, j))),
        out_shape=(jax.ShapeDtypeStruct(du.shape, du.dtype),
                   jax.ShapeDtypeStruct((3, WIDTH), F32), jax.ShapeDtypeStruct((1, WIDTH), F32)),
        scratch_shapes=[pltpu.VMEM((CONV_HALO + s, LANES), F32), pltpu.VMEM((s + CONV_HALO, LANES), F32)],
        input_output_aliases={7: 0},
        compiler_params=_params(("arbitrary",)),
    )(u, u, u, u, conv_w, conv_b, dyc, du)


def _pool_bwd(u, pool_w, pool_scale, dyp, du, name):
    s = u.shape[0]
    t = min(256, s)
    n_tiles = s // t

    def body(pv_ref, pg_ref, w_ref, sc_ref, dy_ref, du_in, du_ref, dw_ref, dsc_ref, vs, es, dps):
        dpv_ref, dpg_ref = du_ref.at[0], du_ref.at[1]
        grp = pl.program_id(0)
        vs[0:POOL_HALO, :] = jnp.zeros((POOL_HALO, LANES), F32)
        vs[POOL_HALO:, :] = pv_ref[...]
        es[s:, :] = jnp.zeros((POOL_HALO, LANES), F32)
        wb = w_ref[...].astype(BF16)
        scale = sc_ref[...]

        def first(i, sums):
            dw, dsc = sums
            t0 = pl.multiple_of(i * t, t)
            win, v = _pool_window(vs, t0, t, grp)
            cnt = _pool_count(t0, t, grp)
            pb = (win / cnt - v).astype(BF16)
            mixed = _dot(pb, wb)
            gate = pg_ref[pl.ds(t0, t), :]
            sg = _sigmoid(gate)
            dy = dy_ref[pl.ds(t0, t), :]
            dpg_ref[pl.ds(t0, t), :] = (dy * (mixed * scale) * (sg * (1.0 + gate * (1.0 - sg)))).astype(BF16)
            dms = dy * (gate * sg)
            dsc = dsc + jnp.sum(dms * mixed, axis=0, keepdims=True)
            dmb = (dms * scale).astype(BF16)
            dw = dw + _dot_tn(pb, dmb)
            dpooled = _dot_nt(dmb, wb)
            dps[pl.ds(t0, t), :] = dpooled
            es[pl.ds(t0, t), :] = dpooled / cnt
            return dw, dsc

        dw, dsc = lax.fori_loop(0, n_tiles, first, (jnp.zeros((LANES, LANES), F32), jnp.zeros((1, LANES), F32)))
        dw_ref[...] = dw
        dsc_ref[...] = dsc

        def second(i, carry):
            t0 = pl.multiple_of(i * t, t)
            ext = es[pl.ds(t0, t + POOL_HALO), :]
            n = t + POOL_HALO
            f2 = ext + pltpu.roll(ext, n - 1, 0)
            f4 = f2 + pltpu.roll(f2, n - 2, 0)
            f8 = f4 + pltpu.roll(f4, n - 4, 0)
            f16 = f8 + pltpu.roll(f8, n - 8, 0)
            sel = jnp.where(grp == 0, f2, jnp.where(grp == 1, f4, jnp.where(grp == 2, f8, f16)))
            dpv_ref[pl.ds(t0, t), :] = (sel[:t, :] - dps[pl.ds(t0, t), :]).astype(BF16)
            return carry

        lax.fori_loop(0, n_tiles, second, 0)

    col = lambda base: pl.BlockSpec((s, LANES), lambda g: (0, base + g))
    first, count = DU_POOL
    return _pcall(
        body, name=name, grid=(4,),
        in_specs=[col(CB_POOL_V), col(CB_POOL_G), pl.BlockSpec((None, LANES, LANES), lambda g: (g, 0, 0)),
                  pl.BlockSpec((1, LANES), lambda g: (0, g)), col(0), ANY_SPEC],
        out_specs=(pl.BlockSpec((count, s, LANES), lambda g: (first // count, 0, g)),
                   pl.BlockSpec((None, LANES, LANES), lambda g: (g, 0, 0)),
                   pl.BlockSpec((1, LANES), lambda g: (0, g))),
        out_shape=(jax.ShapeDtypeStruct(du.shape, du.dtype),
                   jax.ShapeDtypeStruct((4, LANES, LANES), F32), jax.ShapeDtypeStruct((1, WIDTH), F32)),
        scratch_shapes=[pltpu.VMEM((POOL_HALO + s, LANES), F32), pltpu.VMEM((s + POOL_HALO, LANES), F32),
                        pltpu.VMEM((s, LANES), F32)],
        input_output_aliases={5: 0},
        compiler_params=_params(("arbitrary",)),
    )(u, u, pool_w, pool_scale, dyp, du)


def _in_proj_bwd_x(du, w_all, x, g_pre, dy, name):
    s = x.shape[0]
    tm = min(1024, s)
    grid = (s // tm, N_DEV)

    def body(dua_ref, dub_ref, w_ref, x_ref, g_ref, dy_ref, dx_ref, dg_ref, acc):
        i, k = pl.program_id(0), pl.program_id(1)

        @pl.when(k == 0)
        def _():
            acc[...] = jnp.zeros_like(acc)

        @pl.when((k == 0) & (i == 0))
        def _():
            dg_ref[...] = jnp.zeros_like(dg_ref)
        acc[...] += _dot_nt(jnp.concatenate([dua_ref[...], dub_ref[...]], axis=1), w_ref[...])

        @pl.when(k == N_DEV - 1)
        def _():
            dh, xv = acc[...], x_ref[...]
            r = lax.rsqrt(jnp.mean(xv * xv, axis=-1, keepdims=True) + RMS_EPS)
            dg_ref[...] += jnp.sum(dh * xv * r, axis=0, keepdims=True)
            a = dh * g_ref[...]
            dx_ref[...] = dy_ref[...] + r * a - xv * (r * r * r) * jnp.mean(a * xv, axis=-1, keepdims=True)

    rows = lambda: pl.BlockSpec((tm, D_MODEL), lambda i, k: (i, 0))
    vec = lambda: pl.BlockSpec((1, D_MODEL), lambda i, k: (0, 0))
    piece = lambda half: pl.BlockSpec((None, tm, WIDTH), lambda i, k: (_du_pieces_of_block(k)[half], i, 0))
    return _pcall(
        body, name=name, grid=grid,
        in_specs=[piece(0), piece(1), pl.BlockSpec((None, D_MODEL, COLS_PER_DEV), lambda i, k: (k, 0, 0)),
                  rows(), vec(), rows()],
        out_specs=(rows(), vec()),
        out_shape=(jax.ShapeDtypeStruct((s, D_MODEL), F32), jax.ShapeDtypeStruct((1, D_MODEL), F32)),
        scratch_shapes=[pltpu.VMEM((tm, D_MODEL), F32)],
        compiler_params=_params(("arbitrary", "arbitrary")),
    )(du, du, w_all, x, g_pre, dy)


ROW_OFFSETS = (6, 7, 2, 4, 3, 5, 0, 1)


def _in_proj_bwd_send(h, du, w_all, x, g_pre, dy, name):
    s = x.shape[0]
    tk = s // N_DEV
    tm = min(1024, s)
    n_i = s // tm
    grid = (N_DEV + n_i, N_DEV)
    last = N_DEV - 1
    def offset(row):
        return functools.reduce(lambda acc, rn: jnp.where(row == rn[0], rn[1], acc), enumerate(ROW_OFFSETS), 0)

    def body(me_ref, h_ref, duwa_ref, duwb_ref, duxa_ref, duxb_ref, w_ref, x_ref, g_ref, dy_ref,
             dx_ref, dg_ref, recv_ref, part_ref, acc_w, stage, acc_x, send_sems, recv_sems, park_sems):
        r, k = pl.program_id(0), pl.program_id(1)
        x_, y_, c_ = lax.axis_index("x"), lax.axis_index("y"), lax.axis_index("c")
        me = 4 * x_ + 2 * y_ + c_
        flip = lambda v, bit: 1 - v if bit else v
        peer = lambda n: (flip(x_, (n >> 2) & 1), flip(y_, (n >> 1) & 1), flip(c_, n & 1))

        def park(row):
            n = ROW_OFFSETS[row]
            dst = recv_ref.at[me] if n == 0 else part_ref.at[n]
            return pltpu.make_async_copy(stage.at[row % 2], dst, park_sems.at[row % 2])

        def send(n, landing=False):
            px, py, pc = peer(n)
            dst = recv_ref.at[4 * px + 2 * py + pc] if landing else recv_ref.at[me]
            return pltpu.make_async_remote_copy(
                src_ref=part_ref.at[n], dst_ref=dst, send_sem=send_sems.at[n], recv_sem=recv_sems.at[n],
                device_id=(px, py, pc), device_id_type=pl.DeviceIdType.MESH)

        def parked(row):
            park(row).wait()
            if ROW_OFFSETS[row] >= 1:
                send(ROW_OFFSETS[row]).start()

        @pl.when(r < N_DEV)
        def _():
            @pl.when(k == 0)
            def _():
                acc_w[...] = jnp.zeros_like(acc_w)
            acc_w[...] += _dot_tn(h_ref[...], jnp.concatenate([duwa_ref[...], duwb_ref[...]], axis=1))

            for row in range(N_DEV):
                @pl.when((k == last) & (r == row))
                def _():
                    if row >= 1:
                        parked(row - 1)
                    stage[row % 2] = acc_w[...].astype(BF16)
                    park(row).start()

        @pl.when(r >= N_DEV)
        def _():
            @pl.when(k == 0)
            def _():
                acc_x[...] = jnp.zeros_like(acc_x)

            @pl.when((k == 0) & (r == N_DEV))
            def _():
                dg_ref[...] = jnp.zeros_like(dg_ref)
                parked(last)
            acc_x[...] += _dot_nt(jnp.concatenate([duxa_ref[...], duxb_ref[...]], axis=1), w_ref[...])

            @pl.when(k == last)
            def _():
                dh, xv = acc_x[...], x_ref[...]
                rs = lax.rsqrt(jnp.mean(xv * xv, axis=-1, keepdims=True) + RMS_EPS)
                dg_ref[...] += jnp.sum(dh * xv * rs, axis=0, keepdims=True)
                a = dh * g_ref[...]
                dx_ref[...] = dy_ref[...] + rs * a - xv * (rs * rs * rs) * jnp.mean(a * xv, axis=-1, keepdims=True)

        @pl.when((r == grid[0] - 1) & (k == last))
        def _():
            for n in range(1, N_DEV):
                send(n).wait_send()
            for n in range(1, N_DEV):
                send(n, landing=True).wait_recv()

    in_w = lambda r: r < N_DEV
    row_x = lambda r: jnp.maximum(r - N_DEV, 0)
    rows = lambda: pl.BlockSpec((tm, D_MODEL), lambda r, k, me: (row_x(r), 0))
    vec = lambda: pl.BlockSpec((1, D_MODEL), lambda r, k, me: (0, 0))
    block_w = lambda r, me: jnp.bitwise_xor(me[0], offset(jnp.minimum(r, last)))
    block_x = lambda r, k: jnp.where(in_w(r), 0, k)
    piece_w = lambda half: pl.BlockSpec(
        (None, tk, WIDTH), lambda r, k, me: (_du_pieces_of_block(block_w(r, me))[half], jnp.where(in_w(r), k, last), 0))
    piece_x = lambda half: pl.BlockSpec(
        (None, tm, WIDTH), lambda r, k, me: (_du_pieces_of_block(block_x(r, k))[half], row_x(r), 0))
    grid_spec = pltpu.PrefetchScalarGridSpec(
        num_scalar_prefetch=1, grid=grid,
        in_specs=[pl.BlockSpec((tk, D_MODEL), lambda r, k, me: (jnp.where(in_w(r), k, last), 0)),
                  piece_w(0), piece_w(1), piece_x(0), piece_x(1),
                  pl.BlockSpec((None, D_MODEL, COLS_PER_DEV), lambda r, k, me: (block_x(r, k), 0, 0)),
                  rows(), vec(), rows()],
        out_specs=(rows(), vec(), ANY_SPEC, ANY_SPEC),
        scratch_shapes=[pltpu.VMEM((D_MODEL, COLS_PER_DEV), F32), pltpu.VMEM((2, D_MODEL, COLS_PER_DEV), BF16),
                        pltpu.VMEM((tm, D_MODEL), F32), pltpu.SemaphoreType.DMA((N_DEV,)),
                        pltpu.SemaphoreType.DMA((N_DEV,)), pltpu.SemaphoreType.DMA((2,))])
    me = 4 * lax.axis_index("x") + 2 * lax.axis_index("y") + lax.axis_index("c")
    blocks = jax.ShapeDtypeStruct((N_DEV, D_MODEL, COLS_PER_DEV), BF16)
    dx, dg, received, _ = _pcall(
        body, name=name, grid_spec=grid_spec,
        out_shape=(jax.ShapeDtypeStruct((s, D_MODEL), F32), jax.ShapeDtypeStruct((1, D_MODEL), F32), blocks, blocks),
        compiler_params=_params(("arbitrary", "arbitrary")),
    )(jnp.reshape(me, (1,)).astype(jnp.int32), h, du, du, du, du, w_all, x, g_pre, dy)
    return dx, dg, received


def _in_proj_bwd_w(h, du, name):
    s = h.shape[0]
    tk = min(1024, s)
    n_k = s // tk

    def body(h_ref, dua_ref, dub_ref, out_ref, acc):
        k = pl.program_id(1)

        @pl.when(k == 0)
        def _():
            acc[...] = jnp.zeros_like(acc)
        acc[...] += _dot_tn(h_ref[...], jnp.concatenate([dua_ref[...], dub_ref[...]], axis=1))

        @pl.when(k == n_k - 1)
        def _():
            out_ref[...] = acc[...].astype(BF16)

    piece = lambda half: pl.BlockSpec((None, tk, WIDTH), lambda j, k: (_du_pieces_of_block(j)[half], k, 0))
    return _pcall(
        body, name=name, grid=(N_DEV, n_k),
        in_specs=[pl.BlockSpec((tk, D_MODEL), lambda j, k: (k, 0)), piece(0), piece(1)],
        out_specs=pl.BlockSpec((None, D_MODEL, COLS_PER_DEV), lambda j, k: (j, 0, 0)),
        out_shape=jax.ShapeDtypeStruct((N_DEV, D_MODEL, COLS_PER_DEV), BF16),
        scratch_shapes=[pltpu.VMEM((D_MODEL, COLS_PER_DEV), F32)],
        compiler_params=_params(("parallel", "arbitrary")),
    )(h, du, du)


def _adamw_math(g, w, m, v):
    m_new = ADAM_B1 * m + (1.0 - ADAM_B1) * g
    v_new = ADAM_B2 * v + (1.0 - ADAM_B2) * (g * g)
    m_hat = m_new / (1.0 - ADAM_B1 ** ADAM_STEP)
    v_hat = v_new / (1.0 - ADAM_B2 ** ADAM_STEP)
    delta = -ADAM_LR * (m_hat / (jnp.sqrt(v_hat) + ADAM_EPS) + ADAM_WD * w)
    return delta, m_new, v_new


def _sum_partials(p_ref):
    total = p_ref[0].astype(F32)
    for d in range(1, N_DEV):
        total = total + p_ref[d].astype(F32)
    return total


def _adamw_layers(parts0, parts1, w, m, v, name):
    _, r, c = w.shape
    tr = min(128, r)
    n_r = r // tr

    def body(p0_ref, p1_ref, w_ref, m_ref, v_ref, g_ref, d_ref, mo_ref, vo_ref):
        layer = pl.program_id(0)

        @pl.when(layer == 0)
        def _():
            g_ref[...] = _sum_partials(p0_ref)

        @pl.when(layer == 1)
        def _():
            g_ref[...] = _sum_partials(p1_ref)
        d_ref[...], mo_ref[...], vo_ref[...] = _adamw_math(g_ref[...], w_ref[...], m_ref[...], v_ref[...])

    part = lambda which: pl.BlockSpec((N_DEV, tr, c), lambda l, i: (0, jnp.where(l == which, i, 0), 0))
    par = lambda: pl.BlockSpec((None, tr, c), lambda l, i: (l, i, 0))
    out = jax.ShapeDtypeStruct(w.shape, F32)
    return _pcall(
        body, name=name, grid=(2, n_r),
        in_specs=[part(0), part(1), par(), par(), par()],
        out_specs=(par(), par(), par(), par()),
        out_shape=(out, out, out, out),
        compiler_params=_params(("arbitrary", "arbitrary")),
    )(parts0, parts1, w, m, v)


def _adamw_small(parts, w, m, v, name):
    def body(p_ref, w_ref, m_ref, v_ref, g_ref, d_ref, mo_ref, vo_ref):
        g = _sum_partials(p_ref)
        g_ref[...] = g
        d_ref[...], mo_ref[...], vo_ref[...] = _adamw_math(g, w_ref[...], m_ref[...], v_ref[...])

    out = jax.ShapeDtypeStruct(w.shape, F32)
    return _pcall(body, name=name, out_shape=(out, out, out, out), compiler_params=_params())(parts, w, m, v)


def _adamw_plain(g, w, m, v, name):
    def body(g_ref, w_ref, m_ref, v_ref, d_ref, mo_ref, vo_ref):
        d_ref[...], mo_ref[...], vo_ref[...] = _adamw_math(g_ref[...], w_ref[...], m_ref[...], v_ref[...])

    out = jax.ShapeDtypeStruct(w.shape, F32)
    return _pcall(body, name=name, out_shape=(out, out, out), compiler_params=_params())(g, w, m, v)


def _rows128(a):
    return a.reshape(-1, LANES)


SMALL_NAMES = ("pre_norm_g", "pool_w", "pool_scale", "conv_w", "conv_b", "post_norm_g")


def kernel(x, pre_norm_g, w_in, pool_w, pool_scale, conv_w, conv_b, w_branch, w_out, post_norm_g, loss_target, m_pre_norm_g, m_w_in, m_pool_w, m_pool_scale, m_conv_w, m_conv_b, m_w_branch, m_w_out, m_post_norm_g, v_pre_norm_g, v_w_in, v_pool_w, v_pool_scale, v_conv_w, v_conv_b, v_w_branch, v_w_out, v_post_norm_g):
    me = 4 * lax.axis_index("x") + 2 * lax.axis_index("y") + lax.axis_index("c")
    x0 = x[0]
    target = loss_target[0]
    conv_cols = conv_w.shape[-1]

    conv_w_pad = jnp.pad(conv_w.reshape(2 * 3, conv_cols), ((0, 2), (0, LANES - conv_cols)))
    w_in_all = [None, None]
    w_in_all[0], cw_g = _gather_two_level([w_in[0].astype(BF16), conv_w_pad], "gather_w_in_0")
    conv_w_full = cw_g[:, :6, :conv_cols].reshape(N_DEV, 2, 3, conv_cols).transpose(1, 2, 0, 3).reshape(2, 3, WIDTH)
    later_weights = ([w_in[1].astype(BF16), w_branch.astype(BF16), w_out.astype(BF16)], True)

    saved = []
    xin = x0
    for l in range(2):
        u, h = _in_proj_fwd(xin, pre_norm_g[l:l + 1], w_in_all[l], f"in_proj_fwd_{l}")
        y_pool = _pool_fwd(u, pool_w[l], pool_scale[l:l + 1], f"pool_fwd_{l}")
        y_conv = _conv_fwd(u, conv_w_full[l], conv_b[l:l + 1], f"conv_fwd_{l}")
        if l == 0:
            o_sb, y_sb, w_in_all[1], wb_g, wo_all = _sb_fwd(u, f"sb_fwd_{l}", later_weights)
            wb_all = wb_g.transpose(1, 2, 3, 0, 4).reshape(2, 3, WIDTH, D_MODEL)
        else:
            o_sb, y_sb = _sb_fwd(u, f"sb_fwd_{l}")
        if l == 0:
            xout, merged, pre = _merge_out_fwd(y_pool, y_conv, y_sb, u, wb_all, wo_all, xin, post_norm_g[l:l + 1], l,
                                               f"merge_out_fwd_{l}")
        else:
            dy, merged, pre, loss_row = _merge_out_fwd(y_pool, y_conv, y_sb, u, wb_all, wo_all, xin,
                                                       post_norm_g[l:l + 1], l, f"merge_out_fwd_{l}", target)
        saved.append((xin, u, h, y_pool, y_conv, y_sb, o_sb, merged, pre))
        xin = xout

    small = [None, None]
    recv = [None, None]
    ready = []
    for l in (1, 0):
        xl, u, h, y_pool, y_conv, y_sb, o_sb, merged, pre = saved[l]
        dmerged, dwo, dg_post = _out_proj_bwd(dy, pre, post_norm_g[l:l + 1], merged, wo_all, l, f"out_proj_bwd_{l}")
        du, dyp, dyc, dys, dwb = _merge_bwd(dmerged, y_pool, y_conv, y_sb, u, wb_all, l, f"merge_bwd_{l}")
        dwb = dwb.reshape(N_DEV, 3 * WIDTH, D_MODEL // N_DEV)
        dwo = dwo.reshape(N_DEV, D_MODEL // N_DEV, D_MODEL)
        du, dcw, dcb = _conv_bwd(u, conv_w_full[l], conv_b[l:l + 1], dyc, du, f"conv_bwd_{l}")
        du, dpw, dps = _pool_bwd(u, pool_w[l], pool_scale[l:l + 1], dyp, du, f"pool_bwd_{l}")
        small[l] = dict(pool_w=dpw, pool_scale=dps, conv_w=dcw, conv_b=dcb, post_norm_g=dg_post)
        if l == 1:
            du, dk, dv = _sb_bwd(u, o_sb, dys, du, f"sb_bwd_{l}")
        else:
            small[l]["pre_norm_g"] = jnp.zeros((1, D_MODEL), F32)
            packed = jnp.concatenate(
                [_rows128(jnp.stack([small[0][n], small[1][n]])) for n in SMALL_NAMES]
                + [jnp.pad(loss_row, ((0, 7), (0, 0)))], axis=0)
            du, dk, dv, *got, packed_all = _sb_bwd(
                u, o_sb, dys, du, f"sb_bwd_{l}", (ready + [dwb, dwo, packed], (False,) * 5 + (True,)))
            recv[1] = got[:3]
        du = lax.dynamic_update_slice(du, jnp.stack([dk, dv]).astype(BF16), (DU_SB_KV[0], 0, 0))
        if l == 1:
            dwi = _in_proj_bwd_w(h, du, f"in_proj_bwd_w_{l}")
            ready = [dwi, dwb, dwo]
            dx, dg_pre = _in_proj_bwd_x(du, w_in_all[l], xl, pre_norm_g[l:l + 1], dy, f"in_proj_bwd_x_{l}")
            small[l]["pre_norm_g"] = dg_pre
        else:
            dx, dg_pre, got_dwi = _in_proj_bwd_send(h, du, w_in_all[l], xl, pre_norm_g[l:l + 1], dy,
                                                    f"in_proj_bwd_{l}")
            recv[0] = [got_dwi] + got[3:]
        dy = dx
    grad_x = dy[None]

    (g_pre_0_all,) = _exchange([_rows128(dg_pre)], True, "gather_g_pre_0")
    packed_all = lax.dynamic_update_slice(packed_all, g_pre_0_all, (0, 0, 0))
    sizes = dict(pre_norm_g=16, pool_w=1024, pool_scale=8, conv_w=24, conv_b=8, post_norm_g=16)
    n_rows = sum(sizes.values())
    loss = jnp.sum(packed_all[:, n_rows, 0])

    given = dict(pre_norm_g=(pre_norm_g, m_pre_norm_g, v_pre_norm_g), pool_w=(pool_w, m_pool_w, v_pool_w),
                 pool_scale=(pool_scale, m_pool_scale, v_pool_scale), conv_b=(conv_b, m_conv_b, v_conv_b),
                 post_norm_g=(post_norm_g, m_post_norm_g, v_post_norm_g))
    zeros_cw = jnp.zeros((sizes["conv_w"], LANES), F32)
    pack3 = [jnp.concatenate([zeros_cw if n == "conv_w" else _rows128(given[n][k]) for n in SMALL_NAMES], axis=0)
             for k in range(3)]
    sg, sd, sm, sv = _adamw_small(packed_all[:, :n_rows], pack3[0], pack3[1], pack3[2], "adamw_small")

    def unpack(buf, name, shape):
        start = 0
        for n in SMALL_NAMES:
            if n == name:
                return buf[start:start + sizes[n]].reshape(shape)
            start += sizes[n]

    out = {}
    for n in ("pre_norm_g", "pool_w", "pool_scale", "conv_b", "post_norm_g"):
        shape = given[n][0].shape
        out[n] = tuple(unpack(b, n, shape) for b in (sg, sd, sm, sv))
    g_cw = lax.dynamic_slice_in_dim(unpack(sg, "conv_w", (2, 3, WIDTH)), me * conv_cols, conv_cols, axis=2)
    cw2 = lambda a: a.reshape(6, conv_cols)
    d_cw, m_cw, v_cw = _adamw_plain(cw2(g_cw), cw2(conv_w), cw2(m_conv_w), cw2(v_conv_w), "adamw_conv_w")
    out["conv_w"] = (g_cw,) + tuple(a.reshape(2, 3, conv_cols) for a in (d_cw, m_cw, v_cw))

    out["w_in"] = _adamw_layers(recv[0][0], recv[1][0], w_in, m_w_in, v_w_in, "adamw_w_in")
    cols = D_MODEL // N_DEV
    wb3 = lambda a: a.reshape(2, 3 * WIDTH, cols)
    out["w_branch"] = tuple(a.reshape(2, 3, WIDTH, cols) for a in _adamw_layers(
        recv[0][1], recv[1][1], wb3(w_branch), wb3(m_w_branch), wb3(v_w_branch), "adamw_w_branch"))
    out["w_out"] = _adamw_layers(recv[0][2], recv[1][2], w_out, m_w_out, v_w_out, "adamw_w_out")

    order = ("pre_norm_g", "w_in", "pool_w", "pool_scale", "conv_w", "conv_b", "w_branch", "w_out", "post_norm_g")
    return (loss, grad_x) + tuple(out[n][k] for k in range(4) for n in order)
```

```python
import functools

import jax
import jax.numpy as jnp
from jax import lax
from jax.experimental import pallas as pl
from jax.experimental.pallas import tpu as pltpu

F32 = jnp.float32
BF16 = jnp.bfloat16

N_DEV = 8
D_MODEL = 1024
WIDTH = 512
N_IN = 8192
COLS_PER_DEV = N_IN // N_DEV
HEAD_DIM = 64
LANES = 128
SB_SCALE = HEAD_DIM ** -0.5
LOG2E = 1.4426950408889634
RMS_EPS = 1e-6
POOL_HALO = 16
CONV_HALO = 8
ADAM_LR, ADAM_B1, ADAM_B2, ADAM_EPS, ADAM_WD, ADAM_STEP = 0.001, 0.9, 0.999, 1e-08, 0.01, 10
VMEM_LIMIT = 60 * 1024 * 1024

CB_POOL_V, CB_POOL_G = 0, 4
CB_CONV_X, CB_CONV_GB, CB_CONV_GC, CB_CONV_G = 8, 12, 16, 20
CB_SB_Q, CB_SB_K, CB_SB_V, CB_SB_G = 24, 28, 32, 36
MERGE_BLOCK_1024 = 5

DU_PIECES = 16
DU_MERGE = (0, 6)
DU_POOL = (6, 2)
DU_CONV = (8, 4)
DU_SB_QG = (12, 2)
DU_SB_KV = (14, 2)


def _du_pieces_of_block(j):
    first, second = 2 * (j - 5), 2 * (j - 5) + 1
    for block, (a, b) in enumerate(((6, 7), (8, 9), (10, 11), (12, 14), (15, 13))):
        first = jnp.where(j == block, a, first)
        second = jnp.where(j == block, b, second)
    return first, second


def _pcall(body, **kw):
    return pl.pallas_call(body, **kw)


def _params(sem=None):
    if sem is None:
        return pltpu.CompilerParams(vmem_limit_bytes=VMEM_LIMIT)
    return pltpu.CompilerParams(dimension_semantics=sem, vmem_limit_bytes=VMEM_LIMIT)


def _sigmoid(x):
    return 1.0 / (1.0 + jnp.exp(-x))


def _dot(a, b):
    return jnp.dot(a, b, preferred_element_type=F32)


def _dot_nt(a, b):
    return lax.dot_general(a, b, (((1,), (1,)), ((), ())), preferred_element_type=F32)


def _dot_tn(a, b):
    return lax.dot_general(a, b, (((0,), (0,)), ((), ())), preferred_element_type=F32)


def _split_bf16(x):
    hi = x.astype(BF16)
    lo = (x - hi.astype(F32)).astype(BF16)
    return hi, lo


N_PEER = N_DEV - 1
ANY_SPEC = pl.BlockSpec(memory_space=pl.ANY)


def _exchange_copies(ins, outs, send_sems, recv_sems, local_sems, gather, with_recvs=True):
    n = len(ins)
    gathers = _per_array(gather, n)
    x, y, c = lax.axis_index("x"), lax.axis_index("y"), lax.axis_index("c")
    me = 4 * x + 2 * y + c
    flip = lambda v, bit: 1 - v if bit else v
    local, sends, recvs = [], [], []
    for a in range(n):
        src = ins[a] if gathers[a] else ins[a].at[me]
        local.append(pltpu.make_async_copy(src, outs[a].at[me], local_sems.at[a]))
    for k in range(N_PEER):
        px, py, pc = flip(x, ((k + 1) >> 2) & 1), flip(y, ((k + 1) >> 1) & 1), flip(c, (k + 1) & 1)
        peer_id = 4 * px + 2 * py + pc
        for a in range(n):
            src = ins[a] if gathers[a] else ins[a].at[peer_id]
            common = dict(src_ref=src, send_sem=send_sems.at[a * N_PEER + k], recv_sem=recv_sems.at[a * N_PEER + k],
                          device_id=(px, py, pc), device_id_type=pl.DeviceIdType.MESH)
            sends.append(pltpu.make_async_remote_copy(dst_ref=outs[a].at[me], **common))
            if with_recvs:
                recvs.append(pltpu.make_async_remote_copy(dst_ref=outs[a].at[peer_id], **common))
    return local, sends, recvs


def _exchange_start(ins, outs, sems, gather):
    local, sends, _ = _exchange_copies(ins, outs, *sems, gather, with_recvs=False)
    for cp in local + sends:
        cp.start()


def _exchange_wait(ins, outs, sems, gather):
    local, sends, recvs = _exchange_copies(ins, outs, *sems, gather)
    for cp in recvs:
        cp.wait_recv()
    for cp in sends:
        cp.wait_send()
    for cp in local:
        cp.wait()


def _per_array(gather, n):
    return tuple(gather) if isinstance(gather, (tuple, list)) else (gather,) * n


def _exchange_out_shapes(arrs, gather):
    return [jax.ShapeDtypeStruct((N_DEV,) + tuple(a.shape if g else a.shape[1:]), a.dtype)
            for a, g in zip(arrs, _per_array(gather, len(arrs)))]


def _gather_two_level(arrs, name):
    n = len(arrs)

    def body(*refs):
        ins, outs = refs[:n], refs[n:2 * n]
        send_sems, recv_sems, local_sems = refs[2 * n:]
        x, y, c = lax.axis_index("x"), lax.axis_index("y"), lax.axis_index("c")
        me, sibling = (x, y, c), (x, y, 1 - c)
        chips = [(1 - x, y), (x, 1 - y), (1 - x, 1 - y)]
        slot = lambda dev: 4 * dev[0] + 2 * dev[1] + dev[2]

        def copy(a, k, block, to, src=None):
            return pltpu.make_async_remote_copy(
                src_ref=outs[a].at[slot(block)] if src is None else src, dst_ref=outs[a].at[slot(block)],
                send_sem=send_sems.at[a * N_PEER + k], recv_sem=recv_sems.at[a * N_PEER + k],
                device_id=to, device_id_type=pl.DeviceIdType.MESH)

        local = [pltpu.make_async_copy(ins[a], outs[a].at[slot(me)], local_sems.at[a]) for a in range(n)]
        first = []
        for a in range(n):
            first.append(copy(a, 0, me, sibling, src=ins[a]))
            first += [copy(a, 1 + j, me, (*chip, c), src=ins[a]) for j, chip in enumerate(chips)]
        for cp in local + first:
            cp.start()
        passed = []
        for j, chip in enumerate(chips):
            for a in range(n):
                copy(a, 1 + j, (*chip, c), me).wait_recv()
                passed.append(copy(a, 4 + j, (*chip, c), sibling))
                passed[-1].start()
        for a in range(n):
            copy(a, 0, sibling, me).wait_recv()
        for j, chip in enumerate(chips):
            for a in range(n):
                copy(a, 4 + j, (*chip, 1 - c), me).wait_recv()
        for cp in first + passed:
            cp.wait_send()
        for cp in local:
            cp.wait()

    return _pcall(
        body, name=name,
        out_shape=tuple(_exchange_out_shapes(arrs, True)),
        in_specs=[ANY_SPEC] * n, out_specs=tuple([ANY_SPEC] * n),
        scratch_shapes=_exchange_sems(n),
    )(*arrs)


def _exchange_sems(n):
    return [pltpu.SemaphoreType.DMA((n * N_PEER,)), pltpu.SemaphoreType.DMA((n * N_PEER,)),
            pltpu.SemaphoreType.DMA((n,))]


def _exchange(arrs, gather, name):
    n = len(arrs)

    def body(*refs):
        ins, outs, sems = refs[:n], refs[n:2 * n], refs[2 * n:]
        _exchange_start(ins, outs, sems, gather)
        _exchange_wait(ins, outs, sems, gather)

    return _pcall(
        body, name=name,
        out_shape=tuple(_exchange_out_shapes(arrs, gather)),
        in_specs=[ANY_SPEC] * n, out_specs=tuple([ANY_SPEC] * n),
        scratch_shapes=_exchange_sems(n),
    )(*arrs)


def _in_proj_fwd(x, g, w_all, name):
    s = x.shape[0]
    tm = min(1024, s)

    def body(x_ref, g_ref, w_ref, u_ref, h_ref, hs):
        @pl.when(pl.program_id(1) == 0)
        def _():
            xv = x_ref[...]
            r = lax.rsqrt(jnp.mean(xv * xv, axis=-1, keepdims=True) + RMS_EPS)
            hv = (xv * r * g_ref[...]).astype(BF16)
            hs[...] = hv
            h_ref[...] = hv
        u_ref[...] = _dot(hs[...], w_ref[...])

    return _pcall(
        body, name=name, grid=(s // tm, N_DEV),
        in_specs=[pl.BlockSpec((tm, D_MODEL), lambda i, j: (i, 0)),
                  pl.BlockSpec((1, D_MODEL), lambda i, j: (0, 0)),
                  pl.BlockSpec((None, D_MODEL, COLS_PER_DEV), lambda i, j: (j, 0, 0))],
        out_specs=(pl.BlockSpec((tm, COLS_PER_DEV), lambda i, j: (i, j)),
                   pl.BlockSpec((tm, D_MODEL), lambda i, j: (i, 0))),
        out_shape=(jax.ShapeDtypeStruct((s, N_IN), F32), jax.ShapeDtypeStruct((s, D_MODEL), BF16)),
        scratch_shapes=[pltpu.VMEM((tm, D_MODEL), BF16)],
        compiler_params=_params(("parallel", "arbitrary")),
    )(x, g, w_all)


def _pool_window(vs, t0, t, grp):
    ext = vs[pl.ds(t0, t + POOL_HALO), :]
    s2 = ext + pltpu.roll(ext, 1, 0)
    s4 = s2 + pltpu.roll(s2, 2, 0)
    s8 = s4 + pltpu.roll(s4, 4, 0)
    s16 = s8 + pltpu.roll(s8, 8, 0)
    sel = jnp.where(grp == 0, s2, jnp.where(grp == 1, s4, jnp.where(grp == 2, s8, s16)))
    return sel[POOL_HALO:, :], ext[POOL_HALO:, :]


def _pool_count(t0, t, grp):
    pos = t0 + lax.broadcasted_iota(jnp.int32, (t, 1), 0)
    return jnp.minimum(pos + 1, jnp.left_shift(2, grp)).astype(F32)


def _pool_fwd(u, pool_w, pool_scale, name):
    s = u.shape[0]
    t = min(256, s)

    def body(pv_ref, pg_ref, w_ref, sc_ref, y_ref, vs):
        grp = pl.program_id(0)
        vs[0:POOL_HALO, :] = jnp.zeros((POOL_HALO, LANES), F32)
        vs[POOL_HALO:, :] = pv_ref[...]
        wb = w_ref[...].astype(BF16)
        scale = sc_ref[...]

        def tile(i, carry):
            t0 = pl.multiple_of(i * t, t)
            win, v = _pool_window(vs, t0, t, grp)
            pooled = win / _pool_count(t0, t, grp) - v
            mixed = _dot(pooled.astype(BF16), wb)
            gate = pg_ref[pl.ds(t0, t), :]
            y_ref[pl.ds(t0, t), :] = (mixed * scale * (gate * _sigmoid(gate))).astype(BF16)
            return carry

        lax.fori_loop(0, s // t, tile, 0)

    return _pcall(
        body, name=name, grid=(4,),
        in_specs=[pl.BlockSpec((s, LANES), lambda g: (0, CB_POOL_V + g)),
                  pl.BlockSpec((s, LANES), lambda g: (0, CB_POOL_G + g)),
                  pl.BlockSpec((None, LANES, LANES), lambda g: (g, 0, 0)),
                  pl.BlockSpec((1, LANES), lambda g: (0, g))],
        out_specs=pl.BlockSpec((s, LANES), lambda g: (0, g)),
        out_shape=jax.ShapeDtypeStruct((s, WIDTH), BF16),
        scratch_shapes=[pltpu.VMEM((POOL_HALO + s, LANES), F32)],
        compiler_params=_params(("arbitrary",)),
    )(u, u, pool_w, pool_scale)


def _conv_taps(zs, t0, t):
    ext = zs[pl.ds(t0, t + CONV_HALO), :]
    z0 = ext[CONV_HALO:, :]
    z1 = pltpu.roll(ext, 1, 0)[CONV_HALO:, :]
    z2 = pltpu.roll(ext, 2, 0)[CONV_HALO:, :]
    return z0, z1, z2


def _conv_fwd(u, conv_w, conv_b, name):
    s = u.shape[0]
    t = min(256, s)

    def body(xc_ref, gb_ref, gc_ref, cg_ref, w_ref, b_ref, y_ref, zs):
        zs[0:CONV_HALO, :] = jnp.zeros((CONV_HALO, LANES), F32)
        zs[CONV_HALO:, :] = gc_ref[...] * xc_ref[...]
        w0, w1, w2 = w_ref[0:1, :], w_ref[1:2, :], w_ref[2:3, :]
        bias = b_ref[...]

        def tile(i, carry):
            t0 = pl.multiple_of(i * t, t)
            z0, z1, z2 = _conv_taps(zs, t0, t)
            conv = w0 * z2 + w1 * z1 + w2 * z0
            gate = cg_ref[pl.ds(t0, t), :]
            y = gb_ref[pl.ds(t0, t), :] * (conv + bias) * (gate * _sigmoid(gate))
            y_ref[pl.ds(t0, t), :] = y.astype(BF16)
            return carry

        lax.fori_loop(0, s // t, tile, 0)

    col = lambda base: pl.BlockSpec((s, LANES), lambda j: (0, base + j))
    return _pcall(
        body, name=name, grid=(4,),
        in_specs=[col(CB_CONV_X), col(CB_CONV_GB), col(CB_CONV_GC), col(CB_CONV_G),
                  pl.BlockSpec((3, LANES), lambda j: (0, j)),
                  pl.BlockSpec((1, LANES), lambda j: (0, j))],
        out_specs=pl.BlockSpec((s, LANES), lambda j: (0, j)),
        out_shape=jax.ShapeDtypeStruct((s, WIDTH), BF16),
        scratch_shapes=[pltpu.VMEM((CONV_HALO + s, LANES), F32)],
        compiler_params=_params(("arbitrary",)),
    )(u, u, u, u, conv_w, conv_b)


def _first_head_lanes(rows, width=LANES):
    lane = lax.broadcasted_iota(jnp.int32, (rows, width), 1)
    return jnp.bitwise_and(lane, LANES - 1) < HEAD_DIM


def _stack_heads(x, first):
    zero = jnp.zeros_like(x)
    return jnp.concatenate([jnp.where(first, x, zero), jnp.where(first, zero, x)], axis=0).astype(BF16)


def _causal_mask(tq, tk, copies):
    row = lax.broadcasted_iota(jnp.int32, (tq, tk), 0)
    col = lax.broadcasted_iota(jnp.int32, (tq, tk), 1)
    return jnp.concatenate([col < row] * copies, axis=0)


def _suffix_matrix(tk, inclusive, parts):
    r = lax.broadcasted_iota(jnp.int32, (parts * tk, 2 * tk), 0)
    c = lax.broadcasted_iota(jnp.int32, (parts * tk, 2 * tk), 1)
    r = jnp.bitwise_and(r, tk - 1)
    tri = (r >= c) if inclusive else (r > c)
    return jnp.where(c >= tk, 1.0, jnp.where(tri, 1.0, 0.0)).astype(BF16)


def _suffix_sums(x, m):
    hi, lo = _split_bf16(x)
    return _dot(jnp.concatenate([hi, lo], axis=1), m)


def _sb_log_terms(z, mask, m_strict):
    ls = jnp.minimum(z, 0.0) - jnp.log(1.0 + jnp.exp2(jnp.abs(z) * -LOG2E))
    lk = ls - z
    if mask is not None:
        lk = jnp.where(mask, lk, 0.0)
    return ls, _dot(lk.astype(BF16), m_strict)


SB_PAIRS = 4


def _pair_lanes(a):
    return slice(a * LANES, (a + 1) * LANES)


def _sb_fwd(u, name, xchg=None):
    s = u.shape[0]
    tq = tk = min(128, s)
    pairs = SB_PAIRS
    width = pairs * LANES
    rows = 2 * pairs * tq
    x_arrs, x_gather = xchg if xchg else ((), True)
    n_x = len(x_arrs)
    grid = (4 // pairs, s // tq)

    def body(*refs):
        q_ref, k_ref, v_ref, g_ref = refs[:4]
        x_in, refs = refs[4:4 + n_x], refs[4 + n_x:]
        o_ref, y_ref = refs[:2]
        x_out, refs = refs[2:2 + n_x], refs[2 + n_x:]
        kbf, vst, z_s, ell_s, carry_s = refs[:5]
        x_sems = refs[5:]
        i = pl.program_id(1)
        if n_x:
            @pl.when((pl.program_id(0) == 0) & (i == 0))
            def _():
                _exchange_start(x_in, x_out, x_sems, x_gather)

        @pl.when(i == 0)
        def _():
            kbf[...] = k_ref[...].astype(BF16)
            first_s = _first_head_lanes(s, width)
            vf = v_ref[...]
            vst[0] = jnp.where(first_s, vf, 0.0).astype(BF16)
            vst[1] = jnp.where(first_s, 0.0, vf).astype(BF16)

        first = _first_head_lanes(tq)
        mask = _causal_mask(tq, tk, 2 * pairs)
        m_strict = _suffix_matrix(tk, False, 1)
        qcat = jnp.concatenate([_stack_heads(q_ref[:, _pair_lanes(a)] * SB_SCALE, first) for a in range(pairs)],
                               axis=0)

        def scores(b):
            off = pl.multiple_of(jnp.maximum(b, 0) * tk, tk)
            z_s[...] = jnp.concatenate(
                [_dot_nt(qcat[a * 2 * tq:(a + 1) * 2 * tq], kbf[pl.ds(off, tk), _pair_lanes(a)])
                 for a in range(pairs)], axis=0)

        def log_weights(m):
            ls, cs = _sb_log_terms(z_s[...], m, m_strict)
            carry = carry_s[...]
            ell_s[...] = ls + cs[:, :tk] + carry
            carry_s[...] = carry + cs[:, tk:]

        def consume(b, accs, m):
            w = jnp.exp(ell_s[...])
            if m is not None:
                w = jnp.where(m, w, 0.0)
            wb = w.astype(BF16)
            off = pl.multiple_of(b * tk, tk)
            new = []
            for a in range(pairs):
                r0 = a * 2 * tq
                wcat = jnp.concatenate([wb[r0:r0 + tq], wb[r0 + tq:r0 + 2 * tq]], axis=1)
                vcat = jnp.concatenate([vst[0, pl.ds(off, tk), _pair_lanes(a)], vst[1, pl.ds(off, tk), _pair_lanes(a)]],
                                       axis=0)
                new.append(accs[a] + _dot(wcat, vcat))
            return tuple(new)

        carry_s[...] = jnp.zeros((rows, tk), F32)
        scores(i)
        log_weights(mask)
        scores(i - 1)
        accs = consume(i, tuple(jnp.zeros((tq, LANES), F32) for _ in range(pairs)), mask)
        log_weights(None)
        scores(i - 2)

        def step(n, accs):
            accs = consume(i - n, accs, None)
            log_weights(None)
            scores(i - n - 2)
            return accs

        accs = lax.fori_loop(1, i + 1, step, accs)
        o = jnp.concatenate(accs, axis=1)
        o_ref[...] = o
        gate = g_ref[...]
        y_ref[...] = (o * (gate * _sigmoid(gate))).astype(BF16)
        if n_x:
            @pl.when((pl.program_id(0) == grid[0] - 1) & (i == grid[1] - 1))
            def _():
                _exchange_wait(x_in, x_out, x_sems, x_gather)

    base = lambda cb: cb // pairs
    qblk = lambda cb: pl.BlockSpec((tq, width), lambda p, i: (i, base(cb) + p))
    full = lambda cb: pl.BlockSpec((s, width), lambda p, i: (0, base(cb) + p), pipeline_mode=pl.Buffered(1))
    state = pltpu.VMEM((rows, tk), F32)
    return _pcall(
        body, name=name, grid=grid,
        in_specs=[qblk(CB_SB_Q), full(CB_SB_K), full(CB_SB_V), qblk(CB_SB_G)] + [ANY_SPEC] * n_x,
        out_specs=(qblk(0), qblk(0)) + (ANY_SPEC,) * n_x,
        out_shape=(jax.ShapeDtypeStruct((s, WIDTH), F32), jax.ShapeDtypeStruct((s, WIDTH), BF16))
        + tuple(_exchange_out_shapes(x_arrs, x_gather)),
        scratch_shapes=[pltpu.VMEM((s, width), BF16), pltpu.VMEM((2, s, width), BF16), state, state, state]
        + (_exchange_sems(n_x) if n_x else []),
        compiler_params=_params(("arbitrary", "arbitrary")),
    )(u, u, u, u, *x_arrs)


def _merge_out_fwd(y_pool, y_conv, y_sb, u, wb_all, wo_all, x, g_post, layer, name, target=None):
    s = x.shape[0]
    tm = min(512, s)
    n_tiles = s // tm
    with_loss = target is not None

    def body(yp, yc, ys, m0, m1, m2, wb_ref, wo_ref, x_ref, g_ref, *rest):
        merged = jnp.zeros((tm, D_MODEL), F32)
        for n, (y_ref, m_ref) in enumerate(((yp, m0), (yc, m1), (ys, m2))):
            merged = merged + _sigmoid(m_ref[...]) * _dot(y_ref[...], wb_ref[n])
        mb = merged.astype(BF16)
        pre = _dot(mb, wo_ref[...].reshape(D_MODEL, D_MODEL))
        r = lax.rsqrt(jnp.mean(pre * pre, axis=-1, keepdims=True) + RMS_EPS)
        y = x_ref[...] + pre * r * g_ref[...]
        if not with_loss:
            out_ref, merged_ref, pre_ref = rest
            out_ref[...] = y
        else:
            t_ref, out_ref, merged_ref, pre_ref, loss_ref, acc = rest
            i = pl.program_id(0)

            @pl.when(i == 0)
            def _():
                acc[...] = jnp.zeros_like(acc)
            err = y - t_ref[...]
            out_ref[...] = err / D_MODEL
            acc[...] += jnp.sum(err * err, axis=0, keepdims=True)

            @pl.when(i == n_tiles - 1)
            def _():
                total = jnp.sum(acc[...], axis=1, keepdims=True) * (0.5 / D_MODEL)
                loss_ref[...] = jnp.broadcast_to(total, (1, LANES))
        merged_ref[...] = mb
        pre_ref[...] = pre

    rows = lambda w: pl.BlockSpec((tm, w), lambda i: (i, 0))
    merge = lambda n: pl.BlockSpec((tm, D_MODEL), lambda i: (i, MERGE_BLOCK_1024 + n))
    out_specs = (rows(D_MODEL), rows(D_MODEL), rows(D_MODEL))
    out_shape = (jax.ShapeDtypeStruct((s, D_MODEL), F32), jax.ShapeDtypeStruct((s, D_MODEL), BF16),
                 jax.ShapeDtypeStruct((s, D_MODEL), F32))
    if with_loss:
        out_specs += (pl.BlockSpec((1, LANES), lambda i: (0, 0)),)
        out_shape += (jax.ShapeDtypeStruct((1, LANES), F32),)
    return _pcall(
        body, name=name, grid=(n_tiles,),
        in_specs=[rows(WIDTH), rows(WIDTH), rows(WIDTH), merge(0), merge(1), merge(2),
                  pl.BlockSpec((None, 3, WIDTH, D_MODEL), lambda i: (layer, 0, 0, 0)),
                  pl.BlockSpec((N_DEV, None, D_MODEL // N_DEV, D_MODEL), lambda i: (0, layer, 0, 0)),
                  rows(D_MODEL), pl.BlockSpec((1, D_MODEL), lambda i: (0, 0))] + ([rows(D_MODEL)] if with_loss else []),
        out_specs=out_specs, out_shape=out_shape,
        scratch_shapes=[pltpu.VMEM((1, D_MODEL), F32)] if with_loss else [],
        compiler_params=_params(("arbitrary",)),
    )(y_pool, y_conv, y_sb, u, u, u, wb_all, wo_all, x, g_post, *([target] if with_loss else []))


def _out_proj_bwd(dy, pre, g_post, merged, wo_all, layer, name):
    s = dy.shape[0]
    tm = min(512, s)
    n_tiles = s // tm

    def body(dy_ref, pre_ref, g_ref, mg_ref, wo_ref, dm_ref, dwo_ref, dg_ref, acc):
        i = pl.program_id(0)

        @pl.when(i == 0)
        def _():
            acc[...] = jnp.zeros_like(acc)
            dg_ref[...] = jnp.zeros_like(dg_ref)
        dyv, pre_v = dy_ref[...], pre_ref[...]
        r = lax.rsqrt(jnp.mean(pre_v * pre_v, axis=-1, keepdims=True) + RMS_EPS)
        dg_ref[...] += jnp.sum(dyv * pre_v * r, axis=0, keepdims=True)
        a = dyv * g_ref[...]
        dpre = r * a - pre_v * (r * r * r) * jnp.mean(a * pre_v, axis=-1, keepdims=True)
        db = dpre.astype(BF16)
        acc[...] += _dot_tn(mg_ref[...], db)
        dm_ref[...] = _dot_nt(db, wo_ref[...].reshape(D_MODEL, D_MODEL))

        @pl.when(i == n_tiles - 1)
        def _():
            dwo_ref[...] = acc[...].astype(BF16)

    rows = lambda: pl.BlockSpec((tm, D_MODEL), lambda i: (i, 0))
    return _pcall(
        body, name=name, grid=(n_tiles,),
        in_specs=[rows(), rows(), pl.BlockSpec((1, D_MODEL), lambda i: (0, 0)), rows(),
                  pl.BlockSpec((N_DEV, None, D_MODEL // N_DEV, D_MODEL), lambda i: (0, layer, 0, 0))],
        out_specs=(rows(), pl.BlockSpec((D_MODEL, D_MODEL), lambda i: (0, 0)),
                   pl.BlockSpec((1, D_MODEL), lambda i: (0, 0))),
        out_shape=(jax.ShapeDtypeStruct((s, D_MODEL), F32), jax.ShapeDtypeStruct((D_MODEL, D_MODEL), BF16),
                   jax.ShapeDtypeStruct((1, D_MODEL), F32)),
        scratch_shapes=[pltpu.VMEM((D_MODEL, D_MODEL), F32)],
        compiler_params=_params(("arbitrary",)),
    )(dy, pre, g_post, merged, wo_all)


def _merge_bwd(dmerged, y_pool, y_conv, y_sb, u, wb_all, layer, name):
    s = dmerged.shape[0]
    tm = min(512, s)
    n_tiles = s // tm
    cols = D_MODEL // N_DEV

    def body(dm_ref, yp, yc, ys, m0, m1, m2, wb_ref, du_ref, dyp, dyc, dys, dwb_ref, acc):
        i = pl.program_id(0)

        @pl.when(i == 0)
        def _():
            acc[...] = jnp.zeros_like(acc)
        dm = dm_ref[...]
        for n, (y_ref, m_ref, dy_ref) in enumerate(((yp, m0, dyp), (yc, m1, dyc), (ys, m2, dys))):
            yv = y_ref[...]
            wb = wb_ref[n]
            gate = _sigmoid(m_ref[...])
            proj = _dot(yv, wb)
            dgate = (dm * proj * gate * (1.0 - gate)).astype(BF16)
            du_ref[2 * n] = dgate[:, :WIDTH]
            du_ref[2 * n + 1] = dgate[:, WIDTH:]
            dproj = (dm * gate).astype(BF16)
            acc[n] += _dot_tn(yv, dproj)
            dy_ref[...] = _dot_nt(dproj, wb)

        @pl.when(i == n_tiles - 1)
        def _():
            for j in range(N_DEV):
                for n in range(3):
                    dwb_ref[j, n] = acc[n, :, j * cols:(j + 1) * cols].astype(BF16)

    rows = lambda w: pl.BlockSpec((tm, w), lambda i: (i, 0))
    merge = lambda n: pl.BlockSpec((tm, D_MODEL), lambda i: (i, MERGE_BLOCK_1024 + n))
    return _pcall(
        body, name=name, grid=(n_tiles,),
        in_specs=[rows(D_MODEL), rows(WIDTH), rows(WIDTH), rows(WIDTH), merge(0), merge(1), merge(2),
                  pl.BlockSpec((None, 3, WIDTH, D_MODEL), lambda i: (layer, 0, 0, 0))],
        out_specs=(pl.BlockSpec((DU_MERGE[1], tm, WIDTH), lambda i: (DU_MERGE[0] // DU_MERGE[1], i, 0)),
                   rows(WIDTH), rows(WIDTH), rows(WIDTH),
                   pl.BlockSpec((N_DEV, 3, WIDTH, cols), lambda i: (0, 0, 0, 0))),
        out_shape=(jax.ShapeDtypeStruct((DU_PIECES, s, WIDTH), BF16),
                   jax.ShapeDtypeStruct((s, WIDTH), F32), jax.ShapeDtypeStruct((s, WIDTH), F32),
                   jax.ShapeDtypeStruct((s, WIDTH), F32),
                   jax.ShapeDtypeStruct((N_DEV, 3, WIDTH, cols), BF16)),
        scratch_shapes=[pltpu.VMEM((3, WIDTH, D_MODEL), F32)],
        compiler_params=_params(("arbitrary",)),
    )(dmerged, y_pool, y_conv, y_sb, u, u, u, wb_all)


def _sb_bwd(u, o, dys, du, name, xchg=None):
    s = u.shape[0]
    tq = tk = min(128, s)
    pairs = SB_PAIRS
    width = pairs * LANES
    assert width == WIDTH
    rows = 2 * pairs * tq
    pair_rows = lambda a: slice(a * 2 * tq, (a + 1) * 2 * tq)

    x_arrs, x_gather = xchg if xchg else ((), True)
    n_x = len(x_arrs)
    grid = (4 // pairs, s // tq)

    def body(*refs):
        q_ref, k_ref, v_ref, g_ref, o_ref, dys_ref = refs[:6]
        x_in, refs = refs[7:7 + n_x], refs[7 + n_x:]
        du_ref, dk_ref, dv_ref = refs[:3]
        dq_ref, dg_ref = du_ref.at[0], du_ref.at[1]
        x_out, refs = refs[3:3 + n_x], refs[3 + n_x:]
        kbf, vbf, kst, z_s, ell_s, ls_s, cl_s, wb_s, g_s, bef_s, cg_s, beta_s = refs[:12]
        x_sems = refs[12:]
        i = pl.program_id(1)
        if n_x:
            @pl.when((pl.program_id(0) == 0) & (i == 0))
            def _():
                _exchange_start(x_in, x_out, x_sems, x_gather)

        @pl.when(i == 0)
        def _():
            dk_ref[...] = jnp.zeros_like(dk_ref)
            dv_ref[...] = jnp.zeros_like(dv_ref)
            kf = k_ref[...]
            kbf[...] = kf.astype(BF16)
            vbf[...] = v_ref[...].astype(BF16)
            first_s = _first_head_lanes(s, width)
            kst[0] = jnp.where(first_s, kf, 0.0).astype(BF16)
            kst[1] = jnp.where(first_s, 0.0, kf).astype(BF16)

        first = _first_head_lanes(tq)
        mask = _causal_mask(tq, tk, 2 * pairs)
        m_strict = _suffix_matrix(tk, False, 1)
        m_incl = _suffix_matrix(tk, True, 2)

        gate = g_ref[...]
        sg = _sigmoid(gate)
        dy = dys_ref[...]
        ov = o_ref[...]
        dg_ref[...] = (dy * ov * (sg * (1.0 + gate * (1.0 - sg)))).astype(BF16)
        do = (dy * (gate * sg)).astype(BF16)
        prod = do.astype(F32) * ov
        row_sum = lambda v: jnp.broadcast_to(jnp.sum(v, axis=1, keepdims=True), (tq, tk))
        dsum, docat, qcat = [], [], []
        for a in range(pairs):
            pa = prod[:, _pair_lanes(a)]
            dsum += [row_sum(jnp.where(first, pa, 0.0)), row_sum(jnp.where(first, 0.0, pa))]
            docat.append(_stack_heads(do[:, _pair_lanes(a)], first))
            qcat.append(_stack_heads(q_ref[:, _pair_lanes(a)] * SB_SCALE, first))
        dsum = jnp.concatenate(dsum, axis=0)

        def block_start(b):
            return pl.multiple_of(jnp.maximum(b, 0) * tk, tk)

        def scores(b):
            off = block_start(b)
            z_s[...] = jnp.concatenate([_dot_nt(qcat[a], kbf[pl.ds(off, tk), _pair_lanes(a)]) for a in range(pairs)],
                                       axis=0)

        def log_weights(m):
            ls, cs = _sb_log_terms(z_s[...], m, m_strict)
            cl = cl_s[...]
            ell_s[...] = ls + cs[:, :tk] + cl
            cl_s[...] = cl + cs[:, tk:]
            ls_s[...] = ls

        def weights(b, m):
            off = block_start(b)
            dwt = jnp.concatenate([_dot_nt(docat[a], vbf[pl.ds(off, tk), _pair_lanes(a)]) for a in range(pairs)],
                                  axis=0)
            w = jnp.exp(ell_s[...])
            if m is not None:
                w = jnp.where(m, w, 0.0)
            wb = w.astype(BF16)
            g = dwt * wb.astype(F32)
            gs = _suffix_sums(g, m_incl)
            cg = cg_s[...]
            beta = jnp.exp(ls_s[...])
            wb_s[...] = wb
            beta_s[...] = beta
            g_s[...] = g * (1.0 - beta)
            bef_s[...] = gs[:, :tk] + cg
            cg_s[...] = cg + gs[:, tk:]

        def grads(b, dqs, m):
            dz = g_s[...] - beta_s[...] * (dsum - bef_s[...])
            if m is not None:
                dz = jnp.where(m, dz, 0.0)
            dzb = dz.astype(BF16)
            wb = wb_s[...]
            off = pl.multiple_of(b * tk, tk)
            new = []
            for a in range(pairs):
                r0 = a * 2 * tq
                kcat = jnp.concatenate([kst[0, pl.ds(off, tk), _pair_lanes(a)], kst[1, pl.ds(off, tk), _pair_lanes(a)]],
                                       axis=0)
                new.append(dqs[a] + _dot(jnp.concatenate([dzb[r0:r0 + tq], dzb[r0 + tq:r0 + 2 * tq]], axis=1), kcat))
                dk_ref[pl.ds(off, tk), _pair_lanes(a)] += _dot_tn(dzb[pair_rows(a)], qcat[a])
                dv_ref[pl.ds(off, tk), _pair_lanes(a)] += _dot_tn(wb[pair_rows(a)], docat[a])
            return tuple(new)

        zero = jnp.zeros((rows, tk), F32)
        cl_s[...] = zero
        cg_s[...] = zero
        scores(i)
        log_weights(mask)
        scores(i - 1)
        weights(i, mask)
        log_weights(None)
        scores(i - 2)
        dqs = grads(i, tuple(jnp.zeros((tq, LANES), F32) for _ in range(pairs)), mask)
        weights(i - 1, None)
        log_weights(None)
        scores(i - 3)

        def step(n, dqs):
            dqs = grads(i - n, dqs, None)
            weights(i - n - 1, None)
            log_weights(None)
            scores(i - n - 3)
            return dqs

        dqs = lax.fori_loop(1, i + 1, step, dqs)
        dq_ref[...] = (jnp.concatenate(dqs, axis=1) * SB_SCALE).astype(BF16)
        if n_x:
            @pl.when((pl.program_id(0) == grid[0] - 1) & (i == grid[1] - 1))
            def _():
                _exchange_wait(x_in, x_out, x_sems, x_gather)

    base = lambda cb: cb // pairs
    qblk = lambda cb: pl.BlockSpec((tq, width), lambda p, i: (i, base(cb) + p))
    full = lambda cb: pl.BlockSpec((s, width), lambda p, i: (0, base(cb) + p), pipeline_mode=pl.Buffered(1))
    state = pltpu.VMEM((rows, tk), F32)
    return _pcall(
        body, name=name, grid=grid,
        in_specs=[qblk(CB_SB_Q), full(CB_SB_K), full(CB_SB_V), qblk(CB_SB_G), qblk(0), qblk(0), ANY_SPEC]
        + [ANY_SPEC] * n_x,
        out_specs=(pl.BlockSpec((DU_SB_QG[1], tq, WIDTH), lambda p, i: (DU_SB_QG[0] // DU_SB_QG[1], i, 0)),
                   full(0), full(0)) + (ANY_SPEC,) * n_x,
        out_shape=(jax.ShapeDtypeStruct(du.shape, du.dtype), jax.ShapeDtypeStruct((s, WIDTH), F32),
                   jax.ShapeDtypeStruct((s, WIDTH), F32)) + tuple(_exchange_out_shapes(x_arrs, x_gather)),
        input_output_aliases={6: 0},
        scratch_shapes=[pltpu.VMEM((s, width), BF16), pltpu.VMEM((s, width), BF16), pltpu.VMEM((2, s, width), BF16),
                        state, state, state, state, pltpu.VMEM((rows, tk), BF16),
                        state, state, state, state] + (_exchange_sems(n_x) if n_x else []),
        compiler_params=_params(("arbitrary", "arbitrary")),
    )(u, u, u, u, o, dys, du, *x_arrs)


def _conv_bwd(u, conv_w, conv_b, dyc, du, name):
    s = u.shape[0]
    t = min(256, s)
    n_tiles = s // t

    def body(xc_ref, gb_ref, gc_ref, cg_ref, w_ref, b_ref, dy_ref, du_in, du_ref, dw_ref, db_ref, zs, ds):
        dxc_ref, dgb_ref, dgc_ref, dcg_ref = (du_ref.at[p] for p in range(4))
        zs[0:CONV_HALO, :] = jnp.zeros((CONV_HALO, LANES), F32)
        zs[CONV_HALO:, :] = gc_ref[...] * xc_ref[...]
        ds[s:, :] = jnp.zeros((CONV_HALO, LANES), F32)
        w0, w1, w2 = w_ref[0:1, :], w_ref[1:2, :], w_ref[2:3, :]
        bias = b_ref[...]

        def first(i, sums):
            t0 = pl.multiple_of(i * t, t)
            z0, z1, z2 = _conv_taps(zs, t0, t)
            pre = w0 * z2 + w1 * z1 + w2 * z0 + bias
            gate = cg_ref[pl.ds(t0, t), :]
            sg = _sigmoid(gate)
            gb = gb_ref[pl.ds(t0, t), :]
            dy = dy_ref[pl.ds(t0, t), :]
            dcg_ref[pl.ds(t0, t), :] = (dy * gb * pre * (sg * (1.0 + gate * (1.0 - sg)))).astype(BF16)
            dgb_ref[pl.ds(t0, t), :] = (dy * pre * (gate * sg)).astype(BF16)
            dc = dy * gb * (gate * sg)
            ds[pl.ds(t0, t), :] = dc
            red = lambda v: jnp.sum(v, axis=0, keepdims=True)
            return (sums[0] + red(dc * z2), sums[1] + red(dc * z1), sums[2] + red(dc * z0), sums[3] + red(dc))

        zrow = jnp.zeros((1, LANES), F32)
        sw0, sw1, sw2, sb = lax.fori_loop(0, n_tiles, first, (zrow, zrow, zrow, zrow))
        dw_ref[0:1, :] = sw0
        dw_ref[1:2, :] = sw1
        dw_ref[2:3, :] = sw2
        db_ref[...] = sb

        def second(i, carry):
            t0 = pl.multiple_of(i * t, t)
            ext = ds[pl.ds(t0, t + CONV_HALO), :]
            n = t + CONV_HALO
            d0 = ext[:t, :]
            d1 = pltpu.roll(ext, n - 1, 0)[:t, :]
            d2 = pltpu.roll(ext, n - 2, 0)[:t, :]
            dz = w2 * d0 + w1 * d1 + w0 * d2
            dgc_ref[pl.ds(t0, t), :] = (dz * xc_ref[pl.ds(t0, t), :]).astype(BF16)
            dxc_ref[pl.ds(t0, t), :] = (dz * gc_ref[pl.ds(t0, t), :]).astype(BF16)
            return carry

        lax.fori_loop(0, n_tiles, second, 0)

    col = lambda base: pl.BlockSpec((s, LANES), lambda j: (0, base + j))
    first, count = DU_CONV
    return _pcall(
        body, name=name, grid=(4,),
        in_specs=[col(CB_CONV_X), col(CB_CONV_GB), col(CB_CONV_GC), col(CB_CONV_G),
                  pl.BlockSpec((3, LANES), lambda j: (0, j)), pl.BlockSpec((1, LANES), lambda j: (0, j)), col(0),
                  ANY_SPEC],
        out_specs=(pl.BlockSpec((count, s, LANES), lambda j: (first // count, 0, j)),
                   pl.BlockSpec((3, LANES), lambda j: (0, j)), pl.BlockSpec((1, LANES), lambda j: (0, j))),
        out_shape=(jax.ShapeDtypeStruct(du.shape, du.dtype),
                   jax.ShapeDtypeStruct((3, WIDTH), F32), jax.ShapeDtypeStruct((1, WIDTH), F32)),
        scratch_shapes=[pltpu.VMEM((CONV_HALO + s, LANES), F32), pltpu.VMEM((s + CONV_HALO, LANES), F32)],
        input_output_aliases={7: 0},
        compiler_params=_params(("arbitrary",)),
    )(u, u, u, u, conv_w, conv_b, dyc, du)


def _pool_bwd(u, pool_w, pool_scale, dyp, du, name):
    s = u.shape[0]
    t = min(256, s)
    n_tiles = s // t

    def body(pv_ref, pg_ref, w_ref, sc_ref, dy_ref, du_in, du_ref, dw_ref, dsc_ref, vs, es, dps):
        dpv_ref, dpg_ref = du_ref.at[0], du_ref.at[1]
        grp = pl.program_id(0)
        vs[0:POOL_HALO, :] = jnp.zeros((POOL_HALO, LANES), F32)
        vs[POOL_HALO:, :] = pv_ref[...]
        es[s:, :] = jnp.zeros((POOL_HALO, LANES), F32)
        wb = w_ref[...].astype(BF16)
        scale = sc_ref[...]

        def first(i, sums):
            dw, dsc = sums
            t0 = pl.multiple_of(i * t, t)
            win, v = _pool_window(vs, t0, t, grp)
            cnt = _pool_count(t0, t, grp)
            pb = (win / cnt - v).astype(BF16)
            mixed = _dot(pb, wb)
            gate = pg_ref[pl.ds(t0, t), :]
            sg = _sigmoid(gate)
            dy = dy_ref[pl.ds(t0, t), :]
            dpg_ref[pl.ds(t0, t), :] = (dy * (mixed * scale) * (sg * (1.0 + gate * (1.0 - sg)))).astype(BF16)
            dms = dy * (gate * sg)
            dsc = dsc + jnp.sum(dms * mixed, axis=0, keepdims=True)
            dmb = (dms * scale).astype(BF16)
            dw = dw + _dot_tn(pb, dmb)
            dpooled = _dot_nt(dmb, wb)
            dps[pl.ds(t0, t), :] = dpooled
            es[pl.ds(t0, t), :] = dpooled / cnt
            return dw, dsc

        dw, dsc = lax.fori_loop(0, n_tiles, first, (jnp.zeros((LANES, LANES), F32), jnp.zeros((1, LANES), F32)))
        dw_ref[...] = dw
        dsc_ref[...] = dsc

        def second(i, carry):
            t0 = pl.multiple_of(i * t, t)
            ext = es[pl.ds(t0, t + POOL_HALO), :]
            n = t + POOL_HALO
            f2 = ext + pltpu.roll(ext, n - 1, 0)
            f4 = f2 + pltpu.roll(f2, n - 2, 0)
            f8 = f4 + pltpu.roll(f4, n - 4, 0)
            f16 = f8 + pltpu.roll(f8, n - 8, 0)
            sel = jnp.where(grp == 0, f2, jnp.where(grp == 1, f4, jnp.where(grp == 2, f8, f16)))
            dpv_ref[pl.ds(t0, t), :] = (sel[:t, :] - dps[pl.ds(t0, t), :]).astype(BF16)
            return carry

        lax.fori_loop(0, n_tiles, second, 0)

    col = lambda base: pl.BlockSpec((s, LANES), lambda g: (0, base + g))
    first, count = DU_POOL
    return _pcall(
        body, name=name, grid=(4,),
        in_specs=[col(CB_POOL_V), col(CB_POOL_G), pl.BlockSpec((None, LANES, LANES), lambda g: (g, 0, 0)),
                  pl.BlockSpec((1, LANES), lambda g: (0, g)), col(0), ANY_SPEC],
        out_specs=(pl.BlockSpec((count, s, LANES), lambda g: (first // count, 0, g)),
                   pl.BlockSpec((None, LANES, LANES), lambda g: (g, 0, 0)),
                   pl.BlockSpec((1, LANES), lambda g: (0, g))),
        out_shape=(jax.ShapeDtypeStruct(du.shape, du.dtype),
                   jax.ShapeDtypeStruct((4, LANES, LANES), F32), jax.ShapeDtypeStruct((1, WIDTH), F32)),
        scratch_shapes=[pltpu.VMEM((POOL_HALO + s, LANES), F32), pltpu.VMEM((s + POOL_HALO, LANES), F32),
                        pltpu.VMEM((s, LANES), F32)],
        input_output_aliases={5: 0},
        compiler_params=_params(("arbitrary",)),
    )(u, u, pool_w, pool_scale, dyp, du)


def _in_proj_bwd_x(du, w_all, x, g_pre, dy, name):
    s = x.shape[0]
    tm = min(1024, s)
    grid = (s // tm, N_DEV)

    def body(dua_ref, dub_ref, w_ref, x_ref, g_ref, dy_ref, dx_ref, dg_ref, acc):
        i, k = pl.program_id(0), pl.program_id(1)

        @pl.when(k == 0)
        def _():
            acc[...] = jnp.zeros_like(acc)

        @pl.when((k == 0) & (i == 0))
        def _():
            dg_ref[...] = jnp.zeros_like(dg_ref)
        acc[...] += _dot_nt(jnp.concatenate([dua_ref[...], dub_ref[...]], axis=1), w_ref[...])

        @pl.when(k == N_DEV - 1)
        def _():
            dh, xv = acc[...], x_ref[...]
            r = lax.rsqrt(jnp.mean(xv * xv, axis=-1, keepdims=True) + RMS_EPS)
            dg_ref[...] += jnp.sum(dh * xv * r, axis=0, keepdims=True)
            a = dh * g_ref[...]
            dx_ref[...] = dy_ref[...] + r * a - xv * (r * r * r) * jnp.mean(a * xv, axis=-1, keepdims=True)

    rows = lambda: pl.BlockSpec((tm, D_MODEL), lambda i, k: (i, 0))
    vec = lambda: pl.BlockSpec((1, D_MODEL), lambda i, k: (0, 0))
    piece = lambda half: pl.BlockSpec((None, tm, WIDTH), lambda i, k: (_du_pieces_of_block(k)[half], i, 0))
    return _pcall(
        body, name=name, grid=grid,
        in_specs=[piece(0), piece(1), pl.BlockSpec((None, D_MODEL, COLS_PER_DEV), lambda i, k: (k, 0, 0)),
                  rows(), vec(), rows()],
        out_specs=(rows(), vec()),
        out_shape=(jax.ShapeDtypeStruct((s, D_MODEL), F32), jax.ShapeDtypeStruct((1, D_MODEL), F32)),
        scratch_shapes=[pltpu.VMEM((tm, D_MODEL), F32)],
        compiler_params=_params(("arbitrary", "arbitrary")),
    )(du, du, w_all, x, g_pre, dy)


ROW_OFFSETS = (6, 7, 2, 4, 3, 5, 0, 1)


def _in_proj_bwd_send(h, du, w_all, x, g_pre, dy, name):
    s = x.shape[0]
    tk = s // N_DEV
    tm = min(1024, s)
    n_i = s // tm
    grid = (N_DEV + n_i, N_DEV)
    last = N_DEV - 1
    def offset(row):
        return functools.reduce(lambda acc, rn: jnp.where(row == rn[0], rn[1], acc), enumerate(ROW_OFFSETS), 0)

    def body(me_ref, h_ref, duwa_ref, duwb_ref, duxa_ref, duxb_ref, w_ref, x_ref, g_ref, dy_ref,
             dx_ref, dg_ref, recv_ref, part_ref, acc_w, stage, acc_x, send_sems, recv_sems, park_sems):
        r, k = pl.program_id(0), pl.program_id(1)
        x_, y_, c_ = lax.axis_index("x"), lax.axis_index("y"), lax.axis_index("c")
        me = 4 * x_ + 2 * y_ + c_
        flip = lambda v, bit: 1 - v if bit else v
        peer = lambda n: (flip(x_, (n >> 2) & 1), flip(y_, (n >> 1) & 1), flip(c_, n & 1))

        def park(row):
            n = ROW_OFFSETS[row]
            dst = recv_ref.at[me] if n == 0 else part_ref.at[n]
            return pltpu.make_async_copy(stage.at[row % 2], dst, park_sems.at[row % 2])

        def send(n, landing=False):
            px, py, pc = peer(n)
            dst = recv_ref.at[4 * px + 2 * py + pc] if landing else recv_ref.at[me]
            return pltpu.make_async_remote_copy(
                src_ref=part_ref.at[n], dst_ref=dst, send_sem=send_sems.at[n], recv_sem=recv_sems.at[n],
                device_id=(px, py, pc), device_id_type=pl.DeviceIdType.MESH)

        def parked(row):
            park(row).wait()
            if ROW_OFFSETS[row] >= 1:
                send(ROW_OFFSETS[row]).start()

        @pl.when(r < N_DEV)
        def _():
            @pl.when(k == 0)
            def _():
                acc_w[...] = jnp.zeros_like(acc_w)
            acc_w[...] += _dot_tn(h_ref[...], jnp.concatenate([duwa_ref[...], duwb_ref[...]], axis=1))

            for row in range(N_DEV):
                @pl.when((k == last) & (r == row))
                def _():
                    if row >= 1:
                        parked(row - 1)
                    stage[row % 2] = acc_w[...].astype(BF16)
                    park(row).start()

        @pl.when(r >= N_DEV)
        def _():
            @pl.when(k == 0)
            def _():
                acc_x[...] = jnp.zeros_like(acc_x)

            @pl.when((k == 0) & (r == N_DEV))
            def _():
                dg_ref[...] = jnp.zeros_like(dg_ref)
                parked(last)
            acc_x[...] += _dot_nt(jnp.concatenate([duxa_ref[...], duxb_ref[...]], axis=1), w_ref[...])

            @pl.when(k == last)
            def _():
                dh, xv = acc_x[...], x_ref[...]
                rs = lax.rsqrt(jnp.mean(xv * xv, axis=-1, keepdims=True) + RMS_EPS)
                dg_ref[...] += jnp.sum(dh * xv * rs, axis=0, keepdims=True)
                a = dh * g_ref[...]
                dx_ref[...] = dy_ref[...] + rs * a - xv * (rs * rs * rs) * jnp.mean(a * xv, axis=-1, keepdims=True)

        @pl.when((r == grid[0] - 1) & (k == last))
        def _():
            for n in range(1, N_DEV):
                send(n).wait_send()
            for n in range(1, N_DEV):
                send(n, landing=True).wait_recv()

    in_w = lambda r: r < N_DEV
    row_x = lambda r: jnp.maximum(r - N_DEV, 0)
    rows = lambda: pl.BlockSpec((tm, D_MODEL), lambda r, k, me: (row_x(r), 0))
    vec = lambda: pl.BlockSpec((1, D_MODEL), lambda r, k, me: (0, 0))
    block_w = lambda r, me: jnp.bitwise_xor(me[0], offset(jnp.minimum(r, last)))
    block_x = lambda r, k: jnp.where(in_w(r), 0, k)
    piece_w = lambda half: pl.BlockSpec(
        (None, tk, WIDTH), lambda r, k, me: (_du_pieces_of_block(block_w(r, me))[half], jnp.where(in_w(r), k, last), 0))
    piece_x = lambda half: pl.BlockSpec(
        (None, tm, WIDTH), lambda r, k, me: (_du_pieces_of_block(block_x(r, k))[half], row_x(r), 0))
    grid_spec = pltpu.PrefetchScalarGridSpec(
        num_scalar_prefetch=1, grid=grid,
        in_specs=[pl.BlockSpec((tk, D_MODEL), lambda r, k, me: (jnp.where(in_w(r), k, last), 0)),
                  piece_w(0), piece_w(1), piece_x(0), piece_x(1),
                  pl.BlockSpec((None, D_MODEL, COLS_PER_DEV), lambda r, k, me: (block_x(r, k), 0, 0)),
                  rows(), vec(), rows()],
        out_specs=(rows(), vec(), ANY_SPEC, ANY_SPEC),
        scratch_shapes=[pltpu.VMEM((D_MODEL, COLS_PER_DEV), F32), pltpu.VMEM((2, D_MODEL, COLS_PER_DEV), BF16),
                        pltpu.VMEM((tm, D_MODEL), F32), pltpu.SemaphoreType.DMA((N_DEV,)),
                        pltpu.SemaphoreType.DMA((N_DEV,)), pltpu.SemaphoreType.DMA((2,))])
    me = 4 * lax.axis_index("x") + 2 * lax.axis_index("y") + lax.axis_index("c")
    blocks = jax.ShapeDtypeStruct((N_DEV, D_MODEL, COLS_PER_DEV), BF16)
    dx, dg, received, _ = _pcall(
        body, name=name, grid_spec=grid_spec,
        out_shape=(jax.ShapeDtypeStruct((s, D_MODEL), F32), jax.ShapeDtypeStruct((1, D_MODEL), F32), blocks, blocks),
        compiler_params=_params(("arbitrary", "arbitrary")),
    )(jnp.reshape(me, (1,)).astype(jnp.int32), h, du, du, du, du, w_all, x, g_pre, dy)
    return dx, dg, received


def _in_proj_bwd_w(h, du, name):
    s = h.shape[0]
    tk = min(2048, s)
    n_k = s // tk

    def body(h_ref, dua_ref, dub_ref, out_ref, acc):
        k = pl.program_id(1)

        @pl.when(k == 0)
        def _():
            acc[...] = jnp.zeros_like(acc)
        acc[...] += _dot_tn(h_ref[...], jnp.concatenate([dua_ref[...], dub_ref[...]], axis=1))

        @pl.when(k == n_k - 1)
        def _():
            out_ref[...] = acc[...].astype(BF16)

    piece = lambda half: pl.BlockSpec((None, tk, WIDTH), lambda j, k: (_du_pieces_of_block(j)[half], k, 0))
    return _pcall(
        body, name=name, grid=(N_DEV, n_k),
        in_specs=[pl.BlockSpec((tk, D_MODEL), lambda j, k: (k, 0)), piece(0), piece(1)],
        out_specs=pl.BlockSpec((None, D_MODEL, COLS_PER_DEV), lambda j, k: (j, 0, 0)),
        out_shape=jax.ShapeDtypeStruct((N_DEV, D_MODEL, COLS_PER_DEV), BF16),
        scratch_shapes=[pltpu.VMEM((D_MODEL, COLS_PER_DEV), F32)],
        compiler_params=_params(("parallel", "arbitrary")),
    )(h, du, du)


def _adamw_math(g, w, m, v):
    m_new = ADAM_B1 * m + (1.0 - ADAM_B1) * g
    v_new = ADAM_B2 * v + (1.0 - ADAM_B2) * (g * g)
    m_hat = m_new / (1.0 - ADAM_B1 ** ADAM_STEP)
    v_hat = v_new / (1.0 - ADAM_B2 ** ADAM_STEP)
    delta = -ADAM_LR * (m_hat / (jnp.sqrt(v_hat) + ADAM_EPS) + ADAM_WD * w)
    return delta, m_new, v_new


def _sum_partials(p_ref):
    total = p_ref[0].astype(F32)
    for d in range(1, N_DEV):
        total = total + p_ref[d].astype(F32)
    return total


def _adamw_layers(parts0, parts1, w, m, v, name):
    _, r, c = w.shape
    tr = min(256, r)
    n_r = r // tr

    def body(p0_ref, p1_ref, w_ref, m_ref, v_ref, g_ref, d_ref, mo_ref, vo_ref):
        layer = pl.program_id(0)

        @pl.when(layer == 0)
        def _():
            g_ref[...] = _sum_partials(p0_ref)

        @pl.when(layer == 1)
        def _():
            g_ref[...] = _sum_partials(p1_ref)
        d_ref[...], mo_ref[...], vo_ref[...] = _adamw_math(g_ref[...], w_ref[...], m_ref[...], v_ref[...])

    part = lambda which: pl.BlockSpec((N_DEV, tr, c), lambda l, i: (0, jnp.where(l == which, i, 0), 0))
    par = lambda: pl.BlockSpec((None, tr, c), lambda l, i: (l, i, 0))
    out = jax.ShapeDtypeStruct(w.shape, F32)
    return _pcall(
        body, name=name, grid=(2, n_r),
        in_specs=[part(0), part(1), par(), par(), par()],
        out_specs=(par(), par(), par(), par()),
        out_shape=(out, out, out, out),
        compiler_params=_params(("arbitrary", "arbitrary")),
    )(parts0, parts1, w, m, v)


def _adamw_small(parts, w, m, v, name):
    def body(p_ref, w_ref, m_ref, v_ref, g_ref, d_ref, mo_ref, vo_ref):
        g = _sum_partials(p_ref)
        g_ref[...] = g
        d_ref[...], mo_ref[...], vo_ref[...] = _adamw_math(g, w_ref[...], m_ref[...], v_ref[...])

    out = jax.ShapeDtypeStruct(w.shape, F32)
    return _pcall(body, name=name, out_shape=(out, out, out, out), compiler_params=_params())(parts, w, m, v)


def _adamw_plain(g, w, m, v, name):
    def body(g_ref, w_ref, m_ref, v_ref, d_ref, mo_ref, vo_ref):
        d_ref[...], mo_ref[...], vo_ref[...] = _adamw_math(g_ref[...], w_ref[...], m_ref[...], v_ref[...])

    out = jax.ShapeDtypeStruct(w.shape, F32)
    return _pcall(body, name=name, out_shape=(out, out, out), compiler_params=_params())(g, w, m, v)


def _rows128(a):
    return a.reshape(-1, LANES)


SMALL_NAMES = ("pre_norm_g", "pool_w", "pool_scale", "conv_w", "conv_b", "post_norm_g")


def kernel(x, pre_norm_g, w_in, pool_w, pool_scale, conv_w, conv_b, w_branch, w_out, post_norm_g, loss_target, m_pre_norm_g, m_w_in, m_pool_w, m_pool_scale, m_conv_w, m_conv_b, m_w_branch, m_w_out, m_post_norm_g, v_pre_norm_g, v_w_in, v_pool_w, v_pool_scale, v_conv_w, v_conv_b, v_w_branch, v_w_out, v_post_norm_g):
    me = 4 * lax.axis_index("x") + 2 * lax.axis_index("y") + lax.axis_index("c")
    x0 = x[0]
    target = loss_target[0]
    conv_cols = conv_w.shape[-1]

    conv_w_pad = jnp.pad(conv_w.reshape(2 * 3, conv_cols), ((0, 2), (0, LANES - conv_cols)))
    w_in_all = [None, None]
    w_in_all[0], cw_g = _gather_two_level([w_in[0].astype(BF16), conv_w_pad], "gather_w_in_0")
    conv_w_full = cw_g[:, :6, :conv_cols].reshape(N_DEV, 2, 3, conv_cols).transpose(1, 2, 0, 3).reshape(2, 3, WIDTH)
    later_weights = ([w_in[1].astype(BF16), w_branch.astype(BF16), w_out.astype(BF16)], True)

    saved = []
    xin = x0
    for l in range(2):
        u, h = _in_proj_fwd(xin, pre_norm_g[l:l + 1], w_in_all[l], f"in_proj_fwd_{l}")
        y_pool = _pool_fwd(u, pool_w[l], pool_scale[l:l + 1], f"pool_fwd_{l}")
        y_conv = _conv_fwd(u, conv_w_full[l], conv_b[l:l + 1], f"conv_fwd_{l}")
        if l == 0:
            o_sb, y_sb, w_in_all[1], wb_g, wo_all = _sb_fwd(u, f"sb_fwd_{l}", later_weights)
            wb_all = wb_g.transpose(1, 2, 3, 0, 4).reshape(2, 3, WIDTH, D_MODEL)
        else:
            o_sb, y_sb = _sb_fwd(u, f"sb_fwd_{l}")
        if l == 0:
            xout, merged, pre = _merge_out_fwd(y_pool, y_conv, y_sb, u, wb_all, wo_all, xin, post_norm_g[l:l + 1], l,
                                               f"merge_out_fwd_{l}")
        else:
            dy, merged, pre, loss_row = _merge_out_fwd(y_pool, y_conv, y_sb, u, wb_all, wo_all, xin,
                                                       post_norm_g[l:l + 1], l, f"merge_out_fwd_{l}", target)
        saved.append((xin, u, h, y_pool, y_conv, y_sb, o_sb, merged, pre))
        xin = xout

    small = [None, None]
    recv = [None, None]
    ready = []
    for l in (1, 0):
        xl, u, h, y_pool, y_conv, y_sb, o_sb, merged, pre = saved[l]
        dmerged, dwo, dg_post = _out_proj_bwd(dy, pre, post_norm_g[l:l + 1], merged, wo_all, l, f"out_proj_bwd_{l}")
        du, dyp, dyc, dys, dwb = _merge_bwd(dmerged, y_pool, y_conv, y_sb, u, wb_all, l, f"merge_bwd_{l}")
        dwb = dwb.reshape(N_DEV, 3 * WIDTH, D_MODEL // N_DEV)
        dwo = dwo.reshape(N_DEV, D_MODEL // N_DEV, D_MODEL)
        du, dcw, dcb = _conv_bwd(u, conv_w_full[l], conv_b[l:l + 1], dyc, du, f"conv_bwd_{l}")
        du, dpw, dps = _pool_bwd(u, pool_w[l], pool_scale[l:l + 1], dyp, du, f"pool_bwd_{l}")
        small[l] = dict(pool_w=dpw, pool_scale=dps, conv_w=dcw, conv_b=dcb, post_norm_g=dg_post)
        if l == 1:
            du, dk, dv = _sb_bwd(u, o_sb, dys, du, f"sb_bwd_{l}")
        else:
            small[l]["pre_norm_g"] = jnp.zeros((1, D_MODEL), F32)
            packed = jnp.concatenate(
                [_rows128(jnp.stack([small[0][n], small[1][n]])) for n in SMALL_NAMES]
                + [jnp.pad(loss_row, ((0, 7), (0, 0)))], axis=0)
            du, dk, dv, *got, packed_all = _sb_bwd(
                u, o_sb, dys, du, f"sb_bwd_{l}", (ready + [dwb, dwo, packed], (False,) * 5 + (True,)))
            recv[1] = got[:3]
        du = lax.dynamic_update_slice(du, jnp.stack([dk, dv]).astype(BF16), (DU_SB_KV[0], 0, 0))
        if l == 1:
            dwi = _in_proj_bwd_w(h, du, f"in_proj_bwd_w_{l}")
            ready = [dwi, dwb, dwo]
            dx, dg_pre = _in_proj_bwd_x(du, w_in_all[l], xl, pre_norm_g[l:l + 1], dy, f"in_proj_bwd_x_{l}")
            small[l]["pre_norm_g"] = dg_pre
        else:
            dx, dg_pre, got_dwi = _in_proj_bwd_send(h, du, w_in_all[l], xl, pre_norm_g[l:l + 1], dy,
                                                    f"in_proj_bwd_{l}")
            recv[0] = [got_dwi] + got[3:]
        dy = dx
    grad_x = dy[None]

    (g_pre_0_all,) = _exchange([_rows128(dg_pre)], True, "gather_g_pre_0")
    packed_all = lax.dynamic_update_slice(packed_all, g_pre_0_all, (0, 0, 0))
    sizes = dict(pre_norm_g=16, pool_w=1024, pool_scale=8, conv_w=24, conv_b=8, post_norm_g=16)
    n_rows = sum(sizes.values())
    loss = jnp.sum(packed_all[:, n_rows, 0])

    given = dict(pre_norm_g=(pre_norm_g, m_pre_norm_g, v_pre_norm_g), pool_w=(pool_w, m_pool_w, v_pool_w),
                 pool_scale=(pool_scale, m_pool_scale, v_pool_scale), conv_b=(conv_b, m_conv_b, v_conv_b),
                 post_norm_g=(post_norm_g, m_post_norm_g, v_post_norm_g))
    zeros_cw = jnp.zeros((sizes["conv_w"], LANES), F32)
    pack3 = [jnp.concatenate([zeros_cw if n == "conv_w" else _rows128(given[n][k]) for n in SMALL_NAMES], axis=0)
             for k in range(3)]
    sg, sd, sm, sv = _adamw_small(packed_all[:, :n_rows], pack3[0], pack3[1], pack3[2], "adamw_small")

    def unpack(buf, name, shape):
        start = 0
        for n in SMALL_NAMES:
            if n == name:
                return buf[start:start + sizes[n]].reshape(shape)
            start += sizes[n]

    out = {}
    for n in ("pre_norm_g", "pool_w", "pool_scale", "conv_b", "post_norm_g"):
        shape = given[n][0].shape
        out[n] = tuple(unpack(b, n, shape) for b in (sg, sd, sm, sv))
    g_cw = lax.dynamic_slice_in_dim(unpack(sg, "conv_w", (2, 3, WIDTH)), me * conv_cols, conv_cols, axis=2)
    cw2 = lambda a: a.reshape(6, conv_cols)
    d_cw, m_cw, v_cw = _adamw_plain(cw2(g_cw), cw2(conv_w), cw2(m_conv_w), cw2(v_conv_w), "adamw_conv_w")
    out["conv_w"] = (g_cw,) + tuple(a.reshape(2, 3, conv_cols) for a in (d_cw, m_cw, v_cw))

    out["w_in"] = _adamw_layers(recv[0][0], recv[1][0], w_in, m_w_in, v_w_in, "adamw_w_in")
    cols = D_MODEL // N_DEV
    wb3 = lambda a: a.reshape(2, 3 * WIDTH, cols)
    out["w_branch"] = tuple(a.reshape(2, 3, WIDTH, cols) for a in _adamw_layers(
        recv[0][1], recv[1][1], wb3(w_branch), wb3(m_w_branch), wb3(v_w_branch), "adamw_w_branch"))
    out["w_out"] = _adamw_layers(recv[0][2], recv[1][2], w_out, m_w_out, v_w_out, "adamw_w_out")

    order = ("pre_norm_g", "w_in", "pool_w", "pool_scale", "conv_w", "conv_b", "w_branch", "w_out", "post_norm_g")
    return (loss, grad_x) + tuple(out[n][k] for k in range(4) for n in order)
```

```python
import functools

import jax
import jax.numpy as jnp
from jax import lax
from jax.experimental import pallas as pl
from jax.experimental.pallas import tpu as pltpu

F32 = jnp.float32
BF16 = jnp.bfloat16

N_DEV = 8
D_MODEL = 1024
WIDTH = 512
N_IN = 8192
COLS_PER_DEV = N_IN // N_DEV
HEAD_DIM = 64
LANES = 128
SB_SCALE = HEAD_DIM ** -0.5
LOG2E = 1.4426950408889634
RMS_EPS = 1e-6
POOL_HALO = 16
CONV_HALO = 8
ADAM_LR, ADAM_B1, ADAM_B2, ADAM_EPS, ADAM_WD, ADAM_STEP = 0.001, 0.9, 0.999, 1e-08, 0.01, 10
VMEM_LIMIT = 60 * 1024 * 1024

CB_POOL_V, CB_POOL_G = 0, 4
CB_CONV_X, CB_CONV_GB, CB_CONV_GC, CB_CONV_G = 8, 12, 16, 20
CB_SB_Q, CB_SB_K, CB_SB_V, CB_SB_G = 24, 28, 32, 36
MERGE_BLOCK_1024 = 5

DU_PIECES = 16
DU_MERGE = (0, 6)
DU_POOL = (6, 2)
DU_CONV = (8, 4)
DU_SB_QG = (12, 2)
DU_SB_KV = (14, 2)


def _du_pieces_of_block(j):
    first, second = 2 * (j - 5), 2 * (j - 5) + 1
    for block, (a, b) in enumerate(((6, 7), (8, 9), (10, 11), (12, 14), (15, 13))):
        first = jnp.where(j == block, a, first)
        second = jnp.where(j == block, b, second)
    return first, second


def _pcall(body, **kw):
    return pl.pallas_call(body, **kw)


def _params(sem=None):
    if sem is None:
        return pltpu.CompilerParams(vmem_limit_bytes=VMEM_LIMIT)
    return pltpu.CompilerParams(dimension_semantics=sem, vmem_limit_bytes=VMEM_LIMIT)


def _sigmoid(x):
    return 1.0 / (1.0 + jnp.exp(-x))


def _dot(a, b):
    return jnp.dot(a, b, preferred_element_type=F32)


def _dot_nt(a, b):
    return lax.dot_general(a, b, (((1,), (1,)), ((), ())), preferred_element_type=F32)


def _dot_tn(a, b):
    return lax.dot_general(a, b, (((0,), (0,)), ((), ())), preferred_element_type=F32)


def _split_bf16(x):
    hi = x.astype(BF16)
    lo = (x - hi.astype(F32)).astype(BF16)
    return hi, lo


N_PEER = N_DEV - 1
ANY_SPEC = pl.BlockSpec(memory_space=pl.ANY)


def _exchange_copies(ins, outs, send_sems, recv_sems, local_sems, gather, with_recvs=True):
    n = len(ins)
    gathers = _per_array(gather, n)
    x, y, c = lax.axis_index("x"), lax.axis_index("y"), lax.axis_index("c")
    me = 4 * x + 2 * y + c
    flip = lambda v, bit: 1 - v if bit else v
    local, sends, recvs = [], [], []
    for a in range(n):
        src = ins[a] if gathers[a] else ins[a].at[me]
        local.append(pltpu.make_async_copy(src, outs[a].at[me], local_sems.at[a]))
    for k in range(N_PEER):
        px, py, pc = flip(x, ((k + 1) >> 2) & 1), flip(y, ((k + 1) >> 1) & 1), flip(c, (k + 1) & 1)
        peer_id = 4 * px + 2 * py + pc
        for a in range(n):
            src = ins[a] if gathers[a] else ins[a].at[peer_id]
            common = dict(src_ref=src, send_sem=send_sems.at[a * N_PEER + k], recv_sem=recv_sems.at[a * N_PEER + k],
                          device_id=(px, py, pc), device_id_type=pl.DeviceIdType.MESH)
            sends.append(pltpu.make_async_remote_copy(dst_ref=outs[a].at[me], **common))
            if with_recvs:
                recvs.append(pltpu.make_async_remote_copy(dst_ref=outs[a].at[peer_id], **common))
    return local, sends, recvs


def _exchange_start(ins, outs, sems, gather):
    local, sends, _ = _exchange_copies(ins, outs, *sems, gather, with_recvs=False)
    for cp in local + sends:
        cp.start()


def _exchange_wait(ins, outs, sems, gather):
    local, sends, recvs = _exchange_copies(ins, outs, *sems, gather)
    for cp in recvs:
        cp.wait_recv()
    for cp in sends:
        cp.wait_send()
    for cp in local:
        cp.wait()


def _per_array(gather, n):
    return tuple(gather) if isinstance(gather, (tuple, list)) else (gather,) * n


def _exchange_out_shapes(arrs, gather):
    return [jax.ShapeDtypeStruct((N_DEV,) + tuple(a.shape if g else a.shape[1:]), a.dtype)
            for a, g in zip(arrs, _per_array(gather, len(arrs)))]


def _gather_two_level(arrs, name):
    n = len(arrs)

    def body(*refs):
        ins, outs = refs[:n], refs[n:2 * n]
        send_sems, recv_sems, local_sems = refs[2 * n:]
        x, y, c = lax.axis_index("x"), lax.axis_index("y"), lax.axis_index("c")
        me, sibling = (x, y, c), (x, y, 1 - c)
        chips = [(1 - x, y), (x, 1 - y), (1 - x, 1 - y)]
        slot = lambda dev: 4 * dev[0] + 2 * dev[1] + dev[2]

        def copy(a, k, block, to, src=None):
            return pltpu.make_async_remote_copy(
                src_ref=outs[a].at[slot(block)] if src is None else src, dst_ref=outs[a].at[slot(block)],
                send_sem=send_sems.at[a * N_PEER + k], recv_sem=recv_sems.at[a * N_PEER + k],
                device_id=to, device_id_type=pl.DeviceIdType.MESH)

        local = [pltpu.make_async_copy(ins[a], outs[a].at[slot(me)], local_sems.at[a]) for a in range(n)]
        first = []
        for a in range(n):
            first.append(copy(a, 0, me, sibling, src=ins[a]))
            first += [copy(a, 1 + j, me, (*chip, c), src=ins[a]) for j, chip in enumerate(chips)]
        for cp in local + first:
            cp.start()
        passed = []
        for j, chip in enumerate(chips):
            for a in range(n):
                copy(a, 1 + j, (*chip, c), me).wait_recv()
                passed.append(copy(a, 4 + j, (*chip, c), sibling))
                passed[-1].start()
        for a in range(n):
            copy(a, 0, sibling, me).wait_recv()
        for j, chip in enumerate(chips):
            for a in range(n):
                copy(a, 4 + j, (*chip, 1 - c), me).wait_recv()
        for cp in first + passed:
            cp.wait_send()
        for cp in local:
            cp.wait()

    return _pcall(
        body, name=name,
        out_shape=tuple(_exchange_out_shapes(arrs, True)),
        in_specs=[ANY_SPEC] * n, out_specs=tuple([ANY_SPEC] * n),
        scratch_shapes=_exchange_sems(n),
    )(*arrs)


def _exchange_sems(n):
    return [pltpu.SemaphoreType.DMA((n * N_PEER,)), pltpu.SemaphoreType.DMA((n * N_PEER,)),
            pltpu.SemaphoreType.DMA((n,))]


def _exchange(arrs, gather, name):
    n = len(arrs)

    def body(*refs):
        ins, outs, sems = refs[:n], refs[n:2 * n], refs[2 * n:]
        _exchange_start(ins, outs, sems, gather)
        _exchange_wait(ins, outs, sems, gather)

    return _pcall(
        body, name=name,
        out_shape=tuple(_exchange_out_shapes(arrs, gather)),
        in_specs=[ANY_SPEC] * n, out_specs=tuple([ANY_SPEC] * n),
        scratch_shapes=_exchange_sems(n),
    )(*arrs)


def _in_proj_fwd(x, g, w_all, name):
    s = x.shape[0]
    tm = min(1024, s)

    def body(x_ref, g_ref, w_ref, u_ref, h_ref, hs):
        @pl.when(pl.program_id(1) == 0)
        def _():
            xv = x_ref[...]
            r = lax.rsqrt(jnp.mean(xv * xv, axis=-1, keepdims=True) + RMS_EPS)
            hv = (xv * r * g_ref[...]).astype(BF16)
            hs[...] = hv
            h_ref[...] = hv
        u_ref[...] = _dot(hs[...], w_ref[...])

    return _pcall(
        body, name=name, grid=(s // tm, N_DEV),
        in_specs=[pl.BlockSpec((tm, D_MODEL), lambda i, j: (i, 0)),
                  pl.BlockSpec((1, D_MODEL), lambda i, j: (0, 0)),
                  pl.BlockSpec((None, D_MODEL, COLS_PER_DEV), lambda i, j: (j, 0, 0))],
        out_specs=(pl.BlockSpec((tm, COLS_PER_DEV), lambda i, j: (i, j)),
                   pl.BlockSpec((tm, D_MODEL), lambda i, j: (i, 0))),
        out_shape=(jax.ShapeDtypeStruct((s, N_IN), F32), jax.ShapeDtypeStruct((s, D_MODEL), BF16)),
        scratch_shapes=[pltpu.VMEM((tm, D_MODEL), BF16)],
        compiler_params=_params(("parallel", "arbitrary")),
    )(x, g, w_all)


def _pool_window(vs, t0, t, grp):
    ext = vs[pl.ds(t0, t + POOL_HALO), :]
    s2 = ext + pltpu.roll(ext, 1, 0)
    s4 = s2 + pltpu.roll(s2, 2, 0)
    s8 = s4 + pltpu.roll(s4, 4, 0)
    s16 = s8 + pltpu.roll(s8, 8, 0)
    sel = jnp.where(grp == 0, s2, jnp.where(grp == 1, s4, jnp.where(grp == 2, s8, s16)))
    return sel[POOL_HALO:, :], ext[POOL_HALO:, :]


def _pool_count(t0, t, grp):
    pos = t0 + lax.broadcasted_iota(jnp.int32, (t, 1), 0)
    return jnp.minimum(pos + 1, jnp.left_shift(2, grp)).astype(F32)


def _pool_fwd(u, pool_w, pool_scale, name):
    s = u.shape[0]
    t = min(256, s)

    def body(pv_ref, pg_ref, w_ref, sc_ref, y_ref, vs):
        grp = pl.program_id(0)
        vs[0:POOL_HALO, :] = jnp.zeros((POOL_HALO, LANES), F32)
        vs[POOL_HALO:, :] = pv_ref[...]
        wb = w_ref[...].astype(BF16)
        scale = sc_ref[...]

        def tile(i, carry):
            t0 = pl.multiple_of(i * t, t)
            win, v = _pool_window(vs, t0, t, grp)
            pooled = win / _pool_count(t0, t, grp) - v
            mixed = _dot(pooled.astype(BF16), wb)
            gate = pg_ref[pl.ds(t0, t), :]
            y_ref[pl.ds(t0, t), :] = (mixed * scale * (gate * _sigmoid(gate))).astype(BF16)
            return carry

        lax.fori_loop(0, s // t, tile, 0)

    return _pcall(
        body, name=name, grid=(4,),
        in_specs=[pl.BlockSpec((s, LANES), lambda g: (0, CB_POOL_V + g)),
                  pl.BlockSpec((s, LANES), lambda g: (0, CB_POOL_G + g)),
                  pl.BlockSpec((None, LANES, LANES), lambda g: (g, 0, 0)),
                  pl.BlockSpec((1, LANES), lambda g: (0, g))],
        out_specs=pl.BlockSpec((s, LANES), lambda g: (0, g)),
        out_shape=jax.ShapeDtypeStruct((s, WIDTH), BF16),
        scratch_shapes=[pltpu.VMEM((POOL_HALO + s, LANES), F32)],
        compiler_params=_params(("arbitrary",)),
    )(u, u, pool_w, pool_scale)


def _conv_taps(zs, t0, t):
    ext = zs[pl.ds(t0, t + CONV_HALO), :]
    z0 = ext[CONV_HALO:, :]
    z1 = pltpu.roll(ext, 1, 0)[CONV_HALO:, :]
    z2 = pltpu.roll(ext, 2, 0)[CONV_HALO:, :]
    return z0, z1, z2


def _conv_fwd(u, conv_w, conv_b, name):
    s = u.shape[0]
    t = min(256, s)

    def body(xc_ref, gb_ref, gc_ref, cg_ref, w_ref, b_ref, y_ref, zs):
        zs[0:CONV_HALO, :] = jnp.zeros((CONV_HALO, LANES), F32)
        zs[CONV_HALO:, :] = gc_ref[...] * xc_ref[...]
        w0, w1, w2 = w_ref[0:1, :], w_ref[1:2, :], w_ref[2:3, :]
        bias = b_ref[...]

        def tile(i, carry):
            t0 = pl.multiple_of(i * t, t)
            z0, z1, z2 = _conv_taps(zs, t0, t)
            conv = w0 * z2 + w1 * z1 + w2 * z0
            gate = cg_ref[pl.ds(t0, t), :]
            y = gb_ref[pl.ds(t0, t), :] * (conv + bias) * (gate * _sigmoid(gate))
            y_ref[pl.ds(t0, t), :] = y.astype(BF16)
            return carry

        lax.fori_loop(0, s // t, tile, 0)

    col = lambda base: pl.BlockSpec((s, LANES), lambda j: (0, base + j))
    return _pcall(
        body, name=name, grid=(4,),
        in_specs=[col(CB_CONV_X), col(CB_CONV_GB), col(CB_CONV_GC), col(CB_CONV_G),
                  pl.BlockSpec((3, LANES), lambda j: (0, j)),
                  pl.BlockSpec((1, LANES), lambda j: (0, j))],
        out_specs=pl.BlockSpec((s, LANES), lambda j: (0, j)),
        out_shape=jax.ShapeDtypeStruct((s, WIDTH), BF16),
        scratch_shapes=[pltpu.VMEM((CONV_HALO + s, LANES), F32)],
        compiler_params=_params(("arbitrary",)),
    )(u, u, u, u, conv_w, conv_b)


def _first_head_lanes(rows, width=LANES):
    lane = lax.broadcasted_iota(jnp.int32, (rows, width), 1)
    return jnp.bitwise_and(lane, LANES - 1) < HEAD_DIM


def _stack_heads(x, first):
    zero = jnp.zeros_like(x)
    return jnp.concatenate([jnp.where(first, x, zero), jnp.where(first, zero, x)], axis=0).astype(BF16)


def _causal_mask(tq, tk, copies):
    row = lax.broadcasted_iota(jnp.int32, (tq, tk), 0)
    col = lax.broadcasted_iota(jnp.int32, (tq, tk), 1)
    return jnp.concatenate([col < row] * copies, axis=0)


def _suffix_matrix(tk, inclusive, parts):
    r = lax.broadcasted_iota(jnp.int32, (parts * tk, 2 * tk), 0)
    c = lax.broadcasted_iota(jnp.int32, (parts * tk, 2 * tk), 1)
    r = jnp.bitwise_and(r, tk - 1)
    tri = (r >= c) if inclusive else (r > c)
    return jnp.where(c >= tk, 1.0, jnp.where(tri, 1.0, 0.0)).astype(BF16)


def _suffix_sums(x, m):
    hi, lo = _split_bf16(x)
    return _dot(jnp.concatenate([hi, lo], axis=1), m)


def _sb_log_terms(z, mask, m_strict):
    ls = jnp.minimum(z, 0.0) - jnp.log(1.0 + jnp.exp2(jnp.abs(z) * -LOG2E))
    lk = ls - z
    if mask is not None:
        lk = jnp.where(mask, lk, 0.0)
    return ls, _dot(lk.astype(BF16), m_strict)


SB_PAIRS = 4


def _pair_lanes(a):
    return slice(a * LANES, (a + 1) * LANES)


def _sb_fwd(u, name, xchg=None):
    s = u.shape[0]
    tq = tk = min(128, s)
    pairs = SB_PAIRS
    width = pairs * LANES
    rows = 2 * pairs * tq
    x_arrs, x_gather = xchg if xchg else ((), True)
    n_x = len(x_arrs)
    grid = (4 // pairs, s // tq)

    def body(*refs):
        q_ref, k_ref, v_ref, g_ref = refs[:4]
        x_in, refs = refs[4:4 + n_x], refs[4 + n_x:]
        o_ref, y_ref = refs[:2]
        x_out, refs = refs[2:2 + n_x], refs[2 + n_x:]
        kbf, vst, z_s, ell_s, carry_s = refs[:5]
        x_sems = refs[5:]
        i = pl.program_id(1)
        if n_x:
            @pl.when((pl.program_id(0) == 0) & (i == 0))
            def _():
                _exchange_start(x_in, x_out, x_sems, x_gather)

        @pl.when(i == 0)
        def _():
            kbf[...] = k_ref[...].astype(BF16)
            first_s = _first_head_lanes(s, width)
            vf = v_ref[...]
            vst[0] = jnp.where(first_s, vf, 0.0).astype(BF16)
            vst[1] = jnp.where(first_s, 0.0, vf).astype(BF16)

        first = _first_head_lanes(tq)
        mask = _causal_mask(tq, tk, 2 * pairs)
        m_strict = _suffix_matrix(tk, False, 1)
        qcat = jnp.concatenate([_stack_heads(q_ref[:, _pair_lanes(a)] * SB_SCALE, first) for a in range(pairs)],
                               axis=0)

        def scores(b):
            off = pl.multiple_of(jnp.maximum(b, 0) * tk, tk)
            z_s[...] = jnp.concatenate(
                [_dot_nt(qcat[a * 2 * tq:(a + 1) * 2 * tq], kbf[pl.ds(off, tk), _pair_lanes(a)])
                 for a in range(pairs)], axis=0)

        def log_weights(m):
            ls, cs = _sb_log_terms(z_s[...], m, m_strict)
            carry = carry_s[...]
            ell_s[...] = ls + cs[:, :tk] + carry
            carry_s[...] = carry + cs[:, tk:]

        def consume(b, accs, m):
            w = jnp.exp(ell_s[...])
            if m is not None:
                w = jnp.where(m, w, 0.0)
            wb = w.astype(BF16)
            off = pl.multiple_of(b * tk, tk)
            new = []
            for a in range(pairs):
                r0 = a * 2 * tq
                wcat = jnp.concatenate([wb[r0:r0 + tq], wb[r0 + tq:r0 + 2 * tq]], axis=1)
                vcat = jnp.concatenate([vst[0, pl.ds(off, tk), _pair_lanes(a)], vst[1, pl.ds(off, tk), _pair_lanes(a)]],
                                       axis=0)
                new.append(accs[a] + _dot(wcat, vcat))
            return tuple(new)

        carry_s[...] = jnp.zeros((rows, tk), F32)
        scores(i)
        log_weights(mask)
        scores(i - 1)
        accs = consume(i, tuple(jnp.zeros((tq, LANES), F32) for _ in range(pairs)), mask)
        log_weights(None)
        scores(i - 2)

        def step(n, accs):
            accs = consume(i - n, accs, None)
            log_weights(None)
            scores(i - n - 2)
            return accs

        accs = lax.fori_loop(1, i + 1, step, accs)
        o = jnp.concatenate(accs, axis=1)
        o_ref[...] = o
        gate = g_ref[...]
        y_ref[...] = (o * (gate * _sigmoid(gate))).astype(BF16)
        if n_x:
            @pl.when((pl.program_id(0) == grid[0] - 1) & (i == grid[1] - 1))
            def _():
                _exchange_wait(x_in, x_out, x_sems, x_gather)

    base = lambda cb: cb // pairs
    qblk = lambda cb: pl.BlockSpec((tq, width), lambda p, i: (i, base(cb) + p))
    full = lambda cb: pl.BlockSpec((s, width), lambda p, i: (0, base(cb) + p), pipeline_mode=pl.Buffered(1))
    state = pltpu.VMEM((rows, tk), F32)
    return _pcall(
        body, name=name, grid=grid,
        in_specs=[qblk(CB_SB_Q), full(CB_SB_K), full(CB_SB_V), qblk(CB_SB_G)] + [ANY_SPEC] * n_x,
        out_specs=(qblk(0), qblk(0)) + (ANY_SPEC,) * n_x,
        out_shape=(jax.ShapeDtypeStruct((s, WIDTH), F32), jax.ShapeDtypeStruct((s, WIDTH), BF16))
        + tuple(_exchange_out_shapes(x_arrs, x_gather)),
        scratch_shapes=[pltpu.VMEM((s, width), BF16), pltpu.VMEM((2, s, width), BF16), state, state, state]
        + (_exchange_sems(n_x) if n_x else []),
        compiler_params=_params(("arbitrary", "arbitrary")),
    )(u, u, u, u, *x_arrs)


def _merge_out_fwd(y_pool, y_conv, y_sb, u, wb_all, wo_all, x, g_post, layer, name, target=None):
    s = x.shape[0]
    tm = min(512, s)
    n_tiles = s // tm
    with_loss = target is not None

    def body(yp, yc, ys, m0, m1, m2, wb_ref, wo_ref, x_ref, g_ref, *rest):
        merged = jnp.zeros((tm, D_MODEL), F32)
        for n, (y_ref, m_ref) in enumerate(((yp, m0), (yc, m1), (ys, m2))):
            merged = merged + _sigmoid(m_ref[...]) * _dot(y_ref[...], wb_ref[n])
        mb = merged.astype(BF16)
        pre = _dot(mb, wo_ref[...].reshape(D_MODEL, D_MODEL))
        r = lax.rsqrt(jnp.mean(pre * pre, axis=-1, keepdims=True) + RMS_EPS)
        y = x_ref[...] + pre * r * g_ref[...]
        if not with_loss:
            out_ref, merged_ref, pre_ref = rest
            out_ref[...] = y
        else:
            t_ref, out_ref, merged_ref, pre_ref, loss_ref, acc = rest
            i = pl.program_id(0)

            @pl.when(i == 0)
            def _():
                acc[...] = jnp.zeros_like(acc)
            err = y - t_ref[...]
            out_ref[...] = err / D_MODEL
            acc[...] += jnp.sum(err * err, axis=0, keepdims=True)

            @pl.when(i == n_tiles - 1)
            def _():
                total = jnp.sum(acc[...], axis=1, keepdims=True) * (0.5 / D_MODEL)
                loss_ref[...] = jnp.broadcast_to(total, (1, LANES))
        merged_ref[...] = mb
        pre_ref[...] = pre

    rows = lambda w: pl.BlockSpec((tm, w), lambda i: (i, 0))
    merge = lambda n: pl.BlockSpec((tm, D_MODEL), lambda i: (i, MERGE_BLOCK_1024 + n))
    out_specs = (rows(D_MODEL), rows(D_MODEL), rows(D_MODEL))
    out_shape = (jax.ShapeDtypeStruct((s, D_MODEL), F32), jax.ShapeDtypeStruct((s, D_MODEL), BF16),
                 jax.ShapeDtypeStruct((s, D_MODEL), F32))
    if with_loss:
        out_specs += (pl.BlockSpec((1, LANES), lambda i: (0, 0)),)
        out_shape += (jax.ShapeDtypeStruct((1, LANES), F32),)
    return _pcall(
        body, name=name, grid=(n_tiles,),
        in_specs=[rows(WIDTH), rows(WIDTH), rows(WIDTH), merge(0), merge(1), merge(2),
                  pl.BlockSpec((None, 3, WIDTH, D_MODEL), lambda i: (layer, 0, 0, 0)),
                  pl.BlockSpec((N_DEV, None, D_MODEL // N_DEV, D_MODEL), lambda i: (0, layer, 0, 0)),
                  rows(D_MODEL), pl.BlockSpec((1, D_MODEL), lambda i: (0, 0))] + ([rows(D_MODEL)] if with_loss else []),
        out_specs=out_specs, out_shape=out_shape,
        scratch_shapes=[pltpu.VMEM((1, D_MODEL), F32)] if with_loss else [],
        compiler_params=_params(("arbitrary",)),
    )(y_pool, y_conv, y_sb, u, u, u, wb_all, wo_all, x, g_post, *([target] if with_loss else []))


def _out_proj_bwd(dy, pre, g_post, merged, wo_all, layer, name):
    s = dy.shape[0]
    tm = min(512, s)
    n_tiles = s // tm

    def body(dy_ref, pre_ref, g_ref, mg_ref, wo_ref, dm_ref, dwo_ref, dg_ref, acc):
        i = pl.program_id(0)

        @pl.when(i == 0)
        def _():
            acc[...] = jnp.zeros_like(acc)
            dg_ref[...] = jnp.zeros_like(dg_ref)
        dyv, pre_v = dy_ref[...], pre_ref[...]
        r = lax.rsqrt(jnp.mean(pre_v * pre_v, axis=-1, keepdims=True) + RMS_EPS)
        dg_ref[...] += jnp.sum(dyv * pre_v * r, axis=0, keepdims=True)
        a = dyv * g_ref[...]
        dpre = r * a - pre_v * (r * r * r) * jnp.mean(a * pre_v, axis=-1, keepdims=True)
        db = dpre.astype(BF16)
        acc[...] += _dot_tn(mg_ref[...], db)
        dm_ref[...] = _dot_nt(db, wo_ref[...].reshape(D_MODEL, D_MODEL))

        @pl.when(i == n_tiles - 1)
        def _():
            dwo_ref[...] = acc[...].astype(BF16)

    rows = lambda: pl.BlockSpec((tm, D_MODEL), lambda i: (i, 0))
    return _pcall(
        body, name=name, grid=(n_tiles,),
        in_specs=[rows(), rows(), pl.BlockSpec((1, D_MODEL), lambda i: (0, 0)), rows(),
                  pl.BlockSpec((N_DEV, None, D_MODEL // N_DEV, D_MODEL), lambda i: (0, layer, 0, 0))],
        out_specs=(rows(), pl.BlockSpec((D_MODEL, D_MODEL), lambda i: (0, 0)),
                   pl.BlockSpec((1, D_MODEL), lambda i: (0, 0))),
        out_shape=(jax.ShapeDtypeStruct((s, D_MODEL), F32), jax.ShapeDtypeStruct((D_MODEL, D_MODEL), BF16),
                   jax.ShapeDtypeStruct((1, D_MODEL), F32)),
        scratch_shapes=[pltpu.VMEM((D_MODEL, D_MODEL), F32)],
        compiler_params=_params(("arbitrary",)),
    )(dy, pre, g_post, merged, wo_all)


def _merge_bwd(dmerged, y_pool, y_conv, y_sb, u, wb_all, layer, name):
    s = dmerged.shape[0]
    tm = min(512, s)
    n_tiles = s // tm
    cols = D_MODEL // N_DEV

    def body(dm_ref, yp, yc, ys, m0, m1, m2, wb_ref, du_ref, dyp, dyc, dys, dwb_ref, acc):
        i = pl.program_id(0)

        @pl.when(i == 0)
        def _():
            acc[...] = jnp.zeros_like(acc)
        dm = dm_ref[...]
        for n, (y_ref, m_ref, dy_ref) in enumerate(((yp, m0, dyp), (yc, m1, dyc), (ys, m2, dys))):
            yv = y_ref[...]
            wb = wb_ref[n]
            gate = _sigmoid(m_ref[...])
            proj = _dot(yv, wb)
            dgate = (dm * proj * gate * (1.0 - gate)).astype(BF16)
            du_ref[2 * n] = dgate[:, :WIDTH]
            du_ref[2 * n + 1] = dgate[:, WIDTH:]
            dproj = (dm * gate).astype(BF16)
            acc[n] += _dot_tn(yv, dproj)
            dy_ref[...] = _dot_nt(dproj, wb)

        @pl.when(i == n_tiles - 1)
        def _():
            for j in range(N_DEV):
                for n in range(3):
                    dwb_ref[j, n] = acc[n, :, j * cols:(j + 1) * cols].astype(BF16)

    rows = lambda w: pl.BlockSpec((tm, w), lambda i: (i, 0))
    merge = lambda n: pl.BlockSpec((tm, D_MODEL), lambda i: (i, MERGE_BLOCK_1024 + n))
    return _pcall(
        body, name=name, grid=(n_tiles,),
        in_specs=[rows(D_MODEL), rows(WIDTH), rows(WIDTH), rows(WIDTH), merge(0), merge(1), merge(2),
                  pl.BlockSpec((None, 3, WIDTH, D_MODEL), lambda i: (layer, 0, 0, 0))],
        out_specs=(pl.BlockSpec((DU_MERGE[1], tm, WIDTH), lambda i: (DU_MERGE[0] // DU_MERGE[1], i, 0)),
                   rows(WIDTH), rows(WIDTH), rows(WIDTH),
                   pl.BlockSpec((N_DEV, 3, WIDTH, cols), lambda i: (0, 0, 0, 0))),
        out_shape=(jax.ShapeDtypeStruct((DU_PIECES, s, WIDTH), BF16),
                   jax.ShapeDtypeStruct((s, WIDTH), F32), jax.ShapeDtypeStruct((s, WIDTH), F32),
                   jax.ShapeDtypeStruct((s, WIDTH), F32),
                   jax.ShapeDtypeStruct((N_DEV, 3, WIDTH, cols), BF16)),
        scratch_shapes=[pltpu.VMEM((3, WIDTH, D_MODEL), F32)],
        compiler_params=_params(("arbitrary",)),
    )(dmerged, y_pool, y_conv, y_sb, u, u, u, wb_all)


def _sb_bwd(u, o, dys, du, name, xchg=None):
    s = u.shape[0]
    tq = tk = min(128, s)
    pairs = SB_PAIRS
    width = pairs * LANES
    assert width == WIDTH
    rows = 2 * pairs * tq
    pair_rows = lambda a: slice(a * 2 * tq, (a + 1) * 2 * tq)

    x_arrs, x_gather = xchg if xchg else ((), True)
    n_x = len(x_arrs)
    grid = (4 // pairs, s // tq)

    def body(*refs):
        q_ref, k_ref, v_ref, g_ref, o_ref, dys_ref = refs[:6]
        x_in, refs = refs[7:7 + n_x], refs[7 + n_x:]
        du_ref, dk_ref, dv_ref = refs[:3]
        dq_ref, dg_ref = du_ref.at[0], du_ref.at[1]
        x_out, refs = refs[3:3 + n_x], refs[3 + n_x:]
        kbf, vbf, kst, z_s, ell_s, ls_s, cl_s, wb_s, g_s, bef_s, cg_s, beta_s = refs[:12]
        x_sems = refs[12:]
        i = pl.program_id(1)
        if n_x:
            @pl.when((pl.program_id(0) == 0) & (i == 0))
            def _():
                _exchange_start(x_in, x_out, x_sems, x_gather)

        @pl.when(i == 0)
        def _():
            dk_ref[...] = jnp.zeros_like(dk_ref)
            dv_ref[...] = jnp.zeros_like(dv_ref)
            kf = k_ref[...]
            kbf[...] = kf.astype(BF16)
            vbf[...] = v_ref[...].astype(BF16)
            first_s = _first_head_lanes(s, width)
            kst[0] = jnp.where(first_s, kf, 0.0).astype(BF16)
            kst[1] = jnp.where(first_s, 0.0, kf).astype(BF16)

        first = _first_head_lanes(tq)
        mask = _causal_mask(tq, tk, 2 * pairs)
        m_strict = _suffix_matrix(tk, False, 1)
        m_incl = _suffix_matrix(tk, True, 2)

        gate = g_ref[...]
        sg = _sigmoid(gate)
        dy = dys_ref[...]
        ov = o_ref[...]
        dg_ref[...] = (dy * ov * (sg * (1.0 + gate * (1.0 - sg)))).astype(BF16)
        do = (dy * (gate * sg)).astype(BF16)
        prod = do.astype(F32) * ov
        row_sum = lambda v: jnp.broadcast_to(jnp.sum(v, axis=1, keepdims=True), (tq, tk))
        dsum, docat, qcat = [], [], []
        for a in range(pairs):
            pa = prod[:, _pair_lanes(a)]
            dsum += [row_sum(jnp.where(first, pa, 0.0)), row_sum(jnp.where(first, 0.0, pa))]
            docat.append(_stack_heads(do[:, _pair_lanes(a)], first))
            qcat.append(_stack_heads(q_ref[:, _pair_lanes(a)] * SB_SCALE, first))
        dsum = jnp.concatenate(dsum, axis=0)

        def block_start(b):
            return pl.multiple_of(jnp.maximum(b, 0) * tk, tk)

        def scores(b):
            off = block_start(b)
            z_s[...] = jnp.concatenate([_dot_nt(qcat[a], kbf[pl.ds(off, tk), _pair_lanes(a)]) for a in range(pairs)],
                                       axis=0)

        def log_weights(m):
            ls, cs = _sb_log_terms(z_s[...], m, m_strict)
            cl = cl_s[...]
            ell_s[...] = ls + cs[:, :tk] + cl
            cl_s[...] = cl + cs[:, tk:]
            ls_s[...] = ls

        def weights(b, m):
            off = block_start(b)
            dwt = jnp.concatenate([_dot_nt(docat[a], vbf[pl.ds(off, tk), _pair_lanes(a)]) for a in range(pairs)],
                                  axis=0)
            w = jnp.exp(ell_s[...])
            if m is not None:
                w = jnp.where(m, w, 0.0)
            wb = w.astype(BF16)
            g = dwt * wb.astype(F32)
            gs = _suffix_sums(g, m_incl)
            cg = cg_s[...]
            beta = jnp.exp(ls_s[...])
            wb_s[...] = wb
            beta_s[...] = beta
            g_s[...] = g * (1.0 - beta)
            bef_s[...] = gs[:, :tk] + cg
            cg_s[...] = cg + gs[:, tk:]

        def grads(b, dqs, m):
            dz = g_s[...] - beta_s[...] * (dsum - bef_s[...])
            if m is not None:
                dz = jnp.where(m, dz, 0.0)
            dzb = dz.astype(BF16)
            wb = wb_s[...]
            off = pl.multiple_of(b * tk, tk)
            new = []
            for a in range(pairs):
                r0 = a * 2 * tq
                kcat = jnp.concatenate([kst[0, pl.ds(off, tk), _pair_lanes(a)], kst[1, pl.ds(off, tk), _pair_lanes(a)]],
                                       axis=0)
                new.append(dqs[a] + _dot(jnp.concatenate([dzb[r0:r0 + tq], dzb[r0 + tq:r0 + 2 * tq]], axis=1), kcat))
                dk_ref[pl.ds(off, tk), _pair_lanes(a)] += _dot_tn(dzb[pair_rows(a)], qcat[a])
                dv_ref[pl.ds(off, tk), _pair_lanes(a)] += _dot_tn(wb[pair_rows(a)], docat[a])
            return tuple(new)

        zero = jnp.zeros((rows, tk), F32)
        cl_s[...] = zero
        cg_s[...] = zero
        scores(i)
        log_weights(mask)
        scores(i - 1)
        weights(i, mask)
        log_weights(None)
        scores(i - 2)
        dqs = grads(i, tuple(jnp.zeros((tq, LANES), F32) for _ in range(pairs)), mask)
        weights(i - 1, None)
        log_weights(None)
        scores(i - 3)

        def step(n, dqs):
            dqs = grads(i - n, dqs, None)
            weights(i - n - 1, None)
            log_weights(None)
            scores(i - n - 3)
            return dqs

        dqs = lax.fori_loop(1, i + 1, step, dqs)
        dq_ref[...] = (jnp.concatenate(dqs, axis=1) * SB_SCALE).astype(BF16)
        if n_x:
            @pl.when((pl.program_id(0) == grid[0] - 1) & (i == grid[1] - 1))
            def _():
                _exchange_wait(x_in, x_out, x_sems, x_gather)

    base = lambda cb: cb // pairs
    qblk = lambda cb: pl.BlockSpec((tq, width), lambda p, i: (i, base(cb) + p))
    full = lambda cb: pl.BlockSpec((s, width), lambda p, i: (0, base(cb) + p), pipeline_mode=pl.Buffered(1))
    state = pltpu.VMEM((rows, tk), F32)
    return _pcall(
        body, name=name, grid=grid,
        in_specs=[qblk(CB_SB_Q), full(CB_SB_K), full(CB_SB_V), qblk(CB_SB_G), qblk(0), qblk(0), ANY_SPEC]
        + [ANY_SPEC] * n_x,
        out_specs=(pl.BlockSpec((DU_SB_QG[1], tq, WIDTH), lambda p, i: (DU_SB_QG[0] // DU_SB_QG[1], i, 0)),
                   full(0), full(0)) + (ANY_SPEC,) * n_x,
        out_shape=(jax.ShapeDtypeStruct(du.shape, du.dtype), jax.ShapeDtypeStruct((s, WIDTH), F32),
                   jax.ShapeDtypeStruct((s, WIDTH), F32)) + tuple(_exchange_out_shapes(x_arrs, x_gather)),
        input_output_aliases={6: 0},
        scratch_shapes=[pltpu.VMEM((s, width), BF16), pltpu.VMEM((s, width), BF16), pltpu.VMEM((2, s, width), BF16),
                        state, state, state, state, pltpu.VMEM((rows, tk), BF16),
                        state, state, state, state] + (_exchange_sems(n_x) if n_x else []),
        compiler_params=_params(("arbitrary", "arbitrary")),
    )(u, u, u, u, o, dys, du, *x_arrs)


def _conv_bwd(u, conv_w, conv_b, dyc, du, name):
    s = u.shape[0]
    t = min(256, s)
    n_tiles = s // t

    def body(xc_ref, gb_ref, gc_ref, cg_ref, w_ref, b_ref, dy_ref, du_in, du_ref, dw_ref, db_ref, zs, ds):
        dxc_ref, dgb_ref, dgc_ref, dcg_ref = (du_ref.at[p] for p in range(4))
        zs[0:CONV_HALO, :] = jnp.zeros((CONV_HALO, LANES), F32)
        zs[CONV_HALO:, :] = gc_ref[...] * xc_ref[...]
        ds[s:, :] = jnp.zeros((CONV_HALO, LANES), F32)
        w0, w1, w2 = w_ref[0:1, :], w_ref[1:2, :], w_ref[2:3, :]
        bias = b_ref[...]

        def first(i, sums):
            t0 = pl.multiple_of(i * t, t)
            z0, z1, z2 = _conv_taps(zs, t0, t)
            pre = w0 * z2 + w1 * z1 + w2 * z0 + bias
            gate = cg_ref[pl.ds(t0, t), :]
            sg = _sigmoid(gate)
            gb = gb_ref[pl.ds(t0, t), :]
            dy = dy_ref[pl.ds(t0, t), :]
            dcg_ref[pl.ds(t0, t), :] = (dy * gb * pre * (sg * (1.0 + gate * (1.0 - sg)))).astype(BF16)
            dgb_ref[pl.ds(t0, t), :] = (dy * pre * (gate * sg)).astype(BF16)
            dc = dy * gb * (gate * sg)
            ds[pl.ds(t0, t), :] = dc
            red = lambda v: jnp.sum(v, axis=0, keepdims=True)
            return (sums[0] + red(dc * z2), sums[1] + red(dc * z1), sums[2] + red(dc * z0), sums[3] + red(dc))

        zrow = jnp.zeros((1, LANES), F32)
        sw0, sw1, sw2, sb = lax.fori_loop(0, n_tiles, first, (zrow, zrow, zrow, zrow))
        dw_ref[0:1, :] = sw0
        dw_ref[1:2, :] = sw1
        dw_ref[2:3, :] = sw2
        db_ref[...] = sb

        def second(i, carry):
            t0 = pl.multiple_of(i * t, t)
            ext = ds[pl.ds(t0, t + CONV_HALO), :]
            n = t + CONV_HALO
            d0 = ext[:t, :]
            d1 = pltpu.roll(ext, n - 1, 0)[:t, :]
            d2 = pltpu.roll(ext, n - 2, 0)[:t, :]
            dz = w2 * d0 + w1 * d1 + w0 * d2
            dgc_ref[pl.ds(t0, t), :] = (dz * xc_ref[pl.ds(t0, t), :]).astype(BF16)
            dxc_ref[pl.ds(t0, t), :] = (dz * gc_ref[pl.ds(t0, t), :]).astype(BF16)
            return carry

        lax.fori_loop(0, n_tiles, second, 0)

    col = lambda base: pl.BlockSpec((s, LANES), lambda j: (0, base + j))
    first, count = DU_CONV
    return _pcall(
        body, name=name, grid=(4,),
        in_specs=[col(CB_CONV_X), col(CB_CONV_GB), col(CB_CONV_GC), col(CB_CONV_G),
                  pl.BlockSpec((3, LANES), lambda j: (0, j)), pl.BlockSpec((1, LANES), lambda j: (0, j)), col(0),
                  ANY_SPEC],
        out_specs=(pl.BlockSpec((count, s, LANES), lambda j: (first // count, 0, j)),
                   pl.BlockSpec((3, LANES), lambda j: (0, j)), pl.BlockSpec((1, LANES), lambda j: (0, j))),
        out_shape=(jax.ShapeDtypeStruct(du.shape, du.dtype),
                   jax.ShapeDtypeStruct((3, WIDTH), F32), jax.ShapeDtypeStruct((1, WIDTH), F32)),
        scratch_shapes=[pltpu.VMEM((CONV_HALO + s, LANES), F32), pltpu.VMEM((s + CONV_HALO, LANES), F32)],
        input_output_aliases={7: 0},
        compiler_params=_params(("arbitrary",)),
    )(u, u, u, u, conv_w, conv_b, dyc, du)


def _pool_bwd(u, pool_w, pool_scale, dyp, du, name):
    s = u.shape[0]
    t = min(256, s)
    n_tiles = s // t

    def body(pv_ref, pg_ref, w_ref, sc_ref, dy_ref, du_in, du_ref, dw_ref, dsc_ref, vs, es, dps):
        dpv_ref, dpg_ref = du_ref.at[0], du_ref.at[1]
        grp = pl.program_id(0)
        vs[0:POOL_HALO, :] = jnp.zeros((POOL_HALO, LANES), F32)
        vs[POOL_HALO:, :] = pv_ref[...]
        es[s:, :] = jnp.zeros((POOL_HALO, LANES), F32)
        wb = w_ref[...].astype(BF16)
        scale = sc_ref[...]

        def first(i, sums):
            dw, dsc = sums
            t0 = pl.multiple_of(i * t, t)
            win, v = _pool_window(vs, t0, t, grp)
            cnt = _pool_count(t0, t, grp)
            pb = (win / cnt - v).astype(BF16)
            mixed = _dot(pb, wb)
            gate = pg_ref[pl.ds(t0, t), :]
            sg = _sigmoid(gate)
            dy = dy_ref[pl.ds(t0, t), :]
            dpg_ref[pl.ds(t0, t), :] = (dy * (mixed * scale) * (sg * (1.0 + gate * (1.0 - sg)))).astype(BF16)
            dms = dy * (gate * sg)
            dsc = dsc + jnp.sum(dms * mixed, axis=0, keepdims=True)
            dmb = (dms * scale).astype(BF16)
            dw = dw + _dot_tn(pb, dmb)
            dpooled = _dot_nt(dmb, wb)
            dps[pl.ds(t0, t), :] = dpooled
            es[pl.ds(t0, t), :] = dpooled / cnt
            return dw, dsc

        dw, dsc = lax.fori_loop(0, n_tiles, first, (jnp.zeros((LANES, LANES), F32), jnp.zeros((1, LANES), F32)))
        dw_ref[...] = dw
        dsc_ref[...] = dsc

        def second(i, carry):
            t0 = pl.multiple_of(i * t, t)
            ext = es[pl.ds(t0, t + POOL_HALO), :]
            n = t + POOL_HALO
            f2 = ext + pltpu.roll(ext, n - 1, 0)
            f4 = f2 + pltpu.roll(f2, n - 2, 0)
            f8 = f4 + pltpu.roll(f4, n - 4, 0)
            f16 = f8 + pltpu.roll(f8, n - 8, 0)
            sel = jnp.where(grp == 0, f2, jnp.where(grp == 1, f4, jnp.where(grp == 2, f8, f16)))
            dpv_ref[pl.ds(t0, t), :] = (sel[:t, :] - dps[pl.ds(t0, t), :]).astype(BF16)
            return carry

        lax.fori_loop(0, n_tiles, second, 0)

    col = lambda base: pl.BlockSpec((s, LANES), lambda g: (0, base + g))
    first, count = DU_POOL
    return _pcall(
        body, name=name, grid=(4,),
        in_specs=[col(CB_POOL_V), col(CB_POOL_G), pl.BlockSpec((None, LANES, LANES), lambda g: (g, 0, 0)),
                  pl.BlockSpec((1, LANES), lambda g: (0, g)), col(0), ANY_SPEC],
        out_specs=(pl.BlockSpec((count, s, LANES), lambda g: (first // count, 0, g)),
                   pl.BlockSpec((None, LANES, LANES), lambda g: (g, 0, 0)),
                   pl.BlockSpec((1, LANES), lambda g: (0, g))),
        out_shape=(jax.ShapeDtypeStruct(du.shape, du.dtype),
                   jax.ShapeDtypeStruct((4, LANES, LANES), F32), jax.ShapeDtypeStruct((1, WIDTH), F32)),
        scratch_shapes=[pltpu.VMEM((POOL_HALO + s, LANES), F32), pltpu.VMEM((s + POOL_HALO, LANES), F32),
                        pltpu.VMEM((s, LANES), F32)],
        input_output_aliases={5: 0},
        compiler_params=_params(("arbitrary",)),
    )(u, u, pool_w, pool_scale, dyp, du)


def _in_proj_bwd_x(du, w_all, x, g_pre, dy, name):
    s = x.shape[0]
    tm = min(1024, s)
    grid = (s // tm, N_DEV)

    def body(dua_ref, dub_ref, w_ref, x_ref, g_ref, dy_ref, dx_ref, dg_ref, acc):
        i, k = pl.program_id(0), pl.program_id(1)

        @pl.when(k == 0)
        def _():
            acc[...] = jnp.zeros_like(acc)

        @pl.when((k == 0) & (i == 0))
        def _():
            dg_ref[...] = jnp.zeros_like(dg_ref)
        acc[...] += _dot_nt(jnp.concatenate([dua_ref[...], dub_ref[...]], axis=1), w_ref[...])

        @pl.when(k == N_DEV - 1)
        def _():
            dh, xv = acc[...], x_ref[...]
            r = lax.rsqrt(jnp.mean(xv * xv, axis=-1, keepdims=True) + RMS_EPS)
            dg_ref[...] += jnp.sum(dh * xv * r, axis=0, keepdims=True)
            a = dh * g_ref[...]
            dx_ref[...] = dy_ref[...] + r * a - xv * (r * r * r) * jnp.mean(a * xv, axis=-1, keepdims=True)

    rows = lambda: pl.BlockSpec((tm, D_MODEL), lambda i, k: (i, 0))
    vec = lambda: pl.BlockSpec((1, D_MODEL), lambda i, k: (0, 0))
    piece = lambda half: pl.BlockSpec((None, tm, WIDTH), lambda i, k: (_du_pieces_of_block(k)[half], i, 0))
    return _pcall(
        body, name=name, grid=grid,
        in_specs=[piece(0), piece(1), pl.BlockSpec((None, D_MODEL, COLS_PER_DEV), lambda i, k: (k, 0, 0)),
                  rows(), vec(), rows()],
        out_specs=(rows(), vec()),
        out_shape=(jax.ShapeDtypeStruct((s, D_MODEL), F32), jax.ShapeDtypeStruct((1, D_MODEL), F32)),
        scratch_shapes=[pltpu.VMEM((tm, D_MODEL), F32)],
        compiler_params=_params(("arbitrary", "arbitrary")),
    )(du, du, w_all, x, g_pre, dy)


ROW_OFFSETS = (6, 7, 2, 4, 3, 5, 0, 1)


def _in_proj_bwd_send(h, du, w_all, x, g_pre, dy, name):
    s = x.shape[0]
    tk = s // N_DEV
    tm = min(1024, s)
    n_i = s // tm
    grid = (N_DEV + n_i, N_DEV)
    last = N_DEV - 1
    def offset(row):
        return functools.reduce(lambda acc, rn: jnp.where(row == rn[0], rn[1], acc), enumerate(ROW_OFFSETS), 0)

    def body(me_ref, h_ref, duwa_ref, duwb_ref, duxa_ref, duxb_ref, w_ref, x_ref, g_ref, dy_ref,
             dx_ref, dg_ref, recv_ref, part_ref, acc_w, stage, acc_x, send_sems, recv_sems, park_sems):
        r, k = pl.program_id(0), pl.program_id(1)
        x_, y_, c_ = lax.axis_index("x"), lax.axis_index("y"), lax.axis_index("c")
        me = 4 * x_ + 2 * y_ + c_
        flip = lambda v, bit: 1 - v if bit else v
        peer = lambda n: (flip(x_, (n >> 2) & 1), flip(y_, (n >> 1) & 1), flip(c_, n & 1))

        def park(row):
            n = ROW_OFFSETS[row]
            dst = recv_ref.at[me] if n == 0 else part_ref.at[n]
            return pltpu.make_async_copy(stage.at[row % 2], dst, park_sems.at[row % 2])

        def send(n, landing=False):
            px, py, pc = peer(n)
            dst = recv_ref.at[4 * px + 2 * py + pc] if landing else recv_ref.at[me]
            return pltpu.make_async_remote_copy(
                src_ref=part_ref.at[n], dst_ref=dst, send_sem=send_sems.at[n], recv_sem=recv_sems.at[n],
                device_id=(px, py, pc), device_id_type=pl.DeviceIdType.MESH)

        def parked(row):
            park(row).wait()
            if ROW_OFFSETS[row] >= 1:
                send(ROW_OFFSETS[row]).start()

        @pl.when(r < N_DEV)
        def _():
            @pl.when(k == 0)
            def _():
                acc_w[...] = jnp.zeros_like(acc_w)
            acc_w[...] += _dot_tn(h_ref[...], jnp.concatenate([duwa_ref[...], duwb_ref[...]], axis=1))

            for row in range(N_DEV):
                @pl.when((k == last) & (r == row))
                def _():
                    if row >= 1:
                        parked(row - 1)
                    stage[row % 2] = acc_w[...].astype(BF16)
                    park(row).start()

        @pl.when(r >= N_DEV)
        def _():
            @pl.when(k == 0)
            def _():
                acc_x[...] = jnp.zeros_like(acc_x)

            @pl.when((k == 0) & (r == N_DEV))
            def _():
                dg_ref[...] = jnp.zeros_like(dg_ref)
                parked(last)
            acc_x[...] += _dot_nt(jnp.concatenate([duxa_ref[...], duxb_ref[...]], axis=1), w_ref[...])

            @pl.when(k == last)
            def _():
                dh, xv = acc_x[...], x_ref[...]
                rs = lax.rsqrt(jnp.mean(xv * xv, axis=-1, keepdims=True) + RMS_EPS)
                dg_ref[...] += jnp.sum(dh * xv * rs, axis=0, keepdims=True)
                a = dh * g_ref[...]
                dx_ref[...] = dy_ref[...] + rs * a - xv * (rs * rs * rs) * jnp.mean(a * xv, axis=-1, keepdims=True)

        @pl.when((r == grid[0] - 1) & (k == last))
        def _():
            for n in range(1, N_DEV):
                send(n).wait_send()
            for n in range(1, N_DEV):
                send(n, landing=True).wait_recv()

    in_w = lambda r: r < N_DEV
    row_x = lambda r: jnp.maximum(r - N_DEV, 0)
    rows = lambda: pl.BlockSpec((tm, D_MODEL), lambda r, k, me: (row_x(r), 0))
    vec = lambda: pl.BlockSpec((1, D_MODEL), lambda r, k, me: (0, 0))
    block_w = lambda r, me: jnp.bitwise_xor(me[0], offset(jnp.minimum(r, last)))
    block_x = lambda r, k: jnp.where(in_w(r), 0, k)
    piece_w = lambda half: pl.BlockSpec(
        (None, tk, WIDTH), lambda r, k, me: (_du_pieces_of_block(block_w(r, me))[half], jnp.where(in_w(r), k, last), 0))
    piece_x = lambda half: pl.BlockSpec(
        (None, tm, WIDTH), lambda r, k, me: (_du_pieces_of_block(block_x(r, k))[half], row_x(r), 0))
    grid_spec = pltpu.PrefetchScalarGridSpec(
        num_scalar_prefetch=1, grid=grid,
        in_specs=[pl.BlockSpec((tk, D_MODEL), lambda r, k, me: (jnp.where(in_w(r), k, last), 0)),
                  piece_w(0), piece_w(1), piece_x(0), piece_x(1),
                  pl.BlockSpec((None, D_MODEL, COLS_PER_DEV), lambda r, k, me: (block_x(r, k), 0, 0)),
                  rows(), vec(), rows()],
        out_specs=(rows(), vec(), ANY_SPEC, ANY_SPEC),
        scratch_shapes=[pltpu.VMEM((D_MODEL, COLS_PER_DEV), F32), pltpu.VMEM((2, D_MODEL, COLS_PER_DEV), BF16),
                        pltpu.VMEM((tm, D_MODEL), F32), pltpu.SemaphoreType.DMA((N_DEV,)),
                        pltpu.SemaphoreType.DMA((N_DEV,)), pltpu.SemaphoreType.DMA((2,))])
    me = 4 * lax.axis_index("x") + 2 * lax.axis_index("y") + lax.axis_index("c")
    blocks = jax.ShapeDtypeStruct((N_DEV, D_MODEL, COLS_PER_DEV), BF16)
    dx, dg, received, _ = _pcall(
        body, name=name, grid_spec=grid_spec,
        out_shape=(jax.ShapeDtypeStruct((s, D_MODEL), F32), jax.ShapeDtypeStruct((1, D_MODEL), F32), blocks, blocks),
        compiler_params=_params(("arbitrary", "arbitrary")),
    )(jnp.reshape(me, (1,)).astype(jnp.int32), h, du, du, du, du, w_all, x, g_pre, dy)
    return dx, dg, received


def _in_proj_bwd_w(h, du, name):
    s = h.shape[0]
    tk = min(4096, s)
    n_k = s // tk

    def body(h_ref, dua_ref, dub_ref, out_ref, acc):
        k = pl.program_id(1)

        @pl.when(k == 0)
        def _():
            acc[...] = jnp.zeros_like(acc)
        acc[...] += _dot_tn(h_ref[...], jnp.concatenate([dua_ref[...], dub_ref[...]], axis=1))

        @pl.when(k == n_k - 1)
        def _():
            out_ref[...] = acc[...].astype(BF16)

    piece = lambda half: pl.BlockSpec((None, tk, WIDTH), lambda j, k: (_du_pieces_of_block(j)[half], k, 0))
    return _pcall(
        body, name=name, grid=(N_DEV, n_k),
        in_specs=[pl.BlockSpec((tk, D_MODEL), lambda j, k: (k, 0)), piece(0), piece(1)],
        out_specs=pl.BlockSpec((None, D_MODEL, COLS_PER_DEV), lambda j, k: (j, 0, 0)),
        out_shape=jax.ShapeDtypeStruct((N_DEV, D_MODEL, COLS_PER_DEV), BF16),
        scratch_shapes=[pltpu.VMEM((D_MODEL, COLS_PER_DEV), F32)],
        compiler_params=_params(("parallel", "arbitrary")),
    )(h, du, du)


def _adamw_math(g, w, m, v):
    m_new = ADAM_B1 * m + (1.0 - ADAM_B1) * g
    v_new = ADAM_B2 * v + (1.0 - ADAM_B2) * (g * g)
    m_hat = m_new / (1.0 - ADAM_B1 ** ADAM_STEP)
    v_hat = v_new / (1.0 - ADAM_B2 ** ADAM_STEP)
    delta = -ADAM_LR * (m_hat / (jnp.sqrt(v_hat) + ADAM_EPS) + ADAM_WD * w)
    return delta, m_new, v_new


def _sum_partials(p_ref):
    total = p_ref[0].astype(F32)
    for d in range(1, N_DEV):
        total = total + p_ref[d].astype(F32)
    return total


def _adamw_layers(parts0, parts1, w, m, v, name):
    _, r, c = w.shape
    tr = min(256, r)
    n_r = r // tr

    def body(p0_ref, p1_ref, w_ref, m_ref, v_ref, g_ref, d_ref, mo_ref, vo_ref):
        layer = pl.program_id(0)

        @pl.when(layer == 0)
        def _():
            g_ref[...] = _sum_partials(p0_ref)

        @pl.when(layer == 1)
        def _():
            g_ref[...] = _sum_partials(p1_ref)
        d_ref[...], mo_ref[...], vo_ref[...] = _adamw_math(g_ref[...], w_ref[...], m_ref[...], v_ref[...])

    part = lambda which: pl.BlockSpec((N_DEV, tr, c), lambda l, i: (0, jnp.where(l == which, i, 0), 0))
    par = lambda: pl.BlockSpec((None, tr, c), lambda l, i: (l, i, 0))
    out = jax.ShapeDtypeStruct(w.shape, F32)
    return _pcall(
        body, name=name, grid=(2, n_r),
        in_specs=[part(0), part(1), par(), par(), par()],
        out_specs=(par(), par(), par(), par()),
        out_shape=(out, out, out, out),
        compiler_params=_params(("arbitrary", "arbitrary")),
    )(parts0, parts1, w, m, v)


def _adamw_small(parts, w, m, v, name):
    def body(p_ref, w_ref, m_ref, v_ref, g_ref, d_ref, mo_ref, vo_ref):
        g = _sum_partials(p_ref)
        g_ref[...] = g
        d_ref[...], mo_ref[...], vo_ref[...] = _adamw_math(g, w_ref[...], m_ref[...], v_ref[...])

    out = jax.ShapeDtypeStruct(w.shape, F32)
    return _pcall(body, name=name, out_shape=(out, out, out, out), compiler_params=_params())(parts, w, m, v)


def _adamw_plain(g, w, m, v, name):
    def body(g_ref, w_ref, m_ref, v_ref, d_ref, mo_ref, vo_ref):
        d_ref[...], mo_ref[...], vo_ref[...] = _adamw_math(g_ref[...], w_ref[...], m_ref[...], v_ref[...])

    out = jax.ShapeDtypeStruct(w.shape, F32)
    return _pcall(body, name=name, out_shape=(out, out, out), compiler_params=_params())(g, w, m, v)


def _rows128(a):
    return a.reshape(-1, LANES)


SMALL_NAMES = ("pre_norm_g", "pool_w", "pool_scale", "conv_w", "conv_b", "post_norm_g")


def kernel(x, pre_norm_g, w_in, pool_w, pool_scale, conv_w, conv_b, w_branch, w_out, post_norm_g, loss_target, m_pre_norm_g, m_w_in, m_pool_w, m_pool_scale, m_conv_w, m_conv_b, m_w_branch, m_w_out, m_post_norm_g, v_pre_norm_g, v_w_in, v_pool_w, v_pool_scale, v_conv_w, v_conv_b, v_w_branch, v_w_out, v_post_norm_g):
    me = 4 * lax.axis_index("x") + 2 * lax.axis_index("y") + lax.axis_index("c")
    x0 = x[0]
    target = loss_target[0]
    conv_cols = conv_w.shape[-1]

    conv_w_pad = jnp.pad(conv_w.reshape(2 * 3, conv_cols), ((0, 2), (0, LANES - conv_cols)))
    w_in_all = [None, None]
    w_in_all[0], cw_g = _gather_two_level([w_in[0].astype(BF16), conv_w_pad], "gather_w_in_0")
    conv_w_full = cw_g[:, :6, :conv_cols].reshape(N_DEV, 2, 3, conv_cols).transpose(1, 2, 0, 3).reshape(2, 3, WIDTH)
    later_weights = ([w_in[1].astype(BF16), w_branch.astype(BF16), w_out.astype(BF16)], True)

    saved = []
    xin = x0
    for l in range(2):
        u, h = _in_proj_fwd(xin, pre_norm_g[l:l + 1], w_in_all[l], f"in_proj_fwd_{l}")
        y_pool = _pool_fwd(u, pool_w[l], pool_scale[l:l + 1], f"pool_fwd_{l}")
        y_conv = _conv_fwd(u, conv_w_full[l], conv_b[l:l + 1], f"conv_fwd_{l}")
        if l == 0:
            o_sb, y_sb, w_in_all[1], wb_g, wo_all = _sb_fwd(u, f"sb_fwd_{l}", later_weights)
            wb_all = wb_g.transpose(1, 2, 3, 0, 4).reshape(2, 3, WIDTH, D_MODEL)
        else:
            o_sb, y_sb = _sb_fwd(u, f"sb_fwd_{l}")
        if l == 0:
            xout, merged, pre = _merge_out_fwd(y_pool, y_conv, y_sb, u, wb_all, wo_all, xin, post_norm_g[l:l + 1], l,
                                               f"merge_out_fwd_{l}")
        else:
            dy, merged, pre, loss_row = _merge_out_fwd(y_pool, y_conv, y_sb, u, wb_all, wo_all, xin,
                                                       post_norm_g[l:l + 1], l, f"merge_out_fwd_{l}", target)
        saved.append((xin, u, h, y_pool, y_conv, y_sb, o_sb, merged, pre))
        xin = xout

    small = [None, None]
    recv = [None, None]
    ready = []
    for l in (1, 0):
        xl, u, h, y_pool, y_conv, y_sb, o_sb, merged, pre = saved[l]
        dmerged, dwo, dg_post = _out_proj_bwd(dy, pre, post_norm_g[l:l + 1], merged, wo_all, l, f"out_proj_bwd_{l}")
        du, dyp, dyc, dys, dwb = _merge_bwd(dmerged, y_pool, y_conv, y_sb, u, wb_all, l, f"merge_bwd_{l}")
        dwb = dwb.reshape(N_DEV, 3 * WIDTH, D_MODEL // N_DEV)
        dwo = dwo.reshape(N_DEV, D_MODEL // N_DEV, D_MODEL)
        du, dcw, dcb = _conv_bwd(u, conv_w_full[l], conv_b[l:l + 1], dyc, du, f"conv_bwd_{l}")
        du, dpw, dps = _pool_bwd(u, pool_w[l], pool_scale[l:l + 1], dyp, du, f"pool_bwd_{l}")
        small[l] = dict(pool_w=dpw, pool_scale=dps, conv_w=dcw, conv_b=dcb, post_norm_g=dg_post)
        if l == 1:
            du, dk, dv = _sb_bwd(u, o_sb, dys, du, f"sb_bwd_{l}")
        else:
            small[l]["pre_norm_g"] = jnp.zeros((1, D_MODEL), F32)
            packed = jnp.concatenate(
                [_rows128(jnp.stack([small[0][n], small[1][n]])) for n in SMALL_NAMES]
                + [jnp.pad(loss_row, ((0, 7), (0, 0)))], axis=0)
            du, dk, dv, *got, packed_all = _sb_bwd(
                u, o_sb, dys, du, f"sb_bwd_{l}", (ready + [dwb, dwo, packed], (False,) * 5 + (True,)))
            recv[1] = got[:3]
        du = lax.dynamic_update_slice(du, jnp.stack([dk, dv]).astype(BF16), (DU_SB_KV[0], 0, 0))
        if l == 1:
            dwi = _in_proj_bwd_w(h, du, f"in_proj_bwd_w_{l}")
            ready = [dwi, dwb, dwo]
            dx, dg_pre = _in_proj_bwd_x(du, w_in_all[l], xl, pre_norm_g[l:l + 1], dy, f"in_proj_bwd_x_{l}")
            small[l]["pre_norm_g"] = dg_pre
        else:
            dx, dg_pre, got_dwi = _in_proj_bwd_send(h, du, w_in_all[l], xl, pre_norm_g[l:l + 1], dy,
                                                    f"in_proj_bwd_{l}")
            recv[0] = [got_dwi] + got[3:]
        dy = dx
    grad_x = dy[None]

    (g_pre_0_all,) = _exchange([_rows128(dg_pre)], True, "gather_g_pre_0")
    packed_all = lax.dynamic_update_slice(packed_all, g_pre_0_all, (0, 0, 0))
    sizes = dict(pre_norm_g=16, pool_w=1024, pool_scale=8, conv_w=24, conv_b=8, post_norm_g=16)
    n_rows = sum(sizes.values())
    loss = jnp.sum(packed_all[:, n_rows, 0])

    given = dict(pre_norm_g=(pre_norm_g, m_pre_norm_g, v_pre_norm_g), pool_w=(pool_w, m_pool_w, v_pool_w),
                 pool_scale=(pool_scale, m_pool_scale, v_pool_scale), conv_b=(conv_b, m_conv_b, v_conv_b),
                 post_norm_g=(post_norm_g, m_post_norm_g, v_post_norm_g))
    zeros_cw = jnp.zeros((sizes["conv_w"], LANES), F32)
    pack3 = [jnp.concatenate([zeros_cw if n == "conv_w" else _rows128(given[n][k]) for n in SMALL_NAMES], axis=0)
             for k in range(3)]
    sg, sd, sm, sv = _adamw_small(packed_all[:, :n_rows], pack3[0], pack3[1], pack3[2], "adamw_small")

    def unpack(buf, name, shape):
        start = 0
        for n in SMALL_NAMES:
            if n == name:
                return buf[start:start + sizes[n]].reshape(shape)
            start += sizes[n]

    out = {}
    for n in ("pre_norm_g", "pool_w", "pool_scale", "conv_b", "post_norm_g"):
        shape = given[n][0].shape
        out[n] = tuple(unpack(b, n, shape) for b in (sg, sd, sm, sv))
    g_cw = lax.dynamic_slice_in_dim(unpack(sg, "conv_w", (2, 3, WIDTH)), me * conv_cols, conv_cols, axis=2)
    cw2 = lambda a: a.reshape(6, conv_cols)
    d_cw, m_cw, v_cw = _adamw_plain(cw2(g_cw), cw2(conv_w), cw2(m_conv_w), cw2(v_conv_w), "adamw_conv_w")
    out["conv_w"] = (g_cw,) + tuple(a.reshape(2, 3, conv_cols) for a in (d_cw, m_cw, v_cw))

    out["w_in"] = _adamw_layers(recv[0][0], recv[1][0], w_in, m_w_in, v_w_in, "adamw_w_in")
    cols = D_MODEL // N_DEV
    wb3 = lambda a: a.reshape(2, 3 * WIDTH, cols)
    out["w_branch"] = tuple(a.reshape(2, 3, WIDTH, cols) for a in _adamw_layers(
        recv[0][1], recv[1][1], wb3(w_branch), wb3(m_w_branch), wb3(v_w_branch), "adamw_w_branch"))
    out["w_out"] = _adamw_layers(recv[0][2], recv[1][2], w_out, m_w_out, v_w_out, "adamw_w_out")

    order = ("pre_norm_g", "w_in", "pool_w", "pool_scale", "conv_w", "conv_b", "w_branch", "w_out", "post_norm_g")
    return (loss, grad_x) + tuple(out[n][k] for k in range(4) for n in order)
```

```python
import functools

import jax
import jax.numpy as jnp
from jax import lax
from jax.experimental import pallas as pl
from jax.experimental.pallas import tpu as pltpu

F32 = jnp.float32
BF16 = jnp.bfloat16

N_DEV = 8
D_MODEL = 1024
WIDTH = 512
N_IN = 8192
COLS_PER_DEV = N_IN // N_DEV
HEAD_DIM = 64
LANES = 128
SB_SCALE = HEAD_DIM ** -0.5
LOG2E = 1.4426950408889634
RMS_EPS = 1e-6
POOL_HALO = 16
CONV_HALO = 8
ADAM_LR, ADAM_B1, ADAM_B2, ADAM_EPS, ADAM_WD, ADAM_STEP = 0.001, 0.9, 0.999, 1e-08, 0.01, 10
VMEM_LIMIT = 60 * 1024 * 1024

CB_POOL_V, CB_POOL_G = 0, 4
CB_CONV_X, CB_CONV_GB, CB_CONV_GC, CB_CONV_G = 8, 12, 16, 20
CB_SB_Q, CB_SB_K, CB_SB_V, CB_SB_G = 24, 28, 32, 36
MERGE_BLOCK_1024 = 5

DU_PIECES = 16
DU_MERGE = (0, 6)
DU_POOL = (6, 2)
DU_CONV = (8, 4)
DU_SB_QG = (12, 2)
DU_SB_KV = (14, 2)


def _du_pieces_of_block(j):
    first, second = 2 * (j - 5), 2 * (j - 5) + 1
    for block, (a, b) in enumerate(((6, 7), (8, 9), (10, 11), (12, 14), (15, 13))):
        first = jnp.where(j == block, a, first)
        second = jnp.where(j == block, b, second)
    return first, second


def _pcall(body, **kw):
    return pl.pallas_call(body, **kw)


def _params(sem=None):
    if sem is None:
        return pltpu.CompilerParams(vmem_limit_bytes=VMEM_LIMIT)
    return pltpu.CompilerParams(dimension_semantics=sem, vmem_limit_bytes=VMEM_LIMIT)


def _sigmoid(x):
    return 1.0 / (1.0 + jnp.exp(-x))


def _dot(a, b):
    return jnp.dot(a, b, preferred_element_type=F32)


def _dot_nt(a, b):
    return lax.dot_general(a, b, (((1,), (1,)), ((), ())), preferred_element_type=F32)


def _dot_tn(a, b):
    return lax.dot_general(a, b, (((0,), (0,)), ((), ())), preferred_element_type=F32)


def _split_bf16(x):
    hi = x.astype(BF16)
    lo = (x - hi.astype(F32)).astype(BF16)
    return hi, lo


N_PEER = N_DEV - 1
ANY_SPEC = pl.BlockSpec(memory_space=pl.ANY)


def _exchange_copies(ins, outs, send_sems, recv_sems, local_sems, gather, with_recvs=True):
    n = len(ins)
    gathers = _per_array(gather, n)
    x, y, c = lax.axis_index("x"), lax.axis_index("y"), lax.axis_index("c")
    me = 4 * x + 2 * y + c
    flip = lambda v, bit: 1 - v if bit else v
    local, sends, recvs = [], [], []
    for a in range(n):
        src = ins[a] if gathers[a] else ins[a].at[me]
        local.append(pltpu.make_async_copy(src, outs[a].at[me], local_sems.at[a]))
    for k in range(N_PEER):
        px, py, pc = flip(x, ((k + 1) >> 2) & 1), flip(y, ((k + 1) >> 1) & 1), flip(c, (k + 1) & 1)
        peer_id = 4 * px + 2 * py + pc
        for a in range(n):
            src = ins[a] if gathers[a] else ins[a].at[peer_id]
            common = dict(src_ref=src, send_sem=send_sems.at[a * N_PEER + k], recv_sem=recv_sems.at[a * N_PEER + k],
                          device_id=(px, py, pc), device_id_type=pl.DeviceIdType.MESH)
            sends.append(pltpu.make_async_remote_copy(dst_ref=outs[a].at[me], **common))
            if with_recvs:
                recvs.append(pltpu.make_async_remote_copy(dst_ref=outs[a].at[peer_id], **common))
    return local, sends, recvs


def _exchange_start(ins, outs, sems, gather):
    local, sends, _ = _exchange_copies(ins, outs, *sems, gather, with_recvs=False)
    for cp in local + sends:
        cp.start()


def _exchange_wait(ins, outs, sems, gather):
    local, sends, recvs = _exchange_copies(ins, outs, *sems, gather)
    for cp in recvs:
        cp.wait_recv()
    for cp in sends:
        cp.wait_send()
    for cp in local:
        cp.wait()


def _per_array(gather, n):
    return tuple(gather) if isinstance(gather, (tuple, list)) else (gather,) * n


def _exchange_out_shapes(arrs, gather):
    return [jax.ShapeDtypeStruct((N_DEV,) + tuple(a.shape if g else a.shape[1:]), a.dtype)
            for a, g in zip(arrs, _per_array(gather, len(arrs)))]


def _gather_two_level(arrs, name):
    n = len(arrs)

    def body(*refs):
        ins, outs = refs[:n], refs[n:2 * n]
        send_sems, recv_sems, local_sems = refs[2 * n:]
        x, y, c = lax.axis_index("x"), lax.axis_index("y"), lax.axis_index("c")
        me, sibling = (x, y, c), (x, y, 1 - c)
        chips = [(1 - x, y), (x, 1 - y), (1 - x, 1 - y)]
        slot = lambda dev: 4 * dev[0] + 2 * dev[1] + dev[2]

        def copy(a, k, block, to, src=None):
            return pltpu.make_async_remote_copy(
                src_ref=outs[a].at[slot(block)] if src is None else src, dst_ref=outs[a].at[slot(block)],
                send_sem=send_sems.at[a * N_PEER + k], recv_sem=recv_sems.at[a * N_PEER + k],
                device_id=to, device_id_type=pl.DeviceIdType.MESH)

        local = [pltpu.make_async_copy(ins[a], outs[a].at[slot(me)], local_sems.at[a]) for a in range(n)]
        first = []
        for a in range(n):
            first.append(copy(a, 0, me, sibling, src=ins[a]))
            first += [copy(a, 1 + j, me, (*chip, c), src=ins[a]) for j, chip in enumerate(chips)]
        for cp in local + first:
            cp.start()
        passed = []
        for j, chip in enumerate(chips):
            for a in range(n):
                copy(a, 1 + j, (*chip, c), me).wait_recv()
                passed.append(copy(a, 4 + j, (*chip, c), sibling))
                passed[-1].start()
        for a in range(n):
            copy(a, 0, sibling, me).wait_recv()
        for j, chip in enumerate(chips):
            for a in range(n):
                copy(a, 4 + j, (*chip, 1 - c), me).wait_recv()
        for cp in first + passed:
            cp.wait_send()
        for cp in local:
            cp.wait()

    return _pcall(
        body, name=name,
        out_shape=tuple(_exchange_out_shapes(arrs, True)),
        in_specs=[ANY_SPEC] * n, out_specs=tuple([ANY_SPEC] * n),
        scratch_shapes=_exchange_sems(n),
    )(*arrs)


def _exchange_sems(n):
    return [pltpu.SemaphoreType.DMA((n * N_PEER,)), pltpu.SemaphoreType.DMA((n * N_PEER,)),
            pltpu.SemaphoreType.DMA((n,))]


def _exchange(arrs, gather, name):
    n = len(arrs)

    def body(*refs):
        ins, outs, sems = refs[:n], refs[n:2 * n], refs[2 * n:]
        _exchange_start(ins, outs, sems, gather)
        _exchange_wait(ins, outs, sems, gather)

    return _pcall(
        body, name=name,
        out_shape=tuple(_exchange_out_shapes(arrs, gather)),
        in_specs=[ANY_SPEC] * n, out_specs=tuple([ANY_SPEC] * n),
        scratch_shapes=_exchange_sems(n),
    )(*arrs)


def _in_proj_fwd(x, g, w_all, name):
    s = x.shape[0]
    tm = min(2048, s)

    def body(x_ref, g_ref, w_ref, u_ref, h_ref, hs):
        @pl.when(pl.program_id(1) == 0)
        def _():
            xv = x_ref[...]
            r = lax.rsqrt(jnp.mean(xv * xv, axis=-1, keepdims=True) + RMS_EPS)
            hv = (xv * r * g_ref[...]).astype(BF16)
            hs[...] = hv
            h_ref[...] = hv
        u_ref[...] = _dot(hs[...], w_ref[...])

    return _pcall(
        body, name=name, grid=(s // tm, N_DEV),
        in_specs=[pl.BlockSpec((tm, D_MODEL), lambda i, j: (i, 0)),
                  pl.BlockSpec((1, D_MODEL), lambda i, j: (0, 0)),
                  pl.BlockSpec((None, D_MODEL, COLS_PER_DEV), lambda i, j: (j, 0, 0))],
        out_specs=(pl.BlockSpec((tm, COLS_PER_DEV), lambda i, j: (i, j)),
                   pl.BlockSpec((tm, D_MODEL), lambda i, j: (i, 0))),
        out_shape=(jax.ShapeDtypeStruct((s, N_IN), F32), jax.ShapeDtypeStruct((s, D_MODEL), BF16)),
        scratch_shapes=[pltpu.VMEM((tm, D_MODEL), BF16)],
        compiler_params=_params(("parallel", "arbitrary")),
    )(x, g, w_all)


def _pool_window(vs, t0, t, grp):
    ext = vs[pl.ds(t0, t + POOL_HALO), :]
    s2 = ext + pltpu.roll(ext, 1, 0)
    s4 = s2 + pltpu.roll(s2, 2, 0)
    s8 = s4 + pltpu.roll(s4, 4, 0)
    s16 = s8 + pltpu.roll(s8, 8, 0)
    sel = jnp.where(grp == 0, s2, jnp.where(grp == 1, s4, jnp.where(grp == 2, s8, s16)))
    return sel[POOL_HALO:, :], ext[POOL_HALO:, :]


def _pool_count(t0, t, grp):
    pos = t0 + lax.broadcasted_iota(jnp.int32, (t, 1), 0)
    return jnp.minimum(pos + 1, jnp.left_shift(2, grp)).astype(F32)


def _pool_fwd(u, pool_w, pool_scale, name):
    s = u.shape[0]
    t = min(256, s)

    def body(pv_ref, pg_ref, w_ref, sc_ref, y_ref, vs):
        grp = pl.program_id(0)
        vs[0:POOL_HALO, :] = jnp.zeros((POOL_HALO, LANES), F32)
        vs[POOL_HALO:, :] = pv_ref[...]
        wb = w_ref[...].astype(BF16)
        scale = sc_ref[...]

        def tile(i, carry):
            t0 = pl.multiple_of(i * t, t)
            win, v = _pool_window(vs, t0, t, grp)
            pooled = win / _pool_count(t0, t, grp) - v
            mixed = _dot(pooled.astype(BF16), wb)
            gate = pg_ref[pl.ds(t0, t), :]
            y_ref[pl.ds(t0, t), :] = (mixed * scale * (gate * _sigmoid(gate))).astype(BF16)
            return carry

        lax.fori_loop(0, s // t, tile, 0)

    return _pcall(
        body, name=name, grid=(4,),
        in_specs=[pl.BlockSpec((s, LANES), lambda g: (0, CB_POOL_V + g)),
                  pl.BlockSpec((s, LANES), lambda g: (0, CB_POOL_G + g)),
                  pl.BlockSpec((None, LANES, LANES), lambda g: (g, 0, 0)),
                  pl.BlockSpec((1, LANES), lambda g: (0, g))],
        out_specs=pl.BlockSpec((s, LANES), lambda g: (0, g)),
        out_shape=jax.ShapeDtypeStruct((s, WIDTH), BF16),
        scratch_shapes=[pltpu.VMEM((POOL_HALO + s, LANES), F32)],
        compiler_params=_params(("arbitrary",)),
    )(u, u, pool_w, pool_scale)


def _conv_taps(zs, t0, t):
    ext = zs[pl.ds(t0, t + CONV_HALO), :]
    z0 = ext[CONV_HALO:, :]
    z1 = pltpu.roll(ext, 1, 0)[CONV_HALO:, :]
    z2 = pltpu.roll(ext, 2, 0)[CONV_HALO:, :]
    return z0, z1, z2


def _conv_fwd(u, conv_w, conv_b, name):
    s = u.shape[0]
    t = min(256, s)

    def body(xc_ref, gb_ref, gc_ref, cg_ref, w_ref, b_ref, y_ref, zs):
        zs[0:CONV_HALO, :] = jnp.zeros((CONV_HALO, LANES), F32)
        zs[CONV_HALO:, :] = gc_ref[...] * xc_ref[...]
        w0, w1, w2 = w_ref[0:1, :], w_ref[1:2, :], w_ref[2:3, :]
        bias = b_ref[...]

        def tile(i, carry):
            t0 = pl.multiple_of(i * t, t)
            z0, z1, z2 = _conv_taps(zs, t0, t)
            conv = w0 * z2 + w1 * z1 + w2 * z0
            gate = cg_ref[pl.ds(t0, t), :]
            y = gb_ref[pl.ds(t0, t), :] * (conv + bias) * (gate * _sigmoid(gate))
            y_ref[pl.ds(t0, t), :] = y.astype(BF16)
            return carry

        lax.fori_loop(0, s // t, tile, 0)

    col = lambda base: pl.BlockSpec((s, LANES), lambda j: (0, base + j))
    return _pcall(
        body, name=name, grid=(4,),
        in_specs=[col(CB_CONV_X), col(CB_CONV_GB), col(CB_CONV_GC), col(CB_CONV_G),
                  pl.BlockSpec((3, LANES), lambda j: (0, j)),
                  pl.BlockSpec((1, LANES), lambda j: (0, j))],
        out_specs=pl.BlockSpec((s, LANES), lambda j: (0, j)),
        out_shape=jax.ShapeDtypeStruct((s, WIDTH), BF16),
        scratch_shapes=[pltpu.VMEM((CONV_HALO + s, LANES), F32)],
        compiler_params=_params(("arbitrary",)),
    )(u, u, u, u, conv_w, conv_b)


def _first_head_lanes(rows, width=LANES):
    lane = lax.broadcasted_iota(jnp.int32, (rows, width), 1)
    return jnp.bitwise_and(lane, LANES - 1) < HEAD_DIM


def _stack_heads(x, first):
    zero = jnp.zeros_like(x)
    return jnp.concatenate([jnp.where(first, x, zero), jnp.where(first, zero, x)], axis=0).astype(BF16)


def _causal_mask(tq, tk, copies):
    row = lax.broadcasted_iota(jnp.int32, (tq, tk), 0)
    col = lax.broadcasted_iota(jnp.int32, (tq, tk), 1)
    return jnp.concatenate([col < row] * copies, axis=0)


def _suffix_matrix(tk, inclusive, parts):
    r = lax.broadcasted_iota(jnp.int32, (parts * tk, 2 * tk), 0)
    c = lax.broadcasted_iota(jnp.int32, (parts * tk, 2 * tk), 1)
    r = jnp.bitwise_and(r, tk - 1)
    tri = (r >= c) if inclusive else (r > c)
    return jnp.where(c >= tk, 1.0, jnp.where(tri, 1.0, 0.0)).astype(BF16)


def _suffix_sums(x, m):
    hi, lo = _split_bf16(x)
    return _dot(jnp.concatenate([hi, lo], axis=1), m)


def _sb_log_terms(z, mask, m_strict):
    ls = jnp.minimum(z, 0.0) - jnp.log(1.0 + jnp.exp2(jnp.abs(z) * -LOG2E))
    lk = ls - z
    if mask is not None:
        lk = jnp.where(mask, lk, 0.0)
    return ls, _dot(lk.astype(BF16), m_strict)


SB_PAIRS = 4


def _pair_lanes(a):
    return slice(a * LANES, (a + 1) * LANES)


def _sb_fwd(u, name, xchg=None):
    s = u.shape[0]
    tq = tk = min(128, s)
    pairs = SB_PAIRS
    width = pairs * LANES
    rows = 2 * pairs * tq
    x_arrs, x_gather = xchg if xchg else ((), True)
    n_x = len(x_arrs)
    grid = (4 // pairs, s // tq)

    def body(*refs):
        q_ref, k_ref, v_ref, g_ref = refs[:4]
        x_in, refs = refs[4:4 + n_x], refs[4 + n_x:]
        o_ref, y_ref = refs[:2]
        x_out, refs = refs[2:2 + n_x], refs[2 + n_x:]
        kbf, vst, z_s, ell_s, carry_s = refs[:5]
        x_sems = refs[5:]
        i = pl.program_id(1)
        if n_x:
            @pl.when((pl.program_id(0) == 0) & (i == 0))
            def _():
                _exchange_start(x_in, x_out, x_sems, x_gather)

        @pl.when(i == 0)
        def _():
            kbf[...] = k_ref[...].astype(BF16)
            first_s = _first_head_lanes(s, width)
            vf = v_ref[...]
            vst[0] = jnp.where(first_s, vf, 0.0).astype(BF16)
            vst[1] = jnp.where(first_s, 0.0, vf).astype(BF16)

        first = _first_head_lanes(tq)
        mask = _causal_mask(tq, tk, 2 * pairs)
        m_strict = _suffix_matrix(tk, False, 1)
        qcat = jnp.concatenate([_stack_heads(q_ref[:, _pair_lanes(a)] * SB_SCALE, first) for a in range(pairs)],
                               axis=0)

        def scores(b):
            off = pl.multiple_of(jnp.maximum(b, 0) * tk, tk)
            z_s[...] = jnp.concatenate(
                [_dot_nt(qcat[a * 2 * tq:(a + 1) * 2 * tq], kbf[pl.ds(off, tk), _pair_lanes(a)])
                 for a in range(pairs)], axis=0)

        def log_weights(m):
            ls, cs = _sb_log_terms(z_s[...], m, m_strict)
            carry = carry_s[...]
            ell_s[...] = ls + cs[:, :tk] + carry
            carry_s[...] = carry + cs[:, tk:]

        def consume(b, accs, m):
            w = jnp.exp(ell_s[...])
            if m is not None:
                w = jnp.where(m, w, 0.0)
            wb = w.astype(BF16)
            off = pl.multiple_of(b * tk, tk)
            new = []
            for a in range(pairs):
                r0 = a * 2 * tq
                wcat = jnp.concatenate([wb[r0:r0 + tq], wb[r0 + tq:r0 + 2 * tq]], axis=1)
                vcat = jnp.concatenate([vst[0, pl.ds(off, tk), _pair_lanes(a)], vst[1, pl.ds(off, tk), _pair_lanes(a)]],
                                       axis=0)
                new.append(accs[a] + _dot(wcat, vcat))
            return tuple(new)

        carry_s[...] = jnp.zeros((rows, tk), F32)
        scores(i)
        log_weights(mask)
        scores(i - 1)
        accs = consume(i, tuple(jnp.zeros((tq, LANES), F32) for _ in range(pairs)), mask)
        log_weights(None)
        scores(i - 2)

        def step(n, accs):
            accs = consume(i - n, accs, None)
            log_weights(None)
            scores(i - n - 2)
            return accs

        accs = lax.fori_loop(1, i + 1, step, accs)
        o = jnp.concatenate(accs, axis=1)
        o_ref[...] = o
        gate = g_ref[...]
        y_ref[...] = (o * (gate * _sigmoid(gate))).astype(BF16)
        if n_x:
            @pl.when((pl.program_id(0) == grid[0] - 1) & (i == grid[1] - 1))
            def _():
                _exchange_wait(x_in, x_out, x_sems, x_gather)

    base = lambda cb: cb // pairs
    qblk = lambda cb: pl.BlockSpec((tq, width), lambda p, i: (i, base(cb) + p))
    full = lambda cb: pl.BlockSpec((s, width), lambda p, i: (0, base(cb) + p), pipeline_mode=pl.Buffered(1))
    state = pltpu.VMEM((rows, tk), F32)
    return _pcall(
        body, name=name, grid=grid,
        in_specs=[qblk(CB_SB_Q), full(CB_SB_K), full(CB_SB_V), qblk(CB_SB_G)] + [ANY_SPEC] * n_x,
        out_specs=(qblk(0), qblk(0)) + (ANY_SPEC,) * n_x,
        out_shape=(jax.ShapeDtypeStruct((s, WIDTH), F32), jax.ShapeDtypeStruct((s, WIDTH), BF16))
        + tuple(_exchange_out_shapes(x_arrs, x_gather)),
        scratch_shapes=[pltpu.VMEM((s, width), BF16), pltpu.VMEM((2, s, width), BF16), state, state, state]
        + (_exchange_sems(n_x) if n_x else []),
        compiler_params=_params(("arbitrary", "arbitrary")),
    )(u, u, u, u, *x_arrs)


def _merge_out_fwd(y_pool, y_conv, y_sb, u, wb_all, wo_all, x, g_post, layer, name, target=None):
    s = x.shape[0]
    tm = min(512, s)
    n_tiles = s // tm
    with_loss = target is not None

    def body(yp, yc, ys, m0, m1, m2, wb_ref, wo_ref, x_ref, g_ref, *rest):
        merged = jnp.zeros((tm, D_MODEL), F32)
        for n, (y_ref, m_ref) in enumerate(((yp, m0), (yc, m1), (ys, m2))):
            merged = merged + _sigmoid(m_ref[...]) * _dot(y_ref[...], wb_ref[n])
        mb = merged.astype(BF16)
        pre = _dot(mb, wo_ref[...].reshape(D_MODEL, D_MODEL))
        r = lax.rsqrt(jnp.mean(pre * pre, axis=-1, keepdims=True) + RMS_EPS)
        y = x_ref[...] + pre * r * g_ref[...]
        if not with_loss:
            out_ref, merged_ref, pre_ref = rest
            out_ref[...] = y
        else:
            t_ref, out_ref, merged_ref, pre_ref, loss_ref, acc = rest
            i = pl.program_id(0)

            @pl.when(i == 0)
            def _():
                acc[...] = jnp.zeros_like(acc)
            err = y - t_ref[...]
            out_ref[...] = err / D_MODEL
            acc[...] += jnp.sum(err * err, axis=0, keepdims=True)

            @pl.when(i == n_tiles - 1)
            def _():
                total = jnp.sum(acc[...], axis=1, keepdims=True) * (0.5 / D_MODEL)
                loss_ref[...] = jnp.broadcast_to(total, (1, LANES))
        merged_ref[...] = mb
        pre_ref[...] = pre

    rows = lambda w: pl.BlockSpec((tm, w), lambda i: (i, 0))
    merge = lambda n: pl.BlockSpec((tm, D_MODEL), lambda i: (i, MERGE_BLOCK_1024 + n))
    out_specs = (rows(D_MODEL), rows(D_MODEL), rows(D_MODEL))
    out_shape = (jax.ShapeDtypeStruct((s, D_MODEL), F32), jax.ShapeDtypeStruct((s, D_MODEL), BF16),
                 jax.ShapeDtypeStruct((s, D_MODEL), F32))
    if with_loss:
        out_specs += (pl.BlockSpec((1, LANES), lambda i: (0, 0)),)
        out_shape += (jax.ShapeDtypeStruct((1, LANES), F32),)
    return _pcall(
        body, name=name, grid=(n_tiles,),
        in_specs=[rows(WIDTH), rows(WIDTH), rows(WIDTH), merge(0), merge(1), merge(2),
                  pl.BlockSpec((None, 3, WIDTH, D_MODEL), lambda i: (layer, 0, 0, 0)),
                  pl.BlockSpec((N_DEV, None, D_MODEL // N_DEV, D_MODEL), lambda i: (0, layer, 0, 0)),
                  rows(D_MODEL), pl.BlockSpec((1, D_MODEL), lambda i: (0, 0))] + ([rows(D_MODEL)] if with_loss else []),
        out_specs=out_specs, out_shape=out_shape,
        scratch_shapes=[pltpu.VMEM((1, D_MODEL), F32)] if with_loss else [],
        compiler_params=_params(("arbitrary",)),
    )(y_pool, y_conv, y_sb, u, u, u, wb_all, wo_all, x, g_post, *([target] if with_loss else []))


def _out_proj_bwd(dy, pre, g_post, merged, wo_all, layer, name):
    s = dy.shape[0]
    tm = min(512, s)
    n_tiles = s // tm

    def body(dy_ref, pre_ref, g_ref, mg_ref, wo_ref, dm_ref, dwo_ref, dg_ref, acc):
        i = pl.program_id(0)

        @pl.when(i == 0)
        def _():
            acc[...] = jnp.zeros_like(acc)
            dg_ref[...] = jnp.zeros_like(dg_ref)
        dyv, pre_v = dy_ref[...], pre_ref[...]
        r = lax.rsqrt(jnp.mean(pre_v * pre_v, axis=-1, keepdims=True) + RMS_EPS)
        dg_ref[...] += jnp.sum(dyv * pre_v * r, axis=0, keepdims=True)
        a = dyv * g_ref[...]
        dpre = r * a - pre_v * (r * r * r) * jnp.mean(a * pre_v, axis=-1, keepdims=True)
        db = dpre.astype(BF16)
        acc[...] += _dot_tn(mg_ref[...], db)
        dm_ref[...] = _dot_nt(db, wo_ref[...].reshape(D_MODEL, D_MODEL))

        @pl.when(i == n_tiles - 1)
        def _():
            dwo_ref[...] = acc[...].astype(BF16)

    rows = lambda: pl.BlockSpec((tm, D_MODEL), lambda i: (i, 0))
    return _pcall(
        body, name=name, grid=(n_tiles,),
        in_specs=[rows(), rows(), pl.BlockSpec((1, D_MODEL), lambda i: (0, 0)), rows(),
                  pl.BlockSpec((N_DEV, None, D_MODEL // N_DEV, D_MODEL), lambda i: (0, layer, 0, 0))],
        out_specs=(rows(), pl.BlockSpec((D_MODEL, D_MODEL), lambda i: (0, 0)),
                   pl.BlockSpec((1, D_MODEL), lambda i: (0, 0))),
        out_shape=(jax.ShapeDtypeStruct((s, D_MODEL), F32), jax.ShapeDtypeStruct((D_MODEL, D_MODEL), BF16),
                   jax.ShapeDtypeStruct((1, D_MODEL), F32)),
        scratch_shapes=[pltpu.VMEM((D_MODEL, D_MODEL), F32)],
        compiler_params=_params(("arbitrary",)),
    )(dy, pre, g_post, merged, wo_all)


def _merge_bwd(dmerged, y_pool, y_conv, y_sb, u, wb_all, layer, name):
    s = dmerged.shape[0]
    tm = min(512, s)
    n_tiles = s // tm
    cols = D_MODEL // N_DEV

    def body(dm_ref, yp, yc, ys, m0, m1, m2, wb_ref, du_ref, dyp, dyc, dys, dwb_ref, acc):
        i = pl.program_id(0)

        @pl.when(i == 0)
        def _():
            acc[...] = jnp.zeros_like(acc)
        dm = dm_ref[...]
        for n, (y_ref, m_ref, dy_ref) in enumerate(((yp, m0, dyp), (yc, m1, dyc), (ys, m2, dys))):
            yv = y_ref[...]
            wb = wb_ref[n]
            gate = _sigmoid(m_ref[...])
            proj = _dot(yv, wb)
            dgate = (dm * proj * gate * (1.0 - gate)).astype(BF16)
            du_ref[2 * n] = dgate[:, :WIDTH]
            du_ref[2 * n + 1] = dgate[:, WIDTH:]
            dproj = (dm * gate).astype(BF16)
            acc[n] += _dot_tn(yv, dproj)
            dy_ref[...] = _dot_nt(dproj, wb)

        @pl.when(i == n_tiles - 1)
        def _():
            for j in range(N_DEV):
                for n in range(3):
                    dwb_ref[j, n] = acc[n, :, j * cols:(j + 1) * cols].astype(BF16)

    rows = lambda w: pl.BlockSpec((tm, w), lambda i: (i, 0))
    merge = lambda n: pl.BlockSpec((tm, D_MODEL), lambda i: (i, MERGE_BLOCK_1024 + n))
    return _pcall(
        body, name=name, grid=(n_tiles,),
        in_specs=[rows(D_MODEL), rows(WIDTH), rows(WIDTH), rows(WIDTH), merge(0), merge(1), merge(2),
                  pl.BlockSpec((None, 3, WIDTH, D_MODEL), lambda i: (layer, 0, 0, 0))],
        out_specs=(pl.BlockSpec((DU_MERGE[1], tm, WIDTH), lambda i: (DU_MERGE[0] // DU_MERGE[1], i, 0)),
                   rows(WIDTH), rows(WIDTH), rows(WIDTH),
                   pl.BlockSpec((N_DEV, 3, WIDTH, cols), lambda i: (0, 0, 0, 0))),
        out_shape=(jax.ShapeDtypeStruct((DU_PIECES, s, WIDTH), BF16),
                   jax.ShapeDtypeStruct((s, WIDTH), F32), jax.ShapeDtypeStruct((s, WIDTH), F32),
                   jax.ShapeDtypeStruct((s, WIDTH), F32),
                   jax.ShapeDtypeStruct((N_DEV, 3, WIDTH, cols), BF16)),
        scratch_shapes=[pltpu.VMEM((3, WIDTH, D_MODEL), F32)],
        compiler_params=_params(("arbitrary",)),
    )(dmerged, y_pool, y_conv, y_sb, u, u, u, wb_all)


def _sb_bwd(u, o, dys, du, name, xchg=None):
    s = u.shape[0]
    tq = tk = min(128, s)
    pairs = SB_PAIRS
    width = pairs * LANES
    assert width == WIDTH
    rows = 2 * pairs * tq
    pair_rows = lambda a: slice(a * 2 * tq, (a + 1) * 2 * tq)

    x_arrs, x_gather = xchg if xchg else ((), True)
    n_x = len(x_arrs)
    grid = (4 // pairs, s // tq)

    def body(*refs):
        q_ref, k_ref, v_ref, g_ref, o_ref, dys_ref = refs[:6]
        x_in, refs = refs[7:7 + n_x], refs[7 + n_x:]
        du_ref, dk_ref, dv_ref = refs[:3]
        dq_ref, dg_ref = du_ref.at[0], du_ref.at[1]
        x_out, refs = refs[3:3 + n_x], refs[3 + n_x:]
        kbf, vbf, kst, z_s, ell_s, ls_s, cl_s, wb_s, g_s, bef_s, cg_s, beta_s = refs[:12]
        x_sems = refs[12:]
        i = pl.program_id(1)
        if n_x:
            @pl.when((pl.program_id(0) == 0) & (i == 0))
            def _():
                _exchange_start(x_in, x_out, x_sems, x_gather)

        @pl.when(i == 0)
        def _():
            dk_ref[...] = jnp.zeros_like(dk_ref)
            dv_ref[...] = jnp.zeros_like(dv_ref)
            kf = k_ref[...]
            kbf[...] = kf.astype(BF16)
            vbf[...] = v_ref[...].astype(BF16)
            first_s = _first_head_lanes(s, width)
            kst[0] = jnp.where(first_s, kf, 0.0).astype(BF16)
            kst[1] = jnp.where(first_s, 0.0, kf).astype(BF16)

        first = _first_head_lanes(tq)
        mask = _causal_mask(tq, tk, 2 * pairs)
        m_strict = _suffix_matrix(tk, False, 1)
        m_incl = _suffix_matrix(tk, True, 2)

        gate = g_ref[...]
        sg = _sigmoid(gate)
        dy = dys_ref[...]
        ov = o_ref[...]
        dg_ref[...] = (dy * ov * (sg * (1.0 + gate * (1.0 - sg)))).astype(BF16)
        do = (dy * (gate * sg)).astype(BF16)
        prod = do.astype(F32) * ov
        row_sum = lambda v: jnp.broadcast_to(jnp.sum(v, axis=1, keepdims=True), (tq, tk))
        dsum, docat, qcat = [], [], []
        for a in range(pairs):
            pa = prod[:, _pair_lanes(a)]
            dsum += [row_sum(jnp.where(first, pa, 0.0)), row_sum(jnp.where(first, 0.0, pa))]
            docat.append(_stack_heads(do[:, _pair_lanes(a)], first))
            qcat.append(_stack_heads(q_ref[:, _pair_lanes(a)] * SB_SCALE, first))
        dsum = jnp.concatenate(dsum, axis=0)

        def block_start(b):
            return pl.multiple_of(jnp.maximum(b, 0) * tk, tk)

        def scores(b):
            off = block_start(b)
            z_s[...] = jnp.concatenate([_dot_nt(qcat[a], kbf[pl.ds(off, tk), _pair_lanes(a)]) for a in range(pairs)],
                                       axis=0)

        def log_weights(m):
            ls, cs = _sb_log_terms(z_s[...], m, m_strict)
            cl = cl_s[...]
            ell_s[...] = ls + cs[:, :tk] + cl
            cl_s[...] = cl + cs[:, tk:]
            ls_s[...] = ls

        def weights(b, m):
            off = block_start(b)
            dwt = jnp.concatenate([_dot_nt(docat[a], vbf[pl.ds(off, tk), _pair_lanes(a)]) for a in range(pairs)],
                                  axis=0)
            w = jnp.exp(ell_s[...])
            if m is not None:
                w = jnp.where(m, w, 0.0)
            wb = w.astype(BF16)
            g = dwt * wb.astype(F32)
            gs = _suffix_sums(g, m_incl)
            cg = cg_s[...]
            beta = jnp.exp(ls_s[...])
            wb_s[...] = wb
            beta_s[...] = beta
            g_s[...] = g * (1.0 - beta)
            bef_s[...] = gs[:, :tk] + cg
            cg_s[...] = cg + gs[:, tk:]

        def grads(b, dqs, m):
            dz = g_s[...] - beta_s[...] * (dsum - bef_s[...])
            if m is not None:
                dz = jnp.where(m, dz, 0.0)
            dzb = dz.astype(BF16)
            wb = wb_s[...]
            off = pl.multiple_of(b * tk, tk)
            new = []
            for a in range(pairs):
                r0 = a * 2 * tq
                kcat = jnp.concatenate([kst[0, pl.ds(off, tk), _pair_lanes(a)], kst[1, pl.ds(off, tk), _pair_lanes(a)]],
                                       axis=0)
                new.append(dqs[a] + _dot(jnp.concatenate([dzb[r0:r0 + tq], dzb[r0 + tq:r0 + 2 * tq]], axis=1), kcat))
                dk_ref[pl.ds(off, tk), _pair_lanes(a)] += _dot_tn(dzb[pair_rows(a)], qcat[a])
                dv_ref[pl.ds(off, tk), _pair_lanes(a)] += _dot_tn(wb[pair_rows(a)], docat[a])
            return tuple(new)

        zero = jnp.zeros((rows, tk), F32)
        cl_s[...] = zero
        cg_s[...] = zero
        scores(i)
        log_weights(mask)
        scores(i - 1)
        weights(i, mask)
        log_weights(None)
        scores(i - 2)
        dqs = grads(i, tuple(jnp.zeros((tq, LANES), F32) for _ in range(pairs)), mask)
        weights(i - 1, None)
        log_weights(None)
        scores(i - 3)

        def step(n, dqs):
            dqs = grads(i - n, dqs, None)
            weights(i - n - 1, None)
            log_weights(None)
            scores(i - n - 3)
            return dqs

        dqs = lax.fori_loop(1, i + 1, step, dqs)
        dq_ref[...] = (jnp.concatenate(dqs, axis=1) * SB_SCALE).astype(BF16)
        if n_x:
            @pl.when((pl.program_id(0) == grid[0] - 1) & (i == grid[1] - 1))
            def _():
                _exchange_wait(x_in, x_out, x_sems, x_gather)

    base = lambda cb: cb // pairs
    qblk = lambda cb: pl.BlockSpec((tq, width), lambda p, i: (i, base(cb) + p))
    full = lambda cb: pl.BlockSpec((s, width), lambda p, i: (0, base(cb) + p), pipeline_mode=pl.Buffered(1))
    state = pltpu.VMEM((rows, tk), F32)
    return _pcall(
        body, name=name, grid=grid,
        in_specs=[qblk(CB_SB_Q), full(CB_SB_K), full(CB_SB_V), qblk(CB_SB_G), qblk(0), qblk(0), ANY_SPEC]
        + [ANY_SPEC] * n_x,
        out_specs=(pl.BlockSpec((DU_SB_QG[1], tq, WIDTH), lambda p, i: (DU_SB_QG[0] // DU_SB_QG[1], i, 0)),
                   full(0), full(0)) + (ANY_SPEC,) * n_x,
        out_shape=(jax.ShapeDtypeStruct(du.shape, du.dtype), jax.ShapeDtypeStruct((s, WIDTH), F32),
                   jax.ShapeDtypeStruct((s, WIDTH), F32)) + tuple(_exchange_out_shapes(x_arrs, x_gather)),
        input_output_aliases={6: 0},
        scratch_shapes=[pltpu.VMEM((s, width), BF16), pltpu.VMEM((s, width), BF16), pltpu.VMEM((2, s, width), BF16),
                        state, state, state, state, pltpu.VMEM((rows, tk), BF16),
                        state, state, state, state] + (_exchange_sems(n_x) if n_x else []),
        compiler_params=_params(("arbitrary", "arbitrary")),
    )(u, u, u, u, o, dys, du, *x_arrs)


def _conv_bwd(u, conv_w, conv_b, dyc, du, name):
    s = u.shape[0]
    t = min(256, s)
    n_tiles = s // t

    def body(xc_ref, gb_ref, gc_ref, cg_ref, w_ref, b_ref, dy_ref, du_in, du_ref, dw_ref, db_ref, zs, ds):
        dxc_ref, dgb_ref, dgc_ref, dcg_ref = (du_ref.at[p] for p in range(4))
        zs[0:CONV_HALO, :] = jnp.zeros((CONV_HALO, LANES), F32)
        zs[CONV_HALO:, :] = gc_ref[...] * xc_ref[...]
        ds[s:, :] = jnp.zeros((CONV_HALO, LANES), F32)
        w0, w1, w2 = w_ref[0:1, :], w_ref[1:2, :], w_ref[2:3, :]
        bias = b_ref[...]

        def first(i, sums):
            t0 = pl.multiple_of(i * t, t)
            z0, z1, z2 = _conv_taps(zs, t0, t)
            pre = w0 * z2 + w1 * z1 + w2 * z0 + bias
            gate = cg_ref[pl.ds(t0, t), :]
            sg = _sigmoid(gate)
            gb = gb_ref[pl.ds(t0, t), :]
            dy = dy_ref[pl.ds(t0, t), :]
            dcg_ref[pl.ds(t0, t), :] = (dy * gb * pre * (sg * (1.0 + gate * (1.0 - sg)))).astype(BF16)
            dgb_ref[pl.ds(t0, t), :] = (dy * pre * (gate * sg)).astype(BF16)
            dc = dy * gb * (gate * sg)
            ds[pl.ds(t0, t), :] = dc
            red = lambda v: jnp.sum(v, axis=0, keepdims=True)
            return (sums[0] + red(dc * z2), sums[1] + red(dc * z1), sums[2] + red(dc * z0), sums[3] + red(dc))

        zrow = jnp.zeros((1, LANES), F32)
        sw0, sw1, sw2, sb = lax.fori_loop(0, n_tiles, first, (zrow, zrow, zrow, zrow))
        dw_ref[0:1, :] = sw0
        dw_ref[1:2, :] = sw1
        dw_ref[2:3, :] = sw2
        db_ref[...] = sb

        def second(i, carry):
            t0 = pl.multiple_of(i * t, t)
            ext = ds[pl.ds(t0, t + CONV_HALO), :]
            n = t + CONV_HALO
            d0 = ext[:t, :]
            d1 = pltpu.roll(ext, n - 1, 0)[:t, :]
            d2 = pltpu.roll(ext, n - 2, 0)[:t, :]
            dz = w2 * d0 + w1 * d1 + w0 * d2
            dgc_ref[pl.ds(t0, t), :] = (dz * xc_ref[pl.ds(t0, t), :]).astype(BF16)
            dxc_ref[pl.ds(t0, t), :] = (dz * gc_ref[pl.ds(t0, t), :]).astype(BF16)
            return carry

        lax.fori_loop(0, n_tiles, second, 0)

    col = lambda base: pl.BlockSpec((s, LANES), lambda j: (0, base + j))
    first, count = DU_CONV
    return _pcall(
        body, name=name, grid=(4,),
        in_specs=[col(CB_CONV_X), col(CB_CONV_GB), col(CB_CONV_GC), col(CB_CONV_G),
                  pl.BlockSpec((3, LANES), lambda j: (0, j)), pl.BlockSpec((1, LANES), lambda j: (0, j)), col(0),
                  ANY_SPEC],
        out_specs=(pl.BlockSpec((count, s, LANES), lambda j: (first // count, 0, j)),
                   pl.BlockSpec((3, LANES), lambda j: (0, j)), pl.BlockSpec((1, LANES), lambda j: (0, j))),
        out_shape=(jax.ShapeDtypeStruct(du.shape, du.dtype),
                   jax.ShapeDtypeStruct((3, WIDTH), F32), jax.ShapeDtypeStruct((1, WIDTH), F32)),
        scratch_shapes=[pltpu.VMEM((CONV_HALO + s, LANES), F32), pltpu.VMEM((s + CONV_HALO, LANES), F32)],
        input_output_aliases={7: 0},
        compiler_params=_params(("arbitrary",)),
    )(u, u, u, u, conv_w, conv_b, dyc, du)


def _pool_bwd(u, pool_w, pool_scale, dyp, du, name):
    s = u.shape[0]
    t = min(256, s)
    n_tiles = s // t

    def body(pv_ref, pg_ref, w_ref, sc_ref, dy_ref, du_in, du_ref, dw_ref, dsc_ref, vs, es, dps):
        dpv_ref, dpg_ref = du_ref.at[0], du_ref.at[1]
        grp = pl.program_id(0)
        vs[0:POOL_HALO, :] = jnp.zeros((POOL_HALO, LANES), F32)
        vs[POOL_HALO:, :] = pv_ref[...]
        es[s:, :] = jnp.zeros((POOL_HALO, LANES), F32)
        wb = w_ref[...].astype(BF16)
        scale = sc_ref[...]

        def first(i, sums):
            dw, dsc = sums
            t0 = pl.multiple_of(i * t, t)
            win, v = _pool_window(vs, t0, t, grp)
            cnt = _pool_count(t0, t, grp)
            pb = (win / cnt - v).astype(BF16)
            mixed = _dot(pb, wb)
            gate = pg_ref[pl.ds(t0, t), :]
            sg = _sigmoid(gate)
            dy = dy_ref[pl.ds(t0, t), :]
            dpg_ref[pl.ds(t0, t), :] = (dy * (mixed * scale) * (sg * (1.0 + gate * (1.0 - sg)))).astype(BF16)
            dms = dy * (gate * sg)
            dsc = dsc + jnp.sum(dms * mixed, axis=0, keepdims=True)
            dmb = (dms * scale).astype(BF16)
            dw = dw + _dot_tn(pb, dmb)
            dpooled = _dot_nt(dmb, wb)
            dps[pl.ds(t0, t), :] = dpooled
            es[pl.ds(t0, t), :] = dpooled / cnt
            return dw, dsc

        dw, dsc = lax.fori_loop(0, n_tiles, first, (jnp.zeros((LANES, LANES), F32), jnp.zeros((1, LANES), F32)))
        dw_ref[...] = dw
        dsc_ref[...] = dsc

        def second(i, carry):
            t0 = pl.multiple_of(i * t, t)
            ext = es[pl.ds(t0, t + POOL_HALO), :]
            n = t + POOL_HALO
            f2 = ext + pltpu.roll(ext, n - 1, 0)
            f4 = f2 + pltpu.roll(f2, n - 2, 0)
            f8 = f4 + pltpu.roll(f4, n - 4, 0)
            f16 = f8 + pltpu.roll(f8, n - 8, 0)
            sel = jnp.where(grp == 0, f2, jnp.where(grp == 1, f4, jnp.where(grp == 2, f8, f16)))
            dpv_ref[pl.ds(t0, t), :] = (sel[:t, :] - dps[pl.ds(t0, t), :]).astype(BF16)
            return carry

        lax.fori_loop(0, n_tiles, second, 0)

    col = lambda base: pl.BlockSpec((s, LANES), lambda g: (0, base + g))
    first, count = DU_POOL
    return _pcall(
        body, name=name, grid=(4,),
        in_specs=[col(CB_POOL_V), col(CB_POOL_G), pl.BlockSpec((None, LANES, LANES), lambda g: (g, 0, 0)),
                  pl.BlockSpec((1, LANES), lambda g: (0, g)), col(0), ANY_SPEC],
        out_specs=(pl.BlockSpec((count, s, LANES), lambda g: (first // count, 0, g)),
                   pl.BlockSpec((None, LANES, LANES), lambda g: (g, 0, 0)),
                   pl.BlockSpec((1, LANES), lambda g: (0, g))),
        out_shape=(jax.ShapeDtypeStruct(du.shape, du.dtype),
                   jax.ShapeDtypeStruct((4, LANES, LANES), F32), jax.ShapeDtypeStruct((1, WIDTH), F32)),
        scratch_shapes=[pltpu.VMEM((POOL_HALO + s, LANES), F32), pltpu.VMEM((s + POOL_HALO, LANES), F32),
                        pltpu.VMEM((s, LANES), F32)],
        input_output_aliases={5: 0},
        compiler_params=_params(("arbitrary",)),
    )(u, u, pool_w, pool_scale, dyp, du)


def _in_proj_bwd_x(du, w_all, x, g_pre, dy, name):
    s = x.shape[0]
    tm = min(1024, s)
    grid = (s // tm, N_DEV)

    def body(dua_ref, dub_ref, w_ref, x_ref, g_ref, dy_ref, dx_ref, dg_ref, acc):
        i, k = pl.program_id(0), pl.program_id(1)

        @pl.when(k == 0)
        def _():
            acc[...] = jnp.zeros_like(acc)

        @pl.when((k == 0) & (i == 0))
        def _():
            dg_ref[...] = jnp.zeros_like(dg_ref)
        acc[...] += _dot_nt(jnp.concatenate([dua_ref[...], dub_ref[...]], axis=1), w_ref[...])

        @pl.when(k == N_DEV - 1)
        def _():
            dh, xv = acc[...], x_ref[...]
            r = lax.rsqrt(jnp.mean(xv * xv, axis=-1, keepdims=True) + RMS_EPS)
            dg_ref[...] += jnp.sum(dh * xv * r, axis=0, keepdims=True)
            a = dh * g_ref[...]
            dx_ref[...] = dy_ref[...] + r * a - xv * (r * r * r) * jnp.mean(a * xv, axis=-1, keepdims=True)

    rows = lambda: pl.BlockSpec((tm, D_MODEL), lambda i, k: (i, 0))
    vec = lambda: pl.BlockSpec((1, D_MODEL), lambda i, k: (0, 0))
    piece = lambda half: pl.BlockSpec((None, tm, WIDTH), lambda i, k: (_du_pieces_of_block(k)[half], i, 0))
    return _pcall(
        body, name=name, grid=grid,
        in_specs=[piece(0), piece(1), pl.BlockSpec((None, D_MODEL, COLS_PER_DEV), lambda i, k: (k, 0, 0)),
                  rows(), vec(), rows()],
        out_specs=(rows(), vec()),
        out_shape=(jax.ShapeDtypeStruct((s, D_MODEL), F32), jax.ShapeDtypeStruct((1, D_MODEL), F32)),
        scratch_shapes=[pltpu.VMEM((tm, D_MODEL), F32)],
        compiler_params=_params(("arbitrary", "arbitrary")),
    )(du, du, w_all, x, g_pre, dy)


ROW_OFFSETS = (6, 7, 2, 4, 3, 5, 0, 1)


def _in_proj_bwd_send(h, du, w_all, x, g_pre, dy, name):
    s = x.shape[0]
    tk = s // N_DEV
    tm = min(1024, s)
    n_i = s // tm
    grid = (N_DEV + n_i, N_DEV)
    last = N_DEV - 1
    def offset(row):
        return functools.reduce(lambda acc, rn: jnp.where(row == rn[0], rn[1], acc), enumerate(ROW_OFFSETS), 0)

    def body(me_ref, h_ref, duwa_ref, duwb_ref, duxa_ref, duxb_ref, w_ref, x_ref, g_ref, dy_ref,
             dx_ref, dg_ref, recv_ref, part_ref, acc_w, stage, acc_x, send_sems, recv_sems, park_sems):
        r, k = pl.program_id(0), pl.program_id(1)
        x_, y_, c_ = lax.axis_index("x"), lax.axis_index("y"), lax.axis_index("c")
        me = 4 * x_ + 2 * y_ + c_
        flip = lambda v, bit: 1 - v if bit else v
        peer = lambda n: (flip(x_, (n >> 2) & 1), flip(y_, (n >> 1) & 1), flip(c_, n & 1))

        def park(row):
            n = ROW_OFFSETS[row]
            dst = recv_ref.at[me] if n == 0 else part_ref.at[n]
            return pltpu.make_async_copy(stage.at[row % 2], dst, park_sems.at[row % 2])

        def send(n, landing=False):
            px, py, pc = peer(n)
            dst = recv_ref.at[4 * px + 2 * py + pc] if landing else recv_ref.at[me]
            return pltpu.make_async_remote_copy(
                src_ref=part_ref.at[n], dst_ref=dst, send_sem=send_sems.at[n], recv_sem=recv_sems.at[n],
                device_id=(px, py, pc), device_id_type=pl.DeviceIdType.MESH)

        def parked(row):
            park(row).wait()
            if ROW_OFFSETS[row] >= 1:
                send(ROW_OFFSETS[row]).start()

        @pl.when(r < N_DEV)
        def _():
            @pl.when(k == 0)
            def _():
                acc_w[...] = jnp.zeros_like(acc_w)
            acc_w[...] += _dot_tn(h_ref[...], jnp.concatenate([duwa_ref[...], duwb_ref[...]], axis=1))

            for row in range(N_DEV):
                @pl.when((k == last) & (r == row))
                def _():
                    if row >= 1:
                        parked(row - 1)
                    stage[row % 2] = acc_w[...].astype(BF16)
                    park(row).start()

        @pl.when(r >= N_DEV)
        def _():
            @pl.when(k == 0)
            def _():
                acc_x[...] = jnp.zeros_like(acc_x)

            @pl.when((k == 0) & (r == N_DEV))
            def _():
                dg_ref[...] = jnp.zeros_like(dg_ref)
                parked(last)
            acc_x[...] += _dot_nt(jnp.concatenate([duxa_ref[...], duxb_ref[...]], axis=1), w_ref[...])

            @pl.when(k == last)
            def _():
                dh, xv = acc_x[...], x_ref[...]
                rs = lax.rsqrt(jnp.mean(xv * xv, axis=-1, keepdims=True) + RMS_EPS)
                dg_ref[...] += jnp.sum(dh * xv * rs, axis=0, keepdims=True)
                a = dh * g_ref[...]
                dx_ref[...] = dy_ref[...] + rs * a - xv * (rs * rs * rs) * jnp.mean(a * xv, axis=-1, keepdims=True)

        @pl.when((r == grid[0] - 1) & (k == last))
        def _():
            for n in range(1, N_DEV):
                send(n).wait_send()
            for n in range(1, N_DEV):
                send(n, landing=True).wait_recv()

    in_w = lambda r: r < N_DEV
    row_x = lambda r: jnp.maximum(r - N_DEV, 0)
    rows = lambda: pl.BlockSpec((tm, D_MODEL), lambda r, k, me: (row_x(r), 0))
    vec = lambda: pl.BlockSpec((1, D_MODEL), lambda r, k, me: (0, 0))
    block_w = lambda r, me: jnp.bitwise_xor(me[0], offset(jnp.minimum(r, last)))
    block_x = lambda r, k: jnp.where(in_w(r), 0, k)
    piece_w = lambda half: pl.BlockSpec(
        (None, tk, WIDTH), lambda r, k, me: (_du_pieces_of_block(block_w(r, me))[half], jnp.where(in_w(r), k, last), 0))
    piece_x = lambda half: pl.BlockSpec(
        (None, tm, WIDTH), lambda r, k, me: (_du_pieces_of_block(block_x(r, k))[half], row_x(r), 0))
    grid_spec = pltpu.PrefetchScalarGridSpec(
        num_scalar_prefetch=1, grid=grid,
        in_specs=[pl.BlockSpec((tk, D_MODEL), lambda r, k, me: (jnp.where(in_w(r), k, last), 0)),
                  piece_w(0), piece_w(1), piece_x(0), piece_x(1),
                  pl.BlockSpec((None, D_MODEL, COLS_PER_DEV), lambda r, k, me: (block_x(r, k), 0, 0)),
                  rows(), vec(), rows()],
        out_specs=(rows(), vec(), ANY_SPEC, ANY_SPEC),
        scratch_shapes=[pltpu.VMEM((D_MODEL, COLS_PER_DEV), F32), pltpu.VMEM((2, D_MODEL, COLS_PER_DEV), BF16),
                        pltpu.VMEM((tm, D_MODEL), F32), pltpu.SemaphoreType.DMA((N_DEV,)),
                        pltpu.SemaphoreType.DMA((N_DEV,)), pltpu.SemaphoreType.DMA((2,))])
    me = 4 * lax.axis_index("x") + 2 * lax.axis_index("y") + lax.axis_index("c")
    blocks = jax.ShapeDtypeStruct((N_DEV, D_MODEL, COLS_PER_DEV), BF16)
    dx, dg, received, _ = _pcall(
        body, name=name, grid_spec=grid_spec,
        out_shape=(jax.ShapeDtypeStruct((s, D_MODEL), F32), jax.ShapeDtypeStruct((1, D_MODEL), F32), blocks, blocks),
        compiler_params=_params(("arbitrary", "arbitrary")),
    )(jnp.reshape(me, (1,)).astype(jnp.int32), h, du, du, du, du, w_all, x, g_pre, dy)
    return dx, dg, received


def _in_proj_bwd_w(h, du, name):
    s = h.shape[0]
    tk = min(2048, s)
    n_k = s // tk

    def body(h_ref, dua_ref, dub_ref, out_ref, acc):
        k = pl.program_id(1)

        @pl.when(k == 0)
        def _():
            acc[...] = jnp.zeros_like(acc)
        acc[...] += _dot_tn(h_ref[...], jnp.concatenate([dua_ref[...], dub_ref[...]], axis=1))

        @pl.when(k == n_k - 1)
        def _():
            out_ref[...] = acc[...].astype(BF16)

    piece = lambda half: pl.BlockSpec((None, tk, WIDTH), lambda j, k: (_du_pieces_of_block(j)[half], k, 0))
    return _pcall(
        body, name=name, grid=(N_DEV, n_k),
        in_specs=[pl.BlockSpec((tk, D_MODEL), lambda j, k: (k, 0)), piece(0), piece(1)],
        out_specs=pl.BlockSpec((None, D_MODEL, COLS_PER_DEV), lambda j, k: (j, 0, 0)),
        out_shape=jax.ShapeDtypeStruct((N_DEV, D_MODEL, COLS_PER_DEV), BF16),
        scratch_shapes=[pltpu.VMEM((D_MODEL, COLS_PER_DEV), F32)],
        compiler_params=_params(("parallel", "arbitrary")),
    )(h, du, du)


def _adamw_math(g, w, m, v):
    m_new = ADAM_B1 * m + (1.0 - ADAM_B1) * g
    v_new = ADAM_B2 * v + (1.0 - ADAM_B2) * (g * g)
    m_hat = m_new / (1.0 - ADAM_B1 ** ADAM_STEP)
    v_hat = v_new / (1.0 - ADAM_B2 ** ADAM_STEP)
    delta = -ADAM_LR * (m_hat / (jnp.sqrt(v_hat) + ADAM_EPS) + ADAM_WD * w)
    return delta, m_new, v_new


def _sum_partials(p_ref):
    total = p_ref[0].astype(F32)
    for d in range(1, N_DEV):
        total = total + p_ref[d].astype(F32)
    return total


def _adamw_layers(parts0, parts1, w, m, v, name):
    _, r, c = w.shape
    tr = min(256, r)
    n_r = r // tr

    def body(p0_ref, p1_ref, w_ref, m_ref, v_ref, g_ref, d_ref, mo_ref, vo_ref):
        layer = pl.program_id(0)

        @pl.when(layer == 0)
        def _():
            g_ref[...] = _sum_partials(p0_ref)

        @pl.when(layer == 1)
        def _():
            g_ref[...] = _sum_partials(p1_ref)
        d_ref[...], mo_ref[...], vo_ref[...] = _adamw_math(g_ref[...], w_ref[...], m_ref[...], v_ref[...])

    part = lambda which: pl.BlockSpec((N_DEV, tr, c), lambda l, i: (0, jnp.where(l == which, i, 0), 0))
    par = lambda: pl.BlockSpec((None, tr, c), lambda l, i: (l, i, 0))
    out = jax.ShapeDtypeStruct(w.shape, F32)
    return _pcall(
        body, name=name, grid=(2, n_r),
        in_specs=[part(0), part(1), par(), par(), par()],
        out_specs=(par(), par(), par(), par()),
        out_shape=(out, out, out, out),
        compiler_params=_params(("arbitrary", "arbitrary")),
    )(parts0, parts1, w, m, v)


def _adamw_small(parts, w, m, v, name):
    def body(p_ref, w_ref, m_ref, v_ref, g_ref, d_ref, mo_ref, vo_ref):
        g = _sum_partials(p_ref)
        g_ref[...] = g
        d_ref[...], mo_ref[...], vo_ref[...] = _adamw_math(g, w_ref[...], m_ref[...], v_ref[...])

    out = jax.ShapeDtypeStruct(w.shape, F32)
    return _pcall(body, name=name, out_shape=(out, out, out, out), compiler_params=_params())(parts, w, m, v)


def _adamw_plain(g, w, m, v, name):
    def body(g_ref, w_ref, m_ref, v_ref, d_ref, mo_ref, vo_ref):
        d_ref[...], mo_ref[...], vo_ref[...] = _adamw_math(g_ref[...], w_ref[...], m_ref[...], v_ref[...])

    out = jax.ShapeDtypeStruct(w.shape, F32)
    return _pcall(body, name=name, out_shape=(out, out, out), compiler_params=_params())(g, w, m, v)


def _rows128(a):
    return a.reshape(-1, LANES)


SMALL_NAMES = ("pre_norm_g", "pool_w", "pool_scale", "conv_w", "conv_b", "post_norm_g")


def kernel(x, pre_norm_g, w_in, pool_w, pool_scale, conv_w, conv_b, w_branch, w_out, post_norm_g, loss_target, m_pre_norm_g, m_w_in, m_pool_w, m_pool_scale, m_conv_w, m_conv_b, m_w_branch, m_w_out, m_post_norm_g, v_pre_norm_g, v_w_in, v_pool_w, v_pool_scale, v_conv_w, v_conv_b, v_w_branch, v_w_out, v_post_norm_g):
    me = 4 * lax.axis_index("x") + 2 * lax.axis_index("y") + lax.axis_index("c")
    x0 = x[0]
    target = loss_target[0]
    conv_cols = conv_w.shape[-1]

    conv_w_pad = jnp.pad(conv_w.reshape(2 * 3, conv_cols), ((0, 2), (0, LANES - conv_cols)))
    w_in_all = [None, None]
    w_in_all[0], cw_g = _gather_two_level([w_in[0].astype(BF16), conv_w_pad], "gather_w_in_0")
    conv_w_full = cw_g[:, :6, :conv_cols].reshape(N_DEV, 2, 3, conv_cols).transpose(1, 2, 0, 3).reshape(2, 3, WIDTH)
    later_weights = ([w_in[1].astype(BF16), w_branch.astype(BF16), w_out.astype(BF16)], True)

    saved = []
    xin = x0
    for l in range(2):
        u, h = _in_proj_fwd(xin, pre_norm_g[l:l + 1], w_in_all[l], f"in_proj_fwd_{l}")
        y_pool = _pool_fwd(u, pool_w[l], pool_scale[l:l + 1], f"pool_fwd_{l}")
        y_conv = _conv_fwd(u, conv_w_full[l], conv_b[l:l + 1], f"conv_fwd_{l}")
        if l == 0:
            o_sb, y_sb, w_in_all[1], wb_g, wo_all = _sb_fwd(u, f"sb_fwd_{l}", later_weights)
            wb_all = wb_g.transpose(1, 2, 3, 0, 4).reshape(2, 3, WIDTH, D_MODEL)
        else:
            o_sb, y_sb = _sb_fwd(u, f"sb_fwd_{l}")
        if l == 0:
            xout, merged, pre = _merge_out_fwd(y_pool, y_conv, y_sb, u, wb_all, wo_all, xin, post_norm_g[l:l + 1], l,
                                               f"merge_out_fwd_{l}")
        else:
            dy, merged, pre, loss_row = _merge_out_fwd(y_pool, y_conv, y_sb, u, wb_all, wo_all, xin,
                                                       post_norm_g[l:l + 1], l, f"merge_out_fwd_{l}", target)
        saved.append((xin, u, h, y_pool, y_conv, y_sb, o_sb, merged, pre))
        xin = xout

    small = [None, None]
    recv = [None, None]
    ready = []
    for l in (1, 0):
        xl, u, h, y_pool, y_conv, y_sb, o_sb, merged, pre = saved[l]
        dmerged, dwo, dg_post = _out_proj_bwd(dy, pre, post_norm_g[l:l + 1], merged, wo_all, l, f"out_proj_bwd_{l}")
        du, dyp, dyc, dys, dwb = _merge_bwd(dmerged, y_pool, y_conv, y_sb, u, wb_all, l, f"merge_bwd_{l}")
        dwb = dwb.reshape(N_DEV, 3 * WIDTH, D_MODEL // N_DEV)
        dwo = dwo.reshape(N_DEV, D_MODEL // N_DEV, D_MODEL)
        du, dcw, dcb = _conv_bwd(u, conv_w_full[l], conv_b[l:l + 1], dyc, du, f"conv_bwd_{l}")
        du, dpw, dps = _pool_bwd(u, pool_w[l], pool_scale[l:l + 1], dyp, du, f"pool_bwd_{l}")
        small[l] = dict(pool_w=dpw, pool_scale=dps, conv_w=dcw, conv_b=dcb, post_norm_g=dg_post)
        if l == 1:
            du, dk, dv = _sb_bwd(u, o_sb, dys, du, f"sb_bwd_{l}")
        else:
            small[l]["pre_norm_g"] = jnp.zeros((1, D_MODEL), F32)
            packed = jnp.concatenate(
                [_rows128(jnp.stack([small[0][n], small[1][n]])) for n in SMALL_NAMES]
                + [jnp.pad(loss_row, ((0, 7), (0, 0)))], axis=0)
            du, dk, dv, *got, packed_all = _sb_bwd(
                u, o_sb, dys, du, f"sb_bwd_{l}", (ready + [dwb, dwo, packed], (False,) * 5 + (True,)))
            recv[1] = got[:3]
        du = lax.dynamic_update_slice(du, jnp.stack([dk, dv]).astype(BF16), (DU_SB_KV[0], 0, 0))
        if l == 1:
            dwi = _in_proj_bwd_w(h, du, f"in_proj_bwd_w_{l}")
            ready = [dwi, dwb, dwo]
            dx, dg_pre = _in_proj_bwd_x(du, w_in_all[l], xl, pre_norm_g[l:l + 1], dy, f"in_proj_bwd_x_{l}")
            small[l]["pre_norm_g"] = dg_pre
        else:
            dx, dg_pre, got_dwi = _in_proj_bwd_send(h, du, w_in_all[l], xl, pre_norm_g[l:l + 1], dy,
                                                    f"in_proj_bwd_{l}")
            recv[0] = [got_dwi] + got[3:]
        dy = dx
    grad_x = dy[None]

    (g_pre_0_all,) = _exchange([_rows128(dg_pre)], True, "gather_g_pre_0")
    packed_all = lax.dynamic_update_slice(packed_all, g_pre_0_all, (0, 0, 0))
    sizes = dict(pre_norm_g=16, pool_w=1024, pool_scale=8, conv_w=24, conv_b=8, post_norm_g=16)
    n_rows = sum(sizes.values())
    loss = jnp.sum(packed_all[:, n_rows, 0])

    given = dict(pre_norm_g=(pre_norm_g, m_pre_norm_g, v_pre_norm_g), pool_w=(pool_w, m_pool_w, v_pool_w),
                 pool_scale=(pool_scale, m_pool_scale, v_pool_scale), conv_b=(conv_b, m_conv_b, v_conv_b),
                 post_norm_g=(post_norm_g, m_post_norm_g, v_post_norm_g))
    zeros_cw = jnp.zeros((sizes["conv_w"], LANES), F32)
    pack3 = [jnp.concatenate([zeros_cw if n == "conv_w" else _rows128(given[n][k]) for n in SMALL_NAMES], axis=0)
             for k in range(3)]
    sg, sd, sm, sv = _adamw_small(packed_all[:, :n_rows], pack3[0], pack3[1], pack3[2], "adamw_small")

    def unpack(buf, name, shape):
        start = 0
        for n in SMALL_NAMES:
            if n == name:
                return buf[start:start + sizes[n]].reshape(shape)
            start += sizes[n]

    out = {}
    for n in ("pre_norm_g", "pool_w", "pool_scale", "conv_b", "post_norm_g"):
        shape = given[n][0].shape
        out[n] = tuple(unpack(b, n, shape) for b in (sg, sd, sm, sv))
    g_cw = lax.dynamic_slice_in_dim(unpack(sg, "conv_w", (2, 3, WIDTH)), me * conv_cols, conv_cols, axis=2)
    cw2 = lambda a: a.reshape(6, conv_cols)
    d_cw, m_cw, v_cw = _adamw_plain(cw2(g_cw), cw2(conv_w), cw2(m_conv_w), cw2(v_conv_w), "adamw_conv_w")
    out["conv_w"] = (g_cw,) + tuple(a.reshape(2, 3, conv_cols) for a in (d_cw, m_cw, v_cw))

    out["w_in"] = _adamw_layers(recv[0][0], recv[1][0], w_in, m_w_in, v_w_in, "adamw_w_in")
    cols = D_MODEL // N_DEV
    wb3 = lambda a: a.reshape(2, 3 * WIDTH, cols)
    out["w_branch"] = tuple(a.reshape(2, 3, WIDTH, cols) for a in _adamw_layers(
        recv[0][1], recv[1][1], wb3(w_branch), wb3(m_w_branch), wb3(v_w_branch), "adamw_w_branch"))
    out["w_out"] = _adamw_layers(recv[0][2], recv[1][2], w_out, m_w_out, v_w_out, "adamw_w_out")

    order = ("pre_norm_g", "w_in", "pool_w", "pool_scale", "conv_w", "conv_b", "w_branch", "w_out", "post_norm_g")
    return (loss, grad_x) + tuple(out[n][k] for k in range(4) for n in order)
```

```python
import functools

import jax
import jax.numpy as jnp
from jax import lax
from jax.experimental import pallas as pl
from jax.experimental.pallas import tpu as pltpu

F32 = jnp.float32
BF16 = jnp.bfloat16

N_DEV = 8
D_MODEL = 1024
WIDTH = 512
N_IN = 8192
COLS_PER_DEV = N_IN // N_DEV
HEAD_DIM = 64
LANES = 128
SB_SCALE = HEAD_DIM ** -0.5
LOG2E = 1.4426950408889634
RMS_EPS = 1e-6
POOL_HALO = 16
CONV_HALO = 8
ADAM_LR, ADAM_B1, ADAM_B2, ADAM_EPS, ADAM_WD, ADAM_STEP = 0.001, 0.9, 0.999, 1e-08, 0.01, 10
VMEM_LIMIT = 60 * 1024 * 1024

CB_POOL_V, CB_POOL_G = 0, 4
CB_CONV_X, CB_CONV_GB, CB_CONV_GC, CB_CONV_G = 8, 12, 16, 20
CB_SB_Q, CB_SB_K, CB_SB_V, CB_SB_G = 24, 28, 32, 36
MERGE_BLOCK_1024 = 5

DU_PIECES = 16
DU_MERGE = (0, 6)
DU_POOL = (6, 2)
DU_CONV = (8, 4)
DU_SB_QG = (12, 2)
DU_SB_KV = (14, 2)


def _du_pieces_of_block(j):
    first, second = 2 * (j - 5), 2 * (j - 5) + 1
    for block, (a, b) in enumerate(((6, 7), (8, 9), (10, 11), (12, 14), (15, 13))):
        first = jnp.where(j == block, a, first)
        second = jnp.where(j == block, b, second)
    return first, second


def _pcall(body, **kw):
    return pl.pallas_call(body, **kw)


def _params(sem=None):
    if sem is None:
        return pltpu.CompilerParams(vmem_limit_bytes=VMEM_LIMIT)
    return pltpu.CompilerParams(dimension_semantics=sem, vmem_limit_bytes=VMEM_LIMIT)


def _sigmoid(x):
    return 1.0 / (1.0 + jnp.exp(-x))


def _dot(a, b):
    return jnp.dot(a, b, preferred_element_type=F32)


def _dot_nt(a, b):
    return lax.dot_general(a, b, (((1,), (1,)), ((), ())), preferred_element_type=F32)


def _dot_tn(a, b):
    return lax.dot_general(a, b, (((0,), (0,)), ((), ())), preferred_element_type=F32)


def _split_bf16(x):
    hi = x.astype(BF16)
    lo = (x - hi.astype(F32)).astype(BF16)
    return hi, lo


N_PEER = N_DEV - 1
ANY_SPEC = pl.BlockSpec(memory_space=pl.ANY)


def _exchange_copies(ins, outs, send_sems, recv_sems, local_sems, gather, with_recvs=True):
    n = len(ins)
    gathers = _per_array(gather, n)
    x, y, c = lax.axis_index("x"), lax.axis_index("y"), lax.axis_index("c")
    me = 4 * x + 2 * y + c
    flip = lambda v, bit: 1 - v if bit else v
    local, sends, recvs = [], [], []
    for a in range(n):
        src = ins[a] if gathers[a] else ins[a].at[me]
        local.append(pltpu.make_async_copy(src, outs[a].at[me], local_sems.at[a]))
    for k in range(N_PEER):
        px, py, pc = flip(x, ((k + 1) >> 2) & 1), flip(y, ((k + 1) >> 1) & 1), flip(c, (k + 1) & 1)
        peer_id = 4 * px + 2 * py + pc
        for a in range(n):
            src = ins[a] if gathers[a] else ins[a].at[peer_id]
            common = dict(src_ref=src, send_sem=send_sems.at[a * N_PEER + k], recv_sem=recv_sems.at[a * N_PEER + k],
                          device_id=(px, py, pc), device_id_type=pl.DeviceIdType.MESH)
            sends.append(pltpu.make_async_remote_copy(dst_ref=outs[a].at[me], **common))
            if with_recvs:
                recvs.append(pltpu.make_async_remote_copy(dst_ref=outs[a].at[peer_id], **common))
    return local, sends, recvs


def _exchange_start(ins, outs, sems, gather):
    local, sends, _ = _exchange_copies(ins, outs, *sems, gather, with_recvs=False)
    for cp in local + sends:
        cp.start()


def _exchange_wait(ins, outs, sems, gather):
    local, sends, recvs = _exchange_copies(ins, outs, *sems, gather)
    for cp in recvs:
        cp.wait_recv()
    for cp in sends:
        cp.wait_send()
    for cp in local:
        cp.wait()


def _per_array(gather, n):
    return tuple(gather) if isinstance(gather, (tuple, list)) else (gather,) * n


def _exchange_out_shapes(arrs, gather):
    return [jax.ShapeDtypeStruct((N_DEV,) + tuple(a.shape if g else a.shape[1:]), a.dtype)
            for a, g in zip(arrs, _per_array(gather, len(arrs)))]


def _gather_two_level(arrs, name):
    n = len(arrs)

    def body(*refs):
        ins, outs = refs[:n], refs[n:2 * n]
        send_sems, recv_sems, local_sems = refs[2 * n:]
        x, y, c = lax.axis_index("x"), lax.axis_index("y"), lax.axis_index("c")
        me, sibling = (x, y, c), (x, y, 1 - c)
        chips = [(1 - x, y), (x, 1 - y), (1 - x, 1 - y)]
        slot = lambda dev: 4 * dev[0] + 2 * dev[1] + dev[2]

        def copy(a, k, block, to, src=None):
            return pltpu.make_async_remote_copy(
                src_ref=outs[a].at[slot(block)] if src is None else src, dst_ref=outs[a].at[slot(block)],
                send_sem=send_sems.at[a * N_PEER + k], recv_sem=recv_sems.at[a * N_PEER + k],
                device_id=to, device_id_type=pl.DeviceIdType.MESH)

        local = [pltpu.make_async_copy(ins[a], outs[a].at[slot(me)], local_sems.at[a]) for a in range(n)]
        first = []
        for a in range(n):
            first.append(copy(a, 0, me, sibling, src=ins[a]))
            first += [copy(a, 1 + j, me, (*chip, c), src=ins[a]) for j, chip in enumerate(chips)]
        for cp in local + first:
            cp.start()
        passed = []
        for j, chip in enumerate(chips):
            for a in range(n):
                copy(a, 1 + j, (*chip, c), me).wait_recv()
                passed.append(copy(a, 4 + j, (*chip, c), sibling))
                passed[-1].start()
        for a in range(n):
            copy(a, 0, sibling, me).wait_recv()
        for j, chip in enumerate(chips):
            for a in range(n):
                copy(a, 4 + j, (*chip, 1 - c), me).wait_recv()
        for cp in first + passed:
            cp.wait_send()
        for cp in local:
            cp.wait()

    return _pcall(
        body, name=name,
        out_shape=tuple(_exchange_out_shapes(arrs, True)),
        in_specs=[ANY_SPEC] * n, out_specs=tuple([ANY_SPEC] * n),
        scratch_shapes=_exchange_sems(n),
    )(*arrs)


def _exchange_sems(n):
    return [pltpu.SemaphoreType.DMA((n * N_PEER,)), pltpu.SemaphoreType.DMA((n * N_PEER,)),
            pltpu.SemaphoreType.DMA((n,))]


def _exchange(arrs, gather, name):
    n = len(arrs)

    def body(*refs):
        ins, outs, sems = refs[:n], refs[n:2 * n], refs[2 * n:]
        _exchange_start(ins, outs, sems, gather)
        _exchange_wait(ins, outs, sems, gather)

    return _pcall(
        body, name=name,
        out_shape=tuple(_exchange_out_shapes(arrs, gather)),
        in_specs=[ANY_SPEC] * n, out_specs=tuple([ANY_SPEC] * n),
        scratch_shapes=_exchange_sems(n),
    )(*arrs)


def _in_proj_fwd(x, g, w_all, name):
    s = x.shape[0]
    tm = min(2048, s)

    def body(x_ref, g_ref, w_ref, u_ref, h_ref, hs):
        @pl.when(pl.program_id(1) == 0)
        def _():
            xv = x_ref[...]
            r = lax.rsqrt(jnp.mean(xv * xv, axis=-1, keepdims=True) + RMS_EPS)
            hv = (xv * r * g_ref[...]).astype(BF16)
            hs[...] = hv
            h_ref[...] = hv
        u_ref[...] = _dot(hs[...], w_ref[...])

    return _pcall(
        body, name=name, grid=(s // tm, N_DEV),
        in_specs=[pl.BlockSpec((tm, D_MODEL), lambda i, j: (i, 0)),
                  pl.BlockSpec((1, D_MODEL), lambda i, j: (0, 0)),
                  pl.BlockSpec((None, D_MODEL, COLS_PER_DEV), lambda i, j: (j, 0, 0))],
        out_specs=(pl.BlockSpec((tm, COLS_PER_DEV), lambda i, j: (i, j)),
                   pl.BlockSpec((tm, D_MODEL), lambda i, j: (i, 0))),
        out_shape=(jax.ShapeDtypeStruct((s, N_IN), F32), jax.ShapeDtypeStruct((s, D_MODEL), BF16)),
        scratch_shapes=[pltpu.VMEM((tm, D_MODEL), BF16)],
        compiler_params=_params(("parallel", "arbitrary")),
    )(x, g, w_all)


def _pool_window(vs, t0, t, grp):
    ext = vs[pl.ds(t0, t + POOL_HALO), :]
    s2 = ext + pltpu.roll(ext, 1, 0)
    s4 = s2 + pltpu.roll(s2, 2, 0)
    s8 = s4 + pltpu.roll(s4, 4, 0)
    s16 = s8 + pltpu.roll(s8, 8, 0)
    sel = jnp.where(grp == 0, s2, jnp.where(grp == 1, s4, jnp.where(grp == 2, s8, s16)))
    return sel[POOL_HALO:, :], ext[POOL_HALO:, :]


def _pool_count(t0, t, grp):
    pos = t0 + lax.broadcasted_iota(jnp.int32, (t, 1), 0)
    return jnp.minimum(pos + 1, jnp.left_shift(2, grp)).astype(F32)


def _pool_fwd(u, pool_w, pool_scale, name):
    s = u.shape[0]
    t = min(256, s)

    def body(pv_ref, pg_ref, w_ref, sc_ref, y_ref, vs):
        grp = pl.program_id(0)
        vs[0:POOL_HALO, :] = jnp.zeros((POOL_HALO, LANES), F32)
        vs[POOL_HALO:, :] = pv_ref[...]
        wb = w_ref[...].astype(BF16)
        scale = sc_ref[...]

        def tile(i, carry):
            t0 = pl.multiple_of(i * t, t)
            win, v = _pool_window(vs, t0, t, grp)
            pooled = win / _pool_count(t0, t, grp) - v
            mixed = _dot(pooled.astype(BF16), wb)
            gate = pg_ref[pl.ds(t0, t), :]
            y_ref[pl.ds(t0, t), :] = (mixed * scale * (gate * _sigmoid(gate))).astype(BF16)
            return carry

        lax.fori_loop(0, s // t, tile, 0)

    return _pcall(
        body, name=name, grid=(4,),
        in_specs=[pl.BlockSpec((s, LANES), lambda g: (0, CB_POOL_V + g)),
                  pl.BlockSpec((s, LANES), lambda g: (0, CB_POOL_G + g)),
                  pl.BlockSpec((None, LANES, LANES), lambda g: (g, 0, 0)),
                  pl.BlockSpec((1, LANES), lambda g: (0, g))],
        out_specs=pl.BlockSpec((s, LANES), lambda g: (0, g)),
        out_shape=jax.ShapeDtypeStruct((s, WIDTH), BF16),
        scratch_shapes=[pltpu.VMEM((POOL_HALO + s, LANES), F32)],
        compiler_params=_params(("arbitrary",)),
    )(u, u, pool_w, pool_scale)


def _conv_taps(zs, t0, t):
    ext = zs[pl.ds(t0, t + CONV_HALO), :]
    z0 = ext[CONV_HALO:, :]
    z1 = pltpu.roll(ext, 1, 0)[CONV_HALO:, :]
    z2 = pltpu.roll(ext, 2, 0)[CONV_HALO:, :]
    return z0, z1, z2


def _conv_fwd(u, conv_w, conv_b, name):
    s = u.shape[0]
    t = min(256, s)

    def body(xc_ref, gb_ref, gc_ref, cg_ref, w_ref, b_ref, y_ref, zs):
        zs[0:CONV_HALO, :] = jnp.zeros((CONV_HALO, LANES), F32)
        zs[CONV_HALO:, :] = gc_ref[...] * xc_ref[...]
        w0, w1, w2 = w_ref[0:1, :], w_ref[1:2, :], w_ref[2:3, :]
        bias = b_ref[...]

        def tile(i, carry):
            t0 = pl.multiple_of(i * t, t)
            z0, z1, z2 = _conv_taps(zs, t0, t)
            conv = w0 * z2 + w1 * z1 + w2 * z0
            gate = cg_ref[pl.ds(t0, t), :]
            y = gb_ref[pl.ds(t0, t), :] * (conv + bias) * (gate * _sigmoid(gate))
            y_ref[pl.ds(t0, t), :] = y.astype(BF16)
            return carry

        lax.fori_loop(0, s // t, tile, 0)

    col = lambda base: pl.BlockSpec((s, LANES), lambda j: (0, base + j))
    return _pcall(
        body, name=name, grid=(4,),
        in_specs=[col(CB_CONV_X), col(CB_CONV_GB), col(CB_CONV_GC), col(CB_CONV_G),
                  pl.BlockSpec((3, LANES), lambda j: (0, j)),
                  pl.BlockSpec((1, LANES), lambda j: (0, j))],
        out_specs=pl.BlockSpec((s, LANES), lambda j: (0, j)),
        out_shape=jax.ShapeDtypeStruct((s, WIDTH), BF16),
        scratch_shapes=[pltpu.VMEM((CONV_HALO + s, LANES), F32)],
        compiler_params=_params(("arbitrary",)),
    )(u, u, u, u, conv_w, conv_b)


def _first_head_lanes(rows, width=LANES):
    lane = lax.broadcasted_iota(jnp.int32, (rows, width), 1)
    return jnp.bitwise_and(lane, LANES - 1) < HEAD_DIM


def _stack_heads(x, first):
    zero = jnp.zeros_like(x)
    return jnp.concatenate([jnp.where(first, x, zero), jnp.where(first, zero, x)], axis=0).astype(BF16)


def _causal_mask(tq, tk, copies):
    row = lax.broadcasted_iota(jnp.int32, (tq, tk), 0)
    col = lax.broadcasted_iota(jnp.int32, (tq, tk), 1)
    return jnp.concatenate([col < row] * copies, axis=0)


def _suffix_matrix(tk, inclusive, parts):
    r = lax.broadcasted_iota(jnp.int32, (parts * tk, 2 * tk), 0)
    c = lax.broadcasted_iota(jnp.int32, (parts * tk, 2 * tk), 1)
    r = jnp.bitwise_and(r, tk - 1)
    tri = (r >= c) if inclusive else (r > c)
    return jnp.where(c >= tk, 1.0, jnp.where(tri, 1.0, 0.0)).astype(BF16)


def _suffix_sums(x, m):
    hi, lo = _split_bf16(x)
    return _dot(jnp.concatenate([hi, lo], axis=1), m)


def _sb_log_terms(z, mask, m_strict):
    ls = jnp.minimum(z, 0.0) - jnp.log(1.0 + jnp.exp2(jnp.abs(z) * -LOG2E))
    lk = ls - z
    if mask is not None:
        lk = jnp.where(mask, lk, 0.0)
    return ls, _dot(lk.astype(BF16), m_strict)


SB_PAIRS = 4


def _pair_lanes(a):
    return slice(a * LANES, (a + 1) * LANES)


def _sb_fwd(u, name, xchg=None):
    s = u.shape[0]
    tq = tk = min(128, s)
    pairs = SB_PAIRS
    width = pairs * LANES
    rows = 2 * pairs * tq
    x_arrs, x_gather = xchg if xchg else ((), True)
    n_x = len(x_arrs)
    grid = (4 // pairs, s // tq)

    def body(*refs):
        q_ref, k_ref, v_ref, g_ref = refs[:4]
        x_in, refs = refs[4:4 + n_x], refs[4 + n_x:]
        o_ref, y_ref = refs[:2]
        x_out, refs = refs[2:2 + n_x], refs[2 + n_x:]
        kbf, vst, z_s, ell_s, carry_s = refs[:5]
        x_sems = refs[5:]
        i = pl.program_id(1)
        if n_x:
            @pl.when((pl.program_id(0) == 0) & (i == 0))
            def _():
                _exchange_start(x_in, x_out, x_sems, x_gather)

        @pl.when(i == 0)
        def _():
            kbf[...] = k_ref[...].astype(BF16)
            first_s = _first_head_lanes(s, width)
            vf = v_ref[...]
            vst[0] = jnp.where(first_s, vf, 0.0).astype(BF16)
            vst[1] = jnp.where(first_s, 0.0, vf).astype(BF16)

        first = _first_head_lanes(tq)
        mask = _causal_mask(tq, tk, 2 * pairs)
        m_strict = _suffix_matrix(tk, False, 1)
        qcat = jnp.concatenate([_stack_heads(q_ref[:, _pair_lanes(a)] * SB_SCALE, first) for a in range(pairs)],
                               axis=0)

        def log_weights(b, m):
            off = pl.multiple_of(jnp.maximum(b, 0) * tk, tk)
            z = jnp.concatenate(
                [_dot_nt(qcat[a * 2 * tq:(a + 1) * 2 * tq], kbf[pl.ds(off, tk), _pair_lanes(a)])
                 for a in range(pairs)], axis=0)
            ls, cs = _sb_log_terms(z, m, m_strict)
            carry = carry_s[...]
            ell_s[...] = ls + cs[:, :tk] + carry
            carry_s[...] = carry + cs[:, tk:]

        def consume(b, accs, m):
            w = jnp.exp(ell_s[...])
            if m is not None:
                w = jnp.where(m, w, 0.0)
            wb = w.astype(BF16)
            off = pl.multiple_of(b * tk, tk)
            new = []
            for a in range(pairs):
                r0 = a * 2 * tq
                wcat = jnp.concatenate([wb[r0:r0 + tq], wb[r0 + tq:r0 + 2 * tq]], axis=1)
                vcat = jnp.concatenate([vst[0, pl.ds(off, tk), _pair_lanes(a)], vst[1, pl.ds(off, tk), _pair_lanes(a)]],
                                       axis=0)
                new.append(accs[a] + _dot(wcat, vcat))
            return tuple(new)

        carry_s[...] = jnp.zeros((rows, tk), F32)
        log_weights(i, mask)
        accs = consume(i, tuple(jnp.zeros((tq, LANES), F32) for _ in range(pairs)), mask)
        log_weights(i - 1, None)

        def step(n, accs):
            accs = consume(i - n, accs, None)
            log_weights(i - n - 1, None)
            return accs

        accs = lax.fori_loop(1, i + 1, step, accs)
        o = jnp.concatenate(accs, axis=1)
        o_ref[...] = o
        gate = g_ref[...]
        y_ref[...] = (o * (gate * _sigmoid(gate))).astype(BF16)
        if n_x:
            @pl.when((pl.program_id(0) == grid[0] - 1) & (i == grid[1] - 1))
            def _():
                _exchange_wait(x_in, x_out, x_sems, x_gather)

    base = lambda cb: cb // pairs
    qblk = lambda cb: pl.BlockSpec((tq, width), lambda p, i: (i, base(cb) + p))
    full = lambda cb: pl.BlockSpec((s, width), lambda p, i: (0, base(cb) + p), pipeline_mode=pl.Buffered(1))
    state = pltpu.VMEM((rows, tk), F32)
    return _pcall(
        body, name=name, grid=grid,
        in_specs=[qblk(CB_SB_Q), full(CB_SB_K), full(CB_SB_V), qblk(CB_SB_G)] + [ANY_SPEC] * n_x,
        out_specs=(qblk(0), qblk(0)) + (ANY_SPEC,) * n_x,
        out_shape=(jax.ShapeDtypeStruct((s, WIDTH), F32), jax.ShapeDtypeStruct((s, WIDTH), BF16))
        + tuple(_exchange_out_shapes(x_arrs, x_gather)),
        scratch_shapes=[pltpu.VMEM((s, width), BF16), pltpu.VMEM((2, s, width), BF16), state, state, state]
        + (_exchange_sems(n_x) if n_x else []),
        compiler_params=_params(("arbitrary", "arbitrary")),
    )(u, u, u, u, *x_arrs)


def _merge_out_fwd(y_pool, y_conv, y_sb, u, wb_all, wo_all, x, g_post, layer, name, target=None):
    s = x.shape[0]
    tm = min(512, s)
    n_tiles = s // tm
    with_loss = target is not None

    def body(yp, yc, ys, m0, m1, m2, wb_ref, wo_ref, x_ref, g_ref, *rest):
        merged = jnp.zeros((tm, D_MODEL), F32)
        for n, (y_ref, m_ref) in enumerate(((yp, m0), (yc, m1), (ys, m2))):
            merged = merged + _sigmoid(m_ref[...]) * _dot(y_ref[...], wb_ref[n])
        mb = merged.astype(BF16)
        pre = _dot(mb, wo_ref[...].reshape(D_MODEL, D_MODEL))
        r = lax.rsqrt(jnp.mean(pre * pre, axis=-1, keepdims=True) + RMS_EPS)
        y = x_ref[...] + pre * r * g_ref[...]
        if not with_loss:
            out_ref, merged_ref, pre_ref = rest
            out_ref[...] = y
        else:
            t_ref, out_ref, merged_ref, pre_ref, loss_ref, acc = rest
            i = pl.program_id(0)

            @pl.when(i == 0)
            def _():
                acc[...] = jnp.zeros_like(acc)
            err = y - t_ref[...]
            out_ref[...] = err / D_MODEL
            acc[...] += jnp.sum(err * err, axis=0, keepdims=True)

            @pl.when(i == n_tiles - 1)
            def _():
                total = jnp.sum(acc[...], axis=1, keepdims=True) * (0.5 / D_MODEL)
                loss_ref[...] = jnp.broadcast_to(total, (1, LANES))
        merged_ref[...] = mb
        pre_ref[...] = pre

    rows = lambda w: pl.BlockSpec((tm, w), lambda i: (i, 0))
    merge = lambda n: pl.BlockSpec((tm, D_MODEL), lambda i: (i, MERGE_BLOCK_1024 + n))
    out_specs = (rows(D_MODEL), rows(D_MODEL), rows(D_MODEL))
    out_shape = (jax.ShapeDtypeStruct((s, D_MODEL), F32), jax.ShapeDtypeStruct((s, D_MODEL), BF16),
                 jax.ShapeDtypeStruct((s, D_MODEL), F32))
    if with_loss:
        out_specs += (pl.BlockSpec((1, LANES), lambda i: (0, 0)),)
        out_shape += (jax.ShapeDtypeStruct((1, LANES), F32),)
    return _pcall(
        body, name=name, grid=(n_tiles,),
        in_specs=[rows(WIDTH), rows(WIDTH), rows(WIDTH), merge(0), merge(1), merge(2),
                  pl.BlockSpec((None, 3, WIDTH, D_MODEL), lambda i: (layer, 0, 0, 0)),
                  pl.BlockSpec((N_DEV, None, D_MODEL // N_DEV, D_MODEL), lambda i: (0, layer, 0, 0)),
                  rows(D_MODEL), pl.BlockSpec((1, D_MODEL), lambda i: (0, 0))] + ([rows(D_MODEL)] if with_loss else []),
        out_specs=out_specs, out_shape=out_shape,
        scratch_shapes=[pltpu.VMEM((1, D_MODEL), F32)] if with_loss else [],
        compiler_params=_params(("arbitrary",)),
    )(y_pool, y_conv, y_sb, u, u, u, wb_all, wo_all, x, g_post, *([target] if with_loss else []))


def _out_proj_bwd(dy, pre, g_post, merged, wo_all, layer, name):
    s = dy.shape[0]
    tm = min(512, s)
    n_tiles = s // tm

    def body(dy_ref, pre_ref, g_ref, mg_ref, wo_ref, dm_ref, dwo_ref, dg_ref, acc):
        i = pl.program_id(0)

        @pl.when(i == 0)
        def _():
            acc[...] = jnp.zeros_like(acc)
            dg_ref[...] = jnp.zeros_like(dg_ref)
        dyv, pre_v = dy_ref[...], pre_ref[...]
        r = lax.rsqrt(jnp.mean(pre_v * pre_v, axis=-1, keepdims=True) + RMS_EPS)
        dg_ref[...] += jnp.sum(dyv * pre_v * r, axis=0, keepdims=True)
        a = dyv * g_ref[...]
        dpre = r * a - pre_v * (r * r * r) * jnp.mean(a * pre_v, axis=-1, keepdims=True)
        db = dpre.astype(BF16)
        acc[...] += _dot_tn(mg_ref[...], db)
        dm_ref[...] = _dot_nt(db, wo_ref[...].reshape(D_MODEL, D_MODEL))

        @pl.when(i == n_tiles - 1)
        def _():
            dwo_ref[...] = acc[...].astype(BF16)

    rows = lambda: pl.BlockSpec((tm, D_MODEL), lambda i: (i, 0))
    return _pcall(
        body, name=name, grid=(n_tiles,),
        in_specs=[rows(), rows(), pl.BlockSpec((1, D_MODEL), lambda i: (0, 0)), rows(),
                  pl.BlockSpec((N_DEV, None, D_MODEL // N_DEV, D_MODEL), lambda i: (0, layer, 0, 0))],
        out_specs=(rows(), pl.BlockSpec((D_MODEL, D_MODEL), lambda i: (0, 0)),
                   pl.BlockSpec((1, D_MODEL), lambda i: (0, 0))),
        out_shape=(jax.ShapeDtypeStruct((s, D_MODEL), F32), jax.ShapeDtypeStruct((D_MODEL, D_MODEL), BF16),
                   jax.ShapeDtypeStruct((1, D_MODEL), F32)),
        scratch_shapes=[pltpu.VMEM((D_MODEL, D_MODEL), F32)],
        compiler_params=_params(("arbitrary",)),
    )(dy, pre, g_post, merged, wo_all)


def _merge_bwd(dmerged, y_pool, y_conv, y_sb, u, wb_all, layer, name):
    s = dmerged.shape[0]
    tm = min(512, s)
    n_tiles = s // tm
    cols = D_MODEL // N_DEV

    def body(dm_ref, yp, yc, ys, m0, m1, m2, wb_ref, du_ref, dyp, dyc, dys, dwb_ref, acc):
        i = pl.program_id(0)

        @pl.when(i == 0)
        def _():
            acc[...] = jnp.zeros_like(acc)
        dm = dm_ref[...]
        for n, (y_ref, m_ref, dy_ref) in enumerate(((yp, m0, dyp), (yc, m1, dyc), (ys, m2, dys))):
            yv = y_ref[...]
            wb = wb_ref[n]
            gate = _sigmoid(m_ref[...])
            proj = _dot(yv, wb)
            dgate = (dm * proj * gate * (1.0 - gate)).astype(BF16)
            du_ref[2 * n] = dgate[:, :WIDTH]
            du_ref[2 * n + 1] = dgate[:, WIDTH:]
            dproj = (dm * gate).astype(BF16)
            acc[n] += _dot_tn(yv, dproj)
            dy_ref[...] = _dot_nt(dproj, wb)

        @pl.when(i == n_tiles - 1)
        def _():
            for j in range(N_DEV):
                for n in range(3):
                    dwb_ref[j, n] = acc[n, :, j * cols:(j + 1) * cols].astype(BF16)

    rows = lambda w: pl.BlockSpec((tm, w), lambda i: (i, 0))
    merge = lambda n: pl.BlockSpec((tm, D_MODEL), lambda i: (i, MERGE_BLOCK_1024 + n))
    return _pcall(
        body, name=name, grid=(n_tiles,),
        in_specs=[rows(D_MODEL), rows(WIDTH), rows(WIDTH), rows(WIDTH), merge(0), merge(1), merge(2),
                  pl.BlockSpec((None, 3, WIDTH, D_MODEL), lambda i: (layer, 0, 0, 0))],
        out_specs=(pl.BlockSpec((DU_MERGE[1], tm, WIDTH), lambda i: (DU_MERGE[0] // DU_MERGE[1], i, 0)),
                   rows(WIDTH), rows(WIDTH), rows(WIDTH),
                   pl.BlockSpec((N_DEV, 3, WIDTH, cols), lambda i: (0, 0, 0, 0))),
        out_shape=(jax.ShapeDtypeStruct((DU_PIECES, s, WIDTH), BF16),
                   jax.ShapeDtypeStruct((s, WIDTH), F32), jax.ShapeDtypeStruct((s, WIDTH), F32),
                   jax.ShapeDtypeStruct((s, WIDTH), F32),
                   jax.ShapeDtypeStruct((N_DEV, 3, WIDTH, cols), BF16)),
        scratch_shapes=[pltpu.VMEM((3, WIDTH, D_MODEL), F32)],
        compiler_params=_params(("arbitrary",)),
    )(dmerged, y_pool, y_conv, y_sb, u, u, u, wb_all)


def _sb_bwd(u, o, dys, du, name, xchg=None):
    s = u.shape[0]
    tq = tk = min(128, s)
    pairs = SB_PAIRS
    width = pairs * LANES
    assert width == WIDTH
    rows = 2 * pairs * tq
    pair_rows = lambda a: slice(a * 2 * tq, (a + 1) * 2 * tq)

    x_arrs, x_gather = xchg if xchg else ((), True)
    n_x = len(x_arrs)
    grid = (4 // pairs, s // tq)

    def body(*refs):
        q_ref, k_ref, v_ref, g_ref, o_ref, dys_ref = refs[:6]
        x_in, refs = refs[7:7 + n_x], refs[7 + n_x:]
        du_ref, dk_ref, dv_ref = refs[:3]
        dq_ref, dg_ref = du_ref.at[0], du_ref.at[1]
        x_out, refs = refs[3:3 + n_x], refs[3 + n_x:]
        kbf, vbf, kst, z_s, ell_s, ls_s, cl_s, wb_s, g_s, bef_s, cg_s, beta_s = refs[:12]
        x_sems = refs[12:]
        i = pl.program_id(1)
        if n_x:
            @pl.when((pl.program_id(0) == 0) & (i == 0))
            def _():
                _exchange_start(x_in, x_out, x_sems, x_gather)

        @pl.when(i == 0)
        def _():
            dk_ref[...] = jnp.zeros_like(dk_ref)
            dv_ref[...] = jnp.zeros_like(dv_ref)
            kf = k_ref[...]
            kbf[...] = kf.astype(BF16)
            vbf[...] = v_ref[...].astype(BF16)
            first_s = _first_head_lanes(s, width)
            kst[0] = jnp.where(first_s, kf, 0.0).astype(BF16)
            kst[1] = jnp.where(first_s, 0.0, kf).astype(BF16)

        first = _first_head_lanes(tq)
        mask = _causal_mask(tq, tk, 2 * pairs)
        m_strict = _suffix_matrix(tk, False, 1)
        m_incl = _suffix_matrix(tk, True, 2)

        gate = g_ref[...]
        sg = _sigmoid(gate)
        dy = dys_ref[...]
        ov = o_ref[...]
        dg_ref[...] = (dy * ov * (sg * (1.0 + gate * (1.0 - sg)))).astype(BF16)
        do = (dy * (gate * sg)).astype(BF16)
        prod = do.astype(F32) * ov
        row_sum = lambda v: jnp.broadcast_to(jnp.sum(v, axis=1, keepdims=True), (tq, tk))
        dsum, docat, qcat = [], [], []
        for a in range(pairs):
            pa = prod[:, _pair_lanes(a)]
            dsum += [row_sum(jnp.where(first, pa, 0.0)), row_sum(jnp.where(first, 0.0, pa))]
            docat.append(_stack_heads(do[:, _pair_lanes(a)], first))
            qcat.append(_stack_heads(q_ref[:, _pair_lanes(a)] * SB_SCALE, first))
        dsum = jnp.concatenate(dsum, axis=0)

        def block_start(b):
            return pl.multiple_of(jnp.maximum(b, 0) * tk, tk)

        def scores(b):
            off = block_start(b)
            z_s[...] = jnp.concatenate([_dot_nt(qcat[a], kbf[pl.ds(off, tk), _pair_lanes(a)]) for a in range(pairs)],
                                       axis=0)

        def log_weights(m):
            ls, cs = _sb_log_terms(z_s[...], m, m_strict)
            cl = cl_s[...]
            ell_s[...] = ls + cs[:, :tk] + cl
            cl_s[...] = cl + cs[:, tk:]
            ls_s[...] = ls

        def weights(b, m):
            off = block_start(b)
            dwt = jnp.concatenate([_dot_nt(docat[a], vbf[pl.ds(off, tk), _pair_lanes(a)]) for a in range(pairs)],
                                  axis=0)
            w = jnp.exp(ell_s[...])
            if m is not None:
                w = jnp.where(m, w, 0.0)
            wb = w.astype(BF16)
            g = dwt * wb.astype(F32)
            gs = _suffix_sums(g, m_incl)
            cg = cg_s[...]
            beta = jnp.exp(ls_s[...])
            wb_s[...] = wb
            beta_s[...] = beta
            g_s[...] = g * (1.0 - beta)
            bef_s[...] = gs[:, :tk] + cg
            cg_s[...] = cg + gs[:, tk:]

        def grads(b, dqs, m):
            dz = g_s[...] - beta_s[...] * (dsum - bef_s[...])
            if m is not None:
                dz = jnp.where(m, dz, 0.0)
            dzb = dz.astype(BF16)
            wb = wb_s[...]
            off = pl.multiple_of(b * tk, tk)
            new = []
            for a in range(pairs):
                r0 = a * 2 * tq
                kcat = jnp.concatenate([kst[0, pl.ds(off, tk), _pair_lanes(a)], kst[1, pl.ds(off, tk), _pair_lanes(a)]],
                                       axis=0)
                new.append(dqs[a] + _dot(jnp.concatenate([dzb[r0:r0 + tq], dzb[r0 + tq:r0 + 2 * tq]], axis=1), kcat))
                dk_ref[pl.ds(off, tk), _pair_lanes(a)] += _dot_tn(dzb[pair_rows(a)], qcat[a])
                dv_ref[pl.ds(off, tk), _pair_lanes(a)] += _dot_tn(wb[pair_rows(a)], docat[a])
            return tuple(new)

        zero = jnp.zeros((rows, tk), F32)
        cl_s[...] = zero
        cg_s[...] = zero
        scores(i)
        log_weights(mask)
        scores(i - 1)
        weights(i, mask)
        log_weights(None)
        scores(i - 2)
        dqs = grads(i, tuple(jnp.zeros((tq, LANES), F32) for _ in range(pairs)), mask)
        weights(i - 1, None)
        log_weights(None)
        scores(i - 3)

        def step(n, dqs):
            dqs = grads(i - n, dqs, None)
            weights(i - n - 1, None)
            log_weights(None)
            scores(i - n - 3)
            return dqs

        dqs = lax.fori_loop(1, i + 1, step, dqs)
        dq_ref[...] = (jnp.concatenate(dqs, axis=1) * SB_SCALE).astype(BF16)
        if n_x:
            @pl.when((pl.program_id(0) == grid[0] - 1) & (i == grid[1] - 1))
            def _():
                _exchange_wait(x_in, x_out, x_sems, x_gather)

    base = lambda cb: cb // pairs
    qblk = lambda cb: pl.BlockSpec((tq, width), lambda p, i: (i, base(cb) + p))
    full = lambda cb: pl.BlockSpec((s, width), lambda p, i: (0, base(cb) + p), pipeline_mode=pl.Buffered(1))
    state = pltpu.VMEM((rows, tk), F32)
    return _pcall(
        body, name=name, grid=grid,
        in_specs=[qblk(CB_SB_Q), full(CB_SB_K), full(CB_SB_V), qblk(CB_SB_G), qblk(0), qblk(0), ANY_SPEC]
        + [ANY_SPEC] * n_x,
        out_specs=(pl.BlockSpec((DU_SB_QG[1], tq, WIDTH), lambda p, i: (DU_SB_QG[0] // DU_SB_QG[1], i, 0)),
                   full(0), full(0)) + (ANY_SPEC,) * n_x,
        out_shape=(jax.ShapeDtypeStruct(du.shape, du.dtype), jax.ShapeDtypeStruct((s, WIDTH), F32),
                   jax.ShapeDtypeStruct((s, WIDTH), F32)) + tuple(_exchange_out_shapes(x_arrs, x_gather)),
        input_output_aliases={6: 0},
        scratch_shapes=[pltpu.VMEM((s, width), BF16), pltpu.VMEM((s, width), BF16), pltpu.VMEM((2, s, width), BF16),
                        state, state, state, state, pltpu.VMEM((rows, tk), BF16),
                        state, state, state, state] + (_exchange_sems(n_x) if n_x else []),
        compiler_params=_params(("arbitrary", "arbitrary")),
    )(u, u, u, u, o, dys, du, *x_arrs)


def _conv_bwd(u, conv_w, conv_b, dyc, du, name):
    s = u.shape[0]
    t = min(256, s)
    n_tiles = s // t

    def body(xc_ref, gb_ref, gc_ref, cg_ref, w_ref, b_ref, dy_ref, du_in, du_ref, dw_ref, db_ref, zs, ds):
        dxc_ref, dgb_ref, dgc_ref, dcg_ref = (du_ref.at[p] for p in range(4))
        zs[0:CONV_HALO, :] = jnp.zeros((CONV_HALO, LANES), F32)
        zs[CONV_HALO:, :] = gc_ref[...] * xc_ref[...]
        ds[s:, :] = jnp.zeros((CONV_HALO, LANES), F32)
        w0, w1, w2 = w_ref[0:1, :], w_ref[1:2, :], w_ref[2:3, :]
        bias = b_ref[...]

        def first(i, sums):
            t0 = pl.multiple_of(i * t, t)
            z0, z1, z2 = _conv_taps(zs, t0, t)
            pre = w0 * z2 + w1 * z1 + w2 * z0 + bias
            gate = cg_ref[pl.ds(t0, t), :]
            sg = _sigmoid(gate)
            gb = gb_ref[pl.ds(t0, t), :]
            dy = dy_ref[pl.ds(t0, t), :]
            dcg_ref[pl.ds(t0, t), :] = (dy * gb * pre * (sg * (1.0 + gate * (1.0 - sg)))).astype(BF16)
            dgb_ref[pl.ds(t0, t), :] = (dy * pre * (gate * sg)).astype(BF16)
            dc = dy * gb * (gate * sg)
            ds[pl.ds(t0, t), :] = dc
            red = lambda v: jnp.sum(v, axis=0, keepdims=True)
            return (sums[0] + red(dc * z2), sums[1] + red(dc * z1), sums[2] + red(dc * z0), sums[3] + red(dc))

        zrow = jnp.zeros((1, LANES), F32)
        sw0, sw1, sw2, sb = lax.fori_loop(0, n_tiles, first, (zrow, zrow, zrow, zrow))
        dw_ref[0:1, :] = sw0
        dw_ref[1:2, :] = sw1
        dw_ref[2:3, :] = sw2
        db_ref[...] = sb

        def second(i, carry):
            t0 = pl.multiple_of(i * t, t)
            ext = ds[pl.ds(t0, t + CONV_HALO), :]
            n = t + CONV_HALO
            d0 = ext[:t, :]
            d1 = pltpu.roll(ext, n - 1, 0)[:t, :]
            d2 = pltpu.roll(ext, n - 2, 0)[:t, :]
            dz = w2 * d0 + w1 * d1 + w0 * d2
            dgc_ref[pl.ds(t0, t), :] = (dz * xc_ref[pl.ds(t0, t), :]).astype(BF16)
            dxc_ref[pl.ds(t0, t), :] = (dz * gc_ref[pl.ds(t0, t), :]).astype(BF16)
            return carry

        lax.fori_loop(0, n_tiles, second, 0)

    col = lambda base: pl.BlockSpec((s, LANES), lambda j: (0, base + j))
    first, count = DU_CONV
    return _pcall(
        body, name=name, grid=(4,),
        in_specs=[col(CB_CONV_X), col(CB_CONV_GB), col(CB_CONV_GC), col(CB_CONV_G),
                  pl.BlockSpec((3, LANES), lambda j: (0, j)), pl.BlockSpec((1, LANES), lambda j: (0, j)), col(0),
                  ANY_SPEC],
        out_specs=(pl.BlockSpec((count, s, LANES), lambda j: (first // count, 0, j)),
                   pl.BlockSpec((3, LANES), lambda j: (0, j)), pl.BlockSpec((1, LANES), lambda j: (0, j))),
        out_shape=(jax.ShapeDtypeStruct(du.shape, du.dtype),
                   jax.ShapeDtypeStruct((3, WIDTH), F32), jax.ShapeDtypeStruct((1, WIDTH), F32)),
        scratch_shapes=[pltpu.VMEM((CONV_HALO + s, LANES), F32), pltpu.VMEM((s + CONV_HALO, LANES), F32)],
        input_output_aliases={7: 0},
        compiler_params=_params(("arbitrary",)),
    )(u, u, u, u, conv_w, conv_b, dyc, du)


def _pool_bwd(u, pool_w, pool_scale, dyp, du, name):
    s = u.shape[0]
    t = min(256, s)
    n_tiles = s // t

    def body(pv_ref, pg_ref, w_ref, sc_ref, dy_ref, du_in, du_ref, dw_ref, dsc_ref, vs, es, dps):
        dpv_ref, dpg_ref = du_ref.at[0], du_ref.at[1]
        grp = pl.program_id(0)
        vs[0:POOL_HALO, :] = jnp.zeros((POOL_HALO, LANES), F32)
        vs[POOL_HALO:, :] = pv_ref[...]
        es[s:, :] = jnp.zeros((POOL_HALO, LANES), F32)
        wb = w_ref[...].astype(BF16)
        scale = sc_ref[...]

        def first(i, sums):
            dw, dsc = sums
            t0 = pl.multiple_of(i * t, t)
            win, v = _pool_window(vs, t0, t, grp)
            cnt = _pool_count(t0, t, grp)
            pb = (win / cnt - v).astype(BF16)
            mixed = _dot(pb, wb)
            gate = pg_ref[pl.ds(t0, t), :]
            sg = _sigmoid(gate)
            dy = dy_ref[pl.ds(t0, t), :]
            dpg_ref[pl.ds(t0, t), :] = (dy * (mixed * scale) * (sg * (1.0 + gate * (1.0 - sg)))).astype(BF16)
            dms = dy * (gate * sg)
            dsc = dsc + jnp.sum(dms * mixed, axis=0, keepdims=True)
            dmb = (dms * scale).astype(BF16)
            dw = dw + _dot_tn(pb, dmb)
            dpooled = _dot_nt(dmb, wb)
            dps[pl.ds(t0, t), :] = dpooled
            es[pl.ds(t0, t), :] = dpooled / cnt
            return dw, dsc

        dw, dsc = lax.fori_loop(0, n_tiles, first, (jnp.zeros((LANES, LANES), F32), jnp.zeros((1, LANES), F32)))
        dw_ref[...] = dw
        dsc_ref[...] = dsc

        def second(i, carry):
            t0 = pl.multiple_of(i * t, t)
            ext = es[pl.ds(t0, t + POOL_HALO), :]
            n = t + POOL_HALO
            f2 = ext + pltpu.roll(ext, n - 1, 0)
            f4 = f2 + pltpu.roll(f2, n - 2, 0)
            f8 = f4 + pltpu.roll(f4, n - 4, 0)
            f16 = f8 + pltpu.roll(f8, n - 8, 0)
            sel = jnp.where(grp == 0, f2, jnp.where(grp == 1, f4, jnp.where(grp == 2, f8, f16)))
            dpv_ref[pl.ds(t0, t), :] = (sel[:t, :] - dps[pl.ds(t0, t), :]).astype(BF16)
            return carry

        lax.fori_loop(0, n_tiles, second, 0)

    col = lambda base: pl.BlockSpec((s, LANES), lambda g: (0, base + g))
    first, count = DU_POOL
    return _pcall(
        body, name=name, grid=(4,),
        in_specs=[col(CB_POOL_V), col(CB_POOL_G), pl.BlockSpec((None, LANES, LANES), lambda g: (g, 0, 0)),
                  pl.BlockSpec((1, LANES), lambda g: (0, g)), col(0), ANY_SPEC],
        out_specs=(pl.BlockSpec((count, s, LANES), lambda g: (first // count, 0, g)),
                   pl.BlockSpec((None, LANES, LANES), lambda g: (g, 0, 0)),
                   pl.BlockSpec((1, LANES), lambda g: (0, g))),
        out_shape=(jax.ShapeDtypeStruct(du.shape, du.dtype),
                   jax.ShapeDtypeStruct((4, LANES, LANES), F32), jax.ShapeDtypeStruct((1, WIDTH), F32)),
        scratch_shapes=[pltpu.VMEM((POOL_HALO + s, LANES), F32), pltpu.VMEM((s + POOL_HALO, LANES), F32),
                        pltpu.VMEM((s, LANES), F32)],
        input_output_aliases={5: 0},
        compiler_params=_params(("arbitrary",)),
    )(u, u, pool_w, pool_scale, dyp, du)


def _in_proj_bwd_x(du, w_all, x, g_pre, dy, name):
    s = x.shape[0]
    tm = min(1024, s)
    grid = (s // tm, N_DEV)

    def body(dua_ref, dub_ref, w_ref, x_ref, g_ref, dy_ref, dx_ref, dg_ref, acc):
        i, k = pl.program_id(0), pl.program_id(1)

        @pl.when(k == 0)
        def _():
            acc[...] = jnp.zeros_like(acc)

        @pl.when((k == 0) & (i == 0))
        def _():
            dg_ref[...] = jnp.zeros_like(dg_ref)
        acc[...] += _dot_nt(jnp.concatenate([dua_ref[...], dub_ref[...]], axis=1), w_ref[...])

        @pl.when(k == N_DEV - 1)
        def _():
            dh, xv = acc[...], x_ref[...]
            r = lax.rsqrt(jnp.mean(xv * xv, axis=-1, keepdims=True) + RMS_EPS)
            dg_ref[...] += jnp.sum(dh * xv * r, axis=0, keepdims=True)
            a = dh * g_ref[...]
            dx_ref[...] = dy_ref[...] + r * a - xv * (r * r * r) * jnp.mean(a * xv, axis=-1, keepdims=True)

    rows = lambda: pl.BlockSpec((tm, D_MODEL), lambda i, k: (i, 0))
    vec = lambda: pl.BlockSpec((1, D_MODEL), lambda i, k: (0, 0))
    piece = lambda half: pl.BlockSpec((None, tm, WIDTH), lambda i, k: (_du_pieces_of_block(k)[half], i, 0))
    return _pcall(
        body, name=name, grid=grid,
        in_specs=[piece(0), piece(1), pl.BlockSpec((None, D_MODEL, COLS_PER_DEV), lambda i, k: (k, 0, 0)),
                  rows(), vec(), rows()],
        out_specs=(rows(), vec()),
        out_shape=(jax.ShapeDtypeStruct((s, D_MODEL), F32), jax.ShapeDtypeStruct((1, D_MODEL), F32)),
        scratch_shapes=[pltpu.VMEM((tm, D_MODEL), F32)],
        compiler_params=_params(("arbitrary", "arbitrary")),
    )(du, du, w_all, x, g_pre, dy)


ROW_OFFSETS = (6, 7, 2, 4, 3, 5, 0, 1)


def _in_proj_bwd_send(h, du, w_all, x, g_pre, dy, name):
    s = x.shape[0]
    tk = s // N_DEV
    tm = min(1024, s)
    n_i = s // tm
    grid = (N_DEV + n_i, N_DEV)
    last = N_DEV - 1
    def offset(row):
        return functools.reduce(lambda acc, rn: jnp.where(row == rn[0], rn[1], acc), enumerate(ROW_OFFSETS), 0)

    def body(me_ref, h_ref, duwa_ref, duwb_ref, duxa_ref, duxb_ref, w_ref, x_ref, g_ref, dy_ref,
             dx_ref, dg_ref, recv_ref, part_ref, acc_w, stage, acc_x, send_sems, recv_sems, park_sems):
        r, k = pl.program_id(0), pl.program_id(1)
        x_, y_, c_ = lax.axis_index("x"), lax.axis_index("y"), lax.axis_index("c")
        me = 4 * x_ + 2 * y_ + c_
        flip = lambda v, bit: 1 - v if bit else v
        peer = lambda n: (flip(x_, (n >> 2) & 1), flip(y_, (n >> 1) & 1), flip(c_, n & 1))

        def park(row):
            n = ROW_OFFSETS[row]
            dst = recv_ref.at[me] if n == 0 else part_ref.at[n]
            return pltpu.make_async_copy(stage.at[row % 2], dst, park_sems.at[row % 2])

        def send(n, landing=False):
            px, py, pc = peer(n)
            dst = recv_ref.at[4 * px + 2 * py + pc] if landing else recv_ref.at[me]
            return pltpu.make_async_remote_copy(
                src_ref=part_ref.at[n], dst_ref=dst, send_sem=send_sems.at[n], recv_sem=recv_sems.at[n],
                device_id=(px, py, pc), device_id_type=pl.DeviceIdType.MESH)

        def parked(row):
            park(row).wait()
            if ROW_OFFSETS[row] >= 1:
                send(ROW_OFFSETS[row]).start()

        @pl.when(r < N_DEV)
        def _():
            @pl.when(k == 0)
            def _():
                acc_w[...] = jnp.zeros_like(acc_w)
            acc_w[...] += _dot_tn(h_ref[...], jnp.concatenate([duwa_ref[...], duwb_ref[...]], axis=1))

            for row in range(N_DEV):
                @pl.when((k == last) & (r == row))
                def _():
                    if row >= 1:
                        parked(row - 1)
                    stage[row % 2] = acc_w[...].astype(BF16)
                    park(row).start()

        @pl.when(r >= N_DEV)
        def _():
            @pl.when(k == 0)
            def _():
                acc_x[...] = jnp.zeros_like(acc_x)

            @pl.when((k == 0) & (r == N_DEV))
            def _():
                dg_ref[...] = jnp.zeros_like(dg_ref)
                parked(last)
            acc_x[...] += _dot_nt(jnp.concatenate([duxa_ref[...], duxb_ref[...]], axis=1), w_ref[...])

            @pl.when(k == last)
            def _():
                dh, xv = acc_x[...], x_ref[...]
                rs = lax.rsqrt(jnp.mean(xv * xv, axis=-1, keepdims=True) + RMS_EPS)
                dg_ref[...] += jnp.sum(dh * xv * rs, axis=0, keepdims=True)
                a = dh * g_ref[...]
                dx_ref[...] = dy_ref[...] + rs * a - xv * (rs * rs * rs) * jnp.mean(a * xv, axis=-1, keepdims=True)

        @pl.when((r == grid[0] - 1) & (k == last))
        def _():
            for n in range(1, N_DEV):
                send(n).wait_send()
            for n in range(1, N_DEV):
                send(n, landing=True).wait_recv()

    in_w = lambda r: r < N_DEV
    row_x = lambda r: jnp.maximum(r - N_DEV, 0)
    rows = lambda: pl.BlockSpec((tm, D_MODEL), lambda r, k, me: (row_x(r), 0))
    vec = lambda: pl.BlockSpec((1, D_MODEL), lambda r, k, me: (0, 0))
    block_w = lambda r, me: jnp.bitwise_xor(me[0], offset(jnp.minimum(r, last)))
    block_x = lambda r, k: jnp.where(in_w(r), 0, k)
    piece_w = lambda half: pl.BlockSpec(
        (None, tk, WIDTH), lambda r, k, me: (_du_pieces_of_block(block_w(r, me))[half], jnp.where(in_w(r), k, last), 0))
    piece_x = lambda half: pl.BlockSpec(
        (None, tm, WIDTH), lambda r, k, me: (_du_pieces_of_block(block_x(r, k))[half], row_x(r), 0))
    grid_spec = pltpu.PrefetchScalarGridSpec(
        num_scalar_prefetch=1, grid=grid,
        in_specs=[pl.BlockSpec((tk, D_MODEL), lambda r, k, me: (jnp.where(in_w(r), k, last), 0)),
                  piece_w(0), piece_w(1), piece_x(0), piece_x(1),
                  pl.BlockSpec((None, D_MODEL, COLS_PER_DEV), lambda r, k, me: (block_x(r, k), 0, 0)),
                  rows(), vec(), rows()],
        out_specs=(rows(), vec(), ANY_SPEC, ANY_SPEC),
        scratch_shapes=[pltpu.VMEM((D_MODEL, COLS_PER_DEV), F32), pltpu.VMEM((2, D_MODEL, COLS_PER_DEV), BF16),
                        pltpu.VMEM((tm, D_MODEL), F32), pltpu.SemaphoreType.DMA((N_DEV,)),
                        pltpu.SemaphoreType.DMA((N_DEV,)), pltpu.SemaphoreType.DMA((2,))])
    me = 4 * lax.axis_index("x") + 2 * lax.axis_index("y") + lax.axis_index("c")
    blocks = jax.ShapeDtypeStruct((N_DEV, D_MODEL, COLS_PER_DEV), BF16)
    dx, dg, received, _ = _pcall(
        body, name=name, grid_spec=grid_spec,
        out_shape=(jax.ShapeDtypeStruct((s, D_MODEL), F32), jax.ShapeDtypeStruct((1, D_MODEL), F32), blocks, blocks),
        compiler_params=_params(("arbitrary", "arbitrary")),
    )(jnp.reshape(me, (1,)).astype(jnp.int32), h, du, du, du, du, w_all, x, g_pre, dy)
    return dx, dg, received


def _in_proj_bwd_w(h, du, name):
    s = h.shape[0]
    tk = min(2048, s)
    n_k = s // tk

    def body(h_ref, dua_ref, dub_ref, out_ref, acc):
        k = pl.program_id(1)

        @pl.when(k == 0)
        def _():
            acc[...] = jnp.zeros_like(acc)
        acc[...] += _dot_tn(h_ref[...], jnp.concatenate([dua_ref[...], dub_ref[...]], axis=1))

        @pl.when(k == n_k - 1)
        def _():
            out_ref[...] = acc[...].astype(BF16)

    piece = lambda half: pl.BlockSpec((None, tk, WIDTH), lambda j, k: (_du_pieces_of_block(j)[half], k, 0))
    return _pcall(
        body, name=name, grid=(N_DEV, n_k),
        in_specs=[pl.BlockSpec((tk, D_MODEL), lambda j, k: (k, 0)), piece(0), piece(1)],
        out_specs=pl.BlockSpec((None, D_MODEL, COLS_PER_DEV), lambda j, k: (j, 0, 0)),
        out_shape=jax.ShapeDtypeStruct((N_DEV, D_MODEL, COLS_PER_DEV), BF16),
        scratch_shapes=[pltpu.VMEM((D_MODEL, COLS_PER_DEV), F32)],
        compiler_params=_params(("parallel", "arbitrary")),
    )(h, du, du)


def _adamw_math(g, w, m, v):
    m_new = ADAM_B1 * m + (1.0 - ADAM_B1) * g
    v_new = ADAM_B2 * v + (1.0 - ADAM_B2) * (g * g)
    m_hat = m_new / (1.0 - ADAM_B1 ** ADAM_STEP)
    v_hat = v_new / (1.0 - ADAM_B2 ** ADAM_STEP)
    delta = -ADAM_LR * (m_hat / (jnp.sqrt(v_hat) + ADAM_EPS) + ADAM_WD * w)
    return delta, m_new, v_new


def _sum_partials(p_ref):
    total = p_ref[0].astype(F32)
    for d in range(1, N_DEV):
        total = total + p_ref[d].astype(F32)
    return total


def _adamw_layers(parts0, parts1, w, m, v, name):
    _, r, c = w.shape
    tr = min(256, r)
    n_r = r // tr

    def body(p0_ref, p1_ref, w_ref, m_ref, v_ref, g_ref, d_ref, mo_ref, vo_ref):
        layer = pl.program_id(0)

        @pl.when(layer == 0)
        def _():
            g_ref[...] = _sum_partials(p0_ref)

        @pl.when(layer == 1)
        def _():
            g_ref[...] = _sum_partials(p1_ref)
        d_ref[...], mo_ref[...], vo_ref[...] = _adamw_math(g_ref[...], w_ref[...], m_ref[...], v_ref[...])

    part = lambda which: pl.BlockSpec((N_DEV, tr, c), lambda l, i: (0, jnp.where(l == which, i, 0), 0))
    par = lambda: pl.BlockSpec((None, tr, c), lambda l, i: (l, i, 0))
    out = jax.ShapeDtypeStruct(w.shape, F32)
    return _pcall(
        body, name=name, grid=(2, n_r),
        in_specs=[part(0), part(1), par(), par(), par()],
        out_specs=(par(), par(), par(), par()),
        out_shape=(out, out, out, out),
        compiler_params=_params(("arbitrary", "arbitrary")),
    )(parts0, parts1, w, m, v)


def _adamw_small(parts, w, m, v, name):
    def body(p_ref, w_ref, m_ref, v_ref, g_ref, d_ref, mo_ref, vo_ref):
        g = _sum_partials(p_ref)
        g_ref[...] = g
        d_ref[...], mo_ref[...], vo_ref[...] = _adamw_math(g, w_ref[...], m_ref[...], v_ref[...])

    out = jax.ShapeDtypeStruct(w.shape, F32)
    return _pcall(body, name=name, out_shape=(out, out, out, out), compiler_params=_params())(parts, w, m, v)


def _adamw_plain(g, w, m, v, name):
    def body(g_ref, w_ref, m_ref, v_ref, d_ref, mo_ref, vo_ref):
        d_ref[...], mo_ref[...], vo_ref[...] = _adamw_math(g_ref[...], w_ref[...], m_ref[...], v_ref[...])

    out = jax.ShapeDtypeStruct(w.shape, F32)
    return _pcall(body, name=name, out_shape=(out, out, out), compiler_params=_params())(g, w, m, v)


def _rows128(a):
    return a.reshape(-1, LANES)


SMALL_NAMES = ("pre_norm_g", "pool_w", "pool_scale", "conv_w", "conv_b", "post_norm_g")


def kernel(x, pre_norm_g, w_in, pool_w, pool_scale, conv_w, conv_b, w_branch, w_out, post_norm_g, loss_target, m_pre_norm_g, m_w_in, m_pool_w, m_pool_scale, m_conv_w, m_conv_b, m_w_branch, m_w_out, m_post_norm_g, v_pre_norm_g, v_w_in, v_pool_w, v_pool_scale, v_conv_w, v_conv_b, v_w_branch, v_w_out, v_post_norm_g):
    me = 4 * lax.axis_index("x") + 2 * lax.axis_index("y") + lax.axis_index("c")
    x0 = x[0]
    target = loss_target[0]
    conv_cols = conv_w.shape[-1]

    conv_w_pad = jnp.pad(conv_w.reshape(2 * 3, conv_cols), ((0, 2), (0, LANES - conv_cols)))
    w_in_all = [None, None]
    w_in_all[0], cw_g = _gather_two_level([w_in[0].astype(BF16), conv_w_pad], "gather_w_in_0")
    conv_w_full = cw_g[:, :6, :conv_cols].reshape(N_DEV, 2, 3, conv_cols).transpose(1, 2, 0, 3).reshape(2, 3, WIDTH)
    later_weights = ([w_in[1].astype(BF16), w_branch.astype(BF16), w_out.astype(BF16)], True)

    saved = []
    xin = x0
    for l in range(2):
        u, h = _in_proj_fwd(xin, pre_norm_g[l:l + 1], w_in_all[l], f"in_proj_fwd_{l}")
        y_pool = _pool_fwd(u, pool_w[l], pool_scale[l:l + 1], f"pool_fwd_{l}")
        y_conv = _conv_fwd(u, conv_w_full[l], conv_b[l:l + 1], f"conv_fwd_{l}")
        if l == 0:
            o_sb, y_sb, w_in_all[1], wb_g, wo_all = _sb_fwd(u, f"sb_fwd_{l}", later_weights)
            wb_all = wb_g.transpose(1, 2, 3, 0, 4).reshape(2, 3, WIDTH, D_MODEL)
        else:
            o_sb, y_sb = _sb_fwd(u, f"sb_fwd_{l}")
        if l == 0:
            xout, merged, pre = _merge_out_fwd(y_pool, y_conv, y_sb, u, wb_all, wo_all, xin, post_norm_g[l:l + 1], l,
                                               f"merge_out_fwd_{l}")
        else:
            dy, merged, pre, loss_row = _merge_out_fwd(y_pool, y_conv, y_sb, u, wb_all, wo_all, xin,
                                                       post_norm_g[l:l + 1], l, f"merge_out_fwd_{l}", target)
        saved.append((xin, u, h, y_pool, y_conv, y_sb, o_sb, merged, pre))
        xin = xout

    small = [None, None]
    recv = [None, None]
    ready = []
    for l in (1, 0):
        xl, u, h, y_pool, y_conv, y_sb, o_sb, merged, pre = saved[l]
        dmerged, dwo, dg_post = _out_proj_bwd(dy, pre, post_norm_g[l:l + 1], merged, wo_all, l, f"out_proj_bwd_{l}")
        du, dyp, dyc, dys, dwb = _merge_bwd(dmerged, y_pool, y_conv, y_sb, u, wb_all, l, f"merge_bwd_{l}")
        dwb = dwb.reshape(N_DEV, 3 * WIDTH, D_MODEL // N_DEV)
        dwo = dwo.reshape(N_DEV, D_MODEL // N_DEV, D_MODEL)
        du, dcw, dcb = _conv_bwd(u, conv_w_full[l], conv_b[l:l + 1], dyc, du, f"conv_bwd_{l}")
        du, dpw, dps = _pool_bwd(u, pool_w[l], pool_scale[l:l + 1], dyp, du, f"pool_bwd_{l}")
        small[l] = dict(pool_w=dpw, pool_scale=dps, conv_w=dcw, conv_b=dcb, post_norm_g=dg_post)
        if l == 1:
            du, dk, dv = _sb_bwd(u, o_sb, dys, du, f"sb_bwd_{l}")
        else:
            small[l]["pre_norm_g"] = jnp.zeros((1, D_MODEL), F32)
            packed = jnp.concatenate(
                [_rows128(jnp.stack([small[0][n], small[1][n]])) for n in SMALL_NAMES]
                + [jnp.pad(loss_row, ((0, 7), (0, 0)))], axis=0)
            du, dk, dv, *got, packed_all = _sb_bwd(
                u, o_sb, dys, du, f"sb_bwd_{l}", (ready + [dwb, dwo, packed], (False,) * 5 + (True,)))
            recv[1] = got[:3]
        du = lax.dynamic_update_slice(du, jnp.stack([dk, dv]).astype(BF16), (DU_SB_KV[0], 0, 0))
        if l == 1:
            dwi = _in_proj_bwd_w(h, du, f"in_proj_bwd_w_{l}")
            ready = [dwi, dwb, dwo]
            dx, dg_pre = _in_proj_bwd_x(du, w_in_all[l], xl, pre_norm_g[l:l + 1], dy, f"in_proj_bwd_x_{l}")
            small[l]["pre_norm_g"] = dg_pre
        else:
            dx, dg_pre, got_dwi = _in_proj_bwd_send(h, du, w_in_all[l], xl, pre_norm_g[l:l + 1], dy,
                                                    f"in_proj_bwd_{l}")
            recv[0] = [got_dwi] + got[3:]
        dy = dx
    grad_x = dy[None]

    (g_pre_0_all,) = _exchange([_rows128(dg_pre)], True, "gather_g_pre_0")
    packed_all = lax.dynamic_update_slice(packed_all, g_pre_0_all, (0, 0, 0))
    sizes = dict(pre_norm_g=16, pool_w=1024, pool_scale=8, conv_w=24, conv_b=8, post_norm_g=16)
    n_rows = sum(sizes.values())
    loss = jnp.sum(packed_all[:, n_rows, 0])

    given = dict(pre_norm_g=(pre_norm_g, m_pre_norm_g, v_pre_norm_g), pool_w=(pool_w, m_pool_w, v_pool_w),
                 pool_scale=(pool_scale, m_pool_scale, v_pool_scale), conv_b=(conv_b, m_conv_b, v_conv_b),
                 post_norm_g=(post_norm_g, m_post_norm_g, v_post_norm_g))
    zeros_cw = jnp.zeros((sizes["conv_w"], LANES), F32)
    pack3 = [jnp.concatenate([zeros_cw if n == "conv_w" else _rows128(given[n][k]) for n in SMALL_NAMES], axis=0)
             for k in range(3)]
    sg, sd, sm, sv = _adamw_small(packed_all[:, :n_rows], pack3[0], pack3[1], pack3[2], "adamw_small")

    def unpack(buf, name, shape):
        start = 0
        for n in SMALL_NAMES:
            if n == name:
                return buf[start:start + sizes[n]].reshape(shape)
            start += sizes[n]

    out = {}
    for n in ("pre_norm_g", "pool_w", "pool_scale", "conv_b", "post_norm_g"):
        shape = given[n][0].shape
        out[n] = tuple(unpack(b, n, shape) for b in (sg, sd, sm, sv))
    g_cw = lax.dynamic_slice_in_dim(unpack(sg, "conv_w", (2, 3, WIDTH)), me * conv_cols, conv_cols, axis=2)
    cw2 = lambda a: a.reshape(6, conv_cols)
    d_cw, m_cw, v_cw = _adamw_plain(cw2(g_cw), cw2(conv_w), cw2(m_conv_w), cw2(v_conv_w), "adamw_conv_w")
    out["conv_w"] = (g_cw,) + tuple(a.reshape(2, 3, conv_cols) for a in (d_cw, m_cw, v_cw))

    out["w_in"] = _adamw_layers(recv[0][0], recv[1][0], w_in, m_w_in, v_w_in, "adamw_w_in")
    cols = D_MODEL // N_DEV
    wb3 = lambda a: a.reshape(2, 3 * WIDTH, cols)
    out["w_branch"] = tuple(a.reshape(2, 3, WIDTH, cols) for a in _adamw_layers(
        recv[0][1], recv[1][1], wb3(w_branch), wb3(m_w_branch), wb3(v_w_branch), "adamw_w_branch"))
    out["w_out"] = _adamw_layers(recv[0][2], recv[1][2], w_out, m_w_out, v_w_out, "adamw_w_out")

    order = ("pre_norm_g", "w_in", "pool_w", "pool_scale", "conv_w", "conv_b", "w_branch", "w_out", "post_norm_g")
    return (loss, grad_x) + tuple(out[n][k] for k in range(4) for n in order)
```

```python
import functools

import jax
import jax.numpy as jnp
from jax import lax
from jax.experimental import pallas as pl
from jax.experimental.pallas import tpu as pltpu

F32 = jnp.float32
BF16 = jnp.bfloat16

N_DEV = 8
D_MODEL = 1024
WIDTH = 512
N_IN = 8192
COLS_PER_DEV = N_IN // N_DEV
HEAD_DIM = 64
LANES = 128
SB_SCALE = HEAD_DIM ** -0.5
LOG2E = 1.4426950408889634
RMS_EPS = 1e-6
POOL_HALO = 16
CONV_HALO = 8
ADAM_LR, ADAM_B1, ADAM_B2, ADAM_EPS, ADAM_WD, ADAM_STEP = 0.001, 0.9, 0.999, 1e-08, 0.01, 10
VMEM_LIMIT = 60 * 1024 * 1024

CB_POOL_V, CB_POOL_G = 0, 4
CB_CONV_X, CB_CONV_GB, CB_CONV_GC, CB_CONV_G = 8, 12, 16, 20
CB_SB_Q, CB_SB_K, CB_SB_V, CB_SB_G = 24, 28, 32, 36
MERGE_BLOCK_1024 = 5

DU_PIECES = 16
DU_MERGE = (0, 6)
DU_POOL = (6, 2)
DU_CONV = (8, 4)
DU_SB_QG = (12, 2)
DU_SB_KV = (14, 2)


def _du_pieces_of_block(j):
    first, second = 2 * (j - 5), 2 * (j - 5) + 1
    for block, (a, b) in enumerate(((6, 7), (8, 9), (10, 11), (12, 14), (15, 13))):
        first = jnp.where(j == block, a, first)
        second = jnp.where(j == block, b, second)
    return first, second


def _pcall(body, **kw):
    return pl.pallas_call(body, **kw)


def _params(sem=None):
    if sem is None:
        return pltpu.CompilerParams(vmem_limit_bytes=VMEM_LIMIT)
    return pltpu.CompilerParams(dimension_semantics=sem, vmem_limit_bytes=VMEM_LIMIT)


def _sigmoid(x):
    return 1.0 / (1.0 + jnp.exp(-x))


def _dot(a, b):
    return jnp.dot(a, b, preferred_element_type=F32)


def _dot_nt(a, b):
    return lax.dot_general(a, b, (((1,), (1,)), ((), ())), preferred_element_type=F32)


def _dot_tn(a, b):
    return lax.dot_general(a, b, (((0,), (0,)), ((), ())), preferred_element_type=F32)


def _split_bf16(x):
    hi = x.astype(BF16)
    lo = (x - hi.astype(F32)).astype(BF16)
    return hi, lo


N_PEER = N_DEV - 1
ANY_SPEC = pl.BlockSpec(memory_space=pl.ANY)


def _exchange_copies(ins, outs, send_sems, recv_sems, local_sems, gather, with_recvs=True):
    n = len(ins)
    gathers = _per_array(gather, n)
    x, y, c = lax.axis_index("x"), lax.axis_index("y"), lax.axis_index("c")
    me = 4 * x + 2 * y + c
    flip = lambda v, bit: 1 - v if bit else v
    local, sends, recvs = [], [], []
    for a in range(n):
        src = ins[a] if gathers[a] else ins[a].at[me]
        local.append(pltpu.make_async_copy(src, outs[a].at[me], local_sems.at[a]))
    for k in range(N_PEER):
        px, py, pc = flip(x, ((k + 1) >> 2) & 1), flip(y, ((k + 1) >> 1) & 1), flip(c, (k + 1) & 1)
        peer_id = 4 * px + 2 * py + pc
        for a in range(n):
            src = ins[a] if gathers[a] else ins[a].at[peer_id]
            common = dict(src_ref=src, send_sem=send_sems.at[a * N_PEER + k], recv_sem=recv_sems.at[a * N_PEER + k],
                          device_id=(px, py, pc), device_id_type=pl.DeviceIdType.MESH)
            sends.append(pltpu.make_async_remote_copy(dst_ref=outs[a].at[me], **common))
            if with_recvs:
                recvs.append(pltpu.make_async_remote_copy(dst_ref=outs[a].at[peer_id], **common))
    return local, sends, recvs


def _exchange_start(ins, outs, sems, gather):
    local, sends, _ = _exchange_copies(ins, outs, *sems, gather, with_recvs=False)
    for cp in local + sends:
        cp.start()


def _exchange_wait(ins, outs, sems, gather):
    local, sends, recvs = _exchange_copies(ins, outs, *sems, gather)
    for cp in recvs:
        cp.wait_recv()
    for cp in sends:
        cp.wait_send()
    for cp in local:
        cp.wait()


def _per_array(gather, n):
    return tuple(gather) if isinstance(gather, (tuple, list)) else (gather,) * n


def _exchange_out_shapes(arrs, gather):
    return [jax.ShapeDtypeStruct((N_DEV,) + tuple(a.shape if g else a.shape[1:]), a.dtype)
            for a, g in zip(arrs, _per_array(gather, len(arrs)))]


def _gather_two_level(arrs, name):
    n = len(arrs)

    def body(*refs):
        ins, outs = refs[:n], refs[n:2 * n]
        send_sems, recv_sems, local_sems = refs[2 * n:]
        x, y, c = lax.axis_index("x"), lax.axis_index("y"), lax.axis_index("c")
        me, sibling = (x, y, c), (x, y, 1 - c)
        chips = [(1 - x, y), (x, 1 - y), (1 - x, 1 - y)]
        slot = lambda dev: 4 * dev[0] + 2 * dev[1] + dev[2]

        def copy(a, k, block, to, src=None):
            return pltpu.make_async_remote_copy(
                src_ref=outs[a].at[slot(block)] if src is None else src, dst_ref=outs[a].at[slot(block)],
                send_sem=send_sems.at[a * N_PEER + k], recv_sem=recv_sems.at[a * N_PEER + k],
                device_id=to, device_id_type=pl.DeviceIdType.MESH)

        local = [pltpu.make_async_copy(ins[a], outs[a].at[slot(me)], local_sems.at[a]) for a in range(n)]
        first = []
        for a in range(n):
            first.append(copy(a, 0, me, sibling, src=ins[a]))
            first += [copy(a, 1 + j, me, (*chip, c), src=ins[a]) for j, chip in enumerate(chips)]
        for cp in local + first:
            cp.start()
        passed = []
        for j, chip in enumerate(chips):
            for a in range(n):
                copy(a, 1 + j, (*chip, c), me).wait_recv()
                passed.append(copy(a, 4 + j, (*chip, c), sibling))
                passed[-1].start()
        for a in range(n):
            copy(a, 0, sibling, me).wait_recv()
        for j, chip in enumerate(chips):
            for a in range(n):
                copy(a, 4 + j, (*chip, 1 - c), me).wait_recv()
        for cp in first + passed:
            cp.wait_send()
        for cp in local:
            cp.wait()

    return _pcall(
        body, name=name,
        out_shape=tuple(_exchange_out_shapes(arrs, True)),
        in_specs=[ANY_SPEC] * n, out_specs=tuple([ANY_SPEC] * n),
        scratch_shapes=_exchange_sems(n),
    )(*arrs)


def _exchange_sems(n):
    return [pltpu.SemaphoreType.DMA((n * N_PEER,)), pltpu.SemaphoreType.DMA((n * N_PEER,)),
            pltpu.SemaphoreType.DMA((n,))]


def _exchange(arrs, gather, name):
    n = len(arrs)

    def body(*refs):
        ins, outs, sems = refs[:n], refs[n:2 * n], refs[2 * n:]
        _exchange_start(ins, outs, sems, gather)
        _exchange_wait(ins, outs, sems, gather)

    return _pcall(
        body, name=name,
        out_shape=tuple(_exchange_out_shapes(arrs, gather)),
        in_specs=[ANY_SPEC] * n, out_specs=tuple([ANY_SPEC] * n),
        scratch_shapes=_exchange_sems(n),
    )(*arrs)


def _in_proj_fwd(x, g, w_all, name):
    s = x.shape[0]
    tm = min(2048, s)

    def body(x_ref, g_ref, w_ref, u_ref, h_ref, hs):
        @pl.when(pl.program_id(1) == 0)
        def _():
            xv = x_ref[...]
            r = lax.rsqrt(jnp.mean(xv * xv, axis=-1, keepdims=True) + RMS_EPS)
            hv = (xv * r * g_ref[...]).astype(BF16)
            hs[...] = hv
            h_ref[...] = hv
        u_ref[...] = _dot(hs[...], w_ref[...])

    return _pcall(
        body, name=name, grid=(s // tm, N_DEV),
        in_specs=[pl.BlockSpec((tm, D_MODEL), lambda i, j: (i, 0)),
                  pl.BlockSpec((1, D_MODEL), lambda i, j: (0, 0)),
                  pl.BlockSpec((None, D_MODEL, COLS_PER_DEV), lambda i, j: (j, 0, 0))],
        out_specs=(pl.BlockSpec((tm, COLS_PER_DEV), lambda i, j: (i, j)),
                   pl.BlockSpec((tm, D_MODEL), lambda i, j: (i, 0))),
        out_shape=(jax.ShapeDtypeStruct((s, N_IN), F32), jax.ShapeDtypeStruct((s, D_MODEL), BF16)),
        scratch_shapes=[pltpu.VMEM((tm, D_MODEL), BF16)],
        compiler_params=_params(("parallel", "arbitrary")),
    )(x, g, w_all)


def _pool_window(vs, t0, t, grp):
    ext = vs[pl.ds(t0, t + POOL_HALO), :]
    s2 = ext + pltpu.roll(ext, 1, 0)
    s4 = s2 + pltpu.roll(s2, 2, 0)
    s8 = s4 + pltpu.roll(s4, 4, 0)
    s16 = s8 + pltpu.roll(s8, 8, 0)
    sel = jnp.where(grp == 0, s2, jnp.where(grp == 1, s4, jnp.where(grp == 2, s8, s16)))
    return sel[POOL_HALO:, :], ext[POOL_HALO:, :]


def _pool_count(t0, t, grp):
    pos = t0 + lax.broadcasted_iota(jnp.int32, (t, 1), 0)
    return jnp.minimum(pos + 1, jnp.left_shift(2, grp)).astype(F32)


def _pool_fwd(u, pool_w, pool_scale, name):
    s = u.shape[0]
    t = min(256, s)

    def body(pv_ref, pg_ref, w_ref, sc_ref, y_ref, vs):
        grp = pl.program_id(0)
        vs[0:POOL_HALO, :] = jnp.zeros((POOL_HALO, LANES), F32)
        vs[POOL_HALO:, :] = pv_ref[...]
        wb = w_ref[...].astype(BF16)
        scale = sc_ref[...]

        def tile(i, carry):
            t0 = pl.multiple_of(i * t, t)
            win, v = _pool_window(vs, t0, t, grp)
            pooled = win / _pool_count(t0, t, grp) - v
            mixed = _dot(pooled.astype(BF16), wb)
            gate = pg_ref[pl.ds(t0, t), :]
            y_ref[pl.ds(t0, t), :] = (mixed * scale * (gate * _sigmoid(gate))).astype(BF16)
            return carry

        lax.fori_loop(0, s // t, tile, 0)

    return _pcall(
        body, name=name, grid=(4,),
        in_specs=[pl.BlockSpec((s, LANES), lambda g: (0, CB_POOL_V + g)),
                  pl.BlockSpec((s, LANES), lambda g: (0, CB_POOL_G + g)),
                  pl.BlockSpec((None, LANES, LANES), lambda g: (g, 0, 0)),
                  pl.BlockSpec((1, LANES), lambda g: (0, g))],
        out_specs=pl.BlockSpec((s, LANES), lambda g: (0, g)),
        out_shape=jax.ShapeDtypeStruct((s, WIDTH), BF16),
        scratch_shapes=[pltpu.VMEM((POOL_HALO + s, LANES), F32)],
        compiler_params=_params(("arbitrary",)),
    )(u, u, pool_w, pool_scale)


def _conv_taps(zs, t0, t):
    ext = zs[pl.ds(t0, t + CONV_HALO), :]
    z0 = ext[CONV_HALO:, :]
    z1 = pltpu.roll(ext, 1, 0)[CONV_HALO:, :]
    z2 = pltpu.roll(ext, 2, 0)[CONV_HALO:, :]
    return z0, z1, z2


def _conv_fwd(u, conv_w, conv_b, name):
    s = u.shape[0]
    t = min(256, s)

    def body(xc_ref, gb_ref, gc_ref, cg_ref, w_ref, b_ref, y_ref, zs):
        zs[0:CONV_HALO, :] = jnp.zeros((CONV_HALO, LANES), F32)
        zs[CONV_HALO:, :] = gc_ref[...] * xc_ref[...]
        w0, w1, w2 = w_ref[0:1, :], w_ref[1:2, :], w_ref[2:3, :]
        bias = b_ref[...]

        def tile(i, carry):
            t0 = pl.multiple_of(i * t, t)
            z0, z1, z2 = _conv_taps(zs, t0, t)
            conv = w0 * z2 + w1 * z1 + w2 * z0
            gate = cg_ref[pl.ds(t0, t), :]
            y = gb_ref[pl.ds(t0, t), :] * (conv + bias) * (gate * _sigmoid(gate))
            y_ref[pl.ds(t0, t), :] = y.astype(BF16)
            return carry

        lax.fori_loop(0, s // t, tile, 0)

    col = lambda base: pl.BlockSpec((s, LANES), lambda j: (0, base + j))
    return _pcall(
        body, name=name, grid=(4,),
        in_specs=[col(CB_CONV_X), col(CB_CONV_GB), col(CB_CONV_GC), col(CB_CONV_G),
                  pl.BlockSpec((3, LANES), lambda j: (0, j)),
                  pl.BlockSpec((1, LANES), lambda j: (0, j))],
        out_specs=pl.BlockSpec((s, LANES), lambda j: (0, j)),
        out_shape=jax.ShapeDtypeStruct((s, WIDTH), BF16),
        scratch_shapes=[pltpu.VMEM((CONV_HALO + s, LANES), F32)],
        compiler_params=_params(("arbitrary",)),
    )(u, u, u, u, conv_w, conv_b)


def _first_head_lanes(rows, width=LANES):
    lane = lax.broadcasted_iota(jnp.int32, (rows, width), 1)
    return jnp.bitwise_and(lane, LANES - 1) < HEAD_DIM


def _stack_heads(x, first):
    zero = jnp.zeros_like(x)
    return jnp.concatenate([jnp.where(first, x, zero), jnp.where(first, zero, x)], axis=0).astype(BF16)


def _causal_mask(tq, tk, copies):
    row = lax.broadcasted_iota(jnp.int32, (tq, tk), 0)
    col = lax.broadcasted_iota(jnp.int32, (tq, tk), 1)
    return jnp.concatenate([col < row] * copies, axis=0)


def _suffix_matrix(tk, inclusive, parts):
    r = lax.broadcasted_iota(jnp.int32, (parts * tk, 2 * tk), 0)
    c = lax.broadcasted_iota(jnp.int32, (parts * tk, 2 * tk), 1)
    r = jnp.bitwise_and(r, tk - 1)
    tri = (r >= c) if inclusive else (r > c)
    return jnp.where(c >= tk, 1.0, jnp.where(tri, 1.0, 0.0)).astype(BF16)


def _suffix_sums(x, m):
    hi, lo = _split_bf16(x)
    return _dot(jnp.concatenate([hi, lo], axis=1), m)


def _sb_log_terms(z, mask, m_strict):
    ls = jnp.minimum(z, 0.0) - jnp.log(1.0 + jnp.exp2(jnp.abs(z) * -LOG2E))
    lk = ls - z
    if mask is not None:
        lk = jnp.where(mask, lk, 0.0)
    return ls, _dot(lk.astype(BF16), m_strict)


SB_PAIRS = 4


def _pair_lanes(a):
    return slice(a * LANES, (a + 1) * LANES)


def _sb_fwd(u, name, xchg=None):
    s = u.shape[0]
    tq = tk = min(128, s)
    pairs = SB_PAIRS
    width = pairs * LANES
    rows = 2 * pairs * tq
    x_arrs, x_gather = xchg if xchg else ((), True)
    n_x = len(x_arrs)
    grid = (4 // pairs, s // tq)

    def body(*refs):
        q_ref, k_ref, v_ref, g_ref = refs[:4]
        x_in, refs = refs[4:4 + n_x], refs[4 + n_x:]
        o_ref, y_ref = refs[:2]
        x_out, refs = refs[2:2 + n_x], refs[2 + n_x:]
        kbf, vst, z_s, ell_s, carry_s = refs[:5]
        x_sems = refs[5:]
        i = pl.program_id(1)
        if n_x:
            @pl.when((pl.program_id(0) == 0) & (i == 0))
            def _():
                _exchange_start(x_in, x_out, x_sems, x_gather)

        @pl.when(i == 0)
        def _():
            kbf[...] = k_ref[...].astype(BF16)
            first_s = _first_head_lanes(s, width)
            vf = v_ref[...]
            vst[0] = jnp.where(first_s, vf, 0.0).astype(BF16)
            vst[1] = jnp.where(first_s, 0.0, vf).astype(BF16)

        first = _first_head_lanes(tq)
        mask = _causal_mask(tq, tk, 2 * pairs)
        m_strict = _suffix_matrix(tk, False, 1)
        qcat = jnp.concatenate([_stack_heads(q_ref[:, _pair_lanes(a)] * SB_SCALE, first) for a in range(pairs)],
                               axis=0)

        def scores(b):
            off = pl.multiple_of(jnp.maximum(b, 0) * tk, tk)
            z_s[...] = jnp.concatenate(
                [_dot_nt(qcat[a * 2 * tq:(a + 1) * 2 * tq], kbf[pl.ds(off, tk), _pair_lanes(a)])
                 for a in range(pairs)], axis=0)

        def log_weights(m):
            ls, cs = _sb_log_terms(z_s[...], m, m_strict)
            carry = carry_s[...]
            ell_s[...] = ls + cs[:, :tk] + carry
            carry_s[...] = carry + cs[:, tk:]

        def consume(b, accs, m):
            w = jnp.exp(ell_s[...])
            if m is not None:
                w = jnp.where(m, w, 0.0)
            wb = w.astype(BF16)
            off = pl.multiple_of(b * tk, tk)
            new = []
            for a in range(pairs):
                r0 = a * 2 * tq
                wcat = jnp.concatenate([wb[r0:r0 + tq], wb[r0 + tq:r0 + 2 * tq]], axis=1)
                vcat = jnp.concatenate([vst[0, pl.ds(off, tk), _pair_lanes(a)], vst[1, pl.ds(off, tk), _pair_lanes(a)]],
                                       axis=0)
                new.append(accs[a] + _dot(wcat, vcat))
            return tuple(new)

        carry_s[...] = jnp.zeros((rows, tk), F32)
        scores(i)
        log_weights(mask)
        scores(i - 1)
        accs = consume(i, tuple(jnp.zeros((tq, LANES), F32) for _ in range(pairs)), mask)
        log_weights(None)
        scores(i - 2)

        def step(n, accs):
            accs = consume(i - n, accs, None)
            log_weights(None)
            scores(i - n - 2)
            return accs

        accs = lax.fori_loop(1, i + 1, step, accs)
        o = jnp.concatenate(accs, axis=1)
        o_ref[...] = o
        gate = g_ref[...]
        y_ref[...] = (o * (gate * _sigmoid(gate))).astype(BF16)
        if n_x:
            @pl.when((pl.program_id(0) == grid[0] - 1) & (i == grid[1] - 1))
            def _():
                _exchange_wait(x_in, x_out, x_sems, x_gather)

    base = lambda cb: cb // pairs
    qblk = lambda cb: pl.BlockSpec((tq, width), lambda p, i: (i, base(cb) + p))
    full = lambda cb: pl.BlockSpec((s, width), lambda p, i: (0, base(cb) + p), pipeline_mode=pl.Buffered(1))
    state = pltpu.VMEM((rows, tk), F32)
    return _pcall(
        body, name=name, grid=grid,
        in_specs=[qblk(CB_SB_Q), full(CB_SB_K), full(CB_SB_V), qblk(CB_SB_G)] + [ANY_SPEC] * n_x,
        out_specs=(qblk(0), qblk(0)) + (ANY_SPEC,) * n_x,
        out_shape=(jax.ShapeDtypeStruct((s, WIDTH), F32), jax.ShapeDtypeStruct((s, WIDTH), BF16))
        + tuple(_exchange_out_shapes(x_arrs, x_gather)),
        scratch_shapes=[pltpu.VMEM((s, width), BF16), pltpu.VMEM((2, s, width), BF16), state, state, state]
        + (_exchange_sems(n_x) if n_x else []),
        compiler_params=_params(("arbitrary", "arbitrary")),
    )(u, u, u, u, *x_arrs)


def _merge_out_fwd(y_pool, y_conv, y_sb, u, wb_all, wo_all, x, g_post, layer, name, target=None):
    s = x.shape[0]
    tm = min(512, s)
    n_tiles = s // tm
    with_loss = target is not None

    def body(yp, yc, ys, m0, m1, m2, wb_ref, wo_ref, x_ref, g_ref, *rest):
        merged = jnp.zeros((tm, D_MODEL), F32)
        for n, (y_ref, m_ref) in enumerate(((yp, m0), (yc, m1), (ys, m2))):
            merged = merged + _sigmoid(m_ref[...]) * _dot(y_ref[...], wb_ref[n])
        mb = merged.astype(BF16)
        pre = _dot(mb, wo_ref[...].reshape(D_MODEL, D_MODEL))
        r = lax.rsqrt(jnp.mean(pre * pre, axis=-1, keepdims=True) + RMS_EPS)
        y = x_ref[...] + pre * r * g_ref[...]
        if not with_loss:
            out_ref, merged_ref, pre_ref = rest
            out_ref[...] = y
        else:
            t_ref, out_ref, merged_ref, pre_ref, loss_ref, acc = rest
            i = pl.program_id(0)

            @pl.when(i == 0)
            def _():
                acc[...] = jnp.zeros_like(acc)
            err = y - t_ref[...]
            out_ref[...] = err / D_MODEL
            acc[...] += jnp.sum(err * err, axis=0, keepdims=True)

            @pl.when(i == n_tiles - 1)
            def _():
                total = jnp.sum(acc[...], axis=1, keepdims=True) * (0.5 / D_MODEL)
                loss_ref[...] = jnp.broadcast_to(total, (1, LANES))
        merged_ref[...] = mb
        pre_ref[...] = pre

    rows = lambda w: pl.BlockSpec((tm, w), lambda i: (i, 0))
    merge = lambda n: pl.BlockSpec((tm, D_MODEL), lambda i: (i, MERGE_BLOCK_1024 + n))
    out_specs = (rows(D_MODEL), rows(D_MODEL), rows(D_MODEL))
    out_shape = (jax.ShapeDtypeStruct((s, D_MODEL), F32), jax.ShapeDtypeStruct((s, D_MODEL), BF16),
                 jax.ShapeDtypeStruct((s, D_MODEL), F32))
    if with_loss:
        out_specs += (pl.BlockSpec((1, LANES), lambda i: (0, 0)),)
        out_shape += (jax.ShapeDtypeStruct((1, LANES), F32),)
    return _pcall(
        body, name=name, grid=(n_tiles,),
        in_specs=[rows(WIDTH), rows(WIDTH), rows(WIDTH), merge(0), merge(1), merge(2),
                  pl.BlockSpec((None, 3, WIDTH, D_MODEL), lambda i: (layer, 0, 0, 0)),
                  pl.BlockSpec((N_DEV, None, D_MODEL // N_DEV, D_MODEL), lambda i: (0, layer, 0, 0)),
                  rows(D_MODEL), pl.BlockSpec((1, D_MODEL), lambda i: (0, 0))] + ([rows(D_MODEL)] if with_loss else []),
        out_specs=out_specs, out_shape=out_shape,
        scratch_shapes=[pltpu.VMEM((1, D_MODEL), F32)] if with_loss else [],
        compiler_params=_params(("arbitrary",)),
    )(y_pool, y_conv, y_sb, u, u, u, wb_all, wo_all, x, g_post, *([target] if with_loss else []))


def _out_proj_bwd(dy, pre, g_post, merged, wo_all, layer, name):
    s = dy.shape[0]
    tm = min(1024, s)
    n_tiles = s // tm

    def body(dy_ref, pre_ref, g_ref, mg_ref, wo_ref, dm_ref, dwo_ref, dg_ref, acc):
        i = pl.program_id(0)

        @pl.when(i == 0)
        def _():
            acc[...] = jnp.zeros_like(acc)
            dg_ref[...] = jnp.zeros_like(dg_ref)
        dyv, pre_v = dy_ref[...], pre_ref[...]
        r = lax.rsqrt(jnp.mean(pre_v * pre_v, axis=-1, keepdims=True) + RMS_EPS)
        dg_ref[...] += jnp.sum(dyv * pre_v * r, axis=0, keepdims=True)
        a = dyv * g_ref[...]
        dpre = r * a - pre_v * (r * r * r) * jnp.mean(a * pre_v, axis=-1, keepdims=True)
        db = dpre.astype(BF16)
        acc[...] += _dot_tn(mg_ref[...], db)
        dm_ref[...] = _dot_nt(db, wo_ref[...].reshape(D_MODEL, D_MODEL))

        @pl.when(i == n_tiles - 1)
        def _():
            dwo_ref[...] = acc[...].astype(BF16)

    rows = lambda: pl.BlockSpec((tm, D_MODEL), lambda i: (i, 0))
    return _pcall(
        body, name=name, grid=(n_tiles,),
        in_specs=[rows(), rows(), pl.BlockSpec((1, D_MODEL), lambda i: (0, 0)), rows(),
                  pl.BlockSpec((N_DEV, None, D_MODEL // N_DEV, D_MODEL), lambda i: (0, layer, 0, 0))],
        out_specs=(rows(), pl.BlockSpec((D_MODEL, D_MODEL), lambda i: (0, 0)),
                   pl.BlockSpec((1, D_MODEL), lambda i: (0, 0))),
        out_shape=(jax.ShapeDtypeStruct((s, D_MODEL), F32), jax.ShapeDtypeStruct((D_MODEL, D_MODEL), BF16),
                   jax.ShapeDtypeStruct((1, D_MODEL), F32)),
        scratch_shapes=[pltpu.VMEM((D_MODEL, D_MODEL), F32)],
        compiler_params=_params(("arbitrary",)),
    )(dy, pre, g_post, merged, wo_all)


def _merge_bwd(dmerged, y_pool, y_conv, y_sb, u, wb_all, layer, name):
    s = dmerged.shape[0]
    tm = min(512, s)
    n_tiles = s // tm
    cols = D_MODEL // N_DEV

    def body(dm_ref, yp, yc, ys, m0, m1, m2, wb_ref, du_ref, dyp, dyc, dys, dwb_ref, acc):
        i = pl.program_id(0)

        @pl.when(i == 0)
        def _():
            acc[...] = jnp.zeros_like(acc)
        dm = dm_ref[...]
        for n, (y_ref, m_ref, dy_ref) in enumerate(((yp, m0, dyp), (yc, m1, dyc), (ys, m2, dys))):
            yv = y_ref[...]
            wb = wb_ref[n]
            gate = _sigmoid(m_ref[...])
            proj = _dot(yv, wb)
            dgate = (dm * proj * gate * (1.0 - gate)).astype(BF16)
            du_ref[2 * n] = dgate[:, :WIDTH]
            du_ref[2 * n + 1] = dgate[:, WIDTH:]
            dproj = (dm * gate).astype(BF16)
            acc[n] += _dot_tn(yv, dproj)
            dy_ref[...] = _dot_nt(dproj, wb)

        @pl.when(i == n_tiles - 1)
        def _():
            for j in range(N_DEV):
                for n in range(3):
                    dwb_ref[j, n] = acc[n, :, j * cols:(j + 1) * cols].astype(BF16)

    rows = lambda w: pl.BlockSpec((tm, w), lambda i: (i, 0))
    merge = lambda n: pl.BlockSpec((tm, D_MODEL), lambda i: (i, MERGE_BLOCK_1024 + n))
    return _pcall(
        body, name=name, grid=(n_tiles,),
        in_specs=[rows(D_MODEL), rows(WIDTH), rows(WIDTH), rows(WIDTH), merge(0), merge(1), merge(2),
                  pl.BlockSpec((None, 3, WIDTH, D_MODEL), lambda i: (layer, 0, 0, 0))],
        out_specs=(pl.BlockSpec((DU_MERGE[1], tm, WIDTH), lambda i: (DU_MERGE[0] // DU_MERGE[1], i, 0)),
                   rows(WIDTH), rows(WIDTH), rows(WIDTH),
                   pl.BlockSpec((N_DEV, 3, WIDTH, cols), lambda i: (0, 0, 0, 0))),
        out_shape=(jax.ShapeDtypeStruct((DU_PIECES, s, WIDTH), BF16),
                   jax.ShapeDtypeStruct((s, WIDTH), F32), jax.ShapeDtypeStruct((s, WIDTH), F32),
                   jax.ShapeDtypeStruct((s, WIDTH), F32),
                   jax.ShapeDtypeStruct((N_DEV, 3, WIDTH, cols), BF16)),
        scratch_shapes=[pltpu.VMEM((3, WIDTH, D_MODEL), F32)],
        compiler_params=_params(("arbitrary",)),
    )(dmerged, y_pool, y_conv, y_sb, u, u, u, wb_all)


def _sb_bwd(u, o, dys, du, name, xchg=None):
    s = u.shape[0]
    tq = tk = min(128, s)
    pairs = SB_PAIRS
    width = pairs * LANES
    assert width == WIDTH
    rows = 2 * pairs * tq
    pair_rows = lambda a: slice(a * 2 * tq, (a + 1) * 2 * tq)

    x_arrs, x_gather = xchg if xchg else ((), True)
    n_x = len(x_arrs)
    grid = (4 // pairs, s // tq)

    def body(*refs):
        q_ref, k_ref, v_ref, g_ref, o_ref, dys_ref = refs[:6]
        x_in, refs = refs[7:7 + n_x], refs[7 + n_x:]
        du_ref, dk_ref, dv_ref = refs[:3]
        dq_ref, dg_ref = du_ref.at[0], du_ref.at[1]
        x_out, refs = refs[3:3 + n_x], refs[3 + n_x:]
        kbf, vbf, kst, z_s, ell_s, ls_s, cl_s, wb_s, g_s, bef_s, cg_s, beta_s = refs[:12]
        x_sems = refs[12:]
        i = pl.program_id(1)
        if n_x:
            @pl.when((pl.program_id(0) == 0) & (i == 0))
            def _():
                _exchange_start(x_in, x_out, x_sems, x_gather)

        @pl.when(i == 0)
        def _():
            dk_ref[...] = jnp.zeros_like(dk_ref)
            dv_ref[...] = jnp.zeros_like(dv_ref)
            kf = k_ref[...]
            kbf[...] = kf.astype(BF16)
            vbf[...] = v_ref[...].astype(BF16)
            first_s = _first_head_lanes(s, width)
            kst[0] = jnp.where(first_s, kf, 0.0).astype(BF16)
            kst[1] = jnp.where(first_s, 0.0, kf).astype(BF16)

        first = _first_head_lanes(tq)
        mask = _causal_mask(tq, tk, 2 * pairs)
        m_strict = _suffix_matrix(tk, False, 1)
        m_incl = _suffix_matrix(tk, True, 2)

        gate = g_ref[...]
        sg = _sigmoid(gate)
        dy = dys_ref[...]
        ov = o_ref[...]
        dg_ref[...] = (dy * ov * (sg * (1.0 + gate * (1.0 - sg)))).astype(BF16)
        do = (dy * (gate * sg)).astype(BF16)
        prod = do.astype(F32) * ov
        row_sum = lambda v: jnp.broadcast_to(jnp.sum(v, axis=1, keepdims=True), (tq, tk))
        dsum, docat, qcat = [], [], []
        for a in range(pairs):
            pa = prod[:, _pair_lanes(a)]
            dsum += [row_sum(jnp.where(first, pa, 0.0)), row_sum(jnp.where(first, 0.0, pa))]
            docat.append(_stack_heads(do[:, _pair_lanes(a)], first))
            qcat.append(_stack_heads(q_ref[:, _pair_lanes(a)] * SB_SCALE, first))
        dsum = jnp.concatenate(dsum, axis=0)

        def block_start(b):
            return pl.multiple_of(jnp.maximum(b, 0) * tk, tk)

        def scores(b):
            off = block_start(b)
            z_s[...] = jnp.concatenate([_dot_nt(qcat[a], kbf[pl.ds(off, tk), _pair_lanes(a)]) for a in range(pairs)],
                                       axis=0)

        def log_weights(m):
            ls, cs = _sb_log_terms(z_s[...], m, m_strict)
            cl = cl_s[...]
            ell_s[...] = ls + cs[:, :tk] + cl
            cl_s[...] = cl + cs[:, tk:]
            ls_s[...] = ls

        def weights(b, m):
            off = block_start(b)
            dwt = jnp.concatenate([_dot_nt(docat[a], vbf[pl.ds(off, tk), _pair_lanes(a)]) for a in range(pairs)],
                                  axis=0)
            w = jnp.exp(ell_s[...])
            if m is not None:
                w = jnp.where(m, w, 0.0)
            wb = w.astype(BF16)
            g = dwt * wb.astype(F32)
            gs = _suffix_sums(g, m_incl)
            cg = cg_s[...]
            beta = jnp.exp(ls_s[...])
            wb_s[...] = wb
            beta_s[...] = beta
            g_s[...] = g * (1.0 - beta)
            bef_s[...] = gs[:, :tk] + cg
            cg_s[...] = cg + gs[:, tk:]

        def grads(b, dqs, m):
            dz = g_s[...] - beta_s[...] * (dsum - bef_s[...])
            if m is not None:
                dz = jnp.where(m, dz, 0.0)
            dzb = dz.astype(BF16)
            wb = wb_s[...]
            off = pl.multiple_of(b * tk, tk)
            new = []
            for a in range(pairs):
                r0 = a * 2 * tq
                kcat = jnp.concatenate([kst[0, pl.ds(off, tk), _pair_lanes(a)], kst[1, pl.ds(off, tk), _pair_lanes(a)]],
                                       axis=0)
                new.append(dqs[a] + _dot(jnp.concatenate([dzb[r0:r0 + tq], dzb[r0 + tq:r0 + 2 * tq]], axis=1), kcat))
                dk_ref[pl.ds(off, tk), _pair_lanes(a)] += _dot_tn(dzb[pair_rows(a)], qcat[a])
                dv_ref[pl.ds(off, tk), _pair_lanes(a)] += _dot_tn(wb[pair_rows(a)], docat[a])
            return tuple(new)

        zero = jnp.zeros((rows, tk), F32)
        cl_s[...] = zero
        cg_s[...] = zero
        scores(i)
        log_weights(mask)
        scores(i - 1)
        weights(i, mask)
        log_weights(None)
        scores(i - 2)
        dqs = grads(i, tuple(jnp.zeros((tq, LANES), F32) for _ in range(pairs)), mask)
        weights(i - 1, None)
        log_weights(None)
        scores(i - 3)

        def step(n, dqs):
            dqs = grads(i - n, dqs, None)
            weights(i - n - 1, None)
            log_weights(None)
            scores(i - n - 3)
            return dqs

        dqs = lax.fori_loop(1, i + 1, step, dqs)
        dq_ref[...] = (jnp.concatenate(dqs, axis=1) * SB_SCALE).astype(BF16)
        if n_x:
            @pl.when((pl.program_id(0) == grid[0] - 1) & (i == grid[1] - 1))
            def _():
                _exchange_wait(x_in, x_out, x_sems, x_gather)

    base = lambda cb: cb // pairs
    qblk = lambda cb: pl.BlockSpec((tq, width), lambda p, i: (i, base(cb) + p))
    full = lambda cb: pl.BlockSpec((s, width), lambda p, i: (0, base(cb) + p), pipeline_mode=pl.Buffered(1))
    state = pltpu.VMEM((rows, tk), F32)
    return _pcall(
        body, name=name, grid=grid,
        in_specs=[qblk(CB_SB_Q), full(CB_SB_K), full(CB_SB_V), qblk(CB_SB_G), qblk(0), qblk(0), ANY_SPEC]
        + [ANY_SPEC] * n_x,
        out_specs=(pl.BlockSpec((DU_SB_QG[1], tq, WIDTH), lambda p, i: (DU_SB_QG[0] // DU_SB_QG[1], i, 0)),
                   full(0), full(0)) + (ANY_SPEC,) * n_x,
        out_shape=(jax.ShapeDtypeStruct(du.shape, du.dtype), jax.ShapeDtypeStruct((s, WIDTH), F32),
                   jax.ShapeDtypeStruct((s, WIDTH), F32)) + tuple(_exchange_out_shapes(x_arrs, x_gather)),
        input_output_aliases={6: 0},
        scratch_shapes=[pltpu.VMEM((s, width), BF16), pltpu.VMEM((s, width), BF16), pltpu.VMEM((2, s, width), BF16),
                        state, state, state, state, pltpu.VMEM((rows, tk), BF16),
                        state, state, state, state] + (_exchange_sems(n_x) if n_x else []),
        compiler_params=_params(("arbitrary", "arbitrary")),
    )(u, u, u, u, o, dys, du, *x_arrs)


def _conv_bwd(u, conv_w, conv_b, dyc, du, name):
    s = u.shape[0]
    t = min(256, s)
    n_tiles = s // t

    def body(xc_ref, gb_ref, gc_ref, cg_ref, w_ref, b_ref, dy_ref, du_in, du_ref, dw_ref, db_ref, zs, ds):
        dxc_ref, dgb_ref, dgc_ref, dcg_ref = (du_ref.at[p] for p in range(4))
        zs[0:CONV_HALO, :] = jnp.zeros((CONV_HALO, LANES), F32)
        zs[CONV_HALO:, :] = gc_ref[...] * xc_ref[...]
        ds[s:, :] = jnp.zeros((CONV_HALO, LANES), F32)
        w0, w1, w2 = w_ref[0:1, :], w_ref[1:2, :], w_ref[2:3, :]
        bias = b_ref[...]

        def first(i, sums):
            t0 = pl.multiple_of(i * t, t)
            z0, z1, z2 = _conv_taps(zs, t0, t)
            pre = w0 * z2 + w1 * z1 + w2 * z0 + bias
            gate = cg_ref[pl.ds(t0, t), :]
            sg = _sigmoid(gate)
            gb = gb_ref[pl.ds(t0, t), :]
            dy = dy_ref[pl.ds(t0, t), :]
            dcg_ref[pl.ds(t0, t), :] = (dy * gb * pre * (sg * (1.0 + gate * (1.0 - sg)))).astype(BF16)
            dgb_ref[pl.ds(t0, t), :] = (dy * pre * (gate * sg)).astype(BF16)
            dc = dy * gb * (gate * sg)
            ds[pl.ds(t0, t), :] = dc
            red = lambda v: jnp.sum(v, axis=0, keepdims=True)
            return (sums[0] + red(dc * z2), sums[1] + red(dc * z1), sums[2] + red(dc * z0), sums[3] + red(dc))

        zrow = jnp.zeros((1, LANES), F32)
        sw0, sw1, sw2, sb = lax.fori_loop(0, n_tiles, first, (zrow, zrow, zrow, zrow))
        dw_ref[0:1, :] = sw0
        dw_ref[1:2, :] = sw1
        dw_ref[2:3, :] = sw2
        db_ref[...] = sb

        def second(i, carry):
            t0 = pl.multiple_of(i * t, t)
            ext = ds[pl.ds(t0, t + CONV_HALO), :]
            n = t + CONV_HALO
            d0 = ext[:t, :]
            d1 = pltpu.roll(ext, n - 1, 0)[:t, :]
            d2 = pltpu.roll(ext, n - 2, 0)[:t, :]
            dz = w2 * d0 + w1 * d1 + w0 * d2
            dgc_ref[pl.ds(t0, t), :] = (dz * xc_ref[pl.ds(t0, t), :]).astype(BF16)
            dxc_ref[pl.ds(t0, t), :] = (dz * gc_ref[pl.ds(t0, t), :]).astype(BF16)
            return carry

        lax.fori_loop(0, n_tiles, second, 0)

    col = lambda base: pl.BlockSpec((s, LANES), lambda j: (0, base + j))
    first, count = DU_CONV
    return _pcall(
        body, name=name, grid=(4,),
        in_specs=[col(CB_CONV_X), col(CB_CONV_GB), col(CB_CONV_GC), col(CB_CONV_G),
                  pl.BlockSpec((3, LANES), lambda j: (0, j)), pl.BlockSpec((1, LANES), lambda j: (0, j)), col(0),
                  ANY_SPEC],
        out_specs=(pl.BlockSpec((count, s, LANES), lambda j: (first // count, 0, j)),
                   pl.BlockSpec((3, LANES), lambda j: (0, j)), pl.BlockSpec((1, LANES), lambda j: (0, j))),
        out_shape=(jax.ShapeDtypeStruct(du.shape, du.dtype),
                   jax.ShapeDtypeStruct((3, WIDTH), F32), jax.ShapeDtypeStruct((1, WIDTH), F32)),
        scratch_shapes=[pltpu.VMEM((CONV_HALO + s, LANES), F32), pltpu.VMEM((s + CONV_HALO, LANES), F32)],
        input_output_aliases={7: 0},
        compiler_params=_params(("arbitrary",)),
    )(u, u, u, u, conv_w, conv_b, dyc, du)


def _pool_bwd(u, pool_w, pool_scale, dyp, du, name):
    s = u.shape[0]
    t = min(256, s)
    n_tiles = s // t

    def body(pv_ref, pg_ref, w_ref, sc_ref, dy_ref, du_in, du_ref, dw_ref, dsc_ref, vs, es, dps):
        dpv_ref, dpg_ref = du_ref.at[0], du_ref.at[1]
        grp = pl.program_id(0)
        vs[0:POOL_HALO, :] = jnp.zeros((POOL_HALO, LANES), F32)
        vs[POOL_HALO:, :] = pv_ref[...]
        es[s:, :] = jnp.zeros((POOL_HALO, LANES), F32)
        wb = w_ref[...].astype(BF16)
        scale = sc_ref[...]

        def first(i, sums):
            dw, dsc = sums
            t0 = pl.multiple_of(i * t, t)
            win, v = _pool_window(vs, t0, t, grp)
            cnt = _pool_count(t0, t, grp)
            pb = (win / cnt - v).astype(BF16)
            mixed = _dot(pb, wb)
            gate = pg_ref[pl.ds(t0, t), :]
            sg = _sigmoid(gate)
            dy = dy_ref[pl.ds(t0, t), :]
            dpg_ref[pl.ds(t0, t), :] = (dy * (mixed * scale) * (sg * (1.0 + gate * (1.0 - sg)))).astype(BF16)
            dms = dy * (gate * sg)
            dsc = dsc + jnp.sum(dms * mixed, axis=0, keepdims=True)
            dmb = (dms * scale).astype(BF16)
            dw = dw + _dot_tn(pb, dmb)
            dpooled = _dot_nt(dmb, wb)
            dps[pl.ds(t0, t), :] = dpooled
            es[pl.ds(t0, t), :] = dpooled / cnt
            return dw, dsc

        dw, dsc = lax.fori_loop(0, n_tiles, first, (jnp.zeros((LANES, LANES), F32), jnp.zeros((1, LANES), F32)))
        dw_ref[...] = dw
        dsc_ref[...] = dsc

        def second(i, carry):
            t0 = pl.multiple_of(i * t, t)
            ext = es[pl.ds(t0, t + POOL_HALO), :]
            n = t + POOL_HALO
            f2 = ext + pltpu.roll(ext, n - 1, 0)
            f4 = f2 + pltpu.roll(f2, n - 2, 0)
            f8 = f4 + pltpu.roll(f4, n - 4, 0)
            f16 = f8 + pltpu.roll(f8, n - 8, 0)
            sel = jnp.where(grp == 0, f2, jnp.where(grp == 1, f4, jnp.where(grp == 2, f8, f16)))
            dpv_ref[pl.ds(t0, t), :] = (sel[:t, :] - dps[pl.ds(t0, t), :]).astype(BF16)
            return carry

        lax.fori_loop(0, n_tiles, second, 0)

    col = lambda base: pl.BlockSpec((s, LANES), lambda g: (0, base + g))
    first, count = DU_POOL
    return _pcall(
        body, name=name, grid=(4,),
        in_specs=[col(CB_POOL_V), col(CB_POOL_G), pl.BlockSpec((None, LANES, LANES), lambda g: (g, 0, 0)),
                  pl.BlockSpec((1, LANES), lambda g: (0, g)), col(0), ANY_SPEC],
        out_specs=(pl.BlockSpec((count, s, LANES), lambda g: (first // count, 0, g)),
                   pl.BlockSpec((None, LANES, LANES), lambda g: (g, 0, 0)),
                   pl.BlockSpec((1, LANES), lambda g: (0, g))),
        out_shape=(jax.ShapeDtypeStruct(du.shape, du.dtype),
                   jax.ShapeDtypeStruct((4, LANES, LANES), F32), jax.ShapeDtypeStruct((1, WIDTH), F32)),
        scratch_shapes=[pltpu.VMEM((POOL_HALO + s, LANES), F32), pltpu.VMEM((s + POOL_HALO, LANES), F32),
                        pltpu.VMEM((s, LANES), F32)],
        input_output_aliases={5: 0},
        compiler_params=_params(("arbitrary",)),
    )(u, u, pool_w, pool_scale, dyp, du)


def _in_proj_bwd_x(du, w_all, x, g_pre, dy, name):
    s = x.shape[0]
    tm = min(1024, s)
    grid = (s // tm, N_DEV)

    def body(dua_ref, dub_ref, w_ref, x_ref, g_ref, dy_ref, dx_ref, dg_ref, acc):
        i, k = pl.program_id(0), pl.program_id(1)

        @pl.when(k == 0)
        def _():
            acc[...] = jnp.zeros_like(acc)

        @pl.when((k == 0) & (i == 0))
        def _():
            dg_ref[...] = jnp.zeros_like(dg_ref)
        acc[...] += _dot_nt(jnp.concatenate([dua_ref[...], dub_ref[...]], axis=1), w_ref[...])

        @pl.when(k == N_DEV - 1)
        def _():
            dh, xv = acc[...], x_ref[...]
            r = lax.rsqrt(jnp.mean(xv * xv, axis=-1, keepdims=True) + RMS_EPS)
            dg_ref[...] += jnp.sum(dh * xv * r, axis=0, keepdims=True)
            a = dh * g_ref[...]
            dx_ref[...] = dy_ref[...] + r * a - xv * (r * r * r) * jnp.mean(a * xv, axis=-1, keepdims=True)

    rows = lambda: pl.BlockSpec((tm, D_MODEL), lambda i, k: (i, 0))
    vec = lambda: pl.BlockSpec((1, D_MODEL), lambda i, k: (0, 0))
    piece = lambda half: pl.BlockSpec((None, tm, WIDTH), lambda i, k: (_du_pieces_of_block(k)[half], i, 0))
    return _pcall(
        body, name=name, grid=grid,
        in_specs=[piece(0), piece(1), pl.BlockSpec((None, D_MODEL, COLS_PER_DEV), lambda i, k: (k, 0, 0)),
                  rows(), vec(), rows()],
        out_specs=(rows(), vec()),
        out_shape=(jax.ShapeDtypeStruct((s, D_MODEL), F32), jax.ShapeDtypeStruct((1, D_MODEL), F32)),
        scratch_shapes=[pltpu.VMEM((tm, D_MODEL), F32)],
        compiler_params=_params(("arbitrary", "arbitrary")),
    )(du, du, w_all, x, g_pre, dy)


ROW_OFFSETS = (6, 7, 2, 4, 3, 5, 0, 1)


def _in_proj_bwd_send(h, du, w_all, x, g_pre, dy, name):
    s = x.shape[0]
    tk = s // N_DEV
    tm = min(1024, s)
    n_i = s // tm
    grid = (N_DEV + n_i, N_DEV)
    last = N_DEV - 1
    def offset(row):
        return functools.reduce(lambda acc, rn: jnp.where(row == rn[0], rn[1], acc), enumerate(ROW_OFFSETS), 0)

    def body(me_ref, h_ref, duwa_ref, duwb_ref, duxa_ref, duxb_ref, w_ref, x_ref, g_ref, dy_ref,
             dx_ref, dg_ref, recv_ref, part_ref, acc_w, stage, acc_x, send_sems, recv_sems, park_sems):
        r, k = pl.program_id(0), pl.program_id(1)
        x_, y_, c_ = lax.axis_index("x"), lax.axis_index("y"), lax.axis_index("c")
        me = 4 * x_ + 2 * y_ + c_
        flip = lambda v, bit: 1 - v if bit else v
        peer = lambda n: (flip(x_, (n >> 2) & 1), flip(y_, (n >> 1) & 1), flip(c_, n & 1))

        def park(row):
            n = ROW_OFFSETS[row]
            dst = recv_ref.at[me] if n == 0 else part_ref.at[n]
            return pltpu.make_async_copy(stage.at[row % 2], dst, park_sems.at[row % 2])

        def send(n, landing=False):
            px, py, pc = peer(n)
            dst = recv_ref.at[4 * px + 2 * py + pc] if landing else recv_ref.at[me]
            return pltpu.make_async_remote_copy(
                src_ref=part_ref.at[n], dst_ref=dst, send_sem=send_sems.at[n], recv_sem=recv_sems.at[n],
                device_id=(px, py, pc), device_id_type=pl.DeviceIdType.MESH)

        def parked(row):
            park(row).wait()
            if ROW_OFFSETS[row] >= 1:
                send(ROW_OFFSETS[row]).start()

        @pl.when(r < N_DEV)
        def _():
            @pl.when(k == 0)
            def _():
                acc_w[...] = jnp.zeros_like(acc_w)
            acc_w[...] += _dot_tn(h_ref[...], jnp.concatenate([duwa_ref[...], duwb_ref[...]], axis=1))

            for row in range(N_DEV):
                @pl.when((k == last) & (r == row))
                def _():
                    if row >= 1:
                        parked(row - 1)
                    stage[row % 2] = acc_w[...].astype(BF16)
                    park(row).start()

        @pl.when(r >= N_DEV)
        def _():
            @pl.when(k == 0)
            def _():
                acc_x[...] = jnp.zeros_like(acc_x)

            @pl.when((k == 0) & (r == N_DEV))
            def _():
                dg_ref[...] = jnp.zeros_like(dg_ref)
                parked(last)
            acc_x[...] += _dot_nt(jnp.concatenate([duxa_ref[...], duxb_ref[...]], axis=1), w_ref[...])

            @pl.when(k == last)
            def _():
                dh, xv = acc_x[...], x_ref[...]
                rs = lax.rsqrt(jnp.mean(xv * xv, axis=-1, keepdims=True) + RMS_EPS)
                dg_ref[...] += jnp.sum(dh * xv * rs, axis=0, keepdims=True)
                a = dh * g_ref[...]
                dx_ref[...] = dy_ref[...] + rs * a - xv * (rs * rs * rs) * jnp.mean(a * xv, axis=-1, keepdims=True)

        @pl.when((r == grid[0] - 1) & (k == last))
        def _():
            for n in range(1, N_DEV):
                send(n).wait_send()
            for n in range(1, N_DEV):
                send(n, landing=True).wait_recv()

    in_w = lambda r: r < N_DEV
    row_x = lambda r: jnp.maximum(r - N_DEV, 0)
    rows = lambda: pl.BlockSpec((tm, D_MODEL), lambda r, k, me: (row_x(r), 0))
    vec = lambda: pl.BlockSpec((1, D_MODEL), lambda r, k, me: (0, 0))
    block_w = lambda r, me: jnp.bitwise_xor(me[0], offset(jnp.minimum(r, last)))
    block_x = lambda r, k: jnp.where(in_w(r), 0, k)
    piece_w = lambda half: pl.BlockSpec(
        (None, tk, WIDTH), lambda r, k, me: (_du_pieces_of_block(block_w(r, me))[half], jnp.where(in_w(r), k, last), 0))
    piece_x = lambda half: pl.BlockSpec(
        (None, tm, WIDTH), lambda r, k, me: (_du_pieces_of_block(block_x(r, k))[half], row_x(r), 0))
    grid_spec = pltpu.PrefetchScalarGridSpec(
        num_scalar_prefetch=1, grid=grid,
        in_specs=[pl.BlockSpec((tk, D_MODEL), lambda r, k, me: (jnp.where(in_w(r), k, last), 0)),
                  piece_w(0), piece_w(1), piece_x(0), piece_x(1),
                  pl.BlockSpec((None, D_MODEL, COLS_PER_DEV), lambda r, k, me: (block_x(r, k), 0, 0)),
                  rows(), vec(), rows()],
        out_specs=(rows(), vec(), ANY_SPEC, ANY_SPEC),
        scratch_shapes=[pltpu.VMEM((D_MODEL, COLS_PER_DEV), F32), pltpu.VMEM((2, D_MODEL, COLS_PER_DEV), BF16),
                        pltpu.VMEM((tm, D_MODEL), F32), pltpu.SemaphoreType.DMA((N_DEV,)),
                        pltpu.SemaphoreType.DMA((N_DEV,)), pltpu.SemaphoreType.DMA((2,))])
    me = 4 * lax.axis_index("x") + 2 * lax.axis_index("y") + lax.axis_index("c")
    blocks = jax.ShapeDtypeStruct((N_DEV, D_MODEL, COLS_PER_DEV), BF16)
    dx, dg, received, _ = _pcall(
        body, name=name, grid_spec=grid_spec,
        out_shape=(jax.ShapeDtypeStruct((s, D_MODEL), F32), jax.ShapeDtypeStruct((1, D_MODEL), F32), blocks, blocks),
        compiler_params=_params(("arbitrary", "arbitrary")),
    )(jnp.reshape(me, (1,)).astype(jnp.int32), h, du, du, du, du, w_all, x, g_pre, dy)
    return dx, dg, received


def _in_proj_bwd_w(h, du, name):
    s = h.shape[0]
    tk = min(2048, s)
    n_k = s // tk

    def body(h_ref, dua_ref, dub_ref, out_ref, acc):
        k = pl.program_id(1)

        @pl.when(k == 0)
        def _():
            acc[...] = jnp.zeros_like(acc)
        acc[...] += _dot_tn(h_ref[...], jnp.concatenate([dua_ref[...], dub_ref[...]], axis=1))

        @pl.when(k == n_k - 1)
        def _():
            out_ref[...] = acc[...].astype(BF16)

    piece = lambda half: pl.BlockSpec((None, tk, WIDTH), lambda j, k: (_du_pieces_of_block(j)[half], k, 0))
    return _pcall(
        body, name=name, grid=(N_DEV, n_k),
        in_specs=[pl.BlockSpec((tk, D_MODEL), lambda j, k: (k, 0)), piece(0), piece(1)],
        out_specs=pl.BlockSpec((None, D_MODEL, COLS_PER_DEV), lambda j, k: (j, 0, 0)),
        out_shape=jax.ShapeDtypeStruct((N_DEV, D_MODEL, COLS_PER_DEV), BF16),
        scratch_shapes=[pltpu.VMEM((D_MODEL, COLS_PER_DEV), F32)],
        compiler_params=_params(("parallel", "arbitrary")),
    )(h, du, du)


def _adamw_math(g, w, m, v):
    m_new = ADAM_B1 * m + (1.0 - ADAM_B1) * g
    v_new = ADAM_B2 * v + (1.0 - ADAM_B2) * (g * g)
    m_hat = m_new / (1.0 - ADAM_B1 ** ADAM_STEP)
    v_hat = v_new / (1.0 - ADAM_B2 ** ADAM_STEP)
    delta = -ADAM_LR * (m_hat / (jnp.sqrt(v_hat) + ADAM_EPS) + ADAM_WD * w)
    return delta, m_new, v_new


def _sum_partials(p_ref):
    total = p_ref[0].astype(F32)
    for d in range(1, N_DEV):
        total = total + p_ref[d].astype(F32)
    return total


def _adamw_layers(parts0, parts1, w, m, v, name):
    _, r, c = w.shape
    tr = min(256, r)
    n_r = r // tr

    def body(p0_ref, p1_ref, w_ref, m_ref, v_ref, g_ref, d_ref, mo_ref, vo_ref):
        layer = pl.program_id(0)

        @pl.when(layer == 0)
        def _():
            g_ref[...] = _sum_partials(p0_ref)

        @pl.when(layer == 1)
        def _():
            g_ref[...] = _sum_partials(p1_ref)
        d_ref[...], mo_ref[...], vo_ref[...] = _adamw_math(g_ref[...], w_ref[...], m_ref[...], v_ref[...])

    part = lambda which: pl.BlockSpec((N_DEV, tr, c), lambda l, i: (0, jnp.where(l == which, i, 0), 0))
    par = lambda: pl.BlockSpec((None, tr, c), lambda l, i: (l, i, 0))
    out = jax.ShapeDtypeStruct(w.shape, F32)
    return _pcall(
        body, name=name, grid=(2, n_r),
        in_specs=[part(0), part(1), par(), par(), par()],
        out_specs=(par(), par(), par(), par()),
        out_shape=(out, out, out, out),
        compiler_params=_params(("arbitrary", "arbitrary")),
    )(parts0, parts1, w, m, v)


def _adamw_small(parts, w, m, v, name):
    def body(p_ref, w_ref, m_ref, v_ref, g_ref, d_ref, mo_ref, vo_ref):
        g = _sum_partials(p_ref)
        g_ref[...] = g
        d_ref[...], mo_ref[...], vo_ref[...] = _adamw_math(g, w_ref[...], m_ref[...], v_ref[...])

    out = jax.ShapeDtypeStruct(w.shape, F32)
    return _pcall(body, name=name, out_shape=(out, out, out, out), compiler_params=_params())(parts, w, m, v)


def _adamw_plain(g, w, m, v, name):
    def body(g_ref, w_ref, m_ref, v_ref, d_ref, mo_ref, vo_ref):
        d_ref[...], mo_ref[...], vo_ref[...] = _adamw_math(g_ref[...], w_ref[...], m_ref[...], v_ref[...])

    out = jax.ShapeDtypeStruct(w.shape, F32)
    return _pcall(body, name=name, out_shape=(out, out, out), compiler_params=_params())(g, w, m, v)


def _rows128(a):
    return a.reshape(-1, LANES)


SMALL_NAMES = ("pre_norm_g", "pool_w", "pool_scale", "conv_w", "conv_b", "post_norm_g")


def kernel(x, pre_norm_g, w_in, pool_w, pool_scale, conv_w, conv_b, w_branch, w_out, post_norm_g, loss_target, m_pre_norm_g, m_w_in, m_pool_w, m_pool_scale, m_conv_w, m_conv_b, m_w_branch, m_w_out, m_post_norm_g, v_pre_norm_g, v_w_in, v_pool_w, v_pool_scale, v_conv_w, v_conv_b, v_w_branch, v_w_out, v_post_norm_g):
    me = 4 * lax.axis_index("x") + 2 * lax.axis_index("y") + lax.axis_index("c")
    x0 = x[0]
    target = loss_target[0]
    conv_cols = conv_w.shape[-1]

    conv_w_pad = jnp.pad(conv_w.reshape(2 * 3, conv_cols), ((0, 2), (0, LANES - conv_cols)))
    w_in_all = [None, None]
    w_in_all[0], cw_g = _gather_two_level([w_in[0].astype(BF16), conv_w_pad], "gather_w_in_0")
    conv_w_full = cw_g[:, :6, :conv_cols].reshape(N_DEV, 2, 3, conv_cols).transpose(1, 2, 0, 3).reshape(2, 3, WIDTH)
    later_weights = ([w_in[1].astype(BF16), w_branch.astype(BF16), w_out.astype(BF16)], True)

    saved = []
    xin = x0
    for l in range(2):
        u, h = _in_proj_fwd(xin, pre_norm_g[l:l + 1], w_in_all[l], f"in_proj_fwd_{l}")
        y_pool = _pool_fwd(u, pool_w[l], pool_scale[l:l + 1], f"pool_fwd_{l}")
        y_conv = _conv_fwd(u, conv_w_full[l], conv_b[l:l + 1], f"conv_fwd_{l}")
        if l == 0:
            o_sb, y_sb, w_in_all[1], wb_g, wo_all = _sb_fwd(u, f"sb_fwd_{l}", later_weights)
            wb_all = wb_g.transpose(1, 2, 3, 0, 4).reshape(2, 3, WIDTH, D_MODEL)
        else:
            o_sb, y_sb = _sb_fwd(u, f"sb_fwd_{l}")
        if l == 0:
            xout, merged, pre = _merge_out_fwd(y_pool, y_conv, y_sb, u, wb_all, wo_all, xin, post_norm_g[l:l + 1], l,
                                               f"merge_out_fwd_{l}")
        else:
            dy, merged, pre, loss_row = _merge_out_fwd(y_pool, y_conv, y_sb, u, wb_all, wo_all, xin,
                                                       post_norm_g[l:l + 1], l, f"merge_out_fwd_{l}", target)
        saved.append((xin, u, h, y_pool, y_conv, y_sb, o_sb, merged, pre))
        xin = xout

    small = [None, None]
    recv = [None, None]
    ready = []
    for l in (1, 0):
        xl, u, h, y_pool, y_conv, y_sb, o_sb, merged, pre = saved[l]
        dmerged, dwo, dg_post = _out_proj_bwd(dy, pre, post_norm_g[l:l + 1], merged, wo_all, l, f"out_proj_bwd_{l}")
        du, dyp, dyc, dys, dwb = _merge_bwd(dmerged, y_pool, y_conv, y_sb, u, wb_all, l, f"merge_bwd_{l}")
        dwb = dwb.reshape(N_DEV, 3 * WIDTH, D_MODEL // N_DEV)
        dwo = dwo.reshape(N_DEV, D_MODEL // N_DEV, D_MODEL)
        du, dcw, dcb = _conv_bwd(u, conv_w_full[l], conv_b[l:l + 1], dyc, du, f"conv_bwd_{l}")
        du, dpw, dps = _pool_bwd(u, pool_w[l], pool_scale[l:l + 1], dyp, du, f"pool_bwd_{l}")
        small[l] = dict(pool_w=dpw, pool_scale=dps, conv_w=dcw, conv_b=dcb, post_norm_g=dg_post)
        if l == 1:
            du, dk, dv = _sb_bwd(u, o_sb, dys, du, f"sb_bwd_{l}")
        else:
            small[l]["pre_norm_g"] = jnp.zeros((1, D_MODEL), F32)
            packed = jnp.concatenate(
                [_rows128(jnp.stack([small[0][n], small[1][n]])) for n in SMALL_NAMES]
                + [jnp.pad(loss_row, ((0, 7), (0, 0)))], axis=0)
            du, dk, dv, *got, packed_all = _sb_bwd(
                u, o_sb, dys, du, f"sb_bwd_{l}", (ready + [dwb, dwo, packed], (False,) * 5 + (True,)))
            recv[1] = got[:3]
        du = lax.dynamic_update_slice(du, jnp.stack([dk, dv]).astype(BF16), (DU_SB_KV[0], 0, 0))
        if l == 1:
            dwi = _in_proj_bwd_w(h, du, f"in_proj_bwd_w_{l}")
            ready = [dwi, dwb, dwo]
            dx, dg_pre = _in_proj_bwd_x(du, w_in_all[l], xl, pre_norm_g[l:l + 1], dy, f"in_proj_bwd_x_{l}")
            small[l]["pre_norm_g"] = dg_pre
        else:
            dx, dg_pre, got_dwi = _in_proj_bwd_send(h, du, w_in_all[l], xl, pre_norm_g[l:l + 1], dy,
                                                    f"in_proj_bwd_{l}")
            recv[0] = [got_dwi] + got[3:]
        dy = dx
    grad_x = dy[None]

    (g_pre_0_all,) = _exchange([_rows128(dg_pre)], True, "gather_g_pre_0")
    packed_all = lax.dynamic_update_slice(packed_all, g_pre_0_all, (0, 0, 0))
    sizes = dict(pre_norm_g=16, pool_w=1024, pool_scale=8, conv_w=24, conv_b=8, post_norm_g=16)
    n_rows = sum(sizes.values())
    loss = jnp.sum(packed_all[:, n_rows, 0])

    given = dict(pre_norm_g=(pre_norm_g, m_pre_norm_g, v_pre_norm_g), pool_w=(pool_w, m_pool_w, v_pool_w),
                 pool_scale=(pool_scale, m_pool_scale, v_pool_scale), conv_b=(conv_b, m_conv_b, v_conv_b),
                 post_norm_g=(post_norm_g, m_post_norm_g, v_post_norm_g))
    zeros_cw = jnp.zeros((sizes["conv_w"], LANES), F32)
    pack3 = [jnp.concatenate([zeros_cw if n == "conv_w" else _rows128(given[n][k]) for n in SMALL_NAMES], axis=0)
             for k in range(3)]
    sg, sd, sm, sv = _adamw_small(packed_all[:, :n_rows], pack3[0], pack3[1], pack3[2], "adamw_small")

    def unpack(buf, name, shape):
        start = 0
        for n in SMALL_NAMES:
            if n == name:
                return buf[start:start + sizes[n]].reshape(shape)
            start += sizes[n]

    out = {}
    for n in ("pre_norm_g", "pool_w", "pool_scale", "conv_b", "post_norm_g"):
        shape = given[n][0].shape
        out[n] = tuple(unpack(b, n, shape) for b in (sg, sd, sm, sv))
    g_cw = lax.dynamic_slice_in_dim(unpack(sg, "conv_w", (2, 3, WIDTH)), me * conv_cols, conv_cols, axis=2)
    cw2 = lambda a: a.reshape(6, conv_cols)
    d_cw, m_cw, v_cw = _adamw_plain(cw2(g_cw), cw2(conv_w), cw2(m_conv_w), cw2(v_conv_w), "adamw_conv_w")
    out["conv_w"] = (g_cw,) + tuple(a.reshape(2, 3, conv_cols) for a in (d_cw, m_cw, v_cw))

    out["w_in"] = _adamw_layers(recv[0][0], recv[1][0], w_in, m_w_in, v_w_in, "adamw_w_in")
    cols = D_MODEL // N_DEV
    wb3 = lambda a: a.reshape(2, 3 * WIDTH, cols)
    out["w_branch"] = tuple(a.reshape(2, 3, WIDTH, cols) for a in _adamw_layers(
        recv[0][1], recv[1][1], wb3(w_branch), wb3(m_w_branch), wb3(v_w_branch), "adamw_w_branch"))
    out["w_out"] = _adamw_layers(recv[0][2], recv[1][2], w_out, m_w_out, v_w_out, "adamw_w_out")

    order = ("pre_norm_g", "w_in", "pool_w", "pool_scale", "conv_w", "conv_b", "w_branch", "w_out", "post_norm_g")
    return (loss, grad_x) + tuple(out[n][k] for k in range(4) for n in order)
```
